```python
import jax, jax.numpy as jnp
from jax import lax
import numpy as np

D_MODEL = 1024
BATCH = 8
SEQ = 2048
DEPTH = 4

N_A = DEPTH // 2
N_B = DEPTH - N_A
HEAD_DIM = 64
MEM_LEN = 256
N_MEM_HEADS = 4
MEM_WIDTH = N_MEM_HEADS * HEAD_DIM
MIX_WIDTH = D_MODEL - MEM_WIDTH
CONV_CH = MIX_WIDTH
CONV_WIDTH = 31
N_FOX_HEADS = MIX_WIDTH // HEAD_DIM
D_FF = 2816
FFN_CONV_WIDTH = 3
BLOCK_Q = 128
RMS_EPS = 1e-6
LN_EPS = 1e-5

kernel_name = "yoco_conformer_fox_hybrid"


def rmsnorm(x, g):
    xf = x.astype(jnp.float32)
    y = xf * lax.rsqrt(jnp.mean(xf * xf, axis=-1, keepdims=True) + RMS_EPS)
    return (y * g.astype(jnp.float32)).astype(x.dtype)


def layernorm(x, g, b):
    xf = x.astype(jnp.float32)
    mu = jnp.mean(xf, axis=-1, keepdims=True)
    var = jnp.mean(jnp.square(xf - mu), axis=-1, keepdims=True)
    y = (xf - mu) * lax.rsqrt(var + LN_EPS)
    return (y * g.astype(jnp.float32) + b.astype(jnp.float32)).astype(x.dtype)


def causal_dwconv(x, w, b):
    width, ch = w.shape
    y = lax.conv_general_dilated(
        x, w[:, None, :].astype(x.dtype), window_strides=(1,),
        padding=((width - 1, 0),), dimension_numbers=("NWC", "WIO", "NWC"),
        feature_group_count=ch)
    return y + b


def conformer_conv(u, b_glu, w_dw, b_dw, ln_g, ln_b):
    u = u + b_glu
    a, gate = jnp.split(u, 2, axis=-1)
    v = a * jax.nn.sigmoid(gate)
    v = causal_dwconv(v, w_dw, b_dw)
    v = layernorm(v, ln_g, ln_b)
    return jax.nn.silu(v)


def memory_attention(q, mem_k, mem_v):
    b, s, _ = q.shape
    qh = q.reshape(b, s, N_MEM_HEADS, HEAD_DIM)
    kh = mem_k.reshape(b, -1, N_MEM_HEADS, HEAD_DIM)
    vh = mem_v.reshape(b, -1, N_MEM_HEADS, HEAD_DIM)
    logits = jnp.einsum("bshd,bmhd->bhsm", qh, kh).astype(jnp.float32) * (HEAD_DIM ** -0.5)
    p = jax.nn.softmax(logits, axis=-1).astype(vh.dtype)
    o = jnp.einsum("bhsm,bmhd->bshd", p, vh)
    return o.reshape(b, s, MEM_WIDTH)


def forgetting_attention(q, k, v, cum_logf):
    b, s, h, dh = q.shape
    scale = dh ** -0.5
    outs = []
    for i in range(s // BLOCK_Q):
        q0 = i * BLOCK_Q
        kend = q0 + BLOCK_Q
        qb = q[:, q0:kend]
        kb = k[:, :kend]
        vb = v[:, :kend]
        logits = jnp.einsum("bqhd,bkhd->bhqk", qb, kb).astype(jnp.float32) * scale
        logits = logits + cum_logf[:, :, q0:kend, None] - cum_logf[:, :, None, :kend]
        qpos = q0 + jnp.arange(BLOCK_Q)
        kpos = jnp.arange(kend)
        logits = jnp.where(kpos[None, :] <= qpos[:, None], logits, -jnp.inf)
        p = jax.nn.softmax(logits, axis=-1).astype(vb.dtype)
        outs.append(jnp.einsum("bhqk,bkhd->bqhd", p, vb))
    return jnp.concatenate(outs, axis=1)


def conv_ffn(h, w_up, w_dw, b_dw, w_down):
    u = h @ w_up
    u = causal_dwconv(u, w_dw, b_dw)
    gate, val = jnp.split(u, 2, axis=-1)
    return (jax.nn.silu(gate) * val) @ w_down


def _fwd_setup_inputs(seed: int = 0) -> dict:
    key = jax.random.key(seed)
    ks = jax.random.split(key, 24)
    f32 = jnp.float32
    nrm = lambda k, shape, scale: jax.random.normal(k, shape, f32) * scale
    gain = lambda k, shape: 1.0 + 0.05 * jax.random.normal(k, shape, f32)
    d = D_MODEL
    return {
        "x": jax.random.normal(ks[0], (BATCH, SEQ, d), f32),
        "mem": jax.random.normal(ks[1], (BATCH, MEM_LEN, d), f32),
        "g_mix": gain(ks[2], (DEPTH, d)),
        "w_in_a": nrm(ks[3], (N_A, d, 2 * CONV_CH + MEM_WIDTH), d ** -0.5),
        "b_glu": nrm(ks[4], (N_A, 2 * CONV_CH), 0.02),
        "w_dw_a": nrm(ks[5], (N_A, CONV_WIDTH, CONV_CH), CONV_WIDTH ** -0.5),
        "b_dw_a": nrm(ks[6], (N_A, CONV_CH), 0.02),
        "ln_g": gain(ks[7], (N_A, CONV_CH)),
        "ln_b": nrm(ks[8], (N_A, CONV_CH), 0.02),
        "g_kv": gain(ks[9], (d,)),
        "w_kvf": nrm(ks[10], (d, 2 * MIX_WIDTH + N_FOX_HEADS), d ** -0.5),
        "b_f": 2.0 + 0.5 * jax.random.normal(ks[11], (N_FOX_HEADS,), f32),
        "w_in_b": nrm(ks[12], (N_B, d, MIX_WIDTH + MEM_WIDTH), d ** -0.5),
        "g_mem": gain(ks[13], (d,)),
        "w_mem_kv": nrm(ks[14], (DEPTH, d, 2 * MEM_WIDTH), d ** -0.5),
        "w_out": nrm(ks[15], (DEPTH, d, d), d ** -0.5),
        "g_ffn": gain(ks[16], (DEPTH, d)),
        "w_up": nrm(ks[17], (DEPTH, d, 2 * D_FF), d ** -0.5),
        "w_dw_f": nrm(ks[18], (DEPTH, FFN_CONV_WIDTH, 2 * D_FF), FFN_CONV_WIDTH ** -0.5),
        "b_dw_f": nrm(ks[19], (DEPTH, 2 * D_FF), 0.02),
        "w_down": nrm(ks[20], (DEPTH, D_FF, d), D_FF ** -0.5),
        "g_final": gain(ks[21], (d,)),
    }


def _fwd_reference(x, mem, g_mix, w_in_a, b_glu, w_dw_a, b_dw_a, ln_g, ln_b, g_kv, w_kvf, b_f,
              w_in_b, g_mem, w_mem_kv, w_out, g_ffn, w_up, w_dw_f, b_dw_f, w_down, g_final):
    bsz, seq, _ = x.shape
    mem_n = rmsnorm(mem, g_mem)
    k_sh = v_sh = cum_logf = None
    for l in range(DEPTH):
        h = rmsnorm(x, g_mix[l])
        mem_k, mem_v = jnp.split(mem_n @ w_mem_kv[l], 2, axis=-1)
        if l < N_A:
            p = h @ w_in_a[l]
            u, q_mem = p[..., :2 * CONV_CH], p[..., 2 * CONV_CH:]
            mix = conformer_conv(u, b_glu[l], w_dw_a[l], b_dw_a[l], ln_g[l], ln_b[l])
        else:
            if l == N_A:
                hk = rmsnorm(x, g_kv)
                kvf = hk @ w_kvf
                k_sh = kvf[..., :MIX_WIDTH].reshape(bsz, seq, N_FOX_HEADS, HEAD_DIM)
                v_sh = kvf[..., MIX_WIDTH:2 * MIX_WIDTH].reshape(bsz, seq, N_FOX_HEADS, HEAD_DIM)
                f_logit = (kvf[..., 2 * MIX_WIDTH:] + b_f).astype(jnp.float32)
                cum_logf = jnp.cumsum(jax.nn.log_sigmoid(f_logit), axis=1).transpose(0, 2, 1)
            p = h @ w_in_b[l - N_A]
            q, q_mem = p[..., :MIX_WIDTH], p[..., MIX_WIDTH:]
            q = q.reshape(bsz, seq, N_FOX_HEADS, HEAD_DIM)
            mix = forgetting_attention(q, k_sh, v_sh, cum_logf).reshape(bsz, seq, MIX_WIDTH)
        mem_o = memory_attention(q_mem, mem_k, mem_v)
        x = x + jnp.concatenate([mix, mem_o], axis=-1) @ w_out[l]
        x = x + conv_ffn(rmsnorm(x, g_ffn[l]), w_up[l], w_dw_f[l], b_dw_f[l], w_down[l])
    return rmsnorm(x, g_final)


import jax as _jax
import jax.numpy as _jnp

TWIN_FORMAT = 'train_step'
FWD_PARAMS = ['x', 'mem', 'g_mix', 'w_in_a', 'b_glu', 'w_dw_a', 'b_dw_a', 'ln_g', 'ln_b', 'g_kv', 'w_kvf', 'b_f', 'w_in_b', 'g_mem', 'w_mem_kv', 'w_out', 'g_ffn', 'w_up', 'w_dw_f', 'b_dw_f', 'w_down', 'g_final']
TWIN_WEIGHTS = ['g_mix', 'w_in_a', 'b_glu', 'w_dw_a', 'b_dw_a', 'ln_g', 'ln_b', 'g_kv', 'w_kvf', 'b_f', 'w_in_b', 'g_mem', 'w_mem_kv', 'w_out', 'g_ffn', 'w_up', 'w_dw_f', 'b_dw_f', 'w_down', 'g_final']
TWIN_DIFF_INPUT = 'x'
TWIN_INPUTS = ['x', 'mem', 'g_mix', 'w_in_a', 'b_glu', 'w_dw_a', 'b_dw_a', 'ln_g', 'ln_b', 'g_kv', 'w_kvf', 'b_f', 'w_in_b', 'g_mem', 'w_mem_kv', 'w_out', 'g_ffn', 'w_up', 'w_dw_f', 'b_dw_f', 'w_down', 'g_final', 'loss_target', 'm_g_mix', 'm_w_in_a', 'm_b_glu', 'm_w_dw_a', 'm_b_dw_a', 'm_ln_g', 'm_ln_b', 'm_g_kv', 'm_w_kvf', 'm_b_f', 'm_w_in_b', 'm_g_mem', 'm_w_mem_kv', 'm_w_out', 'm_g_ffn', 'm_w_up', 'm_w_dw_f', 'm_b_dw_f', 'm_w_down', 'm_g_final', 'v_g_mix', 'v_w_in_a', 'v_b_glu', 'v_w_dw_a', 'v_b_dw_a', 'v_ln_g', 'v_ln_b', 'v_g_kv', 'v_w_kvf', 'v_b_f', 'v_w_in_b', 'v_g_mem', 'v_w_mem_kv', 'v_w_out', 'v_g_ffn', 'v_w_up', 'v_w_dw_f', 'v_b_dw_f', 'v_w_down', 'v_g_final']
TWIN_OUTPUTS = ['loss', 'grad_x', 'grad_g_mix', 'grad_w_in_a', 'grad_b_glu', 'grad_w_dw_a', 'grad_b_dw_a', 'grad_ln_g', 'grad_ln_b', 'grad_g_kv', 'grad_w_kvf', 'grad_b_f', 'grad_w_in_b', 'grad_g_mem', 'grad_w_mem_kv', 'grad_w_out', 'grad_g_ffn', 'grad_w_up', 'grad_w_dw_f', 'grad_b_dw_f', 'grad_w_down', 'grad_g_final', 'delta_g_mix', 'delta_w_in_a', 'delta_b_glu', 'delta_w_dw_a', 'delta_b_dw_a', 'delta_ln_g', 'delta_ln_b', 'delta_g_kv', 'delta_w_kvf', 'delta_b_f', 'delta_w_in_b', 'delta_g_mem', 'delta_w_mem_kv', 'delta_w_out', 'delta_g_ffn', 'delta_w_up', 'delta_w_dw_f', 'delta_b_dw_f', 'delta_w_down', 'delta_g_final', 'new_m_g_mix', 'new_m_w_in_a', 'new_m_b_glu', 'new_m_w_dw_a', 'new_m_b_dw_a', 'new_m_ln_g', 'new_m_ln_b', 'new_m_g_kv', 'new_m_w_kvf', 'new_m_b_f', 'new_m_w_in_b', 'new_m_g_mem', 'new_m_w_mem_kv', 'new_m_w_out', 'new_m_g_ffn', 'new_m_w_up', 'new_m_w_dw_f', 'new_m_b_dw_f', 'new_m_w_down', 'new_m_g_final', 'new_v_g_mix', 'new_v_w_in_a', 'new_v_b_glu', 'new_v_w_dw_a', 'new_v_b_dw_a', 'new_v_ln_g', 'new_v_ln_b', 'new_v_g_kv', 'new_v_w_kvf', 'new_v_b_f', 'new_v_w_in_b', 'new_v_g_mem', 'new_v_w_mem_kv', 'new_v_w_out', 'new_v_g_ffn', 'new_v_w_up', 'new_v_w_dw_f', 'new_v_b_dw_f', 'new_v_w_down', 'new_v_g_final']
TWIN_LEAF_KINDS = {'loss': 'loss', 'grad_x': 'grad_x', 'grad_g_mix': 'grad_w', 'grad_w_in_a': 'grad_w', 'grad_b_glu': 'grad_w', 'grad_w_dw_a': 'grad_w', 'grad_b_dw_a': 'grad_w', 'grad_ln_g': 'grad_w', 'grad_ln_b': 'grad_w', 'grad_g_kv': 'grad_w', 'grad_w_kvf': 'grad_w', 'grad_b_f': 'grad_w', 'grad_w_in_b': 'grad_w', 'grad_g_mem': 'grad_w', 'grad_w_mem_kv': 'grad_w', 'grad_w_out': 'grad_w', 'grad_g_ffn': 'grad_w', 'grad_w_up': 'grad_w', 'grad_w_dw_f': 'grad_w', 'grad_b_dw_f': 'grad_w', 'grad_w_down': 'grad_w', 'grad_g_final': 'grad_w', 'delta_g_mix': 'delta_w', 'delta_w_in_a': 'delta_w', 'delta_b_glu': 'delta_w', 'delta_w_dw_a': 'delta_w', 'delta_b_dw_a': 'delta_w', 'delta_ln_g': 'delta_w', 'delta_ln_b': 'delta_w', 'delta_g_kv': 'delta_w', 'delta_w_kvf': 'delta_w', 'delta_b_f': 'delta_w', 'delta_w_in_b': 'delta_w', 'delta_g_mem': 'delta_w', 'delta_w_mem_kv': 'delta_w', 'delta_w_out': 'delta_w', 'delta_g_ffn': 'delta_w', 'delta_w_up': 'delta_w', 'delta_w_dw_f': 'delta_w', 'delta_b_dw_f': 'delta_w', 'delta_w_down': 'delta_w', 'delta_g_final': 'delta_w', 'new_m_g_mix': 'new_m', 'new_m_w_in_a': 'new_m', 'new_m_b_glu': 'new_m', 'new_m_w_dw_a': 'new_m', 'new_m_b_dw_a': 'new_m', 'new_m_ln_g': 'new_m', 'new_m_ln_b': 'new_m', 'new_m_g_kv': 'new_m', 'new_m_w_kvf': 'new_m', 'new_m_b_f': 'new_m', 'new_m_w_in_b': 'new_m', 'new_m_g_mem': 'new_m', 'new_m_w_mem_kv': 'new_m', 'new_m_w_out': 'new_m', 'new_m_g_ffn': 'new_m', 'new_m_w_up': 'new_m', 'new_m_w_dw_f': 'new_m', 'new_m_b_dw_f': 'new_m', 'new_m_w_down': 'new_m', 'new_m_g_final': 'new_m', 'new_v_g_mix': 'new_v', 'new_v_w_in_a': 'new_v', 'new_v_b_glu': 'new_v', 'new_v_w_dw_a': 'new_v', 'new_v_b_dw_a': 'new_v', 'new_v_ln_g': 'new_v', 'new_v_ln_b': 'new_v', 'new_v_g_kv': 'new_v', 'new_v_w_kvf': 'new_v', 'new_v_b_f': 'new_v', 'new_v_w_in_b': 'new_v', 'new_v_g_mem': 'new_v', 'new_v_w_mem_kv': 'new_v', 'new_v_w_out': 'new_v', 'new_v_g_ffn': 'new_v', 'new_v_w_up': 'new_v', 'new_v_w_dw_f': 'new_v', 'new_v_b_dw_f': 'new_v', 'new_v_w_down': 'new_v', 'new_v_g_final': 'new_v'}


def _forward(args):
    return _fwd_reference(*[args[k] for k in FWD_PARAMS])


def _output_shape():
    out = _jax.eval_shape(lambda: _forward(_fwd_setup_inputs(0)))
    return out.shape, out.dtype

N_MICROBATCH = 1
ADAM_LR = 0.001
ADAM_B1 = 0.9
ADAM_B2 = 0.999
ADAM_EPS = 1e-08
ADAM_WD = 0.01
ADAM_STEP = 10
PER_EXAMPLE_BATCH_AXIS = {'x': 0, 'mem': 0, 'loss_target': 0}
SHARED_INPUTS = []
_WEIGHT_DTYPES = {'g_mix': _jnp.float32, 'w_in_a': _jnp.float32, 'b_glu': _jnp.float32, 'w_dw_a': _jnp.float32, 'b_dw_a': _jnp.float32, 'ln_g': _jnp.float32, 'ln_b': _jnp.float32, 'g_kv': _jnp.float32, 'w_kvf': _jnp.float32, 'b_f': _jnp.float32, 'w_in_b': _jnp.float32, 'g_mem': _jnp.float32, 'w_mem_kv': _jnp.float32, 'w_out': _jnp.float32, 'g_ffn': _jnp.float32, 'w_up': _jnp.float32, 'w_dw_f': _jnp.float32, 'b_dw_f': _jnp.float32, 'w_down': _jnp.float32, 'g_final': _jnp.float32}
MOMENT_SCALE = {'g_mix': 6.023884e-02, 'w_in_a': 6.200078e-02, 'b_glu': 8.854534e-02, 'w_dw_a': 8.901528e-02, 'b_dw_a': 2.238489e-01, 'ln_g': 1.203466e-01, 'ln_b': 1.216923e-01, 'g_kv': 5.922682e-02, 'w_kvf': 4.727138e-02, 'b_f': 2.097092e-01, 'w_in_b': 2.194228e-02, 'g_mem': 1.885281e-02, 'w_mem_kv': 1.307737e-02, 'w_out': 6.253993e-02, 'g_ffn': 8.444473e-02, 'w_up': 3.615887e-02, 'w_dw_f': 3.657273e-02, 'b_dw_f': 3.978242e-02, 'w_down': 5.935237e-02, 'g_final': 1.609418e+01}


def _to_microbatches(a, axis):
    t = _jnp.moveaxis(a, axis, 0)
    t = t.reshape((N_MICROBATCH, t.shape[0] // N_MICROBATCH) + t.shape[1:])
    return _jnp.moveaxis(t, 1, axis + 1)


def setup_inputs(seed: int = 0) -> dict:
    inp = _fwd_setup_inputs(seed)
    key = _jax.random.fold_in(_jax.random.key(seed), 7919)
    shape, _ = _output_shape()
    out = dict(inp)
    out["loss_target"] = _jax.random.normal(_jax.random.fold_in(key, 0), shape, _jnp.float32)
    for i, name in enumerate(TWIN_WEIGHTS):
        w = inp[name].astype(_jnp.float32)
        if MOMENT_SCALE is None:
            s = _jnp.sqrt(_jnp.mean(_jnp.square(w)) + 1e-30)
        else:
            s = MOMENT_SCALE[name]
        km, kv = _jax.random.split(_jax.random.fold_in(key, i + 1))
        out[name] = w
        out["m_" + name] = s * _jax.random.normal(km, w.shape, _jnp.float32)
        out["v_" + name] = (s * s) * _jax.random.uniform(kv, w.shape, _jnp.float32, 0.5, 1.5)
    if N_MICROBATCH > 1:
        for name, axis in PER_EXAMPLE_BATCH_AXIS.items():
            out[name] = _to_microbatches(out[name], axis)
    return {'x': out['x'], 'mem': out['mem'], 'g_mix': out['g_mix'], 'w_in_a': out['w_in_a'], 'b_glu': out['b_glu'], 'w_dw_a': out['w_dw_a'], 'b_dw_a': out['b_dw_a'], 'ln_g': out['ln_g'], 'ln_b': out['ln_b'], 'g_kv': out['g_kv'], 'w_kvf': out['w_kvf'], 'b_f': out['b_f'], 'w_in_b': out['w_in_b'], 'g_mem': out['g_mem'], 'w_mem_kv': out['w_mem_kv'], 'w_out': out['w_out'], 'g_ffn': out['g_ffn'], 'w_up': out['w_up'], 'w_dw_f': out['w_dw_f'], 'b_dw_f': out['b_dw_f'], 'w_down': out['w_down'], 'g_final': out['g_final'], 'loss_target': out['loss_target'], 'm_g_mix': out['m_g_mix'], 'm_w_in_a': out['m_w_in_a'], 'm_b_glu': out['m_b_glu'], 'm_w_dw_a': out['m_w_dw_a'], 'm_b_dw_a': out['m_b_dw_a'], 'm_ln_g': out['m_ln_g'], 'm_ln_b': out['m_ln_b'], 'm_g_kv': out['m_g_kv'], 'm_w_kvf': out['m_w_kvf'], 'm_b_f': out['m_b_f'], 'm_w_in_b': out['m_w_in_b'], 'm_g_mem': out['m_g_mem'], 'm_w_mem_kv': out['m_w_mem_kv'], 'm_w_out': out['m_w_out'], 'm_g_ffn': out['m_g_ffn'], 'm_w_up': out['m_w_up'], 'm_w_dw_f': out['m_w_dw_f'], 'm_b_dw_f': out['m_b_dw_f'], 'm_w_down': out['m_w_down'], 'm_g_final': out['m_g_final'], 'v_g_mix': out['v_g_mix'], 'v_w_in_a': out['v_w_in_a'], 'v_b_glu': out['v_b_glu'], 'v_w_dw_a': out['v_w_dw_a'], 'v_b_dw_a': out['v_b_dw_a'], 'v_ln_g': out['v_ln_g'], 'v_ln_b': out['v_ln_b'], 'v_g_kv': out['v_g_kv'], 'v_w_kvf': out['v_w_kvf'], 'v_b_f': out['v_b_f'], 'v_w_in_b': out['v_w_in_b'], 'v_g_mem': out['v_g_mem'], 'v_w_mem_kv': out['v_w_mem_kv'], 'v_w_out': out['v_w_out'], 'v_g_ffn': out['v_g_ffn'], 'v_w_up': out['v_w_up'], 'v_w_dw_f': out['v_w_dw_f'], 'v_b_dw_f': out['v_b_dw_f'], 'v_w_down': out['v_w_down'], 'v_g_final': out['v_g_final']}


def _loss(weights, diff, rest, loss_target):
    with _jax.named_scope("forward"):
        args = {**rest, TWIN_DIFF_INPUT: diff, **{k: w.astype(_WEIGHT_DTYPES[k]) for k, w in weights.items()}}
        y = _forward(args)
    with _jax.named_scope("loss_head"):
        err = _jnp.square(y.astype(_jnp.float32) - loss_target)
        return 0.5 * _jnp.sum(_jnp.mean(err, axis=-1)) if err.ndim else 0.5 * err


def _adamw(w, g, m, v):
    m = ADAM_B1 * m + (1.0 - ADAM_B1) * g
    v = ADAM_B2 * v + (1.0 - ADAM_B2) * _jnp.square(g)
    m_hat = m / (1.0 - ADAM_B1 ** ADAM_STEP)
    v_hat = v / (1.0 - ADAM_B2 ** ADAM_STEP)
    delta = -ADAM_LR * (m_hat / (_jnp.sqrt(v_hat) + ADAM_EPS) + ADAM_WD * w)
    return delta, m, v


def reference(x, mem, g_mix, w_in_a, b_glu, w_dw_a, b_dw_a, ln_g, ln_b, g_kv, w_kvf, b_f, w_in_b, g_mem, w_mem_kv, w_out, g_ffn, w_up, w_dw_f, b_dw_f, w_down, g_final, loss_target, m_g_mix, m_w_in_a, m_b_glu, m_w_dw_a, m_b_dw_a, m_ln_g, m_ln_b, m_g_kv, m_w_kvf, m_b_f, m_w_in_b, m_g_mem, m_w_mem_kv, m_w_out, m_g_ffn, m_w_up, m_w_dw_f, m_b_dw_f, m_w_down, m_g_final, v_g_mix, v_w_in_a, v_b_glu, v_w_dw_a, v_b_dw_a, v_ln_g, v_ln_b, v_g_kv, v_w_kvf, v_b_f, v_w_in_b, v_g_mem, v_w_mem_kv, v_w_out, v_g_ffn, v_w_up, v_w_dw_f, v_b_dw_f, v_w_down, v_g_final):
    given = dict(x=x, mem=mem, g_mix=g_mix, w_in_a=w_in_a, b_glu=b_glu, w_dw_a=w_dw_a, b_dw_a=b_dw_a, ln_g=ln_g, ln_b=ln_b, g_kv=g_kv, w_kvf=w_kvf, b_f=b_f, w_in_b=w_in_b, g_mem=g_mem, w_mem_kv=w_mem_kv, w_out=w_out, g_ffn=g_ffn, w_up=w_up, w_dw_f=w_dw_f, b_dw_f=b_dw_f, w_down=w_down, g_final=g_final, loss_target=loss_target, m_g_mix=m_g_mix, m_w_in_a=m_w_in_a, m_b_glu=m_b_glu, m_w_dw_a=m_w_dw_a, m_b_dw_a=m_b_dw_a, m_ln_g=m_ln_g, m_ln_b=m_ln_b, m_g_kv=m_g_kv, m_w_kvf=m_w_kvf, m_b_f=m_b_f, m_w_in_b=m_w_in_b, m_g_mem=m_g_mem, m_w_mem_kv=m_w_mem_kv, m_w_out=m_w_out, m_g_ffn=m_g_ffn, m_w_up=m_w_up, m_w_dw_f=m_w_dw_f, m_b_dw_f=m_b_dw_f, m_w_down=m_w_down, m_g_final=m_g_final, v_g_mix=v_g_mix, v_w_in_a=v_w_in_a, v_b_glu=v_b_glu, v_w_dw_a=v_w_dw_a, v_b_dw_a=v_b_dw_a, v_ln_g=v_ln_g, v_ln_b=v_ln_b, v_g_kv=v_g_kv, v_w_kvf=v_w_kvf, v_b_f=v_b_f, v_w_in_b=v_w_in_b, v_g_mem=v_g_mem, v_w_mem_kv=v_w_mem_kv, v_w_out=v_w_out, v_g_ffn=v_g_ffn, v_w_up=v_w_up, v_w_dw_f=v_w_dw_f, v_b_dw_f=v_b_dw_f, v_w_down=v_w_down, v_g_final=v_g_final)
    weights = {n: given[n] for n in TWIN_WEIGHTS}
    shared = {n: given[n] for n in SHARED_INPUTS}
    per_example = {n: given[n] for n in ['x', 'mem']}
    grad_fn = _jax.value_and_grad(_loss, argnums=(0, 1))

    def one_microbatch(ex, loss_target):
        ex = dict(ex)
        diff = ex.pop(TWIN_DIFF_INPUT)
        return grad_fn(weights, diff, {**shared, **ex}, loss_target)

    if N_MICROBATCH == 1:
        loss, (grad_w, grad_x) = one_microbatch(per_example, given["loss_target"])
    else:
        def body(carry, xs):
            loss_sum, grad_sum = carry
            l_k, (gw_k, gx_k) = one_microbatch(xs[0], xs[1])
            with _jax.named_scope("update"):
                return (loss_sum + l_k, _jax.tree.map(_jnp.add, grad_sum, gw_k)), gx_k

        init = (_jnp.zeros((), _jnp.float32), _jax.tree.map(_jnp.zeros_like, weights))
        (loss, grad_w), grad_x = _jax.lax.scan(body, init, (per_example, given["loss_target"]))
    with _jax.named_scope("update"):
        delta_w, new_m, new_v = {}, {}, {}
        for n in TWIN_WEIGHTS:
            delta_w[n], new_m[n], new_v[n] = _adamw(weights[n], grad_w[n], given["m_" + n], given["v_" + n])
    return (loss, grad_x, *[grad_w[n] for n in TWIN_WEIGHTS], *[delta_w[n] for n in TWIN_WEIGHTS],
            *[new_m[n] for n in TWIN_WEIGHTS], *[new_v[n] for n in TWIN_WEIGHTS])
```

```python
import jax
import jax.numpy as jnp
from jax import lax
from jax.experimental import pallas as pl
from jax.experimental.pallas import tpu as pltpu

F32 = jnp.float32
BF16 = jnp.bfloat16

D_MODEL = 1024
N_LAYERS = 4
N_A = 2
CONV_CH = 768
MEM_W = 256
HEAD_DIM = 64
N_MEM_HEADS = 4
N_FOX_HEADS = 12
N_HEAD_PAIRS = N_FOX_HEADS // 2
D_FF = 2816
CONV_W = 31
CONV_PAD = 32
FFN_CONV_W = 3
FFN_PAD = 8
F_PAD = 128
RMS_EPS = 1e-6
LN_EPS = 1e-5
ATT_SCALE = HEAD_DIM ** -0.5
NEG_BIG = -1e30

ADAM_LR = 0.001
ADAM_B1 = 0.9
ADAM_B2 = 0.999
ADAM_EPS = 1e-08
ADAM_WD = 0.01
ADAM_STEP = 10

LANE = 128
ROW_TILE = 256
CHUNK = 128
VMEM_LIMIT = 48 * 1024 * 1024
FLAT_COLS = 1024
N_CHIPS = 4
N_DEV = 8
MESH_ID = pl.DeviceIdType.MESH


def _params(sem=None):
    return pltpu.CompilerParams(dimension_semantics=sem, vmem_limit_bytes=VMEM_LIMIT)


def _tile(dim, pref):
    if dim <= pref:
        return dim
    best = None
    for m in range(1, dim // LANE + 1):
        d = m * LANE
        if dim % d == 0 and d <= pref:
            best = d
    assert best is not None, dim
    return best


_DIMS = {"nn": (((1,), (0,)), ((), ())), "nt": (((1,), (1,)), ((), ())), "tn": (((0,), (0,)), ((), ()))}


def _mm(a, b, mode="nn", out_dtype=F32, add=None):
    if mode == "nn":
        (m, k), (k2, n) = a.shape, b.shape
    elif mode == "nt":
        (m, k), (n, k2) = a.shape, b.shape
    else:
        (k, m), (k2, n) = a.shape, b.shape
    assert k == k2, (a.shape, b.shape, mode)
    tm, tn, tk = _tile(m, 1024), _tile(n, 512), _tile(k, 512)
    nk = k // tk
    dims = _DIMS[mode]
    has_add = add is not None

    def body(*refs):
        if has_add:
            a_ref, b_ref, add_ref, o_ref, acc_ref = refs
        else:
            a_ref, b_ref, o_ref, acc_ref = refs
        kk = pl.program_id(2)
        part = lax.dot_general(a_ref[...].astype(BF16), b_ref[...].astype(BF16), dims, preferred_element_type=F32)

        @pl.when(kk == 0)
        def _():
            acc_ref[...] = part

        @pl.when(kk > 0)
        def _():
            acc_ref[...] += part

        @pl.when(kk == nk - 1)
        def _():
            r = acc_ref[...]
            if has_add:
                r = r + add_ref[...]
            o_ref[...] = r.astype(out_dtype)

    a_spec = pl.BlockSpec((tk, tm), lambda i, j, kk: (kk, i)) if mode == "tn" else pl.BlockSpec((tm, tk), lambda i, j, kk: (i, kk))
    b_spec = pl.BlockSpec((tn, tk), lambda i, j, kk: (j, kk)) if mode == "nt" else pl.BlockSpec((tk, tn), lambda i, j, kk: (kk, j))
    o_spec = pl.BlockSpec((tm, tn), lambda i, j, kk: (i, j))
    in_specs = [a_spec, b_spec] + ([o_spec] if has_add else [])
    args = (a, b) + ((add,) if has_add else ())
    return pl.pallas_call(
        body,
        out_shape=jax.ShapeDtypeStruct((m, n), out_dtype),
        grid=(m // tm, n // tn, nk),
        in_specs=in_specs,
        out_specs=o_spec,
        scratch_shapes=[pltpu.VMEM((tm, tn), F32)],
        compiler_params=_params(("parallel", "parallel", "arbitrary")),
        name=f"mm_{mode}_{m}x{k}x{n}",
    )(*args)


def _row(width):
    return pl.BlockSpec((ROW_TILE, width), lambda i: (i, 0))


def _vec(width):
    return pl.BlockSpec((1, width), lambda i: (0, 0))


def _rms_fwd(x, g):
    rows, d = x.shape

    def body(x_ref, g_ref, o_ref):
        xv = x_ref[...]
        rstd = lax.rsqrt(jnp.mean(xv * xv, axis=-1, keepdims=True) + RMS_EPS)
        o_ref[...] = (xv * rstd * g_ref[...]).astype(BF16)

    return pl.pallas_call(
        body, out_shape=jax.ShapeDtypeStruct((rows, d), BF16), grid=(rows // ROW_TILE,),
        in_specs=[_row(d), _vec(d)], out_specs=_row(d), compiler_params=_params(("parallel",)), name=f"rms_fwd_{rows}",
    )(x, g)


def _accumulate(ref, value, step):
    @pl.when(step == 0)
    def _():
        ref[...] = value

    @pl.when(step > 0)
    def _():
        ref[...] += value


def _rms_bwd(x, g, dh, dres):
    rows, d = x.shape
    has_res = dres is not None

    def body(*refs):
        if has_res:
            x_ref, g_ref, dh_ref, dres_ref, dx_ref, dxb_ref, dg_ref = refs
        else:
            x_ref, g_ref, dh_ref, dx_ref, dxb_ref, dg_ref = refs
        xv = x_ref[...]
        dhv = dh_ref[...]
        rstd = lax.rsqrt(jnp.mean(xv * xv, axis=-1, keepdims=True) + RMS_EPS)
        xhat = xv * rstd
        gd = dhv * g_ref[...]
        dx = rstd * (gd - xhat * jnp.mean(gd * xhat, axis=-1, keepdims=True))
        if has_res:
            dx = dx + dres_ref[...]
        dx_ref[...] = dx
        dxb_ref[...] = dx.astype(BF16)
        _accumulate(dg_ref, jnp.sum(dhv * xhat, axis=0, keepdims=True), pl.program_id(0))

    in_specs = [_row(d), _vec(d), _row(d)] + ([_row(d)] if has_res else [])
    args = (x, g, dh) + ((dres,) if has_res else ())
    return pl.pallas_call(
        body,
        out_shape=(jax.ShapeDtypeStruct((rows, d), F32), jax.ShapeDtypeStruct((rows, d), BF16), jax.ShapeDtypeStruct((1, d), F32)),
        grid=(rows // ROW_TILE,), in_specs=in_specs, out_specs=(_row(d), _row(d), _vec(d)),
        compiler_params=_params(("arbitrary",)), name=f"rms_bwd_{rows}_{int(has_res)}",
    )(*args)


def _loss_head(x, g, target):
    rows, d = x.shape

    def body(x_ref, g_ref, t_ref, loss_ref, dx_ref, dxb_ref, dg_ref):
        xv = x_ref[...]
        gv = g_ref[...]
        rstd = lax.rsqrt(jnp.mean(xv * xv, axis=-1, keepdims=True) + RMS_EPS)
        xhat = xv * rstd
        err = xhat * gv - t_ref[...]
        part = 0.5 * jnp.sum(jnp.mean(err * err, axis=-1, keepdims=True), axis=0, keepdims=True)
        dy = err * (1.0 / d)
        gd = dy * gv
        dx = rstd * (gd - xhat * jnp.mean(gd * xhat, axis=-1, keepdims=True))
        dx_ref[...] = dx
        dxb_ref[...] = dx.astype(BF16)
        step = pl.program_id(0)
        _accumulate(loss_ref, jnp.broadcast_to(part, (1, LANE)), step)
        _accumulate(dg_ref, jnp.sum(dy * xhat, axis=0, keepdims=True), step)

    return pl.pallas_call(
        body,
        out_shape=(jax.ShapeDtypeStruct((1, LANE), F32), jax.ShapeDtypeStruct((rows, d), F32),
                   jax.ShapeDtypeStruct((rows, d), BF16), jax.ShapeDtypeStruct((1, d), F32)),
        grid=(rows // ROW_TILE,), in_specs=[_row(d), _vec(d), _row(d)],
        out_specs=(_vec(LANE), _row(d), _row(d), _vec(d)),
        compiler_params=_params(("arbitrary",)), name="loss_head",
    )(x, g, target)


def _sigmoid(x):
    return 1.0 / (1.0 + jnp.exp(-x))


def _ln_silu_fwd(c, ln_g, ln_b):
    rows, ch = c.shape

    def body(c_ref, g_ref, b_ref, o_ref):
        cv = c_ref[...]
        mu = jnp.mean(cv, axis=-1, keepdims=True)
        cen = cv - mu
        rstd = lax.rsqrt(jnp.mean(cen * cen, axis=-1, keepdims=True) + LN_EPS)
        y = cen * rstd * g_ref[...] + b_ref[...]
        o_ref[...] = (y * _sigmoid(y)).astype(BF16)

    return pl.pallas_call(
        body, out_shape=jax.ShapeDtypeStruct((rows, ch), BF16), grid=(rows // ROW_TILE,),
        in_specs=[_row(ch), _vec(ch), _vec(ch)], out_specs=_row(ch), compiler_params=_params(("parallel",)), name="ln_silu_fwd",
    )(c, ln_g, ln_b)


def _ln_silu_bwd(dmix, c, ln_g, ln_b):
    rows, ch = c.shape

    def body(dm_ref, c_ref, g_ref, b_ref, dc_ref, dg_ref, db_ref):
        cv = c_ref[...]
        gv = g_ref[...]
        mu = jnp.mean(cv, axis=-1, keepdims=True)
        cen = cv - mu
        rstd = lax.rsqrt(jnp.mean(cen * cen, axis=-1, keepdims=True) + LN_EPS)
        xhat = cen * rstd
        y = xhat * gv + b_ref[...]
        sg = _sigmoid(y)
        dy = dm_ref[...] * (sg * (1.0 + y * (1.0 - sg)))
        dxh = dy * gv
        dc = rstd * (dxh - jnp.mean(dxh, axis=-1, keepdims=True) - xhat * jnp.mean(dxh * xhat, axis=-1, keepdims=True))
        dc_ref[...] = dc
        step = pl.program_id(0)
        _accumulate(dg_ref, jnp.sum(dy * xhat, axis=0, keepdims=True), step)
        _accumulate(db_ref, jnp.sum(dy, axis=0, keepdims=True), step)

    return pl.pallas_call(
        body,
        out_shape=(jax.ShapeDtypeStruct((rows, ch), F32), jax.ShapeDtypeStruct((1, ch), F32), jax.ShapeDtypeStruct((1, ch), F32)),
        grid=(rows // ROW_TILE,), in_specs=[_row(ch), _row(ch), _vec(ch), _vec(ch)],
        out_specs=(_row(ch), _vec(ch), _vec(ch)), compiler_params=_params(("arbitrary",)), name="ln_silu_bwd",
    )(dmix, c, ln_g, ln_b)


def _col(rows, cb):
    return pl.BlockSpec((rows, cb), lambda j: (0, j))


def _conv_glu_fwd(pa, pg, ba, bg, w_dw, b_dw):
    seq, ch = pa.shape
    cb = LANE
    n_chunks = seq // CHUNK

    def body(pa_ref, pg_ref, ba_ref, bg_ref, w_ref, b_ref, v_ref, c_ref, vpad_ref):
        vpad_ref[0:CONV_PAD, :] = jnp.zeros((CONV_PAD, cb), F32)

        def glu(r, carry):
            r0 = pl.multiple_of(r * CHUNK, CHUNK)
            rows = pl.ds(r0, CHUNK)
            v = (pa_ref[rows, :] + ba_ref[...]) * _sigmoid(pg_ref[rows, :] + bg_ref[...])
            v_ref[rows, :] = v
            vpad_ref[pl.ds(r0 + CONV_PAD, CHUNK), :] = v
            return carry

        lax.fori_loop(0, n_chunks, glu, 0)

        def conv(r, carry):
            r0 = pl.multiple_of(r * CHUNK, CHUNK)
            win = vpad_ref[pl.ds(r0, CHUNK + CONV_PAD), :]
            acc = jnp.broadcast_to(b_ref[...], (CHUNK, cb))
            for t in range(CONV_W):
                off = CONV_PAD - (CONV_W - 1) + t
                acc = acc + w_ref[t:t + 1, :] * win[off:off + CHUNK, :]
            c_ref[pl.ds(r0, CHUNK), :] = acc
            return carry

        lax.fori_loop(0, n_chunks, conv, 0)

    return pl.pallas_call(
        body,
        out_shape=(jax.ShapeDtypeStruct((seq, ch), F32), jax.ShapeDtypeStruct((seq, ch), F32)),
        grid=(ch // cb,),
        in_specs=[_col(seq, cb), _col(seq, cb), _col(1, cb), _col(1, cb), _col(CONV_PAD, cb), _col(1, cb)],
        out_specs=(_col(seq, cb), _col(seq, cb)),
        scratch_shapes=[pltpu.VMEM((seq + CONV_PAD, cb), F32)],
        compiler_params=_params(("parallel",)), name="conv_glu_fwd",
    )(pa, pg, ba, bg, w_dw, b_dw)


def _conv_glu_bwd(dc, v, pa, pg, ba, bg, w_dw):
    seq, ch = dc.shape
    cb = LANE
    n_chunks = seq // CHUNK

    def body(dc_ref, v_ref, pa_ref, pg_ref, ba_ref, bg_ref, w_ref, da_ref, dg_ref, dw_ref, dbdw_ref, dba_ref, dbg_ref,
             dcpad_ref, vpad_ref):
        vpad_ref[0:CONV_PAD, :] = jnp.zeros((CONV_PAD, cb), F32)
        dcpad_ref[seq:seq + CONV_PAD, :] = jnp.zeros((CONV_PAD, cb), F32)
        dw_ref[...] = jnp.zeros((CONV_PAD, cb), F32)
        dbdw_ref[...] = jnp.zeros((1, cb), F32)
        dba_ref[...] = jnp.zeros((1, cb), F32)
        dbg_ref[...] = jnp.zeros((1, cb), F32)

        def fill(r, carry):
            r0 = pl.multiple_of(r * CHUNK, CHUNK)
            vpad_ref[pl.ds(r0 + CONV_PAD, CHUNK), :] = v_ref[pl.ds(r0, CHUNK), :]
            dcpad_ref[pl.ds(r0, CHUNK), :] = dc_ref[pl.ds(r0, CHUNK), :]
            return carry

        lax.fori_loop(0, n_chunks, fill, 0)

        def step(r, carry):
            r0 = pl.multiple_of(r * CHUNK, CHUNK)
            rows = pl.ds(r0, CHUNK)
            dwin = dcpad_ref[pl.ds(r0, CHUNK + CONV_PAD), :]
            vwin = vpad_ref[pl.ds(r0, CHUNK + CONV_PAD), :]
            dcur = dwin[0:CHUNK, :]
            dv = jnp.zeros((CHUNK, cb), F32)
            for t in range(CONV_W):
                fwd_off = CONV_W - 1 - t
                dv = dv + w_ref[t:t + 1, :] * dwin[fwd_off:fwd_off + CHUNK, :]
                off = CONV_PAD - (CONV_W - 1) + t
                dw_ref[t:t + 1, :] += jnp.sum(dcur * vwin[off:off + CHUNK, :], axis=0, keepdims=True)
            a = pa_ref[rows, :] + ba_ref[...]
            sg = _sigmoid(pg_ref[rows, :] + bg_ref[...])
            da = dv * sg
            dgate = dv * a * sg * (1.0 - sg)
            da_ref[rows, :] = da.astype(BF16)
            dg_ref[rows, :] = dgate.astype(BF16)
            dbdw_ref[...] += jnp.sum(dcur, axis=0, keepdims=True)
            dba_ref[...] += jnp.sum(da, axis=0, keepdims=True)
            dbg_ref[...] += jnp.sum(dgate, axis=0, keepdims=True)
            return carry

        lax.fori_loop(0, n_chunks, step, 0)

    return pl.pallas_call(
        body,
        out_shape=(jax.ShapeDtypeStruct((seq, ch), BF16), jax.ShapeDtypeStruct((seq, ch), BF16),
                   jax.ShapeDtypeStruct((CONV_PAD, ch), F32), jax.ShapeDtypeStruct((1, ch), F32),
                   jax.ShapeDtypeStruct((1, ch), F32), jax.ShapeDtypeStruct((1, ch), F32)),
        grid=(ch // cb,),
        in_specs=[_col(seq, cb), _col(seq, cb), _col(seq, cb), _col(seq, cb), _col(1, cb), _col(1, cb), _col(CONV_PAD, cb)],
        out_specs=(_col(seq, cb), _col(seq, cb), _col(CONV_PAD, cb), _col(1, cb), _col(1, cb), _col(1, cb)),
        scratch_shapes=[pltpu.VMEM((seq + CONV_PAD, cb), F32), pltpu.VMEM((seq + CONV_PAD, cb), F32)],
        compiler_params=_params(("parallel",)), name="conv_glu_bwd",
    )(dc, v, pa, pg, ba, bg, w_dw)


def _ffn_conv(pad_ref, r0, w_ref, b_ref, cb):
    win = pad_ref[pl.ds(r0, CHUNK + FFN_PAD), :]
    y = jnp.broadcast_to(b_ref[...], (CHUNK, cb))
    for t in range(FFN_CONV_W):
        off = FFN_PAD - (FFN_CONV_W - 1) + t
        y = y + w_ref[t:t + 1, :] * win[off:off + CHUNK, :]
    return y, win


def _ffn_mid_fwd(ug, uv, wg, wv, bg, bv):
    seq, ch = ug.shape
    cb = _tile(ch, 256)
    n_chunks = seq // CHUNK

    def body(ug_ref, uv_ref, wg_ref, wv_ref, bg_ref, bv_ref, z_ref, gpad_ref, vpad_ref):
        gpad_ref[0:FFN_PAD, :] = jnp.zeros((FFN_PAD, cb), F32)
        vpad_ref[0:FFN_PAD, :] = jnp.zeros((FFN_PAD, cb), F32)

        def fill(r, carry):
            r0 = pl.multiple_of(r * CHUNK, CHUNK)
            gpad_ref[pl.ds(r0 + FFN_PAD, CHUNK), :] = ug_ref[pl.ds(r0, CHUNK), :]
            vpad_ref[pl.ds(r0 + FFN_PAD, CHUNK), :] = uv_ref[pl.ds(r0, CHUNK), :]
            return carry

        lax.fori_loop(0, n_chunks, fill, 0)

        def step(r, carry):
            r0 = pl.multiple_of(r * CHUNK, CHUNK)
            yg, _ = _ffn_conv(gpad_ref, r0, wg_ref, bg_ref, cb)
            yv, _ = _ffn_conv(vpad_ref, r0, wv_ref, bv_ref, cb)
            z_ref[pl.ds(r0, CHUNK), :] = (yg * _sigmoid(yg) * yv).astype(BF16)
            return carry

        lax.fori_loop(0, n_chunks, step, 0)

    return pl.pallas_call(
        body, out_shape=jax.ShapeDtypeStruct((seq, ch), BF16), grid=(ch // cb,),
        in_specs=[_col(seq, cb), _col(seq, cb), _col(FFN_PAD, cb), _col(FFN_PAD, cb), _col(1, cb), _col(1, cb)],
        out_specs=_col(seq, cb),
        scratch_shapes=[pltpu.VMEM((seq + FFN_PAD, cb), F32), pltpu.VMEM((seq + FFN_PAD, cb), F32)],
        compiler_params=_params(("parallel",)), name="ffn_mid_fwd",
    )(ug, uv, wg, wv, bg, bv)


def _ffn_mid_bwd(ug, uv, dz, wg, wv, bg, bv):
    seq, ch = ug.shape
    cb = _tile(ch, 256)
    n_chunks = seq // CHUNK

    def body(ug_ref, uv_ref, dz_ref, wg_ref, wv_ref, bg_ref, bv_ref, dug_ref, duv_ref, dwg_ref, dwv_ref, dbg_ref, dbv_ref,
             gpad_ref, vpad_ref, dyg_ref, dyv_ref):
        gpad_ref[0:FFN_PAD, :] = jnp.zeros((FFN_PAD, cb), F32)
        vpad_ref[0:FFN_PAD, :] = jnp.zeros((FFN_PAD, cb), F32)
        dyg_ref[seq:seq + FFN_PAD, :] = jnp.zeros((FFN_PAD, cb), F32)
        dyv_ref[seq:seq + FFN_PAD, :] = jnp.zeros((FFN_PAD, cb), F32)
        dwg_ref[...] = jnp.zeros((FFN_PAD, cb), F32)
        dwv_ref[...] = jnp.zeros((FFN_PAD, cb), F32)
        dbg_ref[...] = jnp.zeros((1, cb), F32)
        dbv_ref[...] = jnp.zeros((1, cb), F32)

        def fill(r, carry):
            r0 = pl.multiple_of(r * CHUNK, CHUNK)
            gpad_ref[pl.ds(r0 + FFN_PAD, CHUNK), :] = ug_ref[pl.ds(r0, CHUNK), :]
            vpad_ref[pl.ds(r0 + FFN_PAD, CHUNK), :] = uv_ref[pl.ds(r0, CHUNK), :]
            return carry

        lax.fori_loop(0, n_chunks, fill, 0)

        def grads_of_conv_out(r, carry):
            r0 = pl.multiple_of(r * CHUNK, CHUNK)
            rows = pl.ds(r0, CHUNK)
            yg, gwin = _ffn_conv(gpad_ref, r0, wg_ref, bg_ref, cb)
            yv, vwin = _ffn_conv(vpad_ref, r0, wv_ref, bv_ref, cb)
            dzv = dz_ref[rows, :]
            sg = _sigmoid(yg)
            dyg = dzv * yv * (sg * (1.0 + yg * (1.0 - sg)))
            dyv = dzv * yg * sg
            dyg_ref[rows, :] = dyg
            dyv_ref[rows, :] = dyv
            for t in range(FFN_CONV_W):
                off = FFN_PAD - (FFN_CONV_W - 1) + t
                dwg_ref[t:t + 1, :] += jnp.sum(dyg * gwin[off:off + CHUNK, :], axis=0, keepdims=True)
                dwv_ref[t:t + 1, :] += jnp.sum(dyv * vwin[off:off + CHUNK, :], axis=0, keepdims=True)
            dbg_ref[...] += jnp.sum(dyg, axis=0, keepdims=True)
            dbv_ref[...] += jnp.sum(dyv, axis=0, keepdims=True)
            return carry

        lax.fori_loop(0, n_chunks, grads_of_conv_out, 0)

        def grads_of_conv_in(r, carry):
            r0 = pl.multiple_of(r * CHUNK, CHUNK)
            gwin = dyg_ref[pl.ds(r0, CHUNK + FFN_PAD), :]
            vwin = dyv_ref[pl.ds(r0, CHUNK + FFN_PAD), :]
            dug = jnp.zeros((CHUNK, cb), F32)
            duv = jnp.zeros((CHUNK, cb), F32)
            for t in range(FFN_CONV_W):
                off = FFN_CONV_W - 1 - t
                dug = dug + wg_ref[t:t + 1, :] * gwin[off:off + CHUNK, :]
                duv = duv + wv_ref[t:t + 1, :] * vwin[off:off + CHUNK, :]
            dug_ref[pl.ds(r0, CHUNK), :] = dug.astype(BF16)
            duv_ref[pl.ds(r0, CHUNK), :] = duv.astype(BF16)
            return carry

        lax.fori_loop(0, n_chunks, grads_of_conv_in, 0)

    return pl.pallas_call(
        body,
        out_shape=(jax.ShapeDtypeStruct((seq, ch), BF16), jax.ShapeDtypeStruct((seq, ch), BF16),
                   jax.ShapeDtypeStruct((FFN_PAD, ch), F32), jax.ShapeDtypeStruct((FFN_PAD, ch), F32),
                   jax.ShapeDtypeStruct((1, ch), F32), jax.ShapeDtypeStruct((1, ch), F32)),
        grid=(ch // cb,),
        in_specs=[_col(seq, cb), _col(seq, cb), _col(seq, cb), _col(FFN_PAD, cb), _col(FFN_PAD, cb), _col(1, cb), _col(1, cb)],
        out_specs=(_col(seq, cb), _col(seq, cb), _col(FFN_PAD, cb), _col(FFN_PAD, cb), _col(1, cb), _col(1, cb)),
        scratch_shapes=[pltpu.VMEM((seq + FFN_PAD, cb), F32) for _ in range(4)],
        compiler_params=_params(("parallel",)), name="ffn_mid_bwd",
    )(ug, uv, dz, wg, wv, bg, bv)


def _dot(a, b, mode):
    return lax.dot_general(a.astype(BF16), b.astype(BF16), _DIMS[mode], preferred_element_type=F32)


def _head(h):
    return slice(h * HEAD_DIM, (h + 1) * HEAD_DIM)


def _mem_softmax(q, k):
    s = _dot(q, k, "nt") * ATT_SCALE
    e = jnp.exp(s - jnp.max(s, axis=-1, keepdims=True))
    return e / jnp.sum(e, axis=-1, keepdims=True)


def _memattn_fwd(pq, mk, mv):
    seq, w = pq.shape
    m = mk.shape[0]

    def body(q_ref, k_ref, v_ref, o_ref):
        for h in range(N_MEM_HEADS):
            p = _mem_softmax(q_ref[:, _head(h)], k_ref[:, _head(h)])
            o_ref[:, _head(h)] = _dot(p, v_ref[:, _head(h)], "nn").astype(BF16)

    full = pl.BlockSpec((m, w), lambda i: (0, 0))
    return pl.pallas_call(
        body, out_shape=jax.ShapeDtypeStruct((seq, w), BF16), grid=(seq // ROW_TILE,),
        in_specs=[_row(w), full, full], out_specs=_row(w), compiler_params=_params(("parallel",)), name="memattn_fwd",
    )(pq, mk, mv)


def _memattn_bwd(pq, mk, mv, dmo):
    seq, w = pq.shape
    m = mk.shape[0]

    def body(q_ref, k_ref, v_ref, do_ref, dq_ref, dk_ref, dv_ref):
        step = pl.program_id(0)

        @pl.when(step == 0)
        def _():
            dk_ref[...] = jnp.zeros((m, w), F32)
            dv_ref[...] = jnp.zeros((m, w), F32)

        for h in range(N_MEM_HEADS):
            q, k, v, do = q_ref[:, _head(h)], k_ref[:, _head(h)], v_ref[:, _head(h)], do_ref[:, _head(h)]
            p = _mem_softmax(q, k)
            dp = _dot(do, v, "nt")
            ds = p * (dp - jnp.sum(dp * p, axis=-1, keepdims=True))
            dq_ref[:, _head(h)] = (_dot(ds, k, "nn") * ATT_SCALE).astype(BF16)
            dk_ref[:, _head(h)] += _dot(ds, q, "tn") * ATT_SCALE
            dv_ref[:, _head(h)] += _dot(p, do, "tn")

    full = pl.BlockSpec((m, w), lambda i: (0, 0))
    return pl.pallas_call(
        body,
        out_shape=(jax.ShapeDtypeStruct((seq, w), BF16), jax.ShapeDtypeStruct((m, w), F32), jax.ShapeDtypeStruct((m, w), F32)),
        grid=(seq // ROW_TILE,), in_specs=[_row(w), full, full, _row(w)], out_specs=(_row(w), full, full),
        compiler_params=_params(("arbitrary",)), name="memattn_bwd",
    )(pq, mk, mv, dmo)


def _causal(s, q_blk, k_blk):
    qpos = q_blk * ROW_TILE + lax.broadcasted_iota(jnp.int32, (ROW_TILE, ROW_TILE), 0)
    kpos = k_blk * ROW_TILE + lax.broadcasted_iota(jnp.int32, (ROW_TILE, ROW_TILE), 1)
    return jnp.where(kpos <= qpos, s, NEG_BIG)


def _fox_fwd(q, k, v, cq, ck):
    seq, w = q.shape
    nblk = seq // ROW_TILE

    def body(q_ref, k_ref, v_ref, cq_ref, ck_ref, o_ref, ob_ref, lse_ref):
        i = pl.program_id(1)
        for h in range(2):
            qh = q_ref[:, _head(h)].astype(BF16)
            cqh = cq_ref[0, :, h * HEAD_DIM:h * HEAD_DIM + 1]

            def step(j, carry):
                m_run, l_run, acc = carry
                rows = pl.ds(pl.multiple_of(j * ROW_TILE, ROW_TILE), ROW_TILE)
                s = _dot(qh, k_ref[rows, _head(h)], "nt") * ATT_SCALE + cqh - ck_ref[0, j, h:h + 1, :]
                s = _causal(s, i, j)
                m_new = jnp.maximum(m_run, jnp.max(s, axis=-1, keepdims=True))
                alpha = jnp.exp(m_run - m_new)
                p = jnp.exp(s - m_new)
                l_new = alpha * l_run + jnp.sum(p, axis=-1, keepdims=True)
                acc = alpha * acc + _dot(p, v_ref[rows, _head(h)], "nn")
                return m_new, l_new, acc

            init = (jnp.full((ROW_TILE, 1), NEG_BIG, F32), jnp.zeros((ROW_TILE, 1), F32), jnp.zeros((ROW_TILE, HEAD_DIM), F32))
            m_run, l_run, acc = lax.fori_loop(0, i + 1, step, init)
            o = acc / l_run
            o_ref[:, _head(h)] = o
            ob_ref[:, _head(h)] = o.astype(BF16)
            lse_ref[0, :, _head(h)] = jnp.broadcast_to(m_run + jnp.log(l_run), (ROW_TILE, HEAD_DIM))

    blk = pl.BlockSpec((ROW_TILE, LANE), lambda hp, i: (i, hp))
    full = pl.BlockSpec((seq, LANE), lambda hp, i: (0, hp))
    cq_spec = pl.BlockSpec((1, ROW_TILE, LANE), lambda hp, i: (hp, i, 0))
    ck_spec = pl.BlockSpec((1, nblk, 8, ROW_TILE), lambda hp, i: (hp, 0, 0, 0))
    return pl.pallas_call(
        body,
        out_shape=(jax.ShapeDtypeStruct((seq, w), F32), jax.ShapeDtypeStruct((seq, w), BF16),
                   jax.ShapeDtypeStruct((N_HEAD_PAIRS, seq, LANE), F32)),
        grid=(N_HEAD_PAIRS, nblk), in_specs=[blk, full, full, cq_spec, ck_spec], out_specs=(blk, blk, cq_spec),
        compiler_params=_params(("parallel", "parallel")), name="fox_fwd",
    )(q, k, v, cq, ck)


def _fox_bwd(q, k, v, cq, ck, o, lse, do):
    seq, w = q.shape
    nblk = seq // ROW_TILE

    def body(q_ref, k_ref, v_ref, cq_ref, ck_ref, o_ref, lse_ref, do_ref, dq_ref, dk_ref, dv_ref, dcq_ref, dck_ref):
        kj = pl.program_id(1)

        @pl.when(kj == 0)
        def _():
            dq_ref[...] = jnp.zeros((seq, LANE), F32)
            dcq_ref[...] = jnp.zeros((1, seq, LANE), F32)

        dck_ref[...] = jnp.zeros((1, 1, 8, ROW_TILE), F32)
        for h in range(2):
            kh = k_ref[:, _head(h)].astype(BF16)
            vh = v_ref[:, _head(h)].astype(BF16)
            ckh = ck_ref[0, 0, h:h + 1, :]
            one = slice(h * HEAD_DIM, h * HEAD_DIM + 1)

            def step(i, carry):
                dk, dv, dc = carry
                rows = pl.ds(pl.multiple_of(i * ROW_TILE, ROW_TILE), ROW_TILE)
                qh = q_ref[rows, _head(h)].astype(BF16)
                doh = do_ref[rows, _head(h)]
                s = _dot(qh, kh, "nt") * ATT_SCALE + cq_ref[0, rows, one] - ckh
                p = jnp.exp(_causal(s, i, kj) - lse_ref[0, rows, one])
                dp = _dot(doh, vh, "nt")
                delta = jnp.sum(doh * o_ref[rows, _head(h)], axis=-1, keepdims=True)
                ds = p * (dp - delta)
                dq_ref[rows, _head(h)] += _dot(ds, kh, "nn") * ATT_SCALE
                dcq_ref[0, rows, one] += jnp.sum(ds, axis=-1, keepdims=True)
                dk = dk + _dot(ds, qh, "tn") * ATT_SCALE
                dv = dv + _dot(p, doh, "tn")
                dc = dc - jnp.sum(ds, axis=0, keepdims=True)
                return dk, dv, dc

            init = (jnp.zeros((ROW_TILE, HEAD_DIM), F32), jnp.zeros((ROW_TILE, HEAD_DIM), F32), jnp.zeros((1, ROW_TILE), F32))
            dk, dv, dc = lax.fori_loop(kj, nblk, step, init)
            dk_ref[:, _head(h)] = dk
            dv_ref[:, _head(h)] = dv
            dck_ref[0, 0, h:h + 1, :] = dc

    blk = pl.BlockSpec((ROW_TILE, LANE), lambda hp, j: (j, hp))
    full = pl.BlockSpec((seq, LANE), lambda hp, j: (0, hp))
    cq_spec = pl.BlockSpec((1, seq, LANE), lambda hp, j: (hp, 0, 0))
    ck_spec = pl.BlockSpec((1, 1, 8, ROW_TILE), lambda hp, j: (hp, j, 0, 0))
    return pl.pallas_call(
        body,
        out_shape=(jax.ShapeDtypeStruct((seq, w), F32), jax.ShapeDtypeStruct((seq, w), F32), jax.ShapeDtypeStruct((seq, w), F32),
                   jax.ShapeDtypeStruct((N_HEAD_PAIRS, seq, LANE), F32), jax.ShapeDtypeStruct((N_HEAD_PAIRS, nblk, 8, ROW_TILE), F32)),
        grid=(N_HEAD_PAIRS, nblk), in_specs=[full, blk, blk, cq_spec, ck_spec, full, cq_spec, full],
        out_specs=(full, blk, blk, cq_spec, ck_spec),
        compiler_params=_params(("parallel", "arbitrary")), name="fox_bwd",
    )(q, k, v, cq, ck, o, lse, do)


def _split3(x):
    hi = x.astype(BF16)
    r1 = x - hi.astype(F32)
    mid = r1.astype(BF16)
    lo = (r1 - mid.astype(F32)).astype(BF16)
    return hi, mid, lo


def _tri_sum(tri, x):
    hi, mid, lo = _split3(x)
    dot = lambda p: lax.dot_general(tri, p, _DIMS["nn"], preferred_element_type=F32)
    return (dot(lo) + dot(mid)) + dot(hi)


def _tri(lower):
    r = lax.broadcasted_iota(jnp.int32, (LANE, LANE), 0)
    c = lax.broadcasted_iota(jnp.int32, (LANE, LANE), 1)
    return jnp.where((c <= r) if lower else (c >= r), 1.0, 0.0).astype(BF16)


def _fgate_fwd(fl, bf):
    seq, w = fl.shape
    nb = seq // LANE

    def body(f_ref, b_ref, cum_ref):
        tri = _tri(True)

        def step(i, carry):
            rows = pl.ds(pl.multiple_of(i * LANE, LANE), LANE)
            x = f_ref[rows, :] + b_ref[...]
            logsig = jnp.minimum(x, 0.0) - jnp.log(1.0 + jnp.exp(-jnp.abs(x)))
            cum = _tri_sum(tri, logsig) + carry
            cum_ref[rows, :] = cum
            return cum[LANE - 1:LANE, :]

        lax.fori_loop(0, nb, step, jnp.zeros((1, w), F32))

    return pl.pallas_call(body, out_shape=jax.ShapeDtypeStruct((seq, w), F32), compiler_params=_params(), name="fgate_fwd")(fl, bf)


def _fgate_bwd(fl, bf, dcum):
    seq, w = fl.shape
    nb = seq // LANE

    def body(f_ref, b_ref, dc_ref, df_ref, db_ref):
        tri = _tri(False)

        def step(i, carry):
            tail, db = carry
            rows = pl.ds(pl.multiple_of((nb - 1 - i) * LANE, LANE), LANE)
            suffix = _tri_sum(tri, dc_ref[rows, :]) + tail
            df = suffix * (1.0 - _sigmoid(f_ref[rows, :] + b_ref[...]))
            df_ref[rows, :] = df
            return suffix[0:1, :], db + jnp.sum(df, axis=0, keepdims=True)

        _, db = lax.fori_loop(0, nb, step, (jnp.zeros((1, w), F32), jnp.zeros((1, w), F32)))
        db_ref[...] = db

    return pl.pallas_call(
        body, out_shape=(jax.ShapeDtypeStruct((seq, w), F32), jax.ShapeDtypeStruct((1, w), F32)),
        compiler_params=_params(), name="fgate_bwd",
    )(fl, bf, dcum)


def _cum_layouts(cum):
    seq = cum.shape[0]
    nblk = seq // ROW_TILE
    heads = cum[:, :N_FOX_HEADS]
    cq = jnp.repeat(heads, HEAD_DIM, axis=1).reshape(seq, N_HEAD_PAIRS, LANE).transpose(1, 0, 2)
    ck = heads.T.reshape(N_HEAD_PAIRS, 2, nblk, ROW_TILE).transpose(0, 2, 1, 3)
    ck = jnp.pad(ck, ((0, 0), (0, 0), (0, 6), (0, 0)))
    return cq, ck


def _dcum_from_layouts(dcq, dck):
    seq = dcq.shape[1]
    key_side = dck[:, :, :2, :].transpose(0, 2, 1, 3).reshape(N_FOX_HEADS, seq).T
    query_side = dcq[:, :, ::HEAD_DIM].transpose(1, 0, 2).reshape(seq, N_FOX_HEADS)
    return jnp.pad(key_side + query_side, ((0, 0), (0, F_PAD - N_FOX_HEADS)))


def _local_step(x, mem, target, w):
    seq = x.shape[0]
    vec = lambda a: a.reshape(1, -1)
    mem_n = _rms_fwd(mem, vec(w["g_mem"]))
    saved = []
    shared = None
    for l in range(N_LAYERS):
        s = {"x_in": x}
        s["h"] = h = _rms_fwd(x, vec(w["g_mix"][l]))
        s["mk"] = mk = _mm(mem_n, w["w_mk"][l])
        s["mv"] = mv = _mm(mem_n, w["w_mv"][l])
        if l < N_A:
            s["pa"] = pa = _mm(h, w["w_a"][l])
            s["pg"] = pg = _mm(h, w["w_g"][l])
            s["pq"] = pq = _mm(h, w["w_qm"][l])
            s["v"], s["c"] = _conv_glu_fwd(pa, pg, vec(w["b_glu_a"][l]), vec(w["b_glu_g"][l]), w["w_dw_a"][l], vec(w["b_dw_a"][l]))
            s["mix"] = mix = _ln_silu_fwd(s["c"], vec(w["ln_g"][l]), vec(w["ln_b"][l]))
        else:
            if l == N_A:
                shared = {"x_in": x}
                shared["hk"] = hk = _rms_fwd(x, vec(w["g_kv"]))
                shared["k"] = _mm(hk, w["w_k"])
                shared["v"] = _mm(hk, w["w_v"])
                shared["fl"] = _mm(hk, w["w_f"])
                cum = _fgate_fwd(shared["fl"], w["b_f"])
                shared["cq"], shared["ck"] = _cum_layouts(cum)
            s["q"] = q = _mm(h, w["w_q"][l - N_A])
            s["pq"] = pq = _mm(h, w["w_qm"][l])
            s["o"], mix, s["lse"] = _fox_fwd(q, shared["k"], shared["v"], shared["cq"], shared["ck"])
            s["mix"] = mix
        s["mo"] = mo = _memattn_fwd(pq, mk, mv)
        x = _mm(mix, w["w_o_mix"][l], add=x)
        s["x_mid"] = x = _mm(mo, w["w_o_mem"][l], add=x)
        s["hf"] = hf = _rms_fwd(x, vec(w["g_ffn"][l]))
        s["ug"] = ug = _mm(hf, w["w_up_g"][l])
        s["uv"] = uv = _mm(hf, w["w_up_v"][l])
        s["z"] = z = _ffn_mid_fwd(ug, uv, w["w_dw_f_g"][l], w["w_dw_f_v"][l], vec(w["b_dw_f_g"][l]), vec(w["b_dw_f_v"][l]))
        x = _mm(z, w["w_down"][l], add=x)
        saved.append(s)

    loss_row, dx, dxb, dg_final = _loss_head(x, vec(w["g_final"]), target)
    loss = loss_row[0, 0]

    g = {n: [None] * N_LAYERS for n in ("g_mix", "w_mem_kv", "w_out", "g_ffn", "w_up", "w_dw_f", "b_dw_f", "w_down")}
    for n in ("w_in_a", "b_glu", "w_dw_a", "b_dw_a", "ln_g", "ln_b", "w_in_b"):
        g[n] = [None] * N_A
    g["g_final"] = dg_final[0]
    dmem_n = None
    dk_sum = dv_sum = dcq_sum = dck_sum = None
    for l in reversed(range(N_LAYERS)):
        s = saved[l]
        dz = _mm(dxb, w["w_down"][l], "nt")
        g["w_down"][l] = _mm(s["z"], dxb, "tn")
        dug, duv, dwg, dwv, dbg, dbv = _ffn_mid_bwd(s["ug"], s["uv"], dz, w["w_dw_f_g"][l], w["w_dw_f_v"][l],
                                                    vec(w["b_dw_f_g"][l]), vec(w["b_dw_f_v"][l]))
        g["w_up"][l] = jnp.concatenate([_mm(s["hf"], dug, "tn"), _mm(s["hf"], duv, "tn")], axis=1)
        g["w_dw_f"][l] = jnp.concatenate([dwg[:FFN_CONV_W], dwv[:FFN_CONV_W]], axis=1)
        g["b_dw_f"][l] = jnp.concatenate([dbg[0], dbv[0]])
        dhf = _mm(duv, w["w_up_v"][l], "nt", add=_mm(dug, w["w_up_g"][l], "nt"))
        dx, dxb, dg_ffn = _rms_bwd(s["x_mid"], vec(w["g_ffn"][l]), dhf, dx)
        g["g_ffn"][l] = dg_ffn[0]
        dmix = _mm(dxb, w["w_o_mix"][l], "nt")
        dmo = _mm(dxb, w["w_o_mem"][l], "nt")
        g["w_out"][l] = jnp.concatenate([_mm(s["mix"], dxb, "tn"), _mm(s["mo"], dxb, "tn")], axis=0)
        dpq, dmk, dmv = _memattn_bwd(s["pq"], s["mk"], s["mv"], dmo)
        g["w_mem_kv"][l] = jnp.concatenate([_mm(mem_n, dmk, "tn"), _mm(mem_n, dmv, "tn")], axis=1)
        dmem_n = _mm(dmk, w["w_mk"][l], "nt", add=dmem_n)
        dmem_n = _mm(dmv, w["w_mv"][l], "nt", add=dmem_n)
        if l < N_A:
            dc, dlng, dlnb = _ln_silu_bwd(dmix, s["c"], vec(w["ln_g"][l]), vec(w["ln_b"][l]))
            da, dgate, dwdw, dbdw, dba, dbg2 = _conv_glu_bwd(dc, s["v"], s["pa"], s["pg"], vec(w["b_glu_a"][l]),
                                                              vec(w["b_glu_g"][l]), w["w_dw_a"][l])
            g["ln_g"][l], g["ln_b"][l], g["b_dw_a"][l] = dlng[0], dlnb[0], dbdw[0]
            g["w_dw_a"][l] = dwdw[:CONV_W]
            g["b_glu"][l] = jnp.concatenate([dba[0], dbg2[0]])
            g["w_in_a"][l] = jnp.concatenate([_mm(s["h"], da, "tn"), _mm(s["h"], dgate, "tn"), _mm(s["h"], dpq, "tn")], axis=1)
            dh = _mm(da, w["w_a"][l], "nt")
            dh = _mm(dgate, w["w_g"][l], "nt", add=dh)
            dh = _mm(dpq, w["w_qm"][l], "nt", add=dh)
        else:
            dq, dk, dv, dcq, dck = _fox_bwd(s["q"], shared["k"], shared["v"], shared["cq"], shared["ck"], s["o"], s["lse"], dmix)
            dk_sum = dk if dk_sum is None else dk_sum + dk
            dv_sum = dv if dv_sum is None else dv_sum + dv
            dcq_sum = dcq if dcq_sum is None else dcq_sum + dcq
            dck_sum = dck if dck_sum is None else dck_sum + dck
            g["w_in_b"][l - N_A] = jnp.concatenate([_mm(s["h"], dq, "tn"), _mm(s["h"], dpq, "tn")], axis=1)
            dh = _mm(dq, w["w_q"][l - N_A], "nt")
            dh = _mm(dpq, w["w_qm"][l], "nt", add=dh)
        dx, dxb, dg_mix = _rms_bwd(s["x_in"], vec(w["g_mix"][l]), dh, dx)
        g["g_mix"][l] = dg_mix[0]
        if l == N_A:
            df, dbf = _fgate_bwd(shared["fl"], w["b_f"], _dcum_from_layouts(dcq_sum, dck_sum))
            hk = shared["hk"]
            g["w_kvf"] = jnp.concatenate([_mm(hk, dk_sum, "tn"), _mm(hk, dv_sum, "tn"), _mm(hk, df, "tn")[:, :N_FOX_HEADS]], axis=1)
            g["b_f"] = dbf[0, :N_FOX_HEADS]
            dhk = _mm(dk_sum, w["w_k"], "nt")
            dhk = _mm(dv_sum, w["w_v"], "nt", add=dhk)
            dhk = _mm(df, w["w_f"], "nt", add=dhk)
            dx, dxb, dg_kv = _rms_bwd(shared["x_in"], vec(w["g_kv"]), dhk, dx)
            g["g_kv"] = dg_kv[0]
    _, _, dg_mem = _rms_bwd(mem, vec(w["g_mem"]), dmem_n, None)
    g["g_mem"] = dg_mem[0]
    grads = {n: (jnp.stack(v) if isinstance(v, list) else v) for n, v in g.items()}
    return loss, dx, grads


SHARDED = {
    "w_in_a": ((N_A, D_MODEL, 2 * CONV_CH + MEM_W), 2),
    "b_glu": ((N_A, 2 * CONV_CH), 1),
    "w_dw_a": ((N_A, CONV_W, CONV_CH), 2),
    "b_dw_a": ((N_A, CONV_CH), 1),
    "ln_g": ((N_A, CONV_CH), 1),
    "ln_b": ((N_A, CONV_CH), 1),
    "w_kvf": ((D_MODEL, 2 * CONV_CH + N_FOX_HEADS), 1),
    "w_in_b": ((N_LAYERS - N_A, D_MODEL, CONV_CH + MEM_W), 1),
    "w_mem_kv": ((N_LAYERS, D_MODEL, 2 * MEM_W), 1),
    "w_out": ((N_LAYERS, D_MODEL, D_MODEL), 1),
    "w_up": ((N_LAYERS, D_MODEL, 2 * D_FF), 2),
    "w_dw_f": ((N_LAYERS, FFN_CONV_W, 2 * D_FF), 2),
    "w_down": ((N_LAYERS, D_FF, D_MODEL), 1),
}
REPLICATED = {
    "g_mix": (N_LAYERS, D_MODEL), "g_kv": (D_MODEL,), "b_f": (N_FOX_HEADS,), "g_mem": (D_MODEL,),
    "g_ffn": (N_LAYERS, D_MODEL), "b_dw_f": (N_LAYERS, 2 * D_FF), "g_final": (D_MODEL,),
}
WEIGHT_ORDER = ["g_mix", "w_in_a", "b_glu", "w_dw_a", "b_dw_a", "ln_g", "ln_b", "g_kv", "w_kvf", "b_f", "w_in_b", "g_mem",
                "w_mem_kv", "w_out", "g_ffn", "w_up", "w_dw_f", "b_dw_f", "w_down", "g_final"]


def _shard_shape(name):
    shape, axis = SHARDED[name]
    return tuple(d // N_CHIPS if i == axis else d for i, d in enumerate(shape))


def _size(shape):
    n = 1
    for d in shape:
        n *= d
    return n


F32_LEAVES = ("b_glu", "w_dw_a", "b_dw_a", "ln_g", "ln_b", "w_dw_f")
BF16_LEAVES = tuple(n for n in SHARDED if n not in F32_LEAVES)
_HALF_ROW_TILE = 16


def _shard_elems(names):
    return sum(_size(_shard_shape(n)) for n in names)


def _flat_rows(elems):
    return -(-elems // (FLAT_COLS * 2 * _HALF_ROW_TILE)) * 2 * _HALF_ROW_TILE


G_ROWS = _flat_rows(_shard_elems(SHARDED))
REP_ELEMS = sum(_size(s) for s in REPLICATED.values())
REP_ROWS = 8
REP_COLS = -(-REP_ELEMS // (REP_ROWS * LANE)) * LANE


def _flatten_weights(leaves, names, dtype):
    rows = _flat_rows(_shard_elems(names))
    flat = jnp.concatenate([leaves[n].reshape(-1).astype(dtype) for n in names])
    flat = jnp.pad(flat, (0, rows * FLAT_COLS - _shard_elems(names)))
    return flat.reshape(2, rows // 2, FLAT_COLS)


def _unflatten_grads(flat):
    flat = flat.reshape(-1)
    out, off = {}, 0
    for n in SHARDED:
        shp = _shard_shape(n)
        out[n] = flat[off:off + _size(shp)].reshape(shp)
        off += _size(shp)
    return out


def _gathered_to_full(gathered, names):
    flat = gathered.reshape(N_CHIPS, -1)
    out, off = {}, 0
    for n in names:
        shape, axis = SHARDED[n]
        shp = _shard_shape(n)
        blocks = flat[:, off:off + _size(shp)].reshape((N_CHIPS,) + shp)
        out[n] = jnp.moveaxis(blocks, 0, axis).reshape(shape)
        off += _size(shp)
    return out


def _full_to_scatter(grads):
    parts = []
    for n, (shape, axis) in SHARDED.items():
        gl = grads[n]
        assert gl.shape == shape, (n, gl.shape, shape)
        split = gl.reshape(shape[:axis] + (N_CHIPS, shape[axis] // N_CHIPS) + shape[axis + 1:])
        parts.append(jnp.moveaxis(split, axis, 0).reshape(N_CHIPS, -1).astype(BF16))
    flat = jnp.concatenate(parts, axis=1)
    flat = jnp.pad(flat, ((0, 0), (0, G_ROWS * FLAT_COLS - _shard_elems(SHARDED))))
    return flat.reshape(N_CHIPS, G_ROWS, FLAT_COLS)


def _split_weights(full, rep):
    w = dict(rep)
    a = full["w_in_a"]
    w["w_a"], w["w_g"] = a[:, :, :CONV_CH], a[:, :, CONV_CH:2 * CONV_CH]
    b = full["w_in_b"]
    w["w_q"] = b[:, :, :CONV_CH]
    w["w_qm"] = jnp.concatenate([a[:, :, 2 * CONV_CH:], b[:, :, CONV_CH:]], axis=0)
    bg = full["b_glu"]
    w["b_glu_a"], w["b_glu_g"] = bg[:, :CONV_CH], bg[:, CONV_CH:]
    w["w_dw_a"] = jnp.pad(full["w_dw_a"], ((0, 0), (0, CONV_PAD - CONV_W), (0, 0)))
    for n in ("b_dw_a", "ln_g", "ln_b"):
        w[n] = full[n]
    kvf = full["w_kvf"]
    w["w_k"], w["w_v"] = kvf[:, :CONV_CH], kvf[:, CONV_CH:2 * CONV_CH]
    w["w_f"] = jnp.pad(kvf[:, 2 * CONV_CH:], ((0, 0), (0, F_PAD - N_FOX_HEADS)))
    w["b_f"] = jnp.pad(rep["b_f"], (0, F_PAD - N_FOX_HEADS)).reshape(1, F_PAD)
    mkv = full["w_mem_kv"]
    w["w_mk"], w["w_mv"] = mkv[:, :, :MEM_W], mkv[:, :, MEM_W:]
    wo = full["w_out"]
    w["w_o_mix"], w["w_o_mem"] = wo[:, :CONV_CH, :], wo[:, CONV_CH:, :]
    up = full["w_up"]
    w["w_up_g"], w["w_up_v"] = up[:, :, :D_FF], up[:, :, D_FF:]
    dwf = jnp.pad(full["w_dw_f"], ((0, 0), (0, FFN_PAD - FFN_CONV_W), (0, 0)))
    w["w_dw_f_g"], w["w_dw_f_v"] = dwf[:, :, :D_FF], dwf[:, :, D_FF:]
    w["b_dw_f_g"], w["b_dw_f_v"] = rep["b_dw_f"][:, :D_FF], rep["b_dw_f"][:, D_FF:]
    w["w_down"] = full["w_down"]
    return w


_ANY = pl.BlockSpec(memory_space=pl.ANY)


def _place():
    x, y, c = lax.axis_index("x"), lax.axis_index("y"), lax.axis_index("c")
    chips = [(1 - x, y), (x, 1 - y), (1 - x, 1 - y)]
    return x, y, c, chips


def _chip_index(chip):
    return 2 * chip[0] + chip[1]


def _gather_weights(w_half, name):
    def body(w_ref, out_ref, send_sems, recv_sems, local_sem):
        x, y, c, chips = _place()
        me = 2 * x + y
        sibling = (x, y, 1 - c)

        def copy(sem, chip_idx, half, to, src=None):
            dst = out_ref.at[chip_idx, half]
            return pltpu.make_async_remote_copy(src_ref=dst if src is None else src, dst_ref=dst, send_sem=send_sems.at[sem],
                                                recv_sem=recv_sems.at[sem], device_id=to, device_id_type=MESH_ID)

        mine = pltpu.make_async_copy(w_ref, out_ref.at[me], local_sem)
        mine.start()
        first = [copy(j, me, c, (*chip, c), src=w_ref.at[c]) for j, chip in enumerate(chips)]
        for cp in first:
            cp.start()
        passed = [copy(3 + j, _chip_index(chip), c, sibling) for j, chip in enumerate(chips)]
        for j, chip in enumerate(chips):
            copy(j, _chip_index(chip), c, (x, y, c)).wait_recv()
            passed[j].start()
        for j, chip in enumerate(chips):
            copy(3 + j, _chip_index(chip), 1 - c, (x, y, c)).wait_recv()
        for cp in first + passed:
            cp.wait_send()
        mine.wait()

    return pl.pallas_call(
        body, out_shape=jax.ShapeDtypeStruct((N_CHIPS,) + w_half.shape, w_half.dtype), in_specs=[_ANY], out_specs=_ANY,
        scratch_shapes=[pltpu.SemaphoreType.DMA((6,)), pltpu.SemaphoreType.DMA((6,)), pltpu.SemaphoreType.DMA],
        name=name,
    )(w_half)


def _swap_halves(g_half):
    def body(g_ref, got_ref, send_sem, recv_sem):
        x, y, c, _ = _place()
        cp = pltpu.make_async_remote_copy(src_ref=g_ref.at[1 - c], dst_ref=got_ref, send_sem=send_sem, recv_sem=recv_sem,
                                          device_id=(x, y, 1 - c), device_id_type=MESH_ID)
        cp.start()
        cp.wait()

    return pl.pallas_call(
        body, out_shape=jax.ShapeDtypeStruct(g_half.shape[1:], g_half.dtype), in_specs=[_ANY], out_specs=_ANY,
        scratch_shapes=[pltpu.SemaphoreType.DMA, pltpu.SemaphoreType.DMA], name="swap_halves",
    )(g_half)


def _scatter_to_chips(part):
    def body(p_ref, got_ref, send_sems, recv_sems, local_sem):
        x, y, c, chips = _place()
        me = 2 * x + y
        mine = pltpu.make_async_copy(p_ref.at[me], got_ref.at[me], local_sem)
        mine.start()
        sends = [pltpu.make_async_remote_copy(src_ref=p_ref.at[_chip_index(chip)], dst_ref=got_ref.at[me], send_sem=send_sems.at[j],
                                              recv_sem=recv_sems.at[j], device_id=(*chip, c), device_id_type=MESH_ID)
                 for j, chip in enumerate(chips)]
        for cp in sends:
            cp.start()
        for j, chip in enumerate(chips):
            landed = got_ref.at[_chip_index(chip)]
            pltpu.make_async_remote_copy(src_ref=landed, dst_ref=landed, send_sem=send_sems.at[j], recv_sem=recv_sems.at[j],
                                         device_id=(x, y, c), device_id_type=MESH_ID).wait_recv()
        for cp in sends:
            cp.wait_send()
        mine.wait()

    return pl.pallas_call(
        body, out_shape=jax.ShapeDtypeStruct(part.shape, part.dtype), in_specs=[_ANY], out_specs=_ANY,
        scratch_shapes=[pltpu.SemaphoreType.DMA((3,)), pltpu.SemaphoreType.DMA((3,)), pltpu.SemaphoreType.DMA],
        name="scatter_to_chips",
    )(part)


def _join_halves(half):
    def body(h_ref, out_ref, send_sem, recv_sem, local_sem):
        x, y, c, _ = _place()
        mine = pltpu.make_async_copy(h_ref, out_ref.at[c], local_sem)
        mine.start()
        cp = pltpu.make_async_remote_copy(src_ref=h_ref, dst_ref=out_ref.at[c], send_sem=send_sem, recv_sem=recv_sem,
                                          device_id=(x, y, 1 - c), device_id_type=MESH_ID)
        cp.start()
        landed = out_ref.at[1 - c]
        pltpu.make_async_remote_copy(src_ref=landed, dst_ref=landed, send_sem=send_sem, recv_sem=recv_sem,
                                     device_id=(x, y, c), device_id_type=MESH_ID).wait_recv()
        cp.wait_send()
        mine.wait()

    return pl.pallas_call(
        body, out_shape=jax.ShapeDtypeStruct((2,) + half.shape, half.dtype), in_specs=[_ANY], out_specs=_ANY,
        scratch_shapes=[pltpu.SemaphoreType.DMA, pltpu.SemaphoreType.DMA, pltpu.SemaphoreType.DMA], name="join_halves",
    )(half)


def _gather_replicated(rows):
    m_per, n = rows.shape

    def body(x_ref, out_ref, send_sems, recv_sems, local_sem):
        x, y, c, chips = _place()
        me, sibling = (x, y, c), (x, y, 1 - c)

        def block(px, py, pc):
            return out_ref.at[pl.ds((4 * px + 2 * py + pc) * m_per, m_per), :]

        def copy(k, blk, to, src=None):
            return pltpu.make_async_remote_copy(src_ref=block(*blk) if src is None else src, dst_ref=block(*blk),
                                                send_sem=send_sems.at[k], recv_sem=recv_sems.at[k], device_id=to, device_id_type=MESH_ID)

        mine = pltpu.make_async_copy(x_ref, block(*me), local_sem)
        mine.start()
        first = [copy(0, me, sibling, src=x_ref)]
        first += [copy(1 + j, me, (*chip, c), src=x_ref) for j, chip in enumerate(chips)]
        for cp in first:
            cp.start()
        passed = [copy(4 + j, (*chip, c), sibling) for j, chip in enumerate(chips)]
        for j, chip in enumerate(chips):
            copy(1 + j, (*chip, c), me).wait_recv()
            passed[j].start()
        copy(0, sibling, me).wait_recv()
        for j, chip in enumerate(chips):
            copy(4 + j, (*chip, 1 - c), me).wait_recv()
        for cp in first + passed:
            cp.wait_send()
        mine.wait()

    vmem = pl.BlockSpec(memory_space=pltpu.VMEM)
    return pl.pallas_call(
        body, out_shape=jax.ShapeDtypeStruct((N_DEV * m_per, n), rows.dtype), in_specs=[vmem], out_specs=vmem,
        scratch_shapes=[pltpu.SemaphoreType.DMA((7,)), pltpu.SemaphoreType.DMA((7,)), pltpu.SemaphoreType.DMA],
        name="gather_replicated",
    )(rows)


_SUM_ROWS = 256


def _tile_rows(rows):
    if rows <= _SUM_ROWS:
        return rows
    best = None
    for r in range(16, _SUM_ROWS + 1, 16):
        if rows % r == 0:
            best = r
    assert best is not None, rows
    return best


def _add_pairs(mine, got):
    shape = mine.shape
    tr = _tile_rows(shape[1])
    spec = pl.BlockSpec((1, tr, shape[2]), lambda t, i: (t, i, 0))

    def body(a_ref, b_ref, o_ref):
        o_ref[...] = (a_ref[...].astype(F32) + b_ref[...].astype(F32)).astype(BF16)

    return pl.pallas_call(
        body, out_shape=jax.ShapeDtypeStruct(shape, BF16), grid=(shape[0], shape[1] // tr),
        in_specs=[spec, spec], out_specs=spec, compiler_params=_params(("parallel", "parallel")), name="add_pairs",
    )(mine, got)


def _sum_slabs(slabs):
    n, rows, cols = slabs.shape
    tr = _tile_rows(rows)

    def body(s_ref, o_ref):
        acc = s_ref[0].astype(F32)
        for t in range(1, n):
            acc = acc + s_ref[t].astype(F32)
        o_ref[...] = acc

    return pl.pallas_call(
        body, out_shape=jax.ShapeDtypeStruct((rows, cols), F32), grid=(rows // tr,),
        in_specs=[pl.BlockSpec((n, tr, cols), lambda i: (0, i, 0))], out_specs=pl.BlockSpec((tr, cols), lambda i: (i, 0)),
        compiler_params=_params(("parallel",)), name=f"sum_slabs_{n}x{rows}x{cols}",
    )(slabs)


_ADAM_BLOCK_BYTES = 2 * 1024 * 1024


def _adamw(w, g, m, v):
    shape = w.shape
    cols = shape[-1]
    rows = _size(shape) // cols
    tr = rows
    if rows * cols * 4 > _ADAM_BLOCK_BYTES:
        for r in range(8, rows, 8):
            if rows % r == 0 and r * cols * 4 <= _ADAM_BLOCK_BYTES:
                tr = r

    def body(w_ref, g_ref, m_ref, v_ref, d_ref, nm_ref, nv_ref):
        gv = g_ref[...]
        nm = ADAM_B1 * m_ref[...] + (1.0 - ADAM_B1) * gv
        nv = ADAM_B2 * v_ref[...] + (1.0 - ADAM_B2) * jnp.square(gv)
        m_hat = nm / (1.0 - ADAM_B1 ** ADAM_STEP)
        v_hat = nv / (1.0 - ADAM_B2 ** ADAM_STEP)
        d_ref[...] = -ADAM_LR * (m_hat / (jnp.sqrt(v_hat) + ADAM_EPS) + ADAM_WD * w_ref[...])
        nm_ref[...] = nm
        nv_ref[...] = nv

    spec = pl.BlockSpec((tr, cols), lambda i: (i, 0))
    out = pl.pallas_call(
        body, out_shape=tuple(jax.ShapeDtypeStruct((rows, cols), F32) for _ in range(3)), grid=(rows // tr,),
        in_specs=[spec] * 4, out_specs=(spec,) * 3, compiler_params=_params(("parallel",)), name=f"adamw_{rows}x{cols}",
    )(*(a.reshape(rows, cols) for a in (w, g, m, v)))
    return tuple(o.reshape(shape) for o in out)


def kernel(x, mem, g_mix, w_in_a, b_glu, w_dw_a, b_dw_a, ln_g, ln_b, g_kv, w_kvf, b_f, w_in_b, g_mem, w_mem_kv, w_out, g_ffn, w_up, w_dw_f, b_dw_f, w_down, g_final, loss_target, m_g_mix, m_w_in_a, m_b_glu, m_w_dw_a, m_b_dw_a, m_ln_g, m_ln_b, m_g_kv, m_w_kvf, m_b_f, m_w_in_b, m_g_mem, m_w_mem_kv, m_w_out, m_g_ffn, m_w_up, m_w_dw_f, m_b_dw_f, m_w_down, m_g_final, v_g_mix, v_w_in_a, v_b_glu, v_w_dw_a, v_b_dw_a, v_ln_g, v_ln_b, v_g_kv, v_w_kvf, v_b_f, v_w_in_b, v_g_mem, v_w_mem_kv, v_w_out, v_g_ffn, v_w_up, v_w_dw_f, v_b_dw_f, v_w_down, v_g_final):
    weights = dict(g_mix=g_mix, w_in_a=w_in_a, b_glu=b_glu, w_dw_a=w_dw_a, b_dw_a=b_dw_a, ln_g=ln_g, ln_b=ln_b, g_kv=g_kv,
                   w_kvf=w_kvf, b_f=b_f, w_in_b=w_in_b, g_mem=g_mem, w_mem_kv=w_mem_kv, w_out=w_out, g_ffn=g_ffn, w_up=w_up,
                   w_dw_f=w_dw_f, b_dw_f=b_dw_f, w_down=w_down, g_final=g_final)
    mom1 = dict(g_mix=m_g_mix, w_in_a=m_w_in_a, b_glu=m_b_glu, w_dw_a=m_w_dw_a, b_dw_a=m_b_dw_a, ln_g=m_ln_g, ln_b=m_ln_b,
                g_kv=m_g_kv, w_kvf=m_w_kvf, b_f=m_b_f, w_in_b=m_w_in_b, g_mem=m_g_mem, w_mem_kv=m_w_mem_kv, w_out=m_w_out,
                g_ffn=m_g_ffn, w_up=m_w_up, w_dw_f=m_w_dw_f, b_dw_f=m_b_dw_f, w_down=m_w_down, g_final=m_g_final)
    mom2 = dict(g_mix=v_g_mix, w_in_a=v_w_in_a, b_glu=v_b_glu, w_dw_a=v_w_dw_a, b_dw_a=v_b_dw_a, ln_g=v_ln_g, ln_b=v_ln_b,
                g_kv=v_g_kv, w_kvf=v_w_kvf, b_f=v_b_f, w_in_b=v_w_in_b, g_mem=v_g_mem, w_mem_kv=v_w_mem_kv, w_out=v_w_out,
                g_ffn=v_g_ffn, w_up=v_w_up, w_dw_f=v_w_dw_f, b_dw_f=v_b_dw_f, w_down=v_w_down, g_final=v_g_final)

    full = _gathered_to_full(_gather_weights(_flatten_weights(weights, BF16_LEAVES, BF16), "gather_matmul_weights"), BF16_LEAVES)
    full.update(_gathered_to_full(_gather_weights(_flatten_weights(weights, F32_LEAVES, F32), "gather_vector_weights"), F32_LEAVES))
    pieces = _split_weights(full, {n: weights[n] for n in REPLICATED})

    loss, dx, grads = _local_step(x[0], mem[0], loss_target[0], pieces)
    loss = lax.psum(loss, ("x", "y", "c"))

    scatter = _full_to_scatter(grads).reshape(N_CHIPS, 2, G_ROWS // 2, FLAT_COLS).transpose(1, 0, 2, 3)
    own_half = lax.dynamic_index_in_dim(scatter, lax.axis_index("c"), axis=0, keepdims=False)
    chip_sum = _add_pairs(own_half, _swap_halves(scatter))
    reduced_half = _sum_slabs(_scatter_to_chips(chip_sum))
    reduced = _join_halves(reduced_half).reshape(G_ROWS, FLAT_COLS)
    grad_leaves = _unflatten_grads(reduced)

    rep_flat = jnp.concatenate([grads[n].reshape(-1) for n in REPLICATED])
    rep_flat = jnp.pad(rep_flat, (0, REP_ROWS * REP_COLS - REP_ELEMS)).reshape(REP_ROWS, REP_COLS)
    rep_sum = _sum_slabs(_gather_replicated(rep_flat).reshape(N_DEV, REP_ROWS, REP_COLS)).reshape(-1)
    off = 0
    for n, shp in REPLICATED.items():
        grad_leaves[n] = rep_sum[off:off + _size(shp)].reshape(shp)
        off += _size(shp)

    deltas, new_m, new_v = {}, {}, {}
    for n in WEIGHT_ORDER:
        deltas[n], new_m[n], new_v[n] = _adamw(weights[n], grad_leaves[n], mom1[n], mom2[n])
    return (loss, dx[None], *[grad_leaves[n] for n in WEIGHT_ORDER], *[deltas[n] for n in WEIGHT_ORDER],
            *[new_m[n] for n in WEIGHT_ORDER], *[new_v[n] for n in WEIGHT_ORDER])
```

```python
import jax
import jax.numpy as jnp
from jax import lax
from jax.experimental import pallas as pl
from jax.experimental.pallas import tpu as pltpu

F32 = jnp.float32
BF16 = jnp.bfloat16

D_MODEL = 1024
N_LAYERS = 4
N_A = 2
CONV_CH = 768
MEM_W = 256
HEAD_DIM = 64
N_MEM_HEADS = 4
N_FOX_HEADS = 12
N_HEAD_PAIRS = N_FOX_HEADS // 2
D_FF = 2816
CONV_W = 31
CONV_PAD = 32
FFN_CONV_W = 3
FFN_PAD = 8
F_PAD = 128
RMS_EPS = 1e-6
LN_EPS = 1e-5
ATT_SCALE = HEAD_DIM ** -0.5
NEG_BIG = -1e30

ADAM_LR = 0.001
ADAM_B1 = 0.9
ADAM_B2 = 0.999
ADAM_EPS = 1e-08
ADAM_WD = 0.01
ADAM_STEP = 10

LANE = 128
ROW_TILE = 256
CHUNK = 128
VMEM_LIMIT = 48 * 1024 * 1024
FLAT_COLS = 1024
N_CHIPS = 4
N_DEV = 8
MESH_ID = pl.DeviceIdType.MESH


def _params(sem=None):
    return pltpu.CompilerParams(dimension_semantics=sem, vmem_limit_bytes=VMEM_LIMIT)


def _tile(dim, pref):
    if dim <= pref:
        return dim
    best = None
    for m in range(1, dim // LANE + 1):
        d = m * LANE
        if dim % d == 0 and d <= pref:
            best = d
    assert best is not None, dim
    return best


_DIMS = {"nn": (((1,), (0,)), ((), ())), "nt": (((1,), (1,)), ((), ())), "tn": (((0,), (0,)), ((), ()))}


def _mm(a, b, mode="nn", out_dtype=F32, add=None):
    if mode == "nn":
        (m, k), (k2, n) = a.shape, b.shape
    elif mode == "nt":
        (m, k), (n, k2) = a.shape, b.shape
    else:
        (k, m), (k2, n) = a.shape, b.shape
    assert k == k2, (a.shape, b.shape, mode)
    tm, tn = _tile(m, 512), _tile(n, 1408)
    dims = _DIMS[mode]
    has_add = add is not None

    def body(*refs):
        if has_add:
            a_ref, b_ref, add_ref, o_ref = refs
        else:
            a_ref, b_ref, o_ref = refs
        r = lax.dot_general(a_ref[...].astype(BF16), b_ref[...].astype(BF16), dims, preferred_element_type=F32)
        if has_add:
            r = r + add_ref[...]
        o_ref[...] = r.astype(out_dtype)

    a_spec = pl.BlockSpec((k, tm), lambda i, j: (0, i)) if mode == "tn" else pl.BlockSpec((tm, k), lambda i, j: (i, 0))
    b_spec = pl.BlockSpec((tn, k), lambda i, j: (j, 0)) if mode == "nt" else pl.BlockSpec((k, tn), lambda i, j: (0, j))
    o_spec = pl.BlockSpec((tm, tn), lambda i, j: (i, j))
    in_specs = [a_spec, b_spec] + ([o_spec] if has_add else [])
    args = (a, b) + ((add,) if has_add else ())
    return pl.pallas_call(
        body,
        out_shape=jax.ShapeDtypeStruct((m, n), out_dtype),
        grid=(m // tm, n // tn),
        in_specs=in_specs,
        out_specs=o_spec,
        compiler_params=_params(("parallel", "parallel")),
        name=f"mm_{mode}_{m}x{k}x{n}",
    )(*args)


def _row(width):
    return pl.BlockSpec((ROW_TILE, width), lambda i: (i, 0))


def _vec(width):
    return pl.BlockSpec((1, width), lambda i: (0, 0))


def _rms_fwd(x, g):
    rows, d = x.shape

    def body(x_ref, g_ref, o_ref):
        xv = x_ref[...]
        rstd = lax.rsqrt(jnp.mean(xv * xv, axis=-1, keepdims=True) + RMS_EPS)
        o_ref[...] = (xv * rstd * g_ref[...]).astype(BF16)

    return pl.pallas_call(
        body, out_shape=jax.ShapeDtypeStruct((rows, d), BF16), grid=(rows // ROW_TILE,),
        in_specs=[_row(d), _vec(d)], out_specs=_row(d), compiler_params=_params(("parallel",)), name=f"rms_fwd_{rows}",
    )(x, g)


def _accumulate(ref, value, step):
    @pl.when(step == 0)
    def _():
        ref[...] = value

    @pl.when(step > 0)
    def _():
        ref[...] += value


def _rms_bwd(x, g, dh, dres):
    rows, d = x.shape
    has_res = dres is not None

    def body(*refs):
        if has_res:
            x_ref, g_ref, dh_ref, dres_ref, dx_ref, dxb_ref, dg_ref = refs
        else:
            x_ref, g_ref, dh_ref, dx_ref, dxb_ref, dg_ref = refs
        xv = x_ref[...]
        dhv = dh_ref[...]
        rstd = lax.rsqrt(jnp.mean(xv * xv, axis=-1, keepdims=True) + RMS_EPS)
        xhat = xv * rstd
        gd = dhv * g_ref[...]
        dx = rstd * (gd - xhat * jnp.mean(gd * xhat, axis=-1, keepdims=True))
        if has_res:
            dx = dx + dres_ref[...]
        dx_ref[...] = dx
        dxb_ref[...] = dx.astype(BF16)
        _accumulate(dg_ref, jnp.sum(dhv * xhat, axis=0, keepdims=True), pl.program_id(0))

    in_specs = [_row(d), _vec(d), _row(d)] + ([_row(d)] if has_res else [])
    args = (x, g, dh) + ((dres,) if has_res else ())
    return pl.pallas_call(
        body,
        out_shape=(jax.ShapeDtypeStruct((rows, d), F32), jax.ShapeDtypeStruct((rows, d), BF16), jax.ShapeDtypeStruct((1, d), F32)),
        grid=(rows // ROW_TILE,), in_specs=in_specs, out_specs=(_row(d), _row(d), _vec(d)),
        compiler_params=_params(("arbitrary",)), name=f"rms_bwd_{rows}_{int(has_res)}",
    )(*args)


def _loss_head(x, g, target):
    rows, d = x.shape

    def body(x_ref, g_ref, t_ref, loss_ref, dx_ref, dxb_ref, dg_ref):
        xv = x_ref[...]
        gv = g_ref[...]
        rstd = lax.rsqrt(jnp.mean(xv * xv, axis=-1, keepdims=True) + RMS_EPS)
        xhat = xv * rstd
        err = xhat * gv - t_ref[...]
        part = 0.5 * jnp.sum(jnp.mean(err * err, axis=-1, keepdims=True), axis=0, keepdims=True)
        dy = err * (1.0 / d)
        gd = dy * gv
        dx = rstd * (gd - xhat * jnp.mean(gd * xhat, axis=-1, keepdims=True))
        dx_ref[...] = dx
        dxb_ref[...] = dx.astype(BF16)
        step = pl.program_id(0)
        _accumulate(loss_ref, jnp.broadcast_to(part, (1, LANE)), step)
        _accumulate(dg_ref, jnp.sum(dy * xhat, axis=0, keepdims=True), step)

    return pl.pallas_call(
        body,
        out_shape=(jax.ShapeDtypeStruct((1, LANE), F32), jax.ShapeDtypeStruct((rows, d), F32),
                   jax.ShapeDtypeStruct((rows, d), BF16), jax.ShapeDtypeStruct((1, d), F32)),
        grid=(rows // ROW_TILE,), in_specs=[_row(d), _vec(d), _row(d)],
        out_specs=(_vec(LANE), _row(d), _row(d), _vec(d)),
        compiler_params=_params(("arbitrary",)), name="loss_head",
    )(x, g, target)


def _sigmoid(x):
    return 1.0 / (1.0 + jnp.exp(-x))


def _ln_silu_fwd(c, ln_g, ln_b):
    rows, ch = c.shape

    def body(c_ref, g_ref, b_ref, o_ref):
        cv = c_ref[...]
        mu = jnp.mean(cv, axis=-1, keepdims=True)
        cen = cv - mu
        rstd = lax.rsqrt(jnp.mean(cen * cen, axis=-1, keepdims=True) + LN_EPS)
        y = cen * rstd * g_ref[...] + b_ref[...]
        o_ref[...] = (y * _sigmoid(y)).astype(BF16)

    return pl.pallas_call(
        body, out_shape=jax.ShapeDtypeStruct((rows, ch), BF16), grid=(rows // ROW_TILE,),
        in_specs=[_row(ch), _vec(ch), _vec(ch)], out_specs=_row(ch), compiler_params=_params(("parallel",)), name="ln_silu_fwd",
    )(c, ln_g, ln_b)


def _ln_silu_bwd(dmix, c, ln_g, ln_b):
    rows, ch = c.shape

    def body(dm_ref, c_ref, g_ref, b_ref, dc_ref, dg_ref, db_ref):
        cv = c_ref[...]
        gv = g_ref[...]
        mu = jnp.mean(cv, axis=-1, keepdims=True)
        cen = cv - mu
        rstd = lax.rsqrt(jnp.mean(cen * cen, axis=-1, keepdims=True) + LN_EPS)
        xhat = cen * rstd
        y = xhat * gv + b_ref[...]
        sg = _sigmoid(y)
        dy = dm_ref[...] * (sg * (1.0 + y * (1.0 - sg)))
        dxh = dy * gv
        dc = rstd * (dxh - jnp.mean(dxh, axis=-1, keepdims=True) - xhat * jnp.mean(dxh * xhat, axis=-1, keepdims=True))
        dc_ref[...] = dc
        step = pl.program_id(0)
        _accumulate(dg_ref, jnp.sum(dy * xhat, axis=0, keepdims=True), step)
        _accumulate(db_ref, jnp.sum(dy, axis=0, keepdims=True), step)

    return pl.pallas_call(
        body,
        out_shape=(jax.ShapeDtypeStruct((rows, ch), F32), jax.ShapeDtypeStruct((1, ch), F32), jax.ShapeDtypeStruct((1, ch), F32)),
        grid=(rows // ROW_TILE,), in_specs=[_row(ch), _row(ch), _vec(ch), _vec(ch)],
        out_specs=(_row(ch), _vec(ch), _vec(ch)), compiler_params=_params(("arbitrary",)), name="ln_silu_bwd",
    )(dmix, c, ln_g, ln_b)


def _col(rows, cb):
    return pl.BlockSpec((rows, cb), lambda j: (0, j))


def _conv_glu_fwd(pa, pg, ba, bg, w_dw, b_dw):
    seq, ch = pa.shape
    cb = LANE
    n_chunks = seq // CHUNK

    def body(pa_ref, pg_ref, ba_ref, bg_ref, w_ref, b_ref, v_ref, c_ref, vpad_ref):
        vpad_ref[0:CONV_PAD, :] = jnp.zeros((CONV_PAD, cb), F32)

        def glu(r, carry):
            r0 = pl.multiple_of(r * CHUNK, CHUNK)
            rows = pl.ds(r0, CHUNK)
            v = (pa_ref[rows, :] + ba_ref[...]) * _sigmoid(pg_ref[rows, :] + bg_ref[...])
            v_ref[rows, :] = v
            vpad_ref[pl.ds(r0 + CONV_PAD, CHUNK), :] = v
            return carry

        lax.fori_loop(0, n_chunks, glu, 0)

        def conv(r, carry):
            r0 = pl.multiple_of(r * CHUNK, CHUNK)
            win = vpad_ref[pl.ds(r0, CHUNK + CONV_PAD), :]
            acc = jnp.broadcast_to(b_ref[...], (CHUNK, cb))
            for t in range(CONV_W):
                off = CONV_PAD - (CONV_W - 1) + t
                acc = acc + w_ref[t:t + 1, :] * win[off:off + CHUNK, :]
            c_ref[pl.ds(r0, CHUNK), :] = acc
            return carry

        lax.fori_loop(0, n_chunks, conv, 0)

    return pl.pallas_call(
        body,
        out_shape=(jax.ShapeDtypeStruct((seq, ch), F32), jax.ShapeDtypeStruct((seq, ch), F32)),
        grid=(ch // cb,),
        in_specs=[_col(seq, cb), _col(seq, cb), _col(1, cb), _col(1, cb), _col(CONV_PAD, cb), _col(1, cb)],
        out_specs=(_col(seq, cb), _col(seq, cb)),
        scratch_shapes=[pltpu.VMEM((seq + CONV_PAD, cb), F32)],
        compiler_params=_params(("parallel",)), name="conv_glu_fwd",
    )(pa, pg, ba, bg, w_dw, b_dw)


def _conv_glu_bwd(dc, v, pa, pg, ba, bg, w_dw):
    seq, ch = dc.shape
    cb = LANE
    n_chunks = seq // CHUNK

    def body(dc_ref, v_ref, pa_ref, pg_ref, ba_ref, bg_ref, w_ref, da_ref, dg_ref, dw_ref, dbdw_ref, dba_ref, dbg_ref,
             dcpad_ref, vpad_ref):
        vpad_ref[0:CONV_PAD, :] = jnp.zeros((CONV_PAD, cb), F32)
        dcpad_ref[seq:seq + CONV_PAD, :] = jnp.zeros((CONV_PAD, cb), F32)
        dw_ref[...] = jnp.zeros((CONV_PAD, cb), F32)
        dbdw_ref[...] = jnp.zeros((1, cb), F32)
        dba_ref[...] = jnp.zeros((1, cb), F32)
        dbg_ref[...] = jnp.zeros((1, cb), F32)

        def fill(r, carry):
            r0 = pl.multiple_of(r * CHUNK, CHUNK)
            vpad_ref[pl.ds(r0 + CONV_PAD, CHUNK), :] = v_ref[pl.ds(r0, CHUNK), :]
            dcpad_ref[pl.ds(r0, CHUNK), :] = dc_ref[pl.ds(r0, CHUNK), :]
            return carry

        lax.fori_loop(0, n_chunks, fill, 0)

        def step(r, carry):
            r0 = pl.multiple_of(r * CHUNK, CHUNK)
            rows = pl.ds(r0, CHUNK)
            dwin = dcpad_ref[pl.ds(r0, CHUNK + CONV_PAD), :]
            vwin = vpad_ref[pl.ds(r0, CHUNK + CONV_PAD), :]
            dcur = dwin[0:CHUNK, :]
            dv = jnp.zeros((CHUNK, cb), F32)
            for t in range(CONV_W):
                fwd_off = CONV_W - 1 - t
                dv = dv + w_ref[t:t + 1, :] * dwin[fwd_off:fwd_off + CHUNK, :]
                off = CONV_PAD - (CONV_W - 1) + t
                dw_ref[t:t + 1, :] += jnp.sum(dcur * vwin[off:off + CHUNK, :], axis=0, keepdims=True)
            a = pa_ref[rows, :] + ba_ref[...]
            sg = _sigmoid(pg_ref[rows, :] + bg_ref[...])
            da = dv * sg
            dgate = dv * a * sg * (1.0 - sg)
            da_ref[rows, :] = da.astype(BF16)
            dg_ref[rows, :] = dgate.astype(BF16)
            dbdw_ref[...] += jnp.sum(dcur, axis=0, keepdims=True)
            dba_ref[...] += jnp.sum(da, axis=0, keepdims=True)
            dbg_ref[...] += jnp.sum(dgate, axis=0, keepdims=True)
            return carry

        lax.fori_loop(0, n_chunks, step, 0)

    return pl.pallas_call(
        body,
        out_shape=(jax.ShapeDtypeStruct((seq, ch), BF16), jax.ShapeDtypeStruct((seq, ch), BF16),
                   jax.ShapeDtypeStruct((CONV_PAD, ch), F32), jax.ShapeDtypeStruct((1, ch), F32),
                   jax.ShapeDtypeStruct((1, ch), F32), jax.ShapeDtypeStruct((1, ch), F32)),
        grid=(ch // cb,),
        in_specs=[_col(seq, cb), _col(seq, cb), _col(seq, cb), _col(seq, cb), _col(1, cb), _col(1, cb), _col(CONV_PAD, cb)],
        out_specs=(_col(seq, cb), _col(seq, cb), _col(CONV_PAD, cb), _col(1, cb), _col(1, cb), _col(1, cb)),
        scratch_shapes=[pltpu.VMEM((seq + CONV_PAD, cb), F32), pltpu.VMEM((seq + CONV_PAD, cb), F32)],
        compiler_params=_params(("parallel",)), name="conv_glu_bwd",
    )(dc, v, pa, pg, ba, bg, w_dw)


def _ffn_conv(pad_ref, r0, w_ref, b_ref, cb):
    win = pad_ref[pl.ds(r0, CHUNK + FFN_PAD), :]
    y = jnp.broadcast_to(b_ref[...], (CHUNK, cb))
    for t in range(FFN_CONV_W):
        off = FFN_PAD - (FFN_CONV_W - 1) + t
        y = y + w_ref[t:t + 1, :] * win[off:off + CHUNK, :]
    return y, win


def _ffn_mid_fwd(ug, uv, wg, wv, bg, bv):
    seq, ch = ug.shape
    cb = _tile(ch, 256)
    n_chunks = seq // CHUNK

    def body(ug_ref, uv_ref, wg_ref, wv_ref, bg_ref, bv_ref, z_ref, gpad_ref, vpad_ref):
        gpad_ref[0:FFN_PAD, :] = jnp.zeros((FFN_PAD, cb), F32)
        vpad_ref[0:FFN_PAD, :] = jnp.zeros((FFN_PAD, cb), F32)

        def fill(r, carry):
            r0 = pl.multiple_of(r * CHUNK, CHUNK)
            gpad_ref[pl.ds(r0 + FFN_PAD, CHUNK), :] = ug_ref[pl.ds(r0, CHUNK), :]
            vpad_ref[pl.ds(r0 + FFN_PAD, CHUNK), :] = uv_ref[pl.ds(r0, CHUNK), :]
            return carry

        lax.fori_loop(0, n_chunks, fill, 0)

        def step(r, carry):
            r0 = pl.multiple_of(r * CHUNK, CHUNK)
            yg, _ = _ffn_conv(gpad_ref, r0, wg_ref, bg_ref, cb)
            yv, _ = _ffn_conv(vpad_ref, r0, wv_ref, bv_ref, cb)
            z_ref[pl.ds(r0, CHUNK), :] = (yg * _sigmoid(yg) * yv).astype(BF16)
            return carry

        lax.fori_loop(0, n_chunks, step, 0)

    return pl.pallas_call(
        body, out_shape=jax.ShapeDtypeStruct((seq, ch), BF16), grid=(ch // cb,),
        in_specs=[_col(seq, cb), _col(seq, cb), _col(FFN_PAD, cb), _col(FFN_PAD, cb), _col(1, cb), _col(1, cb)],
        out_specs=_col(seq, cb),
        scratch_shapes=[pltpu.VMEM((seq + FFN_PAD, cb), F32), pltpu.VMEM((seq + FFN_PAD, cb), F32)],
        compiler_params=_params(("parallel",)), name="ffn_mid_fwd",
    )(ug, uv, wg, wv, bg, bv)


def _ffn_mid_bwd(ug, uv, dz, wg, wv, bg, bv):
    seq, ch = ug.shape
    cb = _tile(ch, 256)
    n_chunks = seq // CHUNK

    def body(ug_ref, uv_ref, dz_ref, wg_ref, wv_ref, bg_ref, bv_ref, dug_ref, duv_ref, dwg_ref, dwv_ref, dbg_ref, dbv_ref,
             gpad_ref, vpad_ref, dyg_ref, dyv_ref):
        gpad_ref[0:FFN_PAD, :] = jnp.zeros((FFN_PAD, cb), F32)
        vpad_ref[0:FFN_PAD, :] = jnp.zeros((FFN_PAD, cb), F32)
        dyg_ref[seq:seq + FFN_PAD, :] = jnp.zeros((FFN_PAD, cb), F32)
        dyv_ref[seq:seq + FFN_PAD, :] = jnp.zeros((FFN_PAD, cb), F32)
        dwg_ref[...] = jnp.zeros((FFN_PAD, cb), F32)
        dwv_ref[...] = jnp.zeros((FFN_PAD, cb), F32)
        dbg_ref[...] = jnp.zeros((1, cb), F32)
        dbv_ref[...] = jnp.zeros((1, cb), F32)

        def fill(r, carry):
            r0 = pl.multiple_of(r * CHUNK, CHUNK)
            gpad_ref[pl.ds(r0 + FFN_PAD, CHUNK), :] = ug_ref[pl.ds(r0, CHUNK), :]
            vpad_ref[pl.ds(r0 + FFN_PAD, CHUNK), :] = uv_ref[pl.ds(r0, CHUNK), :]
            return carry

        lax.fori_loop(0, n_chunks, fill, 0)

        def grads_of_conv_out(r, carry):
            r0 = pl.multiple_of(r * CHUNK, CHUNK)
            rows = pl.ds(r0, CHUNK)
            yg, gwin = _ffn_conv(gpad_ref, r0, wg_ref, bg_ref, cb)
            yv, vwin = _ffn_conv(vpad_ref, r0, wv_ref, bv_ref, cb)
            dzv = dz_ref[rows, :]
            sg = _sigmoid(yg)
            dyg = dzv * yv * (sg * (1.0 + yg * (1.0 - sg)))
            dyv = dzv * yg * sg
            dyg_ref[rows, :] = dyg
            dyv_ref[rows, :] = dyv
            for t in range(FFN_CONV_W):
                off = FFN_PAD - (FFN_CONV_W - 1) + t
                dwg_ref[t:t + 1, :] += jnp.sum(dyg * gwin[off:off + CHUNK, :], axis=0, keepdims=True)
                dwv_ref[t:t + 1, :] += jnp.sum(dyv * vwin[off:off + CHUNK, :], axis=0, keepdims=True)
            dbg_ref[...] += jnp.sum(dyg, axis=0, keepdims=True)
            dbv_ref[...] += jnp.sum(dyv, axis=0, keepdims=True)
            return carry

        lax.fori_loop(0, n_chunks, grads_of_conv_out, 0)

        def grads_of_conv_in(r, carry):
            r0 = pl.multiple_of(r * CHUNK, CHUNK)
            gwin = dyg_ref[pl.ds(r0, CHUNK + FFN_PAD), :]
            vwin = dyv_ref[pl.ds(r0, CHUNK + FFN_PAD), :]
            dug = jnp.zeros((CHUNK, cb), F32)
            duv = jnp.zeros((CHUNK, cb), F32)
            for t in range(FFN_CONV_W):
                off = FFN_CONV_W - 1 - t
                dug = dug + wg_ref[t:t + 1, :] * gwin[off:off + CHUNK, :]
                duv = duv + wv_ref[t:t + 1, :] * vwin[off:off + CHUNK, :]
            dug_ref[pl.ds(r0, CHUNK), :] = dug.astype(BF16)
            duv_ref[pl.ds(r0, CHUNK), :] = duv.astype(BF16)
            return carry

        lax.fori_loop(0, n_chunks, grads_of_conv_in, 0)

    return pl.pallas_call(
        body,
        out_shape=(jax.ShapeDtypeStruct((seq, ch), BF16), jax.ShapeDtypeStruct((seq, ch), BF16),
                   jax.ShapeDtypeStruct((FFN_PAD, ch), F32), jax.ShapeDtypeStruct((FFN_PAD, ch), F32),
                   jax.ShapeDtypeStruct((1, ch), F32), jax.ShapeDtypeStruct((1, ch), F32)),
        grid=(ch // cb,),
        in_specs=[_col(seq, cb), _col(seq, cb), _col(seq, cb), _col(FFN_PAD, cb), _col(FFN_PAD, cb), _col(1, cb), _col(1, cb)],
        out_specs=(_col(seq, cb), _col(seq, cb), _col(FFN_PAD, cb), _col(FFN_PAD, cb), _col(1, cb), _col(1, cb)),
        scratch_shapes=[pltpu.VMEM((seq + FFN_PAD, cb), F32) for _ in range(4)],
        compiler_params=_params(("parallel",)), name="ffn_mid_bwd",
    )(ug, uv, dz, wg, wv, bg, bv)


def _dot(a, b, mode):
    return lax.dot_general(a.astype(BF16), b.astype(BF16), _DIMS[mode], preferred_element_type=F32)


def _head(h):
    return slice(h * HEAD_DIM, (h + 1) * HEAD_DIM)


def _mem_softmax(q, k):
    s = _dot(q, k, "nt") * ATT_SCALE
    e = jnp.exp(s - jnp.max(s, axis=-1, keepdims=True))
    return e / jnp.sum(e, axis=-1, keepdims=True)


def _memattn_fwd(pq, mk, mv):
    seq, w = pq.shape
    m = mk.shape[0]

    def body(q_ref, k_ref, v_ref, o_ref):
        for h in range(N_MEM_HEADS):
            p = _mem_softmax(q_ref[:, _head(h)], k_ref[:, _head(h)])
            o_ref[:, _head(h)] = _dot(p, v_ref[:, _head(h)], "nn").astype(BF16)

    full = pl.BlockSpec((m, w), lambda i: (0, 0))
    return pl.pallas_call(
        body, out_shape=jax.ShapeDtypeStruct((seq, w), BF16), grid=(seq // ROW_TILE,),
        in_specs=[_row(w), full, full], out_specs=_row(w), compiler_params=_params(("parallel",)), name="memattn_fwd",
    )(pq, mk, mv)


def _memattn_bwd(pq, mk, mv, dmo):
    seq, w = pq.shape
    m = mk.shape[0]

    def body(q_ref, k_ref, v_ref, do_ref, dq_ref, dk_ref, dv_ref):
        step = pl.program_id(0)

        @pl.when(step == 0)
        def _():
            dk_ref[...] = jnp.zeros((m, w), F32)
            dv_ref[...] = jnp.zeros((m, w), F32)

        for h in range(N_MEM_HEADS):
            q, k, v, do = q_ref[:, _head(h)], k_ref[:, _head(h)], v_ref[:, _head(h)], do_ref[:, _head(h)]
            p = _mem_softmax(q, k)
            dp = _dot(do, v, "nt")
            ds = p * (dp - jnp.sum(dp * p, axis=-1, keepdims=True))
            dq_ref[:, _head(h)] = (_dot(ds, k, "nn") * ATT_SCALE).astype(BF16)
            dk_ref[:, _head(h)] += _dot(ds, q, "tn") * ATT_SCALE
            dv_ref[:, _head(h)] += _dot(p, do, "tn")

    full = pl.BlockSpec((m, w), lambda i: (0, 0))
    return pl.pallas_call(
        body,
        out_shape=(jax.ShapeDtypeStruct((seq, w), BF16), jax.ShapeDtypeStruct((m, w), F32), jax.ShapeDtypeStruct((m, w), F32)),
        grid=(seq // ROW_TILE,), in_specs=[_row(w), full, full, _row(w)], out_specs=(_row(w), full, full),
        compiler_params=_params(("arbitrary",)), name="memattn_bwd",
    )(pq, mk, mv, dmo)


def _causal(s, q_blk, k_blk):
    qpos = q_blk * ROW_TILE + lax.broadcasted_iota(jnp.int32, (ROW_TILE, ROW_TILE), 0)
    kpos = k_blk * ROW_TILE + lax.broadcasted_iota(jnp.int32, (ROW_TILE, ROW_TILE), 1)
    return jnp.where(kpos <= qpos, s, NEG_BIG)


def _fox_fwd(q, k, v, cq, ck):
    seq, w = q.shape
    nblk = seq // ROW_TILE

    def body(q_ref, k_ref, v_ref, cq_ref, ck_ref, o_ref, ob_ref, lse_ref):
        i = pl.program_id(1)
        for h in range(2):
            qh = q_ref[:, _head(h)].astype(BF16)
            cqh = cq_ref[0, :, h * HEAD_DIM:h * HEAD_DIM + 1]

            def step(j, carry):
                m_run, l_run, acc = carry
                rows = pl.ds(pl.multiple_of(j * ROW_TILE, ROW_TILE), ROW_TILE)
                s = _dot(qh, k_ref[rows, _head(h)], "nt") * ATT_SCALE + cqh - ck_ref[0, j, h:h + 1, :]
                s = _causal(s, i, j)
                m_new = jnp.maximum(m_run, jnp.max(s, axis=-1, keepdims=True))
                alpha = jnp.exp(m_run - m_new)
                p = jnp.exp(s - m_new)
                l_new = alpha * l_run + jnp.sum(p, axis=-1, keepdims=True)
                acc = alpha * acc + _dot(p, v_ref[rows, _head(h)], "nn")
                return m_new, l_new, acc

            init = (jnp.full((ROW_TILE, 1), NEG_BIG, F32), jnp.zeros((ROW_TILE, 1), F32), jnp.zeros((ROW_TILE, HEAD_DIM), F32))
            m_run, l_run, acc = lax.fori_loop(0, i + 1, step, init)
            o = acc / l_run
            o_ref[:, _head(h)] = o
            ob_ref[:, _head(h)] = o.astype(BF16)
            lse_ref[0, :, _head(h)] = jnp.broadcast_to(m_run + jnp.log(l_run), (ROW_TILE, HEAD_DIM))

    blk = pl.BlockSpec((ROW_TILE, LANE), lambda hp, i: (i, hp))
    full = pl.BlockSpec((seq, LANE), lambda hp, i: (0, hp))
    cq_spec = pl.BlockSpec((1, ROW_TILE, LANE), lambda hp, i: (hp, i, 0))
    ck_spec = pl.BlockSpec((1, nblk, 8, ROW_TILE), lambda hp, i: (hp, 0, 0, 0))
    return pl.pallas_call(
        body,
        out_shape=(jax.ShapeDtypeStruct((seq, w), F32), jax.ShapeDtypeStruct((seq, w), BF16),
                   jax.ShapeDtypeStruct((N_HEAD_PAIRS, seq, LANE), F32)),
        grid=(N_HEAD_PAIRS, nblk), in_specs=[blk, full, full, cq_spec, ck_spec], out_specs=(blk, blk, cq_spec),
        compiler_params=_params(("parallel", "parallel")), name="fox_fwd",
    )(q, k, v, cq, ck)


def _fox_bwd(q, k, v, cq, ck, o, lse, do):
    seq, w = q.shape
    nblk = seq // ROW_TILE

    def body(q_ref, k_ref, v_ref, cq_ref, ck_ref, o_ref, lse_ref, do_ref, dq_ref, dk_ref, dv_ref, dcq_ref, dck_ref):
        kj = pl.program_id(1)

        @pl.when(kj == 0)
        def _():
            dq_ref[...] = jnp.zeros((seq, LANE), F32)
            dcq_ref[...] = jnp.zeros((1, seq, LANE), F32)

        dck_ref[...] = jnp.zeros((1, 1, 8, ROW_TILE), F32)
        for h in range(2):
            kh = k_ref[:, _head(h)].astype(BF16)
            vh = v_ref[:, _head(h)].astype(BF16)
            ckh = ck_ref[0, 0, h:h + 1, :]
            one = slice(h * HEAD_DIM, h * HEAD_DIM + 1)

            def step(i, carry):
                dk, dv, dc = carry
                rows = pl.ds(pl.multiple_of(i * ROW_TILE, ROW_TILE), ROW_TILE)
                qh = q_ref[rows, _head(h)].astype(BF16)
                doh = do_ref[rows, _head(h)]
                s = _dot(qh, kh, "nt") * ATT_SCALE + cq_ref[0, rows, one] - ckh
                p = jnp.exp(_causal(s, i, kj) - lse_ref[0, rows, one])
                dp = _dot(doh, vh, "nt")
                delta = jnp.sum(doh * o_ref[rows, _head(h)], axis=-1, keepdims=True)
                ds = p * (dp - delta)
                dq_ref[rows, _head(h)] += _dot(ds, kh, "nn") * ATT_SCALE
                dcq_ref[0, rows, one] += jnp.sum(ds, axis=-1, keepdims=True)
                dk = dk + _dot(ds, qh, "tn") * ATT_SCALE
                dv = dv + _dot(p, doh, "tn")
                dc = dc - jnp.sum(ds, axis=0, keepdims=True)
                return dk, dv, dc

            init = (jnp.zeros((ROW_TILE, HEAD_DIM), F32), jnp.zeros((ROW_TILE, HEAD_DIM), F32), jnp.zeros((1, ROW_TILE), F32))
            dk, dv, dc = lax.fori_loop(kj, nblk, step, init)
            dk_ref[:, _head(h)] = dk
            dv_ref[:, _head(h)] = dv
            dck_ref[0, 0, h:h + 1, :] = dc

    blk = pl.BlockSpec((ROW_TILE, LANE), lambda hp, j: (j, hp))
    full = pl.BlockSpec((seq, LANE), lambda hp, j: (0, hp))
    cq_spec = pl.BlockSpec((1, seq, LANE), lambda hp, j: (hp, 0, 0))
    ck_spec = pl.BlockSpec((1, 1, 8, ROW_TILE), lambda hp, j: (hp, j, 0, 0))
    return pl.pallas_call(
        body,
        out_shape=(jax.ShapeDtypeStruct((seq, w), F32), jax.ShapeDtypeStruct((seq, w), F32), jax.ShapeDtypeStruct((seq, w), F32),
                   jax.ShapeDtypeStruct((N_HEAD_PAIRS, seq, LANE), F32), jax.ShapeDtypeStruct((N_HEAD_PAIRS, nblk, 8, ROW_TILE), F32)),
        grid=(N_HEAD_PAIRS, nblk), in_specs=[full, blk, blk, cq_spec, ck_spec, full, cq_spec, full],
        out_specs=(full, blk, blk, cq_spec, ck_spec),
        compiler_params=_params(("parallel", "arbitrary")), name="fox_bwd",
    )(q, k, v, cq, ck, o, lse, do)


def _split3(x):
    hi = x.astype(BF16)
    r1 = x - hi.astype(F32)
    mid = r1.astype(BF16)
    lo = (r1 - mid.astype(F32)).astype(BF16)
    return hi, mid, lo


def _tri_sum(tri, x):
    hi, mid, lo = _split3(x)
    dot = lambda p: lax.dot_general(tri, p, _DIMS["nn"], preferred_element_type=F32)
    return (dot(lo) + dot(mid)) + dot(hi)


def _tri(lower):
    r = lax.broadcasted_iota(jnp.int32, (LANE, LANE), 0)
    c = lax.broadcasted_iota(jnp.int32, (LANE, LANE), 1)
    return jnp.where((c <= r) if lower else (c >= r), 1.0, 0.0).astype(BF16)


def _fgate_fwd(fl, bf):
    seq, w = fl.shape
    nb = seq // LANE

    def body(f_ref, b_ref, cum_ref):
        tri = _tri(True)

        def step(i, carry):
            rows = pl.ds(pl.multiple_of(i * LANE, LANE), LANE)
            x = f_ref[rows, :] + b_ref[...]
            logsig = jnp.minimum(x, 0.0) - jnp.log(1.0 + jnp.exp(-jnp.abs(x)))
            cum = _tri_sum(tri, logsig) + carry
            cum_ref[rows, :] = cum
            return cum[LANE - 1:LANE, :]

        lax.fori_loop(0, nb, step, jnp.zeros((1, w), F32))

    return pl.pallas_call(body, out_shape=jax.ShapeDtypeStruct((seq, w), F32), compiler_params=_params(), name="fgate_fwd")(fl, bf)


def _fgate_bwd(fl, bf, dcum):
    seq, w = fl.shape
    nb = seq // LANE

    def body(f_ref, b_ref, dc_ref, df_ref, db_ref):
        tri = _tri(False)

        def step(i, carry):
            tail, db = carry
            rows = pl.ds(pl.multiple_of((nb - 1 - i) * LANE, LANE), LANE)
            suffix = _tri_sum(tri, dc_ref[rows, :]) + tail
            df = suffix * (1.0 - _sigmoid(f_ref[rows, :] + b_ref[...]))
            df_ref[rows, :] = df
            return suffix[0:1, :], db + jnp.sum(df, axis=0, keepdims=True)

        _, db = lax.fori_loop(0, nb, step, (jnp.zeros((1, w), F32), jnp.zeros((1, w), F32)))
        db_ref[...] = db

    return pl.pallas_call(
        body, out_shape=(jax.ShapeDtypeStruct((seq, w), F32), jax.ShapeDtypeStruct((1, w), F32)),
        compiler_params=_params(), name="fgate_bwd",
    )(fl, bf, dcum)


def _cum_layouts(cum):
    seq = cum.shape[0]
    nblk = seq // ROW_TILE
    heads = cum[:, :N_FOX_HEADS]
    cq = jnp.repeat(heads, HEAD_DIM, axis=1).reshape(seq, N_HEAD_PAIRS, LANE).transpose(1, 0, 2)
    ck = heads.T.reshape(N_HEAD_PAIRS, 2, nblk, ROW_TILE).transpose(0, 2, 1, 3)
    ck = jnp.pad(ck, ((0, 0), (0, 0), (0, 6), (0, 0)))
    return cq, ck


def _dcum_from_layouts(dcq, dck):
    seq = dcq.shape[1]
    key_side = dck[:, :, :2, :].transpose(0, 2, 1, 3).reshape(N_FOX_HEADS, seq).T
    query_side = dcq[:, :, ::HEAD_DIM].transpose(1, 0, 2).reshape(seq, N_FOX_HEADS)
    return jnp.pad(key_side + query_side, ((0, 0), (0, F_PAD - N_FOX_HEADS)))


def _local_step(x, mem, target, w):
    vec = lambda a: a.reshape(1, -1)
    mem_n = _rms_fwd(mem, vec(w["g_mem"]))
    saved = []
    shared = None
    for l in range(N_LAYERS):
        s = {"x_in": x}
        s["h"] = h = _rms_fwd(x, vec(w["g_mix"][l]))
        s["mk"] = mk = _mm(mem_n, w["w_mk"][l])
        s["mv"] = mv = _mm(mem_n, w["w_mv"][l])
        if l < N_A:
            s["pa"] = pa = _mm(h, w["w_a"][l])
            s["pg"] = pg = _mm(h, w["w_g"][l])
            s["pq"] = pq = _mm(h, w["w_qm"][l])
            s["v"], s["c"] = _conv_glu_fwd(pa, pg, vec(w["b_glu_a"][l]), vec(w["b_glu_g"][l]), w["w_dw_a"][l], vec(w["b_dw_a"][l]))
            s["mix"] = mix = _ln_silu_fwd(s["c"], vec(w["ln_g"][l]), vec(w["ln_b"][l]))
        else:
            if l == N_A:
                shared = {"x_in": x}
                shared["hk"] = hk = _rms_fwd(x, vec(w["g_kv"]))
                shared["k"] = _mm(hk, w["w_k"])
                shared["v"] = _mm(hk, w["w_v"])
                shared["fl"] = _mm(hk, w["w_f"])
                cum = _fgate_fwd(shared["fl"], w["b_f"])
                shared["cq"], shared["ck"] = _cum_layouts(cum)
            s["q"] = q = _mm(h, w["w_q"][l - N_A])
            s["pq"] = pq = _mm(h, w["w_qm"][l])
            s["o"], mix, s["lse"] = _fox_fwd(q, shared["k"], shared["v"], shared["cq"], shared["ck"])
            s["mix"] = mix
        s["mo"] = mo = _memattn_fwd(pq, mk, mv)
        x = _mm(mix, w["w_o_mix"][l], add=x)
        s["x_mid"] = x = _mm(mo, w["w_o_mem"][l], add=x)
        s["hf"] = hf = _rms_fwd(x, vec(w["g_ffn"][l]))
        s["ug"] = ug = _mm(hf, w["w_up_g"][l])
        s["uv"] = uv = _mm(hf, w["w_up_v"][l])
        s["z"] = z = _ffn_mid_fwd(ug, uv, w["w_dw_f_g"][l], w["w_dw_f_v"][l], vec(w["b_dw_f_g"][l]), vec(w["b_dw_f_v"][l]))
        x = _mm(z, w["w_down"][l], add=x)
        saved.append(s)

    loss_row, dx, dxb, dg_final = _loss_head(x, vec(w["g_final"]), target)
    loss = loss_row[0, 0]

    g = {n: [None] * N_LAYERS for n in ("g_mix", "w_mem_kv", "w_out", "g_ffn", "w_up", "w_dw_f", "b_dw_f", "w_down")}
    for n in ("w_in_a", "b_glu", "w_dw_a", "b_dw_a", "ln_g", "ln_b", "w_in_b"):
        g[n] = [None] * N_A
    one = lambda arr: ([arr], 0)
    g["g_final"] = [one(dg_final[0])]
    dmem_n = None
    dk_sum = dv_sum = dcq_sum = dck_sum = None
    for l in reversed(range(N_LAYERS)):
        s = saved[l]
        dz = _mm(dxb, w["w_down"][l], "nt")
        g["w_down"][l] = one(_mm(s["z"], dxb, "tn"))
        dug, duv, dwg, dwv, dbg, dbv = _ffn_mid_bwd(s["ug"], s["uv"], dz, w["w_dw_f_g"][l], w["w_dw_f_v"][l],
                                                    vec(w["b_dw_f_g"][l]), vec(w["b_dw_f_v"][l]))
        g["w_up"][l] = ([_mm(s["hf"], dug, "tn"), _mm(s["hf"], duv, "tn")], 1)
        g["w_dw_f"][l] = ([dwg[:FFN_CONV_W], dwv[:FFN_CONV_W]], 1)
        g["b_dw_f"][l] = ([dbg[0], dbv[0]], 0)
        dhf = _mm(duv, w["w_up_v"][l], "nt", add=_mm(dug, w["w_up_g"][l], "nt"))
        dx, dxb, dg_ffn = _rms_bwd(s["x_mid"], vec(w["g_ffn"][l]), dhf, dx)
        g["g_ffn"][l] = one(dg_ffn[0])
        dmix = _mm(dxb, w["w_o_mix"][l], "nt")
        dmo = _mm(dxb, w["w_o_mem"][l], "nt")
        g["w_out"][l] = ([_mm(s["mix"], dxb, "tn"), _mm(s["mo"], dxb, "tn")], 0)
        dpq, dmk, dmv = _memattn_bwd(s["pq"], s["mk"], s["mv"], dmo)
        g["w_mem_kv"][l] = ([_mm(mem_n, dmk, "tn"), _mm(mem_n, dmv, "tn")], 1)
        dmem_n = _mm(dmk, w["w_mk"][l], "nt", add=dmem_n)
        dmem_n = _mm(dmv, w["w_mv"][l], "nt", add=dmem_n)
        if l < N_A:
            dc, dlng, dlnb = _ln_silu_bwd(dmix, s["c"], vec(w["ln_g"][l]), vec(w["ln_b"][l]))
            da, dgate, dwdw, dbdw, dba, dbg2 = _conv_glu_bwd(dc, s["v"], s["pa"], s["pg"], vec(w["b_glu_a"][l]),
                                                              vec(w["b_glu_g"][l]), w["w_dw_a"][l])
            g["ln_g"][l], g["ln_b"][l], g["b_dw_a"][l] = one(dlng[0]), one(dlnb[0]), one(dbdw[0])
            g["w_dw_a"][l] = ([dwdw[:CONV_W]], 1)
            g["b_glu"][l] = ([dba[0], dbg2[0]], 0)
            g["w_in_a"][l] = ([_mm(s["h"], da, "tn"), _mm(s["h"], dgate, "tn"), _mm(s["h"], dpq, "tn")], 1)
            dh = _mm(da, w["w_a"][l], "nt")
            dh = _mm(dgate, w["w_g"][l], "nt", add=dh)
            dh = _mm(dpq, w["w_qm"][l], "nt", add=dh)
        else:
            dq, dk, dv, dcq, dck = _fox_bwd(s["q"], shared["k"], shared["v"], shared["cq"], shared["ck"], s["o"], s["lse"], dmix)
            dk_sum = dk if dk_sum is None else dk_sum + dk
            dv_sum = dv if dv_sum is None else dv_sum + dv
            dcq_sum = dcq if dcq_sum is None else dcq_sum + dcq
            dck_sum = dck if dck_sum is None else dck_sum + dck
            g["w_in_b"][l - N_A] = ([_mm(s["h"], dq, "tn"), _mm(s["h"], dpq, "tn")], 1)
            dh = _mm(dq, w["w_q"][l - N_A], "nt")
            dh = _mm(dpq, w["w_qm"][l], "nt", add=dh)
        dx, dxb, dg_mix = _rms_bwd(s["x_in"], vec(w["g_mix"][l]), dh, dx)
        g["g_mix"][l] = one(dg_mix[0])
        if l == N_A:
            df, dbf = _fgate_bwd(shared["fl"], w["b_f"], _dcum_from_layouts(dcq_sum, dck_sum))
            hk = shared["hk"]
            g["w_kvf"] = [([_mm(hk, dk_sum, "tn"), _mm(hk, dv_sum, "tn"), _mm(hk, df, "tn")[:, :N_FOX_HEADS]], 1)]
            g["b_f"] = [one(dbf[0, :N_FOX_HEADS])]
            dhk = _mm(dk_sum, w["w_k"], "nt")
            dhk = _mm(dv_sum, w["w_v"], "nt", add=dhk)
            dhk = _mm(df, w["w_f"], "nt", add=dhk)
            dx, dxb, dg_kv = _rms_bwd(shared["x_in"], vec(w["g_kv"]), dhk, dx)
            g["g_kv"] = [one(dg_kv[0])]
    _, _, dg_mem = _rms_bwd(mem, vec(w["g_mem"]), dmem_n, None)
    g["g_mem"] = [one(dg_mem[0])]
    return loss, dx, g


SHARDED = {
    "w_in_a": ((N_A, D_MODEL, 2 * CONV_CH + MEM_W), 2),
    "b_glu": ((N_A, 2 * CONV_CH), 1),
    "w_dw_a": ((N_A, CONV_W, CONV_CH), 2),
    "b_dw_a": ((N_A, CONV_CH), 1),
    "ln_g": ((N_A, CONV_CH), 1),
    "ln_b": ((N_A, CONV_CH), 1),
    "w_kvf": ((D_MODEL, 2 * CONV_CH + N_FOX_HEADS), 1),
    "w_in_b": ((N_LAYERS - N_A, D_MODEL, CONV_CH + MEM_W), 1),
    "w_mem_kv": ((N_LAYERS, D_MODEL, 2 * MEM_W), 1),
    "w_out": ((N_LAYERS, D_MODEL, D_MODEL), 1),
    "w_up": ((N_LAYERS, D_MODEL, 2 * D_FF), 2),
    "w_dw_f": ((N_LAYERS, FFN_CONV_W, 2 * D_FF), 2),
    "w_down": ((N_LAYERS, D_FF, D_MODEL), 1),
}
REPLICATED = {
    "g_mix": (N_LAYERS, D_MODEL), "g_kv": (D_MODEL,), "b_f": (N_FOX_HEADS,), "g_mem": (D_MODEL,),
    "g_ffn": (N_LAYERS, D_MODEL), "b_dw_f": (N_LAYERS, 2 * D_FF), "g_final": (D_MODEL,),
}
WEIGHT_ORDER = ["g_mix", "w_in_a", "b_glu", "w_dw_a", "b_dw_a", "ln_g", "ln_b", "g_kv", "w_kvf", "b_f", "w_in_b", "g_mem",
                "w_mem_kv", "w_out", "g_ffn", "w_up", "w_dw_f", "b_dw_f", "w_down", "g_final"]


def _shard_shape(name):
    shape, axis = SHARDED[name]
    return tuple(d // N_CHIPS if i == axis else d for i, d in enumerate(shape))


def _size(shape):
    n = 1
    for d in shape:
        n *= d
    return n


F32_LEAVES = ("b_glu", "w_dw_a", "b_dw_a", "ln_g", "ln_b", "w_dw_f")
BF16_LEAVES = tuple(n for n in SHARDED if n not in F32_LEAVES)
_HALF_ROW_TILE = 16


def _shard_elems(names):
    return sum(_size(_shard_shape(n)) for n in names)


def _flat_rows(elems):
    return -(-elems // (FLAT_COLS * 2 * _HALF_ROW_TILE)) * 2 * _HALF_ROW_TILE


G_ROWS = _flat_rows(_shard_elems(SHARDED))
REP_ELEMS = sum(_size(s) for s in REPLICATED.values())
REP_ROWS = 8
REP_COLS = -(-REP_ELEMS // (REP_ROWS * LANE)) * LANE


def _flatten_weights(leaves, names, dtype):
    rows = _flat_rows(_shard_elems(names))
    flat = jnp.concatenate([leaves[n].reshape(-1).astype(dtype) for n in names])
    flat = jnp.pad(flat, (0, rows * FLAT_COLS - _shard_elems(names)))
    return flat.reshape(2, rows // 2, FLAT_COLS)


def _unflatten_grads(flat):
    flat = flat.reshape(-1)
    out, off = {}, 0
    for n in SHARDED:
        shp = _shard_shape(n)
        out[n] = flat[off:off + _size(shp)].reshape(shp)
        off += _size(shp)
    return out


def _gathered_blocks(gathered, names):
    flat = gathered.reshape(N_CHIPS, -1)
    out, off = {}, 0
    for n in names:
        shp = _shard_shape(n)
        out[n] = flat[:, off:off + _size(shp)].reshape((N_CHIPS,) + shp)
        off += _size(shp)
    return out


def _span(parts, cat_axis, sel_axis, lo, hi):
    if sel_axis != cat_axis:
        return jnp.concatenate([lax.slice_in_dim(p, lo, hi, axis=sel_axis) for p in parts], axis=cat_axis)
    taken, off = [], 0
    for p in parts:
        n = p.shape[cat_axis]
        a, b = max(lo, off), min(hi, off + n)
        if a < b:
            taken.append(lax.slice_in_dim(p, a - off, b - off, axis=cat_axis))
        off += n
    return taken[0] if len(taken) == 1 else jnp.concatenate(taken, axis=cat_axis)


def _weight_pieces(blocks, rep):
    def cut(name, layer, sel_axis, lo, hi):
        axis = SHARDED[name][1]
        parts = [blocks[name][t] if layer is None else blocks[name][t, layer] for t in range(N_CHIPS)]
        if layer is not None:
            axis -= 1
        return _span(parts, axis, sel_axis, lo, hi)

    def per_layer(name, n_layers, sel_axis, lo, hi):
        return [cut(name, l, sel_axis, lo, hi) for l in range(n_layers)]

    def pad_rows(arr, rows):
        return jnp.pad(arr, ((0, rows - arr.shape[0]), (0, 0)))

    w = dict(rep)
    w["w_a"] = per_layer("w_in_a", N_A, 1, 0, CONV_CH)
    w["w_g"] = per_layer("w_in_a", N_A, 1, CONV_CH, 2 * CONV_CH)
    w["w_q"] = per_layer("w_in_b", N_LAYERS - N_A, 1, 0, CONV_CH)
    w["w_qm"] = per_layer("w_in_a", N_A, 1, 2 * CONV_CH, 2 * CONV_CH + MEM_W) + per_layer("w_in_b", N_LAYERS - N_A, 1, CONV_CH, CONV_CH + MEM_W)
    w["b_glu_a"] = per_layer("b_glu", N_A, 0, 0, CONV_CH)
    w["b_glu_g"] = per_layer("b_glu", N_A, 0, CONV_CH, 2 * CONV_CH)
    w["w_dw_a"] = [pad_rows(p, CONV_PAD) for p in per_layer("w_dw_a", N_A, 1, 0, CONV_CH)]
    for n in ("b_dw_a", "ln_g", "ln_b"):
        w[n] = per_layer(n, N_A, 0, 0, CONV_CH)
    w["w_k"] = cut("w_kvf", None, 1, 0, CONV_CH)
    w["w_v"] = cut("w_kvf", None, 1, CONV_CH, 2 * CONV_CH)
    w["w_f"] = jnp.pad(cut("w_kvf", None, 1, 2 * CONV_CH, 2 * CONV_CH + N_FOX_HEADS), ((0, 0), (0, F_PAD - N_FOX_HEADS)))
    w["b_f"] = jnp.pad(rep["b_f"], (0, F_PAD - N_FOX_HEADS)).reshape(1, F_PAD)
    w["w_mk"] = per_layer("w_mem_kv", N_LAYERS, 1, 0, MEM_W)
    w["w_mv"] = per_layer("w_mem_kv", N_LAYERS, 1, MEM_W, 2 * MEM_W)
    w["w_o_mix"] = per_layer("w_out", N_LAYERS, 0, 0, CONV_CH)
    w["w_o_mem"] = per_layer("w_out", N_LAYERS, 0, CONV_CH, D_MODEL)
    w["w_up_g"] = per_layer("w_up", N_LAYERS, 1, 0, D_FF)
    w["w_up_v"] = per_layer("w_up", N_LAYERS, 1, D_FF, 2 * D_FF)
    w["w_dw_f_g"] = [pad_rows(p, FFN_PAD) for p in per_layer("w_dw_f", N_LAYERS, 1, 0, D_FF)]
    w["w_dw_f_v"] = [pad_rows(p, FFN_PAD) for p in per_layer("w_dw_f", N_LAYERS, 1, D_FF, 2 * D_FF)]
    w["b_dw_f_g"], w["b_dw_f_v"] = rep["b_dw_f"][:, :D_FF], rep["b_dw_f"][:, D_FF:]
    w["w_down"] = per_layer("w_down", N_LAYERS, 0, 0, D_FF)
    return w


def _scatter_slabs(grads):
    slabs = []
    for t in range(N_CHIPS):
        parts = []
        for n, (shape, axis) in SHARDED.items():
            entries = grads[n]
            stacked = len(shape) - len(entries[0][0][0].shape)
            step = shape[axis] // N_CHIPS
            for pieces, cat_axis in entries:
                parts.append(_span(pieces, cat_axis, axis - stacked, t * step, (t + 1) * step).reshape(-1).astype(BF16))
        flat = jnp.concatenate(parts)
        slabs.append(jnp.pad(flat, (0, G_ROWS * FLAT_COLS - _shard_elems(SHARDED))))
    return jnp.stack(slabs).reshape(N_CHIPS, 2, G_ROWS // 2, FLAT_COLS)


def _replicated_rows(grads):
    parts = [p.reshape(-1) for n in REPLICATED for pieces, _ in grads[n] for p in pieces]
    flat = jnp.concatenate(parts)
    return jnp.pad(flat, (0, REP_ROWS * REP_COLS - REP_ELEMS)).reshape(REP_ROWS, REP_COLS)


_ANY = pl.BlockSpec(memory_space=pl.ANY)


def _place():
    x, y, c = lax.axis_index("x"), lax.axis_index("y"), lax.axis_index("c")
    chips = [(1 - x, y), (x, 1 - y), (1 - x, 1 - y)]
    return x, y, c, chips


def _chip_index(chip):
    return 2 * chip[0] + chip[1]


def _gather_weights(w_half, name):
    def body(w_ref, out_ref, send_sems, recv_sems):
        x, y, c, chips = _place()
        me = 2 * x + y
        sibling = (x, y, 1 - c)

        def copy(sem, chip_idx, half, to, src=None):
            dst = out_ref.at[chip_idx, half]
            return pltpu.make_async_remote_copy(src_ref=dst if src is None else src, dst_ref=dst, send_sem=send_sems.at[sem],
                                                recv_sem=recv_sems.at[sem], device_id=to, device_id_type=MESH_ID)

        first = [copy(j, me, c, (*chip, c), src=w_ref.at[c]) for j, chip in enumerate(chips)]
        for cp in first:
            cp.start()
        passed = [copy(3 + j, _chip_index(chip), c, sibling) for j, chip in enumerate(chips)]
        for j, chip in enumerate(chips):
            copy(j, _chip_index(chip), c, (x, y, c)).wait_recv()
            passed[j].start()
        for j, chip in enumerate(chips):
            copy(3 + j, _chip_index(chip), 1 - c, (x, y, c)).wait_recv()
        for cp in first + passed:
            cp.wait_send()

    return pl.pallas_call(
        body, out_shape=jax.ShapeDtypeStruct((N_CHIPS,) + w_half.shape, w_half.dtype), in_specs=[_ANY], out_specs=_ANY,
        scratch_shapes=[pltpu.SemaphoreType.DMA((6,)), pltpu.SemaphoreType.DMA((6,))],
        name=name,
    )(w_half)


def _swap_halves(slabs):
    def body(g_ref, got_ref, send_sems, recv_sems):
        x, y, c, _ = _place()
        copies = [pltpu.make_async_remote_copy(src_ref=g_ref.at[t, 1 - c], dst_ref=got_ref.at[t], send_sem=send_sems.at[t],
                                               recv_sem=recv_sems.at[t], device_id=(x, y, 1 - c), device_id_type=MESH_ID)
                  for t in range(N_CHIPS)]
        for cp in copies:
            cp.start()
        for cp in copies:
            cp.wait()

    shape = (slabs.shape[0],) + slabs.shape[2:]
    return pl.pallas_call(
        body, out_shape=jax.ShapeDtypeStruct(shape, slabs.dtype), in_specs=[_ANY], out_specs=_ANY,
        scratch_shapes=[pltpu.SemaphoreType.DMA((N_CHIPS,)), pltpu.SemaphoreType.DMA((N_CHIPS,))], name="swap_halves",
    )(slabs)


def _scatter_to_chips(part):
    def body(p_ref, got_ref, send_sems, recv_sems):
        x, y, c, chips = _place()
        me = 2 * x + y
        sends = [pltpu.make_async_remote_copy(src_ref=p_ref.at[_chip_index(chip)], dst_ref=got_ref.at[me], send_sem=send_sems.at[j],
                                              recv_sem=recv_sems.at[j], device_id=(*chip, c), device_id_type=MESH_ID)
                 for j, chip in enumerate(chips)]
        for cp in sends:
            cp.start()
        for j, chip in enumerate(chips):
            landed = got_ref.at[_chip_index(chip)]
            pltpu.make_async_remote_copy(src_ref=landed, dst_ref=landed, send_sem=send_sems.at[j], recv_sem=recv_sems.at[j],
                                         device_id=(x, y, c), device_id_type=MESH_ID).wait_recv()
        for cp in sends:
            cp.wait_send()

    return pl.pallas_call(
        body, out_shape=jax.ShapeDtypeStruct(part.shape, part.dtype), in_specs=[_ANY], out_specs=_ANY,
        scratch_shapes=[pltpu.SemaphoreType.DMA((3,)), pltpu.SemaphoreType.DMA((3,))],
        name="scatter_to_chips",
    )(part)


def _sibling_half(half):
    def body(h_ref, got_ref, send_sem, recv_sem):
        x, y, c, _ = _place()
        cp = pltpu.make_async_remote_copy(src_ref=h_ref, dst_ref=got_ref, send_sem=send_sem, recv_sem=recv_sem,
                                          device_id=(x, y, 1 - c), device_id_type=MESH_ID)
        cp.start()
        cp.wait()

    return pl.pallas_call(
        body, out_shape=jax.ShapeDtypeStruct(half.shape, half.dtype), in_specs=[_ANY], out_specs=_ANY,
        scratch_shapes=[pltpu.SemaphoreType.DMA, pltpu.SemaphoreType.DMA], name="sibling_half",
    )(half)


def _gather_replicated(rows):
    m_per, n = rows.shape

    def body(x_ref, out_ref, send_sems, recv_sems, local_sem):
        x, y, c, chips = _place()
        me, sibling = (x, y, c), (x, y, 1 - c)

        def block(px, py, pc):
            return out_ref.at[pl.ds((4 * px + 2 * py + pc) * m_per, m_per), :]

        def copy(k, blk, to, src=None):
            return pltpu.make_async_remote_copy(src_ref=block(*blk) if src is None else src, dst_ref=block(*blk),
                                                send_sem=send_sems.at[k], recv_sem=recv_sems.at[k], device_id=to, device_id_type=MESH_ID)

        mine = pltpu.make_async_copy(x_ref, block(*me), local_sem)
        mine.start()
        first = [copy(0, me, sibling, src=x_ref)]
        first += [copy(1 + j, me, (*chip, c), src=x_ref) for j, chip in enumerate(chips)]
        for cp in first:
            cp.start()
        passed = [copy(4 + j, (*chip, c), sibling) for j, chip in enumerate(chips)]
        for j, chip in enumerate(chips):
            copy(1 + j, (*chip, c), me).wait_recv()
            passed[j].start()
        copy(0, sibling, me).wait_recv()
        for j, chip in enumerate(chips):
            copy(4 + j, (*chip, 1 - c), me).wait_recv()
        for cp in first + passed:
            cp.wait_send()
        mine.wait()

    vmem = pl.BlockSpec(memory_space=pltpu.VMEM)
    return pl.pallas_call(
        body, out_shape=jax.ShapeDtypeStruct((N_DEV * m_per, n), rows.dtype), in_specs=[vmem], out_specs=vmem,
        scratch_shapes=[pltpu.SemaphoreType.DMA((7,)), pltpu.SemaphoreType.DMA((7,)), pltpu.SemaphoreType.DMA],
        name="gather_replicated",
    )(rows)


_SUM_ROWS = 256


def _tile_rows(rows):
    if rows <= _SUM_ROWS:
        return rows
    best = None
    for r in range(16, _SUM_ROWS + 1, 16):
        if rows % r == 0:
            best = r
    assert best is not None, rows
    return best


def _add_pairs(slabs, got, core):
    n, _, rows, cols = slabs.shape
    tr = _tile_rows(rows)

    def body(core_ref, a_ref, b_ref, o_ref):
        o_ref[...] = (a_ref[0].astype(F32) + b_ref[...].astype(F32)).astype(BF16)

    spec = pl.BlockSpec((1, tr, cols), lambda t, i, core_ref: (t, i, 0))
    return pl.pallas_call(
        body, out_shape=jax.ShapeDtypeStruct((n, rows, cols), BF16),
        grid_spec=pltpu.PrefetchScalarGridSpec(
            num_scalar_prefetch=1, grid=(n, rows // tr),
            in_specs=[pl.BlockSpec((1, 1, tr, cols), lambda t, i, core_ref: (t, core_ref[0], i, 0)), spec], out_specs=spec),
        compiler_params=_params(("parallel", "parallel")), name="add_pairs",
    )(core, slabs, got)


def _sum_slabs(slabs):
    n, rows, cols = slabs.shape
    tr = _tile_rows(rows)

    def body(s_ref, o_ref):
        acc = s_ref[0].astype(F32)
        for t in range(1, n):
            acc = acc + s_ref[t].astype(F32)
        o_ref[...] = acc

    return pl.pallas_call(
        body, out_shape=jax.ShapeDtypeStruct((rows, cols), F32), grid=(rows // tr,),
        in_specs=[pl.BlockSpec((n, tr, cols), lambda i: (0, i, 0))], out_specs=pl.BlockSpec((tr, cols), lambda i: (i, 0)),
        compiler_params=_params(("parallel",)), name=f"sum_slabs_{n}x{rows}x{cols}",
    )(slabs)


_ADAM_BLOCK_BYTES = 2 * 1024 * 1024


def _adamw(w, g, m, v):
    shape = w.shape
    cols = shape[-1]
    rows = _size(shape) // cols
    tr = rows
    if rows * cols * 4 > _ADAM_BLOCK_BYTES:
        for r in range(8, rows, 8):
            if rows % r == 0 and r * cols * 4 <= _ADAM_BLOCK_BYTES:
                tr = r

    def body(w_ref, g_ref, m_ref, v_ref, d_ref, nm_ref, nv_ref):
        gv = g_ref[...]
        nm = ADAM_B1 * m_ref[...] + (1.0 - ADAM_B1) * gv
        nv = ADAM_B2 * v_ref[...] + (1.0 - ADAM_B2) * jnp.square(gv)
        m_hat = nm / (1.0 - ADAM_B1 ** ADAM_STEP)
        v_hat = nv / (1.0 - ADAM_B2 ** ADAM_STEP)
        d_ref[...] = -ADAM_LR * (m_hat / (jnp.sqrt(v_hat) + ADAM_EPS) + ADAM_WD * w_ref[...])
        nm_ref[...] = nm
        nv_ref[...] = nv

    spec = pl.BlockSpec((tr, cols), lambda i: (i, 0))
    out = pl.pallas_call(
        body, out_shape=tuple(jax.ShapeDtypeStruct((rows, cols), F32) for _ in range(3)), grid=(rows // tr,),
        in_specs=[spec] * 4, out_specs=(spec,) * 3, compiler_params=_params(("parallel",)), name=f"adamw_{rows}x{cols}",
    )(*(a.reshape(rows, cols) for a in (w, g, m, v)))
    return tuple(o.reshape(shape) for o in out)


def kernel(x, mem, g_mix, w_in_a, b_glu, w_dw_a, b_dw_a, ln_g, ln_b, g_kv, w_kvf, b_f, w_in_b, g_mem, w_mem_kv, w_out, g_ffn, w_up, w_dw_f, b_dw_f, w_down, g_final, loss_target, m_g_mix, m_w_in_a, m_b_glu, m_w_dw_a, m_b_dw_a, m_ln_g, m_ln_b, m_g_kv, m_w_kvf, m_b_f, m_w_in_b, m_g_mem, m_w_mem_kv, m_w_out, m_g_ffn, m_w_up, m_w_dw_f, m_b_dw_f, m_w_down, m_g_final, v_g_mix, v_w_in_a, v_b_glu, v_w_dw_a, v_b_dw_a, v_ln_g, v_ln_b, v_g_kv, v_w_kvf, v_b_f, v_w_in_b, v_g_mem, v_w_mem_kv, v_w_out, v_g_ffn, v_w_up, v_w_dw_f, v_b_dw_f, v_w_down, v_g_final):
    weights = dict(g_mix=g_mix, w_in_a=w_in_a, b_glu=b_glu, w_dw_a=w_dw_a, b_dw_a=b_dw_a, ln_g=ln_g, ln_b=ln_b, g_kv=g_kv,
                   w_kvf=w_kvf, b_f=b_f, w_in_b=w_in_b, g_mem=g_mem, w_mem_kv=w_mem_kv, w_out=w_out, g_ffn=g_ffn, w_up=w_up,
                   w_dw_f=w_dw_f, b_dw_f=b_dw_f, w_down=w_down, g_final=g_final)
    mom1 = dict(g_mix=m_g_mix, w_in_a=m_w_in_a, b_glu=m_b_glu, w_dw_a=m_w_dw_a, b_dw_a=m_b_dw_a, ln_g=m_ln_g, ln_b=m_ln_b,
                g_kv=m_g_kv, w_kvf=m_w_kvf, b_f=m_b_f, w_in_b=m_w_in_b, g_mem=m_g_mem, w_mem_kv=m_w_mem_kv, w_out=m_w_out,
                g_ffn=m_g_ffn, w_up=m_w_up, w_dw_f=m_w_dw_f, b_dw_f=m_b_dw_f, w_down=m_w_down, g_final=m_g_final)
    mom2 = dict(g_mix=v_g_mix, w_in_a=v_w_in_a, b_glu=v_b_glu, w_dw_a=v_w_dw_a, b_dw_a=v_b_dw_a, ln_g=v_ln_g, ln_b=v_ln_b,
                g_kv=v_g_kv, w_kvf=v_w_kvf, b_f=v_b_f, w_in_b=v_w_in_b, g_mem=v_g_mem, w_mem_kv=v_w_mem_kv, w_out=v_w_out,
                g_ffn=v_g_ffn, w_up=v_w_up, w_dw_f=v_w_dw_f, b_dw_f=v_b_dw_f, w_down=v_w_down, g_final=v_g_final)

    x_pos, y_pos, core = lax.axis_index("x"), lax.axis_index("y"), lax.axis_index("c")
    chip = 2 * x_pos + y_pos

    blocks = {}
    for names, dtype, name in ((BF16_LEAVES, BF16, "gather_matmul_weights"), (F32_LEAVES, F32, "gather_vector_weights")):
        local = _flatten_weights(weights, names, dtype)
        gathered = lax.dynamic_update_index_in_dim(_gather_weights(local, name), local[None], chip, 0)
        blocks.update(_gathered_blocks(gathered, names))
    pieces = _weight_pieces(blocks, {n: weights[n] for n in REPLICATED})

    loss, dx, grads = _local_step(x[0], mem[0], loss_target[0], pieces)
    loss = lax.psum(loss, ("x", "y", "c"))

    slabs = _scatter_slabs(grads)
    chip_sum = _add_pairs(slabs, _swap_halves(slabs), core.astype(jnp.int32).reshape(1))
    own = lax.dynamic_index_in_dim(chip_sum, chip, 0, keepdims=True)
    from_chips = lax.dynamic_update_index_in_dim(_scatter_to_chips(chip_sum), own, chip, 0)
    mine = _sum_slabs(from_chips)
    theirs = _sibling_half(mine)
    reduced = jnp.concatenate([jnp.where(core == 0, mine, theirs), jnp.where(core == 0, theirs, mine)])
    grad_leaves = _unflatten_grads(reduced)

    rep_sum = _sum_slabs(_gather_replicated(_replicated_rows(grads)).reshape(N_DEV, REP_ROWS, REP_COLS)).reshape(-1)
    off = 0
    for n, shp in REPLICATED.items():
        grad_leaves[n] = rep_sum[off:off + _size(shp)].reshape(shp)
        off += _size(shp)

    deltas, new_m, new_v = {}, {}, {}
    for n in WEIGHT_ORDER:
        deltas[n], new_m[n], new_v[n] = _adamw(weights[n], grad_leaves[n], mom1[n], mom2[n])
    return (loss, dx[None], *[grad_leaves[n] for n in WEIGHT_ORDER], *[deltas[n] for n in WEIGHT_ORDER],
            *[new_m[n] for n in WEIGHT_ORDER], *[new_v[n] for n in WEIGHT_ORDER])
```

```python
import jax
import jax.numpy as jnp
from jax import lax
from jax.experimental import pallas as pl
from jax.experimental.pallas import tpu as pltpu

F32 = jnp.float32
BF16 = jnp.bfloat16

D_MODEL = 1024
N_LAYERS = 4
N_A = 2
CONV_CH = 768
MEM_W = 256
HEAD_DIM = 64
N_MEM_HEADS = 4
N_FOX_HEADS = 12
N_HEAD_PAIRS = N_FOX_HEADS // 2
D_FF = 2816
CONV_W = 31
CONV_PAD = 32
FFN_CONV_W = 3
FFN_PAD = 8
F_PAD = 128
RMS_EPS = 1e-6
LN_EPS = 1e-5
ATT_SCALE = HEAD_DIM ** -0.5
NEG_BIG = -1e30

ADAM_LR = 0.001
ADAM_B1 = 0.9
ADAM_B2 = 0.999
ADAM_EPS = 1e-08
ADAM_WD = 0.01
ADAM_STEP = 10

LANE = 128
ROW_TILE = 256
CHUNK = 128
VMEM_LIMIT = 48 * 1024 * 1024
FLAT_COLS = 1024
N_CHIPS = 4
N_DEV = 8
MESH_ID = pl.DeviceIdType.MESH


def _params(sem=None):
    return pltpu.CompilerParams(dimension_semantics=sem, vmem_limit_bytes=VMEM_LIMIT)


def _tile(dim, pref):
    if dim <= pref:
        return dim
    best = None
    for m in range(1, dim // LANE + 1):
        d = m * LANE
        if dim % d == 0 and d <= pref:
            best = d
    assert best is not None, dim
    return best


_DIMS = {"nn": (((1,), (0,)), ((), ())), "nt": (((1,), (1,)), ((), ())), "tn": (((0,), (0,)), ((), ()))}


def _mm(a, b, mode="nn", out_dtype=F32, add=None):
    if mode == "nn":
        (m, k), (k2, n) = a.shape, b.shape
    elif mode == "nt":
        (m, k), (n, k2) = a.shape, b.shape
    else:
        (k, m), (k2, n) = a.shape, b.shape
    assert k == k2, (a.shape, b.shape, mode)
    tm, tn = _tile(m, 512), _tile(n, 1408)
    dims = _DIMS[mode]
    has_add = add is not None

    def body(*refs):
        if has_add:
            a_ref, b_ref, add_ref, o_ref = refs
        else:
            a_ref, b_ref, o_ref = refs
        r = lax.dot_general(a_ref[...].astype(BF16), b_ref[...].astype(BF16), dims, preferred_element_type=F32)
        if has_add:
            r = r + add_ref[...]
        o_ref[...] = r.astype(out_dtype)

    a_spec = pl.BlockSpec((k, tm), lambda i, j: (0, i)) if mode == "tn" else pl.BlockSpec((tm, k), lambda i, j: (i, 0))
    b_spec = pl.BlockSpec((tn, k), lambda i, j: (j, 0)) if mode == "nt" else pl.BlockSpec((k, tn), lambda i, j: (0, j))
    o_spec = pl.BlockSpec((tm, tn), lambda i, j: (i, j))
    in_specs = [a_spec, b_spec] + ([o_spec] if has_add else [])
    args = (a, b) + ((add,) if has_add else ())
    return pl.pallas_call(
        body,
        out_shape=jax.ShapeDtypeStruct((m, n), out_dtype),
        grid=(m // tm, n // tn),
        in_specs=in_specs,
        out_specs=o_spec,
        compiler_params=_params(("parallel", "parallel")),
        name=f"mm_{mode}_{m}x{k}x{n}",
    )(*args)


def _row(width):
    return pl.BlockSpec((ROW_TILE, width), lambda i: (i, 0))


def _vec(width):
    return pl.BlockSpec((1, width), lambda i: (0, 0))


def _rms_fwd(x, g):
    rows, d = x.shape

    def body(x_ref, g_ref, o_ref):
        xv = x_ref[...]
        rstd = lax.rsqrt(jnp.mean(xv * xv, axis=-1, keepdims=True) + RMS_EPS)
        o_ref[...] = (xv * rstd * g_ref[...]).astype(BF16)

    return pl.pallas_call(
        body, out_shape=jax.ShapeDtypeStruct((rows, d), BF16), grid=(rows // ROW_TILE,),
        in_specs=[_row(d), _vec(d)], out_specs=_row(d), compiler_params=_params(("parallel",)), name=f"rms_fwd_{rows}",
    )(x, g)


def _accumulate(ref, value, step):
    @pl.when(step == 0)
    def _():
        ref[...] = value

    @pl.when(step > 0)
    def _():
        ref[...] += value


def _rms_bwd(x, g, dh, dres):
    rows, d = x.shape
    has_res = dres is not None

    def body(*refs):
        if has_res:
            x_ref, g_ref, dh_ref, dres_ref, dx_ref, dxb_ref, dg_ref = refs
        else:
            x_ref, g_ref, dh_ref, dx_ref, dxb_ref, dg_ref = refs
        xv = x_ref[...]
        dhv = dh_ref[...]
        rstd = lax.rsqrt(jnp.mean(xv * xv, axis=-1, keepdims=True) + RMS_EPS)
        xhat = xv * rstd
        gd = dhv * g_ref[...]
        dx = rstd * (gd - xhat * jnp.mean(gd * xhat, axis=-1, keepdims=True))
        if has_res:
            dx = dx + dres_ref[...]
        dx_ref[...] = dx
        dxb_ref[...] = dx.astype(BF16)
        _accumulate(dg_ref, jnp.sum(dhv * xhat, axis=0, keepdims=True), pl.program_id(0))

    in_specs = [_row(d), _vec(d), _row(d)] + ([_row(d)] if has_res else [])
    args = (x, g, dh) + ((dres,) if has_res else ())
    return pl.pallas_call(
        body,
        out_shape=(jax.ShapeDtypeStruct((rows, d), F32), jax.ShapeDtypeStruct((rows, d), BF16), jax.ShapeDtypeStruct((1, d), F32)),
        grid=(rows // ROW_TILE,), in_specs=in_specs, out_specs=(_row(d), _row(d), _vec(d)),
        compiler_params=_params(("arbitrary",)), name=f"rms_bwd_{rows}_{int(has_res)}",
    )(*args)


def _loss_head(x, g, target):
    rows, d = x.shape

    def body(x_ref, g_ref, t_ref, loss_ref, dx_ref, dxb_ref, dg_ref):
        xv = x_ref[...]
        gv = g_ref[...]
        rstd = lax.rsqrt(jnp.mean(xv * xv, axis=-1, keepdims=True) + RMS_EPS)
        xhat = xv * rstd
        err = xhat * gv - t_ref[...]
        part = 0.5 * jnp.sum(jnp.mean(err * err, axis=-1, keepdims=True), axis=0, keepdims=True)
        dy = err * (1.0 / d)
        gd = dy * gv
        dx = rstd * (gd - xhat * jnp.mean(gd * xhat, axis=-1, keepdims=True))
        dx_ref[...] = dx
        dxb_ref[...] = dx.astype(BF16)
        step = pl.program_id(0)
        _accumulate(loss_ref, jnp.broadcast_to(part, (1, LANE)), step)
        _accumulate(dg_ref, jnp.sum(dy * xhat, axis=0, keepdims=True), step)

    return pl.pallas_call(
        body,
        out_shape=(jax.ShapeDtypeStruct((1, LANE), F32), jax.ShapeDtypeStruct((rows, d), F32),
                   jax.ShapeDtypeStruct((rows, d), BF16), jax.ShapeDtypeStruct((1, d), F32)),
        grid=(rows // ROW_TILE,), in_specs=[_row(d), _vec(d), _row(d)],
        out_specs=(_vec(LANE), _row(d), _row(d), _vec(d)),
        compiler_params=_params(("arbitrary",)), name="loss_head",
    )(x, g, target)


def _sigmoid(x):
    return 1.0 / (1.0 + jnp.exp(-x))


def _ln_silu_fwd(c, ln_g, ln_b):
    rows, ch = c.shape

    def body(c_ref, g_ref, b_ref, o_ref):
        cv = c_ref[...]
        mu = jnp.mean(cv, axis=-1, keepdims=True)
        cen = cv - mu
        rstd = lax.rsqrt(jnp.mean(cen * cen, axis=-1, keepdims=True) + LN_EPS)
        y = cen * rstd * g_ref[...] + b_ref[...]
        o_ref[...] = (y * _sigmoid(y)).astype(BF16)

    return pl.pallas_call(
        body, out_shape=jax.ShapeDtypeStruct((rows, ch), BF16), grid=(rows // ROW_TILE,),
        in_specs=[_row(ch), _vec(ch), _vec(ch)], out_specs=_row(ch), compiler_params=_params(("parallel",)), name="ln_silu_fwd",
    )(c, ln_g, ln_b)


def _ln_silu_bwd(dmix, c, ln_g, ln_b):
    rows, ch = c.shape

    def body(dm_ref, c_ref, g_ref, b_ref, dc_ref, dg_ref, db_ref):
        cv = c_ref[...]
        gv = g_ref[...]
        mu = jnp.mean(cv, axis=-1, keepdims=True)
        cen = cv - mu
        rstd = lax.rsqrt(jnp.mean(cen * cen, axis=-1, keepdims=True) + LN_EPS)
        xhat = cen * rstd
        y = xhat * gv + b_ref[...]
        sg = _sigmoid(y)
        dy = dm_ref[...] * (sg * (1.0 + y * (1.0 - sg)))
        dxh = dy * gv
        dc = rstd * (dxh - jnp.mean(dxh, axis=-1, keepdims=True) - xhat * jnp.mean(dxh * xhat, axis=-1, keepdims=True))
        dc_ref[...] = dc
        step = pl.program_id(0)
        _accumulate(dg_ref, jnp.sum(dy * xhat, axis=0, keepdims=True), step)
        _accumulate(db_ref, jnp.sum(dy, axis=0, keepdims=True), step)

    return pl.pallas_call(
        body,
        out_shape=(jax.ShapeDtypeStruct((rows, ch), F32), jax.ShapeDtypeStruct((1, ch), F32), jax.ShapeDtypeStruct((1, ch), F32)),
        grid=(rows // ROW_TILE,), in_specs=[_row(ch), _row(ch), _vec(ch), _vec(ch)],
        out_specs=(_row(ch), _vec(ch), _vec(ch)), compiler_params=_params(("arbitrary",)), name="ln_silu_bwd",
    )(dmix, c, ln_g, ln_b)


def _col(rows, cb):
    return pl.BlockSpec((rows, cb), lambda j: (0, j))


def _conv_glu_fwd(pa, pg, ba, bg, w_dw, b_dw):
    seq, ch = pa.shape
    cb = LANE
    n_chunks = seq // CHUNK

    def body(pa_ref, pg_ref, ba_ref, bg_ref, w_ref, b_ref, v_ref, c_ref, vpad_ref):
        vpad_ref[0:CONV_PAD, :] = jnp.zeros((CONV_PAD, cb), F32)

        def glu(r, carry):
            r0 = pl.multiple_of(r * CHUNK, CHUNK)
            rows = pl.ds(r0, CHUNK)
            v = (pa_ref[rows, :] + ba_ref[...]) * _sigmoid(pg_ref[rows, :] + bg_ref[...])
            v_ref[rows, :] = v
            vpad_ref[pl.ds(r0 + CONV_PAD, CHUNK), :] = v
            return carry

        lax.fori_loop(0, n_chunks, glu, 0)

        def conv(r, carry):
            r0 = pl.multiple_of(r * CHUNK, CHUNK)
            win = vpad_ref[pl.ds(r0, CHUNK + CONV_PAD), :]
            acc = jnp.broadcast_to(b_ref[...], (CHUNK, cb))
            for t in range(CONV_W):
                off = CONV_PAD - (CONV_W - 1) + t
                acc = acc + w_ref[t:t + 1, :] * win[off:off + CHUNK, :]
            c_ref[pl.ds(r0, CHUNK), :] = acc
            return carry

        lax.fori_loop(0, n_chunks, conv, 0)

    return pl.pallas_call(
        body,
        out_shape=(jax.ShapeDtypeStruct((seq, ch), F32), jax.ShapeDtypeStruct((seq, ch), F32)),
        grid=(ch // cb,),
        in_specs=[_col(seq, cb), _col(seq, cb), _col(1, cb), _col(1, cb), _col(CONV_PAD, cb), _col(1, cb)],
        out_specs=(_col(seq, cb), _col(seq, cb)),
        scratch_shapes=[pltpu.VMEM((seq + CONV_PAD, cb), F32)],
        compiler_params=_params(("parallel",)), name="conv_glu_fwd",
    )(pa, pg, ba, bg, w_dw, b_dw)


def _conv_glu_bwd(dc, v, pa, pg, ba, bg, w_dw):
    seq, ch = dc.shape
    cb = LANE
    n_chunks = seq // CHUNK

    def body(dc_ref, v_ref, pa_ref, pg_ref, ba_ref, bg_ref, w_ref, da_ref, dg_ref, dw_ref, dbdw_ref, dba_ref, dbg_ref,
             dcpad_ref, vpad_ref):
        vpad_ref[0:CONV_PAD, :] = jnp.zeros((CONV_PAD, cb), F32)
        dcpad_ref[seq:seq + CONV_PAD, :] = jnp.zeros((CONV_PAD, cb), F32)
        dw_ref[...] = jnp.zeros((CONV_PAD, cb), F32)
        dbdw_ref[...] = jnp.zeros((1, cb), F32)
        dba_ref[...] = jnp.zeros((1, cb), F32)
        dbg_ref[...] = jnp.zeros((1, cb), F32)

        def fill(r, carry):
            r0 = pl.multiple_of(r * CHUNK, CHUNK)
            vpad_ref[pl.ds(r0 + CONV_PAD, CHUNK), :] = v_ref[pl.ds(r0, CHUNK), :]
            dcpad_ref[pl.ds(r0, CHUNK), :] = dc_ref[pl.ds(r0, CHUNK), :]
            return carry

        lax.fori_loop(0, n_chunks, fill, 0)

        def step(r, carry):
            r0 = pl.multiple_of(r * CHUNK, CHUNK)
            rows = pl.ds(r0, CHUNK)
            dwin = dcpad_ref[pl.ds(r0, CHUNK + CONV_PAD), :]
            vwin = vpad_ref[pl.ds(r0, CHUNK + CONV_PAD), :]
            dcur = dwin[0:CHUNK, :]
            dv = jnp.zeros((CHUNK, cb), F32)
            for t in range(CONV_W):
                fwd_off = CONV_W - 1 - t
                dv = dv + w_ref[t:t + 1, :] * dwin[fwd_off:fwd_off + CHUNK, :]
                off = CONV_PAD - (CONV_W - 1) + t
                dw_ref[t:t + 1, :] += jnp.sum(dcur * vwin[off:off + CHUNK, :], axis=0, keepdims=True)
            a = pa_ref[rows, :] + ba_ref[...]
            sg = _sigmoid(pg_ref[rows, :] + bg_ref[...])
            da = dv * sg
            dgate = dv * a * sg * (1.0 - sg)
            da_ref[rows, :] = da.astype(BF16)
            dg_ref[rows, :] = dgate.astype(BF16)
            dbdw_ref[...] += jnp.sum(dcur, axis=0, keepdims=True)
            dba_ref[...] += jnp.sum(da, axis=0, keepdims=True)
            dbg_ref[...] += jnp.sum(dgate, axis=0, keepdims=True)
            return carry

        lax.fori_loop(0, n_chunks, step, 0)

    return pl.pallas_call(
        body,
        out_shape=(jax.ShapeDtypeStruct((seq, ch), BF16), jax.ShapeDtypeStruct((seq, ch), BF16),
                   jax.ShapeDtypeStruct((CONV_PAD, ch), F32), jax.ShapeDtypeStruct((1, ch), F32),
                   jax.ShapeDtypeStruct((1, ch), F32), jax.ShapeDtypeStruct((1, ch), F32)),
        grid=(ch // cb,),
        in_specs=[_col(seq, cb), _col(seq, cb), _col(seq, cb), _col(seq, cb), _col(1, cb), _col(1, cb), _col(CONV_PAD, cb)],
        out_specs=(_col(seq, cb), _col(seq, cb), _col(CONV_PAD, cb), _col(1, cb), _col(1, cb), _col(1, cb)),
        scratch_shapes=[pltpu.VMEM((seq + CONV_PAD, cb), F32), pltpu.VMEM((seq + CONV_PAD, cb), F32)],
        compiler_params=_params(("parallel",)), name="conv_glu_bwd",
    )(dc, v, pa, pg, ba, bg, w_dw)


def _ffn_conv(pad_ref, r0, w_ref, b_ref, cb):
    win = pad_ref[pl.ds(r0, CHUNK + FFN_PAD), :]
    y = jnp.broadcast_to(b_ref[...], (CHUNK, cb))
    for t in range(FFN_CONV_W):
        off = FFN_PAD - (FFN_CONV_W - 1) + t
        y = y + w_ref[t:t + 1, :] * win[off:off + CHUNK, :]
    return y, win


def _ffn_mid_fwd(ug, uv, wg, wv, bg, bv):
    seq, ch = ug.shape
    cb = _tile(ch, 256)
    n_chunks = seq // CHUNK

    def body(ug_ref, uv_ref, wg_ref, wv_ref, bg_ref, bv_ref, z_ref, gpad_ref, vpad_ref):
        gpad_ref[0:FFN_PAD, :] = jnp.zeros((FFN_PAD, cb), F32)
        vpad_ref[0:FFN_PAD, :] = jnp.zeros((FFN_PAD, cb), F32)

        def fill(r, carry):
            r0 = pl.multiple_of(r * CHUNK, CHUNK)
            gpad_ref[pl.ds(r0 + FFN_PAD, CHUNK), :] = ug_ref[pl.ds(r0, CHUNK), :]
            vpad_ref[pl.ds(r0 + FFN_PAD, CHUNK), :] = uv_ref[pl.ds(r0, CHUNK), :]
            return carry

        lax.fori_loop(0, n_chunks, fill, 0)

        def step(r, carry):
            r0 = pl.multiple_of(r * CHUNK, CHUNK)
            yg, _ = _ffn_conv(gpad_ref, r0, wg_ref, bg_ref, cb)
            yv, _ = _ffn_conv(vpad_ref, r0, wv_ref, bv_ref, cb)
            z_ref[pl.ds(r0, CHUNK), :] = (yg * _sigmoid(yg) * yv).astype(BF16)
            return carry

        lax.fori_loop(0, n_chunks, step, 0)

    return pl.pallas_call(
        body, out_shape=jax.ShapeDtypeStruct((seq, ch), BF16), grid=(ch // cb,),
        in_specs=[_col(seq, cb), _col(seq, cb), _col(FFN_PAD, cb), _col(FFN_PAD, cb), _col(1, cb), _col(1, cb)],
        out_specs=_col(seq, cb),
        scratch_shapes=[pltpu.VMEM((seq + FFN_PAD, cb), F32), pltpu.VMEM((seq + FFN_PAD, cb), F32)],
        compiler_params=_params(("parallel",)), name="ffn_mid_fwd",
    )(ug, uv, wg, wv, bg, bv)


def _ffn_mid_bwd(ug, uv, dz, wg, wv, bg, bv):
    seq, ch = ug.shape
    cb = _tile(ch, 256)
    n_chunks = seq // CHUNK

    def body(ug_ref, uv_ref, dz_ref, wg_ref, wv_ref, bg_ref, bv_ref, dug_ref, duv_ref, dwg_ref, dwv_ref, dbg_ref, dbv_ref,
             gpad_ref, vpad_ref, dyg_ref, dyv_ref):
        gpad_ref[0:FFN_PAD, :] = jnp.zeros((FFN_PAD, cb), F32)
        vpad_ref[0:FFN_PAD, :] = jnp.zeros((FFN_PAD, cb), F32)
        dyg_ref[seq:seq + FFN_PAD, :] = jnp.zeros((FFN_PAD, cb), F32)
        dyv_ref[seq:seq + FFN_PAD, :] = jnp.zeros((FFN_PAD, cb), F32)
        dwg_ref[...] = jnp.zeros((FFN_PAD, cb), F32)
        dwv_ref[...] = jnp.zeros((FFN_PAD, cb), F32)
        dbg_ref[...] = jnp.zeros((1, cb), F32)
        dbv_ref[...] = jnp.zeros((1, cb), F32)

        def fill(r, carry):
            r0 = pl.multiple_of(r * CHUNK, CHUNK)
            gpad_ref[pl.ds(r0 + FFN_PAD, CHUNK), :] = ug_ref[pl.ds(r0, CHUNK), :]
            vpad_ref[pl.ds(r0 + FFN_PAD, CHUNK), :] = uv_ref[pl.ds(r0, CHUNK), :]
            return carry

        lax.fori_loop(0, n_chunks, fill, 0)

        def grads_of_conv_out(r, carry):
            r0 = pl.multiple_of(r * CHUNK, CHUNK)
            rows = pl.ds(r0, CHUNK)
            yg, gwin = _ffn_conv(gpad_ref, r0, wg_ref, bg_ref, cb)
            yv, vwin = _ffn_conv(vpad_ref, r0, wv_ref, bv_ref, cb)
            dzv = dz_ref[rows, :]
            sg = _sigmoid(yg)
            dyg = dzv * yv * (sg * (1.0 + yg * (1.0 - sg)))
            dyv = dzv * yg * sg
            dyg_ref[rows, :] = dyg
            dyv_ref[rows, :] = dyv
            for t in range(FFN_CONV_W):
                off = FFN_PAD - (FFN_CONV_W - 1) + t
                dwg_ref[t:t + 1, :] += jnp.sum(dyg * gwin[off:off + CHUNK, :], axis=0, keepdims=True)
                dwv_ref[t:t + 1, :] += jnp.sum(dyv * vwin[off:off + CHUNK, :], axis=0, keepdims=True)
            dbg_ref[...] += jnp.sum(dyg, axis=0, keepdims=True)
            dbv_ref[...] += jnp.sum(dyv, axis=0, keepdims=True)
            return carry

        lax.fori_loop(0, n_chunks, grads_of_conv_out, 0)

        def grads_of_conv_in(r, carry):
            r0 = pl.multiple_of(r * CHUNK, CHUNK)
            gwin = dyg_ref[pl.ds(r0, CHUNK + FFN_PAD), :]
            vwin = dyv_ref[pl.ds(r0, CHUNK + FFN_PAD), :]
            dug = jnp.zeros((CHUNK, cb), F32)
            duv = jnp.zeros((CHUNK, cb), F32)
            for t in range(FFN_CONV_W):
                off = FFN_CONV_W - 1 - t
                dug = dug + wg_ref[t:t + 1, :] * gwin[off:off + CHUNK, :]
                duv = duv + wv_ref[t:t + 1, :] * vwin[off:off + CHUNK, :]
            dug_ref[pl.ds(r0, CHUNK), :] = dug.astype(BF16)
            duv_ref[pl.ds(r0, CHUNK), :] = duv.astype(BF16)
            return carry

        lax.fori_loop(0, n_chunks, grads_of_conv_in, 0)

    return pl.pallas_call(
        body,
        out_shape=(jax.ShapeDtypeStruct((seq, ch), BF16), jax.ShapeDtypeStruct((seq, ch), BF16),
                   jax.ShapeDtypeStruct((FFN_PAD, ch), F32), jax.ShapeDtypeStruct((FFN_PAD, ch), F32),
                   jax.ShapeDtypeStruct((1, ch), F32), jax.ShapeDtypeStruct((1, ch), F32)),
        grid=(ch // cb,),
        in_specs=[_col(seq, cb), _col(seq, cb), _col(seq, cb), _col(FFN_PAD, cb), _col(FFN_PAD, cb), _col(1, cb), _col(1, cb)],
        out_specs=(_col(seq, cb), _col(seq, cb), _col(FFN_PAD, cb), _col(FFN_PAD, cb), _col(1, cb), _col(1, cb)),
        scratch_shapes=[pltpu.VMEM((seq + FFN_PAD, cb), F32) for _ in range(4)],
        compiler_params=_params(("parallel",)), name="ffn_mid_bwd",
    )(ug, uv, dz, wg, wv, bg, bv)


def _dot(a, b, mode):
    return lax.dot_general(a.astype(BF16), b.astype(BF16), _DIMS[mode], preferred_element_type=F32)


def _head(h):
    return slice(h * HEAD_DIM, (h + 1) * HEAD_DIM)


def _mem_softmax(q, k):
    s = _dot(q, k, "nt") * ATT_SCALE
    e = jnp.exp(s - jnp.max(s, axis=-1, keepdims=True))
    return e / jnp.sum(e, axis=-1, keepdims=True)


def _memattn_fwd(pq, mk, mv):
    seq, w = pq.shape
    m = mk.shape[0]

    def body(q_ref, k_ref, v_ref, o_ref):
        for h in range(N_MEM_HEADS):
            p = _mem_softmax(q_ref[:, _head(h)], k_ref[:, _head(h)])
            o_ref[:, _head(h)] = _dot(p, v_ref[:, _head(h)], "nn").astype(BF16)

    full = pl.BlockSpec((m, w), lambda i: (0, 0))
    return pl.pallas_call(
        body, out_shape=jax.ShapeDtypeStruct((seq, w), BF16), grid=(seq // ROW_TILE,),
        in_specs=[_row(w), full, full], out_specs=_row(w), compiler_params=_params(("parallel",)), name="memattn_fwd",
    )(pq, mk, mv)


def _memattn_bwd(pq, mk, mv, dmo):
    seq, w = pq.shape
    m = mk.shape[0]

    def body(q_ref, k_ref, v_ref, do_ref, dq_ref, dk_ref, dv_ref):
        step = pl.program_id(0)

        @pl.when(step == 0)
        def _():
            dk_ref[...] = jnp.zeros((m, w), F32)
            dv_ref[...] = jnp.zeros((m, w), F32)

        for h in range(N_MEM_HEADS):
            q, k, v, do = q_ref[:, _head(h)], k_ref[:, _head(h)], v_ref[:, _head(h)], do_ref[:, _head(h)]
            p = _mem_softmax(q, k)
            dp = _dot(do, v, "nt")
            ds = p * (dp - jnp.sum(dp * p, axis=-1, keepdims=True))
            dq_ref[:, _head(h)] = (_dot(ds, k, "nn") * ATT_SCALE).astype(BF16)
            dk_ref[:, _head(h)] += _dot(ds, q, "tn") * ATT_SCALE
            dv_ref[:, _head(h)] += _dot(p, do, "tn")

    full = pl.BlockSpec((m, w), lambda i: (0, 0))
    return pl.pallas_call(
        body,
        out_shape=(jax.ShapeDtypeStruct((seq, w), BF16), jax.ShapeDtypeStruct((m, w), F32), jax.ShapeDtypeStruct((m, w), F32)),
        grid=(seq // ROW_TILE,), in_specs=[_row(w), full, full, _row(w)], out_specs=(_row(w), full, full),
        compiler_params=_params(("arbitrary",)), name="memattn_bwd",
    )(pq, mk, mv, dmo)


KEY_TILE = 128


def _causal_t(s_t, q_blk, k_blk):
    kpos = k_blk * KEY_TILE + lax.broadcasted_iota(jnp.int32, (KEY_TILE, ROW_TILE), 0)
    qpos = q_blk * ROW_TILE + lax.broadcasted_iota(jnp.int32, (KEY_TILE, ROW_TILE), 1)
    return jnp.where(kpos <= qpos, s_t, NEG_BIG)


def _fox_fwd(q, k, v, cum_cols, cum_rows):
    seq, w = q.shape
    nq = seq // ROW_TILE

    def body(q_ref, k_ref, v_ref, cc_ref, cr_ref, o_ref, ob_ref, lse_ref):
        i = pl.program_id(1)
        qs = [q_ref[:, _head(h)].astype(BF16) for h in range(2)]
        crs = [cr_ref[0, 0, h:h + 1, :] for h in range(2)]

        def step(j, carry):
            rows = pl.ds(pl.multiple_of(j * KEY_TILE, KEY_TILE), KEY_TILE)
            new = []
            for h in range(2):
                m_run, l_run, acc = carry[h]
                s_t = _dot(k_ref[rows, _head(h)], qs[h], "nt") * ATT_SCALE + crs[h] - cc_ref[0, rows, h * HEAD_DIM:h * HEAD_DIM + 1]
                s_t = _causal_t(s_t, i, j)
                m_new = jnp.maximum(m_run, jnp.max(s_t, axis=0, keepdims=True))
                alpha = jnp.exp(m_run - m_new)
                p_t = jnp.exp(s_t - m_new)
                l_new = alpha * l_run + jnp.sum(p_t, axis=0, keepdims=True)
                new.append((m_new, l_new, alpha * acc + _dot(v_ref[rows, _head(h)], p_t, "tn")))
            return tuple(new)

        one_head = (jnp.full((1, ROW_TILE), NEG_BIG, F32), jnp.zeros((1, ROW_TILE), F32), jnp.zeros((HEAD_DIM, ROW_TILE), F32))
        res = lax.fori_loop(0, 2 * i + 2, step, (one_head, one_head))
        o = jnp.concatenate([acc / l_run for _, l_run, acc in res], axis=0).T
        o_ref[...] = o
        ob_ref[...] = o.astype(BF16)
        lse_ref[...] = jnp.zeros((1, 1, 8, ROW_TILE), F32)
        for h in range(2):
            lse_ref[0, 0, h:h + 1, :] = res[h][0] + jnp.log(res[h][1])

    blk = pl.BlockSpec((ROW_TILE, LANE), lambda hp, i: (i, hp))
    full = pl.BlockSpec((seq, LANE), lambda hp, i: (0, hp))
    cols = pl.BlockSpec((1, seq, LANE), lambda hp, i: (hp, 0, 0))
    rows = pl.BlockSpec((1, 1, 8, ROW_TILE), lambda hp, i: (hp, i, 0, 0))
    return pl.pallas_call(
        body,
        out_shape=(jax.ShapeDtypeStruct((seq, w), F32), jax.ShapeDtypeStruct((seq, w), BF16),
                   jax.ShapeDtypeStruct((N_HEAD_PAIRS, nq, 8, ROW_TILE), F32)),
        grid=(N_HEAD_PAIRS, nq), in_specs=[blk, full, full, cols, rows], out_specs=(blk, blk, rows),
        compiler_params=_params(("parallel", "parallel")), name="fox_fwd",
    )(q, k, v, cum_cols, cum_rows)


def _fox_bwd(q, k, v, cum_cols, cum_rows, o, lse, do):
    seq, w = q.shape
    nq = seq // ROW_TILE
    nk = seq // KEY_TILE

    def body(q_ref, k_ref, v_ref, cc_ref, cr_ref, o_ref, lse_ref, do_ref, dq_ref, dk_ref, dv_ref, dcc_ref, dcr_ref):
        kj = pl.program_id(1)

        @pl.when(kj == 0)
        def _():
            dq_ref[...] = jnp.zeros((seq, LANE), F32)
            dcr_ref[...] = jnp.zeros((1, nq, 8, ROW_TILE), F32)

        ks = [k_ref[:, _head(h)].astype(BF16) for h in range(2)]
        vs = [v_ref[:, _head(h)].astype(BF16) for h in range(2)]
        ccs = [cc_ref[0, :, h * HEAD_DIM:h * HEAD_DIM + 1] for h in range(2)]
        ones = jnp.ones((8, HEAD_DIM), BF16)

        def step(i, carry):
            rows = pl.ds(pl.multiple_of(i * ROW_TILE, ROW_TILE), ROW_TILE)
            new = []
            for h in range(2):
                dk, dv, fold = carry[h]
                qh = q_ref[rows, _head(h)].astype(BF16)
                doh = do_ref[rows, _head(h)]
                s_t = _dot(ks[h], qh, "nt") * ATT_SCALE + cr_ref[0, i, h:h + 1, :] - ccs[h]
                p_t = jnp.exp(_causal_t(s_t, i, kj) - lse_ref[0, i, h:h + 1, :])
                dp_t = _dot(vs[h], doh, "nt")
                hi, mid, lo = _split3(doh * o_ref[rows, _head(h)])
                row_sum = lambda part: lax.dot_general(ones, part, _DIMS["nt"], preferred_element_type=F32)
                delta = ((row_sum(lo) + row_sum(mid)) + row_sum(hi))[0:1, :]
                ds_t = p_t * (dp_t - delta)
                dq_ref[rows, _head(h)] += _dot(ds_t, ks[h], "tn") * ATT_SCALE
                dcr_ref[0, i, h:h + 1, :] += jnp.sum(ds_t, axis=0, keepdims=True)
                new.append((dk + _dot(ds_t, qh, "nn") * ATT_SCALE, dv + _dot(p_t, doh, "nn"), fold + (ds_t[:, :LANE] + ds_t[:, LANE:])))
            return tuple(new)

        one_head = (jnp.zeros((KEY_TILE, HEAD_DIM), F32), jnp.zeros((KEY_TILE, HEAD_DIM), F32), jnp.zeros((KEY_TILE, LANE), F32))
        res = lax.fori_loop(kj // 2, nq, step, (one_head, one_head))
        for h in range(2):
            dk_ref[:, _head(h)] = res[h][0]
            dv_ref[:, _head(h)] = res[h][1]
            dcc_ref[0, :, _head(h)] = jnp.broadcast_to(-jnp.sum(res[h][2], axis=-1, keepdims=True), (KEY_TILE, HEAD_DIM))

    blk = pl.BlockSpec((KEY_TILE, LANE), lambda hp, j: (j, hp))
    full = pl.BlockSpec((seq, LANE), lambda hp, j: (0, hp))
    cols = pl.BlockSpec((1, KEY_TILE, LANE), lambda hp, j: (hp, j, 0))
    rows = pl.BlockSpec((1, nq, 8, ROW_TILE), lambda hp, j: (hp, 0, 0, 0))
    return pl.pallas_call(
        body,
        out_shape=(jax.ShapeDtypeStruct((seq, w), F32), jax.ShapeDtypeStruct((seq, w), F32), jax.ShapeDtypeStruct((seq, w), F32),
                   jax.ShapeDtypeStruct((N_HEAD_PAIRS, seq, LANE), F32), jax.ShapeDtypeStruct((N_HEAD_PAIRS, nq, 8, ROW_TILE), F32)),
        grid=(N_HEAD_PAIRS, nk), in_specs=[full, blk, blk, cols, rows, full, rows, full],
        out_specs=(full, blk, blk, cols, rows),
        compiler_params=_params(("parallel", "arbitrary")), name="fox_bwd",
    )(q, k, v, cum_cols, cum_rows, o, lse, do)


def _split3(x):
    hi = x.astype(BF16)
    r1 = x - hi.astype(F32)
    mid = r1.astype(BF16)
    lo = (r1 - mid.astype(F32)).astype(BF16)
    return hi, mid, lo


def _tri_sum(tri, x):
    hi, mid, lo = _split3(x)
    dot = lambda p: lax.dot_general(tri, p, _DIMS["nn"], preferred_element_type=F32)
    return (dot(lo) + dot(mid)) + dot(hi)


def _tri(lower):
    r = lax.broadcasted_iota(jnp.int32, (LANE, LANE), 0)
    c = lax.broadcasted_iota(jnp.int32, (LANE, LANE), 1)
    return jnp.where((c <= r) if lower else (c >= r), 1.0, 0.0).astype(BF16)


def _fgate_fwd(fl, bf):
    seq, w = fl.shape
    nb = seq // LANE

    def body(f_ref, b_ref, cum_ref):
        tri = _tri(True)

        def step(i, carry):
            rows = pl.ds(pl.multiple_of(i * LANE, LANE), LANE)
            x = f_ref[rows, :] + b_ref[...]
            logsig = jnp.minimum(x, 0.0) - jnp.log(1.0 + jnp.exp(-jnp.abs(x)))
            cum = _tri_sum(tri, logsig) + carry
            cum_ref[rows, :] = cum
            return cum[LANE - 1:LANE, :]

        lax.fori_loop(0, nb, step, jnp.zeros((1, w), F32))

    return pl.pallas_call(body, out_shape=jax.ShapeDtypeStruct((seq, w), F32), compiler_params=_params(), name="fgate_fwd")(fl, bf)


def _fgate_bwd(fl, bf, dcum):
    seq, w = fl.shape
    nb = seq // LANE

    def body(f_ref, b_ref, dc_ref, df_ref, db_ref):
        tri = _tri(False)

        def step(i, carry):
            tail, db = carry
            rows = pl.ds(pl.multiple_of((nb - 1 - i) * LANE, LANE), LANE)
            suffix = _tri_sum(tri, dc_ref[rows, :]) + tail
            df = suffix * (1.0 - _sigmoid(f_ref[rows, :] + b_ref[...]))
            df_ref[rows, :] = df
            return suffix[0:1, :], db + jnp.sum(df, axis=0, keepdims=True)

        _, db = lax.fori_loop(0, nb, step, (jnp.zeros((1, w), F32), jnp.zeros((1, w), F32)))
        db_ref[...] = db

    return pl.pallas_call(
        body, out_shape=(jax.ShapeDtypeStruct((seq, w), F32), jax.ShapeDtypeStruct((1, w), F32)),
        compiler_params=_params(), name="fgate_bwd",
    )(fl, bf, dcum)


def _cum_layouts(cum):
    seq = cum.shape[0]
    nblk = seq // ROW_TILE
    heads = cum[:, :N_FOX_HEADS]
    cq = jnp.repeat(heads, HEAD_DIM, axis=1).reshape(seq, N_HEAD_PAIRS, LANE).transpose(1, 0, 2)
    ck = heads.T.reshape(N_HEAD_PAIRS, 2, nblk, ROW_TILE).transpose(0, 2, 1, 3)
    ck = jnp.pad(ck, ((0, 0), (0, 0), (0, 6), (0, 0)))
    return cq, ck


def _dcum_from_layouts(dcq, dck):
    seq = dcq.shape[1]
    key_side = dck[:, :, :2, :].transpose(0, 2, 1, 3).reshape(N_FOX_HEADS, seq).T
    query_side = dcq[:, :, ::HEAD_DIM].transpose(1, 0, 2).reshape(seq, N_FOX_HEADS)
    return jnp.pad(key_side + query_side, ((0, 0), (0, F_PAD - N_FOX_HEADS)))


def _local_step(x, mem, target, w):
    vec = lambda a: a.reshape(1, -1)
    mem_n = _rms_fwd(mem, vec(w["g_mem"]))
    saved = []
    shared = None
    for l in range(N_LAYERS):
        s = {"x_in": x}
        s["h"] = h = _rms_fwd(x, vec(w["g_mix"][l]))
        s["mk"] = mk = _mm(mem_n, w["w_mk"][l])
        s["mv"] = mv = _mm(mem_n, w["w_mv"][l])
        if l < N_A:
            s["pa"] = pa = _mm(h, w["w_a"][l])
            s["pg"] = pg = _mm(h, w["w_g"][l])
            s["pq"] = pq = _mm(h, w["w_qm"][l])
            s["v"], s["c"] = _conv_glu_fwd(pa, pg, vec(w["b_glu_a"][l]), vec(w["b_glu_g"][l]), w["w_dw_a"][l], vec(w["b_dw_a"][l]))
            s["mix"] = mix = _ln_silu_fwd(s["c"], vec(w["ln_g"][l]), vec(w["ln_b"][l]))
        else:
            if l == N_A:
                shared = {"x_in": x}
                shared["hk"] = hk = _rms_fwd(x, vec(w["g_kv"]))
                shared["k"] = _mm(hk, w["w_k"])
                shared["v"] = _mm(hk, w["w_v"])
                shared["fl"] = _mm(hk, w["w_f"])
                cum = _fgate_fwd(shared["fl"], w["b_f"])
                shared["cq"], shared["ck"] = _cum_layouts(cum)
            s["q"] = q = _mm(h, w["w_q"][l - N_A])
            s["pq"] = pq = _mm(h, w["w_qm"][l])
            s["o"], mix, s["lse"] = _fox_fwd(q, shared["k"], shared["v"], shared["cq"], shared["ck"])
            s["mix"] = mix
        s["mo"] = mo = _memattn_fwd(pq, mk, mv)
        x = _mm(mix, w["w_o_mix"][l], add=x)
        s["x_mid"] = x = _mm(mo, w["w_o_mem"][l], add=x)
        s["hf"] = hf = _rms_fwd(x, vec(w["g_ffn"][l]))
        s["ug"] = ug = _mm(hf, w["w_up_g"][l])
        s["uv"] = uv = _mm(hf, w["w_up_v"][l])
        s["z"] = z = _ffn_mid_fwd(ug, uv, w["w_dw_f_g"][l], w["w_dw_f_v"][l], vec(w["b_dw_f_g"][l]), vec(w["b_dw_f_v"][l]))
        x = _mm(z, w["w_down"][l], add=x)
        saved.append(s)

    loss_row, dx, dxb, dg_final = _loss_head(x, vec(w["g_final"]), target)
    loss = loss_row[0, 0]

    g = {n: [None] * N_LAYERS for n in ("g_mix", "w_mem_kv", "w_out", "g_ffn", "w_up", "w_dw_f", "b_dw_f", "w_down")}
    for n in ("w_in_a", "b_glu", "w_dw_a", "b_dw_a", "ln_g", "ln_b", "w_in_b"):
        g[n] = [None] * N_A
    one = lambda arr: ([arr], 0)
    g["g_final"] = [one(dg_final[0])]
    dmem_n = None
    dk_sum = dv_sum = dcq_sum = dck_sum = None
    for l in reversed(range(N_LAYERS)):
        s = saved[l]
        dz = _mm(dxb, w["w_down"][l], "nt")
        g["w_down"][l] = one(_mm(s["z"], dxb, "tn", BF16))
        dug, duv, dwg, dwv, dbg, dbv = _ffn_mid_bwd(s["ug"], s["uv"], dz, w["w_dw_f_g"][l], w["w_dw_f_v"][l],
                                                    vec(w["b_dw_f_g"][l]), vec(w["b_dw_f_v"][l]))
        g["w_up"][l] = ([_mm(s["hf"], dug, "tn", BF16), _mm(s["hf"], duv, "tn", BF16)], 1)
        g["w_dw_f"][l] = ([dwg[:FFN_CONV_W], dwv[:FFN_CONV_W]], 1)
        g["b_dw_f"][l] = ([dbg[0], dbv[0]], 0)
        dhf = _mm(duv, w["w_up_v"][l], "nt", add=_mm(dug, w["w_up_g"][l], "nt"))
        dx, dxb, dg_ffn = _rms_bwd(s["x_mid"], vec(w["g_ffn"][l]), dhf, dx)
        g["g_ffn"][l] = one(dg_ffn[0])
        dmix = _mm(dxb, w["w_o_mix"][l], "nt")
        dmo = _mm(dxb, w["w_o_mem"][l], "nt")
        g["w_out"][l] = ([_mm(s["mix"], dxb, "tn", BF16), _mm(s["mo"], dxb, "tn", BF16)], 0)
        dpq, dmk, dmv = _memattn_bwd(s["pq"], s["mk"], s["mv"], dmo)
        g["w_mem_kv"][l] = ([_mm(mem_n, dmk, "tn", BF16), _mm(mem_n, dmv, "tn", BF16)], 1)
        dmem_n = _mm(dmk, w["w_mk"][l], "nt", add=dmem_n)
        dmem_n = _mm(dmv, w["w_mv"][l], "nt", add=dmem_n)
        if l < N_A:
            dc, dlng, dlnb = _ln_silu_bwd(dmix, s["c"], vec(w["ln_g"][l]), vec(w["ln_b"][l]))
            da, dgate, dwdw, dbdw, dba, dbg2 = _conv_glu_bwd(dc, s["v"], s["pa"], s["pg"], vec(w["b_glu_a"][l]),
                                                              vec(w["b_glu_g"][l]), w["w_dw_a"][l])
            g["ln_g"][l], g["ln_b"][l], g["b_dw_a"][l] = one(dlng[0]), one(dlnb[0]), one(dbdw[0])
            g["w_dw_a"][l] = ([dwdw[:CONV_W]], 1)
            g["b_glu"][l] = ([dba[0], dbg2[0]], 0)
            g["w_in_a"][l] = ([_mm(s["h"], da, "tn", BF16), _mm(s["h"], dgate, "tn", BF16), _mm(s["h"], dpq, "tn", BF16)], 1)
            dh = _mm(da, w["w_a"][l], "nt")
            dh = _mm(dgate, w["w_g"][l], "nt", add=dh)
            dh = _mm(dpq, w["w_qm"][l], "nt", add=dh)
        else:
            dq, dk, dv, dcq, dck = _fox_bwd(s["q"], shared["k"], shared["v"], shared["cq"], shared["ck"], s["o"], s["lse"], dmix)
            dk_sum = dk if dk_sum is None else dk_sum + dk
            dv_sum = dv if dv_sum is None else dv_sum + dv
            dcq_sum = dcq if dcq_sum is None else dcq_sum + dcq
            dck_sum = dck if dck_sum is None else dck_sum + dck
            g["w_in_b"][l - N_A] = ([_mm(s["h"], dq, "tn", BF16), _mm(s["h"], dpq, "tn", BF16)], 1)
            dh = _mm(dq, w["w_q"][l - N_A], "nt")
            dh = _mm(dpq, w["w_qm"][l], "nt", add=dh)
        dx, dxb, dg_mix = _rms_bwd(s["x_in"], vec(w["g_mix"][l]), dh, dx)
        g["g_mix"][l] = one(dg_mix[0])
        if l == N_A:
            df, dbf = _fgate_bwd(shared["fl"], w["b_f"], _dcum_from_layouts(dcq_sum, dck_sum))
            hk = shared["hk"]
            g["w_kvf"] = [([_mm(hk, dk_sum, "tn", BF16), _mm(hk, dv_sum, "tn", BF16), _mm(hk, df, "tn", BF16)[:, :N_FOX_HEADS]], 1)]
            g["b_f"] = [one(dbf[0, :N_FOX_HEADS])]
            dhk = _mm(dk_sum, w["w_k"], "nt")
            dhk = _mm(dv_sum, w["w_v"], "nt", add=dhk)
            dhk = _mm(df, w["w_f"], "nt", add=dhk)
            dx, dxb, dg_kv = _rms_bwd(shared["x_in"], vec(w["g_kv"]), dhk, dx)
            g["g_kv"] = [one(dg_kv[0])]
    _, _, dg_mem = _rms_bwd(mem, vec(w["g_mem"]), dmem_n, None)
    g["g_mem"] = [one(dg_mem[0])]
    return loss, dx, g


SHARDED = {
    "w_in_a": ((N_A, D_MODEL, 2 * CONV_CH + MEM_W), 2),
    "b_glu": ((N_A, 2 * CONV_CH), 1),
    "w_dw_a": ((N_A, CONV_W, CONV_CH), 2),
    "b_dw_a": ((N_A, CONV_CH), 1),
    "ln_g": ((N_A, CONV_CH), 1),
    "ln_b": ((N_A, CONV_CH), 1),
    "w_kvf": ((D_MODEL, 2 * CONV_CH + N_FOX_HEADS), 1),
    "w_in_b": ((N_LAYERS - N_A, D_MODEL, CONV_CH + MEM_W), 1),
    "w_mem_kv": ((N_LAYERS, D_MODEL, 2 * MEM_W), 1),
    "w_out": ((N_LAYERS, D_MODEL, D_MODEL), 1),
    "w_up": ((N_LAYERS, D_MODEL, 2 * D_FF), 2),
    "w_dw_f": ((N_LAYERS, FFN_CONV_W, 2 * D_FF), 2),
    "w_down": ((N_LAYERS, D_FF, D_MODEL), 1),
}
REPLICATED = {
    "g_mix": (N_LAYERS, D_MODEL), "g_kv": (D_MODEL,), "b_f": (N_FOX_HEADS,), "g_mem": (D_MODEL,),
    "g_ffn": (N_LAYERS, D_MODEL), "b_dw_f": (N_LAYERS, 2 * D_FF), "g_final": (D_MODEL,),
}
WEIGHT_ORDER = ["g_mix", "w_in_a", "b_glu", "w_dw_a", "b_dw_a", "ln_g", "ln_b", "g_kv", "w_kvf", "b_f", "w_in_b", "g_mem",
                "w_mem_kv", "w_out", "g_ffn", "w_up", "w_dw_f", "b_dw_f", "w_down", "g_final"]


def _shard_shape(name):
    shape, axis = SHARDED[name]
    return tuple(d // N_CHIPS if i == axis else d for i, d in enumerate(shape))


def _size(shape):
    n = 1
    for d in shape:
        n *= d
    return n


VECTOR_LEAVES = ("b_glu", "w_dw_a", "b_dw_a", "ln_g", "ln_b", "w_dw_f")
MATRIX_LEAVES = tuple(n for n in SHARDED if n not in VECTOR_LEAVES)
_HALF_ROW_TILE = 16
_VECTOR_ELEMS = sum(_size(_shard_shape(n)) for n in VECTOR_LEAVES)
VECTOR_ROWS = -(-_VECTOR_ELEMS // (FLAT_COLS * 2 * _HALF_ROW_TILE)) * 2 * _HALF_ROW_TILE
REP_ELEMS = sum(_size(s) for s in REPLICATED.values())
REP_ROWS = 8
REP_COLS = -(-REP_ELEMS // (REP_ROWS * LANE)) * LANE


def _half_shape(name):
    shp = _shard_shape(name)
    assert shp[0] % 2 == 0 and (_size(shp) // 2 // shp[-1]) % _HALF_ROW_TILE == 0, name
    return (2, _size(shp) // 2 // shp[-1], shp[-1])


def _flatten_vectors(parts, dtype):
    flat = jnp.concatenate([p.reshape(-1).astype(dtype) for p in parts])
    return jnp.pad(flat, (0, VECTOR_ROWS * FLAT_COLS - _VECTOR_ELEMS)).reshape(2, VECTOR_ROWS // 2, FLAT_COLS)


def _unflatten_vectors(flat, lead=()):
    flat = flat.reshape(lead + (-1,))
    out, off = {}, 0
    for n in VECTOR_LEAVES:
        shp = _shard_shape(n)
        out[n] = flat[..., off:off + _size(shp)].reshape(lead + shp)
        off += _size(shp)
    return out


def _span(parts, cat_axis, sel_axis, lo, hi):
    if sel_axis != cat_axis:
        return jnp.concatenate([lax.slice_in_dim(p, lo, hi, axis=sel_axis) for p in parts], axis=cat_axis)
    taken, off = [], 0
    for p in parts:
        n = p.shape[cat_axis]
        a, b = max(lo, off), min(hi, off + n)
        if a < b:
            taken.append(lax.slice_in_dim(p, a - off, b - off, axis=cat_axis))
        off += n
    return taken[0] if len(taken) == 1 else jnp.concatenate(taken, axis=cat_axis)


def _weight_pieces(blocks, rep):
    def cut(name, layer, sel_axis, lo, hi):
        axis = SHARDED[name][1]
        parts = [blocks[name][t] if layer is None else blocks[name][t, layer] for t in range(N_CHIPS)]
        if layer is not None:
            axis -= 1
        return _span(parts, axis, sel_axis, lo, hi)

    def per_layer(name, n_layers, sel_axis, lo, hi):
        return [cut(name, l, sel_axis, lo, hi) for l in range(n_layers)]

    def pad_rows(arr, rows):
        return jnp.pad(arr, ((0, rows - arr.shape[0]), (0, 0)))

    w = dict(rep)
    w["w_a"] = per_layer("w_in_a", N_A, 1, 0, CONV_CH)
    w["w_g"] = per_layer("w_in_a", N_A, 1, CONV_CH, 2 * CONV_CH)
    w["w_q"] = per_layer("w_in_b", N_LAYERS - N_A, 1, 0, CONV_CH)
    w["w_qm"] = per_layer("w_in_a", N_A, 1, 2 * CONV_CH, 2 * CONV_CH + MEM_W) + per_layer("w_in_b", N_LAYERS - N_A, 1, CONV_CH, CONV_CH + MEM_W)
    w["b_glu_a"] = per_layer("b_glu", N_A, 0, 0, CONV_CH)
    w["b_glu_g"] = per_layer("b_glu", N_A, 0, CONV_CH, 2 * CONV_CH)
    w["w_dw_a"] = [pad_rows(p, CONV_PAD) for p in per_layer("w_dw_a", N_A, 1, 0, CONV_CH)]
    for n in ("b_dw_a", "ln_g", "ln_b"):
        w[n] = per_layer(n, N_A, 0, 0, CONV_CH)
    w["w_k"] = cut("w_kvf", None, 1, 0, CONV_CH)
    w["w_v"] = cut("w_kvf", None, 1, CONV_CH, 2 * CONV_CH)
    w["w_f"] = jnp.pad(cut("w_kvf", None, 1, 2 * CONV_CH, 2 * CONV_CH + N_FOX_HEADS), ((0, 0), (0, F_PAD - N_FOX_HEADS)))
    w["b_f"] = jnp.pad(rep["b_f"], (0, F_PAD - N_FOX_HEADS)).reshape(1, F_PAD)
    w["w_mk"] = per_layer("w_mem_kv", N_LAYERS, 1, 0, MEM_W)
    w["w_mv"] = per_layer("w_mem_kv", N_LAYERS, 1, MEM_W, 2 * MEM_W)
    w["w_o_mix"] = per_layer("w_out", N_LAYERS, 0, 0, CONV_CH)
    w["w_o_mem"] = per_layer("w_out", N_LAYERS, 0, CONV_CH, D_MODEL)
    w["w_up_g"] = per_layer("w_up", N_LAYERS, 1, 0, D_FF)
    w["w_up_v"] = per_layer("w_up", N_LAYERS, 1, D_FF, 2 * D_FF)
    w["w_dw_f_g"] = [pad_rows(p, FFN_PAD) for p in per_layer("w_dw_f", N_LAYERS, 1, 0, D_FF)]
    w["w_dw_f_v"] = [pad_rows(p, FFN_PAD) for p in per_layer("w_dw_f", N_LAYERS, 1, D_FF, 2 * D_FF)]
    w["b_dw_f_g"], w["b_dw_f_v"] = rep["b_dw_f"][:, :D_FF], rep["b_dw_f"][:, D_FF:]
    w["w_down"] = per_layer("w_down", N_LAYERS, 0, 0, D_FF)
    return w


def _shard_of(entries, name, t):
    shape, axis = SHARDED[name]
    stacked = len(shape) - len(entries[0][0][0].shape)
    step = shape[axis] // N_CHIPS
    layers = [_span(pieces, cat_axis, axis - stacked, t * step, (t + 1) * step) for pieces, cat_axis in entries]
    return jnp.stack(layers) if stacked else layers[0]


def _grad_slabs(grads):
    out = [jnp.stack([_shard_of(grads[n], n, t).astype(BF16) for t in range(N_CHIPS)]).reshape((N_CHIPS,) + _half_shape(n))
           for n in MATRIX_LEAVES]
    out.append(jnp.stack([_flatten_vectors([_shard_of(grads[n], n, t) for n in VECTOR_LEAVES], BF16) for t in range(N_CHIPS)]))
    return out


def _replicated_rows(grads):
    parts = [p.reshape(-1) for n in REPLICATED for pieces, _ in grads[n] for p in pieces]
    flat = jnp.concatenate(parts)
    return jnp.pad(flat, (0, REP_ROWS * REP_COLS - REP_ELEMS)).reshape(REP_ROWS, REP_COLS)


_ANY = pl.BlockSpec(memory_space=pl.ANY)


def _place():
    x, y, c = lax.axis_index("x"), lax.axis_index("y"), lax.axis_index("c")
    chips = [(1 - x, y), (x, 1 - y), (1 - x, 1 - y)]
    return x, y, c, chips


def _chip_index(chip):
    return 2 * chip[0] + chip[1]


def _gather_leaves(halves):
    n = len(halves)

    def body(*refs):
        w_refs, out_refs, (send_sems, recv_sems) = refs[:n], refs[n:2 * n], refs[2 * n:]
        x, y, c, chips = _place()
        me = 2 * x + y
        sibling = (x, y, 1 - c)

        def copy(a, sem, chip_idx, half, to, src=None):
            dst = out_refs[a].at[chip_idx, half]
            return pltpu.make_async_remote_copy(src_ref=dst if src is None else src, dst_ref=dst, send_sem=send_sems.at[6 * a + sem],
                                                recv_sem=recv_sems.at[6 * a + sem], device_id=to, device_id_type=MESH_ID)

        first = [copy(a, j, me, c, (*chip, c), src=w_refs[a].at[c]) for j, chip in enumerate(chips) for a in range(n)]
        for cp in first:
            cp.start()
        passed = []
        for j, chip in enumerate(chips):
            for a in range(n):
                copy(a, j, _chip_index(chip), c, (x, y, c)).wait_recv()
                passed.append(copy(a, 3 + j, _chip_index(chip), c, sibling))
                passed[-1].start()
        for j, chip in enumerate(chips):
            for a in range(n):
                copy(a, 3 + j, _chip_index(chip), 1 - c, (x, y, c)).wait_recv()
        for cp in first + passed:
            cp.wait_send()

    return pl.pallas_call(
        body, out_shape=[jax.ShapeDtypeStruct((N_CHIPS,) + h.shape, h.dtype) for h in halves],
        in_specs=[_ANY] * n, out_specs=[_ANY] * n,
        scratch_shapes=[pltpu.SemaphoreType.DMA((6 * n,)), pltpu.SemaphoreType.DMA((6 * n,))], name="gather_leaves",
    )(*halves)


def _swap_halves(slabs):
    n = len(slabs)

    def body(*refs):
        g_refs, got_refs, (send_sem, recv_sem) = refs[:n], refs[n:2 * n], refs[2 * n:]
        x, y, c, _ = _place()
        copies = [pltpu.make_async_remote_copy(src_ref=g_refs[a].at[t, 1 - c], dst_ref=got_refs[a].at[t], send_sem=send_sem.at[N_CHIPS * a + t],
                                               recv_sem=recv_sem.at[N_CHIPS * a + t], device_id=(x, y, 1 - c), device_id_type=MESH_ID)
                  for a in range(n) for t in range(N_CHIPS)]
        for cp in copies:
            cp.start()
        for cp in copies:
            cp.wait()

    return pl.pallas_call(
        body, out_shape=[jax.ShapeDtypeStruct((g.shape[0],) + g.shape[2:], g.dtype) for g in slabs],
        in_specs=[_ANY] * n, out_specs=[_ANY] * n,
        scratch_shapes=[pltpu.SemaphoreType.DMA((N_CHIPS * n,)), pltpu.SemaphoreType.DMA((N_CHIPS * n,))], name="swap_halves",
    )(*slabs)


def _scatter_to_chips(parts):
    n = len(parts)

    def body(*refs):
        p_refs, got_refs, (send_sems, recv_sems) = refs[:n], refs[n:2 * n], refs[2 * n:]
        x, y, c, chips = _place()
        me = 2 * x + y
        sends = [pltpu.make_async_remote_copy(src_ref=p_refs[a].at[_chip_index(chip)], dst_ref=got_refs[a].at[me],
                                              send_sem=send_sems.at[3 * a + j], recv_sem=recv_sems.at[3 * a + j],
                                              device_id=(*chip, c), device_id_type=MESH_ID)
                 for j, chip in enumerate(chips) for a in range(n)]
        for cp in sends:
            cp.start()
        for j, chip in enumerate(chips):
            for a in range(n):
                landed = got_refs[a].at[_chip_index(chip)]
                pltpu.make_async_remote_copy(src_ref=landed, dst_ref=landed, send_sem=send_sems.at[3 * a + j], recv_sem=recv_sems.at[3 * a + j],
                                             device_id=(x, y, c), device_id_type=MESH_ID).wait_recv()
        for cp in sends:
            cp.wait_send()

    return pl.pallas_call(
        body, out_shape=[jax.ShapeDtypeStruct(p.shape, p.dtype) for p in parts], in_specs=[_ANY] * n, out_specs=[_ANY] * n,
        scratch_shapes=[pltpu.SemaphoreType.DMA((3 * n,)), pltpu.SemaphoreType.DMA((3 * n,))], name="scatter_to_chips",
    )(*parts)


def _sibling_halves(halves):
    n = len(halves)

    def body(*refs):
        h_refs, got_refs, (send_sem, recv_sem) = refs[:n], refs[n:2 * n], refs[2 * n:]
        x, y, c, _ = _place()
        copies = [pltpu.make_async_remote_copy(src_ref=h_refs[a], dst_ref=got_refs[a], send_sem=send_sem.at[a], recv_sem=recv_sem.at[a],
                                               device_id=(x, y, 1 - c), device_id_type=MESH_ID) for a in range(n)]
        for cp in copies:
            cp.start()
        for cp in copies:
            cp.wait()

    return pl.pallas_call(
        body, out_shape=[jax.ShapeDtypeStruct(h.shape, h.dtype) for h in halves], in_specs=[_ANY] * n, out_specs=[_ANY] * n,
        scratch_shapes=[pltpu.SemaphoreType.DMA((n,)), pltpu.SemaphoreType.DMA((n,))], name="sibling_halves",
    )(*halves)


def _gather_replicated(rows):
    m_per, n = rows.shape

    def body(x_ref, out_ref, send_sems, recv_sems, local_sem):
        x, y, c, chips = _place()
        me, sibling = (x, y, c), (x, y, 1 - c)

        def block(px, py, pc):
            return out_ref.at[pl.ds((4 * px + 2 * py + pc) * m_per, m_per), :]

        def copy(k, blk, to, src=None):
            return pltpu.make_async_remote_copy(src_ref=block(*blk) if src is None else src, dst_ref=block(*blk),
                                                send_sem=send_sems.at[k], recv_sem=recv_sems.at[k], device_id=to, device_id_type=MESH_ID)

        mine = pltpu.make_async_copy(x_ref, block(*me), local_sem)
        mine.start()
        first = [copy(0, me, sibling, src=x_ref)]
        first += [copy(1 + j, me, (*chip, c), src=x_ref) for j, chip in enumerate(chips)]
        for cp in first:
            cp.start()
        passed = [copy(4 + j, (*chip, c), sibling) for j, chip in enumerate(chips)]
        for j, chip in enumerate(chips):
            copy(1 + j, (*chip, c), me).wait_recv()
            passed[j].start()
        copy(0, sibling, me).wait_recv()
        for j, chip in enumerate(chips):
            copy(4 + j, (*chip, 1 - c), me).wait_recv()
        for cp in first + passed:
            cp.wait_send()
        mine.wait()

    vmem = pl.BlockSpec(memory_space=pltpu.VMEM)
    return pl.pallas_call(
        body, out_shape=jax.ShapeDtypeStruct((N_DEV * m_per, n), rows.dtype), in_specs=[vmem], out_specs=vmem,
        scratch_shapes=[pltpu.SemaphoreType.DMA((7,)), pltpu.SemaphoreType.DMA((7,)), pltpu.SemaphoreType.DMA],
        name="gather_replicated",
    )(rows)


_SUM_ROWS = 256


def _tile_rows(rows):
    if rows <= _SUM_ROWS:
        return rows
    best = None
    for r in range(16, _SUM_ROWS + 1, 16):
        if rows % r == 0:
            best = r
    assert best is not None, rows
    return best


def _add_pairs(slabs, got, core):
    n, _, rows, cols = slabs.shape
    tr = _tile_rows(rows)

    def body(core_ref, a_ref, b_ref, o_ref):
        o_ref[...] = (a_ref[0].astype(F32) + b_ref[...].astype(F32)).astype(BF16)

    spec = pl.BlockSpec((1, tr, cols), lambda t, i, core_ref: (t, i, 0))
    return pl.pallas_call(
        body, out_shape=jax.ShapeDtypeStruct((n, rows, cols), BF16),
        grid_spec=pltpu.PrefetchScalarGridSpec(
            num_scalar_prefetch=1, grid=(n, rows // tr),
            in_specs=[pl.BlockSpec((1, 1, tr, cols), lambda t, i, core_ref: (t, core_ref[0], i, 0)), spec], out_specs=spec),
        compiler_params=_params(("parallel", "parallel")), name=f"add_pairs_{rows}x{cols}",
    )(core, slabs, got)


def _sum_slabs(slabs):
    n, rows, cols = slabs.shape
    tr = _tile_rows(rows)

    def body(s_ref, o_ref):
        acc = s_ref[0].astype(F32)
        for t in range(1, n):
            acc = acc + s_ref[t].astype(F32)
        o_ref[...] = acc

    return pl.pallas_call(
        body, out_shape=jax.ShapeDtypeStruct((rows, cols), F32), grid=(rows // tr,),
        in_specs=[pl.BlockSpec((n, tr, cols), lambda i: (0, i, 0))], out_specs=pl.BlockSpec((tr, cols), lambda i: (i, 0)),
        compiler_params=_params(("parallel",)), name=f"sum_slabs_{n}x{rows}x{cols}",
    )(slabs)


_ADAM_BLOCK_BYTES = 2 * 1024 * 1024


def _adamw(w, g, m, v):
    shape = w.shape
    cols = shape[-1]
    rows = _size(shape) // cols
    tr = rows
    if rows * cols * 4 > _ADAM_BLOCK_BYTES:
        for r in range(8, rows, 8):
            if rows % r == 0 and r * cols * 4 <= _ADAM_BLOCK_BYTES:
                tr = r

    def body(w_ref, g_ref, m_ref, v_ref, d_ref, nm_ref, nv_ref):
        gv = g_ref[...]
        nm = ADAM_B1 * m_ref[...] + (1.0 - ADAM_B1) * gv
        nv = ADAM_B2 * v_ref[...] + (1.0 - ADAM_B2) * jnp.square(gv)
        m_hat = nm / (1.0 - ADAM_B1 ** ADAM_STEP)
        v_hat = nv / (1.0 - ADAM_B2 ** ADAM_STEP)
        d_ref[...] = -ADAM_LR * (m_hat / (jnp.sqrt(v_hat) + ADAM_EPS) + ADAM_WD * w_ref[...])
        nm_ref[...] = nm
        nv_ref[...] = nv

    spec = pl.BlockSpec((tr, cols), lambda i: (i, 0))
    out = pl.pallas_call(
        body, out_shape=tuple(jax.ShapeDtypeStruct((rows, cols), F32) for _ in range(3)), grid=(rows // tr,),
        in_specs=[spec] * 4, out_specs=(spec,) * 3, compiler_params=_params(("parallel",)), name=f"adamw_{rows}x{cols}",
    )(*(a.reshape(rows, cols) for a in (w, g, m, v)))
    return tuple(o.reshape(shape) for o in out)


def kernel(x, mem, g_mix, w_in_a, b_glu, w_dw_a, b_dw_a, ln_g, ln_b, g_kv, w_kvf, b_f, w_in_b, g_mem, w_mem_kv, w_out, g_ffn, w_up, w_dw_f, b_dw_f, w_down, g_final, loss_target, m_g_mix, m_w_in_a, m_b_glu, m_w_dw_a, m_b_dw_a, m_ln_g, m_ln_b, m_g_kv, m_w_kvf, m_b_f, m_w_in_b, m_g_mem, m_w_mem_kv, m_w_out, m_g_ffn, m_w_up, m_w_dw_f, m_b_dw_f, m_w_down, m_g_final, v_g_mix, v_w_in_a, v_b_glu, v_w_dw_a, v_b_dw_a, v_ln_g, v_ln_b, v_g_kv, v_w_kvf, v_b_f, v_w_in_b, v_g_mem, v_w_mem_kv, v_w_out, v_g_ffn, v_w_up, v_w_dw_f, v_b_dw_f, v_w_down, v_g_final):
    weights = dict(g_mix=g_mix, w_in_a=w_in_a, b_glu=b_glu, w_dw_a=w_dw_a, b_dw_a=b_dw_a, ln_g=ln_g, ln_b=ln_b, g_kv=g_kv,
                   w_kvf=w_kvf, b_f=b_f, w_in_b=w_in_b, g_mem=g_mem, w_mem_kv=w_mem_kv, w_out=w_out, g_ffn=g_ffn, w_up=w_up,
                   w_dw_f=w_dw_f, b_dw_f=b_dw_f, w_down=w_down, g_final=g_final)
    mom1 = dict(g_mix=m_g_mix, w_in_a=m_w_in_a, b_glu=m_b_glu, w_dw_a=m_w_dw_a, b_dw_a=m_b_dw_a, ln_g=m_ln_g, ln_b=m_ln_b,
                g_kv=m_g_kv, w_kvf=m_w_kvf, b_f=m_b_f, w_in_b=m_w_in_b, g_mem=m_g_mem, w_mem_kv=m_w_mem_kv, w_out=m_w_out,
                g_ffn=m_g_ffn, w_up=m_w_up, w_dw_f=m_w_dw_f, b_dw_f=m_b_dw_f, w_down=m_w_down, g_final=m_g_final)
    mom2 = dict(g_mix=v_g_mix, w_in_a=v_w_in_a, b_glu=v_b_glu, w_dw_a=v_w_dw_a, b_dw_a=v_b_dw_a, ln_g=v_ln_g, ln_b=v_ln_b,
                g_kv=v_g_kv, w_kvf=v_w_kvf, b_f=v_b_f, w_in_b=v_w_in_b, g_mem=v_g_mem, w_mem_kv=v_w_mem_kv, w_out=v_w_out,
                g_ffn=v_g_ffn, w_up=v_w_up, w_dw_f=v_w_dw_f, b_dw_f=v_b_dw_f, w_down=v_w_down, g_final=v_g_final)

    x_pos, y_pos, core = lax.axis_index("x"), lax.axis_index("y"), lax.axis_index("c")
    chip = 2 * x_pos + y_pos

    local = [weights[n].astype(BF16).reshape(_half_shape(n)) for n in MATRIX_LEAVES]
    local.append(_flatten_vectors([weights[n] for n in VECTOR_LEAVES], F32))
    gathered = [lax.dynamic_update_index_in_dim(g, own[None], chip, 0) for g, own in zip(_gather_leaves(local), local)]
    blocks = {n: g.reshape((N_CHIPS,) + _shard_shape(n)) for n, g in zip(MATRIX_LEAVES, gathered)}
    blocks.update(_unflatten_vectors(gathered[-1], (N_CHIPS,)))
    pieces = _weight_pieces(blocks, {n: weights[n] for n in REPLICATED})

    loss, dx, grads = _local_step(x[0], mem[0], loss_target[0], pieces)
    loss = lax.psum(loss, ("x", "y", "c"))

    slabs = _grad_slabs(grads)
    core_index = core.astype(jnp.int32).reshape(1)
    chip_sums = [_add_pairs(mine, got, core_index) for mine, got in zip(slabs, _swap_halves(slabs))]
    from_chips = [lax.dynamic_update_index_in_dim(got, lax.dynamic_index_in_dim(mine, chip, 0, keepdims=True), chip, 0)
                  for got, mine in zip(_scatter_to_chips(chip_sums), chip_sums)]
    mine = [_sum_slabs(f) for f in from_chips]
    theirs = _sibling_halves(mine)
    reduced = [jnp.stack([jnp.where(core == 0, m, t), jnp.where(core == 0, t, m)]) for m, t in zip(mine, theirs)]
    grad_leaves = {n: r.reshape(_shard_shape(n)) for n, r in zip(MATRIX_LEAVES, reduced)}
    grad_leaves.update(_unflatten_vectors(reduced[-1]))

    rep_sum = _sum_slabs(_gather_replicated(_replicated_rows(grads)).reshape(N_DEV, REP_ROWS, REP_COLS)).reshape(-1)
    off = 0
    for n, shp in REPLICATED.items():
        grad_leaves[n] = rep_sum[off:off + _size(shp)].reshape(shp)
        off += _size(shp)

    deltas, new_m, new_v = {}, {}, {}
    for n in WEIGHT_ORDER:
        deltas[n], new_m[n], new_v[n] = _adamw(weights[n], grad_leaves[n], mom1[n], mom2[n])
    return (loss, dx[None], *[grad_leaves[n] for n in WEIGHT_ORDER], *[deltas[n] for n in WEIGHT_ORDER],
            *[new_m[n] for n in WEIGHT_ORDER], *[new_v[n] for n in WEIGHT_ORDER])
```

```python
import jax
import jax.numpy as jnp
from jax import lax
from jax.experimental import pallas as pl
from jax.experimental.pallas import tpu as pltpu

F32 = jnp.float32
BF16 = jnp.bfloat16

D_MODEL = 1024
N_LAYERS = 4
N_A = 2
CONV_CH = 768
MEM_W = 256
HEAD_DIM = 64
N_MEM_HEADS = 4
N_FOX_HEADS = 12
N_HEAD_PAIRS = N_FOX_HEADS // 2
D_FF = 2816
CONV_W = 31
CONV_PAD = 32
FFN_CONV_W = 3
FFN_PAD = 8
F_PAD = 128
RMS_EPS = 1e-6
LN_EPS = 1e-5
ATT_SCALE = HEAD_DIM ** -0.5
NEG_BIG = -1e30

ADAM_LR = 0.001
ADAM_B1 = 0.9
ADAM_B2 = 0.999
ADAM_EPS = 1e-08
ADAM_WD = 0.01
ADAM_STEP = 10

LANE = 128
ROW_TILE = 256
CHUNK = 128
VMEM_LIMIT = 48 * 1024 * 1024
FLAT_COLS = 1024
N_CHIPS = 4
N_DEV = 8
MESH_ID = pl.DeviceIdType.MESH


def _params(sem=None):
    return pltpu.CompilerParams(dimension_semantics=sem, vmem_limit_bytes=VMEM_LIMIT)


def _tile(dim, pref):
    if dim <= pref:
        return dim
    best = None
    for m in range(1, dim // LANE + 1):
        d = m * LANE
        if dim % d == 0 and d <= pref:
            best = d
    assert best is not None, dim
    return best


_DIMS = {"nn": (((1,), (0,)), ((), ())), "nt": (((1,), (1,)), ((), ())), "tn": (((0,), (0,)), ((), ()))}


_MM_RESIDENT_BYTES = 8 * 1024 * 1024
_MM_STREAM_BYTES = 6 * 1024 * 1024
_MM_OUT_BYTES = 6 * 1024 * 1024


def _mm_tiles(m, n, k, mode, a_size, b_size, o_size):
    if mode == "tn":
        tm = _tile(m, _MM_RESIDENT_BYTES // (k * a_size))
        tn = _tile(n, min(_MM_STREAM_BYTES // (k * b_size), _MM_OUT_BYTES // (tm * o_size)))
    else:
        tn = _tile(n, _MM_RESIDENT_BYTES // (k * b_size))
        tm = _tile(m, min(_MM_STREAM_BYTES // (k * a_size), _MM_OUT_BYTES // (tn * o_size), 512))
    return tm, tn


def _mm(a, b, mode="nn", out_dtype=F32, add=None):
    if mode == "nn":
        (m, k), (k2, n) = a.shape, b.shape
    elif mode == "nt":
        (m, k), (n, k2) = a.shape, b.shape
    else:
        (k, m), (k2, n) = a.shape, b.shape
    assert k == k2, (a.shape, b.shape, mode)
    tm, tn = _mm_tiles(m, n, k, mode, a.dtype.itemsize, b.dtype.itemsize, jnp.dtype(out_dtype).itemsize)
    dims = _DIMS[mode]
    has_add = add is not None

    def body(*refs):
        if has_add:
            a_ref, b_ref, add_ref, o_ref = refs
        else:
            a_ref, b_ref, o_ref = refs
        r = lax.dot_general(a_ref[...].astype(BF16), b_ref[...].astype(BF16), dims, preferred_element_type=F32)
        if has_add:
            r = r + add_ref[...]
        o_ref[...] = r.astype(out_dtype)

    a_spec = pl.BlockSpec((k, tm), lambda i, j: (0, i)) if mode == "tn" else pl.BlockSpec((tm, k), lambda i, j: (i, 0))
    b_spec = pl.BlockSpec((tn, k), lambda i, j: (j, 0)) if mode == "nt" else pl.BlockSpec((k, tn), lambda i, j: (0, j))
    o_spec = pl.BlockSpec((tm, tn), lambda i, j: (i, j))
    in_specs = [a_spec, b_spec] + ([o_spec] if has_add else [])
    args = (a, b) + ((add,) if has_add else ())
    return pl.pallas_call(
        body,
        out_shape=jax.ShapeDtypeStruct((m, n), out_dtype),
        grid=(m // tm, n // tn),
        in_specs=in_specs,
        out_specs=o_spec,
        compiler_params=_params(("parallel", "parallel")),
        name=f"mm_{mode}_{m}x{k}x{n}",
    )(*args)


def _row(width):
    return pl.BlockSpec((ROW_TILE, width), lambda i: (i, 0))


def _vec(width):
    return pl.BlockSpec((1, width), lambda i: (0, 0))


def _rms_fwd(x, g):
    rows, d = x.shape

    def body(x_ref, g_ref, o_ref):
        xv = x_ref[...]
        rstd = lax.rsqrt(jnp.mean(xv * xv, axis=-1, keepdims=True) + RMS_EPS)
        o_ref[...] = (xv * rstd * g_ref[...]).astype(BF16)

    return pl.pallas_call(
        body, out_shape=jax.ShapeDtypeStruct((rows, d), BF16), grid=(rows // ROW_TILE,),
        in_specs=[_row(d), _vec(d)], out_specs=_row(d), compiler_params=_params(("parallel",)), name=f"rms_fwd_{rows}",
    )(x, g)


def _accumulate(ref, value, step):
    @pl.when(step == 0)
    def _():
        ref[...] = value

    @pl.when(step > 0)
    def _():
        ref[...] += value


def _rms_bwd(x, g, dh, dres):
    rows, d = x.shape
    has_res = dres is not None

    def body(*refs):
        if has_res:
            x_ref, g_ref, dh_ref, dres_ref, dx_ref, dxb_ref, dg_ref = refs
        else:
            x_ref, g_ref, dh_ref, dx_ref, dxb_ref, dg_ref = refs
        xv = x_ref[...]
        dhv = dh_ref[...]
        rstd = lax.rsqrt(jnp.mean(xv * xv, axis=-1, keepdims=True) + RMS_EPS)
        xhat = xv * rstd
        gd = dhv * g_ref[...]
        dx = rstd * (gd - xhat * jnp.mean(gd * xhat, axis=-1, keepdims=True))
        if has_res:
            dx = dx + dres_ref[...]
        dx_ref[...] = dx
        dxb_ref[...] = dx.astype(BF16)
        _accumulate(dg_ref, jnp.sum(dhv * xhat, axis=0, keepdims=True), pl.program_id(0))

    in_specs = [_row(d), _vec(d), _row(d)] + ([_row(d)] if has_res else [])
    args = (x, g, dh) + ((dres,) if has_res else ())
    return pl.pallas_call(
        body,
        out_shape=(jax.ShapeDtypeStruct((rows, d), F32), jax.ShapeDtypeStruct((rows, d), BF16), jax.ShapeDtypeStruct((1, d), F32)),
        grid=(rows // ROW_TILE,), in_specs=in_specs, out_specs=(_row(d), _row(d), _vec(d)),
        compiler_params=_params(("arbitrary",)), name=f"rms_bwd_{rows}_{int(has_res)}",
    )(*args)


def _loss_head(x, g, target):
    rows, d = x.shape

    def body(x_ref, g_ref, t_ref, loss_ref, dx_ref, dxb_ref, dg_ref):
        xv = x_ref[...]
        gv = g_ref[...]
        rstd = lax.rsqrt(jnp.mean(xv * xv, axis=-1, keepdims=True) + RMS_EPS)
        xhat = xv * rstd
        err = xhat * gv - t_ref[...]
        part = 0.5 * jnp.sum(jnp.mean(err * err, axis=-1, keepdims=True), axis=0, keepdims=True)
        dy = err * (1.0 / d)
        gd = dy * gv
        dx = rstd * (gd - xhat * jnp.mean(gd * xhat, axis=-1, keepdims=True))
        dx_ref[...] = dx
        dxb_ref[...] = dx.astype(BF16)
        step = pl.program_id(0)
        _accumulate(loss_ref, jnp.broadcast_to(part, (1, LANE)), step)
        _accumulate(dg_ref, jnp.sum(dy * xhat, axis=0, keepdims=True), step)

    return pl.pallas_call(
        body,
        out_shape=(jax.ShapeDtypeStruct((1, LANE), F32), jax.ShapeDtypeStruct((rows, d), F32),
                   jax.ShapeDtypeStruct((rows, d), BF16), jax.ShapeDtypeStruct((1, d), F32)),
        grid=(rows // ROW_TILE,), in_specs=[_row(d), _vec(d), _row(d)],
        out_specs=(_vec(LANE), _row(d), _row(d), _vec(d)),
        compiler_params=_params(("arbitrary",)), name="loss_head",
    )(x, g, target)


def _sigmoid(x):
    return 1.0 / (1.0 + jnp.exp(-x))


def _ln_silu_fwd(c, ln_g, ln_b):
    rows, ch = c.shape

    def body(c_ref, g_ref, b_ref, o_ref):
        cv = c_ref[...]
        mu = jnp.mean(cv, axis=-1, keepdims=True)
        cen = cv - mu
        rstd = lax.rsqrt(jnp.mean(cen * cen, axis=-1, keepdims=True) + LN_EPS)
        y = cen * rstd * g_ref[...] + b_ref[...]
        o_ref[...] = (y * _sigmoid(y)).astype(BF16)

    return pl.pallas_call(
        body, out_shape=jax.ShapeDtypeStruct((rows, ch), BF16), grid=(rows // ROW_TILE,),
        in_specs=[_row(ch), _vec(ch), _vec(ch)], out_specs=_row(ch), compiler_params=_params(("parallel",)), name="ln_silu_fwd",
    )(c, ln_g, ln_b)


def _ln_silu_bwd(dmix, c, ln_g, ln_b):
    rows, ch = c.shape

    def body(dm_ref, c_ref, g_ref, b_ref, dc_ref, dg_ref, db_ref):
        cv = c_ref[...]
        gv = g_ref[...]
        mu = jnp.mean(cv, axis=-1, keepdims=True)
        cen = cv - mu
        rstd = lax.rsqrt(jnp.mean(cen * cen, axis=-1, keepdims=True) + LN_EPS)
        xhat = cen * rstd
        y = xhat * gv + b_ref[...]
        sg = _sigmoid(y)
        dy = dm_ref[...] * (sg * (1.0 + y * (1.0 - sg)))
        dxh = dy * gv
        dc = rstd * (dxh - jnp.mean(dxh, axis=-1, keepdims=True) - xhat * jnp.mean(dxh * xhat, axis=-1, keepdims=True))
        dc_ref[...] = dc
        step = pl.program_id(0)
        _accumulate(dg_ref, jnp.sum(dy * xhat, axis=0, keepdims=True), step)
        _accumulate(db_ref, jnp.sum(dy, axis=0, keepdims=True), step)

    return pl.pallas_call(
        body,
        out_shape=(jax.ShapeDtypeStruct((rows, ch), F32), jax.ShapeDtypeStruct((1, ch), F32), jax.ShapeDtypeStruct((1, ch), F32)),
        grid=(rows // ROW_TILE,), in_specs=[_row(ch), _row(ch), _vec(ch), _vec(ch)],
        out_specs=(_row(ch), _vec(ch), _vec(ch)), compiler_params=_params(("arbitrary",)), name="ln_silu_bwd",
    )(dmix, c, ln_g, ln_b)


def _col(rows, cb):
    return pl.BlockSpec((rows, cb), lambda j: (0, j))


def _conv_glu_fwd(pa, pg, ba, bg, w_dw, b_dw):
    seq, ch = pa.shape
    cb = LANE
    n_chunks = seq // CHUNK

    def body(pa_ref, pg_ref, ba_ref, bg_ref, w_ref, b_ref, v_ref, c_ref, vpad_ref):
        vpad_ref[0:CONV_PAD, :] = jnp.zeros((CONV_PAD, cb), F32)

        def glu(r, carry):
            r0 = pl.multiple_of(r * CHUNK, CHUNK)
            rows = pl.ds(r0, CHUNK)
            v = (pa_ref[rows, :] + ba_ref[...]) * _sigmoid(pg_ref[rows, :] + bg_ref[...])
            v_ref[rows, :] = v
            vpad_ref[pl.ds(r0 + CONV_PAD, CHUNK), :] = v
            return carry

        lax.fori_loop(0, n_chunks, glu, 0)

        def conv(r, carry):
            r0 = pl.multiple_of(r * CHUNK, CHUNK)
            win = vpad_ref[pl.ds(r0, CHUNK + CONV_PAD), :]
            acc = jnp.broadcast_to(b_ref[...], (CHUNK, cb))
            for t in range(CONV_W):
                off = CONV_PAD - (CONV_W - 1) + t
                acc = acc + w_ref[t:t + 1, :] * win[off:off + CHUNK, :]
            c_ref[pl.ds(r0, CHUNK), :] = acc
            return carry

        lax.fori_loop(0, n_chunks, conv, 0)

    return pl.pallas_call(
        body,
        out_shape=(jax.ShapeDtypeStruct((seq, ch), F32), jax.ShapeDtypeStruct((seq, ch), F32)),
        grid=(ch // cb,),
        in_specs=[_col(seq, cb), _col(seq, cb), _col(1, cb), _col(1, cb), _col(CONV_PAD, cb), _col(1, cb)],
        out_specs=(_col(seq, cb), _col(seq, cb)),
        scratch_shapes=[pltpu.VMEM((seq + CONV_PAD, cb), F32)],
        compiler_params=_params(("parallel",)), name="conv_glu_fwd",
    )(pa, pg, ba, bg, w_dw, b_dw)


def _conv_glu_bwd(dc, v, pa, pg, ba, bg, w_dw):
    seq, ch = dc.shape
    cb = LANE
    n_chunks = seq // CHUNK

    def body(dc_ref, v_ref, pa_ref, pg_ref, ba_ref, bg_ref, w_ref, da_ref, dg_ref, dw_ref, dbdw_ref, dba_ref, dbg_ref,
             dcpad_ref, vpad_ref):
        vpad_ref[0:CONV_PAD, :] = jnp.zeros((CONV_PAD, cb), F32)
        dcpad_ref[seq:seq + CONV_PAD, :] = jnp.zeros((CONV_PAD, cb), F32)
        dw_ref[...] = jnp.zeros((CONV_PAD, cb), F32)
        dbdw_ref[...] = jnp.zeros((1, cb), F32)
        dba_ref[...] = jnp.zeros((1, cb), F32)
        dbg_ref[...] = jnp.zeros((1, cb), F32)

        def fill(r, carry):
            r0 = pl.multiple_of(r * CHUNK, CHUNK)
            vpad_ref[pl.ds(r0 + CONV_PAD, CHUNK), :] = v_ref[pl.ds(r0, CHUNK), :]
            dcpad_ref[pl.ds(r0, CHUNK), :] = dc_ref[pl.ds(r0, CHUNK), :]
            return carry

        lax.fori_loop(0, n_chunks, fill, 0)

        def step(r, carry):
            r0 = pl.multiple_of(r * CHUNK, CHUNK)
            rows = pl.ds(r0, CHUNK)
            dwin = dcpad_ref[pl.ds(r0, CHUNK + CONV_PAD), :]
            vwin = vpad_ref[pl.ds(r0, CHUNK + CONV_PAD), :]
            dcur = dwin[0:CHUNK, :]
            dv = jnp.zeros((CHUNK, cb), F32)
            for t in range(CONV_W):
                fwd_off = CONV_W - 1 - t
                dv = dv + w_ref[t:t + 1, :] * dwin[fwd_off:fwd_off + CHUNK, :]
                off = CONV_PAD - (CONV_W - 1) + t
                dw_ref[t:t + 1, :] += jnp.sum(dcur * vwin[off:off + CHUNK, :], axis=0, keepdims=True)
            a = pa_ref[rows, :] + ba_ref[...]
            sg = _sigmoid(pg_ref[rows, :] + bg_ref[...])
            da = dv * sg
            dgate = dv * a * sg * (1.0 - sg)
            da_ref[rows, :] = da.astype(BF16)
            dg_ref[rows, :] = dgate.astype(BF16)
            dbdw_ref[...] += jnp.sum(dcur, axis=0, keepdims=True)
            dba_ref[...] += jnp.sum(da, axis=0, keepdims=True)
            dbg_ref[...] += jnp.sum(dgate, axis=0, keepdims=True)
            return carry

        lax.fori_loop(0, n_chunks, step, 0)

    return pl.pallas_call(
        body,
        out_shape=(jax.ShapeDtypeStruct((seq, ch), BF16), jax.ShapeDtypeStruct((seq, ch), BF16),
                   jax.ShapeDtypeStruct((CONV_PAD, ch), F32), jax.ShapeDtypeStruct((1, ch), F32),
                   jax.ShapeDtypeStruct((1, ch), F32), jax.ShapeDtypeStruct((1, ch), F32)),
        grid=(ch // cb,),
        in_specs=[_col(seq, cb), _col(seq, cb), _col(seq, cb), _col(seq, cb), _col(1, cb), _col(1, cb), _col(CONV_PAD, cb)],
        out_specs=(_col(seq, cb), _col(seq, cb), _col(CONV_PAD, cb), _col(1, cb), _col(1, cb), _col(1, cb)),
        scratch_shapes=[pltpu.VMEM((seq + CONV_PAD, cb), F32), pltpu.VMEM((seq + CONV_PAD, cb), F32)],
        compiler_params=_params(("parallel",)), name="conv_glu_bwd",
    )(dc, v, pa, pg, ba, bg, w_dw)


def _ffn_conv(pad_ref, r0, w_ref, b_ref, cb):
    win = pad_ref[pl.ds(r0, CHUNK + FFN_PAD), :]
    y = jnp.broadcast_to(b_ref[...], (CHUNK, cb))
    for t in range(FFN_CONV_W):
        off = FFN_PAD - (FFN_CONV_W - 1) + t
        y = y + w_ref[t:t + 1, :] * win[off:off + CHUNK, :]
    return y, win


def _ffn_mid_fwd(ug, uv, wg, wv, bg, bv):
    seq, ch = ug.shape
    cb = _tile(ch, 256)
    n_chunks = seq // CHUNK

    def body(ug_ref, uv_ref, wg_ref, wv_ref, bg_ref, bv_ref, z_ref, gpad_ref, vpad_ref):
        gpad_ref[0:FFN_PAD, :] = jnp.zeros((FFN_PAD, cb), F32)
        vpad_ref[0:FFN_PAD, :] = jnp.zeros((FFN_PAD, cb), F32)

        def fill(r, carry):
            r0 = pl.multiple_of(r * CHUNK, CHUNK)
            gpad_ref[pl.ds(r0 + FFN_PAD, CHUNK), :] = ug_ref[pl.ds(r0, CHUNK), :]
            vpad_ref[pl.ds(r0 + FFN_PAD, CHUNK), :] = uv_ref[pl.ds(r0, CHUNK), :]
            return carry

        lax.fori_loop(0, n_chunks, fill, 0)

        def step(r, carry):
            r0 = pl.multiple_of(r * CHUNK, CHUNK)
            yg, _ = _ffn_conv(gpad_ref, r0, wg_ref, bg_ref, cb)
            yv, _ = _ffn_conv(vpad_ref, r0, wv_ref, bv_ref, cb)
            z_ref[pl.ds(r0, CHUNK), :] = (yg * _sigmoid(yg) * yv).astype(BF16)
            return carry

        lax.fori_loop(0, n_chunks, step, 0)

    return pl.pallas_call(
        body, out_shape=jax.ShapeDtypeStruct((seq, ch), BF16), grid=(ch // cb,),
        in_specs=[_col(seq, cb), _col(seq, cb), _col(FFN_PAD, cb), _col(FFN_PAD, cb), _col(1, cb), _col(1, cb)],
        out_specs=_col(seq, cb),
        scratch_shapes=[pltpu.VMEM((seq + FFN_PAD, cb), F32), pltpu.VMEM((seq + FFN_PAD, cb), F32)],
        compiler_params=_params(("parallel",)), name="ffn_mid_fwd",
    )(ug, uv, wg, wv, bg, bv)


def _ffn_mid_bwd(ug, uv, dz, wg, wv, bg, bv):
    seq, ch = ug.shape
    cb = _tile(ch, 256)
    n_chunks = seq // CHUNK

    def body(ug_ref, uv_ref, dz_ref, wg_ref, wv_ref, bg_ref, bv_ref, dug_ref, duv_ref, dwg_ref, dwv_ref, dbg_ref, dbv_ref,
             gpad_ref, vpad_ref, dyg_ref, dyv_ref):
        gpad_ref[0:FFN_PAD, :] = jnp.zeros((FFN_PAD, cb), F32)
        vpad_ref[0:FFN_PAD, :] = jnp.zeros((FFN_PAD, cb), F32)
        dyg_ref[seq:seq + FFN_PAD, :] = jnp.zeros((FFN_PAD, cb), F32)
        dyv_ref[seq:seq + FFN_PAD, :] = jnp.zeros((FFN_PAD, cb), F32)
        dwg_ref[...] = jnp.zeros((FFN_PAD, cb), F32)
        dwv_ref[...] = jnp.zeros((FFN_PAD, cb), F32)
        dbg_ref[...] = jnp.zeros((1, cb), F32)
        dbv_ref[...] = jnp.zeros((1, cb), F32)

        def fill(r, carry):
            r0 = pl.multiple_of(r * CHUNK, CHUNK)
            gpad_ref[pl.ds(r0 + FFN_PAD, CHUNK), :] = ug_ref[pl.ds(r0, CHUNK), :]
            vpad_ref[pl.ds(r0 + FFN_PAD, CHUNK), :] = uv_ref[pl.ds(r0, CHUNK), :]
            return carry

        lax.fori_loop(0, n_chunks, fill, 0)

        def grads_of_conv_out(r, carry):
            r0 = pl.multiple_of(r * CHUNK, CHUNK)
            rows = pl.ds(r0, CHUNK)
            yg, gwin = _ffn_conv(gpad_ref, r0, wg_ref, bg_ref, cb)
            yv, vwin = _ffn_conv(vpad_ref, r0, wv_ref, bv_ref, cb)
            dzv = dz_ref[rows, :]
            sg = _sigmoid(yg)
            dyg = dzv * yv * (sg * (1.0 + yg * (1.0 - sg)))
            dyv = dzv * yg * sg
            dyg_ref[rows, :] = dyg
            dyv_ref[rows, :] = dyv
            for t in range(FFN_CONV_W):
                off = FFN_PAD - (FFN_CONV_W - 1) + t
                dwg_ref[t:t + 1, :] += jnp.sum(dyg * gwin[off:off + CHUNK, :], axis=0, keepdims=True)
                dwv_ref[t:t + 1, :] += jnp.sum(dyv * vwin[off:off + CHUNK, :], axis=0, keepdims=True)
            dbg_ref[...] += jnp.sum(dyg, axis=0, keepdims=True)
            dbv_ref[...] += jnp.sum(dyv, axis=0, keepdims=True)
            return carry

        lax.fori_loop(0, n_chunks, grads_of_conv_out, 0)

        def grads_of_conv_in(r, carry):
            r0 = pl.multiple_of(r * CHUNK, CHUNK)
            gwin = dyg_ref[pl.ds(r0, CHUNK + FFN_PAD), :]
            vwin = dyv_ref[pl.ds(r0, CHUNK + FFN_PAD), :]
            dug = jnp.zeros((CHUNK, cb), F32)
            duv = jnp.zeros((CHUNK, cb), F32)
            for t in range(FFN_CONV_W):
                off = FFN_CONV_W - 1 - t
                dug = dug + wg_ref[t:t + 1, :] * gwin[off:off + CHUNK, :]
                duv = duv + wv_ref[t:t + 1, :] * vwin[off:off + CHUNK, :]
            dug_ref[pl.ds(r0, CHUNK), :] = dug.astype(BF16)
            duv_ref[pl.ds(r0, CHUNK), :] = duv.astype(BF16)
            return carry

        lax.fori_loop(0, n_chunks, grads_of_conv_in, 0)

    return pl.pallas_call(
        body,
        out_shape=(jax.ShapeDtypeStruct((seq, ch), BF16), jax.ShapeDtypeStruct((seq, ch), BF16),
                   jax.ShapeDtypeStruct((FFN_PAD, ch), F32), jax.ShapeDtypeStruct((FFN_PAD, ch), F32),
                   jax.ShapeDtypeStruct((1, ch), F32), jax.ShapeDtypeStruct((1, ch), F32)),
        grid=(ch // cb,),
        in_specs=[_col(seq, cb), _col(seq, cb), _col(seq, cb), _col(FFN_PAD, cb), _col(FFN_PAD, cb), _col(1, cb), _col(1, cb)],
        out_specs=(_col(seq, cb), _col(seq, cb), _col(FFN_PAD, cb), _col(FFN_PAD, cb), _col(1, cb), _col(1, cb)),
        scratch_shapes=[pltpu.VMEM((seq + FFN_PAD, cb), F32) for _ in range(4)],
        compiler_params=_params(("parallel",)), name="ffn_mid_bwd",
    )(ug, uv, dz, wg, wv, bg, bv)


def _dot(a, b, mode):
    return lax.dot_general(a.astype(BF16), b.astype(BF16), _DIMS[mode], preferred_element_type=F32)


def _head(h):
    return slice(h * HEAD_DIM, (h + 1) * HEAD_DIM)


def _mem_softmax(q, k):
    s = _dot(q, k, "nt") * ATT_SCALE
    e = jnp.exp(s - jnp.max(s, axis=-1, keepdims=True))
    return e / jnp.sum(e, axis=-1, keepdims=True)


def _memattn_fwd(pq, mk, mv):
    seq, w = pq.shape
    m = mk.shape[0]

    def body(q_ref, k_ref, v_ref, o_ref):
        for h in range(N_MEM_HEADS):
            p = _mem_softmax(q_ref[:, _head(h)], k_ref[:, _head(h)])
            o_ref[:, _head(h)] = _dot(p, v_ref[:, _head(h)], "nn").astype(BF16)

    full = pl.BlockSpec((m, w), lambda i: (0, 0))
    return pl.pallas_call(
        body, out_shape=jax.ShapeDtypeStruct((seq, w), BF16), grid=(seq // ROW_TILE,),
        in_specs=[_row(w), full, full], out_specs=_row(w), compiler_params=_params(("parallel",)), name="memattn_fwd",
    )(pq, mk, mv)


def _memattn_bwd(pq, mk, mv, dmo):
    seq, w = pq.shape
    m = mk.shape[0]

    def body(q_ref, k_ref, v_ref, do_ref, dq_ref, dk_ref, dv_ref):
        step = pl.program_id(0)

        @pl.when(step == 0)
        def _():
            dk_ref[...] = jnp.zeros((m, w), F32)
            dv_ref[...] = jnp.zeros((m, w), F32)

        for h in range(N_MEM_HEADS):
            q, k, v, do = q_ref[:, _head(h)], k_ref[:, _head(h)], v_ref[:, _head(h)], do_ref[:, _head(h)]
            p = _mem_softmax(q, k)
            dp = _dot(do, v, "nt")
            ds = p * (dp - jnp.sum(dp * p, axis=-1, keepdims=True))
            dq_ref[:, _head(h)] = (_dot(ds, k, "nn") * ATT_SCALE).astype(BF16)
            dk_ref[:, _head(h)] += _dot(ds, q, "tn") * ATT_SCALE
            dv_ref[:, _head(h)] += _dot(p, do, "tn")

    full = pl.BlockSpec((m, w), lambda i: (0, 0))
    return pl.pallas_call(
        body,
        out_shape=(jax.ShapeDtypeStruct((seq, w), BF16), jax.ShapeDtypeStruct((m, w), F32), jax.ShapeDtypeStruct((m, w), F32)),
        grid=(seq // ROW_TILE,), in_specs=[_row(w), full, full, _row(w)], out_specs=(_row(w), full, full),
        compiler_params=_params(("arbitrary",)), name="memattn_bwd",
    )(pq, mk, mv, dmo)


FWD_KEY_TILE = 512
BWD_KEY_TILE = 512


def _causal_t(s_t, q_blk, k_blk, key_tile):
    kpos = k_blk * key_tile + lax.broadcasted_iota(jnp.int32, (key_tile, ROW_TILE), 0)
    qpos = q_blk * ROW_TILE + lax.broadcasted_iota(jnp.int32, (key_tile, ROW_TILE), 1)
    return jnp.where(kpos <= qpos, s_t, NEG_BIG)


def _fox_fwd(q, k, v, cum_cols, cum_rows):
    seq, w = q.shape
    nq = seq // ROW_TILE

    def body(q_ref, k_ref, v_ref, cc_ref, cr_ref, o_ref, ob_ref, lse_ref):
        i = pl.program_id(1)
        qs = [q_ref[:, _head(h)].astype(BF16) for h in range(2)]
        crs = [cr_ref[0, 0, h:h + 1, :] for h in range(2)]

        def step(j, carry):
            rows = pl.ds(pl.multiple_of(j * FWD_KEY_TILE, FWD_KEY_TILE), FWD_KEY_TILE)
            new = []
            for h in range(2):
                m_run, l_run, acc = carry[h]
                s_t = _dot(k_ref[rows, _head(h)], qs[h], "nt") * ATT_SCALE + crs[h] - cc_ref[0, rows, h * HEAD_DIM:h * HEAD_DIM + 1]
                s_t = _causal_t(s_t, i, j, FWD_KEY_TILE)
                m_new = jnp.maximum(m_run, jnp.max(s_t, axis=0, keepdims=True))
                alpha = jnp.exp(m_run - m_new)
                p_t = jnp.exp(s_t - m_new)
                l_new = alpha * l_run + jnp.sum(p_t, axis=0, keepdims=True)
                new.append((m_new, l_new, alpha * acc + _dot(v_ref[rows, _head(h)], p_t, "tn")))
            return tuple(new)

        one_head = (jnp.full((1, ROW_TILE), NEG_BIG, F32), jnp.zeros((1, ROW_TILE), F32), jnp.zeros((HEAD_DIM, ROW_TILE), F32))
        res = lax.fori_loop(0, ((i + 1) * ROW_TILE + FWD_KEY_TILE - 1) // FWD_KEY_TILE, step, (one_head, one_head))
        o = jnp.concatenate([acc / l_run for _, l_run, acc in res], axis=0).T
        o_ref[...] = o
        ob_ref[...] = o.astype(BF16)
        lse_ref[...] = jnp.zeros((1, 1, 8, ROW_TILE), F32)
        for h in range(2):
            lse_ref[0, 0, h:h + 1, :] = res[h][0] + jnp.log(res[h][1])

    blk = pl.BlockSpec((ROW_TILE, LANE), lambda hp, i: (i, hp))
    full = pl.BlockSpec((seq, LANE), lambda hp, i: (0, hp))
    cols = pl.BlockSpec((1, seq, LANE), lambda hp, i: (hp, 0, 0))
    rows = pl.BlockSpec((1, 1, 8, ROW_TILE), lambda hp, i: (hp, i, 0, 0))
    return pl.pallas_call(
        body,
        out_shape=(jax.ShapeDtypeStruct((seq, w), F32), jax.ShapeDtypeStruct((seq, w), BF16),
                   jax.ShapeDtypeStruct((N_HEAD_PAIRS, nq, 8, ROW_TILE), F32)),
        grid=(N_HEAD_PAIRS, nq), in_specs=[blk, full, full, cols, rows], out_specs=(blk, blk, rows),
        compiler_params=_params(("parallel", "parallel")), name="fox_fwd",
    )(q, k, v, cum_cols, cum_rows)


def _fox_bwd(q, k, v, cum_cols, cum_rows, o, lse, do):
    seq, w = q.shape
    nq = seq // ROW_TILE
    nk = seq // BWD_KEY_TILE

    def body(q_ref, k_ref, v_ref, cc_ref, cr_ref, o_ref, lse_ref, do_ref, dq_ref, dk_ref, dv_ref, dcc_ref, dcr_ref):
        kj = pl.program_id(1)

        @pl.when(kj == 0)
        def _():
            dq_ref[...] = jnp.zeros((seq, LANE), F32)
            dcr_ref[...] = jnp.zeros((1, nq, 8, ROW_TILE), F32)

        ks = [k_ref[:, _head(h)].astype(BF16) for h in range(2)]
        vs = [v_ref[:, _head(h)].astype(BF16) for h in range(2)]
        ccs = [cc_ref[0, :, h * HEAD_DIM:h * HEAD_DIM + 1] for h in range(2)]
        ones = jnp.ones((8, HEAD_DIM), BF16)

        def step(i, carry):
            rows = pl.ds(pl.multiple_of(i * ROW_TILE, ROW_TILE), ROW_TILE)
            new = []
            for h in range(2):
                dk, dv, fold = carry[h]
                qh = q_ref[rows, _head(h)].astype(BF16)
                doh = do_ref[rows, _head(h)]
                s_t = _dot(ks[h], qh, "nt") * ATT_SCALE + cr_ref[0, i, h:h + 1, :] - ccs[h]
                p_t = jnp.exp(_causal_t(s_t, i, kj, BWD_KEY_TILE) - lse_ref[0, i, h:h + 1, :])
                dp_t = _dot(vs[h], doh, "nt")
                hi, mid, lo = _split3(doh * o_ref[rows, _head(h)])
                row_sum = lambda part: lax.dot_general(ones, part, _DIMS["nt"], preferred_element_type=F32)
                delta = ((row_sum(lo) + row_sum(mid)) + row_sum(hi))[0:1, :]
                ds_t = p_t * (dp_t - delta)
                dq_ref[rows, _head(h)] += _dot(ds_t, ks[h], "tn") * ATT_SCALE
                dcr_ref[0, i, h:h + 1, :] += jnp.sum(ds_t, axis=0, keepdims=True)
                new.append((dk + _dot(ds_t, qh, "nn") * ATT_SCALE, dv + _dot(p_t, doh, "nn"), fold + (ds_t[:, :LANE] + ds_t[:, LANE:])))
            return tuple(new)

        one_head = (jnp.zeros((BWD_KEY_TILE, HEAD_DIM), F32), jnp.zeros((BWD_KEY_TILE, HEAD_DIM), F32), jnp.zeros((BWD_KEY_TILE, LANE), F32))
        res = lax.fori_loop(kj * BWD_KEY_TILE // ROW_TILE, nq, step, (one_head, one_head))
        for h in range(2):
            dk_ref[:, _head(h)] = res[h][0]
            dv_ref[:, _head(h)] = res[h][1]
            dcc_ref[0, :, _head(h)] = jnp.broadcast_to(-jnp.sum(res[h][2], axis=-1, keepdims=True), (BWD_KEY_TILE, HEAD_DIM))

    blk = pl.BlockSpec((BWD_KEY_TILE, LANE), lambda hp, j: (j, hp))
    full = pl.BlockSpec((seq, LANE), lambda hp, j: (0, hp))
    cols = pl.BlockSpec((1, BWD_KEY_TILE, LANE), lambda hp, j: (hp, j, 0))
    rows = pl.BlockSpec((1, nq, 8, ROW_TILE), lambda hp, j: (hp, 0, 0, 0))
    return pl.pallas_call(
        body,
        out_shape=(jax.ShapeDtypeStruct((seq, w), F32), jax.ShapeDtypeStruct((seq, w), F32), jax.ShapeDtypeStruct((seq, w), F32),
                   jax.ShapeDtypeStruct((N_HEAD_PAIRS, seq, LANE), F32), jax.ShapeDtypeStruct((N_HEAD_PAIRS, nq, 8, ROW_TILE), F32)),
        grid=(N_HEAD_PAIRS, nk), in_specs=[full, blk, blk, cols, rows, full, rows, full],
        out_specs=(full, blk, blk, cols, rows),
        compiler_params=_params(("parallel", "arbitrary")), name="fox_bwd",
    )(q, k, v, cum_cols, cum_rows, o, lse, do)


def _split3(x):
    hi = x.astype(BF16)
    r1 = x - hi.astype(F32)
    mid = r1.astype(BF16)
    lo = (r1 - mid.astype(F32)).astype(BF16)
    return hi, mid, lo


def _tri_sum(tri, x):
    hi, mid, lo = _split3(x)
    dot = lambda p: lax.dot_general(tri, p, _DIMS["nn"], preferred_element_type=F32)
    return (dot(lo) + dot(mid)) + dot(hi)


def _tri(lower):
    r = lax.broadcasted_iota(jnp.int32, (LANE, LANE), 0)
    c = lax.broadcasted_iota(jnp.int32, (LANE, LANE), 1)
    return jnp.where((c <= r) if lower else (c >= r), 1.0, 0.0).astype(BF16)


def _fgate_fwd(fl, bf):
    seq, w = fl.shape
    nb = seq // LANE

    def body(f_ref, b_ref, cum_ref):
        tri = _tri(True)

        def step(i, carry):
            rows = pl.ds(pl.multiple_of(i * LANE, LANE), LANE)
            x = f_ref[rows, :] + b_ref[...]
            logsig = jnp.minimum(x, 0.0) - jnp.log(1.0 + jnp.exp(-jnp.abs(x)))
            cum = _tri_sum(tri, logsig) + carry
            cum_ref[rows, :] = cum
            return cum[LANE - 1:LANE, :]

        lax.fori_loop(0, nb, step, jnp.zeros((1, w), F32))

    return pl.pallas_call(body, out_shape=jax.ShapeDtypeStruct((seq, w), F32), compiler_params=_params(), name="fgate_fwd")(fl, bf)


def _fgate_bwd(fl, bf, dcum):
    seq, w = fl.shape
    nb = seq // LANE

    def body(f_ref, b_ref, dc_ref, df_ref, db_ref):
        tri = _tri(False)

        def step(i, carry):
            tail, db = carry
            rows = pl.ds(pl.multiple_of((nb - 1 - i) * LANE, LANE), LANE)
            suffix = _tri_sum(tri, dc_ref[rows, :]) + tail
            df = suffix * (1.0 - _sigmoid(f_ref[rows, :] + b_ref[...]))
            df_ref[rows, :] = df
            return suffix[0:1, :], db + jnp.sum(df, axis=0, keepdims=True)

        _, db = lax.fori_loop(0, nb, step, (jnp.zeros((1, w), F32), jnp.zeros((1, w), F32)))
        db_ref[...] = db

    return pl.pallas_call(
        body, out_shape=(jax.ShapeDtypeStruct((seq, w), F32), jax.ShapeDtypeStruct((1, w), F32)),
        compiler_params=_params(), name="fgate_bwd",
    )(fl, bf, dcum)


def _cum_layouts(cum):
    seq = cum.shape[0]
    nblk = seq // ROW_TILE
    heads = cum[:, :N_FOX_HEADS]
    cq = jnp.repeat(heads, HEAD_DIM, axis=1).reshape(seq, N_HEAD_PAIRS, LANE).transpose(1, 0, 2)
    ck = heads.T.reshape(N_HEAD_PAIRS, 2, nblk, ROW_TILE).transpose(0, 2, 1, 3)
    ck = jnp.pad(ck, ((0, 0), (0, 0), (0, 6), (0, 0)))
    return cq, ck


def _dcum_from_layouts(dcq, dck):
    seq = dcq.shape[1]
    key_side = dck[:, :, :2, :].transpose(0, 2, 1, 3).reshape(N_FOX_HEADS, seq).T
    query_side = dcq[:, :, ::HEAD_DIM].transpose(1, 0, 2).reshape(seq, N_FOX_HEADS)
    return jnp.pad(key_side + query_side, ((0, 0), (0, F_PAD - N_FOX_HEADS)))


def _local_step(x, mem, target, w):
    vec = lambda a: a.reshape(1, -1)
    mem_n = _rms_fwd(mem, vec(w["g_mem"]))
    saved = []
    shared = None
    for l in range(N_LAYERS):
        s = {"x_in": x}
        s["h"] = h = _rms_fwd(x, vec(w["g_mix"][l]))
        s["mk"] = mk = _mm(mem_n, w["w_mk"][l])
        s["mv"] = mv = _mm(mem_n, w["w_mv"][l])
        if l < N_A:
            s["pa"] = pa = _mm(h, w["w_a"][l])
            s["pg"] = pg = _mm(h, w["w_g"][l])
            s["pq"] = pq = _mm(h, w["w_qm"][l])
            s["v"], s["c"] = _conv_glu_fwd(pa, pg, vec(w["b_glu_a"][l]), vec(w["b_glu_g"][l]), w["w_dw_a"][l], vec(w["b_dw_a"][l]))
            s["mix"] = mix = _ln_silu_fwd(s["c"], vec(w["ln_g"][l]), vec(w["ln_b"][l]))
        else:
            if l == N_A:
                shared = {"x_in": x}
                shared["hk"] = hk = _rms_fwd(x, vec(w["g_kv"]))
                shared["k"] = _mm(hk, w["w_k"])
                shared["v"] = _mm(hk, w["w_v"])
                shared["fl"] = _mm(hk, w["w_f"])
                cum = _fgate_fwd(shared["fl"], w["b_f"])
                shared["cq"], shared["ck"] = _cum_layouts(cum)
            s["q"] = q = _mm(h, w["w_q"][l - N_A])
            s["pq"] = pq = _mm(h, w["w_qm"][l])
            s["o"], mix, s["lse"] = _fox_fwd(q, shared["k"], shared["v"], shared["cq"], shared["ck"])
            s["mix"] = mix
        s["mo"] = mo = _memattn_fwd(pq, mk, mv)
        x = _mm(mix, w["w_o_mix"][l], add=x)
        s["x_mid"] = x = _mm(mo, w["w_o_mem"][l], add=x)
        s["hf"] = hf = _rms_fwd(x, vec(w["g_ffn"][l]))
        s["ug"] = ug = _mm(hf, w["w_up_g"][l])
        s["uv"] = uv = _mm(hf, w["w_up_v"][l])
        s["z"] = z = _ffn_mid_fwd(ug, uv, w["w_dw_f_g"][l], w["w_dw_f_v"][l], vec(w["b_dw_f_g"][l]), vec(w["b_dw_f_v"][l]))
        x = _mm(z, w["w_down"][l], add=x)
        saved.append(s)

    loss_row, dx, dxb, dg_final = _loss_head(x, vec(w["g_final"]), target)
    loss = loss_row[0, 0]

    g = {n: [None] * N_LAYERS for n in ("g_mix", "w_mem_kv", "w_out", "g_ffn", "w_up", "w_dw_f", "b_dw_f", "w_down")}
    for n in ("w_in_a", "b_glu", "w_dw_a", "b_dw_a", "ln_g", "ln_b", "w_in_b"):
        g[n] = [None] * N_A
    one = lambda arr: ([arr], 0)
    g["g_final"] = [one(dg_final[0])]
    dmem_n = None
    dk_sum = dv_sum = dcq_sum = dck_sum = None
    for l in reversed(range(N_LAYERS)):
        s = saved[l]
        dz = _mm(dxb, w["w_down"][l], "nt")
        g["w_down"][l] = one(_mm(s["z"], dxb, "tn", BF16))
        dug, duv, dwg, dwv, dbg, dbv = _ffn_mid_bwd(s["ug"], s["uv"], dz, w["w_dw_f_g"][l], w["w_dw_f_v"][l],
                                                    vec(w["b_dw_f_g"][l]), vec(w["b_dw_f_v"][l]))
        g["w_up"][l] = ([_mm(s["hf"], dug, "tn", BF16), _mm(s["hf"], duv, "tn", BF16)], 1)
        g["w_dw_f"][l] = ([dwg[:FFN_CONV_W], dwv[:FFN_CONV_W]], 1)
        g["b_dw_f"][l] = ([dbg[0], dbv[0]], 0)
        dhf = _mm(duv, w["w_up_v"][l], "nt", add=_mm(dug, w["w_up_g"][l], "nt"))
        dx, dxb, dg_ffn = _rms_bwd(s["x_mid"], vec(w["g_ffn"][l]), dhf, dx)
        g["g_ffn"][l] = one(dg_ffn[0])
        dmix = _mm(dxb, w["w_o_mix"][l], "nt")
        dmo = _mm(dxb, w["w_o_mem"][l], "nt")
        g["w_out"][l] = ([_mm(s["mix"], dxb, "tn", BF16), _mm(s["mo"], dxb, "tn", BF16)], 0)
        dpq, dmk, dmv = _memattn_bwd(s["pq"], s["mk"], s["mv"], dmo)
        g["w_mem_kv"][l] = ([_mm(mem_n, dmk, "tn", BF16), _mm(mem_n, dmv, "tn", BF16)], 1)
        dmem_n = _mm(dmk, w["w_mk"][l], "nt", add=dmem_n)
        dmem_n = _mm(dmv, w["w_mv"][l], "nt", add=dmem_n)
        if l < N_A:
            dc, dlng, dlnb = _ln_silu_bwd(dmix, s["c"], vec(w["ln_g"][l]), vec(w["ln_b"][l]))
            da, dgate, dwdw, dbdw, dba, dbg2 = _conv_glu_bwd(dc, s["v"], s["pa"], s["pg"], vec(w["b_glu_a"][l]),
                                                              vec(w["b_glu_g"][l]), w["w_dw_a"][l])
            g["ln_g"][l], g["ln_b"][l], g["b_dw_a"][l] = one(dlng[0]), one(dlnb[0]), one(dbdw[0])
            g["w_dw_a"][l] = ([dwdw[:CONV_W]], 1)
            g["b_glu"][l] = ([dba[0], dbg2[0]], 0)
            g["w_in_a"][l] = ([_mm(s["h"], da, "tn", BF16), _mm(s["h"], dgate, "tn", BF16), _mm(s["h"], dpq, "tn", BF16)], 1)
            dh = _mm(da, w["w_a"][l], "nt")
            dh = _mm(dgate, w["w_g"][l], "nt", add=dh)
            dh = _mm(dpq, w["w_qm"][l], "nt", add=dh)
        else:
            dq, dk, dv, dcq, dck = _fox_bwd(s["q"], shared["k"], shared["v"], shared["cq"], shared["ck"], s["o"], s["lse"], dmix)
            dk_sum = dk if dk_sum is None else dk_sum + dk
            dv_sum = dv if dv_sum is None else dv_sum + dv
            dcq_sum = dcq if dcq_sum is None else dcq_sum + dcq
            dck_sum = dck if dck_sum is None else dck_sum + dck
            g["w_in_b"][l - N_A] = ([_mm(s["h"], dq, "tn", BF16), _mm(s["h"], dpq, "tn", BF16)], 1)
            dh = _mm(dq, w["w_q"][l - N_A], "nt")
            dh = _mm(dpq, w["w_qm"][l], "nt", add=dh)
        dx, dxb, dg_mix = _rms_bwd(s["x_in"], vec(w["g_mix"][l]), dh, dx)
        g["g_mix"][l] = one(dg_mix[0])
        if l == N_A:
            df, dbf = _fgate_bwd(shared["fl"], w["b_f"], _dcum_from_layouts(dcq_sum, dck_sum))
            hk = shared["hk"]
            g["w_kvf"] = [([_mm(hk, dk_sum, "tn", BF16), _mm(hk, dv_sum, "tn", BF16), _mm(hk, df, "tn", BF16)[:, :N_FOX_HEADS]], 1)]
            g["b_f"] = [one(dbf[0, :N_FOX_HEADS])]
            dhk = _mm(dk_sum, w["w_k"], "nt")
            dhk = _mm(dv_sum, w["w_v"], "nt", add=dhk)
            dhk = _mm(df, w["w_f"], "nt", add=dhk)
            dx, dxb, dg_kv = _rms_bwd(shared["x_in"], vec(w["g_kv"]), dhk, dx)
            g["g_kv"] = [one(dg_kv[0])]
    _, _, dg_mem = _rms_bwd(mem, vec(w["g_mem"]), dmem_n, None)
    g["g_mem"] = [one(dg_mem[0])]
    return loss, dx, g


SHARDED = {
    "w_in_a": ((N_A, D_MODEL, 2 * CONV_CH + MEM_W), 2),
    "b_glu": ((N_A, 2 * CONV_CH), 1),
    "w_dw_a": ((N_A, CONV_W, CONV_CH), 2),
    "b_dw_a": ((N_A, CONV_CH), 1),
    "ln_g": ((N_A, CONV_CH), 1),
    "ln_b": ((N_A, CONV_CH), 1),
    "w_kvf": ((D_MODEL, 2 * CONV_CH + N_FOX_HEADS), 1),
    "w_in_b": ((N_LAYERS - N_A, D_MODEL, CONV_CH + MEM_W), 1),
    "w_mem_kv": ((N_LAYERS, D_MODEL, 2 * MEM_W), 1),
    "w_out": ((N_LAYERS, D_MODEL, D_MODEL), 1),
    "w_up": ((N_LAYERS, D_MODEL, 2 * D_FF), 2),
    "w_dw_f": ((N_LAYERS, FFN_CONV_W, 2 * D_FF), 2),
    "w_down": ((N_LAYERS, D_FF, D_MODEL), 1),
}
REPLICATED = {
    "g_mix": (N_LAYERS, D_MODEL), "g_kv": (D_MODEL,), "b_f": (N_FOX_HEADS,), "g_mem": (D_MODEL,),
    "g_ffn": (N_LAYERS, D_MODEL), "b_dw_f": (N_LAYERS, 2 * D_FF), "g_final": (D_MODEL,),
}
WEIGHT_ORDER = ["g_mix", "w_in_a", "b_glu", "w_dw_a", "b_dw_a", "ln_g", "ln_b", "g_kv", "w_kvf", "b_f", "w_in_b", "g_mem",
                "w_mem_kv", "w_out", "g_ffn", "w_up", "w_dw_f", "b_dw_f", "w_down", "g_final"]


def _shard_shape(name):
    shape, axis = SHARDED[name]
    return tuple(d // N_CHIPS if i == axis else d for i, d in enumerate(shape))


def _size(shape):
    n = 1
    for d in shape:
        n *= d
    return n


VECTOR_LEAVES = ("b_glu", "w_dw_a", "b_dw_a", "ln_g", "ln_b", "w_dw_f")
MATRIX_LEAVES = tuple(n for n in SHARDED if n not in VECTOR_LEAVES)
_HALF_ROW_TILE = 16
_VECTOR_ELEMS = sum(_size(_shard_shape(n)) for n in VECTOR_LEAVES)
VECTOR_ROWS = -(-_VECTOR_ELEMS // (FLAT_COLS * 2 * _HALF_ROW_TILE)) * 2 * _HALF_ROW_TILE
REP_ELEMS = sum(_size(s) for s in REPLICATED.values())
REP_ROWS = 8
REP_COLS = -(-REP_ELEMS // (REP_ROWS * LANE)) * LANE


def _half_shape(name):
    shp = _shard_shape(name)
    assert shp[0] % 2 == 0 and (_size(shp) // 2 // shp[-1]) % _HALF_ROW_TILE == 0, name
    return (2, _size(shp) // 2 // shp[-1], shp[-1])


def _flatten_vectors(parts, dtype):
    flat = jnp.concatenate([p.reshape(-1).astype(dtype) for p in parts])
    return jnp.pad(flat, (0, VECTOR_ROWS * FLAT_COLS - _VECTOR_ELEMS)).reshape(2, VECTOR_ROWS // 2, FLAT_COLS)


def _unflatten_vectors(flat, lead=()):
    flat = flat.reshape(lead + (-1,))
    out, off = {}, 0
    for n in VECTOR_LEAVES:
        shp = _shard_shape(n)
        out[n] = flat[..., off:off + _size(shp)].reshape(lead + shp)
        off += _size(shp)
    return out


def _span(parts, cat_axis, sel_axis, lo, hi):
    if sel_axis != cat_axis:
        return jnp.concatenate([lax.slice_in_dim(p, lo, hi, axis=sel_axis) for p in parts], axis=cat_axis)
    taken, off = [], 0
    for p in parts:
        n = p.shape[cat_axis]
        a, b = max(lo, off), min(hi, off + n)
        if a < b:
            taken.append(lax.slice_in_dim(p, a - off, b - off, axis=cat_axis))
        off += n
    return taken[0] if len(taken) == 1 else jnp.concatenate(taken, axis=cat_axis)


def _weight_pieces(blocks, rep):
    def cut(name, layer, sel_axis, lo, hi):
        axis = SHARDED[name][1]
        parts = [blocks[name][t] if layer is None else blocks[name][t, layer] for t in range(N_CHIPS)]
        if layer is not None:
            axis -= 1
        return _span(parts, axis, sel_axis, lo, hi)

    def per_layer(name, n_layers, sel_axis, lo, hi):
        return [cut(name, l, sel_axis, lo, hi) for l in range(n_layers)]

    def pad_rows(arr, rows):
        return jnp.pad(arr, ((0, rows - arr.shape[0]), (0, 0)))

    w = dict(rep)
    w["w_a"] = per_layer("w_in_a", N_A, 1, 0, CONV_CH)
    w["w_g"] = per_layer("w_in_a", N_A, 1, CONV_CH, 2 * CONV_CH)
    w["w_q"] = per_layer("w_in_b", N_LAYERS - N_A, 1, 0, CONV_CH)
    w["w_qm"] = per_layer("w_in_a", N_A, 1, 2 * CONV_CH, 2 * CONV_CH + MEM_W) + per_layer("w_in_b", N_LAYERS - N_A, 1, CONV_CH, CONV_CH + MEM_W)
    w["b_glu_a"] = per_layer("b_glu", N_A, 0, 0, CONV_CH)
    w["b_glu_g"] = per_layer("b_glu", N_A, 0, CONV_CH, 2 * CONV_CH)
    w["w_dw_a"] = [pad_rows(p, CONV_PAD) for p in per_layer("w_dw_a", N_A, 1, 0, CONV_CH)]
    for n in ("b_dw_a", "ln_g", "ln_b"):
        w[n] = per_layer(n, N_A, 0, 0, CONV_CH)
    w["w_k"] = cut("w_kvf", None, 1, 0, CONV_CH)
    w["w_v"] = cut("w_kvf", None, 1, CONV_CH, 2 * CONV_CH)
    w["w_f"] = jnp.pad(cut("w_kvf", None, 1, 2 * CONV_CH, 2 * CONV_CH + N_FOX_HEADS), ((0, 0), (0, F_PAD - N_FOX_HEADS)))
    w["b_f"] = jnp.pad(rep["b_f"], (0, F_PAD - N_FOX_HEADS)).reshape(1, F_PAD)
    w["w_mk"] = per_layer("w_mem_kv", N_LAYERS, 1, 0, MEM_W)
    w["w_mv"] = per_layer("w_mem_kv", N_LAYERS, 1, MEM_W, 2 * MEM_W)
    w["w_o_mix"] = per_layer("w_out", N_LAYERS, 0, 0, CONV_CH)
    w["w_o_mem"] = per_layer("w_out", N_LAYERS, 0, CONV_CH, D_MODEL)
    w["w_up_g"] = per_layer("w_up", N_LAYERS, 1, 0, D_FF)
    w["w_up_v"] = per_layer("w_up", N_LAYERS, 1, D_FF, 2 * D_FF)
    w["w_dw_f_g"] = [pad_rows(p, FFN_PAD) for p in per_layer("w_dw_f", N_LAYERS, 1, 0, D_FF)]
    w["w_dw_f_v"] = [pad_rows(p, FFN_PAD) for p in per_layer("w_dw_f", N_LAYERS, 1, D_FF, 2 * D_FF)]
    w["b_dw_f_g"], w["b_dw_f_v"] = rep["b_dw_f"][:, :D_FF], rep["b_dw_f"][:, D_FF:]
    w["w_down"] = per_layer("w_down", N_LAYERS, 0, 0, D_FF)
    return w


def _shard_of(entries, name, t):
    shape, axis = SHARDED[name]
    stacked = len(shape) - len(entries[0][0][0].shape)
    step = shape[axis] // N_CHIPS
    layers = [_span(pieces, cat_axis, axis - stacked, t * step, (t + 1) * step) for pieces, cat_axis in entries]
    return jnp.stack(layers) if stacked else layers[0]


def _grad_slabs(grads):
    out = [jnp.stack([_shard_of(grads[n], n, t).astype(BF16) for t in range(N_CHIPS)]).reshape((N_CHIPS,) + _half_shape(n))
           for n in MATRIX_LEAVES]
    out.append(jnp.stack([_flatten_vectors([_shard_of(grads[n], n, t) for n in VECTOR_LEAVES], BF16) for t in range(N_CHIPS)]))
    return out


def _replicated_rows(grads):
    parts = [p.reshape(-1) for n in REPLICATED for pieces, _ in grads[n] for p in pieces]
    flat = jnp.concatenate(parts)
    return jnp.pad(flat, (0, REP_ROWS * REP_COLS - REP_ELEMS)).reshape(REP_ROWS, REP_COLS)


_ANY = pl.BlockSpec(memory_space=pl.ANY)


def _place():
    x, y, c = lax.axis_index("x"), lax.axis_index("y"), lax.axis_index("c")
    chips = [(1 - x, y), (x, 1 - y), (1 - x, 1 - y)]
    return x, y, c, chips


def _chip_index(chip):
    return 2 * chip[0] + chip[1]


def _gather_leaves(halves):
    n = len(halves)

    def body(*refs):
        w_refs, out_refs, (send_sems, recv_sems) = refs[:n], refs[n:2 * n], refs[2 * n:]
        x, y, c, chips = _place()
        me = 2 * x + y
        sibling = (x, y, 1 - c)

        def copy(a, sem, chip_idx, half, to, src=None):
            dst = out_refs[a].at[chip_idx, half]
            return pltpu.make_async_remote_copy(src_ref=dst if src is None else src, dst_ref=dst, send_sem=send_sems.at[6 * a + sem],
                                                recv_sem=recv_sems.at[6 * a + sem], device_id=to, device_id_type=MESH_ID)

        first = [copy(a, j, me, c, (*chip, c), src=w_refs[a].at[c]) for j, chip in enumerate(chips) for a in range(n)]
        for cp in first:
            cp.start()
        passed = []
        for j, chip in enumerate(chips):
            for a in range(n):
                copy(a, j, _chip_index(chip), c, (x, y, c)).wait_recv()
                passed.append(copy(a, 3 + j, _chip_index(chip), c, sibling))
                passed[-1].start()
        for j, chip in enumerate(chips):
            for a in range(n):
                copy(a, 3 + j, _chip_index(chip), 1 - c, (x, y, c)).wait_recv()
        for cp in first + passed:
            cp.wait_send()

    return pl.pallas_call(
        body, out_shape=[jax.ShapeDtypeStruct((N_CHIPS,) + h.shape, h.dtype) for h in halves],
        in_specs=[_ANY] * n, out_specs=[_ANY] * n,
        scratch_shapes=[pltpu.SemaphoreType.DMA((6 * n,)), pltpu.SemaphoreType.DMA((6 * n,))], name="gather_leaves",
    )(*halves)


def _swap_halves(slabs):
    n = len(slabs)

    def body(*refs):
        g_refs, got_refs, (send_sem, recv_sem) = refs[:n], refs[n:2 * n], refs[2 * n:]
        x, y, c, _ = _place()
        copies = [pltpu.make_async_remote_copy(src_ref=g_refs[a].at[t, 1 - c], dst_ref=got_refs[a].at[t], send_sem=send_sem.at[N_CHIPS * a + t],
                                               recv_sem=recv_sem.at[N_CHIPS * a + t], device_id=(x, y, 1 - c), device_id_type=MESH_ID)
                  for a in range(n) for t in range(N_CHIPS)]
        for cp in copies:
            cp.start()
        for cp in copies:
            cp.wait()

    return pl.pallas_call(
        body, out_shape=[jax.ShapeDtypeStruct((g.shape[0],) + g.shape[2:], g.dtype) for g in slabs],
        in_specs=[_ANY] * n, out_specs=[_ANY] * n,
        scratch_shapes=[pltpu.SemaphoreType.DMA((N_CHIPS * n,)), pltpu.SemaphoreType.DMA((N_CHIPS * n,))], name="swap_halves",
    )(*slabs)


def _scatter_to_chips(parts):
    n = len(parts)

    def body(*refs):
        p_refs, got_refs, (send_sems, recv_sems) = refs[:n], refs[n:2 * n], refs[2 * n:]
        x, y, c, chips = _place()
        me = 2 * x + y
        sends = [pltpu.make_async_remote_copy(src_ref=p_refs[a].at[_chip_index(chip)], dst_ref=got_refs[a].at[me],
                                              send_sem=send_sems.at[3 * a + j], recv_sem=recv_sems.at[3 * a + j],
                                              device_id=(*chip, c), device_id_type=MESH_ID)
                 for j, chip in enumerate(chips) for a in range(n)]
        for cp in sends:
            cp.start()
        for j, chip in enumerate(chips):
            for a in range(n):
                landed = got_refs[a].at[_chip_index(chip)]
                pltpu.make_async_remote_copy(src_ref=landed, dst_ref=landed, send_sem=send_sems.at[3 * a + j], recv_sem=recv_sems.at[3 * a + j],
                                             device_id=(x, y, c), device_id_type=MESH_ID).wait_recv()
        for cp in sends:
            cp.wait_send()

    return pl.pallas_call(
        body, out_shape=[jax.ShapeDtypeStruct(p.shape, p.dtype) for p in parts], in_specs=[_ANY] * n, out_specs=[_ANY] * n,
        scratch_shapes=[pltpu.SemaphoreType.DMA((3 * n,)), pltpu.SemaphoreType.DMA((3 * n,))], name="scatter_to_chips",
    )(*parts)


def _sibling_halves(halves):
    n = len(halves)

    def body(*refs):
        h_refs, got_refs, (send_sem, recv_sem) = refs[:n], refs[n:2 * n], refs[2 * n:]
        x, y, c, _ = _place()
        copies = [pltpu.make_async_remote_copy(src_ref=h_refs[a], dst_ref=got_refs[a], send_sem=send_sem.at[a], recv_sem=recv_sem.at[a],
                                               device_id=(x, y, 1 - c), device_id_type=MESH_ID) for a in range(n)]
        for cp in copies:
            cp.start()
        for cp in copies:
            cp.wait()

    return pl.pallas_call(
        body, out_shape=[jax.ShapeDtypeStruct(h.shape, h.dtype) for h in halves], in_specs=[_ANY] * n, out_specs=[_ANY] * n,
        scratch_shapes=[pltpu.SemaphoreType.DMA((n,)), pltpu.SemaphoreType.DMA((n,))], name="sibling_halves",
    )(*halves)


def _gather_replicated(rows):
    m_per, n = rows.shape

    def body(x_ref, out_ref, send_sems, recv_sems, local_sem):
        x, y, c, chips = _place()
        me, sibling = (x, y, c), (x, y, 1 - c)

        def block(px, py, pc):
            return out_ref.at[pl.ds((4 * px + 2 * py + pc) * m_per, m_per), :]

        def copy(k, blk, to, src=None):
            return pltpu.make_async_remote_copy(src_ref=block(*blk) if src is None else src, dst_ref=block(*blk),
                                                send_sem=send_sems.at[k], recv_sem=recv_sems.at[k], device_id=to, device_id_type=MESH_ID)

        mine = pltpu.make_async_copy(x_ref, block(*me), local_sem)
        mine.start()
        first = [copy(0, me, sibling, src=x_ref)]
        first += [copy(1 + j, me, (*chip, c), src=x_ref) for j, chip in enumerate(chips)]
        for cp in first:
            cp.start()
        passed = [copy(4 + j, (*chip, c), sibling) for j, chip in enumerate(chips)]
        for j, chip in enumerate(chips):
            copy(1 + j, (*chip, c), me).wait_recv()
            passed[j].start()
        copy(0, sibling, me).wait_recv()
        for j, chip in enumerate(chips):
            copy(4 + j, (*chip, 1 - c), me).wait_recv()
        for cp in first + passed:
            cp.wait_send()
        mine.wait()

    vmem = pl.BlockSpec(memory_space=pltpu.VMEM)
    return pl.pallas_call(
        body, out_shape=jax.ShapeDtypeStruct((N_DEV * m_per, n), rows.dtype), in_specs=[vmem], out_specs=vmem,
        scratch_shapes=[pltpu.SemaphoreType.DMA((7,)), pltpu.SemaphoreType.DMA((7,)), pltpu.SemaphoreType.DMA],
        name="gather_replicated",
    )(rows)


_SUM_ROWS = 256


def _tile_rows(rows):
    if rows <= _SUM_ROWS:
        return rows
    best = None
    for r in range(16, _SUM_ROWS + 1, 16):
        if rows % r == 0:
            best = r
    assert best is not None, rows
    return best


def _add_pairs(slabs, got, core):
    n, _, rows, cols = slabs.shape
    tr = _tile_rows(rows)

    def body(core_ref, a_ref, b_ref, o_ref):
        o_ref[...] = (a_ref[0].astype(F32) + b_ref[...].astype(F32)).astype(BF16)

    spec = pl.BlockSpec((1, tr, cols), lambda t, i, core_ref: (t, i, 0))
    return pl.pallas_call(
        body, out_shape=jax.ShapeDtypeStruct((n, rows, cols), BF16),
        grid_spec=pltpu.PrefetchScalarGridSpec(
            num_scalar_prefetch=1, grid=(n, rows // tr),
            in_specs=[pl.BlockSpec((1, 1, tr, cols), lambda t, i, core_ref: (t, core_ref[0], i, 0)), spec], out_specs=spec),
        compiler_params=_params(("parallel", "parallel")), name=f"add_pairs_{rows}x{cols}",
    )(core, slabs, got)


def _sum_slabs(slabs):
    n, rows, cols = slabs.shape
    tr = _tile_rows(rows)

    def body(s_ref, o_ref):
        acc = s_ref[0].astype(F32)
        for t in range(1, n):
            acc = acc + s_ref[t].astype(F32)
        o_ref[...] = acc

    return pl.pallas_call(
        body, out_shape=jax.ShapeDtypeStruct((rows, cols), F32), grid=(rows // tr,),
        in_specs=[pl.BlockSpec((n, tr, cols), lambda i: (0, i, 0))], out_specs=pl.BlockSpec((tr, cols), lambda i: (i, 0)),
        compiler_params=_params(("parallel",)), name=f"sum_slabs_{n}x{rows}x{cols}",
    )(slabs)


_ADAM_BLOCK_BYTES = 2 * 1024 * 1024


def _adamw(w, g, m, v):
    shape = w.shape
    cols = shape[-1]
    rows = _size(shape) // cols
    tr = rows
    if rows * cols * 4 > _ADAM_BLOCK_BYTES:
        for r in range(8, rows, 8):
            if rows % r == 0 and r * cols * 4 <= _ADAM_BLOCK_BYTES:
                tr = r

    def body(w_ref, g_ref, m_ref, v_ref, d_ref, nm_ref, nv_ref):
        gv = g_ref[...]
        nm = ADAM_B1 * m_ref[...] + (1.0 - ADAM_B1) * gv
        nv = ADAM_B2 * v_ref[...] + (1.0 - ADAM_B2) * jnp.square(gv)
        m_hat = nm / (1.0 - ADAM_B1 ** ADAM_STEP)
        v_hat = nv / (1.0 - ADAM_B2 ** ADAM_STEP)
        d_ref[...] = -ADAM_LR * (m_hat / (jnp.sqrt(v_hat) + ADAM_EPS) + ADAM_WD * w_ref[...])
        nm_ref[...] = nm
        nv_ref[...] = nv

    spec = pl.BlockSpec((tr, cols), lambda i: (i, 0))
    out = pl.pallas_call(
        body, out_shape=tuple(jax.ShapeDtypeStruct((rows, cols), F32) for _ in range(3)), grid=(rows // tr,),
        in_specs=[spec] * 4, out_specs=(spec,) * 3, compiler_params=_params(("parallel",)), name=f"adamw_{rows}x{cols}",
    )(*(a.reshape(rows, cols) for a in (w, g, m, v)))
    return tuple(o.reshape(shape) for o in out)


def kernel(x, mem, g_mix, w_in_a, b_glu, w_dw_a, b_dw_a, ln_g, ln_b, g_kv, w_kvf, b_f, w_in_b, g_mem, w_mem_kv, w_out, g_ffn, w_up, w_dw_f, b_dw_f, w_down, g_final, loss_target, m_g_mix, m_w_in_a, m_b_glu, m_w_dw_a, m_b_dw_a, m_ln_g, m_ln_b, m_g_kv, m_w_kvf, m_b_f, m_w_in_b, m_g_mem, m_w_mem_kv, m_w_out, m_g_ffn, m_w_up, m_w_dw_f, m_b_dw_f, m_w_down, m_g_final, v_g_mix, v_w_in_a, v_b_glu, v_w_dw_a, v_b_dw_a, v_ln_g, v_ln_b, v_g_kv, v_w_kvf, v_b_f, v_w_in_b, v_g_mem, v_w_mem_kv, v_w_out, v_g_ffn, v_w_up, v_w_dw_f, v_b_dw_f, v_w_down, v_g_final):
    weights = dict(g_mix=g_mix, w_in_a=w_in_a, b_glu=b_glu, w_dw_a=w_dw_a, b_dw_a=b_dw_a, ln_g=ln_g, ln_b=ln_b, g_kv=g_kv,
                   w_kvf=w_kvf, b_f=b_f, w_in_b=w_in_b, g_mem=g_mem, w_mem_kv=w_mem_kv, w_out=w_out, g_ffn=g_ffn, w_up=w_up,
                   w_dw_f=w_dw_f, b_dw_f=b_dw_f, w_down=w_down, g_final=g_final)
    mom1 = dict(g_mix=m_g_mix, w_in_a=m_w_in_a, b_glu=m_b_glu, w_dw_a=m_w_dw_a, b_dw_a=m_b_dw_a, ln_g=m_ln_g, ln_b=m_ln_b,
                g_kv=m_g_kv, w_kvf=m_w_kvf, b_f=m_b_f, w_in_b=m_w_in_b, g_mem=m_g_mem, w_mem_kv=m_w_mem_kv, w_out=m_w_out,
                g_ffn=m_g_ffn, w_up=m_w_up, w_dw_f=m_w_dw_f, b_dw_f=m_b_dw_f, w_down=m_w_down, g_final=m_g_final)
    mom2 = dict(g_mix=v_g_mix, w_in_a=v_w_in_a, b_glu=v_b_glu, w_dw_a=v_w_dw_a, b_dw_a=v_b_dw_a, ln_g=v_ln_g, ln_b=v_ln_b,
                g_kv=v_g_kv, w_kvf=v_w_kvf, b_f=v_b_f, w_in_b=v_w_in_b, g_mem=v_g_mem, w_mem_kv=v_w_mem_kv, w_out=v_w_out,
                g_ffn=v_g_ffn, w_up=v_w_up, w_dw_f=v_w_dw_f, b_dw_f=v_b_dw_f, w_down=v_w_down, g_final=v_g_final)

    x_pos, y_pos, core = lax.axis_index("x"), lax.axis_index("y"), lax.axis_index("c")
    chip = 2 * x_pos + y_pos

    local = [weights[n].astype(BF16).reshape(_half_shape(n)) for n in MATRIX_LEAVES]
    local.append(_flatten_vectors([weights[n] for n in VECTOR_LEAVES], F32))
    gathered = [lax.dynamic_update_index_in_dim(g, own[None], chip, 0) for g, own in zip(_gather_leaves(local), local)]
    blocks = {n: g.reshape((N_CHIPS,) + _shard_shape(n)) for n, g in zip(MATRIX_LEAVES, gathered)}
    blocks.update(_unflatten_vectors(gathered[-1], (N_CHIPS,)))
    pieces = _weight_pieces(blocks, {n: weights[n] for n in REPLICATED})

    loss, dx, grads = _local_step(x[0], mem[0], loss_target[0], pieces)
    loss = lax.psum(loss, ("x", "y", "c"))

    slabs = _grad_slabs(grads)
    core_index = core.astype(jnp.int32).reshape(1)
    chip_sums = [_add_pairs(mine, got, core_index) for mine, got in zip(slabs, _swap_halves(slabs))]
    from_chips = [lax.dynamic_update_index_in_dim(got, lax.dynamic_index_in_dim(mine, chip, 0, keepdims=True), chip, 0)
                  for got, mine in zip(_scatter_to_chips(chip_sums), chip_sums)]
    mine = [_sum_slabs(f) for f in from_chips]
    theirs = _sibling_halves(mine)
    reduced = [jnp.stack([jnp.where(core == 0, m, t), jnp.where(core == 0, t, m)]) for m, t in zip(mine, theirs)]
    grad_leaves = {n: r.reshape(_shard_shape(n)) for n, r in zip(MATRIX_LEAVES, reduced)}
    grad_leaves.update(_unflatten_vectors(reduced[-1]))

    rep_sum = _sum_slabs(_gather_replicated(_replicated_rows(grads)).reshape(N_DEV, REP_ROWS, REP_COLS)).reshape(-1)
    off = 0
    for n, shp in REPLICATED.items():
        grad_leaves[n] = rep_sum[off:off + _size(shp)].reshape(shp)
        off += _size(shp)

    deltas, new_m, new_v = {}, {}, {}
    for n in WEIGHT_ORDER:
        deltas[n], new_m[n], new_v[n] = _adamw(weights[n], grad_leaves[n], mom1[n], mom2[n])
    return (loss, dx[None], *[grad_leaves[n] for n in WEIGHT_ORDER], *[deltas[n] for n in WEIGHT_ORDER],
            *[new_m[n] for n in WEIGHT_ORDER], *[new_v[n] for n in WEIGHT_ORDER])
```

```python
import jax
import jax.numpy as jnp
from jax import lax
from jax.experimental import pallas as pl
from jax.experimental.pallas import tpu as pltpu

F32 = jnp.float32
BF16 = jnp.bfloat16

D_MODEL = 1024
N_LAYERS = 4
N_A = 2
CONV_CH = 768
MEM_W = 256
HEAD_DIM = 64
N_MEM_HEADS = 4
N_FOX_HEADS = 12
N_HEAD_PAIRS = N_FOX_HEADS // 2
D_FF = 2816
CONV_W = 31
CONV_PAD = 32
FFN_CONV_W = 3
FFN_PAD = 8
F_PAD = 128
RMS_EPS = 1e-6
LN_EPS = 1e-5
ATT_SCALE = HEAD_DIM ** -0.5
NEG_BIG = -1e30

ADAM_LR = 0.001
ADAM_B1 = 0.9
ADAM_B2 = 0.999
ADAM_EPS = 1e-08
ADAM_WD = 0.01
ADAM_STEP = 10

LANE = 128
ROW_TILE = 256
CHUNK = 128
VMEM_LIMIT = 48 * 1024 * 1024
FLAT_COLS = 1024
N_CHIPS = 4
N_DEV = 8
MESH_ID = pl.DeviceIdType.MESH


def _params(sem=None):
    return pltpu.CompilerParams(dimension_semantics=sem, vmem_limit_bytes=VMEM_LIMIT)


def _tile(dim, pref):
    if dim <= pref:
        return dim
    best = None
    for m in range(1, dim // LANE + 1):
        d = m * LANE
        if dim % d == 0 and d <= pref:
            best = d
    assert best is not None, dim
    return best


_DIMS = {"nn": (((1,), (0,)), ((), ())), "nt": (((1,), (1,)), ((), ())), "tn": (((0,), (0,)), ((), ()))}


_MM_RESIDENT_BYTES = 8 * 1024 * 1024
_MM_STREAM_BYTES = 6 * 1024 * 1024
_MM_OUT_BYTES = 6 * 1024 * 1024


def _mm_tiles(m, n, k, mode, a_size, b_size, o_size):
    if mode == "tn":
        tm = _tile(m, _MM_RESIDENT_BYTES // (k * a_size))
        tn = _tile(n, min(_MM_STREAM_BYTES // (k * b_size), _MM_OUT_BYTES // (tm * o_size)))
    else:
        tn = _tile(n, _MM_RESIDENT_BYTES // (k * b_size))
        tm = _tile(m, min(_MM_STREAM_BYTES // (k * a_size), _MM_OUT_BYTES // (tn * o_size), 512))
    return tm, tn


def _mm(a, b, mode="nn", out_dtype=F32, add=None):
    if mode == "nn":
        (m, k), (k2, n) = a.shape, b.shape
    elif mode == "nt":
        (m, k), (n, k2) = a.shape, b.shape
    else:
        (k, m), (k2, n) = a.shape, b.shape
    assert k == k2, (a.shape, b.shape, mode)
    tm, tn = _mm_tiles(m, n, k, mode, a.dtype.itemsize, b.dtype.itemsize, jnp.dtype(out_dtype).itemsize)
    dims = _DIMS[mode]
    has_add = add is not None

    def body(*refs):
        if has_add:
            a_ref, b_ref, add_ref, o_ref = refs
        else:
            a_ref, b_ref, o_ref = refs
        r = lax.dot_general(a_ref[...].astype(BF16), b_ref[...].astype(BF16), dims, preferred_element_type=F32)
        if has_add:
            r = r + add_ref[...]
        o_ref[...] = r.astype(out_dtype)

    a_spec = pl.BlockSpec((k, tm), lambda i, j: (0, i)) if mode == "tn" else pl.BlockSpec((tm, k), lambda i, j: (i, 0))
    b_spec = pl.BlockSpec((tn, k), lambda i, j: (j, 0)) if mode == "nt" else pl.BlockSpec((k, tn), lambda i, j: (0, j))
    o_spec = pl.BlockSpec((tm, tn), lambda i, j: (i, j))
    in_specs = [a_spec, b_spec] + ([o_spec] if has_add else [])
    args = (a, b) + ((add,) if has_add else ())
    return pl.pallas_call(
        body,
        out_shape=jax.ShapeDtypeStruct((m, n), out_dtype),
        grid=(m // tm, n // tn),
        in_specs=in_specs,
        out_specs=o_spec,
        compiler_params=_params(("parallel", "parallel")),
        name=f"mm_{mode}_{m}x{k}x{n}",
    )(*args)


def _row(width):
    return pl.BlockSpec((ROW_TILE, width), lambda i: (i, 0))


def _vec(width):
    return pl.BlockSpec((1, width), lambda i: (0, 0))


def _rms_fwd(x, g):
    rows, d = x.shape

    def body(x_ref, g_ref, o_ref):
        xv = x_ref[...]
        rstd = lax.rsqrt(jnp.mean(xv * xv, axis=-1, keepdims=True) + RMS_EPS)
        o_ref[...] = (xv * rstd * g_ref[...]).astype(BF16)

    return pl.pallas_call(
        body, out_shape=jax.ShapeDtypeStruct((rows, d), BF16), grid=(rows // ROW_TILE,),
        in_specs=[_row(d), _vec(d)], out_specs=_row(d), compiler_params=_params(("parallel",)), name=f"rms_fwd_{rows}",
    )(x, g)


def _accumulate(ref, value, step):
    @pl.when(step == 0)
    def _():
        ref[...] = value

    @pl.when(step > 0)
    def _():
        ref[...] += value


def _rms_bwd(x, g, dh, dres):
    rows, d = x.shape
    has_res = dres is not None

    def body(*refs):
        if has_res:
            x_ref, g_ref, dh_ref, dres_ref, dx_ref, dxb_ref, dg_ref = refs
        else:
            x_ref, g_ref, dh_ref, dx_ref, dxb_ref, dg_ref = refs
        xv = x_ref[...]
        dhv = dh_ref[...]
        rstd = lax.rsqrt(jnp.mean(xv * xv, axis=-1, keepdims=True) + RMS_EPS)
        xhat = xv * rstd
        gd = dhv * g_ref[...]
        dx = rstd * (gd - xhat * jnp.mean(gd * xhat, axis=-1, keepdims=True))
        if has_res:
            dx = dx + dres_ref[...]
        dx_ref[...] = dx
        dxb_ref[...] = dx.astype(BF16)
        _accumulate(dg_ref, jnp.sum(dhv * xhat, axis=0, keepdims=True), pl.program_id(0))

    in_specs = [_row(d), _vec(d), _row(d)] + ([_row(d)] if has_res else [])
    args = (x, g, dh) + ((dres,) if has_res else ())
    return pl.pallas_call(
        body,
        out_shape=(jax.ShapeDtypeStruct((rows, d), F32), jax.ShapeDtypeStruct((rows, d), BF16), jax.ShapeDtypeStruct((1, d), F32)),
        grid=(rows // ROW_TILE,), in_specs=in_specs, out_specs=(_row(d), _row(d), _vec(d)),
        compiler_params=_params(("arbitrary",)), name=f"rms_bwd_{rows}_{int(has_res)}",
    )(*args)


def _loss_head(x, g, target):
    rows, d = x.shape

    def body(x_ref, g_ref, t_ref, loss_ref, dx_ref, dxb_ref, dg_ref):
        xv = x_ref[...]
        gv = g_ref[...]
        rstd = lax.rsqrt(jnp.mean(xv * xv, axis=-1, keepdims=True) + RMS_EPS)
        xhat = xv * rstd
        err = xhat * gv - t_ref[...]
        part = 0.5 * jnp.sum(jnp.mean(err * err, axis=-1, keepdims=True), axis=0, keepdims=True)
        dy = err * (1.0 / d)
        gd = dy * gv
        dx = rstd * (gd - xhat * jnp.mean(gd * xhat, axis=-1, keepdims=True))
        dx_ref[...] = dx
        dxb_ref[...] = dx.astype(BF16)
        step = pl.program_id(0)
        _accumulate(loss_ref, jnp.broadcast_to(part, (1, LANE)), step)
        _accumulate(dg_ref, jnp.sum(dy * xhat, axis=0, keepdims=True), step)

    return pl.pallas_call(
        body,
        out_shape=(jax.ShapeDtypeStruct((1, LANE), F32), jax.ShapeDtypeStruct((rows, d), F32),
                   jax.ShapeDtypeStruct((rows, d), BF16), jax.ShapeDtypeStruct((1, d), F32)),
        grid=(rows // ROW_TILE,), in_specs=[_row(d), _vec(d), _row(d)],
        out_specs=(_vec(LANE), _row(d), _row(d), _vec(d)),
        compiler_params=_params(("arbitrary",)), name="loss_head",
    )(x, g, target)


def _sigmoid(x):
    return 1.0 / (1.0 + jnp.exp(-x))


def _ln_silu_fwd(c, ln_g, ln_b):
    rows, ch = c.shape

    def body(c_ref, g_ref, b_ref, o_ref):
        cv = c_ref[...]
        mu = jnp.mean(cv, axis=-1, keepdims=True)
        cen = cv - mu
        rstd = lax.rsqrt(jnp.mean(cen * cen, axis=-1, keepdims=True) + LN_EPS)
        y = cen * rstd * g_ref[...] + b_ref[...]
        o_ref[...] = (y * _sigmoid(y)).astype(BF16)

    return pl.pallas_call(
        body, out_shape=jax.ShapeDtypeStruct((rows, ch), BF16), grid=(rows // ROW_TILE,),
        in_specs=[_row(ch), _vec(ch), _vec(ch)], out_specs=_row(ch), compiler_params=_params(("parallel",)), name="ln_silu_fwd",
    )(c, ln_g, ln_b)


def _ln_silu_bwd(dmix, c, ln_g, ln_b):
    rows, ch = c.shape

    def body(dm_ref, c_ref, g_ref, b_ref, dc_ref, dg_ref, db_ref):
        cv = c_ref[...]
        gv = g_ref[...]
        mu = jnp.mean(cv, axis=-1, keepdims=True)
        cen = cv - mu
        rstd = lax.rsqrt(jnp.mean(cen * cen, axis=-1, keepdims=True) + LN_EPS)
        xhat = cen * rstd
        y = xhat * gv + b_ref[...]
        sg = _sigmoid(y)
        dy = dm_ref[...] * (sg * (1.0 + y * (1.0 - sg)))
        dxh = dy * gv
        dc = rstd * (dxh - jnp.mean(dxh, axis=-1, keepdims=True) - xhat * jnp.mean(dxh * xhat, axis=-1, keepdims=True))
        dc_ref[...] = dc
        step = pl.program_id(0)
        _accumulate(dg_ref, jnp.sum(dy * xhat, axis=0, keepdims=True), step)
        _accumulate(db_ref, jnp.sum(dy, axis=0, keepdims=True), step)

    return pl.pallas_call(
        body,
        out_shape=(jax.ShapeDtypeStruct((rows, ch), F32), jax.ShapeDtypeStruct((1, ch), F32), jax.ShapeDtypeStruct((1, ch), F32)),
        grid=(rows // ROW_TILE,), in_specs=[_row(ch), _row(ch), _vec(ch), _vec(ch)],
        out_specs=(_row(ch), _vec(ch), _vec(ch)), compiler_params=_params(("arbitrary",)), name="ln_silu_bwd",
    )(dmix, c, ln_g, ln_b)


def _col(rows, cb):
    return pl.BlockSpec((rows, cb), lambda j: (0, j))


def _conv_glu_fwd(pa, pg, ba, bg, w_dw, b_dw):
    seq, ch = pa.shape
    cb = LANE
    n_chunks = seq // CHUNK

    def body(pa_ref, pg_ref, ba_ref, bg_ref, w_ref, b_ref, v_ref, c_ref, vpad_ref):
        vpad_ref[0:CONV_PAD, :] = jnp.zeros((CONV_PAD, cb), F32)

        def glu(r, carry):
            r0 = pl.multiple_of(r * CHUNK, CHUNK)
            rows = pl.ds(r0, CHUNK)
            v = (pa_ref[rows, :] + ba_ref[...]) * _sigmoid(pg_ref[rows, :] + bg_ref[...])
            v_ref[rows, :] = v
            vpad_ref[pl.ds(r0 + CONV_PAD, CHUNK), :] = v
            return carry

        lax.fori_loop(0, n_chunks, glu, 0)

        def conv(r, carry):
            r0 = pl.multiple_of(r * CHUNK, CHUNK)
            win = vpad_ref[pl.ds(r0, CHUNK + CONV_PAD), :]
            acc = jnp.broadcast_to(b_ref[...], (CHUNK, cb))
            for t in range(CONV_W):
                off = CONV_PAD - (CONV_W - 1) + t
                acc = acc + w_ref[t:t + 1, :] * win[off:off + CHUNK, :]
            c_ref[pl.ds(r0, CHUNK), :] = acc
            return carry

        lax.fori_loop(0, n_chunks, conv, 0)

    return pl.pallas_call(
        body,
        out_shape=(jax.ShapeDtypeStruct((seq, ch), F32), jax.ShapeDtypeStruct((seq, ch), F32)),
        grid=(ch // cb,),
        in_specs=[_col(seq, cb), _col(seq, cb), _col(1, cb), _col(1, cb), _col(CONV_PAD, cb), _col(1, cb)],
        out_specs=(_col(seq, cb), _col(seq, cb)),
        scratch_shapes=[pltpu.VMEM((seq + CONV_PAD, cb), F32)],
        compiler_params=_params(("parallel",)), name="conv_glu_fwd",
    )(pa, pg, ba, bg, w_dw, b_dw)


def _conv_glu_bwd(dc, v, pa, pg, ba, bg, w_dw):
    seq, ch = dc.shape
    cb = LANE
    n_chunks = seq // CHUNK

    def body(dc_ref, v_ref, pa_ref, pg_ref, ba_ref, bg_ref, w_ref, da_ref, dg_ref, dw_ref, dbdw_ref, dba_ref, dbg_ref,
             dcpad_ref, vpad_ref):
        vpad_ref[0:CONV_PAD, :] = jnp.zeros((CONV_PAD, cb), F32)
        dcpad_ref[seq:seq + CONV_PAD, :] = jnp.zeros((CONV_PAD, cb), F32)
        dw_ref[...] = jnp.zeros((CONV_PAD, cb), F32)
        dbdw_ref[...] = jnp.zeros((1, cb), F32)
        dba_ref[...] = jnp.zeros((1, cb), F32)
        dbg_ref[...] = jnp.zeros((1, cb), F32)

        def fill(r, carry):
            r0 = pl.multiple_of(r * CHUNK, CHUNK)
            vpad_ref[pl.ds(r0 + CONV_PAD, CHUNK), :] = v_ref[pl.ds(r0, CHUNK), :]
            dcpad_ref[pl.ds(r0, CHUNK), :] = dc_ref[pl.ds(r0, CHUNK), :]
            return carry

        lax.fori_loop(0, n_chunks, fill, 0)

        def step(r, carry):
            r0 = pl.multiple_of(r * CHUNK, CHUNK)
            rows = pl.ds(r0, CHUNK)
            dwin = dcpad_ref[pl.ds(r0, CHUNK + CONV_PAD), :]
            vwin = vpad_ref[pl.ds(r0, CHUNK + CONV_PAD), :]
            dcur = dwin[0:CHUNK, :]
            dv = jnp.zeros((CHUNK, cb), F32)
            for t in range(CONV_W):
                fwd_off = CONV_W - 1 - t
                dv = dv + w_ref[t:t + 1, :] * dwin[fwd_off:fwd_off + CHUNK, :]
                off = CONV_PAD - (CONV_W - 1) + t
                dw_ref[t:t + 1, :] += jnp.sum(dcur * vwin[off:off + CHUNK, :], axis=0, keepdims=True)
            a = pa_ref[rows, :] + ba_ref[...]
            sg = _sigmoid(pg_ref[rows, :] + bg_ref[...])
            da = dv * sg
            dgate = dv * a * sg * (1.0 - sg)
            da_ref[rows, :] = da.astype(BF16)
            dg_ref[rows, :] = dgate.astype(BF16)
            dbdw_ref[...] += jnp.sum(dcur, axis=0, keepdims=True)
            dba_ref[...] += jnp.sum(da, axis=0, keepdims=True)
            dbg_ref[...] += jnp.sum(dgate, axis=0, keepdims=True)
            return carry

        lax.fori_loop(0, n_chunks, step, 0)

    return pl.pallas_call(
        body,
        out_shape=(jax.ShapeDtypeStruct((seq, ch), BF16), jax.ShapeDtypeStruct((seq, ch), BF16),
                   jax.ShapeDtypeStruct((CONV_PAD, ch), F32), jax.ShapeDtypeStruct((1, ch), F32),
                   jax.ShapeDtypeStruct((1, ch), F32), jax.ShapeDtypeStruct((1, ch), F32)),
        grid=(ch // cb,),
        in_specs=[_col(seq, cb), _col(seq, cb), _col(seq, cb), _col(seq, cb), _col(1, cb), _col(1, cb), _col(CONV_PAD, cb)],
        out_specs=(_col(seq, cb), _col(seq, cb), _col(CONV_PAD, cb), _col(1, cb), _col(1, cb), _col(1, cb)),
        scratch_shapes=[pltpu.VMEM((seq + CONV_PAD, cb), F32), pltpu.VMEM((seq + CONV_PAD, cb), F32)],
        compiler_params=_params(("parallel",)), name="conv_glu_bwd",
    )(dc, v, pa, pg, ba, bg, w_dw)


def _ffn_conv(pad_ref, r0, w_ref, b_ref, cb):
    win = pad_ref[pl.ds(r0, CHUNK + FFN_PAD), :]
    y = jnp.broadcast_to(b_ref[...], (CHUNK, cb))
    for t in range(FFN_CONV_W):
        off = FFN_PAD - (FFN_CONV_W - 1) + t
        y = y + w_ref[t:t + 1, :] * win[off:off + CHUNK, :]
    return y, win


def _ffn_mid_fwd(ug, uv, wg, wv, bg, bv):
    seq, ch = ug.shape
    cb = _tile(ch, 256)
    n_chunks = seq // CHUNK

    def body(ug_ref, uv_ref, wg_ref, wv_ref, bg_ref, bv_ref, z_ref, gpad_ref, vpad_ref):
        gpad_ref[0:FFN_PAD, :] = jnp.zeros((FFN_PAD, cb), F32)
        vpad_ref[0:FFN_PAD, :] = jnp.zeros((FFN_PAD, cb), F32)

        def fill(r, carry):
            r0 = pl.multiple_of(r * CHUNK, CHUNK)
            gpad_ref[pl.ds(r0 + FFN_PAD, CHUNK), :] = ug_ref[pl.ds(r0, CHUNK), :]
            vpad_ref[pl.ds(r0 + FFN_PAD, CHUNK), :] = uv_ref[pl.ds(r0, CHUNK), :]
            return carry

        lax.fori_loop(0, n_chunks, fill, 0)

        def step(r, carry):
            r0 = pl.multiple_of(r * CHUNK, CHUNK)
            yg, _ = _ffn_conv(gpad_ref, r0, wg_ref, bg_ref, cb)
            yv, _ = _ffn_conv(vpad_ref, r0, wv_ref, bv_ref, cb)
            z_ref[pl.ds(r0, CHUNK), :] = (yg * _sigmoid(yg) * yv).astype(BF16)
            return carry

        lax.fori_loop(0, n_chunks, step, 0)

    return pl.pallas_call(
        body, out_shape=jax.ShapeDtypeStruct((seq, ch), BF16), grid=(ch // cb,),
        in_specs=[_col(seq, cb), _col(seq, cb), _col(FFN_PAD, cb), _col(FFN_PAD, cb), _col(1, cb), _col(1, cb)],
        out_specs=_col(seq, cb),
        scratch_shapes=[pltpu.VMEM((seq + FFN_PAD, cb), F32), pltpu.VMEM((seq + FFN_PAD, cb), F32)],
        compiler_params=_params(("parallel",)), name="ffn_mid_fwd",
    )(ug, uv, wg, wv, bg, bv)


def _ffn_mid_bwd(ug, uv, dz, wg, wv, bg, bv):
    seq, ch = ug.shape
    cb = _tile(ch, 256)
    n_chunks = seq // CHUNK

    def body(ug_ref, uv_ref, dz_ref, wg_ref, wv_ref, bg_ref, bv_ref, dug_ref, duv_ref, dwg_ref, dwv_ref, dbg_ref, dbv_ref,
             gpad_ref, vpad_ref, dyg_ref, dyv_ref):
        gpad_ref[0:FFN_PAD, :] = jnp.zeros((FFN_PAD, cb), F32)
        vpad_ref[0:FFN_PAD, :] = jnp.zeros((FFN_PAD, cb), F32)
        dyg_ref[seq:seq + FFN_PAD, :] = jnp.zeros((FFN_PAD, cb), F32)
        dyv_ref[seq:seq + FFN_PAD, :] = jnp.zeros((FFN_PAD, cb), F32)
        dwg_ref[...] = jnp.zeros((FFN_PAD, cb), F32)
        dwv_ref[...] = jnp.zeros((FFN_PAD, cb), F32)
        dbg_ref[...] = jnp.zeros((1, cb), F32)
        dbv_ref[...] = jnp.zeros((1, cb), F32)

        def fill(r, carry):
            r0 = pl.multiple_of(r * CHUNK, CHUNK)
            gpad_ref[pl.ds(r0 + FFN_PAD, CHUNK), :] = ug_ref[pl.ds(r0, CHUNK), :]
            vpad_ref[pl.ds(r0 + FFN_PAD, CHUNK), :] = uv_ref[pl.ds(r0, CHUNK), :]
            return carry

        lax.fori_loop(0, n_chunks, fill, 0)

        def grads_of_conv_out(r, carry):
            r0 = pl.multiple_of(r * CHUNK, CHUNK)
            rows = pl.ds(r0, CHUNK)
            yg, gwin = _ffn_conv(gpad_ref, r0, wg_ref, bg_ref, cb)
            yv, vwin = _ffn_conv(vpad_ref, r0, wv_ref, bv_ref, cb)
            dzv = dz_ref[rows, :]
            sg = _sigmoid(yg)
            dyg = dzv * yv * (sg * (1.0 + yg * (1.0 - sg)))
            dyv = dzv * yg * sg
            dyg_ref[rows, :] = dyg
            dyv_ref[rows, :] = dyv
            for t in range(FFN_CONV_W):
                off = FFN_PAD - (FFN_CONV_W - 1) + t
                dwg_ref[t:t + 1, :] += jnp.sum(dyg * gwin[off:off + CHUNK, :], axis=0, keepdims=True)
                dwv_ref[t:t + 1, :] += jnp.sum(dyv * vwin[off:off + CHUNK, :], axis=0, keepdims=True)
            dbg_ref[...] += jnp.sum(dyg, axis=0, keepdims=True)
            dbv_ref[...] += jnp.sum(dyv, axis=0, keepdims=True)
            return carry

        lax.fori_loop(0, n_chunks, grads_of_conv_out, 0)

        def grads_of_conv_in(r, carry):
            r0 = pl.multiple_of(r * CHUNK, CHUNK)
            gwin = dyg_ref[pl.ds(r0, CHUNK + FFN_PAD), :]
            vwin = dyv_ref[pl.ds(r0, CHUNK + FFN_PAD), :]
            dug = jnp.zeros((CHUNK, cb), F32)
            duv = jnp.zeros((CHUNK, cb), F32)
            for t in range(FFN_CONV_W):
                off = FFN_CONV_W - 1 - t
                dug = dug + wg_ref[t:t + 1, :] * gwin[off:off + CHUNK, :]
                duv = duv + wv_ref[t:t + 1, :] * vwin[off:off + CHUNK, :]
            dug_ref[pl.ds(r0, CHUNK), :] = dug.astype(BF16)
            duv_ref[pl.ds(r0, CHUNK), :] = duv.astype(BF16)
            return carry

        lax.fori_loop(0, n_chunks, grads_of_conv_in, 0)

    return pl.pallas_call(
        body,
        out_shape=(jax.ShapeDtypeStruct((seq, ch), BF16), jax.ShapeDtypeStruct((seq, ch), BF16),
                   jax.ShapeDtypeStruct((FFN_PAD, ch), F32), jax.ShapeDtypeStruct((FFN_PAD, ch), F32),
                   jax.ShapeDtypeStruct((1, ch), F32), jax.ShapeDtypeStruct((1, ch), F32)),
        grid=(ch // cb,),
        in_specs=[_col(seq, cb), _col(seq, cb), _col(seq, cb), _col(FFN_PAD, cb), _col(FFN_PAD, cb), _col(1, cb), _col(1, cb)],
        out_specs=(_col(seq, cb), _col(seq, cb), _col(FFN_PAD, cb), _col(FFN_PAD, cb), _col(1, cb), _col(1, cb)),
        scratch_shapes=[pltpu.VMEM((seq + FFN_PAD, cb), F32) for _ in range(4)],
        compiler_params=_params(("parallel",)), name="ffn_mid_bwd",
    )(ug, uv, dz, wg, wv, bg, bv)


def _dot(a, b, mode):
    return lax.dot_general(a.astype(BF16), b.astype(BF16), _DIMS[mode], preferred_element_type=F32)


def _head(h):
    return slice(h * HEAD_DIM, (h + 1) * HEAD_DIM)


def _mem_softmax(q, k):
    s = _dot(q, k, "nt") * ATT_SCALE
    e = jnp.exp(s - jnp.max(s, axis=-1, keepdims=True))
    return e / jnp.sum(e, axis=-1, keepdims=True)


def _memattn_fwd(pq, mk, mv):
    seq, w = pq.shape
    m = mk.shape[0]

    def body(q_ref, k_ref, v_ref, o_ref):
        for h in range(N_MEM_HEADS):
            p = _mem_softmax(q_ref[:, _head(h)], k_ref[:, _head(h)])
            o_ref[:, _head(h)] = _dot(p, v_ref[:, _head(h)], "nn").astype(BF16)

    full = pl.BlockSpec((m, w), lambda i: (0, 0))
    return pl.pallas_call(
        body, out_shape=jax.ShapeDtypeStruct((seq, w), BF16), grid=(seq // ROW_TILE,),
        in_specs=[_row(w), full, full], out_specs=_row(w), compiler_params=_params(("parallel",)), name="memattn_fwd",
    )(pq, mk, mv)


def _memattn_bwd(pq, mk, mv, dmo):
    seq, w = pq.shape
    m = mk.shape[0]

    def body(q_ref, k_ref, v_ref, do_ref, dq_ref, dk_ref, dv_ref):
        step = pl.program_id(0)

        @pl.when(step == 0)
        def _():
            dk_ref[...] = jnp.zeros((m, w), F32)
            dv_ref[...] = jnp.zeros((m, w), F32)

        for h in range(N_MEM_HEADS):
            q, k, v, do = q_ref[:, _head(h)], k_ref[:, _head(h)], v_ref[:, _head(h)], do_ref[:, _head(h)]
            p = _mem_softmax(q, k)
            dp = _dot(do, v, "nt")
            ds = p * (dp - jnp.sum(dp * p, axis=-1, keepdims=True))
            dq_ref[:, _head(h)] = (_dot(ds, k, "nn") * ATT_SCALE).astype(BF16)
            dk_ref[:, _head(h)] += _dot(ds, q, "tn") * ATT_SCALE
            dv_ref[:, _head(h)] += _dot(p, do, "tn")

    full = pl.BlockSpec((m, w), lambda i: (0, 0))
    return pl.pallas_call(
        body,
        out_shape=(jax.ShapeDtypeStruct((seq, w), BF16), jax.ShapeDtypeStruct((m, w), F32), jax.ShapeDtypeStruct((m, w), F32)),
        grid=(seq // ROW_TILE,), in_specs=[_row(w), full, full, _row(w)], out_specs=(_row(w), full, full),
        compiler_params=_params(("arbitrary",)), name="memattn_bwd",
    )(pq, mk, mv, dmo)


FWD_KEY_TILE = 512
BWD_KEY_TILE = 512


def _causal_t(s_t, q_blk, k_blk, key_tile):
    kpos = k_blk * key_tile + lax.broadcasted_iota(jnp.int32, (key_tile, ROW_TILE), 0)
    qpos = q_blk * ROW_TILE + lax.broadcasted_iota(jnp.int32, (key_tile, ROW_TILE), 1)
    return jnp.where(kpos <= qpos, s_t, NEG_BIG)


def _fox_fwd(q, k, v, cum_cols, cum_rows):
    seq, w = q.shape
    nq = seq // ROW_TILE

    def body(q_ref, k_ref, v_ref, cc_ref, cr_ref, o_ref, ob_ref, lse_ref):
        i = pl.program_id(1)
        qs = [q_ref[:, _head(h)].astype(BF16) for h in range(2)]
        crs = [cr_ref[0, 0, h:h + 1, :] for h in range(2)]

        def step(j, carry):
            rows = pl.ds(pl.multiple_of(j * FWD_KEY_TILE, FWD_KEY_TILE), FWD_KEY_TILE)
            new = []
            for h in range(2):
                m_run, l_run, acc = carry[h]
                s_t = _dot(k_ref[rows, _head(h)], qs[h], "nt") * ATT_SCALE + crs[h] - cc_ref[0, rows, h * HEAD_DIM:h * HEAD_DIM + 1]
                s_t = _causal_t(s_t, i, j, FWD_KEY_TILE)
                m_new = jnp.maximum(m_run, jnp.max(s_t, axis=0, keepdims=True))
                alpha = jnp.exp(m_run - m_new)
                p_t = jnp.exp(s_t - m_new)
                l_new = alpha * l_run + jnp.sum(p_t, axis=0, keepdims=True)
                new.append((m_new, l_new, alpha * acc + _dot(v_ref[rows, _head(h)], p_t, "tn")))
            return tuple(new)

        one_head = (jnp.full((1, ROW_TILE), NEG_BIG, F32), jnp.zeros((1, ROW_TILE), F32), jnp.zeros((HEAD_DIM, ROW_TILE), F32))
        res = lax.fori_loop(0, ((i + 1) * ROW_TILE + FWD_KEY_TILE - 1) // FWD_KEY_TILE, step, (one_head, one_head))
        o = jnp.concatenate([acc / l_run for _, l_run, acc in res], axis=0).T
        o_ref[...] = o
        ob_ref[...] = o.astype(BF16)
        lse_ref[...] = jnp.zeros((1, 1, 8, ROW_TILE), F32)
        for h in range(2):
            lse_ref[0, 0, h:h + 1, :] = res[h][0] + jnp.log(res[h][1])

    blk = pl.BlockSpec((ROW_TILE, LANE), lambda hp, i: (i, hp))
    full = pl.BlockSpec((seq, LANE), lambda hp, i: (0, hp))
    cols = pl.BlockSpec((1, seq, LANE), lambda hp, i: (hp, 0, 0))
    rows = pl.BlockSpec((1, 1, 8, ROW_TILE), lambda hp, i: (hp, i, 0, 0))
    return pl.pallas_call(
        body,
        out_shape=(jax.ShapeDtypeStruct((seq, w), F32), jax.ShapeDtypeStruct((seq, w), BF16),
                   jax.ShapeDtypeStruct((N_HEAD_PAIRS, nq, 8, ROW_TILE), F32)),
        grid=(N_HEAD_PAIRS, nq), in_specs=[blk, full, full, cols, rows], out_specs=(blk, blk, rows),
        compiler_params=_params(("parallel", "parallel")), name="fox_fwd",
    )(q, k, v, cum_cols, cum_rows)


def _fox_bwd(q, k, v, cum_cols, cum_rows, o, lse, do):
    seq, w = q.shape
    nq = seq // ROW_TILE
    nk = seq // BWD_KEY_TILE

    def body(q_ref, k_ref, v_ref, cc_ref, cr_ref, o_ref, lse_ref, do_ref, dq_ref, dk_ref, dv_ref, dcc_ref, dcr_ref):
        kj = pl.program_id(1)

        @pl.when(kj == 0)
        def _():
            dq_ref[...] = jnp.zeros((seq, LANE), F32)
            dcr_ref[...] = jnp.zeros((1, nq, 8, ROW_TILE), F32)

        ks = [k_ref[:, _head(h)].astype(BF16) for h in range(2)]
        vs = [v_ref[:, _head(h)].astype(BF16) for h in range(2)]
        ccs = [cc_ref[0, :, h * HEAD_DIM:h * HEAD_DIM + 1] for h in range(2)]
        ones = jnp.ones((8, HEAD_DIM), BF16)

        def step(i, carry):
            rows = pl.ds(pl.multiple_of(i * ROW_TILE, ROW_TILE), ROW_TILE)
            new = []
            for h in range(2):
                dk, dv, fold = carry[h]
                qh = q_ref[rows, _head(h)].astype(BF16)
                doh = do_ref[rows, _head(h)]
                s_t = _dot(ks[h], qh, "nt") * ATT_SCALE + cr_ref[0, i, h:h + 1, :] - ccs[h]
                p_t = jnp.exp(_causal_t(s_t, i, kj, BWD_KEY_TILE) - lse_ref[0, i, h:h + 1, :])
                dp_t = _dot(vs[h], doh, "nt")
                hi, mid, lo = _split3(doh * o_ref[rows, _head(h)])
                row_sum = lambda part: lax.dot_general(ones, part, _DIMS["nt"], preferred_element_type=F32)
                delta = ((row_sum(lo) + row_sum(mid)) + row_sum(hi))[0:1, :]
                ds_t = p_t * (dp_t - delta)
                dq_ref[rows, _head(h)] += _dot(ds_t, ks[h], "tn") * ATT_SCALE
                dcr_ref[0, i, h:h + 1, :] += jnp.sum(ds_t, axis=0, keepdims=True)
                new.append((dk + _dot(ds_t, qh, "nn") * ATT_SCALE, dv + _dot(p_t, doh, "nn"), fold + (ds_t[:, :LANE] + ds_t[:, LANE:])))
            return tuple(new)

        one_head = (jnp.zeros((BWD_KEY_TILE, HEAD_DIM), F32), jnp.zeros((BWD_KEY_TILE, HEAD_DIM), F32), jnp.zeros((BWD_KEY_TILE, LANE), F32))
        res = lax.fori_loop(kj * BWD_KEY_TILE // ROW_TILE, nq, step, (one_head, one_head))
        for h in range(2):
            dk_ref[:, _head(h)] = res[h][0]
            dv_ref[:, _head(h)] = res[h][1]
            dcc_ref[0, :, _head(h)] = jnp.broadcast_to(-jnp.sum(res[h][2], axis=-1, keepdims=True), (BWD_KEY_TILE, HEAD_DIM))

    blk = pl.BlockSpec((BWD_KEY_TILE, LANE), lambda hp, j: (j, hp))
    full = pl.BlockSpec((seq, LANE), lambda hp, j: (0, hp))
    cols = pl.BlockSpec((1, BWD_KEY_TILE, LANE), lambda hp, j: (hp, j, 0))
    rows = pl.BlockSpec((1, nq, 8, ROW_TILE), lambda hp, j: (hp, 0, 0, 0))
    return pl.pallas_call(
        body,
        out_shape=(jax.ShapeDtypeStruct((seq, w), F32), jax.ShapeDtypeStruct((seq, w), F32), jax.ShapeDtypeStruct((seq, w), F32),
                   jax.ShapeDtypeStruct((N_HEAD_PAIRS, seq, LANE), F32), jax.ShapeDtypeStruct((N_HEAD_PAIRS, nq, 8, ROW_TILE), F32)),
        grid=(N_HEAD_PAIRS, nk), in_specs=[full, blk, blk, cols, rows, full, rows, full],
        out_specs=(full, blk, blk, cols, rows),
        compiler_params=_params(("parallel", "arbitrary")), name="fox_bwd",
    )(q, k, v, cum_cols, cum_rows, o, lse, do)


def _split3(x):
    hi = x.astype(BF16)
    r1 = x - hi.astype(F32)
    mid = r1.astype(BF16)
    lo = (r1 - mid.astype(F32)).astype(BF16)
    return hi, mid, lo


def _tri_sum(tri, x):
    hi, mid, lo = _split3(x)
    dot = lambda p: lax.dot_general(tri, p, _DIMS["nn"], preferred_element_type=F32)
    return (dot(lo) + dot(mid)) + dot(hi)


def _tri(lower):
    r = lax.broadcasted_iota(jnp.int32, (LANE, LANE), 0)
    c = lax.broadcasted_iota(jnp.int32, (LANE, LANE), 1)
    return jnp.where((c <= r) if lower else (c >= r), 1.0, 0.0).astype(BF16)


def _fgate_fwd(fl, bf):
    seq, w = fl.shape
    nb = seq // LANE

    def body(f_ref, b_ref, cum_ref):
        tri = _tri(True)

        def step(i, carry):
            rows = pl.ds(pl.multiple_of(i * LANE, LANE), LANE)
            x = f_ref[rows, :] + b_ref[...]
            logsig = jnp.minimum(x, 0.0) - jnp.log(1.0 + jnp.exp(-jnp.abs(x)))
            cum = _tri_sum(tri, logsig) + carry
            cum_ref[rows, :] = cum
            return cum[LANE - 1:LANE, :]

        lax.fori_loop(0, nb, step, jnp.zeros((1, w), F32))

    return pl.pallas_call(body, out_shape=jax.ShapeDtypeStruct((seq, w), F32), compiler_params=_params(), name="fgate_fwd")(fl, bf)


def _fgate_bwd(fl, bf, dcum):
    seq, w = fl.shape
    nb = seq // LANE

    def body(f_ref, b_ref, dc_ref, df_ref, db_ref):
        tri = _tri(False)

        def step(i, carry):
            tail, db = carry
            rows = pl.ds(pl.multiple_of((nb - 1 - i) * LANE, LANE), LANE)
            suffix = _tri_sum(tri, dc_ref[rows, :]) + tail
            df = suffix * (1.0 - _sigmoid(f_ref[rows, :] + b_ref[...]))
            df_ref[rows, :] = df
            return suffix[0:1, :], db + jnp.sum(df, axis=0, keepdims=True)

        _, db = lax.fori_loop(0, nb, step, (jnp.zeros((1, w), F32), jnp.zeros((1, w), F32)))
        db_ref[...] = db

    return pl.pallas_call(
        body, out_shape=(jax.ShapeDtypeStruct((seq, w), F32), jax.ShapeDtypeStruct((1, w), F32)),
        compiler_params=_params(), name="fgate_bwd",
    )(fl, bf, dcum)


def _cum_layouts(cum):
    seq = cum.shape[0]
    nblk = seq // ROW_TILE
    heads = cum[:, :N_FOX_HEADS]
    cq = jnp.repeat(heads, HEAD_DIM, axis=1).reshape(seq, N_HEAD_PAIRS, LANE).transpose(1, 0, 2)
    ck = heads.T.reshape(N_HEAD_PAIRS, 2, nblk, ROW_TILE).transpose(0, 2, 1, 3)
    ck = jnp.pad(ck, ((0, 0), (0, 0), (0, 6), (0, 0)))
    return cq, ck


def _dcum_from_layouts(dcq, dck):
    seq = dcq.shape[1]
    key_side = dck[:, :, :2, :].transpose(0, 2, 1, 3).reshape(N_FOX_HEADS, seq).T
    query_side = dcq[:, :, ::HEAD_DIM].transpose(1, 0, 2).reshape(seq, N_FOX_HEADS)
    return jnp.pad(key_side + query_side, ((0, 0), (0, F_PAD - N_FOX_HEADS)))


def _local_step(x, mem, target, w, arrive):
    vec = lambda a: a.reshape(1, -1)
    mem_n = _rms_fwd(mem, vec(w["g_mem"]))
    saved = []
    shared = None
    for l in range(N_LAYERS):
        arrive(l, x)
        s = {"x_in": x}
        s["h"] = h = _rms_fwd(x, vec(w["g_mix"][l]))
        s["mk"] = mk = _mm(mem_n, w["w_mk"][l])
        s["mv"] = mv = _mm(mem_n, w["w_mv"][l])
        if l < N_A:
            s["pa"] = pa = _mm(h, w["w_a"][l])
            s["pg"] = pg = _mm(h, w["w_g"][l])
            s["pq"] = pq = _mm(h, w["w_qm"][l])
            s["v"], s["c"] = _conv_glu_fwd(pa, pg, vec(w["b_glu_a"][l]), vec(w["b_glu_g"][l]), w["w_dw_a"][l], vec(w["b_dw_a"][l]))
            s["mix"] = mix = _ln_silu_fwd(s["c"], vec(w["ln_g"][l]), vec(w["ln_b"][l]))
        else:
            if l == N_A:
                shared = {"x_in": x}
                shared["hk"] = hk = _rms_fwd(x, vec(w["g_kv"]))
                shared["k"] = _mm(hk, w["w_k"])
                shared["v"] = _mm(hk, w["w_v"])
                shared["fl"] = _mm(hk, w["w_f"])
                cum = _fgate_fwd(shared["fl"], w["b_f"])
                shared["cq"], shared["ck"] = _cum_layouts(cum)
            s["q"] = q = _mm(h, w["w_q"][l - N_A])
            s["pq"] = pq = _mm(h, w["w_qm"][l])
            s["o"], mix, s["lse"] = _fox_fwd(q, shared["k"], shared["v"], shared["cq"], shared["ck"])
            s["mix"] = mix
        s["mo"] = mo = _memattn_fwd(pq, mk, mv)
        x = _mm(mix, w["w_o_mix"][l], add=x)
        s["x_mid"] = x = _mm(mo, w["w_o_mem"][l], add=x)
        s["hf"] = hf = _rms_fwd(x, vec(w["g_ffn"][l]))
        s["ug"] = ug = _mm(hf, w["w_up_g"][l])
        s["uv"] = uv = _mm(hf, w["w_up_v"][l])
        s["z"] = z = _ffn_mid_fwd(ug, uv, w["w_dw_f_g"][l], w["w_dw_f_v"][l], vec(w["b_dw_f_g"][l]), vec(w["b_dw_f_v"][l]))
        x = _mm(z, w["w_down"][l], add=x)
        saved.append(s)

    loss_row, dx, dxb, dg_final = _loss_head(x, vec(w["g_final"]), target)
    loss = loss_row[0, 0]

    g = {n: [None] * N_LAYERS for n in ("g_mix", "w_mem_kv", "w_out", "g_ffn", "w_up", "w_dw_f", "b_dw_f", "w_down")}
    for n in ("w_in_a", "b_glu", "w_dw_a", "b_dw_a", "ln_g", "ln_b", "w_in_b"):
        g[n] = [None] * N_A
    one = lambda arr: ([arr], 0)
    g["g_final"] = [one(dg_final[0])]
    dmem_n = None
    dk_sum = dv_sum = dcq_sum = dck_sum = None
    for l in reversed(range(N_LAYERS)):
        s = saved[l]
        dz = _mm(dxb, w["w_down"][l], "nt")
        g["w_down"][l] = one(_mm(s["z"], dxb, "tn", BF16))
        dug, duv, dwg, dwv, dbg, dbv = _ffn_mid_bwd(s["ug"], s["uv"], dz, w["w_dw_f_g"][l], w["w_dw_f_v"][l],
                                                    vec(w["b_dw_f_g"][l]), vec(w["b_dw_f_v"][l]))
        g["w_up"][l] = ([_mm(s["hf"], dug, "tn", BF16), _mm(s["hf"], duv, "tn", BF16)], 1)
        g["w_dw_f"][l] = ([dwg[:FFN_CONV_W], dwv[:FFN_CONV_W]], 1)
        g["b_dw_f"][l] = ([dbg[0], dbv[0]], 0)
        dhf = _mm(duv, w["w_up_v"][l], "nt", add=_mm(dug, w["w_up_g"][l], "nt"))
        dx, dxb, dg_ffn = _rms_bwd(s["x_mid"], vec(w["g_ffn"][l]), dhf, dx)
        g["g_ffn"][l] = one(dg_ffn[0])
        dmix = _mm(dxb, w["w_o_mix"][l], "nt")
        dmo = _mm(dxb, w["w_o_mem"][l], "nt")
        g["w_out"][l] = ([_mm(s["mix"], dxb, "tn", BF16), _mm(s["mo"], dxb, "tn", BF16)], 0)
        dpq, dmk, dmv = _memattn_bwd(s["pq"], s["mk"], s["mv"], dmo)
        g["w_mem_kv"][l] = ([_mm(mem_n, dmk, "tn", BF16), _mm(mem_n, dmv, "tn", BF16)], 1)
        dmem_n = _mm(dmk, w["w_mk"][l], "nt", add=dmem_n)
        dmem_n = _mm(dmv, w["w_mv"][l], "nt", add=dmem_n)
        if l < N_A:
            dc, dlng, dlnb = _ln_silu_bwd(dmix, s["c"], vec(w["ln_g"][l]), vec(w["ln_b"][l]))
            da, dgate, dwdw, dbdw, dba, dbg2 = _conv_glu_bwd(dc, s["v"], s["pa"], s["pg"], vec(w["b_glu_a"][l]),
                                                              vec(w["b_glu_g"][l]), w["w_dw_a"][l])
            g["ln_g"][l], g["ln_b"][l], g["b_dw_a"][l] = one(dlng[0]), one(dlnb[0]), one(dbdw[0])
            g["w_dw_a"][l] = ([dwdw[:CONV_W]], 1)
            g["b_glu"][l] = ([dba[0], dbg2[0]], 0)
            g["w_in_a"][l] = ([_mm(s["h"], da, "tn", BF16), _mm(s["h"], dgate, "tn", BF16), _mm(s["h"], dpq, "tn", BF16)], 1)
            dh = _mm(da, w["w_a"][l], "nt")
            dh = _mm(dgate, w["w_g"][l], "nt", add=dh)
            dh = _mm(dpq, w["w_qm"][l], "nt", add=dh)
        else:
            dq, dk, dv, dcq, dck = _fox_bwd(s["q"], shared["k"], shared["v"], shared["cq"], shared["ck"], s["o"], s["lse"], dmix)
            dk_sum = dk if dk_sum is None else dk_sum + dk
            dv_sum = dv if dv_sum is None else dv_sum + dv
            dcq_sum = dcq if dcq_sum is None else dcq_sum + dcq
            dck_sum = dck if dck_sum is None else dck_sum + dck
            g["w_in_b"][l - N_A] = ([_mm(s["h"], dq, "tn", BF16), _mm(s["h"], dpq, "tn", BF16)], 1)
            dh = _mm(dq, w["w_q"][l - N_A], "nt")
            dh = _mm(dpq, w["w_qm"][l], "nt", add=dh)
        dx, dxb, dg_mix = _rms_bwd(s["x_in"], vec(w["g_mix"][l]), dh, dx)
        g["g_mix"][l] = one(dg_mix[0])
        if l == N_A:
            df, dbf = _fgate_bwd(shared["fl"], w["b_f"], _dcum_from_layouts(dcq_sum, dck_sum))
            hk = shared["hk"]
            g["w_kvf"] = [([_mm(hk, dk_sum, "tn", BF16), _mm(hk, dv_sum, "tn", BF16), _mm(hk, df, "tn", BF16)[:, :N_FOX_HEADS]], 1)]
            g["b_f"] = [one(dbf[0, :N_FOX_HEADS])]
            dhk = _mm(dk_sum, w["w_k"], "nt")
            dhk = _mm(dv_sum, w["w_v"], "nt", add=dhk)
            dhk = _mm(df, w["w_f"], "nt", add=dhk)
            dx, dxb, dg_kv = _rms_bwd(shared["x_in"], vec(w["g_kv"]), dhk, dx)
            g["g_kv"] = [one(dg_kv[0])]
    _, _, dg_mem = _rms_bwd(mem, vec(w["g_mem"]), dmem_n, None)
    g["g_mem"] = [one(dg_mem[0])]
    return loss, dx, g


SHARDED = {
    "w_in_a": ((N_A, D_MODEL, 2 * CONV_CH + MEM_W), 2),
    "b_glu": ((N_A, 2 * CONV_CH), 1),
    "w_dw_a": ((N_A, CONV_W, CONV_CH), 2),
    "b_dw_a": ((N_A, CONV_CH), 1),
    "ln_g": ((N_A, CONV_CH), 1),
    "ln_b": ((N_A, CONV_CH), 1),
    "w_kvf": ((D_MODEL, 2 * CONV_CH + N_FOX_HEADS), 1),
    "w_in_b": ((N_LAYERS - N_A, D_MODEL, CONV_CH + MEM_W), 1),
    "w_mem_kv": ((N_LAYERS, D_MODEL, 2 * MEM_W), 1),
    "w_out": ((N_LAYERS, D_MODEL, D_MODEL), 1),
    "w_up": ((N_LAYERS, D_MODEL, 2 * D_FF), 2),
    "w_dw_f": ((N_LAYERS, FFN_CONV_W, 2 * D_FF), 2),
    "w_down": ((N_LAYERS, D_FF, D_MODEL), 1),
}
REPLICATED = {
    "g_mix": (N_LAYERS, D_MODEL), "g_kv": (D_MODEL,), "b_f": (N_FOX_HEADS,), "g_mem": (D_MODEL,),
    "g_ffn": (N_LAYERS, D_MODEL), "b_dw_f": (N_LAYERS, 2 * D_FF), "g_final": (D_MODEL,),
}
WEIGHT_ORDER = ["g_mix", "w_in_a", "b_glu", "w_dw_a", "b_dw_a", "ln_g", "ln_b", "g_kv", "w_kvf", "b_f", "w_in_b", "g_mem",
                "w_mem_kv", "w_out", "g_ffn", "w_up", "w_dw_f", "b_dw_f", "w_down", "g_final"]


def _shard_shape(name):
    shape, axis = SHARDED[name]
    return tuple(d // N_CHIPS if i == axis else d for i, d in enumerate(shape))


def _size(shape):
    n = 1
    for d in shape:
        n *= d
    return n


VECTOR_LEAVES = ("b_glu", "w_dw_a", "b_dw_a", "ln_g", "ln_b", "w_dw_f")
MATRIX_LEAVES = tuple(n for n in SHARDED if n not in VECTOR_LEAVES)
_HALF_ROW_TILE = 16
_VECTOR_ELEMS = sum(_size(_shard_shape(n)) for n in VECTOR_LEAVES)
VECTOR_ROWS = -(-_VECTOR_ELEMS // (FLAT_COLS * 2 * _HALF_ROW_TILE)) * 2 * _HALF_ROW_TILE
REP_ELEMS = sum(_size(s) for s in REPLICATED.values())
REP_ROWS = 8
REP_COLS = -(-REP_ELEMS // (REP_ROWS * LANE)) * LANE


def _half_shape(name):
    shp = _shard_shape(name)
    assert shp[0] % 2 == 0 and (_size(shp) // 2 // shp[-1]) % _HALF_ROW_TILE == 0, name
    return (2, _size(shp) // 2 // shp[-1], shp[-1])


def _flatten_vectors(parts, dtype):
    flat = jnp.concatenate([p.reshape(-1).astype(dtype) for p in parts])
    return jnp.pad(flat, (0, VECTOR_ROWS * FLAT_COLS - _VECTOR_ELEMS)).reshape(2, VECTOR_ROWS // 2, FLAT_COLS)


def _unflatten_vectors(flat, lead=()):
    flat = flat.reshape(lead + (-1,))
    out, off = {}, 0
    for n in VECTOR_LEAVES:
        shp = _shard_shape(n)
        out[n] = flat[..., off:off + _size(shp)].reshape(lead + shp)
        off += _size(shp)
    return out


def _span(parts, cat_axis, sel_axis, lo, hi):
    if sel_axis != cat_axis:
        return jnp.concatenate([lax.slice_in_dim(p, lo, hi, axis=sel_axis) for p in parts], axis=cat_axis)
    taken, off = [], 0
    for p in parts:
        n = p.shape[cat_axis]
        a, b = max(lo, off), min(hi, off + n)
        if a < b:
            taken.append(lax.slice_in_dim(p, a - off, b - off, axis=cat_axis))
        off += n
    return taken[0] if len(taken) == 1 else jnp.concatenate(taken, axis=cat_axis)


LAYER_LEAVES = tuple(
    ([("w_in_a", l)] if l < N_A else ([("w_kvf", None)] if l == N_A else []) + [("w_in_b", l - N_A)])
    + [("w_mem_kv", l), ("w_out", l), ("w_up", l), ("w_down", l)]
    for l in range(N_LAYERS))


def _layer_halves(weights, l):
    out = []
    for name, idx in LAYER_LEAVES[l]:
        shard = weights[name] if idx is None else weights[name][idx]
        rows, cols = shard.shape
        assert (rows // 2) % _HALF_ROW_TILE == 0, name
        out.append(shard.astype(BF16).reshape(2, rows // 2, cols))
    return out


def _layer_pieces(l, gathered, vectors):
    parts = {}
    for (name, idx), g in zip(LAYER_LEAVES[l], gathered):
        parts[name] = [g[t].reshape(-1, g.shape[-1]) for t in range(N_CHIPS)]

    def cut(name, sel_axis, lo, hi):
        axis = SHARDED[name][1] - (len(SHARDED[name][0]) - 2)
        return _span(parts[name], axis, sel_axis, lo, hi)

    def vcut(name, idx, sel_axis, lo, hi):
        axis = SHARDED[name][1] - 1
        return _span([vectors[name][t, idx] for t in range(N_CHIPS)], axis, sel_axis, lo, hi)

    def pad_rows(arr, rows):
        return jnp.pad(arr, ((0, rows - arr.shape[0]), (0, 0)))

    w = {}
    if l < N_A:
        w["w_a"] = cut("w_in_a", 1, 0, CONV_CH)
        w["w_g"] = cut("w_in_a", 1, CONV_CH, 2 * CONV_CH)
        w["w_qm"] = cut("w_in_a", 1, 2 * CONV_CH, 2 * CONV_CH + MEM_W)
        w["b_glu_a"] = vcut("b_glu", l, 0, 0, CONV_CH)
        w["b_glu_g"] = vcut("b_glu", l, 0, CONV_CH, 2 * CONV_CH)
        w["w_dw_a"] = pad_rows(vcut("w_dw_a", l, 1, 0, CONV_CH), CONV_PAD)
        for n in ("b_dw_a", "ln_g", "ln_b"):
            w[n] = vcut(n, l, 0, 0, CONV_CH)
    else:
        if l == N_A:
            w["w_k"] = cut("w_kvf", 1, 0, CONV_CH)
            w["w_v"] = cut("w_kvf", 1, CONV_CH, 2 * CONV_CH)
            w["w_f"] = jnp.pad(cut("w_kvf", 1, 2 * CONV_CH, 2 * CONV_CH + N_FOX_HEADS), ((0, 0), (0, F_PAD - N_FOX_HEADS)))
        w["w_q"] = cut("w_in_b", 1, 0, CONV_CH)
        w["w_qm"] = cut("w_in_b", 1, CONV_CH, CONV_CH + MEM_W)
    w["w_mk"] = cut("w_mem_kv", 1, 0, MEM_W)
    w["w_mv"] = cut("w_mem_kv", 1, MEM_W, 2 * MEM_W)
    w["w_o_mix"] = cut("w_out", 0, 0, CONV_CH)
    w["w_o_mem"] = cut("w_out", 0, CONV_CH, D_MODEL)
    w["w_up_g"] = cut("w_up", 1, 0, D_FF)
    w["w_up_v"] = cut("w_up", 1, D_FF, 2 * D_FF)
    w["w_dw_f_g"] = pad_rows(vcut("w_dw_f", l, 1, 0, D_FF), FFN_PAD)
    w["w_dw_f_v"] = pad_rows(vcut("w_dw_f", l, 1, D_FF, 2 * D_FF), FFN_PAD)
    w["w_down"] = cut("w_down", 0, 0, D_FF)
    return w


_PER_LAYER = ("w_a", "w_g", "w_q", "w_qm", "b_glu_a", "b_glu_g", "w_dw_a", "b_dw_a", "ln_g", "ln_b", "w_mk", "w_mv", "w_o_mix", "w_o_mem",
              "w_up_g", "w_up_v", "w_dw_f_g", "w_dw_f_v", "w_down")


def _weight_table(rep):
    w = dict(rep)
    w["b_f"] = jnp.pad(rep["b_f"], (0, F_PAD - N_FOX_HEADS)).reshape(1, F_PAD)
    w["b_dw_f_g"], w["b_dw_f_v"] = rep["b_dw_f"][:, :D_FF], rep["b_dw_f"][:, D_FF:]
    for n in _PER_LAYER:
        w[n] = {}
    return w


def _install_layer(w, l, pieces):
    for n, p in pieces.items():
        if n in _PER_LAYER:
            w[n][l - N_A if n == "w_q" else l] = p
        else:
            w[n] = p


def _shard_of(entries, name, t):
    shape, axis = SHARDED[name]
    stacked = len(shape) - len(entries[0][0][0].shape)
    step = shape[axis] // N_CHIPS
    layers = [_span(pieces, cat_axis, axis - stacked, t * step, (t + 1) * step) for pieces, cat_axis in entries]
    return jnp.stack(layers) if stacked else layers[0]


def _grad_slabs(grads):
    out = [jnp.stack([_shard_of(grads[n], n, t).astype(BF16) for t in range(N_CHIPS)]).reshape((N_CHIPS,) + _half_shape(n))
           for n in MATRIX_LEAVES]
    out.append(jnp.stack([_flatten_vectors([_shard_of(grads[n], n, t) for n in VECTOR_LEAVES], BF16) for t in range(N_CHIPS)]))
    return out


def _replicated_rows(grads):
    parts = [p.reshape(-1) for n in REPLICATED for pieces, _ in grads[n] for p in pieces]
    flat = jnp.concatenate(parts)
    return jnp.pad(flat, (0, REP_ROWS * REP_COLS - REP_ELEMS)).reshape(REP_ROWS, REP_COLS)


_ANY = pl.BlockSpec(memory_space=pl.ANY)


def _place():
    x, y, c = lax.axis_index("x"), lax.axis_index("y"), lax.axis_index("c")
    chips = [(1 - x, y), (x, 1 - y), (1 - x, 1 - y)]
    return x, y, c, chips


def _chip_index(chip):
    return 2 * chip[0] + chip[1]


def _gather_leaves(halves):
    n = len(halves)

    def body(*refs):
        w_refs, out_refs, (send_sems, recv_sems) = refs[:n], refs[n:2 * n], refs[2 * n:]
        x, y, c, chips = _place()
        me = 2 * x + y
        sibling = (x, y, 1 - c)

        def copy(a, sem, chip_idx, half, to, src=None):
            dst = out_refs[a].at[chip_idx, half]
            return pltpu.make_async_remote_copy(src_ref=dst if src is None else src, dst_ref=dst, send_sem=send_sems.at[6 * a + sem],
                                                recv_sem=recv_sems.at[6 * a + sem], device_id=to, device_id_type=MESH_ID)

        first = [copy(a, j, me, c, (*chip, c), src=w_refs[a].at[c]) for j, chip in enumerate(chips) for a in range(n)]
        for cp in first:
            cp.start()
        passed = []
        for j, chip in enumerate(chips):
            for a in range(n):
                copy(a, j, _chip_index(chip), c, (x, y, c)).wait_recv()
                passed.append(copy(a, 3 + j, _chip_index(chip), c, sibling))
                passed[-1].start()
        for j, chip in enumerate(chips):
            for a in range(n):
                copy(a, 3 + j, _chip_index(chip), 1 - c, (x, y, c)).wait_recv()
        for cp in first + passed:
            cp.wait_send()

    return pl.pallas_call(
        body, out_shape=[jax.ShapeDtypeStruct((N_CHIPS,) + h.shape, h.dtype) for h in halves],
        in_specs=[_ANY] * n, out_specs=[_ANY] * n,
        scratch_shapes=[pltpu.SemaphoreType.DMA((6 * n,)), pltpu.SemaphoreType.DMA((6 * n,))], name="gather_leaves",
    )(*halves)


_HBM = pl.BlockSpec(memory_space=pltpu.HBM)
_SEM = pl.BlockSpec(memory_space=pltpu.SEMAPHORE)


def _in_hbm(a):
    return pltpu.with_memory_space_constraint(a, pltpu.HBM)


def _gather_start(groups, after):
    flat = [h for g in groups for h in g]
    n, ng = len(flat), len(groups)

    def body(*refs):
        srcs, lands, sems = refs[1:1 + n], refs[1 + n:1 + 2 * n], refs[1 + 2 * n:1 + 2 * n + 2 * ng]
        token = refs[-1]
        x, y, c, chips = _place()
        me = 2 * x + y
        a = 0
        for gi, g in enumerate(groups):
            for k in range(len(g)):
                for j, chip in enumerate(chips):
                    pltpu.make_async_remote_copy(src_ref=srcs[a].at[c], dst_ref=lands[a].at[me, c], send_sem=sems[2 * gi].at[3 * k + j],
                                                 recv_sem=sems[2 * gi + 1].at[3 * k + j], device_id=(*chip, c), device_id_type=MESH_ID).start()
                a += 1
        token[...] = jnp.zeros_like(token)

    sem_shapes = [pltpu.SemaphoreType.DMA((3 * len(g),)) for g in groups for _ in range(2)]
    out = pl.pallas_call(
        body, name="gather_start",
        out_shape=sem_shapes + [pltpu.HBM(h.shape, h.dtype) for h in flat] + [pltpu.HBM((N_CHIPS,) + h.shape, h.dtype) for h in flat]
        + [jax.ShapeDtypeStruct((8, LANE), F32)],
        in_specs=[_ANY] + [_HBM] * (2 * n), out_specs=[_SEM] * (2 * ng) + [_HBM] * (2 * n) + [pl.BlockSpec(memory_space=pltpu.VMEM)],
        input_output_aliases={1 + i: 2 * ng + i for i in range(2 * n)},
        compiler_params=pltpu.CompilerParams(has_side_effects=pltpu.SideEffectType.DATAFLOW_SIDE_EFFECTING),
    )(after, *[_in_hbm(h) for h in flat], *[_in_hbm(lax.empty((N_CHIPS,) + h.shape, h.dtype)) for h in flat])
    sems, srcs, lands = out[:2 * ng], out[2 * ng:2 * ng + n], out[2 * ng + n:2 * ng + 2 * n]
    res, a = [], 0
    for gi, g in enumerate(groups):
        res.append((sems[2 * gi], sems[2 * gi + 1], list(srcs[a:a + len(g)]), list(lands[a:a + len(g)])))
        a += len(g)
    return res


def _gather_wait(send_sems, recv_sems, srcs, lands, after, name):
    n = len(srcs)

    def body(*refs):
        src_refs, land_refs, send, recv = refs[:n], refs[n:2 * n], refs[2 * n], refs[2 * n + 1]
        x, y, c, chips = _place()
        for k in range(n):
            for j, chip in enumerate(chips):
                cp = pltpu.make_async_remote_copy(src_ref=src_refs[k].at[c], dst_ref=land_refs[k].at[_chip_index(chip), c], send_sem=send.at[3 * k + j],
                                                  recv_sem=recv.at[3 * k + j], device_id=(*chip, c), device_id_type=MESH_ID)
                cp.wait_send()
                cp.wait_recv()

    out = pl.pallas_call(
        body, name=name, out_shape=[pltpu.HBM(a.shape, a.dtype) for a in list(srcs) + list(lands)],
        in_specs=[_HBM] * (2 * n) + [_SEM, _SEM, _ANY], out_specs=[_HBM] * (2 * n), input_output_aliases={i: i for i in range(2 * n)},
        compiler_params=pltpu.CompilerParams(has_side_effects=pltpu.SideEffectType.DATAFLOW_SIDE_EFFECTING),
    )(*srcs, *lands, send_sems, recv_sems, after)
    return list(out[n:])


def _forward_halves(lands, name):
    n = len(lands)

    def body(*refs):
        out_refs, (send_sems, recv_sems) = refs[n:2 * n], refs[2 * n:]
        x, y, c, chips = _place()

        def copy(a, j, half, to):
            blk = out_refs[a].at[_chip_index(chips[j]), half]
            return pltpu.make_async_remote_copy(src_ref=blk, dst_ref=blk, send_sem=send_sems.at[3 * a + j], recv_sem=recv_sems.at[3 * a + j],
                                                device_id=to, device_id_type=MESH_ID)

        sends = [copy(a, j, c, (x, y, 1 - c)) for a in range(n) for j in range(3)]
        for cp in sends:
            cp.start()
        for a in range(n):
            for j in range(3):
                copy(a, j, 1 - c, (x, y, c)).wait_recv()
        for cp in sends:
            cp.wait_send()

    return pl.pallas_call(
        body, out_shape=[jax.ShapeDtypeStruct(a.shape, a.dtype) for a in lands], in_specs=[_ANY] * n, out_specs=[_ANY] * n,
        input_output_aliases={i: i for i in range(n)},
        scratch_shapes=[pltpu.SemaphoreType.DMA((3 * n,)), pltpu.SemaphoreType.DMA((3 * n,))], name=name,
    )(*lands)


def _swap_halves(slabs):
    n = len(slabs)

    def body(*refs):
        g_refs, got_refs, (send_sem, recv_sem) = refs[:n], refs[n:2 * n], refs[2 * n:]
        x, y, c, _ = _place()
        copies = [pltpu.make_async_remote_copy(src_ref=g_refs[a].at[t, 1 - c], dst_ref=got_refs[a].at[t], send_sem=send_sem.at[N_CHIPS * a + t],
                                               recv_sem=recv_sem.at[N_CHIPS * a + t], device_id=(x, y, 1 - c), device_id_type=MESH_ID)
                  for a in range(n) for t in range(N_CHIPS)]
        for cp in copies:
            cp.start()
        for cp in copies:
            cp.wait()

    return pl.pallas_call(
        body, out_shape=[jax.ShapeDtypeStruct((g.shape[0],) + g.shape[2:], g.dtype) for g in slabs],
        in_specs=[_ANY] * n, out_specs=[_ANY] * n,
        scratch_shapes=[pltpu.SemaphoreType.DMA((N_CHIPS * n,)), pltpu.SemaphoreType.DMA((N_CHIPS * n,))], name="swap_halves",
    )(*slabs)


def _scatter_to_chips(parts):
    n = len(parts)

    def body(*refs):
        p_refs, got_refs, (send_sems, recv_sems) = refs[:n], refs[n:2 * n], refs[2 * n:]
        x, y, c, chips = _place()
        me = 2 * x + y
        sends = [pltpu.make_async_remote_copy(src_ref=p_refs[a].at[_chip_index(chip)], dst_ref=got_refs[a].at[me],
                                              send_sem=send_sems.at[3 * a + j], recv_sem=recv_sems.at[3 * a + j],
                                              device_id=(*chip, c), device_id_type=MESH_ID)
                 for j, chip in enumerate(chips) for a in range(n)]
        for cp in sends:
            cp.start()
        for j, chip in enumerate(chips):
            for a in range(n):
                landed = got_refs[a].at[_chip_index(chip)]
                pltpu.make_async_remote_copy(src_ref=landed, dst_ref=landed, send_sem=send_sems.at[3 * a + j], recv_sem=recv_sems.at[3 * a + j],
                                             device_id=(x, y, c), device_id_type=MESH_ID).wait_recv()
        for cp in sends:
            cp.wait_send()

    return pl.pallas_call(
        body, out_shape=[jax.ShapeDtypeStruct(p.shape, p.dtype) for p in parts], in_specs=[_ANY] * n, out_specs=[_ANY] * n,
        scratch_shapes=[pltpu.SemaphoreType.DMA((3 * n,)), pltpu.SemaphoreType.DMA((3 * n,))], name="scatter_to_chips",
    )(*parts)


def _sibling_halves(halves):
    n = len(halves)

    def body(*refs):
        h_refs, got_refs, (send_sem, recv_sem) = refs[:n], refs[n:2 * n], refs[2 * n:]
        x, y, c, _ = _place()
        copies = [pltpu.make_async_remote_copy(src_ref=h_refs[a], dst_ref=got_refs[a], send_sem=send_sem.at[a], recv_sem=recv_sem.at[a],
                                               device_id=(x, y, 1 - c), device_id_type=MESH_ID) for a in range(n)]
        for cp in copies:
            cp.start()
        for cp in copies:
            cp.wait()

    return pl.pallas_call(
        body, out_shape=[jax.ShapeDtypeStruct(h.shape, h.dtype) for h in halves], in_specs=[_ANY] * n, out_specs=[_ANY] * n,
        scratch_shapes=[pltpu.SemaphoreType.DMA((n,)), pltpu.SemaphoreType.DMA((n,))], name="sibling_halves",
    )(*halves)


def _gather_replicated(rows):
    m_per, n = rows.shape

    def body(x_ref, out_ref, send_sems, recv_sems, local_sem):
        x, y, c, chips = _place()
        me, sibling = (x, y, c), (x, y, 1 - c)

        def block(px, py, pc):
            return out_ref.at[pl.ds((4 * px + 2 * py + pc) * m_per, m_per), :]

        def copy(k, blk, to, src=None):
            return pltpu.make_async_remote_copy(src_ref=block(*blk) if src is None else src, dst_ref=block(*blk),
                                                send_sem=send_sems.at[k], recv_sem=recv_sems.at[k], device_id=to, device_id_type=MESH_ID)

        mine = pltpu.make_async_copy(x_ref, block(*me), local_sem)
        mine.start()
        first = [copy(0, me, sibling, src=x_ref)]
        first += [copy(1 + j, me, (*chip, c), src=x_ref) for j, chip in enumerate(chips)]
        for cp in first:
            cp.start()
        passed = [copy(4 + j, (*chip, c), sibling) for j, chip in enumerate(chips)]
        for j, chip in enumerate(chips):
            copy(1 + j, (*chip, c), me).wait_recv()
            passed[j].start()
        copy(0, sibling, me).wait_recv()
        for j, chip in enumerate(chips):
            copy(4 + j, (*chip, 1 - c), me).wait_recv()
        for cp in first + passed:
            cp.wait_send()
        mine.wait()

    vmem = pl.BlockSpec(memory_space=pltpu.VMEM)
    return pl.pallas_call(
        body, out_shape=jax.ShapeDtypeStruct((N_DEV * m_per, n), rows.dtype), in_specs=[vmem], out_specs=vmem,
        scratch_shapes=[pltpu.SemaphoreType.DMA((7,)), pltpu.SemaphoreType.DMA((7,)), pltpu.SemaphoreType.DMA],
        name="gather_replicated",
    )(rows)


_SUM_ROWS = 256


def _tile_rows(rows):
    if rows <= _SUM_ROWS:
        return rows
    best = None
    for r in range(16, _SUM_ROWS + 1, 16):
        if rows % r == 0:
            best = r
    assert best is not None, rows
    return best


def _add_pairs(slabs, got, core):
    n, _, rows, cols = slabs.shape
    tr = _tile_rows(rows)

    def body(core_ref, a_ref, b_ref, o_ref):
        o_ref[...] = (a_ref[0].astype(F32) + b_ref[...].astype(F32)).astype(BF16)

    spec = pl.BlockSpec((1, tr, cols), lambda t, i, core_ref: (t, i, 0))
    return pl.pallas_call(
        body, out_shape=jax.ShapeDtypeStruct((n, rows, cols), BF16),
        grid_spec=pltpu.PrefetchScalarGridSpec(
            num_scalar_prefetch=1, grid=(n, rows // tr),
            in_specs=[pl.BlockSpec((1, 1, tr, cols), lambda t, i, core_ref: (t, core_ref[0], i, 0)), spec], out_specs=spec),
        compiler_params=_params(("parallel", "parallel")), name=f"add_pairs_{rows}x{cols}",
    )(core, slabs, got)


def _sum_slabs(slabs):
    n, rows, cols = slabs.shape
    tr = _tile_rows(rows)

    def body(s_ref, o_ref):
        acc = s_ref[0].astype(F32)
        for t in range(1, n):
            acc = acc + s_ref[t].astype(F32)
        o_ref[...] = acc

    return pl.pallas_call(
        body, out_shape=jax.ShapeDtypeStruct((rows, cols), F32), grid=(rows // tr,),
        in_specs=[pl.BlockSpec((n, tr, cols), lambda i: (0, i, 0))], out_specs=pl.BlockSpec((tr, cols), lambda i: (i, 0)),
        compiler_params=_params(("parallel",)), name=f"sum_slabs_{n}x{rows}x{cols}",
    )(slabs)


_ADAM_BLOCK_BYTES = 2 * 1024 * 1024


def _adamw(w, g, m, v):
    shape = w.shape
    cols = shape[-1]
    rows = _size(shape) // cols
    tr = rows
    if rows * cols * 4 > _ADAM_BLOCK_BYTES:
        for r in range(8, rows, 8):
            if rows % r == 0 and r * cols * 4 <= _ADAM_BLOCK_BYTES:
                tr = r

    def body(w_ref, g_ref, m_ref, v_ref, d_ref, nm_ref, nv_ref):
        gv = g_ref[...]
        nm = ADAM_B1 * m_ref[...] + (1.0 - ADAM_B1) * gv
        nv = ADAM_B2 * v_ref[...] + (1.0 - ADAM_B2) * jnp.square(gv)
        m_hat = nm / (1.0 - ADAM_B1 ** ADAM_STEP)
        v_hat = nv / (1.0 - ADAM_B2 ** ADAM_STEP)
        d_ref[...] = -ADAM_LR * (m_hat / (jnp.sqrt(v_hat) + ADAM_EPS) + ADAM_WD * w_ref[...])
        nm_ref[...] = nm
        nv_ref[...] = nv

    spec = pl.BlockSpec((tr, cols), lambda i: (i, 0))
    out = pl.pallas_call(
        body, out_shape=tuple(jax.ShapeDtypeStruct((rows, cols), F32) for _ in range(3)), grid=(rows // tr,),
        in_specs=[spec] * 4, out_specs=(spec,) * 3, compiler_params=_params(("parallel",)), name=f"adamw_{rows}x{cols}",
    )(*(a.reshape(rows, cols) for a in (w, g, m, v)))
    return tuple(o.reshape(shape) for o in out)


def kernel(x, mem, g_mix, w_in_a, b_glu, w_dw_a, b_dw_a, ln_g, ln_b, g_kv, w_kvf, b_f, w_in_b, g_mem, w_mem_kv, w_out, g_ffn, w_up, w_dw_f, b_dw_f, w_down, g_final, loss_target, m_g_mix, m_w_in_a, m_b_glu, m_w_dw_a, m_b_dw_a, m_ln_g, m_ln_b, m_g_kv, m_w_kvf, m_b_f, m_w_in_b, m_g_mem, m_w_mem_kv, m_w_out, m_g_ffn, m_w_up, m_w_dw_f, m_b_dw_f, m_w_down, m_g_final, v_g_mix, v_w_in_a, v_b_glu, v_w_dw_a, v_b_dw_a, v_ln_g, v_ln_b, v_g_kv, v_w_kvf, v_b_f, v_w_in_b, v_g_mem, v_w_mem_kv, v_w_out, v_g_ffn, v_w_up, v_w_dw_f, v_b_dw_f, v_w_down, v_g_final):
    weights = dict(g_mix=g_mix, w_in_a=w_in_a, b_glu=b_glu, w_dw_a=w_dw_a, b_dw_a=b_dw_a, ln_g=ln_g, ln_b=ln_b, g_kv=g_kv,
                   w_kvf=w_kvf, b_f=b_f, w_in_b=w_in_b, g_mem=g_mem, w_mem_kv=w_mem_kv, w_out=w_out, g_ffn=g_ffn, w_up=w_up,
                   w_dw_f=w_dw_f, b_dw_f=b_dw_f, w_down=w_down, g_final=g_final)
    mom1 = dict(g_mix=m_g_mix, w_in_a=m_w_in_a, b_glu=m_b_glu, w_dw_a=m_w_dw_a, b_dw_a=m_b_dw_a, ln_g=m_ln_g, ln_b=m_ln_b,
                g_kv=m_g_kv, w_kvf=m_w_kvf, b_f=m_b_f, w_in_b=m_w_in_b, g_mem=m_g_mem, w_mem_kv=m_w_mem_kv, w_out=m_w_out,
                g_ffn=m_g_ffn, w_up=m_w_up, w_dw_f=m_w_dw_f, b_dw_f=m_b_dw_f, w_down=m_w_down, g_final=m_g_final)
    mom2 = dict(g_mix=v_g_mix, w_in_a=v_w_in_a, b_glu=v_b_glu, w_dw_a=v_w_dw_a, b_dw_a=v_b_dw_a, ln_g=v_ln_g, ln_b=v_ln_b,
                g_kv=v_g_kv, w_kvf=v_w_kvf, b_f=v_b_f, w_in_b=v_w_in_b, g_mem=v_g_mem, w_mem_kv=v_w_mem_kv, w_out=v_w_out,
                g_ffn=v_g_ffn, w_up=v_w_up, w_dw_f=v_w_dw_f, b_dw_f=v_b_dw_f, w_down=v_w_down, g_final=v_g_final)

    x_pos, y_pos, core = lax.axis_index("x"), lax.axis_index("y"), lax.axis_index("c")
    chip = 2 * x_pos + y_pos

    local = [_layer_halves(weights, l) for l in range(N_LAYERS)]
    vector_halves = _flatten_vectors([weights[n] for n in VECTOR_LEAVES], F32)
    first = _gather_leaves(local[0] + [vector_halves])
    in_flight = _gather_start(local[1:], first[-1])

    def with_own(gathered, own):
        return [lax.dynamic_update_index_in_dim(g, o[None], chip, 0) for g, o in zip(gathered, own)]

    first = with_own(first, local[0] + [vector_halves])
    vectors = _unflatten_vectors(first[-1], (N_CHIPS,))
    table = _weight_table({n: weights[n] for n in REPLICATED})

    def arrive(l, after):
        if l == 0:
            gathered = first[:-1]
        else:
            send_sems, recv_sems, srcs, lands = in_flight[l - 1]
            lands = _gather_wait(send_sems, recv_sems, srcs, lands, after, f"gather_wait_{l}")
            gathered = with_own(_forward_halves(lands, f"forward_halves_{l}"), local[l])
        _install_layer(table, l, _layer_pieces(l, gathered, vectors))

    loss, dx, grads = _local_step(x[0], mem[0], loss_target[0], table, arrive)
    loss = lax.psum(loss, ("x", "y", "c"))

    slabs = _grad_slabs(grads)
    core_index = core.astype(jnp.int32).reshape(1)
    chip_sums = [_add_pairs(mine, got, core_index) for mine, got in zip(slabs, _swap_halves(slabs))]
    from_chips = [lax.dynamic_update_index_in_dim(got, lax.dynamic_index_in_dim(mine, chip, 0, keepdims=True), chip, 0)
                  for got, mine in zip(_scatter_to_chips(chip_sums), chip_sums)]
    mine = [_sum_slabs(f) for f in from_chips]
    theirs = _sibling_halves(mine)
    reduced = [jnp.stack([jnp.where(core == 0, m, t), jnp.where(core == 0, t, m)]) for m, t in zip(mine, theirs)]
    grad_leaves = {n: r.reshape(_shard_shape(n)) for n, r in zip(MATRIX_LEAVES, reduced)}
    grad_leaves.update(_unflatten_vectors(reduced[-1]))

    rep_sum = _sum_slabs(_gather_replicated(_replicated_rows(grads)).reshape(N_DEV, REP_ROWS, REP_COLS)).reshape(-1)
    off = 0
    for n, shp in REPLICATED.items():
        grad_leaves[n] = rep_sum[off:off + _size(shp)].reshape(shp)
        off += _size(shp)

    deltas, new_m, new_v = {}, {}, {}
    for n in WEIGHT_ORDER:
        deltas[n], new_m[n], new_v[n] = _adamw(weights[n], grad_leaves[n], mom1[n], mom2[n])
    return (loss, dx[None], *[grad_leaves[n] for n in WEIGHT_ORDER], *[deltas[n] for n in WEIGHT_ORDER],
            *[new_m[n] for n in WEIGHT_ORDER], *[new_v[n] for n in WEIGHT_ORDER])
```

```python
import jax
import jax.numpy as jnp
from jax import lax
from jax.experimental import pallas as pl
from jax.experimental.pallas import tpu as pltpu

F32 = jnp.float32
BF16 = jnp.bfloat16

D_MODEL = 1024
N_LAYERS = 4
N_A = 2
CONV_CH = 768
MEM_W = 256
HEAD_DIM = 64
N_MEM_HEADS = 4
N_FOX_HEADS = 12
N_HEAD_PAIRS = N_FOX_HEADS // 2
D_FF = 2816
CONV_W = 31
CONV_PAD = 32
FFN_CONV_W = 3
FFN_PAD = 8
F_PAD = 128
RMS_EPS = 1e-6
LN_EPS = 1e-5
ATT_SCALE = HEAD_DIM ** -0.5
NEG_BIG = -1e30

ADAM_LR = 0.001
ADAM_B1 = 0.9
ADAM_B2 = 0.999
ADAM_EPS = 1e-08
ADAM_WD = 0.01
ADAM_STEP = 10

LANE = 128
ROW_TILE = 256
CHUNK = 128
VMEM_LIMIT = 48 * 1024 * 1024
FLAT_COLS = 1024
N_CHIPS = 4
N_DEV = 8
MESH_ID = pl.DeviceIdType.MESH


def _params(sem=None):
    return pltpu.CompilerParams(dimension_semantics=sem, vmem_limit_bytes=VMEM_LIMIT)


def _tile(dim, pref):
    if dim <= pref:
        return dim
    best = None
    for m in range(1, dim // LANE + 1):
        d = m * LANE
        if dim % d == 0 and d <= pref:
            best = d
    assert best is not None, dim
    return best


_DIMS = {"nn": (((1,), (0,)), ((), ())), "nt": (((1,), (1,)), ((), ())), "tn": (((0,), (0,)), ((), ()))}


_MM_RESIDENT_BYTES = 8 * 1024 * 1024
_MM_STREAM_BYTES = 6 * 1024 * 1024
_MM_OUT_BYTES = 6 * 1024 * 1024


def _mm_tiles(m, n, k, mode, a_size, b_size, o_size):
    if mode == "tn":
        tm = _tile(m, _MM_RESIDENT_BYTES // (k * a_size))
        tn = _tile(n, min(_MM_STREAM_BYTES // (k * b_size), _MM_OUT_BYTES // (tm * o_size)))
    else:
        tn = _tile(n, _MM_RESIDENT_BYTES // (k * b_size))
        tm = _tile(m, min(_MM_STREAM_BYTES // (k * a_size), _MM_OUT_BYTES // (tn * o_size), 512))
    return tm, tn


def _mm(a, b, mode="nn", out_dtype=F32, add=None):
    if mode == "nn":
        (m, k), (k2, n) = a.shape, b.shape
    elif mode == "nt":
        (m, k), (n, k2) = a.shape, b.shape
    else:
        (k, m), (k2, n) = a.shape, b.shape
    assert k == k2, (a.shape, b.shape, mode)
    tm, tn = _mm_tiles(m, n, k, mode, a.dtype.itemsize, b.dtype.itemsize, jnp.dtype(out_dtype).itemsize)
    dims = _DIMS[mode]
    has_add = add is not None

    def body(*refs):
        if has_add:
            a_ref, b_ref, add_ref, o_ref = refs
        else:
            a_ref, b_ref, o_ref = refs
        r = lax.dot_general(a_ref[...].astype(BF16), b_ref[...].astype(BF16), dims, preferred_element_type=F32)
        if has_add:
            r = r + add_ref[...]
        o_ref[...] = r.astype(out_dtype)

    a_spec = pl.BlockSpec((k, tm), lambda i, j: (0, i)) if mode == "tn" else pl.BlockSpec((tm, k), lambda i, j: (i, 0))
    b_spec = pl.BlockSpec((tn, k), lambda i, j: (j, 0)) if mode == "nt" else pl.BlockSpec((k, tn), lambda i, j: (0, j))
    o_spec = pl.BlockSpec((tm, tn), lambda i, j: (i, j))
    in_specs = [a_spec, b_spec] + ([o_spec] if has_add else [])
    args = (a, b) + ((add,) if has_add else ())
    return pl.pallas_call(
        body,
        out_shape=jax.ShapeDtypeStruct((m, n), out_dtype),
        grid=(m // tm, n // tn),
        in_specs=in_specs,
        out_specs=o_spec,
        compiler_params=_params(("parallel", "parallel")),
        name=f"mm_{mode}_{m}x{k}x{n}",
    )(*args)


def _row(width):
    return pl.BlockSpec((ROW_TILE, width), lambda i: (i, 0))


def _vec(width):
    return pl.BlockSpec((1, width), lambda i: (0, 0))


def _rms_fwd(x, g):
    rows, d = x.shape

    def body(x_ref, g_ref, o_ref):
        xv = x_ref[...]
        rstd = lax.rsqrt(jnp.mean(xv * xv, axis=-1, keepdims=True) + RMS_EPS)
        o_ref[...] = (xv * rstd * g_ref[...]).astype(BF16)

    return pl.pallas_call(
        body, out_shape=jax.ShapeDtypeStruct((rows, d), BF16), grid=(rows // ROW_TILE,),
        in_specs=[_row(d), _vec(d)], out_specs=_row(d), compiler_params=_params(("parallel",)), name=f"rms_fwd_{rows}",
    )(x, g)


def _accumulate(ref, value, step):
    @pl.when(step == 0)
    def _():
        ref[...] = value

    @pl.when(step > 0)
    def _():
        ref[...] += value


def _rms_bwd(x, g, dh, dres):
    rows, d = x.shape
    has_res = dres is not None

    def body(*refs):
        if has_res:
            x_ref, g_ref, dh_ref, dres_ref, dx_ref, dxb_ref, dg_ref = refs
        else:
            x_ref, g_ref, dh_ref, dx_ref, dxb_ref, dg_ref = refs
        xv = x_ref[...]
        dhv = dh_ref[...]
        rstd = lax.rsqrt(jnp.mean(xv * xv, axis=-1, keepdims=True) + RMS_EPS)
        xhat = xv * rstd
        gd = dhv * g_ref[...]
        dx = rstd * (gd - xhat * jnp.mean(gd * xhat, axis=-1, keepdims=True))
        if has_res:
            dx = dx + dres_ref[...]
        dx_ref[...] = dx
        dxb_ref[...] = dx.astype(BF16)
        _accumulate(dg_ref, jnp.sum(dhv * xhat, axis=0, keepdims=True), pl.program_id(0))

    in_specs = [_row(d), _vec(d), _row(d)] + ([_row(d)] if has_res else [])
    args = (x, g, dh) + ((dres,) if has_res else ())
    return pl.pallas_call(
        body,
        out_shape=(jax.ShapeDtypeStruct((rows, d), F32), jax.ShapeDtypeStruct((rows, d), BF16), jax.ShapeDtypeStruct((1, d), F32)),
        grid=(rows // ROW_TILE,), in_specs=in_specs, out_specs=(_row(d), _row(d), _vec(d)),
        compiler_params=_params(("arbitrary",)), name=f"rms_bwd_{rows}_{int(has_res)}",
    )(*args)


def _loss_head(x, g, target):
    rows, d = x.shape

    def body(x_ref, g_ref, t_ref, loss_ref, dx_ref, dxb_ref, dg_ref):
        xv = x_ref[...]
        gv = g_ref[...]
        rstd = lax.rsqrt(jnp.mean(xv * xv, axis=-1, keepdims=True) + RMS_EPS)
        xhat = xv * rstd
        err = xhat * gv - t_ref[...]
        part = 0.5 * jnp.sum(jnp.mean(err * err, axis=-1, keepdims=True), axis=0, keepdims=True)
        dy = err * (1.0 / d)
        gd = dy * gv
        dx = rstd * (gd - xhat * jnp.mean(gd * xhat, axis=-1, keepdims=True))
        dx_ref[...] = dx
        dxb_ref[...] = dx.astype(BF16)
        step = pl.program_id(0)
        _accumulate(loss_ref, jnp.broadcast_to(part, (1, LANE)), step)
        _accumulate(dg_ref, jnp.sum(dy * xhat, axis=0, keepdims=True), step)

    return pl.pallas_call(
        body,
        out_shape=(jax.ShapeDtypeStruct((1, LANE), F32), jax.ShapeDtypeStruct((rows, d), F32),
                   jax.ShapeDtypeStruct((rows, d), BF16), jax.ShapeDtypeStruct((1, d), F32)),
        grid=(rows // ROW_TILE,), in_specs=[_row(d), _vec(d), _row(d)],
        out_specs=(_vec(LANE), _row(d), _row(d), _vec(d)),
        compiler_params=_params(("arbitrary",)), name="loss_head",
    )(x, g, target)


def _sigmoid(x):
    return 1.0 / (1.0 + jnp.exp(-x))


def _ln_silu_fwd(c, ln_g, ln_b):
    rows, ch = c.shape

    def body(c_ref, g_ref, b_ref, o_ref):
        cv = c_ref[...]
        mu = jnp.mean(cv, axis=-1, keepdims=True)
        cen = cv - mu
        rstd = lax.rsqrt(jnp.mean(cen * cen, axis=-1, keepdims=True) + LN_EPS)
        y = cen * rstd * g_ref[...] + b_ref[...]
        o_ref[...] = (y * _sigmoid(y)).astype(BF16)

    return pl.pallas_call(
        body, out_shape=jax.ShapeDtypeStruct((rows, ch), BF16), grid=(rows // ROW_TILE,),
        in_specs=[_row(ch), _vec(ch), _vec(ch)], out_specs=_row(ch), compiler_params=_params(("parallel",)), name="ln_silu_fwd",
    )(c, ln_g, ln_b)


def _ln_silu_bwd(dmix, c, ln_g, ln_b):
    rows, ch = c.shape

    def body(dm_ref, c_ref, g_ref, b_ref, dc_ref, dg_ref, db_ref):
        cv = c_ref[...]
        gv = g_ref[...]
        mu = jnp.mean(cv, axis=-1, keepdims=True)
        cen = cv - mu
        rstd = lax.rsqrt(jnp.mean(cen * cen, axis=-1, keepdims=True) + LN_EPS)
        xhat = cen * rstd
        y = xhat * gv + b_ref[...]
        sg = _sigmoid(y)
        dy = dm_ref[...] * (sg * (1.0 + y * (1.0 - sg)))
        dxh = dy * gv
        dc = rstd * (dxh - jnp.mean(dxh, axis=-1, keepdims=True) - xhat * jnp.mean(dxh * xhat, axis=-1, keepdims=True))
        dc_ref[...] = dc
        step = pl.program_id(0)
        _accumulate(dg_ref, jnp.sum(dy * xhat, axis=0, keepdims=True), step)
        _accumulate(db_ref, jnp.sum(dy, axis=0, keepdims=True), step)

    return pl.pallas_call(
        body,
        out_shape=(jax.ShapeDtypeStruct((rows, ch), F32), jax.ShapeDtypeStruct((1, ch), F32), jax.ShapeDtypeStruct((1, ch), F32)),
        grid=(rows // ROW_TILE,), in_specs=[_row(ch), _row(ch), _vec(ch), _vec(ch)],
        out_specs=(_row(ch), _vec(ch), _vec(ch)), compiler_params=_params(("arbitrary",)), name="ln_silu_bwd",
    )(dmix, c, ln_g, ln_b)


def _col(rows, cb):
    return pl.BlockSpec((rows, cb), lambda j: (0, j))


def _conv_glu_fwd(pa, pg, ba, bg, w_dw, b_dw):
    seq, ch = pa.shape
    cb = LANE
    n_chunks = seq // CHUNK

    def body(pa_ref, pg_ref, ba_ref, bg_ref, w_ref, b_ref, v_ref, c_ref, vpad_ref):
        vpad_ref[0:CONV_PAD, :] = jnp.zeros((CONV_PAD, cb), F32)

        def glu(r, carry):
            r0 = pl.multiple_of(r * CHUNK, CHUNK)
            rows = pl.ds(r0, CHUNK)
            v = (pa_ref[rows, :] + ba_ref[...]) * _sigmoid(pg_ref[rows, :] + bg_ref[...])
            v_ref[rows, :] = v
            vpad_ref[pl.ds(r0 + CONV_PAD, CHUNK), :] = v
            return carry

        lax.fori_loop(0, n_chunks, glu, 0)

        def conv(r, carry):
            r0 = pl.multiple_of(r * CHUNK, CHUNK)
            win = vpad_ref[pl.ds(r0, CHUNK + CONV_PAD), :]
            acc = jnp.broadcast_to(b_ref[...], (CHUNK, cb))
            for t in range(CONV_W):
                off = CONV_PAD - (CONV_W - 1) + t
                acc = acc + w_ref[t:t + 1, :] * win[off:off + CHUNK, :]
            c_ref[pl.ds(r0, CHUNK), :] = acc
            return carry

        lax.fori_loop(0, n_chunks, conv, 0)

    return pl.pallas_call(
        body,
        out_shape=(jax.ShapeDtypeStruct((seq, ch), F32), jax.ShapeDtypeStruct((seq, ch), F32)),
        grid=(ch // cb,),
        in_specs=[_col(seq, cb), _col(seq, cb), _col(1, cb), _col(1, cb), _col(CONV_PAD, cb), _col(1, cb)],
        out_specs=(_col(seq, cb), _col(seq, cb)),
        scratch_shapes=[pltpu.VMEM((seq + CONV_PAD, cb), F32)],
        compiler_params=_params(("parallel",)), name="conv_glu_fwd",
    )(pa, pg, ba, bg, w_dw, b_dw)


def _conv_glu_bwd(dc, v, pa, pg, ba, bg, w_dw):
    seq, ch = dc.shape
    cb = LANE
    n_chunks = seq // CHUNK

    def body(dc_ref, v_ref, pa_ref, pg_ref, ba_ref, bg_ref, w_ref, da_ref, dg_ref, dw_ref, dbdw_ref, dba_ref, dbg_ref,
             dcpad_ref, vpad_ref):
        vpad_ref[0:CONV_PAD, :] = jnp.zeros((CONV_PAD, cb), F32)
        dcpad_ref[seq:seq + CONV_PAD, :] = jnp.zeros((CONV_PAD, cb), F32)
        dw_ref[...] = jnp.zeros((CONV_PAD, cb), F32)
        dbdw_ref[...] = jnp.zeros((1, cb), F32)
        dba_ref[...] = jnp.zeros((1, cb), F32)
        dbg_ref[...] = jnp.zeros((1, cb), F32)

        def fill(r, carry):
            r0 = pl.multiple_of(r * CHUNK, CHUNK)
            vpad_ref[pl.ds(r0 + CONV_PAD, CHUNK), :] = v_ref[pl.ds(r0, CHUNK), :]
            dcpad_ref[pl.ds(r0, CHUNK), :] = dc_ref[pl.ds(r0, CHUNK), :]
            return carry

        lax.fori_loop(0, n_chunks, fill, 0)

        def step(r, carry):
            r0 = pl.multiple_of(r * CHUNK, CHUNK)
            rows = pl.ds(r0, CHUNK)
            dwin = dcpad_ref[pl.ds(r0, CHUNK + CONV_PAD), :]
            vwin = vpad_ref[pl.ds(r0, CHUNK + CONV_PAD), :]
            dcur = dwin[0:CHUNK, :]
            dv = jnp.zeros((CHUNK, cb), F32)
            for t in range(CONV_W):
                fwd_off = CONV_W - 1 - t
                dv = dv + w_ref[t:t + 1, :] * dwin[fwd_off:fwd_off + CHUNK, :]
                off = CONV_PAD - (CONV_W - 1) + t
                dw_ref[t:t + 1, :] += jnp.sum(dcur * vwin[off:off + CHUNK, :], axis=0, keepdims=True)
            a = pa_ref[rows, :] + ba_ref[...]
            sg = _sigmoid(pg_ref[rows, :] + bg_ref[...])
            da = dv * sg
            dgate = dv * a * sg * (1.0 - sg)
            da_ref[rows, :] = da.astype(BF16)
            dg_ref[rows, :] = dgate.astype(BF16)
            dbdw_ref[...] += jnp.sum(dcur, axis=0, keepdims=True)
            dba_ref[...] += jnp.sum(da, axis=0, keepdims=True)
            dbg_ref[...] += jnp.sum(dgate, axis=0, keepdims=True)
            return carry

        lax.fori_loop(0, n_chunks, step, 0)

    return pl.pallas_call(
        body,
        out_shape=(jax.ShapeDtypeStruct((seq, ch), BF16), jax.ShapeDtypeStruct((seq, ch), BF16),
                   jax.ShapeDtypeStruct((CONV_PAD, ch), F32), jax.ShapeDtypeStruct((1, ch), F32),
                   jax.ShapeDtypeStruct((1, ch), F32), jax.ShapeDtypeStruct((1, ch), F32)),
        grid=(ch // cb,),
        in_specs=[_col(seq, cb), _col(seq, cb), _col(seq, cb), _col(seq, cb), _col(1, cb), _col(1, cb), _col(CONV_PAD, cb)],
        out_specs=(_col(seq, cb), _col(seq, cb), _col(CONV_PAD, cb), _col(1, cb), _col(1, cb), _col(1, cb)),
        scratch_shapes=[pltpu.VMEM((seq + CONV_PAD, cb), F32), pltpu.VMEM((seq + CONV_PAD, cb), F32)],
        compiler_params=_params(("parallel",)), name="conv_glu_bwd",
    )(dc, v, pa, pg, ba, bg, w_dw)


def _ffn_conv(pad_ref, r0, w_ref, b_ref, cb):
    win = pad_ref[pl.ds(r0, CHUNK + FFN_PAD), :]
    y = jnp.broadcast_to(b_ref[...], (CHUNK, cb))
    for t in range(FFN_CONV_W):
        off = FFN_PAD - (FFN_CONV_W - 1) + t
        y = y + w_ref[t:t + 1, :] * win[off:off + CHUNK, :]
    return y, win


def _ffn_mid_fwd(ug, uv, wg, wv, bg, bv):
    seq, ch = ug.shape
    cb = _tile(ch, 256)
    n_chunks = seq // CHUNK

    def body(ug_ref, uv_ref, wg_ref, wv_ref, bg_ref, bv_ref, z_ref, gpad_ref, vpad_ref):
        gpad_ref[0:FFN_PAD, :] = jnp.zeros((FFN_PAD, cb), F32)
        vpad_ref[0:FFN_PAD, :] = jnp.zeros((FFN_PAD, cb), F32)

        def fill(r, carry):
            r0 = pl.multiple_of(r * CHUNK, CHUNK)
            gpad_ref[pl.ds(r0 + FFN_PAD, CHUNK), :] = ug_ref[pl.ds(r0, CHUNK), :]
            vpad_ref[pl.ds(r0 + FFN_PAD, CHUNK), :] = uv_ref[pl.ds(r0, CHUNK), :]
            return carry

        lax.fori_loop(0, n_chunks, fill, 0)

        def step(r, carry):
            r0 = pl.multiple_of(r * CHUNK, CHUNK)
            yg, _ = _ffn_conv(gpad_ref, r0, wg_ref, bg_ref, cb)
            yv, _ = _ffn_conv(vpad_ref, r0, wv_ref, bv_ref, cb)
            z_ref[pl.ds(r0, CHUNK), :] = (yg * _sigmoid(yg) * yv).astype(BF16)
            return carry

        lax.fori_loop(0, n_chunks, step, 0)

    return pl.pallas_call(
        body, out_shape=jax.ShapeDtypeStruct((seq, ch), BF16), grid=(ch // cb,),
        in_specs=[_col(seq, cb), _col(seq, cb), _col(FFN_PAD, cb), _col(FFN_PAD, cb), _col(1, cb), _col(1, cb)],
        out_specs=_col(seq, cb),
        scratch_shapes=[pltpu.VMEM((seq + FFN_PAD, cb), F32), pltpu.VMEM((seq + FFN_PAD, cb), F32)],
        compiler_params=_params(("parallel",)), name="ffn_mid_fwd",
    )(ug, uv, wg, wv, bg, bv)


def _ffn_mid_bwd(ug, uv, dz, wg, wv, bg, bv):
    seq, ch = ug.shape
    cb = _tile(ch, 256)
    n_chunks = seq // CHUNK

    def body(ug_ref, uv_ref, dz_ref, wg_ref, wv_ref, bg_ref, bv_ref, dug_ref, duv_ref, dwg_ref, dwv_ref, dbg_ref, dbv_ref,
             gpad_ref, vpad_ref, dyg_ref, dyv_ref):
        gpad_ref[0:FFN_PAD, :] = jnp.zeros((FFN_PAD, cb), F32)
        vpad_ref[0:FFN_PAD, :] = jnp.zeros((FFN_PAD, cb), F32)
        dyg_ref[seq:seq + FFN_PAD, :] = jnp.zeros((FFN_PAD, cb), F32)
        dyv_ref[seq:seq + FFN_PAD, :] = jnp.zeros((FFN_PAD, cb), F32)
        dwg_ref[...] = jnp.zeros((FFN_PAD, cb), F32)
        dwv_ref[...] = jnp.zeros((FFN_PAD, cb), F32)
        dbg_ref[...] = jnp.zeros((1, cb), F32)
        dbv_ref[...] = jnp.zeros((1, cb), F32)

        def fill(r, carry):
            r0 = pl.multiple_of(r * CHUNK, CHUNK)
            gpad_ref[pl.ds(r0 + FFN_PAD, CHUNK), :] = ug_ref[pl.ds(r0, CHUNK), :]
            vpad_ref[pl.ds(r0 + FFN_PAD, CHUNK), :] = uv_ref[pl.ds(r0, CHUNK), :]
            return carry

        lax.fori_loop(0, n_chunks, fill, 0)

        def grads_of_conv_out(r, carry):
            r0 = pl.multiple_of(r * CHUNK, CHUNK)
            rows = pl.ds(r0, CHUNK)
            yg, gwin = _ffn_conv(gpad_ref, r0, wg_ref, bg_ref, cb)
            yv, vwin = _ffn_conv(vpad_ref, r0, wv_ref, bv_ref, cb)
            dzv = dz_ref[rows, :]
            sg = _sigmoid(yg)
            dyg = dzv * yv * (sg * (1.0 + yg * (1.0 - sg)))
            dyv = dzv * yg * sg
            dyg_ref[rows, :] = dyg
            dyv_ref[rows, :] = dyv
            for t in range(FFN_CONV_W):
                off = FFN_PAD - (FFN_CONV_W - 1) + t
                dwg_ref[t:t + 1, :] += jnp.sum(dyg * gwin[off:off + CHUNK, :], axis=0, keepdims=True)
                dwv_ref[t:t + 1, :] += jnp.sum(dyv * vwin[off:off + CHUNK, :], axis=0, keepdims=True)
            dbg_ref[...] += jnp.sum(dyg, axis=0, keepdims=True)
            dbv_ref[...] += jnp.sum(dyv, axis=0, keepdims=True)
            return carry

        lax.fori_loop(0, n_chunks, grads_of_conv_out, 0)

        def grads_of_conv_in(r, carry):
            r0 = pl.multiple_of(r * CHUNK, CHUNK)
            gwin = dyg_ref[pl.ds(r0, CHUNK + FFN_PAD), :]
            vwin = dyv_ref[pl.ds(r0, CHUNK + FFN_PAD), :]
            dug = jnp.zeros((CHUNK, cb), F32)
            duv = jnp.zeros((CHUNK, cb), F32)
            for t in range(FFN_CONV_W):
                off = FFN_CONV_W - 1 - t
                dug = dug + wg_ref[t:t + 1, :] * gwin[off:off + CHUNK, :]
                duv = duv + wv_ref[t:t + 1, :] * vwin[off:off + CHUNK, :]
            dug_ref[pl.ds(r0, CHUNK), :] = dug.astype(BF16)
            duv_ref[pl.ds(r0, CHUNK), :] = duv.astype(BF16)
            return carry

        lax.fori_loop(0, n_chunks, grads_of_conv_in, 0)

    return pl.pallas_call(
        body,
        out_shape=(jax.ShapeDtypeStruct((seq, ch), BF16), jax.ShapeDtypeStruct((seq, ch), BF16),
                   jax.ShapeDtypeStruct((FFN_PAD, ch), F32), jax.ShapeDtypeStruct((FFN_PAD, ch), F32),
                   jax.ShapeDtypeStruct((1, ch), F32), jax.ShapeDtypeStruct((1, ch), F32)),
        grid=(ch // cb,),
        in_specs=[_col(seq, cb), _col(seq, cb), _col(seq, cb), _col(FFN_PAD, cb), _col(FFN_PAD, cb), _col(1, cb), _col(1, cb)],
        out_specs=(_col(seq, cb), _col(seq, cb), _col(FFN_PAD, cb), _col(FFN_PAD, cb), _col(1, cb), _col(1, cb)),
        scratch_shapes=[pltpu.VMEM((seq + FFN_PAD, cb), F32) for _ in range(4)],
        compiler_params=_params(("parallel",)), name="ffn_mid_bwd",
    )(ug, uv, dz, wg, wv, bg, bv)


def _dot(a, b, mode):
    return lax.dot_general(a.astype(BF16), b.astype(BF16), _DIMS[mode], preferred_element_type=F32)


def _head(h):
    return slice(h * HEAD_DIM, (h + 1) * HEAD_DIM)


def _mem_softmax(q, k):
    s = _dot(q, k, "nt") * ATT_SCALE
    e = jnp.exp(s - jnp.max(s, axis=-1, keepdims=True))
    return e / jnp.sum(e, axis=-1, keepdims=True)


def _memattn_fwd(pq, mk, mv):
    seq, w = pq.shape
    m = mk.shape[0]

    def body(q_ref, k_ref, v_ref, o_ref):
        for h in range(N_MEM_HEADS):
            p = _mem_softmax(q_ref[:, _head(h)], k_ref[:, _head(h)])
            o_ref[:, _head(h)] = _dot(p, v_ref[:, _head(h)], "nn").astype(BF16)

    full = pl.BlockSpec((m, w), lambda i: (0, 0))
    return pl.pallas_call(
        body, out_shape=jax.ShapeDtypeStruct((seq, w), BF16), grid=(seq // ROW_TILE,),
        in_specs=[_row(w), full, full], out_specs=_row(w), compiler_params=_params(("parallel",)), name="memattn_fwd",
    )(pq, mk, mv)


def _memattn_bwd(pq, mk, mv, dmo):
    seq, w = pq.shape
    m = mk.shape[0]

    def body(q_ref, k_ref, v_ref, do_ref, dq_ref, dk_ref, dv_ref):
        step = pl.program_id(0)

        @pl.when(step == 0)
        def _():
            dk_ref[...] = jnp.zeros((m, w), F32)
            dv_ref[...] = jnp.zeros((m, w), F32)

        for h in range(N_MEM_HEADS):
            q, k, v, do = q_ref[:, _head(h)], k_ref[:, _head(h)], v_ref[:, _head(h)], do_ref[:, _head(h)]
            p = _mem_softmax(q, k)
            dp = _dot(do, v, "nt")
            ds = p * (dp - jnp.sum(dp * p, axis=-1, keepdims=True))
            dq_ref[:, _head(h)] = (_dot(ds, k, "nn") * ATT_SCALE).astype(BF16)
            dk_ref[:, _head(h)] += _dot(ds, q, "tn") * ATT_SCALE
            dv_ref[:, _head(h)] += _dot(p, do, "tn")

    full = pl.BlockSpec((m, w), lambda i: (0, 0))
    return pl.pallas_call(
        body,
        out_shape=(jax.ShapeDtypeStruct((seq, w), BF16), jax.ShapeDtypeStruct((m, w), F32), jax.ShapeDtypeStruct((m, w), F32)),
        grid=(seq // ROW_TILE,), in_specs=[_row(w), full, full, _row(w)], out_specs=(_row(w), full, full),
        compiler_params=_params(("arbitrary",)), name="memattn_bwd",
    )(pq, mk, mv, dmo)


FWD_KEY_TILE = 512
BWD_KEY_TILE = 512


def _causal_t(s_t, q_blk, k_blk, key_tile):
    kpos = k_blk * key_tile + lax.broadcasted_iota(jnp.int32, (key_tile, ROW_TILE), 0)
    qpos = q_blk * ROW_TILE + lax.broadcasted_iota(jnp.int32, (key_tile, ROW_TILE), 1)
    return jnp.where(kpos <= qpos, s_t, NEG_BIG)


def _fox_fwd(q, k, v, cum_cols, cum_rows):
    seq, w = q.shape
    nq = seq // ROW_TILE

    def body(q_ref, k_ref, v_ref, cc_ref, cr_ref, o_ref, ob_ref, lse_ref):
        i = pl.program_id(1)
        qs = [q_ref[:, _head(h)].astype(BF16) for h in range(2)]
        crs = [cr_ref[0, 0, h:h + 1, :] for h in range(2)]

        def step(j, carry):
            rows = pl.ds(pl.multiple_of(j * FWD_KEY_TILE, FWD_KEY_TILE), FWD_KEY_TILE)
            new = []
            for h in range(2):
                m_run, l_run, acc = carry[h]
                s_t = _dot(k_ref[rows, _head(h)], qs[h], "nt") * ATT_SCALE + crs[h] - cc_ref[0, rows, h * HEAD_DIM:h * HEAD_DIM + 1]
                s_t = _causal_t(s_t, i, j, FWD_KEY_TILE)
                m_new = jnp.maximum(m_run, jnp.max(s_t, axis=0, keepdims=True))
                alpha = jnp.exp(m_run - m_new)
                p_t = jnp.exp(s_t - m_new)
                l_new = alpha * l_run + jnp.sum(p_t, axis=0, keepdims=True)
                new.append((m_new, l_new, alpha * acc + _dot(v_ref[rows, _head(h)], p_t, "tn")))
            return tuple(new)

        one_head = (jnp.full((1, ROW_TILE), NEG_BIG, F32), jnp.zeros((1, ROW_TILE), F32), jnp.zeros((HEAD_DIM, ROW_TILE), F32))
        res = lax.fori_loop(0, ((i + 1) * ROW_TILE + FWD_KEY_TILE - 1) // FWD_KEY_TILE, step, (one_head, one_head))
        o = jnp.concatenate([acc / l_run for _, l_run, acc in res], axis=0).T
        o_ref[...] = o
        ob_ref[...] = o.astype(BF16)
        lse_ref[...] = jnp.zeros((1, 1, 8, ROW_TILE), F32)
        for h in range(2):
            lse_ref[0, 0, h:h + 1, :] = res[h][0] + jnp.log(res[h][1])

    blk = pl.BlockSpec((ROW_TILE, LANE), lambda hp, i: (i, hp))
    full = pl.BlockSpec((seq, LANE), lambda hp, i: (0, hp))
    cols = pl.BlockSpec((1, seq, LANE), lambda hp, i: (hp, 0, 0))
    rows = pl.BlockSpec((1, 1, 8, ROW_TILE), lambda hp, i: (hp, i, 0, 0))
    return pl.pallas_call(
        body,
        out_shape=(jax.ShapeDtypeStruct((seq, w), F32), jax.ShapeDtypeStruct((seq, w), BF16),
                   jax.ShapeDtypeStruct((N_HEAD_PAIRS, nq, 8, ROW_TILE), F32)),
        grid=(N_HEAD_PAIRS, nq), in_specs=[blk, full, full, cols, rows], out_specs=(blk, blk, rows),
        compiler_params=_params(("parallel", "parallel")), name="fox_fwd",
    )(q, k, v, cum_cols, cum_rows)


def _fox_bwd(q, k, v, cum_cols, cum_rows, o, lse, do):
    seq, w = q.shape
    nq = seq // ROW_TILE
    nk = seq // BWD_KEY_TILE

    def body(q_ref, k_ref, v_ref, cc_ref, cr_ref, o_ref, lse_ref, do_ref, dq_ref, dk_ref, dv_ref, dcc_ref, dcr_ref):
        kj = pl.program_id(1)

        @pl.when(kj == 0)
        def _():
            dq_ref[...] = jnp.zeros((seq, LANE), F32)
            dcr_ref[...] = jnp.zeros((1, nq, 8, ROW_TILE), F32)

        ks = [k_ref[:, _head(h)].astype(BF16) for h in range(2)]
        vs = [v_ref[:, _head(h)].astype(BF16) for h in range(2)]
        ccs = [cc_ref[0, :, h * HEAD_DIM:h * HEAD_DIM + 1] for h in range(2)]
        ones = jnp.ones((8, HEAD_DIM), BF16)

        def step(i, carry):
            rows = pl.ds(pl.multiple_of(i * ROW_TILE, ROW_TILE), ROW_TILE)
            new = []
            for h in range(2):
                dk, dv, fold = carry[h]
                qh = q_ref[rows, _head(h)].astype(BF16)
                doh = do_ref[rows, _head(h)]
                s_t = _dot(ks[h], qh, "nt") * ATT_SCALE + cr_ref[0, i, h:h + 1, :] - ccs[h]
                p_t = jnp.exp(_causal_t(s_t, i, kj, BWD_KEY_TILE) - lse_ref[0, i, h:h + 1, :])
                dp_t = _dot(vs[h], doh, "nt")
                hi, mid, lo = _split3(doh * o_ref[rows, _head(h)])
                row_sum = lambda part: lax.dot_general(ones, part, _DIMS["nt"], preferred_element_type=F32)
                delta = ((row_sum(lo) + row_sum(mid)) + row_sum(hi))[0:1, :]
                ds_t = p_t * (dp_t - delta)
                dq_ref[rows, _head(h)] += _dot(ds_t, ks[h], "tn") * ATT_SCALE
                dcr_ref[0, i, h:h + 1, :] += jnp.sum(ds_t, axis=0, keepdims=True)
                new.append((dk + _dot(ds_t, qh, "nn") * ATT_SCALE, dv + _dot(p_t, doh, "nn"), fold + (ds_t[:, :LANE] + ds_t[:, LANE:])))
            return tuple(new)

        one_head = (jnp.zeros((BWD_KEY_TILE, HEAD_DIM), F32), jnp.zeros((BWD_KEY_TILE, HEAD_DIM), F32), jnp.zeros((BWD_KEY_TILE, LANE), F32))
        res = lax.fori_loop(kj * BWD_KEY_TILE // ROW_TILE, nq, step, (one_head, one_head))
        for h in range(2):
            dk_ref[:, _head(h)] = res[h][0]
            dv_ref[:, _head(h)] = res[h][1]
            dcc_ref[0, :, _head(h)] = jnp.broadcast_to(-jnp.sum(res[h][2], axis=-1, keepdims=True), (BWD_KEY_TILE, HEAD_DIM))

    blk = pl.BlockSpec((BWD_KEY_TILE, LANE), lambda hp, j: (j, hp))
    full = pl.BlockSpec((seq, LANE), lambda hp, j: (0, hp))
    cols = pl.BlockSpec((1, BWD_KEY_TILE, LANE), lambda hp, j: (hp, j, 0))
    rows = pl.BlockSpec((1, nq, 8, ROW_TILE), lambda hp, j: (hp, 0, 0, 0))
    return pl.pallas_call(
        body,
        out_shape=(jax.ShapeDtypeStruct((seq, w), F32), jax.ShapeDtypeStruct((seq, w), F32), jax.ShapeDtypeStruct((seq, w), F32),
                   jax.ShapeDtypeStruct((N_HEAD_PAIRS, seq, LANE), F32), jax.ShapeDtypeStruct((N_HEAD_PAIRS, nq, 8, ROW_TILE), F32)),
        grid=(N_HEAD_PAIRS, nk), in_specs=[full, blk, blk, cols, rows, full, rows, full],
        out_specs=(full, blk, blk, cols, rows),
        compiler_params=_params(("parallel", "arbitrary")), name="fox_bwd",
    )(q, k, v, cum_cols, cum_rows, o, lse, do)


def _split3(x):
    hi = x.astype(BF16)
    r1 = x - hi.astype(F32)
    mid = r1.astype(BF16)
    lo = (r1 - mid.astype(F32)).astype(BF16)
    return hi, mid, lo


def _tri_sum(tri, x):
    hi, mid, lo = _split3(x)
    dot = lambda p: lax.dot_general(tri, p, _DIMS["nn"], preferred_element_type=F32)
    return (dot(lo) + dot(mid)) + dot(hi)


def _tri(lower):
    r = lax.broadcasted_iota(jnp.int32, (LANE, LANE), 0)
    c = lax.broadcasted_iota(jnp.int32, (LANE, LANE), 1)
    return jnp.where((c <= r) if lower else (c >= r), 1.0, 0.0).astype(BF16)


def _fgate_fwd(fl, bf):
    seq, w = fl.shape
    nb = seq // LANE

    def body(f_ref, b_ref, cum_ref):
        tri = _tri(True)

        def step(i, carry):
            rows = pl.ds(pl.multiple_of(i * LANE, LANE), LANE)
            x = f_ref[rows, :] + b_ref[...]
            logsig = jnp.minimum(x, 0.0) - jnp.log(1.0 + jnp.exp(-jnp.abs(x)))
            cum = _tri_sum(tri, logsig) + carry
            cum_ref[rows, :] = cum
            return cum[LANE - 1:LANE, :]

        lax.fori_loop(0, nb, step, jnp.zeros((1, w), F32))

    return pl.pallas_call(body, out_shape=jax.ShapeDtypeStruct((seq, w), F32), compiler_params=_params(), name="fgate_fwd")(fl, bf)


def _fgate_bwd(fl, bf, dcum):
    seq, w = fl.shape
    nb = seq // LANE

    def body(f_ref, b_ref, dc_ref, df_ref, db_ref):
        tri = _tri(False)

        def step(i, carry):
            tail, db = carry
            rows = pl.ds(pl.multiple_of((nb - 1 - i) * LANE, LANE), LANE)
            suffix = _tri_sum(tri, dc_ref[rows, :]) + tail
            df = suffix * (1.0 - _sigmoid(f_ref[rows, :] + b_ref[...]))
            df_ref[rows, :] = df
            return suffix[0:1, :], db + jnp.sum(df, axis=0, keepdims=True)

        _, db = lax.fori_loop(0, nb, step, (jnp.zeros((1, w), F32), jnp.zeros((1, w), F32)))
        db_ref[...] = db

    return pl.pallas_call(
        body, out_shape=(jax.ShapeDtypeStruct((seq, w), F32), jax.ShapeDtypeStruct((1, w), F32)),
        compiler_params=_params(), name="fgate_bwd",
    )(fl, bf, dcum)


def _cum_layouts(cum):
    seq = cum.shape[0]
    nblk = seq // ROW_TILE
    heads = cum[:, :N_FOX_HEADS]
    cq = jnp.repeat(heads, HEAD_DIM, axis=1).reshape(seq, N_HEAD_PAIRS, LANE).transpose(1, 0, 2)
    ck = heads.T.reshape(N_HEAD_PAIRS, 2, nblk, ROW_TILE).transpose(0, 2, 1, 3)
    ck = jnp.pad(ck, ((0, 0), (0, 0), (0, 6), (0, 0)))
    return cq, ck


def _dcum_from_layouts(dcq, dck):
    seq = dcq.shape[1]
    key_side = dck[:, :, :2, :].transpose(0, 2, 1, 3).reshape(N_FOX_HEADS, seq).T
    query_side = dcq[:, :, ::HEAD_DIM].transpose(1, 0, 2).reshape(seq, N_FOX_HEADS)
    return jnp.pad(key_side + query_side, ((0, 0), (0, F_PAD - N_FOX_HEADS)))


def _local_step(x, mem, target, w, arrive, emit):
    vec = lambda a: a.reshape(1, -1)
    mem_n = _rms_fwd(mem, vec(w["g_mem"]))
    saved = []
    shared = None
    for l in range(N_LAYERS):
        arrive(l, x)
        s = {"x_in": x}
        s["h"] = h = _rms_fwd(x, vec(w["g_mix"][l]))
        s["mk"] = mk = _mm(mem_n, w["w_mk"][l])
        s["mv"] = mv = _mm(mem_n, w["w_mv"][l])
        if l < N_A:
            s["pa"] = pa = _mm(h, w["w_a"][l])
            s["pg"] = pg = _mm(h, w["w_g"][l])
            s["pq"] = pq = _mm(h, w["w_qm"][l])
            s["v"], s["c"] = _conv_glu_fwd(pa, pg, vec(w["b_glu_a"][l]), vec(w["b_glu_g"][l]), w["w_dw_a"][l], vec(w["b_dw_a"][l]))
            s["mix"] = mix = _ln_silu_fwd(s["c"], vec(w["ln_g"][l]), vec(w["ln_b"][l]))
        else:
            if l == N_A:
                shared = {"x_in": x}
                shared["hk"] = hk = _rms_fwd(x, vec(w["g_kv"]))
                shared["k"] = _mm(hk, w["w_k"])
                shared["v"] = _mm(hk, w["w_v"])
                shared["fl"] = _mm(hk, w["w_f"])
                cum = _fgate_fwd(shared["fl"], w["b_f"])
                shared["cq"], shared["ck"] = _cum_layouts(cum)
            s["q"] = q = _mm(h, w["w_q"][l - N_A])
            s["pq"] = pq = _mm(h, w["w_qm"][l])
            s["o"], mix, s["lse"] = _fox_fwd(q, shared["k"], shared["v"], shared["cq"], shared["ck"])
            s["mix"] = mix
        s["mo"] = mo = _memattn_fwd(pq, mk, mv)
        x = _mm(mix, w["w_o_mix"][l], add=x)
        s["x_mid"] = x = _mm(mo, w["w_o_mem"][l], add=x)
        s["hf"] = hf = _rms_fwd(x, vec(w["g_ffn"][l]))
        s["ug"] = ug = _mm(hf, w["w_up_g"][l])
        s["uv"] = uv = _mm(hf, w["w_up_v"][l])
        s["z"] = z = _ffn_mid_fwd(ug, uv, w["w_dw_f_g"][l], w["w_dw_f_v"][l], vec(w["b_dw_f_g"][l]), vec(w["b_dw_f_v"][l]))
        x = _mm(z, w["w_down"][l], add=x)
        saved.append(s)

    loss_row, dx, dxb, dg_final = _loss_head(x, vec(w["g_final"]), target)
    loss = loss_row[0, 0]

    g = {n: [None] * N_LAYERS for n in ("g_mix", "w_mem_kv", "w_out", "g_ffn", "w_up", "w_dw_f", "b_dw_f", "w_down")}
    for n in ("w_in_a", "b_glu", "w_dw_a", "b_dw_a", "ln_g", "ln_b", "w_in_b"):
        g[n] = [None] * N_A
    one = lambda arr: ([arr], 0)
    g["g_final"] = [one(dg_final[0])]
    dmem_n = None
    dk_sum = dv_sum = dcq_sum = dck_sum = None
    for l in reversed(range(N_LAYERS)):
        s = saved[l]
        dz = _mm(dxb, w["w_down"][l], "nt")
        g["w_down"][l] = one(_mm(s["z"], dxb, "tn", BF16))
        dug, duv, dwg, dwv, dbg, dbv = _ffn_mid_bwd(s["ug"], s["uv"], dz, w["w_dw_f_g"][l], w["w_dw_f_v"][l],
                                                    vec(w["b_dw_f_g"][l]), vec(w["b_dw_f_v"][l]))
        g["w_up"][l] = ([_mm(s["hf"], dug, "tn", BF16), _mm(s["hf"], duv, "tn", BF16)], 1)
        g["w_dw_f"][l] = ([dwg[:FFN_CONV_W], dwv[:FFN_CONV_W]], 1)
        g["b_dw_f"][l] = ([dbg[0], dbv[0]], 0)
        dhf = _mm(duv, w["w_up_v"][l], "nt", add=_mm(dug, w["w_up_g"][l], "nt"))
        dx, dxb, dg_ffn = _rms_bwd(s["x_mid"], vec(w["g_ffn"][l]), dhf, dx)
        g["g_ffn"][l] = one(dg_ffn[0])
        dmix = _mm(dxb, w["w_o_mix"][l], "nt")
        dmo = _mm(dxb, w["w_o_mem"][l], "nt")
        g["w_out"][l] = ([_mm(s["mix"], dxb, "tn", BF16), _mm(s["mo"], dxb, "tn", BF16)], 0)
        dpq, dmk, dmv = _memattn_bwd(s["pq"], s["mk"], s["mv"], dmo)
        g["w_mem_kv"][l] = ([_mm(mem_n, dmk, "tn", BF16), _mm(mem_n, dmv, "tn", BF16)], 1)
        dmem_n = _mm(dmk, w["w_mk"][l], "nt", add=dmem_n)
        dmem_n = _mm(dmv, w["w_mv"][l], "nt", add=dmem_n)
        if l < N_A:
            dc, dlng, dlnb = _ln_silu_bwd(dmix, s["c"], vec(w["ln_g"][l]), vec(w["ln_b"][l]))
            da, dgate, dwdw, dbdw, dba, dbg2 = _conv_glu_bwd(dc, s["v"], s["pa"], s["pg"], vec(w["b_glu_a"][l]),
                                                              vec(w["b_glu_g"][l]), w["w_dw_a"][l])
            g["ln_g"][l], g["ln_b"][l], g["b_dw_a"][l] = one(dlng[0]), one(dlnb[0]), one(dbdw[0])
            g["w_dw_a"][l] = ([dwdw[:CONV_W]], 1)
            g["b_glu"][l] = ([dba[0], dbg2[0]], 0)
            g["w_in_a"][l] = ([_mm(s["h"], da, "tn", BF16), _mm(s["h"], dgate, "tn", BF16), _mm(s["h"], dpq, "tn", BF16)], 1)
            dh = _mm(da, w["w_a"][l], "nt")
            dh = _mm(dgate, w["w_g"][l], "nt", add=dh)
            dh = _mm(dpq, w["w_qm"][l], "nt", add=dh)
        else:
            dq, dk, dv, dcq, dck = _fox_bwd(s["q"], shared["k"], shared["v"], shared["cq"], shared["ck"], s["o"], s["lse"], dmix)
            dk_sum = dk if dk_sum is None else dk_sum + dk
            dv_sum = dv if dv_sum is None else dv_sum + dv
            dcq_sum = dcq if dcq_sum is None else dcq_sum + dcq
            dck_sum = dck if dck_sum is None else dck_sum + dck
            g["w_in_b"][l - N_A] = ([_mm(s["h"], dq, "tn", BF16), _mm(s["h"], dpq, "tn", BF16)], 1)
            dh = _mm(dq, w["w_q"][l - N_A], "nt")
            dh = _mm(dpq, w["w_qm"][l], "nt", add=dh)
        dx, dxb, dg_mix = _rms_bwd(s["x_in"], vec(w["g_mix"][l]), dh, dx)
        g["g_mix"][l] = one(dg_mix[0])
        if l == N_A:
            df, dbf = _fgate_bwd(shared["fl"], w["b_f"], _dcum_from_layouts(dcq_sum, dck_sum))
            hk = shared["hk"]
            g["w_kvf"] = [([_mm(hk, dk_sum, "tn", BF16), _mm(hk, dv_sum, "tn", BF16), _mm(hk, df, "tn", BF16)[:, :N_FOX_HEADS]], 1)]
            g["b_f"] = [one(dbf[0, :N_FOX_HEADS])]
            dhk = _mm(dk_sum, w["w_k"], "nt")
            dhk = _mm(dv_sum, w["w_v"], "nt", add=dhk)
            dhk = _mm(df, w["w_f"], "nt", add=dhk)
            dx, dxb, dg_kv = _rms_bwd(shared["x_in"], vec(w["g_kv"]), dhk, dx)
            g["g_kv"] = [one(dg_kv[0])]
        emit(l, g)
    _, _, dg_mem = _rms_bwd(mem, vec(w["g_mem"]), dmem_n, None)
    g["g_mem"] = [one(dg_mem[0])]
    return loss, dx, g


SHARDED = {
    "w_in_a": ((N_A, D_MODEL, 2 * CONV_CH + MEM_W), 2),
    "b_glu": ((N_A, 2 * CONV_CH), 1),
    "w_dw_a": ((N_A, CONV_W, CONV_CH), 2),
    "b_dw_a": ((N_A, CONV_CH), 1),
    "ln_g": ((N_A, CONV_CH), 1),
    "ln_b": ((N_A, CONV_CH), 1),
    "w_kvf": ((D_MODEL, 2 * CONV_CH + N_FOX_HEADS), 1),
    "w_in_b": ((N_LAYERS - N_A, D_MODEL, CONV_CH + MEM_W), 1),
    "w_mem_kv": ((N_LAYERS, D_MODEL, 2 * MEM_W), 1),
    "w_out": ((N_LAYERS, D_MODEL, D_MODEL), 1),
    "w_up": ((N_LAYERS, D_MODEL, 2 * D_FF), 2),
    "w_dw_f": ((N_LAYERS, FFN_CONV_W, 2 * D_FF), 2),
    "w_down": ((N_LAYERS, D_FF, D_MODEL), 1),
}
REPLICATED = {
    "g_mix": (N_LAYERS, D_MODEL), "g_kv": (D_MODEL,), "b_f": (N_FOX_HEADS,), "g_mem": (D_MODEL,),
    "g_ffn": (N_LAYERS, D_MODEL), "b_dw_f": (N_LAYERS, 2 * D_FF), "g_final": (D_MODEL,),
}
WEIGHT_ORDER = ["g_mix", "w_in_a", "b_glu", "w_dw_a", "b_dw_a", "ln_g", "ln_b", "g_kv", "w_kvf", "b_f", "w_in_b", "g_mem",
                "w_mem_kv", "w_out", "g_ffn", "w_up", "w_dw_f", "b_dw_f", "w_down", "g_final"]


def _shard_shape(name):
    shape, axis = SHARDED[name]
    return tuple(d // N_CHIPS if i == axis else d for i, d in enumerate(shape))


def _size(shape):
    n = 1
    for d in shape:
        n *= d
    return n


VECTOR_LEAVES = ("b_glu", "w_dw_a", "b_dw_a", "ln_g", "ln_b", "w_dw_f")
_HALF_ROW_TILE = 16
_VECTOR_ELEMS = sum(_size(_shard_shape(n)) for n in VECTOR_LEAVES)
VECTOR_ROWS = -(-_VECTOR_ELEMS // (FLAT_COLS * 2 * _HALF_ROW_TILE)) * 2 * _HALF_ROW_TILE
REP_ELEMS = sum(_size(s) for s in REPLICATED.values())
REP_ROWS = 8
REP_COLS = -(-REP_ELEMS // (REP_ROWS * LANE)) * LANE


def _flatten_vectors(parts, dtype):
    flat = jnp.concatenate([p.reshape(-1).astype(dtype) for p in parts])
    return jnp.pad(flat, (0, VECTOR_ROWS * FLAT_COLS - _VECTOR_ELEMS)).reshape(2, VECTOR_ROWS // 2, FLAT_COLS)


def _unflatten_vectors(flat, lead=()):
    flat = flat.reshape(lead + (-1,))
    out, off = {}, 0
    for n in VECTOR_LEAVES:
        shp = _shard_shape(n)
        out[n] = flat[..., off:off + _size(shp)].reshape(lead + shp)
        off += _size(shp)
    return out


def _span(parts, cat_axis, sel_axis, lo, hi):
    if sel_axis != cat_axis:
        return jnp.concatenate([lax.slice_in_dim(p, lo, hi, axis=sel_axis) for p in parts], axis=cat_axis)
    taken, off = [], 0
    for p in parts:
        n = p.shape[cat_axis]
        a, b = max(lo, off), min(hi, off + n)
        if a < b:
            taken.append(lax.slice_in_dim(p, a - off, b - off, axis=cat_axis))
        off += n
    return taken[0] if len(taken) == 1 else jnp.concatenate(taken, axis=cat_axis)


LAYER_LEAVES = tuple(
    ([("w_in_a", l)] if l < N_A else ([("w_kvf", None)] if l == N_A else []) + [("w_in_b", l - N_A)])
    + [("w_mem_kv", l), ("w_out", l), ("w_up", l), ("w_down", l)]
    for l in range(N_LAYERS))


def _layer_halves(weights, l):
    out = []
    for name, idx in LAYER_LEAVES[l]:
        shard = weights[name] if idx is None else weights[name][idx]
        rows, cols = shard.shape
        assert (rows // 2) % _HALF_ROW_TILE == 0, name
        out.append(shard.astype(BF16).reshape(2, rows // 2, cols))
    return out


def _layer_pieces(l, gathered, vectors):
    parts = {}
    for (name, idx), g in zip(LAYER_LEAVES[l], gathered):
        parts[name] = [g[t].reshape(-1, g.shape[-1]) for t in range(N_CHIPS)]

    def cut(name, sel_axis, lo, hi):
        axis = SHARDED[name][1] - (len(SHARDED[name][0]) - 2)
        return _span(parts[name], axis, sel_axis, lo, hi)

    def vcut(name, idx, sel_axis, lo, hi):
        axis = SHARDED[name][1] - 1
        return _span([vectors[name][t, idx] for t in range(N_CHIPS)], axis, sel_axis, lo, hi)

    def pad_rows(arr, rows):
        return jnp.pad(arr, ((0, rows - arr.shape[0]), (0, 0)))

    w = {}
    if l < N_A:
        w["w_a"] = cut("w_in_a", 1, 0, CONV_CH)
        w["w_g"] = cut("w_in_a", 1, CONV_CH, 2 * CONV_CH)
        w["w_qm"] = cut("w_in_a", 1, 2 * CONV_CH, 2 * CONV_CH + MEM_W)
        w["b_glu_a"] = vcut("b_glu", l, 0, 0, CONV_CH)
        w["b_glu_g"] = vcut("b_glu", l, 0, CONV_CH, 2 * CONV_CH)
        w["w_dw_a"] = pad_rows(vcut("w_dw_a", l, 1, 0, CONV_CH), CONV_PAD)
        for n in ("b_dw_a", "ln_g", "ln_b"):
            w[n] = vcut(n, l, 0, 0, CONV_CH)
    else:
        if l == N_A:
            w["w_k"] = cut("w_kvf", 1, 0, CONV_CH)
            w["w_v"] = cut("w_kvf", 1, CONV_CH, 2 * CONV_CH)
            w["w_f"] = jnp.pad(cut("w_kvf", 1, 2 * CONV_CH, 2 * CONV_CH + N_FOX_HEADS), ((0, 0), (0, F_PAD - N_FOX_HEADS)))
        w["w_q"] = cut("w_in_b", 1, 0, CONV_CH)
        w["w_qm"] = cut("w_in_b", 1, CONV_CH, CONV_CH + MEM_W)
    w["w_mk"] = cut("w_mem_kv", 1, 0, MEM_W)
    w["w_mv"] = cut("w_mem_kv", 1, MEM_W, 2 * MEM_W)
    w["w_o_mix"] = cut("w_out", 0, 0, CONV_CH)
    w["w_o_mem"] = cut("w_out", 0, CONV_CH, D_MODEL)
    w["w_up_g"] = cut("w_up", 1, 0, D_FF)
    w["w_up_v"] = cut("w_up", 1, D_FF, 2 * D_FF)
    w["w_dw_f_g"] = pad_rows(vcut("w_dw_f", l, 1, 0, D_FF), FFN_PAD)
    w["w_dw_f_v"] = pad_rows(vcut("w_dw_f", l, 1, D_FF, 2 * D_FF), FFN_PAD)
    w["w_down"] = cut("w_down", 0, 0, D_FF)
    return w


_PER_LAYER = ("w_a", "w_g", "w_q", "w_qm", "b_glu_a", "b_glu_g", "w_dw_a", "b_dw_a", "ln_g", "ln_b", "w_mk", "w_mv", "w_o_mix", "w_o_mem",
              "w_up_g", "w_up_v", "w_dw_f_g", "w_dw_f_v", "w_down")


def _weight_table(rep):
    w = dict(rep)
    w["b_f"] = jnp.pad(rep["b_f"], (0, F_PAD - N_FOX_HEADS)).reshape(1, F_PAD)
    w["b_dw_f_g"], w["b_dw_f_v"] = rep["b_dw_f"][:, :D_FF], rep["b_dw_f"][:, D_FF:]
    for n in _PER_LAYER:
        w[n] = {}
    return w


def _install_layer(w, l, pieces):
    for n, p in pieces.items():
        if n in _PER_LAYER:
            w[n][l - N_A if n == "w_q" else l] = p
        else:
            w[n] = p


def _shard_of(entry, name, t):
    shape, axis = SHARDED[name]
    pieces, cat_axis = entry
    axis -= len(shape) - len(pieces[0].shape)
    step = shape[SHARDED[name][1]] // N_CHIPS
    return _span(pieces, cat_axis, axis, t * step, (t + 1) * step)


def _layer_slabs(grads, l):
    out = []
    for name, idx in LAYER_LEAVES[l]:
        shards = [_shard_of(grads[name][0 if idx is None else idx], name, t).astype(BF16) for t in range(N_CHIPS)]
        rows, cols = shards[0].shape
        out.append(jnp.stack(shards).reshape(N_CHIPS, 2, rows // 2, cols))
    return out


def _vector_slabs(grads):
    def shard(name, t):
        return jnp.stack([_shard_of(entry, name, t) for entry in grads[name]])
    return jnp.stack([_flatten_vectors([shard(n, t) for n in VECTOR_LEAVES], BF16) for t in range(N_CHIPS)])


def _replicated_rows(grads):
    parts = [p.reshape(-1) for n in REPLICATED for pieces, _ in grads[n] for p in pieces]
    flat = jnp.concatenate(parts)
    return jnp.pad(flat, (0, REP_ROWS * REP_COLS - REP_ELEMS)).reshape(REP_ROWS, REP_COLS)


_ANY = pl.BlockSpec(memory_space=pl.ANY)


def _place():
    x, y, c = lax.axis_index("x"), lax.axis_index("y"), lax.axis_index("c")
    chips = [(1 - x, y), (x, 1 - y), (1 - x, 1 - y)]
    return x, y, c, chips


def _chip_index(chip):
    return 2 * chip[0] + chip[1]


def _gather_leaves(halves):
    n = len(halves)

    def body(*refs):
        w_refs, out_refs, (send_sems, recv_sems) = refs[:n], refs[n:2 * n], refs[2 * n:]
        x, y, c, chips = _place()
        me = 2 * x + y
        sibling = (x, y, 1 - c)

        def copy(a, sem, chip_idx, half, to, src=None):
            dst = out_refs[a].at[chip_idx, half]
            return pltpu.make_async_remote_copy(src_ref=dst if src is None else src, dst_ref=dst, send_sem=send_sems.at[6 * a + sem],
                                                recv_sem=recv_sems.at[6 * a + sem], device_id=to, device_id_type=MESH_ID)

        first = [copy(a, j, me, c, (*chip, c), src=w_refs[a].at[c]) for j, chip in enumerate(chips) for a in range(n)]
        for cp in first:
            cp.start()
        passed = []
        for j, chip in enumerate(chips):
            for a in range(n):
                copy(a, j, _chip_index(chip), c, (x, y, c)).wait_recv()
                passed.append(copy(a, 3 + j, _chip_index(chip), c, sibling))
                passed[-1].start()
        for j, chip in enumerate(chips):
            for a in range(n):
                copy(a, 3 + j, _chip_index(chip), 1 - c, (x, y, c)).wait_recv()
        for cp in first + passed:
            cp.wait_send()

    return pl.pallas_call(
        body, out_shape=[jax.ShapeDtypeStruct((N_CHIPS,) + h.shape, h.dtype) for h in halves],
        in_specs=[_ANY] * n, out_specs=[_ANY] * n,
        scratch_shapes=[pltpu.SemaphoreType.DMA((6 * n,)), pltpu.SemaphoreType.DMA((6 * n,))], name="gather_leaves",
    )(*halves)


_HBM = pl.BlockSpec(memory_space=pltpu.HBM)
_SEM = pl.BlockSpec(memory_space=pltpu.SEMAPHORE)


def _in_hbm(a):
    return pltpu.with_memory_space_constraint(a, pltpu.HBM)


def _gather_start(groups, after):
    flat = [h for g in groups for h in g]
    n, ng = len(flat), len(groups)

    def body(*refs):
        srcs, lands, sems = refs[1:1 + n], refs[1 + n:1 + 2 * n], refs[1 + 2 * n:1 + 2 * n + 2 * ng]
        token = refs[-1]
        x, y, c, chips = _place()
        me = 2 * x + y
        a = 0
        for gi, g in enumerate(groups):
            for k in range(len(g)):
                for j, chip in enumerate(chips):
                    pltpu.make_async_remote_copy(src_ref=srcs[a].at[c], dst_ref=lands[a].at[me, c], send_sem=sems[2 * gi].at[3 * k + j],
                                                 recv_sem=sems[2 * gi + 1].at[3 * k + j], device_id=(*chip, c), device_id_type=MESH_ID).start()
                a += 1
        token[...] = jnp.zeros_like(token)

    sem_shapes = [pltpu.SemaphoreType.DMA((3 * len(g),)) for g in groups for _ in range(2)]
    out = pl.pallas_call(
        body, name="gather_start",
        out_shape=sem_shapes + [pltpu.HBM(h.shape, h.dtype) for h in flat] + [pltpu.HBM((N_CHIPS,) + h.shape, h.dtype) for h in flat]
        + [jax.ShapeDtypeStruct((8, LANE), F32)],
        in_specs=[_ANY] + [_HBM] * (2 * n), out_specs=[_SEM] * (2 * ng) + [_HBM] * (2 * n) + [pl.BlockSpec(memory_space=pltpu.VMEM)],
        input_output_aliases={1 + i: 2 * ng + i for i in range(2 * n)},
        compiler_params=pltpu.CompilerParams(has_side_effects=pltpu.SideEffectType.DATAFLOW_SIDE_EFFECTING),
    )(after, *[_in_hbm(h) for h in flat], *[_in_hbm(lax.empty((N_CHIPS,) + h.shape, h.dtype)) for h in flat])
    sems, srcs, lands = out[:2 * ng], out[2 * ng:2 * ng + n], out[2 * ng + n:2 * ng + 2 * n]
    res, a = [], 0
    for gi, g in enumerate(groups):
        res.append((sems[2 * gi], sems[2 * gi + 1], list(srcs[a:a + len(g)]), list(lands[a:a + len(g)])))
        a += len(g)
    return res


def _gather_wait(send_sems, recv_sems, srcs, lands, after, name):
    n = len(srcs)

    def body(*refs):
        src_refs, land_refs, send, recv = refs[:n], refs[n:2 * n], refs[2 * n], refs[2 * n + 1]
        x, y, c, chips = _place()
        for k in range(n):
            for j, chip in enumerate(chips):
                cp = pltpu.make_async_remote_copy(src_ref=src_refs[k].at[c], dst_ref=land_refs[k].at[_chip_index(chip), c], send_sem=send.at[3 * k + j],
                                                  recv_sem=recv.at[3 * k + j], device_id=(*chip, c), device_id_type=MESH_ID)
                cp.wait_send()
                cp.wait_recv()

    out = pl.pallas_call(
        body, name=name, out_shape=[pltpu.HBM(a.shape, a.dtype) for a in list(srcs) + list(lands)],
        in_specs=[_HBM] * (2 * n) + [_SEM, _SEM, _ANY], out_specs=[_HBM] * (2 * n), input_output_aliases={i: i for i in range(2 * n)},
        compiler_params=pltpu.CompilerParams(has_side_effects=pltpu.SideEffectType.DATAFLOW_SIDE_EFFECTING),
    )(*srcs, *lands, send_sems, recv_sems, after)
    return list(out[n:])


def _forward_halves(lands, name):
    n = len(lands)

    def body(*refs):
        out_refs, (send_sems, recv_sems) = refs[n:2 * n], refs[2 * n:]
        x, y, c, chips = _place()

        def copy(a, j, half, to):
            blk = out_refs[a].at[_chip_index(chips[j]), half]
            return pltpu.make_async_remote_copy(src_ref=blk, dst_ref=blk, send_sem=send_sems.at[3 * a + j], recv_sem=recv_sems.at[3 * a + j],
                                                device_id=to, device_id_type=MESH_ID)

        sends = [copy(a, j, c, (x, y, 1 - c)) for a in range(n) for j in range(3)]
        for cp in sends:
            cp.start()
        for a in range(n):
            for j in range(3):
                copy(a, j, 1 - c, (x, y, c)).wait_recv()
        for cp in sends:
            cp.wait_send()

    return pl.pallas_call(
        body, out_shape=[jax.ShapeDtypeStruct(a.shape, a.dtype) for a in lands], in_specs=[_ANY] * n, out_specs=[_ANY] * n,
        input_output_aliases={i: i for i in range(n)},
        scratch_shapes=[pltpu.SemaphoreType.DMA((3 * n,)), pltpu.SemaphoreType.DMA((3 * n,))], name=name,
    )(*lands)


def _swap_halves(slabs):
    n = len(slabs)

    def body(*refs):
        g_refs, got_refs, (send_sem, recv_sem) = refs[:n], refs[n:2 * n], refs[2 * n:]
        x, y, c, _ = _place()
        copies = [pltpu.make_async_remote_copy(src_ref=g_refs[a].at[t, 1 - c], dst_ref=got_refs[a].at[t], send_sem=send_sem.at[N_CHIPS * a + t],
                                               recv_sem=recv_sem.at[N_CHIPS * a + t], device_id=(x, y, 1 - c), device_id_type=MESH_ID)
                  for a in range(n) for t in range(N_CHIPS)]
        for cp in copies:
            cp.start()
        for cp in copies:
            cp.wait()

    return pl.pallas_call(
        body, out_shape=[jax.ShapeDtypeStruct((g.shape[0],) + g.shape[2:], g.dtype) for g in slabs],
        in_specs=[_ANY] * n, out_specs=[_ANY] * n,
        scratch_shapes=[pltpu.SemaphoreType.DMA((N_CHIPS * n,)), pltpu.SemaphoreType.DMA((N_CHIPS * n,))], name="swap_halves",
    )(*slabs)


def _scatter_start(parts, name):
    n = len(parts)

    def body(*refs):
        srcs, lands, send, recv = refs[:n], refs[n:2 * n], refs[2 * n], refs[2 * n + 1]
        token = refs[-1]
        x, y, c, chips = _place()
        me = 2 * x + y
        for k in range(n):
            for j, chip in enumerate(chips):
                pltpu.make_async_remote_copy(src_ref=srcs[k].at[_chip_index(chip)], dst_ref=lands[k].at[me], send_sem=send.at[3 * k + j],
                                             recv_sem=recv.at[3 * k + j], device_id=(*chip, c), device_id_type=MESH_ID).start()
        token[...] = jnp.zeros_like(token)

    out = pl.pallas_call(
        body, name=name,
        out_shape=[pltpu.SemaphoreType.DMA((3 * n,)), pltpu.SemaphoreType.DMA((3 * n,))] + [pltpu.HBM(p.shape, p.dtype) for p in parts] * 2
        + [jax.ShapeDtypeStruct((8, LANE), F32)],
        in_specs=[_HBM] * (2 * n), out_specs=[_SEM, _SEM] + [_HBM] * (2 * n) + [pl.BlockSpec(memory_space=pltpu.VMEM)],
        input_output_aliases={i: 2 + i for i in range(2 * n)},
        compiler_params=pltpu.CompilerParams(has_side_effects=pltpu.SideEffectType.DATAFLOW_SIDE_EFFECTING),
    )(*[_in_hbm(p) for p in parts], *[_in_hbm(lax.empty(p.shape, p.dtype)) for p in parts])
    return out[0], out[1], list(out[2:2 + n]), list(out[2 + n:2 + 2 * n])


def _scatter_wait(send_sems, recv_sems, srcs, lands, after, name):
    n = len(srcs)

    def body(*refs):
        src_refs, land_refs, send, recv = refs[:n], refs[n:2 * n], refs[2 * n], refs[2 * n + 1]
        x, y, c, chips = _place()
        for k in range(n):
            for j, chip in enumerate(chips):
                cp = pltpu.make_async_remote_copy(src_ref=src_refs[k].at[_chip_index(chip)], dst_ref=land_refs[k].at[_chip_index(chip)],
                                                  send_sem=send.at[3 * k + j], recv_sem=recv.at[3 * k + j], device_id=(*chip, c), device_id_type=MESH_ID)
                cp.wait_send()
                cp.wait_recv()

    out = pl.pallas_call(
        body, name=name, out_shape=[pltpu.HBM(a.shape, a.dtype) for a in list(srcs) + list(lands)],
        in_specs=[_HBM] * (2 * n) + [_SEM, _SEM, _ANY], out_specs=[_HBM] * (2 * n), input_output_aliases={i: i for i in range(2 * n)},
        compiler_params=pltpu.CompilerParams(has_side_effects=pltpu.SideEffectType.DATAFLOW_SIDE_EFFECTING),
    )(*srcs, *lands, send_sems, recv_sems, after)
    return list(out[n:])


def _sibling_halves(halves):
    n = len(halves)

    def body(*refs):
        h_refs, got_refs, (send_sem, recv_sem) = refs[:n], refs[n:2 * n], refs[2 * n:]
        x, y, c, _ = _place()
        copies = [pltpu.make_async_remote_copy(src_ref=h_refs[a], dst_ref=got_refs[a], send_sem=send_sem.at[a], recv_sem=recv_sem.at[a],
                                               device_id=(x, y, 1 - c), device_id_type=MESH_ID) for a in range(n)]
        for cp in copies:
            cp.start()
        for cp in copies:
            cp.wait()

    return pl.pallas_call(
        body, out_shape=[jax.ShapeDtypeStruct(h.shape, h.dtype) for h in halves], in_specs=[_ANY] * n, out_specs=[_ANY] * n,
        scratch_shapes=[pltpu.SemaphoreType.DMA((n,)), pltpu.SemaphoreType.DMA((n,))], name="sibling_halves",
    )(*halves)


def _gather_replicated(rows):
    m_per, n = rows.shape

    def body(x_ref, out_ref, send_sems, recv_sems, local_sem):
        x, y, c, chips = _place()
        me, sibling = (x, y, c), (x, y, 1 - c)

        def block(px, py, pc):
            return out_ref.at[pl.ds((4 * px + 2 * py + pc) * m_per, m_per), :]

        def copy(k, blk, to, src=None):
            return pltpu.make_async_remote_copy(src_ref=block(*blk) if src is None else src, dst_ref=block(*blk),
                                                send_sem=send_sems.at[k], recv_sem=recv_sems.at[k], device_id=to, device_id_type=MESH_ID)

        mine = pltpu.make_async_copy(x_ref, block(*me), local_sem)
        mine.start()
        first = [copy(0, me, sibling, src=x_ref)]
        first += [copy(1 + j, me, (*chip, c), src=x_ref) for j, chip in enumerate(chips)]
        for cp in first:
            cp.start()
        passed = [copy(4 + j, (*chip, c), sibling) for j, chip in enumerate(chips)]
        for j, chip in enumerate(chips):
            copy(1 + j, (*chip, c), me).wait_recv()
            passed[j].start()
        copy(0, sibling, me).wait_recv()
        for j, chip in enumerate(chips):
            copy(4 + j, (*chip, 1 - c), me).wait_recv()
        for cp in first + passed:
            cp.wait_send()
        mine.wait()

    vmem = pl.BlockSpec(memory_space=pltpu.VMEM)
    return pl.pallas_call(
        body, out_shape=jax.ShapeDtypeStruct((N_DEV * m_per, n), rows.dtype), in_specs=[vmem], out_specs=vmem,
        scratch_shapes=[pltpu.SemaphoreType.DMA((7,)), pltpu.SemaphoreType.DMA((7,)), pltpu.SemaphoreType.DMA],
        name="gather_replicated",
    )(rows)


_SUM_ROWS = 256


def _tile_rows(rows):
    if rows <= _SUM_ROWS:
        return rows
    best = None
    for r in range(16, _SUM_ROWS + 1, 16):
        if rows % r == 0:
            best = r
    assert best is not None, rows
    return best


def _add_pairs(slabs, got, core):
    n, _, rows, cols = slabs.shape
    tr = _tile_rows(rows)

    def body(core_ref, a_ref, b_ref, o_ref):
        o_ref[...] = (a_ref[0].astype(F32) + b_ref[...].astype(F32)).astype(BF16)

    spec = pl.BlockSpec((1, tr, cols), lambda t, i, core_ref: (t, i, 0))
    return pl.pallas_call(
        body, out_shape=jax.ShapeDtypeStruct((n, rows, cols), BF16),
        grid_spec=pltpu.PrefetchScalarGridSpec(
            num_scalar_prefetch=1, grid=(n, rows // tr),
            in_specs=[pl.BlockSpec((1, 1, tr, cols), lambda t, i, core_ref: (t, core_ref[0], i, 0)), spec], out_specs=spec),
        compiler_params=_params(("parallel", "parallel")), name=f"add_pairs_{rows}x{cols}",
    )(core, slabs, got)


def _sum_slabs(slabs):
    n, rows, cols = slabs.shape
    tr = _tile_rows(rows)

    def body(s_ref, o_ref):
        acc = s_ref[0].astype(F32)
        for t in range(1, n):
            acc = acc + s_ref[t].astype(F32)
        o_ref[...] = acc

    return pl.pallas_call(
        body, out_shape=jax.ShapeDtypeStruct((rows, cols), F32), grid=(rows // tr,),
        in_specs=[pl.BlockSpec((n, tr, cols), lambda i: (0, i, 0))], out_specs=pl.BlockSpec((tr, cols), lambda i: (i, 0)),
        compiler_params=_params(("parallel",)), name=f"sum_slabs_{n}x{rows}x{cols}",
    )(slabs)


_ADAM_BLOCK_BYTES = 2 * 1024 * 1024


def _adamw(w, g, m, v):
    shape = w.shape
    cols = shape[-1]
    rows = _size(shape) // cols
    tr = rows
    if rows * cols * 4 > _ADAM_BLOCK_BYTES:
        for r in range(8, rows, 8):
            if rows % r == 0 and r * cols * 4 <= _ADAM_BLOCK_BYTES:
                tr = r

    def body(w_ref, g_ref, m_ref, v_ref, d_ref, nm_ref, nv_ref):
        gv = g_ref[...]
        nm = ADAM_B1 * m_ref[...] + (1.0 - ADAM_B1) * gv
        nv = ADAM_B2 * v_ref[...] + (1.0 - ADAM_B2) * jnp.square(gv)
        m_hat = nm / (1.0 - ADAM_B1 ** ADAM_STEP)
        v_hat = nv / (1.0 - ADAM_B2 ** ADAM_STEP)
        d_ref[...] = -ADAM_LR * (m_hat / (jnp.sqrt(v_hat) + ADAM_EPS) + ADAM_WD * w_ref[...])
        nm_ref[...] = nm
        nv_ref[...] = nv

    spec = pl.BlockSpec((tr, cols), lambda i: (i, 0))
    out = pl.pallas_call(
        body, out_shape=tuple(jax.ShapeDtypeStruct((rows, cols), F32) for _ in range(3)), grid=(rows // tr,),
        in_specs=[spec] * 4, out_specs=(spec,) * 3, compiler_params=_params(("parallel",)), name=f"adamw_{rows}x{cols}",
    )(*(a.reshape(rows, cols) for a in (w, g, m, v)))
    return tuple(o.reshape(shape) for o in out)


def kernel(x, mem, g_mix, w_in_a, b_glu, w_dw_a, b_dw_a, ln_g, ln_b, g_kv, w_kvf, b_f, w_in_b, g_mem, w_mem_kv, w_out, g_ffn, w_up, w_dw_f, b_dw_f, w_down, g_final, loss_target, m_g_mix, m_w_in_a, m_b_glu, m_w_dw_a, m_b_dw_a, m_ln_g, m_ln_b, m_g_kv, m_w_kvf, m_b_f, m_w_in_b, m_g_mem, m_w_mem_kv, m_w_out, m_g_ffn, m_w_up, m_w_dw_f, m_b_dw_f, m_w_down, m_g_final, v_g_mix, v_w_in_a, v_b_glu, v_w_dw_a, v_b_dw_a, v_ln_g, v_ln_b, v_g_kv, v_w_kvf, v_b_f, v_w_in_b, v_g_mem, v_w_mem_kv, v_w_out, v_g_ffn, v_w_up, v_w_dw_f, v_b_dw_f, v_w_down, v_g_final):
    weights = dict(g_mix=g_mix, w_in_a=w_in_a, b_glu=b_glu, w_dw_a=w_dw_a, b_dw_a=b_dw_a, ln_g=ln_g, ln_b=ln_b, g_kv=g_kv,
                   w_kvf=w_kvf, b_f=b_f, w_in_b=w_in_b, g_mem=g_mem, w_mem_kv=w_mem_kv, w_out=w_out, g_ffn=g_ffn, w_up=w_up,
                   w_dw_f=w_dw_f, b_dw_f=b_dw_f, w_down=w_down, g_final=g_final)
    mom1 = dict(g_mix=m_g_mix, w_in_a=m_w_in_a, b_glu=m_b_glu, w_dw_a=m_w_dw_a, b_dw_a=m_b_dw_a, ln_g=m_ln_g, ln_b=m_ln_b,
                g_kv=m_g_kv, w_kvf=m_w_kvf, b_f=m_b_f, w_in_b=m_w_in_b, g_mem=m_g_mem, w_mem_kv=m_w_mem_kv, w_out=m_w_out,
                g_ffn=m_g_ffn, w_up=m_w_up, w_dw_f=m_w_dw_f, b_dw_f=m_b_dw_f, w_down=m_w_down, g_final=m_g_final)
    mom2 = dict(g_mix=v_g_mix, w_in_a=v_w_in_a, b_glu=v_b_glu, w_dw_a=v_w_dw_a, b_dw_a=v_b_dw_a, ln_g=v_ln_g, ln_b=v_ln_b,
                g_kv=v_g_kv, w_kvf=v_w_kvf, b_f=v_b_f, w_in_b=v_w_in_b, g_mem=v_g_mem, w_mem_kv=v_w_mem_kv, w_out=v_w_out,
                g_ffn=v_g_ffn, w_up=v_w_up, w_dw_f=v_w_dw_f, b_dw_f=v_b_dw_f, w_down=v_w_down, g_final=v_g_final)

    x_pos, y_pos, core = lax.axis_index("x"), lax.axis_index("y"), lax.axis_index("c")
    chip = 2 * x_pos + y_pos

    local = [_layer_halves(weights, l) for l in range(N_LAYERS)]
    vector_halves = _flatten_vectors([weights[n] for n in VECTOR_LEAVES], F32)
    first = _gather_leaves(local[0] + [vector_halves])
    in_flight = _gather_start(local[1:], first[-1])

    def with_own(gathered, own):
        return [lax.dynamic_update_index_in_dim(g, o[None], chip, 0) for g, o in zip(gathered, own)]

    first = with_own(first, local[0] + [vector_halves])
    vectors = _unflatten_vectors(first[-1], (N_CHIPS,))
    table = _weight_table({n: weights[n] for n in REPLICATED})

    def arrive(l, after):
        if l == 0:
            gathered = first[:-1]
        else:
            send_sems, recv_sems, srcs, lands = in_flight[l - 1]
            lands = _gather_wait(send_sems, recv_sems, srcs, lands, after, f"gather_wait_{l}")
            gathered = with_own(_forward_halves(lands, f"forward_halves_{l}"), local[l])
        _install_layer(table, l, _layer_pieces(l, gathered, vectors))

    core_index = core.astype(jnp.int32).reshape(1)
    in_flight_back = {}

    def pre_sum(slabs):
        return [_add_pairs(mine, got, core_index) for mine, got in zip(slabs, _swap_halves(slabs))]

    def emit(l, g):
        slabs = _layer_slabs(g, l) + ([_vector_slabs(g)] if l == 0 else [])
        chip_sums = pre_sum(slabs)
        in_flight_back[l] = (chip_sums, _scatter_start(chip_sums, f"scatter_start_{l}"))

    loss, dx, grads = _local_step(x[0], mem[0], loss_target[0], table, arrive, emit)
    loss = lax.psum(loss, ("x", "y", "c"))

    mine = {}
    for l in reversed(range(N_LAYERS)):
        chip_sums, (send_sems, recv_sems, srcs, lands) = in_flight_back[l]
        lands = _scatter_wait(send_sems, recv_sems, srcs, lands, dx, f"scatter_wait_{l}")
        from_chips = [lax.dynamic_update_index_in_dim(got, lax.dynamic_index_in_dim(own, chip, 0, keepdims=True), chip, 0)
                      for got, own in zip(lands, chip_sums)]
        mine[l] = [_sum_slabs(f) for f in from_chips]
    flat_mine = [m for l in range(N_LAYERS) for m in mine[l]]
    theirs = _sibling_halves(flat_mine)
    joined = [jnp.concatenate([jnp.where(core == 0, m, t), jnp.where(core == 0, t, m)]) for m, t in zip(flat_mine, theirs)]
    per_leaf, a = {}, 0
    for l in range(N_LAYERS):
        for name, idx in LAYER_LEAVES[l]:
            per_leaf.setdefault(name, {})[idx] = joined[a]
            a += 1
        if l == 0:
            vector_grads = joined[a]
            a += 1
    grad_leaves = {n: (d[None] if None in d else jnp.stack([d[i] for i in sorted(d)])) for n, d in per_leaf.items()}
    grad_leaves.update(_unflatten_vectors(vector_grads))

    rep_sum = _sum_slabs(_gather_replicated(_replicated_rows(grads)).reshape(N_DEV, REP_ROWS, REP_COLS)).reshape(-1)
    off = 0
    for n, shp in REPLICATED.items():
        grad_leaves[n] = rep_sum[off:off + _size(shp)].reshape(shp)
        off += _size(shp)

    deltas, new_m, new_v = {}, {}, {}
    for n in WEIGHT_ORDER:
        deltas[n], new_m[n], new_v[n] = _adamw(weights[n], grad_leaves[n], mom1[n], mom2[n])
    return (loss, dx[None], *[grad_leaves[n] for n in WEIGHT_ORDER], *[deltas[n] for n in WEIGHT_ORDER],
            *[new_m[n] for n in WEIGHT_ORDER], *[new_v[n] for n in WEIGHT_ORDER])
```

```python
import jax
import jax.numpy as jnp
from jax import lax
from jax.experimental import pallas as pl
from jax.experimental.pallas import tpu as pltpu

F32 = jnp.float32
BF16 = jnp.bfloat16

D_MODEL = 1024
N_LAYERS = 4
N_A = 2
CONV_CH = 768
MEM_W = 256
HEAD_DIM = 64
N_MEM_HEADS = 4
N_FOX_HEADS = 12
N_HEAD_PAIRS = N_FOX_HEADS // 2
D_FF = 2816
CONV_W = 31
CONV_PAD = 32
FFN_CONV_W = 3
FFN_PAD = 8
F_PAD = 128
RMS_EPS = 1e-6
LN_EPS = 1e-5
ATT_SCALE = HEAD_DIM ** -0.5
NEG_BIG = -1e30

ADAM_LR = 0.001
ADAM_B1 = 0.9
ADAM_B2 = 0.999
ADAM_EPS = 1e-08
ADAM_WD = 0.01
ADAM_STEP = 10

LANE = 128
ROW_TILE = 256
CHUNK = 128
VMEM_LIMIT = 48 * 1024 * 1024
FLAT_COLS = 1024
N_CHIPS = 4
N_DEV = 8
MESH_ID = pl.DeviceIdType.MESH


def _params(sem=None):
    return pltpu.CompilerParams(dimension_semantics=sem, vmem_limit_bytes=VMEM_LIMIT)


def _tile(dim, pref):
    if dim <= pref:
        return dim
    best = None
    for m in range(1, dim // LANE + 1):
        d = m * LANE
        if dim % d == 0 and d <= pref:
            best = d
    assert best is not None, dim
    return best


_DIMS = {"nn": (((1,), (0,)), ((), ())), "nt": (((1,), (1,)), ((), ())), "tn": (((0,), (0,)), ((), ()))}


_MM_RESIDENT_BYTES = 8 * 1024 * 1024
_MM_STREAM_BYTES = 6 * 1024 * 1024
_MM_OUT_BYTES = 6 * 1024 * 1024


def _mm_tiles(m, n, k, mode, a_size, b_size, o_size):
    if mode == "tn":
        tm = _tile(m, _MM_RESIDENT_BYTES // (k * a_size))
        tn = _tile(n, min(_MM_STREAM_BYTES // (k * b_size), _MM_OUT_BYTES // (tm * o_size)))
    else:
        tn = _tile(n, _MM_RESIDENT_BYTES // (k * b_size))
        tm = _tile(m, min(_MM_STREAM_BYTES // (k * a_size), _MM_OUT_BYTES // (tn * o_size), 512))
    return tm, tn


def _mm(a, b, mode="nn", out_dtype=F32, add=None):
    if mode == "nn":
        (m, k), (k2, n) = a.shape, b.shape
    elif mode == "nt":
        (m, k), (n, k2) = a.shape, b.shape
    else:
        (k, m), (k2, n) = a.shape, b.shape
    assert k == k2, (a.shape, b.shape, mode)
    tm, tn = _mm_tiles(m, n, k, mode, a.dtype.itemsize, b.dtype.itemsize, jnp.dtype(out_dtype).itemsize)
    dims = _DIMS[mode]
    has_add = add is not None

    def body(*refs):
        if has_add:
            a_ref, b_ref, add_ref, o_ref = refs
        else:
            a_ref, b_ref, o_ref = refs
        r = lax.dot_general(a_ref[...].astype(BF16), b_ref[...].astype(BF16), dims, preferred_element_type=F32)
        if has_add:
            r = r + add_ref[...]
        o_ref[...] = r.astype(out_dtype)

    a_spec = pl.BlockSpec((k, tm), lambda i, j: (0, i)) if mode == "tn" else pl.BlockSpec((tm, k), lambda i, j: (i, 0))
    b_spec = pl.BlockSpec((tn, k), lambda i, j: (j, 0)) if mode == "nt" else pl.BlockSpec((k, tn), lambda i, j: (0, j))
    o_spec = pl.BlockSpec((tm, tn), lambda i, j: (i, j))
    in_specs = [a_spec, b_spec] + ([o_spec] if has_add else [])
    args = (a, b) + ((add,) if has_add else ())
    return pl.pallas_call(
        body,
        out_shape=jax.ShapeDtypeStruct((m, n), out_dtype),
        grid=(m // tm, n // tn),
        in_specs=in_specs,
        out_specs=o_spec,
        compiler_params=_params(("parallel", "parallel")),
        name=f"mm_{mode}_{m}x{k}x{n}",
    )(*args)


def _row(width):
    return pl.BlockSpec((ROW_TILE, width), lambda i: (i, 0))


def _vec(width):
    return pl.BlockSpec((1, width), lambda i: (0, 0))


def _rms_fwd(x, g):
    rows, d = x.shape

    def body(x_ref, g_ref, o_ref):
        xv = x_ref[...]
        rstd = lax.rsqrt(jnp.mean(xv * xv, axis=-1, keepdims=True) + RMS_EPS)
        o_ref[...] = (xv * rstd * g_ref[...]).astype(BF16)

    return pl.pallas_call(
        body, out_shape=jax.ShapeDtypeStruct((rows, d), BF16), grid=(rows // ROW_TILE,),
        in_specs=[_row(d), _vec(d)], out_specs=_row(d), compiler_params=_params(("parallel",)), name=f"rms_fwd_{rows}",
    )(x, g)


def _accumulate(ref, value, step):
    @pl.when(step == 0)
    def _():
        ref[...] = value

    @pl.when(step > 0)
    def _():
        ref[...] += value


def _rms_bwd(x, g, dh, dres):
    rows, d = x.shape
    has_res = dres is not None

    def body(*refs):
        if has_res:
            x_ref, g_ref, dh_ref, dres_ref, dx_ref, dxb_ref, dg_ref = refs
        else:
            x_ref, g_ref, dh_ref, dx_ref, dxb_ref, dg_ref = refs
        xv = x_ref[...]
        dhv = dh_ref[...]
        rstd = lax.rsqrt(jnp.mean(xv * xv, axis=-1, keepdims=True) + RMS_EPS)
        xhat = xv * rstd
        gd = dhv * g_ref[...]
        dx = rstd * (gd - xhat * jnp.mean(gd * xhat, axis=-1, keepdims=True))
        if has_res:
            dx = dx + dres_ref[...]
        dx_ref[...] = dx
        dxb_ref[...] = dx.astype(BF16)
        _accumulate(dg_ref, jnp.sum(dhv * xhat, axis=0, keepdims=True), pl.program_id(0))

    in_specs = [_row(d), _vec(d), _row(d)] + ([_row(d)] if has_res else [])
    args = (x, g, dh) + ((dres,) if has_res else ())
    return pl.pallas_call(
        body,
        out_shape=(jax.ShapeDtypeStruct((rows, d), F32), jax.ShapeDtypeStruct((rows, d), BF16), jax.ShapeDtypeStruct((1, d), F32)),
        grid=(rows // ROW_TILE,), in_specs=in_specs, out_specs=(_row(d), _row(d), _vec(d)),
        compiler_params=_params(("arbitrary",)), name=f"rms_bwd_{rows}_{int(has_res)}",
    )(*args)


def _loss_head(x, g, target):
    rows, d = x.shape

    def body(x_ref, g_ref, t_ref, loss_ref, dx_ref, dxb_ref, dg_ref):
        xv = x_ref[...]
        gv = g_ref[...]
        rstd = lax.rsqrt(jnp.mean(xv * xv, axis=-1, keepdims=True) + RMS_EPS)
        xhat = xv * rstd
        err = xhat * gv - t_ref[...]
        part = 0.5 * jnp.sum(jnp.mean(err * err, axis=-1, keepdims=True), axis=0, keepdims=True)
        dy = err * (1.0 / d)
        gd = dy * gv
        dx = rstd * (gd - xhat * jnp.mean(gd * xhat, axis=-1, keepdims=True))
        dx_ref[...] = dx
        dxb_ref[...] = dx.astype(BF16)
        step = pl.program_id(0)
        _accumulate(loss_ref, jnp.broadcast_to(part, (1, LANE)), step)
        _accumulate(dg_ref, jnp.sum(dy * xhat, axis=0, keepdims=True), step)

    return pl.pallas_call(
        body,
        out_shape=(jax.ShapeDtypeStruct((1, LANE), F32), jax.ShapeDtypeStruct((rows, d), F32),
                   jax.ShapeDtypeStruct((rows, d), BF16), jax.ShapeDtypeStruct((1, d), F32)),
        grid=(rows // ROW_TILE,), in_specs=[_row(d), _vec(d), _row(d)],
        out_specs=(_vec(LANE), _row(d), _row(d), _vec(d)),
        compiler_params=_params(("arbitrary",)), name="loss_head",
    )(x, g, target)


def _sigmoid(x):
    return 1.0 / (1.0 + jnp.exp(-x))


def _ln_silu_fwd(c, ln_g, ln_b):
    rows, ch = c.shape

    def body(c_ref, g_ref, b_ref, o_ref):
        cv = c_ref[...]
        mu = jnp.mean(cv, axis=-1, keepdims=True)
        cen = cv - mu
        rstd = lax.rsqrt(jnp.mean(cen * cen, axis=-1, keepdims=True) + LN_EPS)
        y = cen * rstd * g_ref[...] + b_ref[...]
        o_ref[...] = (y * _sigmoid(y)).astype(BF16)

    return pl.pallas_call(
        body, out_shape=jax.ShapeDtypeStruct((rows, ch), BF16), grid=(rows // ROW_TILE,),
        in_specs=[_row(ch), _vec(ch), _vec(ch)], out_specs=_row(ch), compiler_params=_params(("parallel",)), name="ln_silu_fwd",
    )(c, ln_g, ln_b)


def _ln_silu_bwd(dmix, c, ln_g, ln_b):
    rows, ch = c.shape

    def body(dm_ref, c_ref, g_ref, b_ref, dc_ref, dg_ref, db_ref):
        cv = c_ref[...]
        gv = g_ref[...]
        mu = jnp.mean(cv, axis=-1, keepdims=True)
        cen = cv - mu
        rstd = lax.rsqrt(jnp.mean(cen * cen, axis=-1, keepdims=True) + LN_EPS)
        xhat = cen * rstd
        y = xhat * gv + b_ref[...]
        sg = _sigmoid(y)
        dy = dm_ref[...] * (sg * (1.0 + y * (1.0 - sg)))
        dxh = dy * gv
        dc = rstd * (dxh - jnp.mean(dxh, axis=-1, keepdims=True) - xhat * jnp.mean(dxh * xhat, axis=-1, keepdims=True))
        dc_ref[...] = dc
        step = pl.program_id(0)
        _accumulate(dg_ref, jnp.sum(dy * xhat, axis=0, keepdims=True), step)
        _accumulate(db_ref, jnp.sum(dy, axis=0, keepdims=True), step)

    return pl.pallas_call(
        body,
        out_shape=(jax.ShapeDtypeStruct((rows, ch), F32), jax.ShapeDtypeStruct((1, ch), F32), jax.ShapeDtypeStruct((1, ch), F32)),
        grid=(rows // ROW_TILE,), in_specs=[_row(ch), _row(ch), _vec(ch), _vec(ch)],
        out_specs=(_row(ch), _vec(ch), _vec(ch)), compiler_params=_params(("arbitrary",)), name="ln_silu_bwd",
    )(dmix, c, ln_g, ln_b)


def _col(rows, cb):
    return pl.BlockSpec((rows, cb), lambda j: (0, j))


def _conv_glu_fwd(pa, pg, ba, bg, w_dw, b_dw):
    seq, ch = pa.shape
    cb = LANE
    n_chunks = seq // CHUNK

    def body(pa_ref, pg_ref, ba_ref, bg_ref, w_ref, b_ref, v_ref, c_ref, vpad_ref):
        vpad_ref[0:CONV_PAD, :] = jnp.zeros((CONV_PAD, cb), F32)

        def glu(r, carry):
            r0 = pl.multiple_of(r * CHUNK, CHUNK)
            rows = pl.ds(r0, CHUNK)
            v = (pa_ref[rows, :] + ba_ref[...]) * _sigmoid(pg_ref[rows, :] + bg_ref[...])
            v_ref[rows, :] = v
            vpad_ref[pl.ds(r0 + CONV_PAD, CHUNK), :] = v
            return carry

        lax.fori_loop(0, n_chunks, glu, 0)

        def conv(r, carry):
            r0 = pl.multiple_of(r * CHUNK, CHUNK)
            win = vpad_ref[pl.ds(r0, CHUNK + CONV_PAD), :]
            acc = jnp.broadcast_to(b_ref[...], (CHUNK, cb))
            for t in range(CONV_W):
                off = CONV_PAD - (CONV_W - 1) + t
                acc = acc + w_ref[t:t + 1, :] * win[off:off + CHUNK, :]
            c_ref[pl.ds(r0, CHUNK), :] = acc
            return carry

        lax.fori_loop(0, n_chunks, conv, 0)

    return pl.pallas_call(
        body,
        out_shape=(jax.ShapeDtypeStruct((seq, ch), F32), jax.ShapeDtypeStruct((seq, ch), F32)),
        grid=(ch // cb,),
        in_specs=[_col(seq, cb), _col(seq, cb), _col(1, cb), _col(1, cb), _col(CONV_PAD, cb), _col(1, cb)],
        out_specs=(_col(seq, cb), _col(seq, cb)),
        scratch_shapes=[pltpu.VMEM((seq + CONV_PAD, cb), F32)],
        compiler_params=_params(("parallel",)), name="conv_glu_fwd",
    )(pa, pg, ba, bg, w_dw, b_dw)


def _conv_glu_bwd(dc, v, pa, pg, ba, bg, w_dw):
    seq, ch = dc.shape
    cb = LANE
    n_chunks = seq // CHUNK

    def body(dc_ref, v_ref, pa_ref, pg_ref, ba_ref, bg_ref, w_ref, da_ref, dg_ref, dw_ref, dbdw_ref, dba_ref, dbg_ref,
             dcpad_ref, vpad_ref):
        vpad_ref[0:CONV_PAD, :] = jnp.zeros((CONV_PAD, cb), F32)
        dcpad_ref[seq:seq + CONV_PAD, :] = jnp.zeros((CONV_PAD, cb), F32)
        dw_ref[...] = jnp.zeros((CONV_PAD, cb), F32)
        dbdw_ref[...] = jnp.zeros((1, cb), F32)
        dba_ref[...] = jnp.zeros((1, cb), F32)
        dbg_ref[...] = jnp.zeros((1, cb), F32)

        def fill(r, carry):
            r0 = pl.multiple_of(r * CHUNK, CHUNK)
            vpad_ref[pl.ds(r0 + CONV_PAD, CHUNK), :] = v_ref[pl.ds(r0, CHUNK), :]
            dcpad_ref[pl.ds(r0, CHUNK), :] = dc_ref[pl.ds(r0, CHUNK), :]
            return carry

        lax.fori_loop(0, n_chunks, fill, 0)

        def step(r, carry):
            r0 = pl.multiple_of(r * CHUNK, CHUNK)
            rows = pl.ds(r0, CHUNK)
            dwin = dcpad_ref[pl.ds(r0, CHUNK + CONV_PAD), :]
            vwin = vpad_ref[pl.ds(r0, CHUNK + CONV_PAD), :]
            dcur = dwin[0:CHUNK, :]
            dv = jnp.zeros((CHUNK, cb), F32)
            for t in range(CONV_W):
                fwd_off = CONV_W - 1 - t
                dv = dv + w_ref[t:t + 1, :] * dwin[fwd_off:fwd_off + CHUNK, :]
                off = CONV_PAD - (CONV_W - 1) + t
                dw_ref[t:t + 1, :] += jnp.sum(dcur * vwin[off:off + CHUNK, :], axis=0, keepdims=True)
            a = pa_ref[rows, :] + ba_ref[...]
            sg = _sigmoid(pg_ref[rows, :] + bg_ref[...])
            da = dv * sg
            dgate = dv * a * sg * (1.0 - sg)
            da_ref[rows, :] = da.astype(BF16)
            dg_ref[rows, :] = dgate.astype(BF16)
            dbdw_ref[...] += jnp.sum(dcur, axis=0, keepdims=True)
            dba_ref[...] += jnp.sum(da, axis=0, keepdims=True)
            dbg_ref[...] += jnp.sum(dgate, axis=0, keepdims=True)
            return carry

        lax.fori_loop(0, n_chunks, step, 0)

    return pl.pallas_call(
        body,
        out_shape=(jax.ShapeDtypeStruct((seq, ch), BF16), jax.ShapeDtypeStruct((seq, ch), BF16),
                   jax.ShapeDtypeStruct((CONV_PAD, ch), F32), jax.ShapeDtypeStruct((1, ch), F32),
                   jax.ShapeDtypeStruct((1, ch), F32), jax.ShapeDtypeStruct((1, ch), F32)),
        grid=(ch // cb,),
        in_specs=[_col(seq, cb), _col(seq, cb), _col(seq, cb), _col(seq, cb), _col(1, cb), _col(1, cb), _col(CONV_PAD, cb)],
        out_specs=(_col(seq, cb), _col(seq, cb), _col(CONV_PAD, cb), _col(1, cb), _col(1, cb), _col(1, cb)),
        scratch_shapes=[pltpu.VMEM((seq + CONV_PAD, cb), F32), pltpu.VMEM((seq + CONV_PAD, cb), F32)],
        compiler_params=_params(("parallel",)), name="conv_glu_bwd",
    )(dc, v, pa, pg, ba, bg, w_dw)


def _ffn_conv(pad_ref, r0, w_ref, b_ref, cb):
    win = pad_ref[pl.ds(r0, CHUNK + FFN_PAD), :]
    y = jnp.broadcast_to(b_ref[...], (CHUNK, cb))
    for t in range(FFN_CONV_W):
        off = FFN_PAD - (FFN_CONV_W - 1) + t
        y = y + w_ref[t:t + 1, :] * win[off:off + CHUNK, :]
    return y, win


def _ffn_mid_fwd(ug, uv, wg, wv, bg, bv):
    seq, ch = ug.shape
    cb = _tile(ch, 256)
    n_chunks = seq // CHUNK

    def body(ug_ref, uv_ref, wg_ref, wv_ref, bg_ref, bv_ref, z_ref, gpad_ref, vpad_ref):
        gpad_ref[0:FFN_PAD, :] = jnp.zeros((FFN_PAD, cb), F32)
        vpad_ref[0:FFN_PAD, :] = jnp.zeros((FFN_PAD, cb), F32)

        def fill(r, carry):
            r0 = pl.multiple_of(r * CHUNK, CHUNK)
            gpad_ref[pl.ds(r0 + FFN_PAD, CHUNK), :] = ug_ref[pl.ds(r0, CHUNK), :]
            vpad_ref[pl.ds(r0 + FFN_PAD, CHUNK), :] = uv_ref[pl.ds(r0, CHUNK), :]
            return carry

        lax.fori_loop(0, n_chunks, fill, 0)

        def step(r, carry):
            r0 = pl.multiple_of(r * CHUNK, CHUNK)
            yg, _ = _ffn_conv(gpad_ref, r0, wg_ref, bg_ref, cb)
            yv, _ = _ffn_conv(vpad_ref, r0, wv_ref, bv_ref, cb)
            z_ref[pl.ds(r0, CHUNK), :] = (yg * _sigmoid(yg) * yv).astype(BF16)
            return carry

        lax.fori_loop(0, n_chunks, step, 0)

    return pl.pallas_call(
        body, out_shape=jax.ShapeDtypeStruct((seq, ch), BF16), grid=(ch // cb,),
        in_specs=[_col(seq, cb), _col(seq, cb), _col(FFN_PAD, cb), _col(FFN_PAD, cb), _col(1, cb), _col(1, cb)],
        out_specs=_col(seq, cb),
        scratch_shapes=[pltpu.VMEM((seq + FFN_PAD, cb), F32), pltpu.VMEM((seq + FFN_PAD, cb), F32)],
        compiler_params=_params(("parallel",)), name="ffn_mid_fwd",
    )(ug, uv, wg, wv, bg, bv)


def _ffn_mid_bwd(ug, uv, dz, wg, wv, bg, bv):
    seq, ch = ug.shape
    cb = _tile(ch, 256)
    n_chunks = seq // CHUNK

    def body(ug_ref, uv_ref, dz_ref, wg_ref, wv_ref, bg_ref, bv_ref, dug_ref, duv_ref, dwg_ref, dwv_ref, dbg_ref, dbv_ref,
             gpad_ref, vpad_ref, dyg_ref, dyv_ref):
        gpad_ref[0:FFN_PAD, :] = jnp.zeros((FFN_PAD, cb), F32)
        vpad_ref[0:FFN_PAD, :] = jnp.zeros((FFN_PAD, cb), F32)
        dyg_ref[seq:seq + FFN_PAD, :] = jnp.zeros((FFN_PAD, cb), F32)
        dyv_ref[seq:seq + FFN_PAD, :] = jnp.zeros((FFN_PAD, cb), F32)
        dwg_ref[...] = jnp.zeros((FFN_PAD, cb), F32)
        dwv_ref[...] = jnp.zeros((FFN_PAD, cb), F32)
        dbg_ref[...] = jnp.zeros((1, cb), F32)
        dbv_ref[...] = jnp.zeros((1, cb), F32)

        def fill(r, carry):
            r0 = pl.multiple_of(r * CHUNK, CHUNK)
            gpad_ref[pl.ds(r0 + FFN_PAD, CHUNK), :] = ug_ref[pl.ds(r0, CHUNK), :]
            vpad_ref[pl.ds(r0 + FFN_PAD, CHUNK), :] = uv_ref[pl.ds(r0, CHUNK), :]
            return carry

        lax.fori_loop(0, n_chunks, fill, 0)

        def grads_of_conv_out(r, carry):
            r0 = pl.multiple_of(r * CHUNK, CHUNK)
            rows = pl.ds(r0, CHUNK)
            yg, gwin = _ffn_conv(gpad_ref, r0, wg_ref, bg_ref, cb)
            yv, vwin = _ffn_conv(vpad_ref, r0, wv_ref, bv_ref, cb)
            dzv = dz_ref[rows, :]
            sg = _sigmoid(yg)
            dyg = dzv * yv * (sg * (1.0 + yg * (1.0 - sg)))
            dyv = dzv * yg * sg
            dyg_ref[rows, :] = dyg
            dyv_ref[rows, :] = dyv
            for t in range(FFN_CONV_W):
                off = FFN_PAD - (FFN_CONV_W - 1) + t
                dwg_ref[t:t + 1, :] += jnp.sum(dyg * gwin[off:off + CHUNK, :], axis=0, keepdims=True)
                dwv_ref[t:t + 1, :] += jnp.sum(dyv * vwin[off:off + CHUNK, :], axis=0, keepdims=True)
            dbg_ref[...] += jnp.sum(dyg, axis=0, keepdims=True)
            dbv_ref[...] += jnp.sum(dyv, axis=0, keepdims=True)
            return carry

        lax.fori_loop(0, n_chunks, grads_of_conv_out, 0)

        def grads_of_conv_in(r, carry):
            r0 = pl.multiple_of(r * CHUNK, CHUNK)
            gwin = dyg_ref[pl.ds(r0, CHUNK + FFN_PAD), :]
            vwin = dyv_ref[pl.ds(r0, CHUNK + FFN_PAD), :]
            dug = jnp.zeros((CHUNK, cb), F32)
            duv = jnp.zeros((CHUNK, cb), F32)
            for t in range(FFN_CONV_W):
                off = FFN_CONV_W - 1 - t
                dug = dug + wg_ref[t:t + 1, :] * gwin[off:off + CHUNK, :]
                duv = duv + wv_ref[t:t + 1, :] * vwin[off:off + CHUNK, :]
            dug_ref[pl.ds(r0, CHUNK), :] = dug.astype(BF16)
            duv_ref[pl.ds(r0, CHUNK), :] = duv.astype(BF16)
            return carry

        lax.fori_loop(0, n_chunks, grads_of_conv_in, 0)

    return pl.pallas_call(
        body,
        out_shape=(jax.ShapeDtypeStruct((seq, ch), BF16), jax.ShapeDtypeStruct((seq, ch), BF16),
                   jax.ShapeDtypeStruct((FFN_PAD, ch), F32), jax.ShapeDtypeStruct((FFN_PAD, ch), F32),
                   jax.ShapeDtypeStruct((1, ch), F32), jax.ShapeDtypeStruct((1, ch), F32)),
        grid=(ch // cb,),
        in_specs=[_col(seq, cb), _col(seq, cb), _col(seq, cb), _col(FFN_PAD, cb), _col(FFN_PAD, cb), _col(1, cb), _col(1, cb)],
        out_specs=(_col(seq, cb), _col(seq, cb), _col(FFN_PAD, cb), _col(FFN_PAD, cb), _col(1, cb), _col(1, cb)),
        scratch_shapes=[pltpu.VMEM((seq + FFN_PAD, cb), F32) for _ in range(4)],
        compiler_params=_params(("parallel",)), name="ffn_mid_bwd",
    )(ug, uv, dz, wg, wv, bg, bv)


def _dot(a, b, mode):
    return lax.dot_general(a.astype(BF16), b.astype(BF16), _DIMS[mode], preferred_element_type=F32)


def _head(h):
    return slice(h * HEAD_DIM, (h + 1) * HEAD_DIM)


def _mem_softmax(q, k):
    s = _dot(q, k, "nt") * ATT_SCALE
    e = jnp.exp(s - jnp.max(s, axis=-1, keepdims=True))
    return e / jnp.sum(e, axis=-1, keepdims=True)


def _memattn_fwd(pq, mk, mv):
    seq, w = pq.shape
    m = mk.shape[0]

    def body(q_ref, k_ref, v_ref, o_ref):
        for h in range(N_MEM_HEADS):
            p = _mem_softmax(q_ref[:, _head(h)], k_ref[:, _head(h)])
            o_ref[:, _head(h)] = _dot(p, v_ref[:, _head(h)], "nn").astype(BF16)

    full = pl.BlockSpec((m, w), lambda i: (0, 0))
    return pl.pallas_call(
        body, out_shape=jax.ShapeDtypeStruct((seq, w), BF16), grid=(seq // ROW_TILE,),
        in_specs=[_row(w), full, full], out_specs=_row(w), compiler_params=_params(("parallel",)), name="memattn_fwd",
    )(pq, mk, mv)


def _memattn_bwd(pq, mk, mv, dmo):
    seq, w = pq.shape
    m = mk.shape[0]

    def body(q_ref, k_ref, v_ref, do_ref, dq_ref, dk_ref, dv_ref):
        step = pl.program_id(0)

        @pl.when(step == 0)
        def _():
            dk_ref[...] = jnp.zeros((m, w), F32)
            dv_ref[...] = jnp.zeros((m, w), F32)

        for h in range(N_MEM_HEADS):
            q, k, v, do = q_ref[:, _head(h)], k_ref[:, _head(h)], v_ref[:, _head(h)], do_ref[:, _head(h)]
            p = _mem_softmax(q, k)
            dp = _dot(do, v, "nt")
            ds = p * (dp - jnp.sum(dp * p, axis=-1, keepdims=True))
            dq_ref[:, _head(h)] = (_dot(ds, k, "nn") * ATT_SCALE).astype(BF16)
            dk_ref[:, _head(h)] += _dot(ds, q, "tn") * ATT_SCALE
            dv_ref[:, _head(h)] += _dot(p, do, "tn")

    full = pl.BlockSpec((m, w), lambda i: (0, 0))
    return pl.pallas_call(
        body,
        out_shape=(jax.ShapeDtypeStruct((seq, w), BF16), jax.ShapeDtypeStruct((m, w), F32), jax.ShapeDtypeStruct((m, w), F32)),
        grid=(seq // ROW_TILE,), in_specs=[_row(w), full, full, _row(w)], out_specs=(_row(w), full, full),
        compiler_params=_params(("arbitrary",)), name="memattn_bwd",
    )(pq, mk, mv, dmo)


FWD_KEY_TILE = 512
BWD_KEY_TILE = 512


def _causal_t(s_t, q_blk, k_blk, key_tile):
    kpos = k_blk * key_tile + lax.broadcasted_iota(jnp.int32, (key_tile, ROW_TILE), 0)
    qpos = q_blk * ROW_TILE + lax.broadcasted_iota(jnp.int32, (key_tile, ROW_TILE), 1)
    return jnp.where(kpos <= qpos, s_t, NEG_BIG)


def _fox_fwd(q, k, v, cum_cols, cum_rows):
    seq, w = q.shape
    nq = seq // ROW_TILE

    def body(q_ref, k_ref, v_ref, cc_ref, cr_ref, o_ref, ob_ref, lse_ref):
        i = pl.program_id(1)
        qs = [q_ref[:, _head(h)].astype(BF16) for h in range(2)]
        crs = [cr_ref[0, 0, h:h + 1, :] for h in range(2)]

        def step(j, carry):
            rows = pl.ds(pl.multiple_of(j * FWD_KEY_TILE, FWD_KEY_TILE), FWD_KEY_TILE)
            new = []
            for h in range(2):
                m_run, l_run, acc = carry[h]
                s_t = _dot(k_ref[rows, _head(h)], qs[h], "nt") * ATT_SCALE + crs[h] - cc_ref[0, rows, h * HEAD_DIM:h * HEAD_DIM + 1]
                s_t = _causal_t(s_t, i, j, FWD_KEY_TILE)
                m_new = jnp.maximum(m_run, jnp.max(s_t, axis=0, keepdims=True))
                alpha = jnp.exp(m_run - m_new)
                p_t = jnp.exp(s_t - m_new)
                l_new = alpha * l_run + jnp.sum(p_t, axis=0, keepdims=True)
                new.append((m_new, l_new, alpha * acc + _dot(v_ref[rows, _head(h)], p_t, "tn")))
            return tuple(new)

        one_head = (jnp.full((1, ROW_TILE), NEG_BIG, F32), jnp.zeros((1, ROW_TILE), F32), jnp.zeros((HEAD_DIM, ROW_TILE), F32))
        res = lax.fori_loop(0, ((i + 1) * ROW_TILE + FWD_KEY_TILE - 1) // FWD_KEY_TILE, step, (one_head, one_head))
        o = jnp.concatenate([acc / l_run for _, l_run, acc in res], axis=0).T
        o_ref[...] = o
        ob_ref[...] = o.astype(BF16)
        lse_ref[...] = jnp.zeros((1, 1, 8, ROW_TILE), F32)
        for h in range(2):
            lse_ref[0, 0, h:h + 1, :] = res[h][0] + jnp.log(res[h][1])

    blk = pl.BlockSpec((ROW_TILE, LANE), lambda hp, i: (i, hp))
    full = pl.BlockSpec((seq, LANE), lambda hp, i: (0, hp))
    cols = pl.BlockSpec((1, seq, LANE), lambda hp, i: (hp, 0, 0))
    rows = pl.BlockSpec((1, 1, 8, ROW_TILE), lambda hp, i: (hp, i, 0, 0))
    return pl.pallas_call(
        body,
        out_shape=(jax.ShapeDtypeStruct((seq, w), F32), jax.ShapeDtypeStruct((seq, w), BF16),
                   jax.ShapeDtypeStruct((N_HEAD_PAIRS, nq, 8, ROW_TILE), F32)),
        grid=(N_HEAD_PAIRS, nq), in_specs=[blk, full, full, cols, rows], out_specs=(blk, blk, rows),
        compiler_params=_params(("parallel", "parallel")), name="fox_fwd",
    )(q, k, v, cum_cols, cum_rows)


def _fox_bwd(q, k, v, cum_cols, cum_rows, o, lse, do):
    seq, w = q.shape
    nq = seq // ROW_TILE
    nk = seq // BWD_KEY_TILE

    def body(q_ref, k_ref, v_ref, cc_ref, cr_ref, o_ref, lse_ref, do_ref, dq_ref, dk_ref, dv_ref, dcc_ref, dcr_ref):
        kj = pl.program_id(1)

        @pl.when(kj == 0)
        def _():
            dq_ref[...] = jnp.zeros((seq, LANE), F32)
            dcr_ref[...] = jnp.zeros((1, nq, 8, ROW_TILE), F32)

        ks = [k_ref[:, _head(h)].astype(BF16) for h in range(2)]
        vs = [v_ref[:, _head(h)].astype(BF16) for h in range(2)]
        ccs = [cc_ref[0, :, h * HEAD_DIM:h * HEAD_DIM + 1] for h in range(2)]
        ones = jnp.ones((8, HEAD_DIM), BF16)

        def step(i, carry):
            rows = pl.ds(pl.multiple_of(i * ROW_TILE, ROW_TILE), ROW_TILE)
            new = []
            for h in range(2):
                dk, dv, fold = carry[h]
                qh = q_ref[rows, _head(h)].astype(BF16)
                doh = do_ref[rows, _head(h)]
                s_t = _dot(ks[h], qh, "nt") * ATT_SCALE + cr_ref[0, i, h:h + 1, :] - ccs[h]
                p_t = jnp.exp(_causal_t(s_t, i, kj, BWD_KEY_TILE) - lse_ref[0, i, h:h + 1, :])
                dp_t = _dot(vs[h], doh, "nt")
                hi, mid, lo = _split3(doh * o_ref[rows, _head(h)])
                row_sum = lambda part: lax.dot_general(ones, part, _DIMS["nt"], preferred_element_type=F32)
                delta = ((row_sum(lo) + row_sum(mid)) + row_sum(hi))[0:1, :]
                ds_t = p_t * (dp_t - delta)
                dq_ref[rows, _head(h)] += _dot(ds_t, ks[h], "tn") * ATT_SCALE
                dcr_ref[0, i, h:h + 1, :] += jnp.sum(ds_t, axis=0, keepdims=True)
                new.append((dk + _dot(ds_t, qh, "nn") * ATT_SCALE, dv + _dot(p_t, doh, "nn"), fold + (ds_t[:, :LANE] + ds_t[:, LANE:])))
            return tuple(new)

        one_head = (jnp.zeros((BWD_KEY_TILE, HEAD_DIM), F32), jnp.zeros((BWD_KEY_TILE, HEAD_DIM), F32), jnp.zeros((BWD_KEY_TILE, LANE), F32))
        res = lax.fori_loop(kj * BWD_KEY_TILE // ROW_TILE, nq, step, (one_head, one_head))
        for h in range(2):
            dk_ref[:, _head(h)] = res[h][0]
            dv_ref[:, _head(h)] = res[h][1]
            dcc_ref[0, :, _head(h)] = jnp.broadcast_to(-jnp.sum(res[h][2], axis=-1, keepdims=True), (BWD_KEY_TILE, HEAD_DIM))

    blk = pl.BlockSpec((BWD_KEY_TILE, LANE), lambda hp, j: (j, hp))
    full = pl.BlockSpec((seq, LANE), lambda hp, j: (0, hp))
    cols = pl.BlockSpec((1, BWD_KEY_TILE, LANE), lambda hp, j: (hp, j, 0))
    rows = pl.BlockSpec((1, nq, 8, ROW_TILE), lambda hp, j: (hp, 0, 0, 0))
    return pl.pallas_call(
        body,
        out_shape=(jax.ShapeDtypeStruct((seq, w), F32), jax.ShapeDtypeStruct((seq, w), F32), jax.ShapeDtypeStruct((seq, w), F32),
                   jax.ShapeDtypeStruct((N_HEAD_PAIRS, seq, LANE), F32), jax.ShapeDtypeStruct((N_HEAD_PAIRS, nq, 8, ROW_TILE), F32)),
        grid=(N_HEAD_PAIRS, nk), in_specs=[full, blk, blk, cols, rows, full, rows, full],
        out_specs=(full, blk, blk, cols, rows),
        compiler_params=_params(("parallel", "arbitrary")), name="fox_bwd",
    )(q, k, v, cum_cols, cum_rows, o, lse, do)


def _split3(x):
    hi = x.astype(BF16)
    r1 = x - hi.astype(F32)
    mid = r1.astype(BF16)
    lo = (r1 - mid.astype(F32)).astype(BF16)
    return hi, mid, lo


def _tri_sum(tri, x):
    hi, mid, lo = _split3(x)
    dot = lambda p: lax.dot_general(tri, p, _DIMS["nn"], preferred_element_type=F32)
    return (dot(lo) + dot(mid)) + dot(hi)


def _tri(lower):
    r = lax.broadcasted_iota(jnp.int32, (LANE, LANE), 0)
    c = lax.broadcasted_iota(jnp.int32, (LANE, LANE), 1)
    return jnp.where((c <= r) if lower else (c >= r), 1.0, 0.0).astype(BF16)


def _fgate_fwd(fl, bf):
    seq, w = fl.shape
    nb = seq // LANE

    def body(f_ref, b_ref, cum_ref):
        tri = _tri(True)

        def step(i, carry):
            rows = pl.ds(pl.multiple_of(i * LANE, LANE), LANE)
            x = f_ref[rows, :] + b_ref[...]
            logsig = jnp.minimum(x, 0.0) - jnp.log(1.0 + jnp.exp(-jnp.abs(x)))
            cum = _tri_sum(tri, logsig) + carry
            cum_ref[rows, :] = cum
            return cum[LANE - 1:LANE, :]

        lax.fori_loop(0, nb, step, jnp.zeros((1, w), F32))

    return pl.pallas_call(body, out_shape=jax.ShapeDtypeStruct((seq, w), F32), compiler_params=_params(), name="fgate_fwd")(fl, bf)


def _fgate_bwd(fl, bf, dcum):
    seq, w = fl.shape
    nb = seq // LANE

    def body(f_ref, b_ref, dc_ref, df_ref, db_ref):
        tri = _tri(False)

        def step(i, carry):
            tail, db = carry
            rows = pl.ds(pl.multiple_of((nb - 1 - i) * LANE, LANE), LANE)
            suffix = _tri_sum(tri, dc_ref[rows, :]) + tail
            df = suffix * (1.0 - _sigmoid(f_ref[rows, :] + b_ref[...]))
            df_ref[rows, :] = df
            return suffix[0:1, :], db + jnp.sum(df, axis=0, keepdims=True)

        _, db = lax.fori_loop(0, nb, step, (jnp.zeros((1, w), F32), jnp.zeros((1, w), F32)))
        db_ref[...] = db

    return pl.pallas_call(
        body, out_shape=(jax.ShapeDtypeStruct((seq, w), F32), jax.ShapeDtypeStruct((1, w), F32)),
        compiler_params=_params(), name="fgate_bwd",
    )(fl, bf, dcum)


def _cum_layouts(cum):
    seq = cum.shape[0]
    nblk = seq // ROW_TILE
    heads = cum[:, :N_FOX_HEADS]
    cq = jnp.repeat(heads, HEAD_DIM, axis=1).reshape(seq, N_HEAD_PAIRS, LANE).transpose(1, 0, 2)
    ck = heads.T.reshape(N_HEAD_PAIRS, 2, nblk, ROW_TILE).transpose(0, 2, 1, 3)
    ck = jnp.pad(ck, ((0, 0), (0, 0), (0, 6), (0, 0)))
    return cq, ck


def _dcum_from_layouts(dcq, dck):
    seq = dcq.shape[1]
    key_side = dck[:, :, :2, :].transpose(0, 2, 1, 3).reshape(N_FOX_HEADS, seq).T
    query_side = dcq[:, :, ::HEAD_DIM].transpose(1, 0, 2).reshape(seq, N_FOX_HEADS)
    return jnp.pad(key_side + query_side, ((0, 0), (0, F_PAD - N_FOX_HEADS)))


def _local_step(x, mem, target, w, arrive, emit):
    vec = lambda a: a.reshape(1, -1)
    mem_n = _rms_fwd(mem, vec(w["g_mem"]))
    saved = []
    shared = None
    for l in range(N_LAYERS):
        arrive(l, x)
        s = {"x_in": x}
        s["h"] = h = _rms_fwd(x, vec(w["g_mix"][l]))
        s["mk"] = mk = _mm(mem_n, w["w_mk"][l])
        s["mv"] = mv = _mm(mem_n, w["w_mv"][l])
        if l < N_A:
            s["pa"] = pa = _mm(h, w["w_a"][l])
            s["pg"] = pg = _mm(h, w["w_g"][l])
            s["pq"] = pq = _mm(h, w["w_qm"][l])
            s["v"], s["c"] = _conv_glu_fwd(pa, pg, vec(w["b_glu_a"][l]), vec(w["b_glu_g"][l]), w["w_dw_a"][l], vec(w["b_dw_a"][l]))
            s["mix"] = mix = _ln_silu_fwd(s["c"], vec(w["ln_g"][l]), vec(w["ln_b"][l]))
        else:
            if l == N_A:
                shared = {"x_in": x}
                shared["hk"] = hk = _rms_fwd(x, vec(w["g_kv"]))
                shared["k"] = _mm(hk, w["w_k"])
                shared["v"] = _mm(hk, w["w_v"])
                shared["fl"] = _mm(hk, w["w_f"])
                cum = _fgate_fwd(shared["fl"], w["b_f"])
                shared["cq"], shared["ck"] = _cum_layouts(cum)
            s["q"] = q = _mm(h, w["w_q"][l - N_A])
            s["pq"] = pq = _mm(h, w["w_qm"][l])
            s["o"], mix, s["lse"] = _fox_fwd(q, shared["k"], shared["v"], shared["cq"], shared["ck"])
            s["mix"] = mix
        s["mo"] = mo = _memattn_fwd(pq, mk, mv)
        x = _mm(mix, w["w_o_mix"][l], add=x)
        s["x_mid"] = x = _mm(mo, w["w_o_mem"][l], add=x)
        s["hf"] = hf = _rms_fwd(x, vec(w["g_ffn"][l]))
        s["ug"] = ug = _mm(hf, w["w_up_g"][l])
        s["uv"] = uv = _mm(hf, w["w_up_v"][l])
        s["z"] = z = _ffn_mid_fwd(ug, uv, w["w_dw_f_g"][l], w["w_dw_f_v"][l], vec(w["b_dw_f_g"][l]), vec(w["b_dw_f_v"][l]))
        x = _mm(z, w["w_down"][l], add=x)
        saved.append(s)

    loss_row, dx, dxb, dg_final = _loss_head(x, vec(w["g_final"]), target)
    loss = loss_row[0, 0]

    g = {n: [None] * N_LAYERS for n in ("g_mix", "w_mem_kv", "w_out", "g_ffn", "w_up", "w_dw_f", "b_dw_f", "w_down")}
    for n in ("w_in_a", "b_glu", "w_dw_a", "b_dw_a", "ln_g", "ln_b", "w_in_b"):
        g[n] = [None] * N_A
    one = lambda arr: ([arr], 0)
    g["g_final"] = [one(dg_final[0])]
    dmem_n = None
    dk_sum = dv_sum = dcq_sum = dck_sum = None
    for l in reversed(range(N_LAYERS)):
        s = saved[l]
        dz = _mm(dxb, w["w_down"][l], "nt")
        g["w_down"][l] = one(_mm(s["z"], dxb, "tn", BF16))
        dug, duv, dwg, dwv, dbg, dbv = _ffn_mid_bwd(s["ug"], s["uv"], dz, w["w_dw_f_g"][l], w["w_dw_f_v"][l],
                                                    vec(w["b_dw_f_g"][l]), vec(w["b_dw_f_v"][l]))
        g["w_up"][l] = ([_mm(s["hf"], dug, "tn", BF16), _mm(s["hf"], duv, "tn", BF16)], 1)
        g["w_dw_f"][l] = ([dwg[:FFN_CONV_W], dwv[:FFN_CONV_W]], 1)
        g["b_dw_f"][l] = ([dbg[0], dbv[0]], 0)
        dhf = _mm(duv, w["w_up_v"][l], "nt", add=_mm(dug, w["w_up_g"][l], "nt"))
        dx, dxb, dg_ffn = _rms_bwd(s["x_mid"], vec(w["g_ffn"][l]), dhf, dx)
        g["g_ffn"][l] = one(dg_ffn[0])
        dmix = _mm(dxb, w["w_o_mix"][l], "nt")
        dmo = _mm(dxb, w["w_o_mem"][l], "nt")
        g["w_out"][l] = ([_mm(s["mix"], dxb, "tn", BF16), _mm(s["mo"], dxb, "tn", BF16)], 0)
        dpq, dmk, dmv = _memattn_bwd(s["pq"], s["mk"], s["mv"], dmo)
        g["w_mem_kv"][l] = ([_mm(mem_n, dmk, "tn", BF16), _mm(mem_n, dmv, "tn", BF16)], 1)
        dmem_n = _mm(dmk, w["w_mk"][l], "nt", add=dmem_n)
        dmem_n = _mm(dmv, w["w_mv"][l], "nt", add=dmem_n)
        if l < N_A:
            dc, dlng, dlnb = _ln_silu_bwd(dmix, s["c"], vec(w["ln_g"][l]), vec(w["ln_b"][l]))
            da, dgate, dwdw, dbdw, dba, dbg2 = _conv_glu_bwd(dc, s["v"], s["pa"], s["pg"], vec(w["b_glu_a"][l]),
                                                              vec(w["b_glu_g"][l]), w["w_dw_a"][l])
            g["ln_g"][l], g["ln_b"][l], g["b_dw_a"][l] = one(dlng[0]), one(dlnb[0]), one(dbdw[0])
            g["w_dw_a"][l] = ([dwdw[:CONV_W]], 1)
            g["b_glu"][l] = ([dba[0], dbg2[0]], 0)
            g["w_in_a"][l] = ([_mm(s["h"], da, "tn", BF16), _mm(s["h"], dgate, "tn", BF16), _mm(s["h"], dpq, "tn", BF16)], 1)
            dh = _mm(da, w["w_a"][l], "nt")
            dh = _mm(dgate, w["w_g"][l], "nt", add=dh)
            dh = _mm(dpq, w["w_qm"][l], "nt", add=dh)
        else:
            dq, dk, dv, dcq, dck = _fox_bwd(s["q"], shared["k"], shared["v"], shared["cq"], shared["ck"], s["o"], s["lse"], dmix)
            dk_sum = dk if dk_sum is None else dk_sum + dk
            dv_sum = dv if dv_sum is None else dv_sum + dv
            dcq_sum = dcq if dcq_sum is None else dcq_sum + dcq
            dck_sum = dck if dck_sum is None else dck_sum + dck
            g["w_in_b"][l - N_A] = ([_mm(s["h"], dq, "tn", BF16), _mm(s["h"], dpq, "tn", BF16)], 1)
            dh = _mm(dq, w["w_q"][l - N_A], "nt")
            dh = _mm(dpq, w["w_qm"][l], "nt", add=dh)
        dx, dxb, dg_mix = _rms_bwd(s["x_in"], vec(w["g_mix"][l]), dh, dx)
        g["g_mix"][l] = one(dg_mix[0])
        if l == N_A:
            df, dbf = _fgate_bwd(shared["fl"], w["b_f"], _dcum_from_layouts(dcq_sum, dck_sum))
            hk = shared["hk"]
            g["w_kvf"] = [([_mm(hk, dk_sum, "tn", BF16), _mm(hk, dv_sum, "tn", BF16), _mm(hk, df, "tn", BF16)[:, :N_FOX_HEADS]], 1)]
            g["b_f"] = [one(dbf[0, :N_FOX_HEADS])]
            dhk = _mm(dk_sum, w["w_k"], "nt")
            dhk = _mm(dv_sum, w["w_v"], "nt", add=dhk)
            dhk = _mm(df, w["w_f"], "nt", add=dhk)
            dx, dxb, dg_kv = _rms_bwd(shared["x_in"], vec(w["g_kv"]), dhk, dx)
            g["g_kv"] = [one(dg_kv[0])]
        started = emit(l, g)
        if started is not None:
            dxb = dxb + started[0, 0].astype(BF16)
    _, _, dg_mem = _rms_bwd(mem, vec(w["g_mem"]), dmem_n, None)
    g["g_mem"] = [one(dg_mem[0])]
    return loss, dx, g


SHARDED = {
    "w_in_a": ((N_A, D_MODEL, 2 * CONV_CH + MEM_W), 2),
    "b_glu": ((N_A, 2 * CONV_CH), 1),
    "w_dw_a": ((N_A, CONV_W, CONV_CH), 2),
    "b_dw_a": ((N_A, CONV_CH), 1),
    "ln_g": ((N_A, CONV_CH), 1),
    "ln_b": ((N_A, CONV_CH), 1),
    "w_kvf": ((D_MODEL, 2 * CONV_CH + N_FOX_HEADS), 1),
    "w_in_b": ((N_LAYERS - N_A, D_MODEL, CONV_CH + MEM_W), 1),
    "w_mem_kv": ((N_LAYERS, D_MODEL, 2 * MEM_W), 1),
    "w_out": ((N_LAYERS, D_MODEL, D_MODEL), 1),
    "w_up": ((N_LAYERS, D_MODEL, 2 * D_FF), 2),
    "w_dw_f": ((N_LAYERS, FFN_CONV_W, 2 * D_FF), 2),
    "w_down": ((N_LAYERS, D_FF, D_MODEL), 1),
}
REPLICATED = {
    "g_mix": (N_LAYERS, D_MODEL), "g_kv": (D_MODEL,), "b_f": (N_FOX_HEADS,), "g_mem": (D_MODEL,),
    "g_ffn": (N_LAYERS, D_MODEL), "b_dw_f": (N_LAYERS, 2 * D_FF), "g_final": (D_MODEL,),
}
WEIGHT_ORDER = ["g_mix", "w_in_a", "b_glu", "w_dw_a", "b_dw_a", "ln_g", "ln_b", "g_kv", "w_kvf", "b_f", "w_in_b", "g_mem",
                "w_mem_kv", "w_out", "g_ffn", "w_up", "w_dw_f", "b_dw_f", "w_down", "g_final"]


def _shard_shape(name):
    shape, axis = SHARDED[name]
    return tuple(d // N_CHIPS if i == axis else d for i, d in enumerate(shape))


def _size(shape):
    n = 1
    for d in shape:
        n *= d
    return n


VECTOR_LEAVES = ("b_glu", "w_dw_a", "b_dw_a", "ln_g", "ln_b", "w_dw_f")
_HALF_ROW_TILE = 16
_VECTOR_ELEMS = sum(_size(_shard_shape(n)) for n in VECTOR_LEAVES)
VECTOR_ROWS = -(-_VECTOR_ELEMS // (FLAT_COLS * 2 * _HALF_ROW_TILE)) * 2 * _HALF_ROW_TILE
REP_ELEMS = sum(_size(s) for s in REPLICATED.values())
REP_ROWS = 8
REP_COLS = -(-REP_ELEMS // (REP_ROWS * LANE)) * LANE


def _flatten_vectors(parts, dtype):
    flat = jnp.concatenate([p.reshape(-1).astype(dtype) for p in parts])
    return jnp.pad(flat, (0, VECTOR_ROWS * FLAT_COLS - _VECTOR_ELEMS)).reshape(2, VECTOR_ROWS // 2, FLAT_COLS)


def _unflatten_vectors(flat, lead=()):
    flat = flat.reshape(lead + (-1,))
    out, off = {}, 0
    for n in VECTOR_LEAVES:
        shp = _shard_shape(n)
        out[n] = flat[..., off:off + _size(shp)].reshape(lead + shp)
        off += _size(shp)
    return out


def _span(parts, cat_axis, sel_axis, lo, hi):
    if sel_axis != cat_axis:
        return jnp.concatenate([lax.slice_in_dim(p, lo, hi, axis=sel_axis) for p in parts], axis=cat_axis)
    taken, off = [], 0
    for p in parts:
        n = p.shape[cat_axis]
        a, b = max(lo, off), min(hi, off + n)
        if a < b:
            taken.append(lax.slice_in_dim(p, a - off, b - off, axis=cat_axis))
        off += n
    return taken[0] if len(taken) == 1 else jnp.concatenate(taken, axis=cat_axis)


LAYER_LEAVES = tuple(
    ([("w_in_a", l)] if l < N_A else ([("w_kvf", None)] if l == N_A else []) + [("w_in_b", l - N_A)])
    + [("w_mem_kv", l), ("w_out", l), ("w_up", l), ("w_down", l)]
    for l in range(N_LAYERS))


def _layer_halves(weights, l):
    out = []
    for name, idx in LAYER_LEAVES[l]:
        shard = weights[name] if idx is None else weights[name][idx]
        rows, cols = shard.shape
        assert (rows // 2) % _HALF_ROW_TILE == 0, name
        out.append(shard.astype(BF16).reshape(2, rows // 2, cols))
    return out


def _layer_pieces(l, gathered, vectors):
    parts = {}
    for (name, idx), g in zip(LAYER_LEAVES[l], gathered):
        parts[name] = [g[t].reshape(-1, g.shape[-1]) for t in range(N_CHIPS)]

    def cut(name, sel_axis, lo, hi):
        axis = SHARDED[name][1] - (len(SHARDED[name][0]) - 2)
        return _span(parts[name], axis, sel_axis, lo, hi)

    def vcut(name, idx, sel_axis, lo, hi):
        axis = SHARDED[name][1] - 1
        return _span([vectors[name][t, idx] for t in range(N_CHIPS)], axis, sel_axis, lo, hi)

    def pad_rows(arr, rows):
        return jnp.pad(arr, ((0, rows - arr.shape[0]), (0, 0)))

    w = {}
    if l < N_A:
        w["w_a"] = cut("w_in_a", 1, 0, CONV_CH)
        w["w_g"] = cut("w_in_a", 1, CONV_CH, 2 * CONV_CH)
        w["w_qm"] = cut("w_in_a", 1, 2 * CONV_CH, 2 * CONV_CH + MEM_W)
        w["b_glu_a"] = vcut("b_glu", l, 0, 0, CONV_CH)
        w["b_glu_g"] = vcut("b_glu", l, 0, CONV_CH, 2 * CONV_CH)
        w["w_dw_a"] = pad_rows(vcut("w_dw_a", l, 1, 0, CONV_CH), CONV_PAD)
        for n in ("b_dw_a", "ln_g", "ln_b"):
            w[n] = vcut(n, l, 0, 0, CONV_CH)
    else:
        if l == N_A:
            w["w_k"] = cut("w_kvf", 1, 0, CONV_CH)
            w["w_v"] = cut("w_kvf", 1, CONV_CH, 2 * CONV_CH)
            w["w_f"] = jnp.pad(cut("w_kvf", 1, 2 * CONV_CH, 2 * CONV_CH + N_FOX_HEADS), ((0, 0), (0, F_PAD - N_FOX_HEADS)))
        w["w_q"] = cut("w_in_b", 1, 0, CONV_CH)
        w["w_qm"] = cut("w_in_b", 1, CONV_CH, CONV_CH + MEM_W)
    w["w_mk"] = cut("w_mem_kv", 1, 0, MEM_W)
    w["w_mv"] = cut("w_mem_kv", 1, MEM_W, 2 * MEM_W)
    w["w_o_mix"] = cut("w_out", 0, 0, CONV_CH)
    w["w_o_mem"] = cut("w_out", 0, CONV_CH, D_MODEL)
    w["w_up_g"] = cut("w_up", 1, 0, D_FF)
    w["w_up_v"] = cut("w_up", 1, D_FF, 2 * D_FF)
    w["w_dw_f_g"] = pad_rows(vcut("w_dw_f", l, 1, 0, D_FF), FFN_PAD)
    w["w_dw_f_v"] = pad_rows(vcut("w_dw_f", l, 1, D_FF, 2 * D_FF), FFN_PAD)
    w["w_down"] = cut("w_down", 0, 0, D_FF)
    return w


_PER_LAYER = ("w_a", "w_g", "w_q", "w_qm", "b_glu_a", "b_glu_g", "w_dw_a", "b_dw_a", "ln_g", "ln_b", "w_mk", "w_mv", "w_o_mix", "w_o_mem",
              "w_up_g", "w_up_v", "w_dw_f_g", "w_dw_f_v", "w_down")


def _weight_table(rep):
    w = dict(rep)
    w["b_f"] = jnp.pad(rep["b_f"], (0, F_PAD - N_FOX_HEADS)).reshape(1, F_PAD)
    w["b_dw_f_g"], w["b_dw_f_v"] = rep["b_dw_f"][:, :D_FF], rep["b_dw_f"][:, D_FF:]
    for n in _PER_LAYER:
        w[n] = {}
    return w


def _install_layer(w, l, pieces):
    for n, p in pieces.items():
        if n in _PER_LAYER:
            w[n][l - N_A if n == "w_q" else l] = p
        else:
            w[n] = p


def _shard_of(entry, name, t):
    shape, axis = SHARDED[name]
    pieces, cat_axis = entry
    axis -= len(shape) - len(pieces[0].shape)
    step = shape[SHARDED[name][1]] // N_CHIPS
    return _span(pieces, cat_axis, axis, t * step, (t + 1) * step)


def _layer_slabs(grads, l):
    out = []
    for name, idx in LAYER_LEAVES[l]:
        shards = [_shard_of(grads[name][0 if idx is None else idx], name, t).astype(BF16) for t in range(N_CHIPS)]
        rows, cols = shards[0].shape
        out.append(jnp.stack(shards).reshape(N_CHIPS, 2, rows // 2, cols))
    return out


def _vector_slabs(grads):
    def shard(name, t):
        return jnp.stack([_shard_of(entry, name, t) for entry in grads[name]])
    return jnp.stack([_flatten_vectors([shard(n, t) for n in VECTOR_LEAVES], BF16) for t in range(N_CHIPS)])


def _replicated_rows(grads):
    parts = [p.reshape(-1) for n in REPLICATED for pieces, _ in grads[n] for p in pieces]
    flat = jnp.concatenate(parts)
    return jnp.pad(flat, (0, REP_ROWS * REP_COLS - REP_ELEMS)).reshape(REP_ROWS, REP_COLS)


_ANY = pl.BlockSpec(memory_space=pl.ANY)


def _place():
    x, y, c = lax.axis_index("x"), lax.axis_index("y"), lax.axis_index("c")
    chips = [(1 - x, y), (x, 1 - y), (1 - x, 1 - y)]
    return x, y, c, chips


def _chip_index(chip):
    return 2 * chip[0] + chip[1]


def _gather_leaves(halves):
    n = len(halves)

    def body(*refs):
        w_refs, out_refs, (send_sems, recv_sems) = refs[:n], refs[n:2 * n], refs[2 * n:]
        x, y, c, chips = _place()
        me = 2 * x + y
        sibling = (x, y, 1 - c)

        def copy(a, sem, chip_idx, half, to, src=None):
            dst = out_refs[a].at[chip_idx, half]
            return pltpu.make_async_remote_copy(src_ref=dst if src is None else src, dst_ref=dst, send_sem=send_sems.at[6 * a + sem],
                                                recv_sem=recv_sems.at[6 * a + sem], device_id=to, device_id_type=MESH_ID)

        first = [copy(a, j, me, c, (*chip, c), src=w_refs[a].at[c]) for j, chip in enumerate(chips) for a in range(n)]
        for cp in first:
            cp.start()
        passed = []
        for j, chip in enumerate(chips):
            for a in range(n):
                copy(a, j, _chip_index(chip), c, (x, y, c)).wait_recv()
                passed.append(copy(a, 3 + j, _chip_index(chip), c, sibling))
                passed[-1].start()
        for j, chip in enumerate(chips):
            for a in range(n):
                copy(a, 3 + j, _chip_index(chip), 1 - c, (x, y, c)).wait_recv()
        for cp in first + passed:
            cp.wait_send()

    return pl.pallas_call(
        body, out_shape=[jax.ShapeDtypeStruct((N_CHIPS,) + h.shape, h.dtype) for h in halves],
        in_specs=[_ANY] * n, out_specs=[_ANY] * n,
        scratch_shapes=[pltpu.SemaphoreType.DMA((6 * n,)), pltpu.SemaphoreType.DMA((6 * n,))], name="gather_leaves",
    )(*halves)


_HBM = pl.BlockSpec(memory_space=pltpu.HBM)
_SEM = pl.BlockSpec(memory_space=pltpu.SEMAPHORE)


def _in_hbm(a):
    return pltpu.with_memory_space_constraint(a, pltpu.HBM)


def _gather_start(groups, after):
    flat = [h for g in groups for h in g]
    n, ng = len(flat), len(groups)

    def body(*refs):
        srcs, lands, sems = refs[1:1 + n], refs[1 + n:1 + 2 * n], refs[1 + 2 * n:1 + 2 * n + 2 * ng]
        token = refs[-1]
        x, y, c, chips = _place()
        me = 2 * x + y
        a = 0
        for gi, g in enumerate(groups):
            for k in range(len(g)):
                for j, chip in enumerate(chips):
                    pltpu.make_async_remote_copy(src_ref=srcs[a].at[c], dst_ref=lands[a].at[me, c], send_sem=sems[2 * gi].at[3 * k + j],
                                                 recv_sem=sems[2 * gi + 1].at[3 * k + j], device_id=(*chip, c), device_id_type=MESH_ID).start()
                a += 1
        token[...] = jnp.zeros_like(token)

    sem_shapes = [pltpu.SemaphoreType.DMA((3 * len(g),)) for g in groups for _ in range(2)]
    out = pl.pallas_call(
        body, name="gather_start",
        out_shape=sem_shapes + [pltpu.HBM(h.shape, h.dtype) for h in flat] + [pltpu.HBM((N_CHIPS,) + h.shape, h.dtype) for h in flat]
        + [jax.ShapeDtypeStruct((8, LANE), F32)],
        in_specs=[_ANY] + [_HBM] * (2 * n), out_specs=[_SEM] * (2 * ng) + [_HBM] * (2 * n) + [pl.BlockSpec(memory_space=pltpu.VMEM)],
        input_output_aliases={1 + i: 2 * ng + i for i in range(2 * n)},
        compiler_params=pltpu.CompilerParams(has_side_effects=pltpu.SideEffectType.DATAFLOW_SIDE_EFFECTING),
    )(after, *[_in_hbm(h) for h in flat], *[_in_hbm(lax.empty((N_CHIPS,) + h.shape, h.dtype)) for h in flat])
    sems, srcs, lands = out[:2 * ng], out[2 * ng:2 * ng + n], out[2 * ng + n:2 * ng + 2 * n]
    res, a = [], 0
    for gi, g in enumerate(groups):
        res.append((sems[2 * gi], sems[2 * gi + 1], list(srcs[a:a + len(g)]), list(lands[a:a + len(g)])))
        a += len(g)
    return res


def _gather_wait(send_sems, recv_sems, srcs, lands, after, name):
    n = len(srcs)

    def body(*refs):
        src_refs, land_refs, send, recv = refs[:n], refs[n:2 * n], refs[2 * n], refs[2 * n + 1]
        x, y, c, chips = _place()
        for k in range(n):
            for j, chip in enumerate(chips):
                cp = pltpu.make_async_remote_copy(src_ref=src_refs[k].at[c], dst_ref=land_refs[k].at[_chip_index(chip), c], send_sem=send.at[3 * k + j],
                                                  recv_sem=recv.at[3 * k + j], device_id=(*chip, c), device_id_type=MESH_ID)
                cp.wait_send()
                cp.wait_recv()

    out = pl.pallas_call(
        body, name=name, out_shape=[pltpu.HBM(a.shape, a.dtype) for a in list(srcs) + list(lands)],
        in_specs=[_HBM] * (2 * n) + [_SEM, _SEM, _ANY], out_specs=[_HBM] * (2 * n), input_output_aliases={i: i for i in range(2 * n)},
        compiler_params=pltpu.CompilerParams(has_side_effects=pltpu.SideEffectType.DATAFLOW_SIDE_EFFECTING),
    )(*srcs, *lands, send_sems, recv_sems, after)
    return list(out[n:])


def _forward_halves(lands, name):
    n = len(lands)

    def body(*refs):
        out_refs, (send_sems, recv_sems) = refs[n:2 * n], refs[2 * n:]
        x, y, c, chips = _place()

        def copy(a, j, half, to):
            blk = out_refs[a].at[_chip_index(chips[j]), half]
            return pltpu.make_async_remote_copy(src_ref=blk, dst_ref=blk, send_sem=send_sems.at[3 * a + j], recv_sem=recv_sems.at[3 * a + j],
                                                device_id=to, device_id_type=MESH_ID)

        sends = [copy(a, j, c, (x, y, 1 - c)) for a in range(n) for j in range(3)]
        for cp in sends:
            cp.start()
        for a in range(n):
            for j in range(3):
                copy(a, j, 1 - c, (x, y, c)).wait_recv()
        for cp in sends:
            cp.wait_send()

    return pl.pallas_call(
        body, out_shape=[jax.ShapeDtypeStruct(a.shape, a.dtype) for a in lands], in_specs=[_ANY] * n, out_specs=[_ANY] * n,
        input_output_aliases={i: i for i in range(n)},
        scratch_shapes=[pltpu.SemaphoreType.DMA((3 * n,)), pltpu.SemaphoreType.DMA((3 * n,))], name=name,
    )(*lands)


def _swap_halves(slabs):
    n = len(slabs)

    def body(*refs):
        g_refs, got_refs, (send_sem, recv_sem) = refs[:n], refs[n:2 * n], refs[2 * n:]
        x, y, c, _ = _place()
        copies = [pltpu.make_async_remote_copy(src_ref=g_refs[a].at[t, 1 - c], dst_ref=got_refs[a].at[t], send_sem=send_sem.at[N_CHIPS * a + t],
                                               recv_sem=recv_sem.at[N_CHIPS * a + t], device_id=(x, y, 1 - c), device_id_type=MESH_ID)
                  for a in range(n) for t in range(N_CHIPS)]
        for cp in copies:
            cp.start()
        for cp in copies:
            cp.wait()

    return pl.pallas_call(
        body, out_shape=[jax.ShapeDtypeStruct((g.shape[0],) + g.shape[2:], g.dtype) for g in slabs],
        in_specs=[_ANY] * n, out_specs=[_ANY] * n,
        scratch_shapes=[pltpu.SemaphoreType.DMA((N_CHIPS * n,)), pltpu.SemaphoreType.DMA((N_CHIPS * n,))], name="swap_halves",
    )(*slabs)


def _scatter_start(parts, name):
    n = len(parts)

    def body(*refs):
        srcs, lands, send, recv = refs[:n], refs[n:2 * n], refs[2 * n], refs[2 * n + 1]
        token = refs[-1]
        x, y, c, chips = _place()
        me = 2 * x + y
        for k in range(n):
            for j, chip in enumerate(chips):
                pltpu.make_async_remote_copy(src_ref=srcs[k].at[_chip_index(chip)], dst_ref=lands[k].at[me], send_sem=send.at[3 * k + j],
                                             recv_sem=recv.at[3 * k + j], device_id=(*chip, c), device_id_type=MESH_ID).start()
        token[...] = jnp.zeros_like(token)

    out = pl.pallas_call(
        body, name=name,
        out_shape=[pltpu.SemaphoreType.DMA((3 * n,)), pltpu.SemaphoreType.DMA((3 * n,))] + [pltpu.HBM(p.shape, p.dtype) for p in parts] * 2
        + [jax.ShapeDtypeStruct((8, LANE), F32)],
        in_specs=[_HBM] * (2 * n), out_specs=[_SEM, _SEM] + [_HBM] * (2 * n) + [pl.BlockSpec(memory_space=pltpu.VMEM)],
        input_output_aliases={i: 2 + i for i in range(2 * n)},
        compiler_params=pltpu.CompilerParams(has_side_effects=pltpu.SideEffectType.DATAFLOW_SIDE_EFFECTING),
    )(*[_in_hbm(p) for p in parts], *[_in_hbm(lax.empty(p.shape, p.dtype)) for p in parts])
    return (out[0], out[1], list(out[2:2 + n]), list(out[2 + n:2 + 2 * n])), out[-1]


def _scatter_wait(send_sems, recv_sems, srcs, lands, after, name):
    n = len(srcs)

    def body(*refs):
        src_refs, land_refs, send, recv = refs[:n], refs[n:2 * n], refs[2 * n], refs[2 * n + 1]
        x, y, c, chips = _place()
        for k in range(n):
            for j, chip in enumerate(chips):
                cp = pltpu.make_async_remote_copy(src_ref=src_refs[k].at[_chip_index(chip)], dst_ref=land_refs[k].at[_chip_index(chip)],
                                                  send_sem=send.at[3 * k + j], recv_sem=recv.at[3 * k + j], device_id=(*chip, c), device_id_type=MESH_ID)
                cp.wait_send()
                cp.wait_recv()

    out = pl.pallas_call(
        body, name=name, out_shape=[pltpu.HBM(a.shape, a.dtype) for a in list(srcs) + list(lands)],
        in_specs=[_HBM] * (2 * n) + [_SEM, _SEM, _ANY], out_specs=[_HBM] * (2 * n), input_output_aliases={i: i for i in range(2 * n)},
        compiler_params=pltpu.CompilerParams(has_side_effects=pltpu.SideEffectType.DATAFLOW_SIDE_EFFECTING),
    )(*srcs, *lands, send_sems, recv_sems, after)
    return list(out[n:])


def _sibling_halves(halves):
    n = len(halves)

    def body(*refs):
        h_refs, got_refs, (send_sem, recv_sem) = refs[:n], refs[n:2 * n], refs[2 * n:]
        x, y, c, _ = _place()
        copies = [pltpu.make_async_remote_copy(src_ref=h_refs[a], dst_ref=got_refs[a], send_sem=send_sem.at[a], recv_sem=recv_sem.at[a],
                                               device_id=(x, y, 1 - c), device_id_type=MESH_ID) for a in range(n)]
        for cp in copies:
            cp.start()
        for cp in copies:
            cp.wait()

    return pl.pallas_call(
        body, out_shape=[jax.ShapeDtypeStruct(h.shape, h.dtype) for h in halves], in_specs=[_ANY] * n, out_specs=[_ANY] * n,
        scratch_shapes=[pltpu.SemaphoreType.DMA((n,)), pltpu.SemaphoreType.DMA((n,))], name="sibling_halves",
    )(*halves)


def _gather_replicated(rows):
    m_per, n = rows.shape

    def body(x_ref, out_ref, send_sems, recv_sems, local_sem):
        x, y, c, chips = _place()
        me, sibling = (x, y, c), (x, y, 1 - c)

        def block(px, py, pc):
            return out_ref.at[pl.ds((4 * px + 2 * py + pc) * m_per, m_per), :]

        def copy(k, blk, to, src=None):
            return pltpu.make_async_remote_copy(src_ref=block(*blk) if src is None else src, dst_ref=block(*blk),
                                                send_sem=send_sems.at[k], recv_sem=recv_sems.at[k], device_id=to, device_id_type=MESH_ID)

        mine = pltpu.make_async_copy(x_ref, block(*me), local_sem)
        mine.start()
        first = [copy(0, me, sibling, src=x_ref)]
        first += [copy(1 + j, me, (*chip, c), src=x_ref) for j, chip in enumerate(chips)]
        for cp in first:
            cp.start()
        passed = [copy(4 + j, (*chip, c), sibling) for j, chip in enumerate(chips)]
        for j, chip in enumerate(chips):
            copy(1 + j, (*chip, c), me).wait_recv()
            passed[j].start()
        copy(0, sibling, me).wait_recv()
        for j, chip in enumerate(chips):
            copy(4 + j, (*chip, 1 - c), me).wait_recv()
        for cp in first + passed:
            cp.wait_send()
        mine.wait()

    vmem = pl.BlockSpec(memory_space=pltpu.VMEM)
    return pl.pallas_call(
        body, out_shape=jax.ShapeDtypeStruct((N_DEV * m_per, n), rows.dtype), in_specs=[vmem], out_specs=vmem,
        scratch_shapes=[pltpu.SemaphoreType.DMA((7,)), pltpu.SemaphoreType.DMA((7,)), pltpu.SemaphoreType.DMA],
        name="gather_replicated",
    )(rows)


_SUM_ROWS = 256


def _tile_rows(rows):
    if rows <= _SUM_ROWS:
        return rows
    best = None
    for r in range(16, _SUM_ROWS + 1, 16):
        if rows % r == 0:
            best = r
    assert best is not None, rows
    return best


def _add_pairs(slabs, got, core):
    n, _, rows, cols = slabs.shape
    tr = _tile_rows(rows)

    def body(core_ref, a_ref, b_ref, o_ref):
        o_ref[...] = (a_ref[0].astype(F32) + b_ref[...].astype(F32)).astype(BF16)

    spec = pl.BlockSpec((1, tr, cols), lambda t, i, core_ref: (t, i, 0))
    return pl.pallas_call(
        body, out_shape=jax.ShapeDtypeStruct((n, rows, cols), BF16),
        grid_spec=pltpu.PrefetchScalarGridSpec(
            num_scalar_prefetch=1, grid=(n, rows // tr),
            in_specs=[pl.BlockSpec((1, 1, tr, cols), lambda t, i, core_ref: (t, core_ref[0], i, 0)), spec], out_specs=spec),
        compiler_params=_params(("parallel", "parallel")), name=f"add_pairs_{rows}x{cols}",
    )(core, slabs, got)


def _sum_slabs(slabs):
    n, rows, cols = slabs.shape
    tr = _tile_rows(rows)

    def body(s_ref, o_ref):
        acc = s_ref[0].astype(F32)
        for t in range(1, n):
            acc = acc + s_ref[t].astype(F32)
        o_ref[...] = acc

    return pl.pallas_call(
        body, out_shape=jax.ShapeDtypeStruct((rows, cols), F32), grid=(rows // tr,),
        in_specs=[pl.BlockSpec((n, tr, cols), lambda i: (0, i, 0))], out_specs=pl.BlockSpec((tr, cols), lambda i: (i, 0)),
        compiler_params=_params(("parallel",)), name=f"sum_slabs_{n}x{rows}x{cols}",
    )(slabs)


_ADAM_BLOCK_BYTES = 2 * 1024 * 1024


def _adamw(w, g, m, v):
    shape = w.shape
    cols = shape[-1]
    rows = _size(shape) // cols
    tr = rows
    if rows * cols * 4 > _ADAM_BLOCK_BYTES:
        for r in range(8, rows, 8):
            if rows % r == 0 and r * cols * 4 <= _ADAM_BLOCK_BYTES:
                tr = r

    def body(w_ref, g_ref, m_ref, v_ref, d_ref, nm_ref, nv_ref):
        gv = g_ref[...]
        nm = ADAM_B1 * m_ref[...] + (1.0 - ADAM_B1) * gv
        nv = ADAM_B2 * v_ref[...] + (1.0 - ADAM_B2) * jnp.square(gv)
        m_hat = nm / (1.0 - ADAM_B1 ** ADAM_STEP)
        v_hat = nv / (1.0 - ADAM_B2 ** ADAM_STEP)
        d_ref[...] = -ADAM_LR * (m_hat / (jnp.sqrt(v_hat) + ADAM_EPS) + ADAM_WD * w_ref[...])
        nm_ref[...] = nm
        nv_ref[...] = nv

    spec = pl.BlockSpec((tr, cols), lambda i: (i, 0))
    out = pl.pallas_call(
        body, out_shape=tuple(jax.ShapeDtypeStruct((rows, cols), F32) for _ in range(3)), grid=(rows // tr,),
        in_specs=[spec] * 4, out_specs=(spec,) * 3, compiler_params=_params(("parallel",)), name=f"adamw_{rows}x{cols}",
    )(*(a.reshape(rows, cols) for a in (w, g, m, v)))
    return tuple(o.reshape(shape) for o in out)


def kernel(x, mem, g_mix, w_in_a, b_glu, w_dw_a, b_dw_a, ln_g, ln_b, g_kv, w_kvf, b_f, w_in_b, g_mem, w_mem_kv, w_out, g_ffn, w_up, w_dw_f, b_dw_f, w_down, g_final, loss_target, m_g_mix, m_w_in_a, m_b_glu, m_w_dw_a, m_b_dw_a, m_ln_g, m_ln_b, m_g_kv, m_w_kvf, m_b_f, m_w_in_b, m_g_mem, m_w_mem_kv, m_w_out, m_g_ffn, m_w_up, m_w_dw_f, m_b_dw_f, m_w_down, m_g_final, v_g_mix, v_w_in_a, v_b_glu, v_w_dw_a, v_b_dw_a, v_ln_g, v_ln_b, v_g_kv, v_w_kvf, v_b_f, v_w_in_b, v_g_mem, v_w_mem_kv, v_w_out, v_g_ffn, v_w_up, v_w_dw_f, v_b_dw_f, v_w_down, v_g_final):
    weights = dict(g_mix=g_mix, w_in_a=w_in_a, b_glu=b_glu, w_dw_a=w_dw_a, b_dw_a=b_dw_a, ln_g=ln_g, ln_b=ln_b, g_kv=g_kv,
                   w_kvf=w_kvf, b_f=b_f, w_in_b=w_in_b, g_mem=g_mem, w_mem_kv=w_mem_kv, w_out=w_out, g_ffn=g_ffn, w_up=w_up,
                   w_dw_f=w_dw_f, b_dw_f=b_dw_f, w_down=w_down, g_final=g_final)
    mom1 = dict(g_mix=m_g_mix, w_in_a=m_w_in_a, b_glu=m_b_glu, w_dw_a=m_w_dw_a, b_dw_a=m_b_dw_a, ln_g=m_ln_g, ln_b=m_ln_b,
                g_kv=m_g_kv, w_kvf=m_w_kvf, b_f=m_b_f, w_in_b=m_w_in_b, g_mem=m_g_mem, w_mem_kv=m_w_mem_kv, w_out=m_w_out,
                g_ffn=m_g_ffn, w_up=m_w_up, w_dw_f=m_w_dw_f, b_dw_f=m_b_dw_f, w_down=m_w_down, g_final=m_g_final)
    mom2 = dict(g_mix=v_g_mix, w_in_a=v_w_in_a, b_glu=v_b_glu, w_dw_a=v_w_dw_a, b_dw_a=v_b_dw_a, ln_g=v_ln_g, ln_b=v_ln_b,
                g_kv=v_g_kv, w_kvf=v_w_kvf, b_f=v_b_f, w_in_b=v_w_in_b, g_mem=v_g_mem, w_mem_kv=v_w_mem_kv, w_out=v_w_out,
                g_ffn=v_g_ffn, w_up=v_w_up, w_dw_f=v_w_dw_f, b_dw_f=v_b_dw_f, w_down=v_w_down, g_final=v_g_final)

    x_pos, y_pos, core = lax.axis_index("x"), lax.axis_index("y"), lax.axis_index("c")
    chip = 2 * x_pos + y_pos

    local = [_layer_halves(weights, l) for l in range(N_LAYERS)]
    vector_halves = _flatten_vectors([weights[n] for n in VECTOR_LEAVES], F32)
    first = _gather_leaves(local[0] + [vector_halves])
    in_flight = _gather_start(local[1:], first[-1])

    def with_own(gathered, own):
        return [lax.dynamic_update_index_in_dim(g, o[None], chip, 0) for g, o in zip(gathered, own)]

    first = with_own(first, local[0] + [vector_halves])
    vectors = _unflatten_vectors(first[-1], (N_CHIPS,))
    table = _weight_table({n: weights[n] for n in REPLICATED})

    def arrive(l, after):
        if l == 0:
            gathered = first[:-1]
        else:
            send_sems, recv_sems, srcs, lands = in_flight[l - 1]
            lands = _gather_wait(send_sems, recv_sems, srcs, lands, after, f"gather_wait_{l}")
            gathered = with_own(_forward_halves(lands, f"forward_halves_{l}"), local[l])
        _install_layer(table, l, _layer_pieces(l, gathered, vectors))

    core_index = core.astype(jnp.int32).reshape(1)
    in_flight_back = {}

    def pre_sum(slabs):
        return [_add_pairs(mine, got, core_index) for mine, got in zip(slabs, _swap_halves(slabs))]

    def emit(l, g):
        slabs = _layer_slabs(g, l) + ([_vector_slabs(g)] if l == 0 else [])
        chip_sums = pre_sum(slabs)
        handles, started = _scatter_start(chip_sums, f"scatter_start_{l}")
        in_flight_back[l] = (chip_sums, handles)
        return started

    loss, dx, grads = _local_step(x[0], mem[0], loss_target[0], table, arrive, emit)
    loss = lax.psum(loss, ("x", "y", "c"))

    mine = {}
    for l in reversed(range(N_LAYERS)):
        chip_sums, (send_sems, recv_sems, srcs, lands) = in_flight_back[l]
        lands = _scatter_wait(send_sems, recv_sems, srcs, lands, dx, f"scatter_wait_{l}")
        from_chips = [lax.dynamic_update_index_in_dim(got, lax.dynamic_index_in_dim(own, chip, 0, keepdims=True), chip, 0)
                      for got, own in zip(lands, chip_sums)]
        mine[l] = [_sum_slabs(f) for f in from_chips]
    flat_mine = [m for l in range(N_LAYERS) for m in mine[l]]
    theirs = _sibling_halves(flat_mine)
    joined = [jnp.concatenate([jnp.where(core == 0, m, t), jnp.where(core == 0, t, m)]) for m, t in zip(flat_mine, theirs)]
    per_leaf, a = {}, 0
    for l in range(N_LAYERS):
        for name, idx in LAYER_LEAVES[l]:
            per_leaf.setdefault(name, {})[idx] = joined[a]
            a += 1
        if l == 0:
            vector_grads = joined[a]
            a += 1
    grad_leaves = {n: (d[None] if None in d else jnp.stack([d[i] for i in sorted(d)])) for n, d in per_leaf.items()}
    grad_leaves.update(_unflatten_vectors(vector_grads))

    rep_sum = _sum_slabs(_gather_replicated(_replicated_rows(grads)).reshape(N_DEV, REP_ROWS, REP_COLS)).reshape(-1)
    off = 0
    for n, shp in REPLICATED.items():
        grad_leaves[n] = rep_sum[off:off + _size(shp)].reshape(shp)
        off += _size(shp)

    deltas, new_m, new_v = {}, {}, {}
    for n in WEIGHT_ORDER:
        deltas[n], new_m[n], new_v[n] = _adamw(weights[n], grad_leaves[n], mom1[n], mom2[n])
    return (loss, dx[None], *[grad_leaves[n] for n in WEIGHT_ORDER], *[deltas[n] for n in WEIGHT_ORDER],
            *[new_m[n] for n in WEIGHT_ORDER], *[new_v[n] for n in WEIGHT_ORDER])
```

```python
import jax
import jax.numpy as jnp
from jax import lax
from jax.experimental import pallas as pl
from jax.experimental.pallas import tpu as pltpu

F32 = jnp.float32
BF16 = jnp.bfloat16

D_MODEL = 1024
N_LAYERS = 4
N_A = 2
CONV_CH = 768
MEM_W = 256
HEAD_DIM = 64
N_MEM_HEADS = 4
N_FOX_HEADS = 12
N_HEAD_PAIRS = N_FOX_HEADS // 2
D_FF = 2816
CONV_W = 31
CONV_PAD = 32
FFN_CONV_W = 3
FFN_PAD = 8
F_PAD = 128
RMS_EPS = 1e-6
LN_EPS = 1e-5
ATT_SCALE = HEAD_DIM ** -0.5
NEG_BIG = -1e30

ADAM_LR = 0.001
ADAM_B1 = 0.9
ADAM_B2 = 0.999
ADAM_EPS = 1e-08
ADAM_WD = 0.01
ADAM_STEP = 10

LANE = 128
ROW_TILE = 256
CHUNK = 128
VMEM_LIMIT = 48 * 1024 * 1024
FLAT_COLS = 1024
N_CHIPS = 4
N_DEV = 8
MESH_ID = pl.DeviceIdType.MESH


def _params(sem=None):
    return pltpu.CompilerParams(dimension_semantics=sem, vmem_limit_bytes=VMEM_LIMIT)


def _tile(dim, pref):
    if dim <= pref:
        return dim
    best = None
    for m in range(1, dim // LANE + 1):
        d = m * LANE
        if dim % d == 0 and d <= pref:
            best = d
    assert best is not None, dim
    return best


_DIMS = {"nn": (((1,), (0,)), ((), ())), "nt": (((1,), (1,)), ((), ())), "tn": (((0,), (0,)), ((), ()))}


_MM_RESIDENT_BYTES = 8 * 1024 * 1024
_MM_STREAM_BYTES = 6 * 1024 * 1024
_MM_OUT_BYTES = 6 * 1024 * 1024


def _mm_tiles(m, n, k, mode, a_size, b_size, o_size):
    if mode == "tn":
        tm = _tile(m, _MM_RESIDENT_BYTES // (k * a_size))
        tn = _tile(n, min(_MM_STREAM_BYTES // (k * b_size), _MM_OUT_BYTES // (tm * o_size)))
    else:
        tn = _tile(n, _MM_RESIDENT_BYTES // (k * b_size))
        tm = _tile(m, min(_MM_STREAM_BYTES // (k * a_size), _MM_OUT_BYTES // (tn * o_size), 512))
    return tm, tn


def _mm(a, b, mode="nn", out_dtype=F32, add=None):
    if mode == "nn":
        (m, k), (k2, n) = a.shape, b.shape
    elif mode == "nt":
        (m, k), (n, k2) = a.shape, b.shape
    else:
        (k, m), (k2, n) = a.shape, b.shape
    assert k == k2, (a.shape, b.shape, mode)
    tm, tn = _mm_tiles(m, n, k, mode, a.dtype.itemsize, b.dtype.itemsize, jnp.dtype(out_dtype).itemsize)
    dims = _DIMS[mode]
    has_add = add is not None

    def body(*refs):
        if has_add:
            a_ref, b_ref, add_ref, o_ref = refs
        else:
            a_ref, b_ref, o_ref = refs
        r = lax.dot_general(a_ref[...].astype(BF16), b_ref[...].astype(BF16), dims, preferred_element_type=F32)
        if has_add:
            r = r + add_ref[...]
        o_ref[...] = r.astype(out_dtype)

    a_spec = pl.BlockSpec((k, tm), lambda i, j: (0, i)) if mode == "tn" else pl.BlockSpec((tm, k), lambda i, j: (i, 0))
    b_spec = pl.BlockSpec((tn, k), lambda i, j: (j, 0)) if mode == "nt" else pl.BlockSpec((k, tn), lambda i, j: (0, j))
    o_spec = pl.BlockSpec((tm, tn), lambda i, j: (i, j))
    in_specs = [a_spec, b_spec] + ([o_spec] if has_add else [])
    args = (a, b) + ((add,) if has_add else ())
    return pl.pallas_call(
        body,
        out_shape=jax.ShapeDtypeStruct((m, n), out_dtype),
        grid=(m // tm, n // tn),
        in_specs=in_specs,
        out_specs=o_spec,
        compiler_params=_params(("parallel", "parallel")),
        name=f"mm_{mode}_{m}x{k}x{n}",
    )(*args)


def _row(width):
    return pl.BlockSpec((ROW_TILE, width), lambda i: (i, 0))


def _vec(width):
    return pl.BlockSpec((1, width), lambda i: (0, 0))


def _rms_fwd(x, g):
    rows, d = x.shape

    def body(x_ref, g_ref, o_ref):
        xv = x_ref[...]
        rstd = lax.rsqrt(jnp.mean(xv * xv, axis=-1, keepdims=True) + RMS_EPS)
        o_ref[...] = (xv * rstd * g_ref[...]).astype(BF16)

    return pl.pallas_call(
        body, out_shape=jax.ShapeDtypeStruct((rows, d), BF16), grid=(rows // ROW_TILE,),
        in_specs=[_row(d), _vec(d)], out_specs=_row(d), compiler_params=_params(("parallel",)), name=f"rms_fwd_{rows}",
    )(x, g)


def _accumulate(ref, value, step):
    @pl.when(step == 0)
    def _():
        ref[...] = value

    @pl.when(step > 0)
    def _():
        ref[...] += value


def _rms_bwd(x, g, dh, dres):
    rows, d = x.shape
    has_res = dres is not None

    def body(*refs):
        if has_res:
            x_ref, g_ref, dh_ref, dres_ref, dx_ref, dxb_ref, dg_ref = refs
        else:
            x_ref, g_ref, dh_ref, dx_ref, dxb_ref, dg_ref = refs
        xv = x_ref[...]
        dhv = dh_ref[...]
        rstd = lax.rsqrt(jnp.mean(xv * xv, axis=-1, keepdims=True) + RMS_EPS)
        xhat = xv * rstd
        gd = dhv * g_ref[...]
        dx = rstd * (gd - xhat * jnp.mean(gd * xhat, axis=-1, keepdims=True))
        if has_res:
            dx = dx + dres_ref[...]
        dx_ref[...] = dx
        dxb_ref[...] = dx.astype(BF16)
        _accumulate(dg_ref, jnp.sum(dhv * xhat, axis=0, keepdims=True), pl.program_id(0))

    in_specs = [_row(d), _vec(d), _row(d)] + ([_row(d)] if has_res else [])
    args = (x, g, dh) + ((dres,) if has_res else ())
    return pl.pallas_call(
        body,
        out_shape=(jax.ShapeDtypeStruct((rows, d), F32), jax.ShapeDtypeStruct((rows, d), BF16), jax.ShapeDtypeStruct((1, d), F32)),
        grid=(rows // ROW_TILE,), in_specs=in_specs, out_specs=(_row(d), _row(d), _vec(d)),
        compiler_params=_params(("arbitrary",)), name=f"rms_bwd_{rows}_{int(has_res)}",
    )(*args)


def _loss_head(x, g, target):
    rows, d = x.shape

    def body(x_ref, g_ref, t_ref, loss_ref, dx_ref, dxb_ref, dg_ref):
        xv = x_ref[...]
        gv = g_ref[...]
        rstd = lax.rsqrt(jnp.mean(xv * xv, axis=-1, keepdims=True) + RMS_EPS)
        xhat = xv * rstd
        err = xhat * gv - t_ref[...]
        part = 0.5 * jnp.sum(jnp.mean(err * err, axis=-1, keepdims=True), axis=0, keepdims=True)
        dy = err * (1.0 / d)
        gd = dy * gv
        dx = rstd * (gd - xhat * jnp.mean(gd * xhat, axis=-1, keepdims=True))
        dx_ref[...] = dx
        dxb_ref[...] = dx.astype(BF16)
        step = pl.program_id(0)
        _accumulate(loss_ref, jnp.broadcast_to(part, (1, LANE)), step)
        _accumulate(dg_ref, jnp.sum(dy * xhat, axis=0, keepdims=True), step)

    return pl.pallas_call(
        body,
        out_shape=(jax.ShapeDtypeStruct((1, LANE), F32), jax.ShapeDtypeStruct((rows, d), F32),
                   jax.ShapeDtypeStruct((rows, d), BF16), jax.ShapeDtypeStruct((1, d), F32)),
        grid=(rows // ROW_TILE,), in_specs=[_row(d), _vec(d), _row(d)],
        out_specs=(_vec(LANE), _row(d), _row(d), _vec(d)),
        compiler_params=_params(("arbitrary",)), name="loss_head",
    )(x, g, target)


def _sigmoid(x):
    return 1.0 / (1.0 + jnp.exp(-x))


def _ln_silu_fwd(c, ln_g, ln_b):
    rows, ch = c.shape

    def body(c_ref, g_ref, b_ref, o_ref):
        cv = c_ref[...]
        mu = jnp.mean(cv, axis=-1, keepdims=True)
        cen = cv - mu
        rstd = lax.rsqrt(jnp.mean(cen * cen, axis=-1, keepdims=True) + LN_EPS)
        y = cen * rstd * g_ref[...] + b_ref[...]
        o_ref[...] = (y * _sigmoid(y)).astype(BF16)

    return pl.pallas_call(
        body, out_shape=jax.ShapeDtypeStruct((rows, ch), BF16), grid=(rows // ROW_TILE,),
        in_specs=[_row(ch), _vec(ch), _vec(ch)], out_specs=_row(ch), compiler_params=_params(("parallel",)), name="ln_silu_fwd",
    )(c, ln_g, ln_b)


def _ln_silu_bwd(dmix, c, ln_g, ln_b):
    rows, ch = c.shape

    def body(dm_ref, c_ref, g_ref, b_ref, dc_ref, dg_ref, db_ref):
        cv = c_ref[...]
        gv = g_ref[...]
        mu = jnp.mean(cv, axis=-1, keepdims=True)
        cen = cv - mu
        rstd = lax.rsqrt(jnp.mean(cen * cen, axis=-1, keepdims=True) + LN_EPS)
        xhat = cen * rstd
        y = xhat * gv + b_ref[...]
        sg = _sigmoid(y)
        dy = dm_ref[...] * (sg * (1.0 + y * (1.0 - sg)))
        dxh = dy * gv
        dc = rstd * (dxh - jnp.mean(dxh, axis=-1, keepdims=True) - xhat * jnp.mean(dxh * xhat, axis=-1, keepdims=True))
        dc_ref[...] = dc
        step = pl.program_id(0)
        _accumulate(dg_ref, jnp.sum(dy * xhat, axis=0, keepdims=True), step)
        _accumulate(db_ref, jnp.sum(dy, axis=0, keepdims=True), step)

    return pl.pallas_call(
        body,
        out_shape=(jax.ShapeDtypeStruct((rows, ch), F32), jax.ShapeDtypeStruct((1, ch), F32), jax.ShapeDtypeStruct((1, ch), F32)),
        grid=(rows // ROW_TILE,), in_specs=[_row(ch), _row(ch), _vec(ch), _vec(ch)],
        out_specs=(_row(ch), _vec(ch), _vec(ch)), compiler_params=_params(("arbitrary",)), name="ln_silu_bwd",
    )(dmix, c, ln_g, ln_b)


def _col(rows, cb):
    return pl.BlockSpec((rows, cb), lambda j: (0, j))


def _conv_glu_fwd(pa, pg, ba, bg, w_dw, b_dw):
    seq, ch = pa.shape
    cb = LANE
    n_chunks = seq // CHUNK

    def body(pa_ref, pg_ref, ba_ref, bg_ref, w_ref, b_ref, v_ref, c_ref, vpad_ref):
        vpad_ref[0:CONV_PAD, :] = jnp.zeros((CONV_PAD, cb), F32)

        def glu(r, carry):
            r0 = pl.multiple_of(r * CHUNK, CHUNK)
            rows = pl.ds(r0, CHUNK)
            v = (pa_ref[rows, :] + ba_ref[...]) * _sigmoid(pg_ref[rows, :] + bg_ref[...])
            v_ref[rows, :] = v
            vpad_ref[pl.ds(r0 + CONV_PAD, CHUNK), :] = v
            return carry

        lax.fori_loop(0, n_chunks, glu, 0)

        def conv(r, carry):
            r0 = pl.multiple_of(r * CHUNK, CHUNK)
            win = vpad_ref[pl.ds(r0, CHUNK + CONV_PAD), :]
            acc = jnp.broadcast_to(b_ref[...], (CHUNK, cb))
            for t in range(CONV_W):
                off = CONV_PAD - (CONV_W - 1) + t
                acc = acc + w_ref[t:t + 1, :] * win[off:off + CHUNK, :]
            c_ref[pl.ds(r0, CHUNK), :] = acc
            return carry

        lax.fori_loop(0, n_chunks, conv, 0)

    return pl.pallas_call(
        body,
        out_shape=(jax.ShapeDtypeStruct((seq, ch), F32), jax.ShapeDtypeStruct((seq, ch), F32)),
        grid=(ch // cb,),
        in_specs=[_col(seq, cb), _col(seq, cb), _col(1, cb), _col(1, cb), _col(CONV_PAD, cb), _col(1, cb)],
        out_specs=(_col(seq, cb), _col(seq, cb)),
        scratch_shapes=[pltpu.VMEM((seq + CONV_PAD, cb), F32)],
        compiler_params=_params(("parallel",)), name="conv_glu_fwd",
    )(pa, pg, ba, bg, w_dw, b_dw)


def _conv_glu_bwd(dc, v, pa, pg, ba, bg, w_dw):
    seq, ch = dc.shape
    cb = LANE
    n_chunks = seq // CHUNK

    def body(dc_ref, v_ref, pa_ref, pg_ref, ba_ref, bg_ref, w_ref, da_ref, dg_ref, dw_ref, dbdw_ref, dba_ref, dbg_ref,
             dcpad_ref, vpad_ref):
        vpad_ref[0:CONV_PAD, :] = jnp.zeros((CONV_PAD, cb), F32)
        dcpad_ref[seq:seq + CONV_PAD, :] = jnp.zeros((CONV_PAD, cb), F32)
        dw_ref[...] = jnp.zeros((CONV_PAD, cb), F32)
        dbdw_ref[...] = jnp.zeros((1, cb), F32)
        dba_ref[...] = jnp.zeros((1, cb), F32)
        dbg_ref[...] = jnp.zeros((1, cb), F32)

        def fill(r, carry):
            r0 = pl.multiple_of(r * CHUNK, CHUNK)
            vpad_ref[pl.ds(r0 + CONV_PAD, CHUNK), :] = v_ref[pl.ds(r0, CHUNK), :]
            dcpad_ref[pl.ds(r0, CHUNK), :] = dc_ref[pl.ds(r0, CHUNK), :]
            return carry

        lax.fori_loop(0, n_chunks, fill, 0)

        def step(r, carry):
            r0 = pl.multiple_of(r * CHUNK, CHUNK)
            rows = pl.ds(r0, CHUNK)
            dwin = dcpad_ref[pl.ds(r0, CHUNK + CONV_PAD), :]
            vwin = vpad_ref[pl.ds(r0, CHUNK + CONV_PAD), :]
            dcur = dwin[0:CHUNK, :]
            dv = jnp.zeros((CHUNK, cb), F32)
            for t in range(CONV_W):
                fwd_off = CONV_W - 1 - t
                dv = dv + w_ref[t:t + 1, :] * dwin[fwd_off:fwd_off + CHUNK, :]
                off = CONV_PAD - (CONV_W - 1) + t
                dw_ref[t:t + 1, :] += jnp.sum(dcur * vwin[off:off + CHUNK, :], axis=0, keepdims=True)
            a = pa_ref[rows, :] + ba_ref[...]
            sg = _sigmoid(pg_ref[rows, :] + bg_ref[...])
            da = dv * sg
            dgate = dv * a * sg * (1.0 - sg)
            da_ref[rows, :] = da.astype(BF16)
            dg_ref[rows, :] = dgate.astype(BF16)
            dbdw_ref[...] += jnp.sum(dcur, axis=0, keepdims=True)
            dba_ref[...] += jnp.sum(da, axis=0, keepdims=True)
            dbg_ref[...] += jnp.sum(dgate, axis=0, keepdims=True)
            return carry

        lax.fori_loop(0, n_chunks, step, 0)

    return pl.pallas_call(
        body,
        out_shape=(jax.ShapeDtypeStruct((seq, ch), BF16), jax.ShapeDtypeStruct((seq, ch), BF16),
                   jax.ShapeDtypeStruct((CONV_PAD, ch), F32), jax.ShapeDtypeStruct((1, ch), F32),
                   jax.ShapeDtypeStruct((1, ch), F32), jax.ShapeDtypeStruct((1, ch), F32)),
        grid=(ch // cb,),
        in_specs=[_col(seq, cb), _col(seq, cb), _col(seq, cb), _col(seq, cb), _col(1, cb), _col(1, cb), _col(CONV_PAD, cb)],
        out_specs=(_col(seq, cb), _col(seq, cb), _col(CONV_PAD, cb), _col(1, cb), _col(1, cb), _col(1, cb)),
        scratch_shapes=[pltpu.VMEM((seq + CONV_PAD, cb), F32), pltpu.VMEM((seq + CONV_PAD, cb), F32)],
        compiler_params=_params(("parallel",)), name="conv_glu_bwd",
    )(dc, v, pa, pg, ba, bg, w_dw)


def _ffn_conv(pad_ref, r0, w_ref, b_ref, cb):
    win = pad_ref[pl.ds(r0, CHUNK + FFN_PAD), :]
    y = jnp.broadcast_to(b_ref[...], (CHUNK, cb))
    for t in range(FFN_CONV_W):
        off = FFN_PAD - (FFN_CONV_W - 1) + t
        y = y + w_ref[t:t + 1, :] * win[off:off + CHUNK, :]
    return y, win


def _ffn_mid_fwd(ug, uv, wg, wv, bg, bv):
    seq, ch = ug.shape
    cb = _tile(ch, 256)
    n_chunks = seq // CHUNK

    def body(ug_ref, uv_ref, wg_ref, wv_ref, bg_ref, bv_ref, z_ref, gpad_ref, vpad_ref):
        gpad_ref[0:FFN_PAD, :] = jnp.zeros((FFN_PAD, cb), F32)
        vpad_ref[0:FFN_PAD, :] = jnp.zeros((FFN_PAD, cb), F32)

        def fill(r, carry):
            r0 = pl.multiple_of(r * CHUNK, CHUNK)
            gpad_ref[pl.ds(r0 + FFN_PAD, CHUNK), :] = ug_ref[pl.ds(r0, CHUNK), :]
            vpad_ref[pl.ds(r0 + FFN_PAD, CHUNK), :] = uv_ref[pl.ds(r0, CHUNK), :]
            return carry

        lax.fori_loop(0, n_chunks, fill, 0)

        def step(r, carry):
            r0 = pl.multiple_of(r * CHUNK, CHUNK)
            yg, _ = _ffn_conv(gpad_ref, r0, wg_ref, bg_ref, cb)
            yv, _ = _ffn_conv(vpad_ref, r0, wv_ref, bv_ref, cb)
            z_ref[pl.ds(r0, CHUNK), :] = (yg * _sigmoid(yg) * yv).astype(BF16)
            return carry

        lax.fori_loop(0, n_chunks, step, 0)

    return pl.pallas_call(
        body, out_shape=jax.ShapeDtypeStruct((seq, ch), BF16), grid=(ch // cb,),
        in_specs=[_col(seq, cb), _col(seq, cb), _col(FFN_PAD, cb), _col(FFN_PAD, cb), _col(1, cb), _col(1, cb)],
        out_specs=_col(seq, cb),
        scratch_shapes=[pltpu.VMEM((seq + FFN_PAD, cb), F32), pltpu.VMEM((seq + FFN_PAD, cb), F32)],
        compiler_params=_params(("parallel",)), name="ffn_mid_fwd",
    )(ug, uv, wg, wv, bg, bv)


def _ffn_mid_bwd(ug, uv, dz, wg, wv, bg, bv):
    seq, ch = ug.shape
    cb = _tile(ch, 256)
    n_chunks = seq // CHUNK

    def body(ug_ref, uv_ref, dz_ref, wg_ref, wv_ref, bg_ref, bv_ref, dug_ref, duv_ref, dwg_ref, dwv_ref, dbg_ref, dbv_ref,
             gpad_ref, vpad_ref, dyg_ref, dyv_ref):
        gpad_ref[0:FFN_PAD, :] = jnp.zeros((FFN_PAD, cb), F32)
        vpad_ref[0:FFN_PAD, :] = jnp.zeros((FFN_PAD, cb), F32)
        dyg_ref[seq:seq + FFN_PAD, :] = jnp.zeros((FFN_PAD, cb), F32)
        dyv_ref[seq:seq + FFN_PAD, :] = jnp.zeros((FFN_PAD, cb), F32)
        dwg_ref[...] = jnp.zeros((FFN_PAD, cb), F32)
        dwv_ref[...] = jnp.zeros((FFN_PAD, cb), F32)
        dbg_ref[...] = jnp.zeros((1, cb), F32)
        dbv_ref[...] = jnp.zeros((1, cb), F32)

        def fill(r, carry):
            r0 = pl.multiple_of(r * CHUNK, CHUNK)
            gpad_ref[pl.ds(r0 + FFN_PAD, CHUNK), :] = ug_ref[pl.ds(r0, CHUNK), :]
            vpad_ref[pl.ds(r0 + FFN_PAD, CHUNK), :] = uv_ref[pl.ds(r0, CHUNK), :]
            return carry

        lax.fori_loop(0, n_chunks, fill, 0)

        def grads_of_conv_out(r, carry):
            r0 = pl.multiple_of(r * CHUNK, CHUNK)
            rows = pl.ds(r0, CHUNK)
            yg, gwin = _ffn_conv(gpad_ref, r0, wg_ref, bg_ref, cb)
            yv, vwin = _ffn_conv(vpad_ref, r0, wv_ref, bv_ref, cb)
            dzv = dz_ref[rows, :]
            sg = _sigmoid(yg)
            dyg = dzv * yv * (sg * (1.0 + yg * (1.0 - sg)))
            dyv = dzv * yg * sg
            dyg_ref[rows, :] = dyg
            dyv_ref[rows, :] = dyv
            for t in range(FFN_CONV_W):
                off = FFN_PAD - (FFN_CONV_W - 1) + t
                dwg_ref[t:t + 1, :] += jnp.sum(dyg * gwin[off:off + CHUNK, :], axis=0, keepdims=True)
                dwv_ref[t:t + 1, :] += jnp.sum(dyv * vwin[off:off + CHUNK, :], axis=0, keepdims=True)
            dbg_ref[...] += jnp.sum(dyg, axis=0, keepdims=True)
            dbv_ref[...] += jnp.sum(dyv, axis=0, keepdims=True)
            return carry

        lax.fori_loop(0, n_chunks, grads_of_conv_out, 0)

        def grads_of_conv_in(r, carry):
            r0 = pl.multiple_of(r * CHUNK, CHUNK)
            gwin = dyg_ref[pl.ds(r0, CHUNK + FFN_PAD), :]
            vwin = dyv_ref[pl.ds(r0, CHUNK + FFN_PAD), :]
            dug = jnp.zeros((CHUNK, cb), F32)
            duv = jnp.zeros((CHUNK, cb), F32)
            for t in range(FFN_CONV_W):
                off = FFN_CONV_W - 1 - t
                dug = dug + wg_ref[t:t + 1, :] * gwin[off:off + CHUNK, :]
                duv = duv + wv_ref[t:t + 1, :] * vwin[off:off + CHUNK, :]
            dug_ref[pl.ds(r0, CHUNK), :] = dug.astype(BF16)
            duv_ref[pl.ds(r0, CHUNK), :] = duv.astype(BF16)
            return carry

        lax.fori_loop(0, n_chunks, grads_of_conv_in, 0)

    return pl.pallas_call(
        body,
        out_shape=(jax.ShapeDtypeStruct((seq, ch), BF16), jax.ShapeDtypeStruct((seq, ch), BF16),
                   jax.ShapeDtypeStruct((FFN_PAD, ch), F32), jax.ShapeDtypeStruct((FFN_PAD, ch), F32),
                   jax.ShapeDtypeStruct((1, ch), F32), jax.ShapeDtypeStruct((1, ch), F32)),
        grid=(ch // cb,),
        in_specs=[_col(seq, cb), _col(seq, cb), _col(seq, cb), _col(FFN_PAD, cb), _col(FFN_PAD, cb), _col(1, cb), _col(1, cb)],
        out_specs=(_col(seq, cb), _col(seq, cb), _col(FFN_PAD, cb), _col(FFN_PAD, cb), _col(1, cb), _col(1, cb)),
        scratch_shapes=[pltpu.VMEM((seq + FFN_PAD, cb), F32) for _ in range(4)],
        compiler_params=_params(("parallel",)), name="ffn_mid_bwd",
    )(ug, uv, dz, wg, wv, bg, bv)


def _dot(a, b, mode):
    return lax.dot_general(a.astype(BF16), b.astype(BF16), _DIMS[mode], preferred_element_type=F32)


def _head(h):
    return slice(h * HEAD_DIM, (h + 1) * HEAD_DIM)


def _mem_softmax(q, k):
    s = _dot(q, k, "nt") * ATT_SCALE
    e = jnp.exp(s - jnp.max(s, axis=-1, keepdims=True))
    return e / jnp.sum(e, axis=-1, keepdims=True)


def _memattn_fwd(pq, mk, mv):
    seq, w = pq.shape
    m = mk.shape[0]

    def body(q_ref, k_ref, v_ref, o_ref):
        for h in range(N_MEM_HEADS):
            p = _mem_softmax(q_ref[:, _head(h)], k_ref[:, _head(h)])
            o_ref[:, _head(h)] = _dot(p, v_ref[:, _head(h)], "nn").astype(BF16)

    full = pl.BlockSpec((m, w), lambda i: (0, 0))
    return pl.pallas_call(
        body, out_shape=jax.ShapeDtypeStruct((seq, w), BF16), grid=(seq // ROW_TILE,),
        in_specs=[_row(w), full, full], out_specs=_row(w), compiler_params=_params(("parallel",)), name="memattn_fwd",
    )(pq, mk, mv)


def _memattn_bwd(pq, mk, mv, dmo):
    seq, w = pq.shape
    m = mk.shape[0]

    def body(q_ref, k_ref, v_ref, do_ref, dq_ref, dk_ref, dv_ref):
        step = pl.program_id(0)

        @pl.when(step == 0)
        def _():
            dk_ref[...] = jnp.zeros((m, w), F32)
            dv_ref[...] = jnp.zeros((m, w), F32)

        for h in range(N_MEM_HEADS):
            q, k, v, do = q_ref[:, _head(h)], k_ref[:, _head(h)], v_ref[:, _head(h)], do_ref[:, _head(h)]
            p = _mem_softmax(q, k)
            dp = _dot(do, v, "nt")
            ds = p * (dp - jnp.sum(dp * p, axis=-1, keepdims=True))
            dq_ref[:, _head(h)] = (_dot(ds, k, "nn") * ATT_SCALE).astype(BF16)
            dk_ref[:, _head(h)] += _dot(ds, q, "tn") * ATT_SCALE
            dv_ref[:, _head(h)] += _dot(p, do, "tn")

    full = pl.BlockSpec((m, w), lambda i: (0, 0))
    return pl.pallas_call(
        body,
        out_shape=(jax.ShapeDtypeStruct((seq, w), BF16), jax.ShapeDtypeStruct((m, w), F32), jax.ShapeDtypeStruct((m, w), F32)),
        grid=(seq // ROW_TILE,), in_specs=[_row(w), full, full, _row(w)], out_specs=(_row(w), full, full),
        compiler_params=_params(("arbitrary",)), name="memattn_bwd",
    )(pq, mk, mv, dmo)


FWD_KEY_TILE = 512
BWD_KEY_TILE = 512


def _causal_t(s_t, q_blk, k_blk, key_tile):
    kpos = k_blk * key_tile + lax.broadcasted_iota(jnp.int32, (key_tile, ROW_TILE), 0)
    qpos = q_blk * ROW_TILE + lax.broadcasted_iota(jnp.int32, (key_tile, ROW_TILE), 1)
    return jnp.where(kpos <= qpos, s_t, NEG_BIG)


def _fox_fwd(q, k, v, cum_cols, cum_rows):
    seq, w = q.shape
    nq = seq // ROW_TILE

    def body(q_ref, k_ref, v_ref, cc_ref, cr_ref, o_ref, ob_ref, lse_ref):
        i = pl.program_id(1)
        qs = [q_ref[:, _head(h)].astype(BF16) for h in range(2)]
        crs = [cr_ref[0, 0, h:h + 1, :] for h in range(2)]

        def step(j, carry):
            rows = pl.ds(pl.multiple_of(j * FWD_KEY_TILE, FWD_KEY_TILE), FWD_KEY_TILE)
            new = []
            for h in range(2):
                m_run, l_run, acc = carry[h]
                s_t = _dot(k_ref[rows, _head(h)], qs[h], "nt") * ATT_SCALE + crs[h] - cc_ref[0, rows, h * HEAD_DIM:h * HEAD_DIM + 1]
                s_t = _causal_t(s_t, i, j, FWD_KEY_TILE)
                m_new = jnp.maximum(m_run, jnp.max(s_t, axis=0, keepdims=True))
                alpha = jnp.exp(m_run - m_new)
                p_t = jnp.exp(s_t - m_new)
                l_new = alpha * l_run + jnp.sum(p_t, axis=0, keepdims=True)
                new.append((m_new, l_new, alpha * acc + _dot(v_ref[rows, _head(h)], p_t, "tn")))
            return tuple(new)

        one_head = (jnp.full((1, ROW_TILE), NEG_BIG, F32), jnp.zeros((1, ROW_TILE), F32), jnp.zeros((HEAD_DIM, ROW_TILE), F32))
        res = lax.fori_loop(0, ((i + 1) * ROW_TILE + FWD_KEY_TILE - 1) // FWD_KEY_TILE, step, (one_head, one_head))
        o = jnp.concatenate([acc / l_run for _, l_run, acc in res], axis=0).T
        o_ref[...] = o
        ob_ref[...] = o.astype(BF16)
        lse_ref[...] = jnp.zeros((1, 1, 8, ROW_TILE), F32)
        for h in range(2):
            lse_ref[0, 0, h:h + 1, :] = res[h][0] + jnp.log(res[h][1])

    blk = pl.BlockSpec((ROW_TILE, LANE), lambda hp, i: (i, hp))
    full = pl.BlockSpec((seq, LANE), lambda hp, i: (0, hp))
    cols = pl.BlockSpec((1, seq, LANE), lambda hp, i: (hp, 0, 0))
    rows = pl.BlockSpec((1, 1, 8, ROW_TILE), lambda hp, i: (hp, i, 0, 0))
    return pl.pallas_call(
        body,
        out_shape=(jax.ShapeDtypeStruct((seq, w), F32), jax.ShapeDtypeStruct((seq, w), BF16),
                   jax.ShapeDtypeStruct((N_HEAD_PAIRS, nq, 8, ROW_TILE), F32)),
        grid=(N_HEAD_PAIRS, nq), in_specs=[blk, full, full, cols, rows], out_specs=(blk, blk, rows),
        compiler_params=_params(("parallel", "parallel")), name="fox_fwd",
    )(q, k, v, cum_cols, cum_rows)


def _fox_bwd(q, k, v, cum_cols, cum_rows, o, lse, do):
    seq, w = q.shape
    nq = seq // ROW_TILE
    nk = seq // BWD_KEY_TILE

    def body(q_ref, k_ref, v_ref, cc_ref, cr_ref, o_ref, lse_ref, do_ref, dq_ref, dk_ref, dv_ref, dcc_ref, dcr_ref):
        kj = pl.program_id(1)

        @pl.when(kj == 0)
        def _():
            dq_ref[...] = jnp.zeros((seq, LANE), F32)
            dcr_ref[...] = jnp.zeros((1, nq, 8, ROW_TILE), F32)

        ks = [k_ref[:, _head(h)].astype(BF16) for h in range(2)]
        vs = [v_ref[:, _head(h)].astype(BF16) for h in range(2)]
        ccs = [cc_ref[0, :, h * HEAD_DIM:h * HEAD_DIM + 1] for h in range(2)]
        ones = jnp.ones((8, HEAD_DIM), BF16)

        def step(i, carry):
            rows = pl.ds(pl.multiple_of(i * ROW_TILE, ROW_TILE), ROW_TILE)
            new = []
            for h in range(2):
                dk, dv, fold = carry[h]
                qh = q_ref[rows, _head(h)].astype(BF16)
                doh = do_ref[rows, _head(h)]
                s_t = _dot(ks[h], qh, "nt") * ATT_SCALE + cr_ref[0, i, h:h + 1, :] - ccs[h]
                p_t = jnp.exp(_causal_t(s_t, i, kj, BWD_KEY_TILE) - lse_ref[0, i, h:h + 1, :])
                dp_t = _dot(vs[h], doh, "nt")
                hi, mid, lo = _split3(doh * o_ref[rows, _head(h)])
                row_sum = lambda part: lax.dot_general(ones, part, _DIMS["nt"], preferred_element_type=F32)
                delta = ((row_sum(lo) + row_sum(mid)) + row_sum(hi))[0:1, :]
                ds_t = p_t * (dp_t - delta)
                dq_ref[rows, _head(h)] += _dot(ds_t, ks[h], "tn") * ATT_SCALE
                dcr_ref[0, i, h:h + 1, :] += jnp.sum(ds_t, axis=0, keepdims=True)
                new.append((dk + _dot(ds_t, qh, "nn") * ATT_SCALE, dv + _dot(p_t, doh, "nn"), fold + (ds_t[:, :LANE] + ds_t[:, LANE:])))
            return tuple(new)

        one_head = (jnp.zeros((BWD_KEY_TILE, HEAD_DIM), F32), jnp.zeros((BWD_KEY_TILE, HEAD_DIM), F32), jnp.zeros((BWD_KEY_TILE, LANE), F32))
        res = lax.fori_loop(kj * BWD_KEY_TILE // ROW_TILE, nq, step, (one_head, one_head))
        for h in range(2):
            dk_ref[:, _head(h)] = res[h][0]
            dv_ref[:, _head(h)] = res[h][1]
            dcc_ref[0, :, _head(h)] = jnp.broadcast_to(-jnp.sum(res[h][2], axis=-1, keepdims=True), (BWD_KEY_TILE, HEAD_DIM))

    blk = pl.BlockSpec((BWD_KEY_TILE, LANE), lambda hp, j: (j, hp))
    full = pl.BlockSpec((seq, LANE), lambda hp, j: (0, hp))
    cols = pl.BlockSpec((1, BWD_KEY_TILE, LANE), lambda hp, j: (hp, j, 0))
    rows = pl.BlockSpec((1, nq, 8, ROW_TILE), lambda hp, j: (hp, 0, 0, 0))
    return pl.pallas_call(
        body,
        out_shape=(jax.ShapeDtypeStruct((seq, w), F32), jax.ShapeDtypeStruct((seq, w), F32), jax.ShapeDtypeStruct((seq, w), F32),
                   jax.ShapeDtypeStruct((N_HEAD_PAIRS, seq, LANE), F32), jax.ShapeDtypeStruct((N_HEAD_PAIRS, nq, 8, ROW_TILE), F32)),
        grid=(N_HEAD_PAIRS, nk), in_specs=[full, blk, blk, cols, rows, full, rows, full],
        out_specs=(full, blk, blk, cols, rows),
        compiler_params=_params(("parallel", "arbitrary")), name="fox_bwd",
    )(q, k, v, cum_cols, cum_rows, o, lse, do)


def _split3(x):
    hi = x.astype(BF16)
    r1 = x - hi.astype(F32)
    mid = r1.astype(BF16)
    lo = (r1 - mid.astype(F32)).astype(BF16)
    return hi, mid, lo


def _tri_sum(tri, x):
    hi, mid, lo = _split3(x)
    dot = lambda p: lax.dot_general(tri, p, _DIMS["nn"], preferred_element_type=F32)
    return (dot(lo) + dot(mid)) + dot(hi)


def _tri(lower):
    r = lax.broadcasted_iota(jnp.int32, (LANE, LANE), 0)
    c = lax.broadcasted_iota(jnp.int32, (LANE, LANE), 1)
    return jnp.where((c <= r) if lower else (c >= r), 1.0, 0.0).astype(BF16)


def _fgate_fwd(fl, bf):
    seq, w = fl.shape
    nb = seq // LANE

    def body(f_ref, b_ref, cum_ref):
        tri = _tri(True)

        def step(i, carry):
            rows = pl.ds(pl.multiple_of(i * LANE, LANE), LANE)
            x = f_ref[rows, :] + b_ref[...]
            logsig = jnp.minimum(x, 0.0) - jnp.log(1.0 + jnp.exp(-jnp.abs(x)))
            cum = _tri_sum(tri, logsig) + carry
            cum_ref[rows, :] = cum
            return cum[LANE - 1:LANE, :]

        lax.fori_loop(0, nb, step, jnp.zeros((1, w), F32))

    return pl.pallas_call(body, out_shape=jax.ShapeDtypeStruct((seq, w), F32), compiler_params=_params(), name="fgate_fwd")(fl, bf)


def _fgate_bwd(fl, bf, dcum):
    seq, w = fl.shape
    nb = seq // LANE

    def body(f_ref, b_ref, dc_ref, df_ref, db_ref):
        tri = _tri(False)

        def step(i, carry):
            tail, db = carry
            rows = pl.ds(pl.multiple_of((nb - 1 - i) * LANE, LANE), LANE)
            suffix = _tri_sum(tri, dc_ref[rows, :]) + tail
            df = suffix * (1.0 - _sigmoid(f_ref[rows, :] + b_ref[...]))
            df_ref[rows, :] = df
            return suffix[0:1, :], db + jnp.sum(df, axis=0, keepdims=True)

        _, db = lax.fori_loop(0, nb, step, (jnp.zeros((1, w), F32), jnp.zeros((1, w), F32)))
        db_ref[...] = db

    return pl.pallas_call(
        body, out_shape=(jax.ShapeDtypeStruct((seq, w), F32), jax.ShapeDtypeStruct((1, w), F32)),
        compiler_params=_params(), name="fgate_bwd",
    )(fl, bf, dcum)


def _cum_layouts(cum):
    seq = cum.shape[0]
    nblk = seq // ROW_TILE
    heads = cum[:, :N_FOX_HEADS]
    cq = jnp.repeat(heads, HEAD_DIM, axis=1).reshape(seq, N_HEAD_PAIRS, LANE).transpose(1, 0, 2)
    ck = heads.T.reshape(N_HEAD_PAIRS, 2, nblk, ROW_TILE).transpose(0, 2, 1, 3)
    ck = jnp.pad(ck, ((0, 0), (0, 0), (0, 6), (0, 0)))
    return cq, ck


def _dcum_from_layouts(dcq, dck):
    seq = dcq.shape[1]
    key_side = dck[:, :, :2, :].transpose(0, 2, 1, 3).reshape(N_FOX_HEADS, seq).T
    query_side = dcq[:, :, ::HEAD_DIM].transpose(1, 0, 2).reshape(seq, N_FOX_HEADS)
    return jnp.pad(key_side + query_side, ((0, 0), (0, F_PAD - N_FOX_HEADS)))


def _local_step(x, mem, target, w, arrive, emit):
    vec = lambda a: a.reshape(1, -1)
    mem_n = _rms_fwd(mem, vec(w["g_mem"]))
    saved = []
    shared = None
    for l in range(N_LAYERS):
        arrive(l, x, "mixer")
        s = {"x_in": x}
        s["h"] = h = _rms_fwd(x, vec(w["g_mix"][l]))
        s["mk"] = mk = _mm(mem_n, w["w_mk"][l])
        s["mv"] = mv = _mm(mem_n, w["w_mv"][l])
        if l < N_A:
            s["pa"] = pa = _mm(h, w["w_a"][l])
            s["pg"] = pg = _mm(h, w["w_g"][l])
            s["pq"] = pq = _mm(h, w["w_qm"][l])
            s["v"], s["c"] = _conv_glu_fwd(pa, pg, vec(w["b_glu_a"][l]), vec(w["b_glu_g"][l]), w["w_dw_a"][l], vec(w["b_dw_a"][l]))
            s["mix"] = mix = _ln_silu_fwd(s["c"], vec(w["ln_g"][l]), vec(w["ln_b"][l]))
        else:
            if l == N_A:
                shared = {"x_in": x}
                shared["hk"] = hk = _rms_fwd(x, vec(w["g_kv"]))
                shared["k"] = _mm(hk, w["w_k"])
                shared["v"] = _mm(hk, w["w_v"])
                shared["fl"] = _mm(hk, w["w_f"])
                cum = _fgate_fwd(shared["fl"], w["b_f"])
                shared["cq"], shared["ck"] = _cum_layouts(cum)
            s["q"] = q = _mm(h, w["w_q"][l - N_A])
            s["pq"] = pq = _mm(h, w["w_qm"][l])
            s["o"], mix, s["lse"] = _fox_fwd(q, shared["k"], shared["v"], shared["cq"], shared["ck"])
            s["mix"] = mix
        s["mo"] = mo = _memattn_fwd(pq, mk, mv)
        x = _mm(mix, w["w_o_mix"][l], add=x)
        s["x_mid"] = x = _mm(mo, w["w_o_mem"][l], add=x)
        arrive(l, x, "ffn")
        s["hf"] = hf = _rms_fwd(x, vec(w["g_ffn"][l]))
        s["ug"] = ug = _mm(hf, w["w_up_g"][l])
        s["uv"] = uv = _mm(hf, w["w_up_v"][l])
        s["z"] = z = _ffn_mid_fwd(ug, uv, w["w_dw_f_g"][l], w["w_dw_f_v"][l], vec(w["b_dw_f_g"][l]), vec(w["b_dw_f_v"][l]))
        x = _mm(z, w["w_down"][l], add=x)
        saved.append(s)

    loss_row, dx, dxb, dg_final = _loss_head(x, vec(w["g_final"]), target)
    loss = loss_row[0, 0]

    g = {n: [None] * N_LAYERS for n in ("g_mix", "w_mem_kv", "w_out", "g_ffn", "w_up", "w_dw_f", "b_dw_f", "w_down")}
    for n in ("w_in_a", "b_glu", "w_dw_a", "b_dw_a", "ln_g", "ln_b", "w_in_b"):
        g[n] = [None] * N_A
    one = lambda arr: ([arr], 0)
    after_start = lambda started, grad: grad if started is None else grad + started[0, 0].astype(BF16)
    g["g_final"] = [one(dg_final[0])]
    dmem_n = None
    dk_sum = dv_sum = dcq_sum = dck_sum = None
    for l in reversed(range(N_LAYERS)):
        s = saved[l]
        dz = _mm(dxb, w["w_down"][l], "nt")
        g["w_down"][l] = one(_mm(s["z"], dxb, "tn", BF16))
        dug, duv, dwg, dwv, dbg, dbv = _ffn_mid_bwd(s["ug"], s["uv"], dz, w["w_dw_f_g"][l], w["w_dw_f_v"][l],
                                                    vec(w["b_dw_f_g"][l]), vec(w["b_dw_f_v"][l]))
        g["w_up"][l] = ([_mm(s["hf"], dug, "tn", BF16), _mm(s["hf"], duv, "tn", BF16)], 1)
        g["w_dw_f"][l] = ([dwg[:FFN_CONV_W], dwv[:FFN_CONV_W]], 1)
        g["b_dw_f"][l] = ([dbg[0], dbv[0]], 0)
        dhf = _mm(duv, w["w_up_v"][l], "nt", add=_mm(dug, w["w_up_g"][l], "nt"))
        dx, dxb, dg_ffn = _rms_bwd(s["x_mid"], vec(w["g_ffn"][l]), dhf, dx)
        g["g_ffn"][l] = one(dg_ffn[0])
        dxb = after_start(emit(l, g, "ffn"), dxb)
        dmix = _mm(dxb, w["w_o_mix"][l], "nt")
        dmo = _mm(dxb, w["w_o_mem"][l], "nt")
        g["w_out"][l] = ([_mm(s["mix"], dxb, "tn", BF16), _mm(s["mo"], dxb, "tn", BF16)], 0)
        dpq, dmk, dmv = _memattn_bwd(s["pq"], s["mk"], s["mv"], dmo)
        g["w_mem_kv"][l] = ([_mm(mem_n, dmk, "tn", BF16), _mm(mem_n, dmv, "tn", BF16)], 1)
        dmem_n = _mm(dmk, w["w_mk"][l], "nt", add=dmem_n)
        dmem_n = _mm(dmv, w["w_mv"][l], "nt", add=dmem_n)
        if l < N_A:
            dc, dlng, dlnb = _ln_silu_bwd(dmix, s["c"], vec(w["ln_g"][l]), vec(w["ln_b"][l]))
            da, dgate, dwdw, dbdw, dba, dbg2 = _conv_glu_bwd(dc, s["v"], s["pa"], s["pg"], vec(w["b_glu_a"][l]),
                                                              vec(w["b_glu_g"][l]), w["w_dw_a"][l])
            g["ln_g"][l], g["ln_b"][l], g["b_dw_a"][l] = one(dlng[0]), one(dlnb[0]), one(dbdw[0])
            g["w_dw_a"][l] = ([dwdw[:CONV_W]], 1)
            g["b_glu"][l] = ([dba[0], dbg2[0]], 0)
            g["w_in_a"][l] = ([_mm(s["h"], da, "tn", BF16), _mm(s["h"], dgate, "tn", BF16), _mm(s["h"], dpq, "tn", BF16)], 1)
            dh = _mm(da, w["w_a"][l], "nt")
            dh = _mm(dgate, w["w_g"][l], "nt", add=dh)
            dh = _mm(dpq, w["w_qm"][l], "nt", add=dh)
        else:
            dq, dk, dv, dcq, dck = _fox_bwd(s["q"], shared["k"], shared["v"], shared["cq"], shared["ck"], s["o"], s["lse"], dmix)
            dk_sum = dk if dk_sum is None else dk_sum + dk
            dv_sum = dv if dv_sum is None else dv_sum + dv
            dcq_sum = dcq if dcq_sum is None else dcq_sum + dcq
            dck_sum = dck if dck_sum is None else dck_sum + dck
            g["w_in_b"][l - N_A] = ([_mm(s["h"], dq, "tn", BF16), _mm(s["h"], dpq, "tn", BF16)], 1)
            dh = _mm(dq, w["w_q"][l - N_A], "nt")
            dh = _mm(dpq, w["w_qm"][l], "nt", add=dh)
        dx, dxb, dg_mix = _rms_bwd(s["x_in"], vec(w["g_mix"][l]), dh, dx)
        g["g_mix"][l] = one(dg_mix[0])
        if l == N_A:
            df, dbf = _fgate_bwd(shared["fl"], w["b_f"], _dcum_from_layouts(dcq_sum, dck_sum))
            hk = shared["hk"]
            g["w_kvf"] = [([_mm(hk, dk_sum, "tn", BF16), _mm(hk, dv_sum, "tn", BF16), _mm(hk, df, "tn", BF16)[:, :N_FOX_HEADS]], 1)]
            g["b_f"] = [one(dbf[0, :N_FOX_HEADS])]
            dhk = _mm(dk_sum, w["w_k"], "nt")
            dhk = _mm(dv_sum, w["w_v"], "nt", add=dhk)
            dhk = _mm(df, w["w_f"], "nt", add=dhk)
            dx, dxb, dg_kv = _rms_bwd(shared["x_in"], vec(w["g_kv"]), dhk, dx)
            g["g_kv"] = [one(dg_kv[0])]
        dxb = after_start(emit(l, g, "mixer"), dxb)
    _, _, dg_mem = _rms_bwd(mem, vec(w["g_mem"]), dmem_n, None)
    g["g_mem"] = [one(dg_mem[0])]
    return loss, dx, g


SHARDED = {
    "w_in_a": ((N_A, D_MODEL, 2 * CONV_CH + MEM_W), 2),
    "b_glu": ((N_A, 2 * CONV_CH), 1),
    "w_dw_a": ((N_A, CONV_W, CONV_CH), 2),
    "b_dw_a": ((N_A, CONV_CH), 1),
    "ln_g": ((N_A, CONV_CH), 1),
    "ln_b": ((N_A, CONV_CH), 1),
    "w_kvf": ((D_MODEL, 2 * CONV_CH + N_FOX_HEADS), 1),
    "w_in_b": ((N_LAYERS - N_A, D_MODEL, CONV_CH + MEM_W), 1),
    "w_mem_kv": ((N_LAYERS, D_MODEL, 2 * MEM_W), 1),
    "w_out": ((N_LAYERS, D_MODEL, D_MODEL), 1),
    "w_up": ((N_LAYERS, D_MODEL, 2 * D_FF), 2),
    "w_dw_f": ((N_LAYERS, FFN_CONV_W, 2 * D_FF), 2),
    "w_down": ((N_LAYERS, D_FF, D_MODEL), 1),
}
REPLICATED = {
    "g_mix": (N_LAYERS, D_MODEL), "g_kv": (D_MODEL,), "b_f": (N_FOX_HEADS,), "g_mem": (D_MODEL,),
    "g_ffn": (N_LAYERS, D_MODEL), "b_dw_f": (N_LAYERS, 2 * D_FF), "g_final": (D_MODEL,),
}
WEIGHT_ORDER = ["g_mix", "w_in_a", "b_glu", "w_dw_a", "b_dw_a", "ln_g", "ln_b", "g_kv", "w_kvf", "b_f", "w_in_b", "g_mem",
                "w_mem_kv", "w_out", "g_ffn", "w_up", "w_dw_f", "b_dw_f", "w_down", "g_final"]


def _shard_shape(name):
    shape, axis = SHARDED[name]
    return tuple(d // N_CHIPS if i == axis else d for i, d in enumerate(shape))


def _size(shape):
    n = 1
    for d in shape:
        n *= d
    return n


VECTOR_LEAVES = ("b_glu", "w_dw_a", "b_dw_a", "ln_g", "ln_b", "w_dw_f")
_HALF_ROW_TILE = 16
_VECTOR_ELEMS = sum(_size(_shard_shape(n)) for n in VECTOR_LEAVES)
VECTOR_ROWS = -(-_VECTOR_ELEMS // (FLAT_COLS * 2 * _HALF_ROW_TILE)) * 2 * _HALF_ROW_TILE
REP_ELEMS = sum(_size(s) for s in REPLICATED.values())
REP_ROWS = 8
REP_COLS = -(-REP_ELEMS // (REP_ROWS * LANE)) * LANE


def _flatten_vectors(parts, dtype):
    flat = jnp.concatenate([p.reshape(-1).astype(dtype) for p in parts])
    return jnp.pad(flat, (0, VECTOR_ROWS * FLAT_COLS - _VECTOR_ELEMS)).reshape(2, VECTOR_ROWS // 2, FLAT_COLS)


def _unflatten_vectors(flat, lead=()):
    flat = flat.reshape(lead + (-1,))
    out, off = {}, 0
    for n in VECTOR_LEAVES:
        shp = _shard_shape(n)
        out[n] = flat[..., off:off + _size(shp)].reshape(lead + shp)
        off += _size(shp)
    return out


def _span(parts, cat_axis, sel_axis, lo, hi):
    if sel_axis != cat_axis:
        return jnp.concatenate([lax.slice_in_dim(p, lo, hi, axis=sel_axis) for p in parts], axis=cat_axis)
    taken, off = [], 0
    for p in parts:
        n = p.shape[cat_axis]
        a, b = max(lo, off), min(hi, off + n)
        if a < b:
            taken.append(lax.slice_in_dim(p, a - off, b - off, axis=cat_axis))
        off += n
    return taken[0] if len(taken) == 1 else jnp.concatenate(taken, axis=cat_axis)


FFN_LEAVES = 2
LAYER_LEAVES = tuple(
    ([("w_in_a", l)] if l < N_A else ([("w_kvf", None)] if l == N_A else []) + [("w_in_b", l - N_A)])
    + [("w_mem_kv", l), ("w_out", l), ("w_up", l), ("w_down", l)]
    for l in range(N_LAYERS))


def _layer_halves(weights, leaves):
    out = []
    for name, idx in leaves:
        shard = weights[name] if idx is None else weights[name][idx]
        rows, cols = shard.shape
        assert (rows // 2) % _HALF_ROW_TILE == 0, name
        out.append(shard.astype(BF16).reshape(2, rows // 2, cols))
    return out


def _layer_pieces(l, gathered, vectors):
    parts = {}
    for (name, idx), g in zip(LAYER_LEAVES[l], gathered):
        if g is not None:
            parts[name] = [g[t].reshape(-1, g.shape[-1]) for t in range(N_CHIPS)]

    def cut(name, sel_axis, lo, hi):
        axis = SHARDED[name][1] - (len(SHARDED[name][0]) - 2)
        return _span(parts[name], axis, sel_axis, lo, hi)

    def vcut(name, idx, sel_axis, lo, hi):
        axis = SHARDED[name][1] - 1
        return _span([vectors[name][t, idx] for t in range(N_CHIPS)], axis, sel_axis, lo, hi)

    def pad_rows(arr, rows):
        return jnp.pad(arr, ((0, rows - arr.shape[0]), (0, 0)))

    w = {}
    if "w_up" in parts:
        w["w_up_g"] = cut("w_up", 1, 0, D_FF)
        w["w_up_v"] = cut("w_up", 1, D_FF, 2 * D_FF)
        w["w_dw_f_g"] = pad_rows(vcut("w_dw_f", l, 1, 0, D_FF), FFN_PAD)
        w["w_dw_f_v"] = pad_rows(vcut("w_dw_f", l, 1, D_FF, 2 * D_FF), FFN_PAD)
        w["w_down"] = cut("w_down", 0, 0, D_FF)
    if "w_out" not in parts:
        return w
    if l < N_A:
        w["w_a"] = cut("w_in_a", 1, 0, CONV_CH)
        w["w_g"] = cut("w_in_a", 1, CONV_CH, 2 * CONV_CH)
        w["w_qm"] = cut("w_in_a", 1, 2 * CONV_CH, 2 * CONV_CH + MEM_W)
        w["b_glu_a"] = vcut("b_glu", l, 0, 0, CONV_CH)
        w["b_glu_g"] = vcut("b_glu", l, 0, CONV_CH, 2 * CONV_CH)
        w["w_dw_a"] = pad_rows(vcut("w_dw_a", l, 1, 0, CONV_CH), CONV_PAD)
        for n in ("b_dw_a", "ln_g", "ln_b"):
            w[n] = vcut(n, l, 0, 0, CONV_CH)
    else:
        if l == N_A:
            w["w_k"] = cut("w_kvf", 1, 0, CONV_CH)
            w["w_v"] = cut("w_kvf", 1, CONV_CH, 2 * CONV_CH)
            w["w_f"] = jnp.pad(cut("w_kvf", 1, 2 * CONV_CH, 2 * CONV_CH + N_FOX_HEADS), ((0, 0), (0, F_PAD - N_FOX_HEADS)))
        w["w_q"] = cut("w_in_b", 1, 0, CONV_CH)
        w["w_qm"] = cut("w_in_b", 1, CONV_CH, CONV_CH + MEM_W)
    w["w_mk"] = cut("w_mem_kv", 1, 0, MEM_W)
    w["w_mv"] = cut("w_mem_kv", 1, MEM_W, 2 * MEM_W)
    w["w_o_mix"] = cut("w_out", 0, 0, CONV_CH)
    w["w_o_mem"] = cut("w_out", 0, CONV_CH, D_MODEL)
    return w


_PER_LAYER = ("w_a", "w_g", "w_q", "w_qm", "b_glu_a", "b_glu_g", "w_dw_a", "b_dw_a", "ln_g", "ln_b", "w_mk", "w_mv", "w_o_mix", "w_o_mem",
              "w_up_g", "w_up_v", "w_dw_f_g", "w_dw_f_v", "w_down")


def _weight_table(rep):
    w = dict(rep)
    w["b_f"] = jnp.pad(rep["b_f"], (0, F_PAD - N_FOX_HEADS)).reshape(1, F_PAD)
    w["b_dw_f_g"], w["b_dw_f_v"] = rep["b_dw_f"][:, :D_FF], rep["b_dw_f"][:, D_FF:]
    for n in _PER_LAYER:
        w[n] = {}
    return w


def _install_layer(w, l, pieces):
    for n, p in pieces.items():
        if n in _PER_LAYER:
            w[n][l - N_A if n == "w_q" else l] = p
        else:
            w[n] = p


def _shard_of(entry, name, t):
    shape, axis = SHARDED[name]
    pieces, cat_axis = entry
    axis -= len(shape) - len(pieces[0].shape)
    step = shape[SHARDED[name][1]] // N_CHIPS
    return _span(pieces, cat_axis, axis, t * step, (t + 1) * step)


def _layer_slabs(grads, leaves):
    out = []
    for name, idx in leaves:
        shards = [_shard_of(grads[name][0 if idx is None else idx], name, t).astype(BF16) for t in range(N_CHIPS)]
        rows, cols = shards[0].shape
        out.append(jnp.stack(shards).reshape(N_CHIPS, 2, rows // 2, cols))
    return out


def _vector_slabs(grads):
    def shard(name, t):
        return jnp.stack([_shard_of(entry, name, t) for entry in grads[name]])
    return jnp.stack([_flatten_vectors([shard(n, t) for n in VECTOR_LEAVES], BF16) for t in range(N_CHIPS)])


def _replicated_rows(grads):
    parts = [p.reshape(-1) for n in REPLICATED for pieces, _ in grads[n] for p in pieces]
    flat = jnp.concatenate(parts)
    return jnp.pad(flat, (0, REP_ROWS * REP_COLS - REP_ELEMS)).reshape(REP_ROWS, REP_COLS)


_ANY = pl.BlockSpec(memory_space=pl.ANY)


def _place():
    x, y, c = lax.axis_index("x"), lax.axis_index("y"), lax.axis_index("c")
    chips = [(1 - x, y), (x, 1 - y), (1 - x, 1 - y)]
    return x, y, c, chips


def _chip_index(chip):
    return 2 * chip[0] + chip[1]


def _gather_leaves(halves):
    n = len(halves)

    def body(*refs):
        w_refs, out_refs, (send_sems, recv_sems) = refs[:n], refs[n:2 * n], refs[2 * n:]
        x, y, c, chips = _place()
        me = 2 * x + y
        sibling = (x, y, 1 - c)

        def copy(a, sem, chip_idx, half, to, src=None):
            dst = out_refs[a].at[chip_idx, half]
            return pltpu.make_async_remote_copy(src_ref=dst if src is None else src, dst_ref=dst, send_sem=send_sems.at[6 * a + sem],
                                                recv_sem=recv_sems.at[6 * a + sem], device_id=to, device_id_type=MESH_ID)

        first = [copy(a, j, me, c, (*chip, c), src=w_refs[a].at[c]) for j, chip in enumerate(chips) for a in range(n)]
        for cp in first:
            cp.start()
        passed = []
        for j, chip in enumerate(chips):
            for a in range(n):
                copy(a, j, _chip_index(chip), c, (x, y, c)).wait_recv()
                passed.append(copy(a, 3 + j, _chip_index(chip), c, sibling))
                passed[-1].start()
        for j, chip in enumerate(chips):
            for a in range(n):
                copy(a, 3 + j, _chip_index(chip), 1 - c, (x, y, c)).wait_recv()
        for cp in first + passed:
            cp.wait_send()

    return pl.pallas_call(
        body, out_shape=[jax.ShapeDtypeStruct((N_CHIPS,) + h.shape, h.dtype) for h in halves],
        in_specs=[_ANY] * n, out_specs=[_ANY] * n,
        scratch_shapes=[pltpu.SemaphoreType.DMA((6 * n,)), pltpu.SemaphoreType.DMA((6 * n,))], name="gather_leaves",
    )(*halves)


_HBM = pl.BlockSpec(memory_space=pltpu.HBM)
_SEM = pl.BlockSpec(memory_space=pltpu.SEMAPHORE)


def _in_hbm(a):
    return pltpu.with_memory_space_constraint(a, pltpu.HBM)


def _gather_start(groups, after):
    flat = [h for g in groups for h in g]
    n, ng = len(flat), len(groups)

    def body(*refs):
        srcs, lands, sems = refs[1:1 + n], refs[1 + n:1 + 2 * n], refs[1 + 2 * n:1 + 2 * n + 2 * ng]
        token = refs[-1]
        x, y, c, chips = _place()
        me = 2 * x + y
        a = 0
        for gi, g in enumerate(groups):
            for k in range(len(g)):
                for j, chip in enumerate(chips):
                    pltpu.make_async_remote_copy(src_ref=srcs[a].at[c], dst_ref=lands[a].at[me, c], send_sem=sems[2 * gi].at[3 * k + j],
                                                 recv_sem=sems[2 * gi + 1].at[3 * k + j], device_id=(*chip, c), device_id_type=MESH_ID).start()
                a += 1
        token[...] = jnp.zeros_like(token)

    sem_shapes = [pltpu.SemaphoreType.DMA((3 * len(g),)) for g in groups for _ in range(2)]
    out = pl.pallas_call(
        body, name="gather_start",
        out_shape=sem_shapes + [pltpu.HBM(h.shape, h.dtype) for h in flat] + [pltpu.HBM((N_CHIPS,) + h.shape, h.dtype) for h in flat]
        + [jax.ShapeDtypeStruct((8, LANE), F32)],
        in_specs=[_ANY] + [_HBM] * (2 * n), out_specs=[_SEM] * (2 * ng) + [_HBM] * (2 * n) + [pl.BlockSpec(memory_space=pltpu.VMEM)],
        input_output_aliases={1 + i: 2 * ng + i for i in range(2 * n)},
        compiler_params=pltpu.CompilerParams(has_side_effects=pltpu.SideEffectType.DATAFLOW_SIDE_EFFECTING),
    )(after, *[_in_hbm(h) for h in flat], *[_in_hbm(lax.empty((N_CHIPS,) + h.shape, h.dtype)) for h in flat])
    sems, srcs, lands = out[:2 * ng], out[2 * ng:2 * ng + n], out[2 * ng + n:2 * ng + 2 * n]
    res, a = [], 0
    for gi, g in enumerate(groups):
        res.append((sems[2 * gi], sems[2 * gi + 1], list(srcs[a:a + len(g)]), list(lands[a:a + len(g)])))
        a += len(g)
    return res


def _gather_wait(send_sems, recv_sems, srcs, lands, after, name):
    n = len(srcs)

    def body(*refs):
        src_refs, land_refs, send, recv = refs[:n], refs[n:2 * n], refs[2 * n], refs[2 * n + 1]
        x, y, c, chips = _place()
        for k in range(n):
            for j, chip in enumerate(chips):
                cp = pltpu.make_async_remote_copy(src_ref=src_refs[k].at[c], dst_ref=land_refs[k].at[_chip_index(chip), c], send_sem=send.at[3 * k + j],
                                                  recv_sem=recv.at[3 * k + j], device_id=(*chip, c), device_id_type=MESH_ID)
                cp.wait_send()
                cp.wait_recv()

    out = pl.pallas_call(
        body, name=name, out_shape=[pltpu.HBM(a.shape, a.dtype) for a in list(srcs) + list(lands)],
        in_specs=[_HBM] * (2 * n) + [_SEM, _SEM, _ANY], out_specs=[_HBM] * (2 * n), input_output_aliases={i: i for i in range(2 * n)},
        compiler_params=pltpu.CompilerParams(has_side_effects=pltpu.SideEffectType.DATAFLOW_SIDE_EFFECTING),
    )(*srcs, *lands, send_sems, recv_sems, after)
    return list(out[n:])


def _forward_halves(lands, name):
    n = len(lands)

    def body(*refs):
        out_refs, (send_sems, recv_sems) = refs[n:2 * n], refs[2 * n:]
        x, y, c, chips = _place()

        def copy(a, j, half, to):
            blk = out_refs[a].at[_chip_index(chips[j]), half]
            return pltpu.make_async_remote_copy(src_ref=blk, dst_ref=blk, send_sem=send_sems.at[3 * a + j], recv_sem=recv_sems.at[3 * a + j],
                                                device_id=to, device_id_type=MESH_ID)

        sends = [copy(a, j, c, (x, y, 1 - c)) for a in range(n) for j in range(3)]
        for cp in sends:
            cp.start()
        for a in range(n):
            for j in range(3):
                copy(a, j, 1 - c, (x, y, c)).wait_recv()
        for cp in sends:
            cp.wait_send()

    return pl.pallas_call(
        body, out_shape=[jax.ShapeDtypeStruct(a.shape, a.dtype) for a in lands], in_specs=[_ANY] * n, out_specs=[_ANY] * n,
        input_output_aliases={i: i for i in range(n)},
        scratch_shapes=[pltpu.SemaphoreType.DMA((3 * n,)), pltpu.SemaphoreType.DMA((3 * n,))], name=name,
    )(*lands)


def _swap_halves(slabs):
    n = len(slabs)

    def body(*refs):
        g_refs, got_refs, (send_sem, recv_sem) = refs[:n], refs[n:2 * n], refs[2 * n:]
        x, y, c, _ = _place()
        copies = [pltpu.make_async_remote_copy(src_ref=g_refs[a].at[t, 1 - c], dst_ref=got_refs[a].at[t], send_sem=send_sem.at[N_CHIPS * a + t],
                                               recv_sem=recv_sem.at[N_CHIPS * a + t], device_id=(x, y, 1 - c), device_id_type=MESH_ID)
                  for a in range(n) for t in range(N_CHIPS)]
        for cp in copies:
            cp.start()
        for cp in copies:
            cp.wait()

    return pl.pallas_call(
        body, out_shape=[jax.ShapeDtypeStruct((g.shape[0],) + g.shape[2:], g.dtype) for g in slabs],
        in_specs=[_ANY] * n, out_specs=[_ANY] * n,
        scratch_shapes=[pltpu.SemaphoreType.DMA((N_CHIPS * n,)), pltpu.SemaphoreType.DMA((N_CHIPS * n,))], name="swap_halves",
    )(*slabs)


def _scatter_start(parts, name):
    n = len(parts)

    def body(*refs):
        srcs, lands, send, recv = refs[:n], refs[n:2 * n], refs[2 * n], refs[2 * n + 1]
        token = refs[-1]
        x, y, c, chips = _place()
        me = 2 * x + y
        for k in range(n):
            for j, chip in enumerate(chips):
                pltpu.make_async_remote_copy(src_ref=srcs[k].at[_chip_index(chip)], dst_ref=lands[k].at[me], send_sem=send.at[3 * k + j],
                                             recv_sem=recv.at[3 * k + j], device_id=(*chip, c), device_id_type=MESH_ID).start()
        token[...] = jnp.zeros_like(token)

    out = pl.pallas_call(
        body, name=name,
        out_shape=[pltpu.SemaphoreType.DMA((3 * n,)), pltpu.SemaphoreType.DMA((3 * n,))] + [pltpu.HBM(p.shape, p.dtype) for p in parts] * 2
        + [jax.ShapeDtypeStruct((8, LANE), F32)],
        in_specs=[_HBM] * (2 * n), out_specs=[_SEM, _SEM] + [_HBM] * (2 * n) + [pl.BlockSpec(memory_space=pltpu.VMEM)],
        input_output_aliases={i: 2 + i for i in range(2 * n)},
        compiler_params=pltpu.CompilerParams(has_side_effects=pltpu.SideEffectType.DATAFLOW_SIDE_EFFECTING),
    )(*[_in_hbm(p) for p in parts], *[_in_hbm(lax.empty(p.shape, p.dtype)) for p in parts])
    return (out[0], out[1], list(out[2:2 + n]), list(out[2 + n:2 + 2 * n])), out[-1]


def _scatter_wait(send_sems, recv_sems, srcs, lands, after, name):
    n = len(srcs)

    def body(*refs):
        src_refs, land_refs, send, recv = refs[:n], refs[n:2 * n], refs[2 * n], refs[2 * n + 1]
        x, y, c, chips = _place()
        for k in range(n):
            for j, chip in enumerate(chips):
                cp = pltpu.make_async_remote_copy(src_ref=src_refs[k].at[_chip_index(chip)], dst_ref=land_refs[k].at[_chip_index(chip)],
                                                  send_sem=send.at[3 * k + j], recv_sem=recv.at[3 * k + j], device_id=(*chip, c), device_id_type=MESH_ID)
                cp.wait_send()
                cp.wait_recv()

    out = pl.pallas_call(
        body, name=name, out_shape=[pltpu.HBM(a.shape, a.dtype) for a in list(srcs) + list(lands)],
        in_specs=[_HBM] * (2 * n) + [_SEM, _SEM, _ANY], out_specs=[_HBM] * (2 * n), input_output_aliases={i: i for i in range(2 * n)},
        compiler_params=pltpu.CompilerParams(has_side_effects=pltpu.SideEffectType.DATAFLOW_SIDE_EFFECTING),
    )(*srcs, *lands, send_sems, recv_sems, after)
    return list(out[n:])


def _sibling_halves(halves):
    n = len(halves)

    def body(*refs):
        h_refs, got_refs, (send_sem, recv_sem) = refs[:n], refs[n:2 * n], refs[2 * n:]
        x, y, c, _ = _place()
        copies = [pltpu.make_async_remote_copy(src_ref=h_refs[a], dst_ref=got_refs[a], send_sem=send_sem.at[a], recv_sem=recv_sem.at[a],
                                               device_id=(x, y, 1 - c), device_id_type=MESH_ID) for a in range(n)]
        for cp in copies:
            cp.start()
        for cp in copies:
            cp.wait()

    return pl.pallas_call(
        body, out_shape=[jax.ShapeDtypeStruct(h.shape, h.dtype) for h in halves], in_specs=[_ANY] * n, out_specs=[_ANY] * n,
        scratch_shapes=[pltpu.SemaphoreType.DMA((n,)), pltpu.SemaphoreType.DMA((n,))], name="sibling_halves",
    )(*halves)


def _gather_replicated(rows):
    m_per, n = rows.shape

    def body(x_ref, out_ref, send_sems, recv_sems, local_sem):
        x, y, c, chips = _place()
        me, sibling = (x, y, c), (x, y, 1 - c)

        def block(px, py, pc):
            return out_ref.at[pl.ds((4 * px + 2 * py + pc) * m_per, m_per), :]

        def copy(k, blk, to, src=None):
            return pltpu.make_async_remote_copy(src_ref=block(*blk) if src is None else src, dst_ref=block(*blk),
                                                send_sem=send_sems.at[k], recv_sem=recv_sems.at[k], device_id=to, device_id_type=MESH_ID)

        mine = pltpu.make_async_copy(x_ref, block(*me), local_sem)
        mine.start()
        first = [copy(0, me, sibling, src=x_ref)]
        first += [copy(1 + j, me, (*chip, c), src=x_ref) for j, chip in enumerate(chips)]
        for cp in first:
            cp.start()
        passed = [copy(4 + j, (*chip, c), sibling) for j, chip in enumerate(chips)]
        for j, chip in enumerate(chips):
            copy(1 + j, (*chip, c), me).wait_recv()
            passed[j].start()
        copy(0, sibling, me).wait_recv()
        for j, chip in enumerate(chips):
            copy(4 + j, (*chip, 1 - c), me).wait_recv()
        for cp in first + passed:
            cp.wait_send()
        mine.wait()

    vmem = pl.BlockSpec(memory_space=pltpu.VMEM)
    return pl.pallas_call(
        body, out_shape=jax.ShapeDtypeStruct((N_DEV * m_per, n), rows.dtype), in_specs=[vmem], out_specs=vmem,
        scratch_shapes=[pltpu.SemaphoreType.DMA((7,)), pltpu.SemaphoreType.DMA((7,)), pltpu.SemaphoreType.DMA],
        name="gather_replicated",
    )(rows)


_SUM_ROWS = 256


def _tile_rows(rows):
    if rows <= _SUM_ROWS:
        return rows
    best = None
    for r in range(16, _SUM_ROWS + 1, 16):
        if rows % r == 0:
            best = r
    assert best is not None, rows
    return best


def _add_pairs(slabs, got, core):
    n, _, rows, cols = slabs.shape
    tr = _tile_rows(rows)

    def body(core_ref, a_ref, b_ref, o_ref):
        o_ref[...] = (a_ref[0].astype(F32) + b_ref[...].astype(F32)).astype(BF16)

    spec = pl.BlockSpec((1, tr, cols), lambda t, i, core_ref: (t, i, 0))
    return pl.pallas_call(
        body, out_shape=jax.ShapeDtypeStruct((n, rows, cols), BF16),
        grid_spec=pltpu.PrefetchScalarGridSpec(
            num_scalar_prefetch=1, grid=(n, rows // tr),
            in_specs=[pl.BlockSpec((1, 1, tr, cols), lambda t, i, core_ref: (t, core_ref[0], i, 0)), spec], out_specs=spec),
        compiler_params=_params(("parallel", "parallel")), name=f"add_pairs_{rows}x{cols}",
    )(core, slabs, got)


def _sum_slabs(slabs):
    n, rows, cols = slabs.shape
    tr = _tile_rows(rows)

    def body(s_ref, o_ref):
        acc = s_ref[0].astype(F32)
        for t in range(1, n):
            acc = acc + s_ref[t].astype(F32)
        o_ref[...] = acc

    return pl.pallas_call(
        body, out_shape=jax.ShapeDtypeStruct((rows, cols), F32), grid=(rows // tr,),
        in_specs=[pl.BlockSpec((n, tr, cols), lambda i: (0, i, 0))], out_specs=pl.BlockSpec((tr, cols), lambda i: (i, 0)),
        compiler_params=_params(("parallel",)), name=f"sum_slabs_{n}x{rows}x{cols}",
    )(slabs)


_ADAM_BLOCK_BYTES = 2 * 1024 * 1024


def _adamw(w, g, m, v):
    shape = w.shape
    cols = shape[-1]
    rows = _size(shape) // cols
    tr = rows
    if rows * cols * 4 > _ADAM_BLOCK_BYTES:
        for r in range(8, rows, 8):
            if rows % r == 0 and r * cols * 4 <= _ADAM_BLOCK_BYTES:
                tr = r

    def body(w_ref, g_ref, m_ref, v_ref, d_ref, nm_ref, nv_ref):
        gv = g_ref[...]
        nm = ADAM_B1 * m_ref[...] + (1.0 - ADAM_B1) * gv
        nv = ADAM_B2 * v_ref[...] + (1.0 - ADAM_B2) * jnp.square(gv)
        m_hat = nm / (1.0 - ADAM_B1 ** ADAM_STEP)
        v_hat = nv / (1.0 - ADAM_B2 ** ADAM_STEP)
        d_ref[...] = -ADAM_LR * (m_hat / (jnp.sqrt(v_hat) + ADAM_EPS) + ADAM_WD * w_ref[...])
        nm_ref[...] = nm
        nv_ref[...] = nv

    spec = pl.BlockSpec((tr, cols), lambda i: (i, 0))
    out = pl.pallas_call(
        body, out_shape=tuple(jax.ShapeDtypeStruct((rows, cols), F32) for _ in range(3)), grid=(rows // tr,),
        in_specs=[spec] * 4, out_specs=(spec,) * 3, compiler_params=_params(("parallel",)), name=f"adamw_{rows}x{cols}",
    )(*(a.reshape(rows, cols) for a in (w, g, m, v)))
    return tuple(o.reshape(shape) for o in out)


def kernel(x, mem, g_mix, w_in_a, b_glu, w_dw_a, b_dw_a, ln_g, ln_b, g_kv, w_kvf, b_f, w_in_b, g_mem, w_mem_kv, w_out, g_ffn, w_up, w_dw_f, b_dw_f, w_down, g_final, loss_target, m_g_mix, m_w_in_a, m_b_glu, m_w_dw_a, m_b_dw_a, m_ln_g, m_ln_b, m_g_kv, m_w_kvf, m_b_f, m_w_in_b, m_g_mem, m_w_mem_kv, m_w_out, m_g_ffn, m_w_up, m_w_dw_f, m_b_dw_f, m_w_down, m_g_final, v_g_mix, v_w_in_a, v_b_glu, v_w_dw_a, v_b_dw_a, v_ln_g, v_ln_b, v_g_kv, v_w_kvf, v_b_f, v_w_in_b, v_g_mem, v_w_mem_kv, v_w_out, v_g_ffn, v_w_up, v_w_dw_f, v_b_dw_f, v_w_down, v_g_final):
    weights = dict(g_mix=g_mix, w_in_a=w_in_a, b_glu=b_glu, w_dw_a=w_dw_a, b_dw_a=b_dw_a, ln_g=ln_g, ln_b=ln_b, g_kv=g_kv,
                   w_kvf=w_kvf, b_f=b_f, w_in_b=w_in_b, g_mem=g_mem, w_mem_kv=w_mem_kv, w_out=w_out, g_ffn=g_ffn, w_up=w_up,
                   w_dw_f=w_dw_f, b_dw_f=b_dw_f, w_down=w_down, g_final=g_final)
    mom1 = dict(g_mix=m_g_mix, w_in_a=m_w_in_a, b_glu=m_b_glu, w_dw_a=m_w_dw_a, b_dw_a=m_b_dw_a, ln_g=m_ln_g, ln_b=m_ln_b,
                g_kv=m_g_kv, w_kvf=m_w_kvf, b_f=m_b_f, w_in_b=m_w_in_b, g_mem=m_g_mem, w_mem_kv=m_w_mem_kv, w_out=m_w_out,
                g_ffn=m_g_ffn, w_up=m_w_up, w_dw_f=m_w_dw_f, b_dw_f=m_b_dw_f, w_down=m_w_down, g_final=m_g_final)
    mom2 = dict(g_mix=v_g_mix, w_in_a=v_w_in_a, b_glu=v_b_glu, w_dw_a=v_w_dw_a, b_dw_a=v_b_dw_a, ln_g=v_ln_g, ln_b=v_ln_b,
                g_kv=v_g_kv, w_kvf=v_w_kvf, b_f=v_b_f, w_in_b=v_w_in_b, g_mem=v_g_mem, w_mem_kv=v_w_mem_kv, w_out=v_w_out,
                g_ffn=v_g_ffn, w_up=v_w_up, w_dw_f=v_w_dw_f, b_dw_f=v_b_dw_f, w_down=v_w_down, g_final=v_g_final)

    x_pos, y_pos, core = lax.axis_index("x"), lax.axis_index("y"), lax.axis_index("c")
    chip = 2 * x_pos + y_pos

    mixer_of = lambda l: LAYER_LEAVES[l][:-FFN_LEAVES]
    ffn_of = lambda l: LAYER_LEAVES[l][-FFN_LEAVES:]
    local_first = _layer_halves(weights, mixer_of(0))
    vector_halves = _flatten_vectors([weights[n] for n in VECTOR_LEAVES], F32)
    first = _gather_leaves(local_first + [vector_halves])
    later = [ffn_of(0)] + [LAYER_LEAVES[l] for l in range(1, N_LAYERS)]
    local_later = [_layer_halves(weights, leaves) for leaves in later]
    in_flight = _gather_start(local_later, first[-1])

    def with_own(gathered, own):
        return [lax.dynamic_update_index_in_dim(g, o[None], chip, 0) for g, o in zip(gathered, own)]

    first = with_own(first, local_first + [vector_halves])
    vectors = _unflatten_vectors(first[-1], (N_CHIPS,))
    table = _weight_table({n: weights[n] for n in REPLICATED})

    def fetch(k, after):
        send_sems, recv_sems, srcs, lands = in_flight[k]
        lands = _gather_wait(send_sems, recv_sems, srcs, lands, after, f"gather_wait_{k}")
        return with_own(_forward_halves(lands, f"forward_halves_{k}"), local_later[k])

    def arrive(l, after, part):
        if l == 0:
            gathered = first[:-1] + [None] * FFN_LEAVES if part == "mixer" else [None] * len(mixer_of(0)) + fetch(0, after)
        elif part == "mixer":
            gathered = fetch(l, after)
        else:
            return
        _install_layer(table, l, _layer_pieces(l, gathered, vectors))

    core_index = core.astype(jnp.int32).reshape(1)
    in_flight_back = []

    def emit(l, g, part):
        if l == 0:
            leaves = ffn_of(0) if part == "ffn" else mixer_of(0)
        elif part == "mixer":
            leaves = LAYER_LEAVES[l]
        else:
            return None
        with_vectors = l == 0 and part == "mixer"
        slabs = _layer_slabs(g, leaves) + ([_vector_slabs(g)] if with_vectors else [])
        chip_sums = [_add_pairs(mine, got, core_index) for mine, got in zip(slabs, _swap_halves(slabs))]
        handles, started = _scatter_start(chip_sums, f"scatter_start_{len(in_flight_back)}")
        in_flight_back.append((leaves, with_vectors, chip_sums, handles))
        return started

    loss, dx, grads = _local_step(x[0], mem[0], loss_target[0], table, arrive, emit)
    loss = lax.psum(loss, ("x", "y", "c"))

    names, mine = [], []
    for k, (leaves, with_vectors, chip_sums, (send_sems, recv_sems, srcs, lands)) in enumerate(in_flight_back):
        lands = _scatter_wait(send_sems, recv_sems, srcs, lands, dx, f"scatter_wait_{k}")
        from_chips = [lax.dynamic_update_index_in_dim(got, lax.dynamic_index_in_dim(own, chip, 0, keepdims=True), chip, 0)
                      for got, own in zip(lands, chip_sums)]
        mine += [_sum_slabs(f) for f in from_chips]
        names += list(leaves) + ([("vectors", None)] if with_vectors else [])
    theirs = _sibling_halves(mine)
    per_leaf = {}
    for (name, idx), m, t in zip(names, mine, theirs):
        per_leaf.setdefault(name, {})[idx] = jnp.concatenate([jnp.where(core == 0, m, t), jnp.where(core == 0, t, m)])
    vector_grads = per_leaf.pop("vectors")[None]
    grad_leaves = {n: (d[None] if None in d else jnp.stack([d[i] for i in sorted(d)])) for n, d in per_leaf.items()}
    grad_leaves.update(_unflatten_vectors(vector_grads))

    rep_sum = _sum_slabs(_gather_replicated(_replicated_rows(grads)).reshape(N_DEV, REP_ROWS, REP_COLS)).reshape(-1)
    off = 0
    for n, shp in REPLICATED.items():
        grad_leaves[n] = rep_sum[off:off + _size(shp)].reshape(shp)
        off += _size(shp)

    deltas, new_m, new_v = {}, {}, {}
    for n in WEIGHT_ORDER:
        deltas[n], new_m[n], new_v[n] = _adamw(weights[n], grad_leaves[n], mom1[n], mom2[n])
    return (loss, dx[None], *[grad_leaves[n] for n in WEIGHT_ORDER], *[deltas[n] for n in WEIGHT_ORDER],
            *[new_m[n] for n in WEIGHT_ORDER], *[new_v[n] for n in WEIGHT_ORDER])
```

```python
import jax
import jax.numpy as jnp
from jax import lax
from jax.experimental import pallas as pl
from jax.experimental.pallas import tpu as pltpu

F32 = jnp.float32
BF16 = jnp.bfloat16

D_MODEL = 1024
N_LAYERS = 4
N_A = 2
CONV_CH = 768
MEM_W = 256
HEAD_DIM = 64
N_MEM_HEADS = 4
N_FOX_HEADS = 12
N_HEAD_PAIRS = N_FOX_HEADS // 2
D_FF = 2816
CONV_W = 31
CONV_PAD = 32
FFN_CONV_W = 3
FFN_PAD = 8
F_PAD = 128
RMS_EPS = 1e-6
LN_EPS = 1e-5
ATT_SCALE = HEAD_DIM ** -0.5
NEG_BIG = -1e30

ADAM_LR = 0.001
ADAM_B1 = 0.9
ADAM_B2 = 0.999
ADAM_EPS = 1e-08
ADAM_WD = 0.01
ADAM_STEP = 10

LANE = 128
ROW_TILE = 256
CHUNK = 128
VMEM_LIMIT = 48 * 1024 * 1024
FLAT_COLS = 1024
N_CHIPS = 4
N_DEV = 8
MESH_ID = pl.DeviceIdType.MESH


def _params(sem=None):
    return pltpu.CompilerParams(dimension_semantics=sem, vmem_limit_bytes=VMEM_LIMIT)


def _tile(dim, pref):
    if dim <= pref:
        return dim
    best = None
    for m in range(1, dim // LANE + 1):
        d = m * LANE
        if dim % d == 0 and d <= pref:
            best = d
    assert best is not None, dim
    return best


_DIMS = {"nn": (((1,), (0,)), ((), ())), "nt": (((1,), (1,)), ((), ())), "tn": (((0,), (0,)), ((), ()))}


_MM_RESIDENT_BYTES = 8 * 1024 * 1024
_MM_STREAM_BYTES = 6 * 1024 * 1024
_MM_OUT_BYTES = 6 * 1024 * 1024


def _mm_tiles(m, n, k, mode, a_size, b_size, o_size):
    if mode == "tn":
        tm = _tile(m, _MM_RESIDENT_BYTES // (k * a_size))
        tn = _tile(n, min(_MM_STREAM_BYTES // (k * b_size), _MM_OUT_BYTES // (tm * o_size)))
    else:
        tn = _tile(n, _MM_RESIDENT_BYTES // (k * b_size))
        tm = _tile(m, min(_MM_STREAM_BYTES // (k * a_size), _MM_OUT_BYTES // (tn * o_size), 512))
    return tm, tn


def _mm(a, b, mode="nn", out_dtype=F32, add=None):
    if mode == "nn":
        (m, k), (k2, n) = a.shape, b.shape
    elif mode == "nt":
        (m, k), (n, k2) = a.shape, b.shape
    else:
        (k, m), (k2, n) = a.shape, b.shape
    assert k == k2, (a.shape, b.shape, mode)
    tm, tn = _mm_tiles(m, n, k, mode, a.dtype.itemsize, b.dtype.itemsize, jnp.dtype(out_dtype).itemsize)
    dims = _DIMS[mode]
    has_add = add is not None

    def body(*refs):
        if has_add:
            a_ref, b_ref, add_ref, o_ref = refs
        else:
            a_ref, b_ref, o_ref = refs
        r = lax.dot_general(a_ref[...].astype(BF16), b_ref[...].astype(BF16), dims, preferred_element_type=F32)
        if has_add:
            r = r + add_ref[...]
        o_ref[...] = r.astype(out_dtype)

    a_spec = pl.BlockSpec((k, tm), lambda i, j: (0, i)) if mode == "tn" else pl.BlockSpec((tm, k), lambda i, j: (i, 0))
    b_spec = pl.BlockSpec((tn, k), lambda i, j: (j, 0)) if mode == "nt" else pl.BlockSpec((k, tn), lambda i, j: (0, j))
    o_spec = pl.BlockSpec((tm, tn), lambda i, j: (i, j))
    in_specs = [a_spec, b_spec] + ([o_spec] if has_add else [])
    args = (a, b) + ((add,) if has_add else ())
    return pl.pallas_call(
        body,
        out_shape=jax.ShapeDtypeStruct((m, n), out_dtype),
        grid=(m // tm, n // tn),
        in_specs=in_specs,
        out_specs=o_spec,
        compiler_params=_params(("parallel", "parallel")),
        name=f"mm_{mode}_{m}x{k}x{n}",
    )(*args)


def _row(width):
    return pl.BlockSpec((ROW_TILE, width), lambda i: (i, 0))


def _vec(width):
    return pl.BlockSpec((1, width), lambda i: (0, 0))


def _rms_fwd(x, g):
    rows, d = x.shape

    def body(x_ref, g_ref, o_ref):
        xv = x_ref[...]
        rstd = lax.rsqrt(jnp.mean(xv * xv, axis=-1, keepdims=True) + RMS_EPS)
        o_ref[...] = (xv * rstd * g_ref[...]).astype(BF16)

    return pl.pallas_call(
        body, out_shape=jax.ShapeDtypeStruct((rows, d), BF16), grid=(rows // ROW_TILE,),
        in_specs=[_row(d), _vec(d)], out_specs=_row(d), compiler_params=_params(("parallel",)), name=f"rms_fwd_{rows}",
    )(x, g)


def _accumulate(ref, value, step):
    @pl.when(step == 0)
    def _():
        ref[...] = value

    @pl.when(step > 0)
    def _():
        ref[...] += value


def _rms_bwd(x, g, dh, dres):
    rows, d = x.shape
    has_res = dres is not None

    def body(*refs):
        if has_res:
            x_ref, g_ref, dh_ref, dres_ref, dx_ref, dxb_ref, dg_ref = refs
        else:
            x_ref, g_ref, dh_ref, dx_ref, dxb_ref, dg_ref = refs
        xv = x_ref[...]
        dhv = dh_ref[...]
        rstd = lax.rsqrt(jnp.mean(xv * xv, axis=-1, keepdims=True) + RMS_EPS)
        xhat = xv * rstd
        gd = dhv * g_ref[...]
        dx = rstd * (gd - xhat * jnp.mean(gd * xhat, axis=-1, keepdims=True))
        if has_res:
            dx = dx + dres_ref[...]
        dx_ref[...] = dx
        dxb_ref[...] = dx.astype(BF16)
        _accumulate(dg_ref, jnp.sum(dhv * xhat, axis=0, keepdims=True), pl.program_id(0))

    in_specs = [_row(d), _vec(d), _row(d)] + ([_row(d)] if has_res else [])
    args = (x, g, dh) + ((dres,) if has_res else ())
    return pl.pallas_call(
        body,
        out_shape=(jax.ShapeDtypeStruct((rows, d), F32), jax.ShapeDtypeStruct((rows, d), BF16), jax.ShapeDtypeStruct((1, d), F32)),
        grid=(rows // ROW_TILE,), in_specs=in_specs, out_specs=(_row(d), _row(d), _vec(d)),
        compiler_params=_params(("arbitrary",)), name=f"rms_bwd_{rows}_{int(has_res)}",
    )(*args)


def _loss_head(x, g, target):
    rows, d = x.shape

    def body(x_ref, g_ref, t_ref, loss_ref, dx_ref, dxb_ref, dg_ref):
        xv = x_ref[...]
        gv = g_ref[...]
        rstd = lax.rsqrt(jnp.mean(xv * xv, axis=-1, keepdims=True) + RMS_EPS)
        xhat = xv * rstd
        err = xhat * gv - t_ref[...]
        part = 0.5 * jnp.sum(jnp.mean(err * err, axis=-1, keepdims=True), axis=0, keepdims=True)
        dy = err * (1.0 / d)
        gd = dy * gv
        dx = rstd * (gd - xhat * jnp.mean(gd * xhat, axis=-1, keepdims=True))
        dx_ref[...] = dx
        dxb_ref[...] = dx.astype(BF16)
        step = pl.program_id(0)
        _accumulate(loss_ref, jnp.broadcast_to(part, (1, LANE)), step)
        _accumulate(dg_ref, jnp.sum(dy * xhat, axis=0, keepdims=True), step)

    return pl.pallas_call(
        body,
        out_shape=(jax.ShapeDtypeStruct((1, LANE), F32), jax.ShapeDtypeStruct((rows, d), F32),
                   jax.ShapeDtypeStruct((rows, d), BF16), jax.ShapeDtypeStruct((1, d), F32)),
        grid=(rows // ROW_TILE,), in_specs=[_row(d), _vec(d), _row(d)],
        out_specs=(_vec(LANE), _row(d), _row(d), _vec(d)),
        compiler_params=_params(("arbitrary",)), name="loss_head",
    )(x, g, target)


def _sigmoid(x):
    return 1.0 / (1.0 + jnp.exp(-x))


def _ln_silu_fwd(c, ln_g, ln_b):
    rows, ch = c.shape

    def body(c_ref, g_ref, b_ref, o_ref):
        cv = c_ref[...]
        mu = jnp.mean(cv, axis=-1, keepdims=True)
        cen = cv - mu
        rstd = lax.rsqrt(jnp.mean(cen * cen, axis=-1, keepdims=True) + LN_EPS)
        y = cen * rstd * g_ref[...] + b_ref[...]
        o_ref[...] = (y * _sigmoid(y)).astype(BF16)

    return pl.pallas_call(
        body, out_shape=jax.ShapeDtypeStruct((rows, ch), BF16), grid=(rows // ROW_TILE,),
        in_specs=[_row(ch), _vec(ch), _vec(ch)], out_specs=_row(ch), compiler_params=_params(("parallel",)), name="ln_silu_fwd",
    )(c, ln_g, ln_b)


def _ln_silu_bwd(dmix, c, ln_g, ln_b):
    rows, ch = c.shape

    def body(dm_ref, c_ref, g_ref, b_ref, dc_ref, dg_ref, db_ref):
        cv = c_ref[...]
        gv = g_ref[...]
        mu = jnp.mean(cv, axis=-1, keepdims=True)
        cen = cv - mu
        rstd = lax.rsqrt(jnp.mean(cen * cen, axis=-1, keepdims=True) + LN_EPS)
        xhat = cen * rstd
        y = xhat * gv + b_ref[...]
        sg = _sigmoid(y)
        dy = dm_ref[...] * (sg * (1.0 + y * (1.0 - sg)))
        dxh = dy * gv
        dc = rstd * (dxh - jnp.mean(dxh, axis=-1, keepdims=True) - xhat * jnp.mean(dxh * xhat, axis=-1, keepdims=True))
        dc_ref[...] = dc
        step = pl.program_id(0)
        _accumulate(dg_ref, jnp.sum(dy * xhat, axis=0, keepdims=True), step)
        _accumulate(db_ref, jnp.sum(dy, axis=0, keepdims=True), step)

    return pl.pallas_call(
        body,
        out_shape=(jax.ShapeDtypeStruct((rows, ch), F32), jax.ShapeDtypeStruct((1, ch), F32), jax.ShapeDtypeStruct((1, ch), F32)),
        grid=(rows // ROW_TILE,), in_specs=[_row(ch), _row(ch), _vec(ch), _vec(ch)],
        out_specs=(_row(ch), _vec(ch), _vec(ch)), compiler_params=_params(("arbitrary",)), name="ln_silu_bwd",
    )(dmix, c, ln_g, ln_b)


def _col(rows, cb):
    return pl.BlockSpec((rows, cb), lambda j: (0, j))


SUBLANES = 8
_WINDOW = CHUNK + CONV_PAD


def _realign(win, shifted_ref):
    shifted_ref[0] = win
    for r in range(1, SUBLANES):
        shifted_ref[r, 0:_WINDOW - SUBLANES, :] = win[r:r + _WINDOW - SUBLANES, :]


def _tap(shifted_ref, off):
    r = off % SUBLANES
    return shifted_ref[r, off - r:off - r + CHUNK, :]


def _conv_glu_fwd(pa, pg, ba, bg, w_dw, b_dw):
    seq, ch = pa.shape
    cb = LANE
    n_chunks = seq // CHUNK

    def body(pa_ref, pg_ref, ba_ref, bg_ref, w_ref, b_ref, v_ref, c_ref, vpad_ref, vshift_ref):
        vpad_ref[0:CONV_PAD, :] = jnp.zeros((CONV_PAD, cb), F32)

        def glu(r, carry):
            r0 = pl.multiple_of(r * CHUNK, CHUNK)
            rows = pl.ds(r0, CHUNK)
            v = (pa_ref[rows, :] + ba_ref[...]) * _sigmoid(pg_ref[rows, :] + bg_ref[...])
            v_ref[rows, :] = v
            vpad_ref[pl.ds(r0 + CONV_PAD, CHUNK), :] = v
            return carry

        lax.fori_loop(0, n_chunks, glu, 0)

        def conv(r, carry):
            r0 = pl.multiple_of(r * CHUNK, CHUNK)
            _realign(vpad_ref[pl.ds(r0, _WINDOW), :], vshift_ref)
            acc = jnp.broadcast_to(b_ref[...], (CHUNK, cb))
            for t in range(CONV_W):
                acc = acc + w_ref[t:t + 1, :] * _tap(vshift_ref, CONV_PAD - (CONV_W - 1) + t)
            c_ref[pl.ds(r0, CHUNK), :] = acc
            return carry

        lax.fori_loop(0, n_chunks, conv, 0)

    return pl.pallas_call(
        body,
        out_shape=(jax.ShapeDtypeStruct((seq, ch), F32), jax.ShapeDtypeStruct((seq, ch), F32)),
        grid=(ch // cb,),
        in_specs=[_col(seq, cb), _col(seq, cb), _col(1, cb), _col(1, cb), _col(CONV_PAD, cb), _col(1, cb)],
        out_specs=(_col(seq, cb), _col(seq, cb)),
        scratch_shapes=[pltpu.VMEM((seq + CONV_PAD, cb), F32), pltpu.VMEM((SUBLANES, _WINDOW, cb), F32)],
        compiler_params=_params(("parallel",)), name="conv_glu_fwd",
    )(pa, pg, ba, bg, w_dw, b_dw)


def _conv_glu_bwd(dc, v, pa, pg, ba, bg, w_dw):
    seq, ch = dc.shape
    cb = LANE
    n_chunks = seq // CHUNK

    def body(dc_ref, v_ref, pa_ref, pg_ref, ba_ref, bg_ref, w_ref, da_ref, dg_ref, dw_ref, dbdw_ref, dba_ref, dbg_ref,
             dcpad_ref, vpad_ref, dcshift_ref, vshift_ref):
        vpad_ref[0:CONV_PAD, :] = jnp.zeros((CONV_PAD, cb), F32)
        dcpad_ref[seq:seq + CONV_PAD, :] = jnp.zeros((CONV_PAD, cb), F32)
        dw_ref[...] = jnp.zeros((CONV_PAD, cb), F32)
        dbdw_ref[...] = jnp.zeros((1, cb), F32)
        dba_ref[...] = jnp.zeros((1, cb), F32)
        dbg_ref[...] = jnp.zeros((1, cb), F32)

        def fill(r, carry):
            r0 = pl.multiple_of(r * CHUNK, CHUNK)
            vpad_ref[pl.ds(r0 + CONV_PAD, CHUNK), :] = v_ref[pl.ds(r0, CHUNK), :]
            dcpad_ref[pl.ds(r0, CHUNK), :] = dc_ref[pl.ds(r0, CHUNK), :]
            return carry

        lax.fori_loop(0, n_chunks, fill, 0)

        def step(r, carry):
            r0 = pl.multiple_of(r * CHUNK, CHUNK)
            rows = pl.ds(r0, CHUNK)
            _realign(dcpad_ref[pl.ds(r0, _WINDOW), :], dcshift_ref)
            _realign(vpad_ref[pl.ds(r0, _WINDOW), :], vshift_ref)
            dcur = _tap(dcshift_ref, 0)
            dv = jnp.zeros((CHUNK, cb), F32)
            for t in range(CONV_W):
                dv = dv + w_ref[t:t + 1, :] * _tap(dcshift_ref, CONV_W - 1 - t)
                dw_ref[t:t + 1, :] += jnp.sum(dcur * _tap(vshift_ref, CONV_PAD - (CONV_W - 1) + t), axis=0, keepdims=True)
            a = pa_ref[rows, :] + ba_ref[...]
            sg = _sigmoid(pg_ref[rows, :] + bg_ref[...])
            da = dv * sg
            dgate = dv * a * sg * (1.0 - sg)
            da_ref[rows, :] = da.astype(BF16)
            dg_ref[rows, :] = dgate.astype(BF16)
            dbdw_ref[...] += jnp.sum(dcur, axis=0, keepdims=True)
            dba_ref[...] += jnp.sum(da, axis=0, keepdims=True)
            dbg_ref[...] += jnp.sum(dgate, axis=0, keepdims=True)
            return carry

        lax.fori_loop(0, n_chunks, step, 0)

    return pl.pallas_call(
        body,
        out_shape=(jax.ShapeDtypeStruct((seq, ch), BF16), jax.ShapeDtypeStruct((seq, ch), BF16),
                   jax.ShapeDtypeStruct((CONV_PAD, ch), F32), jax.ShapeDtypeStruct((1, ch), F32),
                   jax.ShapeDtypeStruct((1, ch), F32), jax.ShapeDtypeStruct((1, ch), F32)),
        grid=(ch // cb,),
        in_specs=[_col(seq, cb), _col(seq, cb), _col(seq, cb), _col(seq, cb), _col(1, cb), _col(1, cb), _col(CONV_PAD, cb)],
        out_specs=(_col(seq, cb), _col(seq, cb), _col(CONV_PAD, cb), _col(1, cb), _col(1, cb), _col(1, cb)),
        scratch_shapes=[pltpu.VMEM((seq + CONV_PAD, cb), F32), pltpu.VMEM((seq + CONV_PAD, cb), F32),
                        pltpu.VMEM((SUBLANES, _WINDOW, cb), F32), pltpu.VMEM((SUBLANES, _WINDOW, cb), F32)],
        compiler_params=_params(("parallel",)), name="conv_glu_bwd",
    )(dc, v, pa, pg, ba, bg, w_dw)


def _ffn_conv(pad_ref, r0, w_ref, b_ref, cb):
    win = pad_ref[pl.ds(r0, CHUNK + FFN_PAD), :]
    y = jnp.broadcast_to(b_ref[...], (CHUNK, cb))
    for t in range(FFN_CONV_W):
        off = FFN_PAD - (FFN_CONV_W - 1) + t
        y = y + w_ref[t:t + 1, :] * win[off:off + CHUNK, :]
    return y, win


def _ffn_mid_fwd(ug, uv, wg, wv, bg, bv):
    seq, ch = ug.shape
    cb = _tile(ch, 256)
    n_chunks = seq // CHUNK

    def body(ug_ref, uv_ref, wg_ref, wv_ref, bg_ref, bv_ref, z_ref, gpad_ref, vpad_ref):
        gpad_ref[0:FFN_PAD, :] = jnp.zeros((FFN_PAD, cb), F32)
        vpad_ref[0:FFN_PAD, :] = jnp.zeros((FFN_PAD, cb), F32)

        def fill(r, carry):
            r0 = pl.multiple_of(r * CHUNK, CHUNK)
            gpad_ref[pl.ds(r0 + FFN_PAD, CHUNK), :] = ug_ref[pl.ds(r0, CHUNK), :]
            vpad_ref[pl.ds(r0 + FFN_PAD, CHUNK), :] = uv_ref[pl.ds(r0, CHUNK), :]
            return carry

        lax.fori_loop(0, n_chunks, fill, 0)

        def step(r, carry):
            r0 = pl.multiple_of(r * CHUNK, CHUNK)
            yg, _ = _ffn_conv(gpad_ref, r0, wg_ref, bg_ref, cb)
            yv, _ = _ffn_conv(vpad_ref, r0, wv_ref, bv_ref, cb)
            z_ref[pl.ds(r0, CHUNK), :] = (yg * _sigmoid(yg) * yv).astype(BF16)
            return carry

        lax.fori_loop(0, n_chunks, step, 0)

    return pl.pallas_call(
        body, out_shape=jax.ShapeDtypeStruct((seq, ch), BF16), grid=(ch // cb,),
        in_specs=[_col(seq, cb), _col(seq, cb), _col(FFN_PAD, cb), _col(FFN_PAD, cb), _col(1, cb), _col(1, cb)],
        out_specs=_col(seq, cb),
        scratch_shapes=[pltpu.VMEM((seq + FFN_PAD, cb), F32), pltpu.VMEM((seq + FFN_PAD, cb), F32)],
        compiler_params=_params(("parallel",)), name="ffn_mid_fwd",
    )(ug, uv, wg, wv, bg, bv)


def _ffn_mid_bwd(ug, uv, dz, wg, wv, bg, bv):
    seq, ch = ug.shape
    cb = _tile(ch, 256)
    n_chunks = seq // CHUNK

    def body(ug_ref, uv_ref, dz_ref, wg_ref, wv_ref, bg_ref, bv_ref, dug_ref, duv_ref, dwg_ref, dwv_ref, dbg_ref, dbv_ref,
             gpad_ref, vpad_ref, dyg_ref, dyv_ref):
        gpad_ref[0:FFN_PAD, :] = jnp.zeros((FFN_PAD, cb), F32)
        vpad_ref[0:FFN_PAD, :] = jnp.zeros((FFN_PAD, cb), F32)
        dyg_ref[seq:seq + FFN_PAD, :] = jnp.zeros((FFN_PAD, cb), F32)
        dyv_ref[seq:seq + FFN_PAD, :] = jnp.zeros((FFN_PAD, cb), F32)
        dwg_ref[...] = jnp.zeros((FFN_PAD, cb), F32)
        dwv_ref[...] = jnp.zeros((FFN_PAD, cb), F32)
        dbg_ref[...] = jnp.zeros((1, cb), F32)
        dbv_ref[...] = jnp.zeros((1, cb), F32)

        def fill(r, carry):
            r0 = pl.multiple_of(r * CHUNK, CHUNK)
            gpad_ref[pl.ds(r0 + FFN_PAD, CHUNK), :] = ug_ref[pl.ds(r0, CHUNK), :]
            vpad_ref[pl.ds(r0 + FFN_PAD, CHUNK), :] = uv_ref[pl.ds(r0, CHUNK), :]
            return carry

        lax.fori_loop(0, n_chunks, fill, 0)

        def grads_of_conv_out(r, carry):
            r0 = pl.multiple_of(r * CHUNK, CHUNK)
            rows = pl.ds(r0, CHUNK)
            yg, gwin = _ffn_conv(gpad_ref, r0, wg_ref, bg_ref, cb)
            yv, vwin = _ffn_conv(vpad_ref, r0, wv_ref, bv_ref, cb)
            dzv = dz_ref[rows, :]
            sg = _sigmoid(yg)
            dyg = dzv * yv * (sg * (1.0 + yg * (1.0 - sg)))
            dyv = dzv * yg * sg
            dyg_ref[rows, :] = dyg
            dyv_ref[rows, :] = dyv
            for t in range(FFN_CONV_W):
                off = FFN_PAD - (FFN_CONV_W - 1) + t
                dwg_ref[t:t + 1, :] += jnp.sum(dyg * gwin[off:off + CHUNK, :], axis=0, keepdims=True)
                dwv_ref[t:t + 1, :] += jnp.sum(dyv * vwin[off:off + CHUNK, :], axis=0, keepdims=True)
            dbg_ref[...] += jnp.sum(dyg, axis=0, keepdims=True)
            dbv_ref[...] += jnp.sum(dyv, axis=0, keepdims=True)
            return carry

        lax.fori_loop(0, n_chunks, grads_of_conv_out, 0)

        def grads_of_conv_in(r, carry):
            r0 = pl.multiple_of(r * CHUNK, CHUNK)
            gwin = dyg_ref[pl.ds(r0, CHUNK + FFN_PAD), :]
            vwin = dyv_ref[pl.ds(r0, CHUNK + FFN_PAD), :]
            dug = jnp.zeros((CHUNK, cb), F32)
            duv = jnp.zeros((CHUNK, cb), F32)
            for t in range(FFN_CONV_W):
                off = FFN_CONV_W - 1 - t
                dug = dug + wg_ref[t:t + 1, :] * gwin[off:off + CHUNK, :]
                duv = duv + wv_ref[t:t + 1, :] * vwin[off:off + CHUNK, :]
            dug_ref[pl.ds(r0, CHUNK), :] = dug.astype(BF16)
            duv_ref[pl.ds(r0, CHUNK), :] = duv.astype(BF16)
            return carry

        lax.fori_loop(0, n_chunks, grads_of_conv_in, 0)

    return pl.pallas_call(
        body,
        out_shape=(jax.ShapeDtypeStruct((seq, ch), BF16), jax.ShapeDtypeStruct((seq, ch), BF16),
                   jax.ShapeDtypeStruct((FFN_PAD, ch), F32), jax.ShapeDtypeStruct((FFN_PAD, ch), F32),
                   jax.ShapeDtypeStruct((1, ch), F32), jax.ShapeDtypeStruct((1, ch), F32)),
        grid=(ch // cb,),
        in_specs=[_col(seq, cb), _col(seq, cb), _col(seq, cb), _col(FFN_PAD, cb), _col(FFN_PAD, cb), _col(1, cb), _col(1, cb)],
        out_specs=(_col(seq, cb), _col(seq, cb), _col(FFN_PAD, cb), _col(FFN_PAD, cb), _col(1, cb), _col(1, cb)),
        scratch_shapes=[pltpu.VMEM((seq + FFN_PAD, cb), F32) for _ in range(4)],
        compiler_params=_params(("parallel",)), name="ffn_mid_bwd",
    )(ug, uv, dz, wg, wv, bg, bv)


def _dot(a, b, mode):
    return lax.dot_general(a.astype(BF16), b.astype(BF16), _DIMS[mode], preferred_element_type=F32)


def _head(h):
    return slice(h * HEAD_DIM, (h + 1) * HEAD_DIM)


def _mem_softmax(q, k):
    s = _dot(q, k, "nt") * ATT_SCALE
    e = jnp.exp(s - jnp.max(s, axis=-1, keepdims=True))
    return e / jnp.sum(e, axis=-1, keepdims=True)


def _memattn_fwd(pq, mk, mv):
    seq, w = pq.shape
    m = mk.shape[0]

    def body(q_ref, k_ref, v_ref, o_ref):
        for h in range(N_MEM_HEADS):
            p = _mem_softmax(q_ref[:, _head(h)], k_ref[:, _head(h)])
            o_ref[:, _head(h)] = _dot(p, v_ref[:, _head(h)], "nn").astype(BF16)

    full = pl.BlockSpec((m, w), lambda i: (0, 0))
    return pl.pallas_call(
        body, out_shape=jax.ShapeDtypeStruct((seq, w), BF16), grid=(seq // ROW_TILE,),
        in_specs=[_row(w), full, full], out_specs=_row(w), compiler_params=_params(("parallel",)), name="memattn_fwd",
    )(pq, mk, mv)


def _memattn_bwd(pq, mk, mv, dmo):
    seq, w = pq.shape
    m = mk.shape[0]

    def body(q_ref, k_ref, v_ref, do_ref, dq_ref, dk_ref, dv_ref):
        step = pl.program_id(0)

        @pl.when(step == 0)
        def _():
            dk_ref[...] = jnp.zeros((m, w), F32)
            dv_ref[...] = jnp.zeros((m, w), F32)

        for h in range(N_MEM_HEADS):
            q, k, v, do = q_ref[:, _head(h)], k_ref[:, _head(h)], v_ref[:, _head(h)], do_ref[:, _head(h)]
            p = _mem_softmax(q, k)
            dp = _dot(do, v, "nt")
            ds = p * (dp - jnp.sum(dp * p, axis=-1, keepdims=True))
            dq_ref[:, _head(h)] = (_dot(ds, k, "nn") * ATT_SCALE).astype(BF16)
            dk_ref[:, _head(h)] += _dot(ds, q, "tn") * ATT_SCALE
            dv_ref[:, _head(h)] += _dot(p, do, "tn")

    full = pl.BlockSpec((m, w), lambda i: (0, 0))
    return pl.pallas_call(
        body,
        out_shape=(jax.ShapeDtypeStruct((seq, w), BF16), jax.ShapeDtypeStruct((m, w), F32), jax.ShapeDtypeStruct((m, w), F32)),
        grid=(seq // ROW_TILE,), in_specs=[_row(w), full, full, _row(w)], out_specs=(_row(w), full, full),
        compiler_params=_params(("arbitrary",)), name="memattn_bwd",
    )(pq, mk, mv, dmo)


FWD_KEY_TILE = 512
BWD_KEY_TILE = 512


def _causal_t(s_t, q_blk, k_blk, key_tile):
    kpos = k_blk * key_tile + lax.broadcasted_iota(jnp.int32, (key_tile, ROW_TILE), 0)
    qpos = q_blk * ROW_TILE + lax.broadcasted_iota(jnp.int32, (key_tile, ROW_TILE), 1)
    return jnp.where(kpos <= qpos, s_t, NEG_BIG)


def _fox_fwd(q, k, v, cum_cols, cum_rows):
    seq, w = q.shape
    nq = seq // ROW_TILE

    def body(q_ref, k_ref, v_ref, cc_ref, cr_ref, o_ref, ob_ref, lse_ref):
        i = pl.program_id(1)
        qs = [q_ref[:, _head(h)].astype(BF16) for h in range(2)]
        crs = [cr_ref[0, 0, h:h + 1, :] for h in range(2)]

        def step(j, carry):
            rows = pl.ds(pl.multiple_of(j * FWD_KEY_TILE, FWD_KEY_TILE), FWD_KEY_TILE)
            new = []
            for h in range(2):
                m_run, l_run, acc = carry[h]
                s_t = _dot(k_ref[rows, _head(h)], qs[h], "nt") * ATT_SCALE + crs[h] - cc_ref[0, rows, h * HEAD_DIM:h * HEAD_DIM + 1]
                s_t = _causal_t(s_t, i, j, FWD_KEY_TILE)
                m_new = jnp.maximum(m_run, jnp.max(s_t, axis=0, keepdims=True))
                alpha = jnp.exp(m_run - m_new)
                p_t = jnp.exp(s_t - m_new)
                l_new = alpha * l_run + jnp.sum(p_t, axis=0, keepdims=True)
                new.append((m_new, l_new, alpha * acc + _dot(v_ref[rows, _head(h)], p_t, "tn")))
            return tuple(new)

        one_head = (jnp.full((1, ROW_TILE), NEG_BIG, F32), jnp.zeros((1, ROW_TILE), F32), jnp.zeros((HEAD_DIM, ROW_TILE), F32))
        res = lax.fori_loop(0, ((i + 1) * ROW_TILE + FWD_KEY_TILE - 1) // FWD_KEY_TILE, step, (one_head, one_head))
        o = jnp.concatenate([acc / l_run for _, l_run, acc in res], axis=0).T
        o_ref[...] = o
        ob_ref[...] = o.astype(BF16)
        lse_ref[...] = jnp.zeros((1, 1, 8, ROW_TILE), F32)
        for h in range(2):
            lse_ref[0, 0, h:h + 1, :] = res[h][0] + jnp.log(res[h][1])

    blk = pl.BlockSpec((ROW_TILE, LANE), lambda hp, i: (i, hp))
    full = pl.BlockSpec((seq, LANE), lambda hp, i: (0, hp))
    cols = pl.BlockSpec((1, seq, LANE), lambda hp, i: (hp, 0, 0))
    rows = pl.BlockSpec((1, 1, 8, ROW_TILE), lambda hp, i: (hp, i, 0, 0))
    return pl.pallas_call(
        body,
        out_shape=(jax.ShapeDtypeStruct((seq, w), F32), jax.ShapeDtypeStruct((seq, w), BF16),
                   jax.ShapeDtypeStruct((N_HEAD_PAIRS, nq, 8, ROW_TILE), F32)),
        grid=(N_HEAD_PAIRS, nq), in_specs=[blk, full, full, cols, rows], out_specs=(blk, blk, rows),
        compiler_params=_params(("parallel", "parallel")), name="fox_fwd",
    )(q, k, v, cum_cols, cum_rows)


def _fox_bwd(q, k, v, cum_cols, cum_rows, o, lse, do):
    seq, w = q.shape
    nq = seq // ROW_TILE
    nk = seq // BWD_KEY_TILE

    def body(q_ref, k_ref, v_ref, cc_ref, cr_ref, o_ref, lse_ref, do_ref, dq_ref, dk_ref, dv_ref, dcc_ref, dcr_ref):
        kj = pl.program_id(1)

        @pl.when(kj == 0)
        def _():
            dq_ref[...] = jnp.zeros((seq, LANE), F32)
            dcr_ref[...] = jnp.zeros((1, nq, 8, ROW_TILE), F32)

        ks = [k_ref[:, _head(h)].astype(BF16) for h in range(2)]
        vs = [v_ref[:, _head(h)].astype(BF16) for h in range(2)]
        ccs = [cc_ref[0, :, h * HEAD_DIM:h * HEAD_DIM + 1] for h in range(2)]
        ones = jnp.ones((8, HEAD_DIM), BF16)

        def step(i, carry):
            rows = pl.ds(pl.multiple_of(i * ROW_TILE, ROW_TILE), ROW_TILE)
            new = []
            for h in range(2):
                dk, dv, fold = carry[h]
                qh = q_ref[rows, _head(h)].astype(BF16)
                doh = do_ref[rows, _head(h)]
                s_t = _dot(ks[h], qh, "nt") * ATT_SCALE + cr_ref[0, i, h:h + 1, :] - ccs[h]
                p_t = jnp.exp(_causal_t(s_t, i, kj, BWD_KEY_TILE) - lse_ref[0, i, h:h + 1, :])
                dp_t = _dot(vs[h], doh, "nt")
                hi, mid, lo = _split3(doh * o_ref[rows, _head(h)])
                row_sum = lambda part: lax.dot_general(ones, part, _DIMS["nt"], preferred_element_type=F32)
                delta = ((row_sum(lo) + row_sum(mid)) + row_sum(hi))[0:1, :]
                ds_t = p_t * (dp_t - delta)
                dq_ref[rows, _head(h)] += _dot(ds_t, ks[h], "tn") * ATT_SCALE
                dcr_ref[0, i, h:h + 1, :] += jnp.sum(ds_t, axis=0, keepdims=True)
                new.append((dk + _dot(ds_t, qh, "nn") * ATT_SCALE, dv + _dot(p_t, doh, "nn"), fold + (ds_t[:, :LANE] + ds_t[:, LANE:])))
            return tuple(new)

        one_head = (jnp.zeros((BWD_KEY_TILE, HEAD_DIM), F32), jnp.zeros((BWD_KEY_TILE, HEAD_DIM), F32), jnp.zeros((BWD_KEY_TILE, LANE), F32))
        res = lax.fori_loop(kj * BWD_KEY_TILE // ROW_TILE, nq, step, (one_head, one_head))
        for h in range(2):
            dk_ref[:, _head(h)] = res[h][0]
            dv_ref[:, _head(h)] = res[h][1]
            dcc_ref[0, :, _head(h)] = jnp.broadcast_to(-jnp.sum(res[h][2], axis=-1, keepdims=True), (BWD_KEY_TILE, HEAD_DIM))

    blk = pl.BlockSpec((BWD_KEY_TILE, LANE), lambda hp, j: (j, hp))
    full = pl.BlockSpec((seq, LANE), lambda hp, j: (0, hp))
    cols = pl.BlockSpec((1, BWD_KEY_TILE, LANE), lambda hp, j: (hp, j, 0))
    rows = pl.BlockSpec((1, nq, 8, ROW_TILE), lambda hp, j: (hp, 0, 0, 0))
    return pl.pallas_call(
        body,
        out_shape=(jax.ShapeDtypeStruct((seq, w), F32), jax.ShapeDtypeStruct((seq, w), F32), jax.ShapeDtypeStruct((seq, w), F32),
                   jax.ShapeDtypeStruct((N_HEAD_PAIRS, seq, LANE), F32), jax.ShapeDtypeStruct((N_HEAD_PAIRS, nq, 8, ROW_TILE), F32)),
        grid=(N_HEAD_PAIRS, nk), in_specs=[full, blk, blk, cols, rows, full, rows, full],
        out_specs=(full, blk, blk, cols, rows),
        compiler_params=_params(("parallel", "arbitrary")), name="fox_bwd",
    )(q, k, v, cum_cols, cum_rows, o, lse, do)


def _split3(x):
    hi = x.astype(BF16)
    r1 = x - hi.astype(F32)
    mid = r1.astype(BF16)
    lo = (r1 - mid.astype(F32)).astype(BF16)
    return hi, mid, lo


def _tri_sum(tri, x):
    hi, mid, lo = _split3(x)
    dot = lambda p: lax.dot_general(tri, p, _DIMS["nn"], preferred_element_type=F32)
    return (dot(lo) + dot(mid)) + dot(hi)


def _tri(lower):
    r = lax.broadcasted_iota(jnp.int32, (LANE, LANE), 0)
    c = lax.broadcasted_iota(jnp.int32, (LANE, LANE), 1)
    return jnp.where((c <= r) if lower else (c >= r), 1.0, 0.0).astype(BF16)


def _fgate_fwd(fl, bf):
    seq, w = fl.shape
    nb = seq // LANE

    def body(f_ref, b_ref, cum_ref):
        tri = _tri(True)

        def step(i, carry):
            rows = pl.ds(pl.multiple_of(i * LANE, LANE), LANE)
            x = f_ref[rows, :] + b_ref[...]
            logsig = jnp.minimum(x, 0.0) - jnp.log(1.0 + jnp.exp(-jnp.abs(x)))
            cum = _tri_sum(tri, logsig) + carry
            cum_ref[rows, :] = cum
            return cum[LANE - 1:LANE, :]

        lax.fori_loop(0, nb, step, jnp.zeros((1, w), F32))

    return pl.pallas_call(body, out_shape=jax.ShapeDtypeStruct((seq, w), F32), compiler_params=_params(), name="fgate_fwd")(fl, bf)


def _fgate_bwd(fl, bf, dcum):
    seq, w = fl.shape
    nb = seq // LANE

    def body(f_ref, b_ref, dc_ref, df_ref, db_ref):
        tri = _tri(False)

        def step(i, carry):
            tail, db = carry
            rows = pl.ds(pl.multiple_of((nb - 1 - i) * LANE, LANE), LANE)
            suffix = _tri_sum(tri, dc_ref[rows, :]) + tail
            df = suffix * (1.0 - _sigmoid(f_ref[rows, :] + b_ref[...]))
            df_ref[rows, :] = df
            return suffix[0:1, :], db + jnp.sum(df, axis=0, keepdims=True)

        _, db = lax.fori_loop(0, nb, step, (jnp.zeros((1, w), F32), jnp.zeros((1, w), F32)))
        db_ref[...] = db

    return pl.pallas_call(
        body, out_shape=(jax.ShapeDtypeStruct((seq, w), F32), jax.ShapeDtypeStruct((1, w), F32)),
        compiler_params=_params(), name="fgate_bwd",
    )(fl, bf, dcum)


def _cum_layouts(cum):
    seq = cum.shape[0]
    nblk = seq // ROW_TILE
    heads = cum[:, :N_FOX_HEADS]
    cq = jnp.repeat(heads, HEAD_DIM, axis=1).reshape(seq, N_HEAD_PAIRS, LANE).transpose(1, 0, 2)
    ck = heads.T.reshape(N_HEAD_PAIRS, 2, nblk, ROW_TILE).transpose(0, 2, 1, 3)
    ck = jnp.pad(ck, ((0, 0), (0, 0), (0, 6), (0, 0)))
    return cq, ck


def _dcum_from_layouts(dcq, dck):
    seq = dcq.shape[1]
    key_side = dck[:, :, :2, :].transpose(0, 2, 1, 3).reshape(N_FOX_HEADS, seq).T
    query_side = dcq[:, :, ::HEAD_DIM].transpose(1, 0, 2).reshape(seq, N_FOX_HEADS)
    return jnp.pad(key_side + query_side, ((0, 0), (0, F_PAD - N_FOX_HEADS)))


def _local_step(x, mem, target, w, arrive, emit):
    vec = lambda a: a.reshape(1, -1)
    mem_n = _rms_fwd(mem, vec(w["g_mem"]))
    saved = []
    shared = None
    for l in range(N_LAYERS):
        arrive(l, x, "mixer")
        s = {"x_in": x}
        s["h"] = h = _rms_fwd(x, vec(w["g_mix"][l]))
        s["mk"] = mk = _mm(mem_n, w["w_mk"][l])
        s["mv"] = mv = _mm(mem_n, w["w_mv"][l])
        if l < N_A:
            s["pa"] = pa = _mm(h, w["w_a"][l])
            s["pg"] = pg = _mm(h, w["w_g"][l])
            s["pq"] = pq = _mm(h, w["w_qm"][l])
            s["v"], s["c"] = _conv_glu_fwd(pa, pg, vec(w["b_glu_a"][l]), vec(w["b_glu_g"][l]), w["w_dw_a"][l], vec(w["b_dw_a"][l]))
            s["mix"] = mix = _ln_silu_fwd(s["c"], vec(w["ln_g"][l]), vec(w["ln_b"][l]))
        else:
            if l == N_A:
                shared = {"x_in": x}
                shared["hk"] = hk = _rms_fwd(x, vec(w["g_kv"]))
                shared["k"] = _mm(hk, w["w_k"])
                shared["v"] = _mm(hk, w["w_v"])
                shared["fl"] = _mm(hk, w["w_f"])
                cum = _fgate_fwd(shared["fl"], w["b_f"])
                shared["cq"], shared["ck"] = _cum_layouts(cum)
            s["q"] = q = _mm(h, w["w_q"][l - N_A])
            s["pq"] = pq = _mm(h, w["w_qm"][l])
            s["o"], mix, s["lse"] = _fox_fwd(q, shared["k"], shared["v"], shared["cq"], shared["ck"])
            s["mix"] = mix
        s["mo"] = mo = _memattn_fwd(pq, mk, mv)
        x = _mm(mix, w["w_o_mix"][l], add=x)
        s["x_mid"] = x = _mm(mo, w["w_o_mem"][l], add=x)
        arrive(l, x, "ffn")
        s["hf"] = hf = _rms_fwd(x, vec(w["g_ffn"][l]))
        s["ug"] = ug = _mm(hf, w["w_up_g"][l])
        s["uv"] = uv = _mm(hf, w["w_up_v"][l])
        s["z"] = z = _ffn_mid_fwd(ug, uv, w["w_dw_f_g"][l], w["w_dw_f_v"][l], vec(w["b_dw_f_g"][l]), vec(w["b_dw_f_v"][l]))
        x = _mm(z, w["w_down"][l], add=x)
        saved.append(s)

    loss_row, dx, dxb, dg_final = _loss_head(x, vec(w["g_final"]), target)
    loss = loss_row[0, 0]

    g = {n: [None] * N_LAYERS for n in ("g_mix", "w_mem_kv", "w_out", "g_ffn", "w_up", "w_dw_f", "b_dw_f", "w_down")}
    for n in ("w_in_a", "b_glu", "w_dw_a", "b_dw_a", "ln_g", "ln_b", "w_in_b"):
        g[n] = [None] * N_A
    one = lambda arr: ([arr], 0)
    after_start = lambda started, grad: grad if started is None else grad + started[0, 0].astype(BF16)
    g["g_final"] = [one(dg_final[0])]
    dmem_n = None
    dk_sum = dv_sum = dcq_sum = dck_sum = None
    for l in reversed(range(N_LAYERS)):
        s = saved[l]
        dz = _mm(dxb, w["w_down"][l], "nt")
        g["w_down"][l] = one(_mm(s["z"], dxb, "tn", BF16))
        dug, duv, dwg, dwv, dbg, dbv = _ffn_mid_bwd(s["ug"], s["uv"], dz, w["w_dw_f_g"][l], w["w_dw_f_v"][l],
                                                    vec(w["b_dw_f_g"][l]), vec(w["b_dw_f_v"][l]))
        g["w_up"][l] = ([_mm(s["hf"], dug, "tn", BF16), _mm(s["hf"], duv, "tn", BF16)], 1)
        g["w_dw_f"][l] = ([dwg[:FFN_CONV_W], dwv[:FFN_CONV_W]], 1)
        g["b_dw_f"][l] = ([dbg[0], dbv[0]], 0)
        dhf = _mm(duv, w["w_up_v"][l], "nt", add=_mm(dug, w["w_up_g"][l], "nt"))
        dx, dxb, dg_ffn = _rms_bwd(s["x_mid"], vec(w["g_ffn"][l]), dhf, dx)
        g["g_ffn"][l] = one(dg_ffn[0])
        dxb = after_start(emit(l, g, "ffn"), dxb)
        dmix = _mm(dxb, w["w_o_mix"][l], "nt")
        dmo = _mm(dxb, w["w_o_mem"][l], "nt")
        g["w_out"][l] = ([_mm(s["mix"], dxb, "tn", BF16), _mm(s["mo"], dxb, "tn", BF16)], 0)
        dpq, dmk, dmv = _memattn_bwd(s["pq"], s["mk"], s["mv"], dmo)
        g["w_mem_kv"][l] = ([_mm(mem_n, dmk, "tn", BF16), _mm(mem_n, dmv, "tn", BF16)], 1)
        dmem_n = _mm(dmk, w["w_mk"][l], "nt", add=dmem_n)
        dmem_n = _mm(dmv, w["w_mv"][l], "nt", add=dmem_n)
        if l < N_A:
            dc, dlng, dlnb = _ln_silu_bwd(dmix, s["c"], vec(w["ln_g"][l]), vec(w["ln_b"][l]))
            da, dgate, dwdw, dbdw, dba, dbg2 = _conv_glu_bwd(dc, s["v"], s["pa"], s["pg"], vec(w["b_glu_a"][l]),
                                                              vec(w["b_glu_g"][l]), w["w_dw_a"][l])
            g["ln_g"][l], g["ln_b"][l], g["b_dw_a"][l] = one(dlng[0]), one(dlnb[0]), one(dbdw[0])
            g["w_dw_a"][l] = ([dwdw[:CONV_W]], 1)
            g["b_glu"][l] = ([dba[0], dbg2[0]], 0)
            g["w_in_a"][l] = ([_mm(s["h"], da, "tn", BF16), _mm(s["h"], dgate, "tn", BF16), _mm(s["h"], dpq, "tn", BF16)], 1)
            dh = _mm(da, w["w_a"][l], "nt")
            dh = _mm(dgate, w["w_g"][l], "nt", add=dh)
            dh = _mm(dpq, w["w_qm"][l], "nt", add=dh)
        else:
            dq, dk, dv, dcq, dck = _fox_bwd(s["q"], shared["k"], shared["v"], shared["cq"], shared["ck"], s["o"], s["lse"], dmix)
            dk_sum = dk if dk_sum is None else dk_sum + dk
            dv_sum = dv if dv_sum is None else dv_sum + dv
            dcq_sum = dcq if dcq_sum is None else dcq_sum + dcq
            dck_sum = dck if dck_sum is None else dck_sum + dck
            g["w_in_b"][l - N_A] = ([_mm(s["h"], dq, "tn", BF16), _mm(s["h"], dpq, "tn", BF16)], 1)
            dh = _mm(dq, w["w_q"][l - N_A], "nt")
            dh = _mm(dpq, w["w_qm"][l], "nt", add=dh)
        dx, dxb, dg_mix = _rms_bwd(s["x_in"], vec(w["g_mix"][l]), dh, dx)
        g["g_mix"][l] = one(dg_mix[0])
        if l == N_A:
            df, dbf = _fgate_bwd(shared["fl"], w["b_f"], _dcum_from_layouts(dcq_sum, dck_sum))
            hk = shared["hk"]
            g["w_kvf"] = [([_mm(hk, dk_sum, "tn", BF16), _mm(hk, dv_sum, "tn", BF16), _mm(hk, df, "tn", BF16)[:, :N_FOX_HEADS]], 1)]
            g["b_f"] = [one(dbf[0, :N_FOX_HEADS])]
            dhk = _mm(dk_sum, w["w_k"], "nt")
            dhk = _mm(dv_sum, w["w_v"], "nt", add=dhk)
            dhk = _mm(df, w["w_f"], "nt", add=dhk)
            dx, dxb, dg_kv = _rms_bwd(shared["x_in"], vec(w["g_kv"]), dhk, dx)
            g["g_kv"] = [one(dg_kv[0])]
        dxb = after_start(emit(l, g, "mixer"), dxb)
    _, _, dg_mem = _rms_bwd(mem, vec(w["g_mem"]), dmem_n, None)
    g["g_mem"] = [one(dg_mem[0])]
    return loss, dx, g


SHARDED = {
    "w_in_a": ((N_A, D_MODEL, 2 * CONV_CH + MEM_W), 2),
    "b_glu": ((N_A, 2 * CONV_CH), 1),
    "w_dw_a": ((N_A, CONV_W, CONV_CH), 2),
    "b_dw_a": ((N_A, CONV_CH), 1),
    "ln_g": ((N_A, CONV_CH), 1),
    "ln_b": ((N_A, CONV_CH), 1),
    "w_kvf": ((D_MODEL, 2 * CONV_CH + N_FOX_HEADS), 1),
    "w_in_b": ((N_LAYERS - N_A, D_MODEL, CONV_CH + MEM_W), 1),
    "w_mem_kv": ((N_LAYERS, D_MODEL, 2 * MEM_W), 1),
    "w_out": ((N_LAYERS, D_MODEL, D_MODEL), 1),
    "w_up": ((N_LAYERS, D_MODEL, 2 * D_FF), 2),
    "w_dw_f": ((N_LAYERS, FFN_CONV_W, 2 * D_FF), 2),
    "w_down": ((N_LAYERS, D_FF, D_MODEL), 1),
}
REPLICATED = {
    "g_mix": (N_LAYERS, D_MODEL), "g_kv": (D_MODEL,), "b_f": (N_FOX_HEADS,), "g_mem": (D_MODEL,),
    "g_ffn": (N_LAYERS, D_MODEL), "b_dw_f": (N_LAYERS, 2 * D_FF), "g_final": (D_MODEL,),
}
WEIGHT_ORDER = ["g_mix", "w_in_a", "b_glu", "w_dw_a", "b_dw_a", "ln_g", "ln_b", "g_kv", "w_kvf", "b_f", "w_in_b", "g_mem",
                "w_mem_kv", "w_out", "g_ffn", "w_up", "w_dw_f", "b_dw_f", "w_down", "g_final"]


def _shard_shape(name):
    shape, axis = SHARDED[name]
    return tuple(d // N_CHIPS if i == axis else d for i, d in enumerate(shape))


def _size(shape):
    n = 1
    for d in shape:
        n *= d
    return n


VECTOR_LEAVES = ("b_glu", "w_dw_a", "b_dw_a", "ln_g", "ln_b", "w_dw_f")
_HALF_ROW_TILE = 16
_VECTOR_ELEMS = sum(_size(_shard_shape(n)) for n in VECTOR_LEAVES)
VECTOR_ROWS = -(-_VECTOR_ELEMS // (FLAT_COLS * 2 * _HALF_ROW_TILE)) * 2 * _HALF_ROW_TILE
REP_ELEMS = sum(_size(s) for s in REPLICATED.values())
REP_ROWS = 8
REP_COLS = -(-REP_ELEMS // (REP_ROWS * LANE)) * LANE


def _flatten_vectors(parts, dtype):
    flat = jnp.concatenate([p.reshape(-1).astype(dtype) for p in parts])
    return jnp.pad(flat, (0, VECTOR_ROWS * FLAT_COLS - _VECTOR_ELEMS)).reshape(2, VECTOR_ROWS // 2, FLAT_COLS)


def _unflatten_vectors(flat, lead=()):
    flat = flat.reshape(lead + (-1,))
    out, off = {}, 0
    for n in VECTOR_LEAVES:
        shp = _shard_shape(n)
        out[n] = flat[..., off:off + _size(shp)].reshape(lead + shp)
        off += _size(shp)
    return out


def _span(parts, cat_axis, sel_axis, lo, hi):
    if sel_axis != cat_axis:
        return jnp.concatenate([lax.slice_in_dim(p, lo, hi, axis=sel_axis) for p in parts], axis=cat_axis)
    taken, off = [], 0
    for p in parts:
        n = p.shape[cat_axis]
        a, b = max(lo, off), min(hi, off + n)
        if a < b:
            taken.append(lax.slice_in_dim(p, a - off, b - off, axis=cat_axis))
        off += n
    return taken[0] if len(taken) == 1 else jnp.concatenate(taken, axis=cat_axis)


FFN_LEAVES = 2
LAYER_LEAVES = tuple(
    ([("w_in_a", l)] if l < N_A else ([("w_kvf", None)] if l == N_A else []) + [("w_in_b", l - N_A)])
    + [("w_mem_kv", l), ("w_out", l), ("w_up", l), ("w_down", l)]
    for l in range(N_LAYERS))


def _layer_halves(weights, leaves):
    out = []
    for name, idx in leaves:
        shard = weights[name] if idx is None else weights[name][idx]
        rows, cols = shard.shape
        assert (rows // 2) % _HALF_ROW_TILE == 0, name
        out.append(shard.astype(BF16).reshape(2, rows // 2, cols))
    return out


def _layer_pieces(l, gathered, vectors):
    parts = {}
    for (name, idx), g in zip(LAYER_LEAVES[l], gathered):
        if g is not None:
            parts[name] = [g[t].reshape(-1, g.shape[-1]) for t in range(N_CHIPS)]

    def cut(name, sel_axis, lo, hi):
        axis = SHARDED[name][1] - (len(SHARDED[name][0]) - 2)
        return _span(parts[name], axis, sel_axis, lo, hi)

    def vcut(name, idx, sel_axis, lo, hi):
        axis = SHARDED[name][1] - 1
        return _span([vectors[name][t, idx] for t in range(N_CHIPS)], axis, sel_axis, lo, hi)

    def pad_rows(arr, rows):
        return jnp.pad(arr, ((0, rows - arr.shape[0]), (0, 0)))

    w = {}
    if "w_up" in parts:
        w["w_up_g"] = cut("w_up", 1, 0, D_FF)
        w["w_up_v"] = cut("w_up", 1, D_FF, 2 * D_FF)
        w["w_dw_f_g"] = pad_rows(vcut("w_dw_f", l, 1, 0, D_FF), FFN_PAD)
        w["w_dw_f_v"] = pad_rows(vcut("w_dw_f", l, 1, D_FF, 2 * D_FF), FFN_PAD)
        w["w_down"] = cut("w_down", 0, 0, D_FF)
    if "w_out" not in parts:
        return w
    if l < N_A:
        w["w_a"] = cut("w_in_a", 1, 0, CONV_CH)
        w["w_g"] = cut("w_in_a", 1, CONV_CH, 2 * CONV_CH)
        w["w_qm"] = cut("w_in_a", 1, 2 * CONV_CH, 2 * CONV_CH + MEM_W)
        w["b_glu_a"] = vcut("b_glu", l, 0, 0, CONV_CH)
        w["b_glu_g"] = vcut("b_glu", l, 0, CONV_CH, 2 * CONV_CH)
        w["w_dw_a"] = pad_rows(vcut("w_dw_a", l, 1, 0, CONV_CH), CONV_PAD)
        for n in ("b_dw_a", "ln_g", "ln_b"):
            w[n] = vcut(n, l, 0, 0, CONV_CH)
    else:
        if l == N_A:
            w["w_k"] = cut("w_kvf", 1, 0, CONV_CH)
            w["w_v"] = cut("w_kvf", 1, CONV_CH, 2 * CONV_CH)
            w["w_f"] = jnp.pad(cut("w_kvf", 1, 2 * CONV_CH, 2 * CONV_CH + N_FOX_HEADS), ((0, 0), (0, F_PAD - N_FOX_HEADS)))
        w["w_q"] = cut("w_in_b", 1, 0, CONV_CH)
        w["w_qm"] = cut("w_in_b", 1, CONV_CH, CONV_CH + MEM_W)
    w["w_mk"] = cut("w_mem_kv", 1, 0, MEM_W)
    w["w_mv"] = cut("w_mem_kv", 1, MEM_W, 2 * MEM_W)
    w["w_o_mix"] = cut("w_out", 0, 0, CONV_CH)
    w["w_o_mem"] = cut("w_out", 0, CONV_CH, D_MODEL)
    return w


_PER_LAYER = ("w_a", "w_g", "w_q", "w_qm", "b_glu_a", "b_glu_g", "w_dw_a", "b_dw_a", "ln_g", "ln_b", "w_mk", "w_mv", "w_o_mix", "w_o_mem",
              "w_up_g", "w_up_v", "w_dw_f_g", "w_dw_f_v", "w_down")


def _weight_table(rep):
    w = dict(rep)
    w["b_f"] = jnp.pad(rep["b_f"], (0, F_PAD - N_FOX_HEADS)).reshape(1, F_PAD)
    w["b_dw_f_g"], w["b_dw_f_v"] = rep["b_dw_f"][:, :D_FF], rep["b_dw_f"][:, D_FF:]
    for n in _PER_LAYER:
        w[n] = {}
    return w


def _install_layer(w, l, pieces):
    for n, p in pieces.items():
        if n in _PER_LAYER:
            w[n][l - N_A if n == "w_q" else l] = p
        else:
            w[n] = p


def _shard_of(entry, name, t):
    shape, axis = SHARDED[name]
    pieces, cat_axis = entry
    axis -= len(shape) - len(pieces[0].shape)
    step = shape[SHARDED[name][1]] // N_CHIPS
    return _span(pieces, cat_axis, axis, t * step, (t + 1) * step)


def _layer_slabs(grads, leaves):
    out = []
    for name, idx in leaves:
        shards = [_shard_of(grads[name][0 if idx is None else idx], name, t).astype(BF16) for t in range(N_CHIPS)]
        rows, cols = shards[0].shape
        out.append(jnp.stack(shards).reshape(N_CHIPS, 2, rows // 2, cols))
    return out


def _vector_slabs(grads):
    def shard(name, t):
        return jnp.stack([_shard_of(entry, name, t) for entry in grads[name]])
    return jnp.stack([_flatten_vectors([shard(n, t) for n in VECTOR_LEAVES], BF16) for t in range(N_CHIPS)])


def _replicated_rows(grads):
    parts = [p.reshape(-1) for n in REPLICATED for pieces, _ in grads[n] for p in pieces]
    flat = jnp.concatenate(parts)
    return jnp.pad(flat, (0, REP_ROWS * REP_COLS - REP_ELEMS)).reshape(REP_ROWS, REP_COLS)


_ANY = pl.BlockSpec(memory_space=pl.ANY)


def _place():
    x, y, c = lax.axis_index("x"), lax.axis_index("y"), lax.axis_index("c")
    chips = [(1 - x, y), (x, 1 - y), (1 - x, 1 - y)]
    return x, y, c, chips


def _chip_index(chip):
    return 2 * chip[0] + chip[1]


def _gather_leaves(halves):
    n = len(halves)

    def body(*refs):
        w_refs, out_refs, (send_sems, recv_sems) = refs[:n], refs[n:2 * n], refs[2 * n:]
        x, y, c, chips = _place()
        me = 2 * x + y
        sibling = (x, y, 1 - c)

        def copy(a, sem, chip_idx, half, to, src=None):
            dst = out_refs[a].at[chip_idx, half]
            return pltpu.make_async_remote_copy(src_ref=dst if src is None else src, dst_ref=dst, send_sem=send_sems.at[6 * a + sem],
                                                recv_sem=recv_sems.at[6 * a + sem], device_id=to, device_id_type=MESH_ID)

        first = [copy(a, j, me, c, (*chip, c), src=w_refs[a].at[c]) for j, chip in enumerate(chips) for a in range(n)]
        for cp in first:
            cp.start()
        passed = []
        for j, chip in enumerate(chips):
            for a in range(n):
                copy(a, j, _chip_index(chip), c, (x, y, c)).wait_recv()
                passed.append(copy(a, 3 + j, _chip_index(chip), c, sibling))
                passed[-1].start()
        for j, chip in enumerate(chips):
            for a in range(n):
                copy(a, 3 + j, _chip_index(chip), 1 - c, (x, y, c)).wait_recv()
        for cp in first + passed:
            cp.wait_send()

    return pl.pallas_call(
        body, out_shape=[jax.ShapeDtypeStruct((N_CHIPS,) + h.shape, h.dtype) for h in halves],
        in_specs=[_ANY] * n, out_specs=[_ANY] * n,
        scratch_shapes=[pltpu.SemaphoreType.DMA((6 * n,)), pltpu.SemaphoreType.DMA((6 * n,))], name="gather_leaves",
    )(*halves)


_HBM = pl.BlockSpec(memory_space=pltpu.HBM)
_SEM = pl.BlockSpec(memory_space=pltpu.SEMAPHORE)


def _in_hbm(a):
    return pltpu.with_memory_space_constraint(a, pltpu.HBM)


def _gather_start(groups, after):
    flat = [h for g in groups for h in g]
    n, ng = len(flat), len(groups)

    def body(*refs):
        srcs, lands, sems = refs[1:1 + n], refs[1 + n:1 + 2 * n], refs[1 + 2 * n:1 + 2 * n + 2 * ng]
        token = refs[-1]
        x, y, c, chips = _place()
        me = 2 * x + y
        a = 0
        for gi, g in enumerate(groups):
            for k in range(len(g)):
                for j, chip in enumerate(chips):
                    pltpu.make_async_remote_copy(src_ref=srcs[a].at[c], dst_ref=lands[a].at[me, c], send_sem=sems[2 * gi].at[3 * k + j],
                                                 recv_sem=sems[2 * gi + 1].at[3 * k + j], device_id=(*chip, c), device_id_type=MESH_ID).start()
                a += 1
        token[...] = jnp.zeros_like(token)

    sem_shapes = [pltpu.SemaphoreType.DMA((3 * len(g),)) for g in groups for _ in range(2)]
    out = pl.pallas_call(
        body, name="gather_start",
        out_shape=sem_shapes + [pltpu.HBM(h.shape, h.dtype) for h in flat] + [pltpu.HBM((N_CHIPS,) + h.shape, h.dtype) for h in flat]
        + [jax.ShapeDtypeStruct((8, LANE), F32)],
        in_specs=[_ANY] + [_HBM] * (2 * n), out_specs=[_SEM] * (2 * ng) + [_HBM] * (2 * n) + [pl.BlockSpec(memory_space=pltpu.VMEM)],
        input_output_aliases={1 + i: 2 * ng + i for i in range(2 * n)},
        compiler_params=pltpu.CompilerParams(has_side_effects=pltpu.SideEffectType.DATAFLOW_SIDE_EFFECTING),
    )(after, *[_in_hbm(h) for h in flat], *[_in_hbm(lax.empty((N_CHIPS,) + h.shape, h.dtype)) for h in flat])
    sems, srcs, lands = out[:2 * ng], out[2 * ng:2 * ng + n], out[2 * ng + n:2 * ng + 2 * n]
    res, a = [], 0
    for gi, g in enumerate(groups):
        res.append((sems[2 * gi], sems[2 * gi + 1], list(srcs[a:a + len(g)]), list(lands[a:a + len(g)])))
        a += len(g)
    return res


def _gather_wait(send_sems, recv_sems, srcs, lands, after, name):
    n = len(srcs)

    def body(*refs):
        src_refs, land_refs, send, recv = refs[:n], refs[n:2 * n], refs[2 * n], refs[2 * n + 1]
        x, y, c, chips = _place()
        for k in range(n):
            for j, chip in enumerate(chips):
                cp = pltpu.make_async_remote_copy(src_ref=src_refs[k].at[c], dst_ref=land_refs[k].at[_chip_index(chip), c], send_sem=send.at[3 * k + j],
                                                  recv_sem=recv.at[3 * k + j], device_id=(*chip, c), device_id_type=MESH_ID)
                cp.wait_send()
                cp.wait_recv()

    out = pl.pallas_call(
        body, name=name, out_shape=[pltpu.HBM(a.shape, a.dtype) for a in list(srcs) + list(lands)],
        in_specs=[_HBM] * (2 * n) + [_SEM, _SEM, _ANY], out_specs=[_HBM] * (2 * n), input_output_aliases={i: i for i in range(2 * n)},
        compiler_params=pltpu.CompilerParams(has_side_effects=pltpu.SideEffectType.DATAFLOW_SIDE_EFFECTING),
    )(*srcs, *lands, send_sems, recv_sems, after)
    return list(out[n:])


def _forward_halves(lands, name):
    n = len(lands)

    def body(*refs):
        out_refs, (send_sems, recv_sems) = refs[n:2 * n], refs[2 * n:]
        x, y, c, chips = _place()

        def copy(a, j, half, to):
            blk = out_refs[a].at[_chip_index(chips[j]), half]
            return pltpu.make_async_remote_copy(src_ref=blk, dst_ref=blk, send_sem=send_sems.at[3 * a + j], recv_sem=recv_sems.at[3 * a + j],
                                                device_id=to, device_id_type=MESH_ID)

        sends = [copy(a, j, c, (x, y, 1 - c)) for a in range(n) for j in range(3)]
        for cp in sends:
            cp.start()
        for a in range(n):
            for j in range(3):
                copy(a, j, 1 - c, (x, y, c)).wait_recv()
        for cp in sends:
            cp.wait_send()

    return pl.pallas_call(
        body, out_shape=[jax.ShapeDtypeStruct(a.shape, a.dtype) for a in lands], in_specs=[_ANY] * n, out_specs=[_ANY] * n,
        input_output_aliases={i: i for i in range(n)},
        scratch_shapes=[pltpu.SemaphoreType.DMA((3 * n,)), pltpu.SemaphoreType.DMA((3 * n,))], name=name,
    )(*lands)


def _swap_halves(slabs):
    n = len(slabs)

    def body(*refs):
        g_refs, got_refs, (send_sem, recv_sem) = refs[:n], refs[n:2 * n], refs[2 * n:]
        x, y, c, _ = _place()
        copies = [pltpu.make_async_remote_copy(src_ref=g_refs[a].at[t, 1 - c], dst_ref=got_refs[a].at[t], send_sem=send_sem.at[N_CHIPS * a + t],
                                               recv_sem=recv_sem.at[N_CHIPS * a + t], device_id=(x, y, 1 - c), device_id_type=MESH_ID)
                  for a in range(n) for t in range(N_CHIPS)]
        for cp in copies:
            cp.start()
        for cp in copies:
            cp.wait()

    return pl.pallas_call(
        body, out_shape=[jax.ShapeDtypeStruct((g.shape[0],) + g.shape[2:], g.dtype) for g in slabs],
        in_specs=[_ANY] * n, out_specs=[_ANY] * n,
        scratch_shapes=[pltpu.SemaphoreType.DMA((N_CHIPS * n,)), pltpu.SemaphoreType.DMA((N_CHIPS * n,))], name="swap_halves",
    )(*slabs)


def _scatter_start(parts, name):
    n = len(parts)

    def body(*refs):
        srcs, lands, send, recv = refs[:n], refs[n:2 * n], refs[2 * n], refs[2 * n + 1]
        token = refs[-1]
        x, y, c, chips = _place()
        me = 2 * x + y
        for k in range(n):
            for j, chip in enumerate(chips):
                pltpu.make_async_remote_copy(src_ref=srcs[k].at[_chip_index(chip)], dst_ref=lands[k].at[me], send_sem=send.at[3 * k + j],
                                             recv_sem=recv.at[3 * k + j], device_id=(*chip, c), device_id_type=MESH_ID).start()
        token[...] = jnp.zeros_like(token)

    out = pl.pallas_call(
        body, name=name,
        out_shape=[pltpu.SemaphoreType.DMA((3 * n,)), pltpu.SemaphoreType.DMA((3 * n,))] + [pltpu.HBM(p.shape, p.dtype) for p in parts] * 2
        + [jax.ShapeDtypeStruct((8, LANE), F32)],
        in_specs=[_HBM] * (2 * n), out_specs=[_SEM, _SEM] + [_HBM] * (2 * n) + [pl.BlockSpec(memory_space=pltpu.VMEM)],
        input_output_aliases={i: 2 + i for i in range(2 * n)},
        compiler_params=pltpu.CompilerParams(has_side_effects=pltpu.SideEffectType.DATAFLOW_SIDE_EFFECTING),
    )(*[_in_hbm(p) for p in parts], *[_in_hbm(lax.empty(p.shape, p.dtype)) for p in parts])
    return (out[0], out[1], list(out[2:2 + n]), list(out[2 + n:2 + 2 * n])), out[-1]


def _scatter_wait(send_sems, recv_sems, srcs, lands, after, name):
    n = len(srcs)

    def body(*refs):
        src_refs, land_refs, send, recv = refs[:n], refs[n:2 * n], refs[2 * n], refs[2 * n + 1]
        x, y, c, chips = _place()
        for k in range(n):
            for j, chip in enumerate(chips):
                cp = pltpu.make_async_remote_copy(src_ref=src_refs[k].at[_chip_index(chip)], dst_ref=land_refs[k].at[_chip_index(chip)],
                                                  send_sem=send.at[3 * k + j], recv_sem=recv.at[3 * k + j], device_id=(*chip, c), device_id_type=MESH_ID)
                cp.wait_send()
                cp.wait_recv()

    out = pl.pallas_call(
        body, name=name, out_shape=[pltpu.HBM(a.shape, a.dtype) for a in list(srcs) + list(lands)],
        in_specs=[_HBM] * (2 * n) + [_SEM, _SEM, _ANY], out_specs=[_HBM] * (2 * n), input_output_aliases={i: i for i in range(2 * n)},
        compiler_params=pltpu.CompilerParams(has_side_effects=pltpu.SideEffectType.DATAFLOW_SIDE_EFFECTING),
    )(*srcs, *lands, send_sems, recv_sems, after)
    return list(out[n:])


def _sibling_halves(halves):
    n = len(halves)

    def body(*refs):
        h_refs, got_refs, (send_sem, recv_sem) = refs[:n], refs[n:2 * n], refs[2 * n:]
        x, y, c, _ = _place()
        copies = [pltpu.make_async_remote_copy(src_ref=h_refs[a], dst_ref=got_refs[a], send_sem=send_sem.at[a], recv_sem=recv_sem.at[a],
                                               device_id=(x, y, 1 - c), device_id_type=MESH_ID) for a in range(n)]
        for cp in copies:
            cp.start()
        for cp in copies:
            cp.wait()

    return pl.pallas_call(
        body, out_shape=[jax.ShapeDtypeStruct(h.shape, h.dtype) for h in halves], in_specs=[_ANY] * n, out_specs=[_ANY] * n,
        scratch_shapes=[pltpu.SemaphoreType.DMA((n,)), pltpu.SemaphoreType.DMA((n,))], name="sibling_halves",
    )(*halves)


def _gather_replicated(rows):
    m_per, n = rows.shape

    def body(x_ref, out_ref, send_sems, recv_sems, local_sem):
        x, y, c, chips = _place()
        me, sibling = (x, y, c), (x, y, 1 - c)

        def block(px, py, pc):
            return out_ref.at[pl.ds((4 * px + 2 * py + pc) * m_per, m_per), :]

        def copy(k, blk, to, src=None):
            return pltpu.make_async_remote_copy(src_ref=block(*blk) if src is None else src, dst_ref=block(*blk),
                                                send_sem=send_sems.at[k], recv_sem=recv_sems.at[k], device_id=to, device_id_type=MESH_ID)

        mine = pltpu.make_async_copy(x_ref, block(*me), local_sem)
        mine.start()
        first = [copy(0, me, sibling, src=x_ref)]
        first += [copy(1 + j, me, (*chip, c), src=x_ref) for j, chip in enumerate(chips)]
        for cp in first:
            cp.start()
        passed = [copy(4 + j, (*chip, c), sibling) for j, chip in enumerate(chips)]
        for j, chip in enumerate(chips):
            copy(1 + j, (*chip, c), me).wait_recv()
            passed[j].start()
        copy(0, sibling, me).wait_recv()
        for j, chip in enumerate(chips):
            copy(4 + j, (*chip, 1 - c), me).wait_recv()
        for cp in first + passed:
            cp.wait_send()
        mine.wait()

    vmem = pl.BlockSpec(memory_space=pltpu.VMEM)
    return pl.pallas_call(
        body, out_shape=jax.ShapeDtypeStruct((N_DEV * m_per, n), rows.dtype), in_specs=[vmem], out_specs=vmem,
        scratch_shapes=[pltpu.SemaphoreType.DMA((7,)), pltpu.SemaphoreType.DMA((7,)), pltpu.SemaphoreType.DMA],
        name="gather_replicated",
    )(rows)


_SUM_ROWS = 256


def _tile_rows(rows):
    if rows <= _SUM_ROWS:
        return rows
    best = None
    for r in range(16, _SUM_ROWS + 1, 16):
        if rows % r == 0:
            best = r
    assert best is not None, rows
    return best


def _add_pairs(slabs, got, core):
    n, _, rows, cols = slabs.shape
    tr = _tile_rows(rows)

    def body(core_ref, a_ref, b_ref, o_ref):
        o_ref[...] = (a_ref[0].astype(F32) + b_ref[...].astype(F32)).astype(BF16)

    spec = pl.BlockSpec((1, tr, cols), lambda t, i, core_ref: (t, i, 0))
    return pl.pallas_call(
        body, out_shape=jax.ShapeDtypeStruct((n, rows, cols), BF16),
        grid_spec=pltpu.PrefetchScalarGridSpec(
            num_scalar_prefetch=1, grid=(n, rows // tr),
            in_specs=[pl.BlockSpec((1, 1, tr, cols), lambda t, i, core_ref: (t, core_ref[0], i, 0)), spec], out_specs=spec),
        compiler_params=_params(("parallel", "parallel")), name=f"add_pairs_{rows}x{cols}",
    )(core, slabs, got)


def _sum_slabs(slabs):
    n, rows, cols = slabs.shape
    tr = _tile_rows(rows)

    def body(s_ref, o_ref):
        acc = s_ref[0].astype(F32)
        for t in range(1, n):
            acc = acc + s_ref[t].astype(F32)
        o_ref[...] = acc

    return pl.pallas_call(
        body, out_shape=jax.ShapeDtypeStruct((rows, cols), F32), grid=(rows // tr,),
        in_specs=[pl.BlockSpec((n, tr, cols), lambda i: (0, i, 0))], out_specs=pl.BlockSpec((tr, cols), lambda i: (i, 0)),
        compiler_params=_params(("parallel",)), name=f"sum_slabs_{n}x{rows}x{cols}",
    )(slabs)


_ADAM_BLOCK_BYTES = 2 * 1024 * 1024


def _adamw(w, g, m, v):
    shape = w.shape
    cols = shape[-1]
    rows = _size(shape) // cols
    tr = rows
    if rows * cols * 4 > _ADAM_BLOCK_BYTES:
        for r in range(8, rows, 8):
            if rows % r == 0 and r * cols * 4 <= _ADAM_BLOCK_BYTES:
                tr = r

    def body(w_ref, g_ref, m_ref, v_ref, d_ref, nm_ref, nv_ref):
        gv = g_ref[...]
        nm = ADAM_B1 * m_ref[...] + (1.0 - ADAM_B1) * gv
        nv = ADAM_B2 * v_ref[...] + (1.0 - ADAM_B2) * jnp.square(gv)
        m_hat = nm / (1.0 - ADAM_B1 ** ADAM_STEP)
        v_hat = nv / (1.0 - ADAM_B2 ** ADAM_STEP)
        d_ref[...] = -ADAM_LR * (m_hat / (jnp.sqrt(v_hat) + ADAM_EPS) + ADAM_WD * w_ref[...])
        nm_ref[...] = nm
        nv_ref[...] = nv

    spec = pl.BlockSpec((tr, cols), lambda i: (i, 0))
    out = pl.pallas_call(
        body, out_shape=tuple(jax.ShapeDtypeStruct((rows, cols), F32) for _ in range(3)), grid=(rows // tr,),
        in_specs=[spec] * 4, out_specs=(spec,) * 3, compiler_params=_params(("parallel",)), name=f"adamw_{rows}x{cols}",
    )(*(a.reshape(rows, cols) for a in (w, g, m, v)))
    return tuple(o.reshape(shape) for o in out)


def kernel(x, mem, g_mix, w_in_a, b_glu, w_dw_a, b_dw_a, ln_g, ln_b, g_kv, w_kvf, b_f, w_in_b, g_mem, w_mem_kv, w_out, g_ffn, w_up, w_dw_f, b_dw_f, w_down, g_final, loss_target, m_g_mix, m_w_in_a, m_b_glu, m_w_dw_a, m_b_dw_a, m_ln_g, m_ln_b, m_g_kv, m_w_kvf, m_b_f, m_w_in_b, m_g_mem, m_w_mem_kv, m_w_out, m_g_ffn, m_w_up, m_w_dw_f, m_b_dw_f, m_w_down, m_g_final, v_g_mix, v_w_in_a, v_b_glu, v_w_dw_a, v_b_dw_a, v_ln_g, v_ln_b, v_g_kv, v_w_kvf, v_b_f, v_w_in_b, v_g_mem, v_w_mem_kv, v_w_out, v_g_ffn, v_w_up, v_w_dw_f, v_b_dw_f, v_w_down, v_g_final):
    weights = dict(g_mix=g_mix, w_in_a=w_in_a, b_glu=b_glu, w_dw_a=w_dw_a, b_dw_a=b_dw_a, ln_g=ln_g, ln_b=ln_b, g_kv=g_kv,
                   w_kvf=w_kvf, b_f=b_f, w_in_b=w_in_b, g_mem=g_mem, w_mem_kv=w_mem_kv, w_out=w_out, g_ffn=g_ffn, w_up=w_up,
                   w_dw_f=w_dw_f, b_dw_f=b_dw_f, w_down=w_down, g_final=g_final)
    mom1 = dict(g_mix=m_g_mix, w_in_a=m_w_in_a, b_glu=m_b_glu, w_dw_a=m_w_dw_a, b_dw_a=m_b_dw_a, ln_g=m_ln_g, ln_b=m_ln_b,
                g_kv=m_g_kv, w_kvf=m_w_kvf, b_f=m_b_f, w_in_b=m_w_in_b, g_mem=m_g_mem, w_mem_kv=m_w_mem_kv, w_out=m_w_out,
                g_ffn=m_g_ffn, w_up=m_w_up, w_dw_f=m_w_dw_f, b_dw_f=m_b_dw_f, w_down=m_w_down, g_final=m_g_final)
    mom2 = dict(g_mix=v_g_mix, w_in_a=v_w_in_a, b_glu=v_b_glu, w_dw_a=v_w_dw_a, b_dw_a=v_b_dw_a, ln_g=v_ln_g, ln_b=v_ln_b,
                g_kv=v_g_kv, w_kvf=v_w_kvf, b_f=v_b_f, w_in_b=v_w_in_b, g_mem=v_g_mem, w_mem_kv=v_w_mem_kv, w_out=v_w_out,
                g_ffn=v_g_ffn, w_up=v_w_up, w_dw_f=v_w_dw_f, b_dw_f=v_b_dw_f, w_down=v_w_down, g_final=v_g_final)

    x_pos, y_pos, core = lax.axis_index("x"), lax.axis_index("y"), lax.axis_index("c")
    chip = 2 * x_pos + y_pos

    mixer_of = lambda l: LAYER_LEAVES[l][:-FFN_LEAVES]
    ffn_of = lambda l: LAYER_LEAVES[l][-FFN_LEAVES:]
    local_first = _layer_halves(weights, mixer_of(0))
    vector_halves = _flatten_vectors([weights[n] for n in VECTOR_LEAVES], F32)
    first = _gather_leaves(local_first + [vector_halves])
    later = [ffn_of(0)] + [LAYER_LEAVES[l] for l in range(1, N_LAYERS)]
    local_later = [_layer_halves(weights, leaves) for leaves in later]
    in_flight = _gather_start(local_later, first[-1])

    def with_own(gathered, own):
        return [lax.dynamic_update_index_in_dim(g, o[None], chip, 0) for g, o in zip(gathered, own)]

    first = with_own(first, local_first + [vector_halves])
    vectors = _unflatten_vectors(first[-1], (N_CHIPS,))
    table = _weight_table({n: weights[n] for n in REPLICATED})

    def fetch(k, after):
        send_sems, recv_sems, srcs, lands = in_flight[k]
        lands = _gather_wait(send_sems, recv_sems, srcs, lands, after, f"gather_wait_{k}")
        return with_own(_forward_halves(lands, f"forward_halves_{k}"), local_later[k])

    def arrive(l, after, part):
        if l == 0:
            gathered = first[:-1] + [None] * FFN_LEAVES if part == "mixer" else [None] * len(mixer_of(0)) + fetch(0, after)
        elif part == "mixer":
            gathered = fetch(l, after)
        else:
            return
        _install_layer(table, l, _layer_pieces(l, gathered, vectors))

    core_index = core.astype(jnp.int32).reshape(1)
    in_flight_back = []

    def emit(l, g, part):
        if l == 0:
            leaves = ffn_of(0) if part == "ffn" else mixer_of(0)
        elif part == "mixer":
            leaves = LAYER_LEAVES[l]
        else:
            return None
        with_vectors = l == 0 and part == "mixer"
        slabs = _layer_slabs(g, leaves) + ([_vector_slabs(g)] if with_vectors else [])
        chip_sums = [_add_pairs(mine, got, core_index) for mine, got in zip(slabs, _swap_halves(slabs))]
        handles, started = _scatter_start(chip_sums, f"scatter_start_{len(in_flight_back)}")
        in_flight_back.append((leaves, with_vectors, chip_sums, handles))
        return started

    loss, dx, grads = _local_step(x[0], mem[0], loss_target[0], table, arrive, emit)
    loss = lax.psum(loss, ("x", "y", "c"))

    names, mine = [], []
    for k, (leaves, with_vectors, chip_sums, (send_sems, recv_sems, srcs, lands)) in enumerate(in_flight_back):
        lands = _scatter_wait(send_sems, recv_sems, srcs, lands, dx, f"scatter_wait_{k}")
        from_chips = [lax.dynamic_update_index_in_dim(got, lax.dynamic_index_in_dim(own, chip, 0, keepdims=True), chip, 0)
                      for got, own in zip(lands, chip_sums)]
        mine += [_sum_slabs(f) for f in from_chips]
        names += list(leaves) + ([("vectors", None)] if with_vectors else [])
    theirs = _sibling_halves(mine)
    per_leaf = {}
    for (name, idx), m, t in zip(names, mine, theirs):
        per_leaf.setdefault(name, {})[idx] = jnp.concatenate([jnp.where(core == 0, m, t), jnp.where(core == 0, t, m)])
    vector_grads = per_leaf.pop("vectors")[None]
    grad_leaves = {n: (d[None] if None in d else jnp.stack([d[i] for i in sorted(d)])) for n, d in per_leaf.items()}
    grad_leaves.update(_unflatten_vectors(vector_grads))

    rep_sum = _sum_slabs(_gather_replicated(_replicated_rows(grads)).reshape(N_DEV, REP_ROWS, REP_COLS)).reshape(-1)
    off = 0
    for n, shp in REPLICATED.items():
        grad_leaves[n] = rep_sum[off:off + _size(shp)].reshape(shp)
        off += _size(shp)

    deltas, new_m, new_v = {}, {}, {}
    for n in WEIGHT_ORDER:
        deltas[n], new_m[n], new_v[n] = _adamw(weights[n], grad_leaves[n], mom1[n], mom2[n])
    return (loss, dx[None], *[grad_leaves[n] for n in WEIGHT_ORDER], *[deltas[n] for n in WEIGHT_ORDER],
            *[new_m[n] for n in WEIGHT_ORDER], *[new_v[n] for n in WEIGHT_ORDER])
```

```python
import jax
import jax.numpy as jnp
from jax import lax
from jax.experimental import pallas as pl
from jax.experimental.pallas import tpu as pltpu

F32 = jnp.float32
BF16 = jnp.bfloat16

D_MODEL = 1024
N_LAYERS = 4
N_A = 2
CONV_CH = 768
MEM_W = 256
HEAD_DIM = 64
N_MEM_HEADS = 4
N_FOX_HEADS = 12
N_HEAD_PAIRS = N_FOX_HEADS // 2
D_FF = 2816
CONV_W = 31
CONV_PAD = 32
FFN_CONV_W = 3
FFN_PAD = 8
F_PAD = 128
RMS_EPS = 1e-6
LN_EPS = 1e-5
ATT_SCALE = HEAD_DIM ** -0.5
NEG_BIG = -1e30

ADAM_LR = 0.001
ADAM_B1 = 0.9
ADAM_B2 = 0.999
ADAM_EPS = 1e-08
ADAM_WD = 0.01
ADAM_STEP = 10

LANE = 128
ROW_TILE = 256
CHUNK = 128
VMEM_LIMIT = 48 * 1024 * 1024
FLAT_COLS = 1024
N_CHIPS = 4
N_DEV = 8
MESH_ID = pl.DeviceIdType.MESH


def _params(sem=None):
    return pltpu.CompilerParams(dimension_semantics=sem, vmem_limit_bytes=VMEM_LIMIT)


def _tile(dim, pref):
    if dim <= pref:
        return dim
    best = None
    for m in range(1, dim // LANE + 1):
        d = m * LANE
        if dim % d == 0 and d <= pref:
            best = d
    assert best is not None, dim
    return best


_DIMS = {"nn": (((1,), (0,)), ((), ())), "nt": (((1,), (1,)), ((), ())), "tn": (((0,), (0,)), ((), ()))}


_MM_RESIDENT_BYTES = 8 * 1024 * 1024
_MM_STREAM_BYTES = 6 * 1024 * 1024
_MM_OUT_BYTES = 6 * 1024 * 1024


def _mm_tiles(m, n, k, mode, a_size, b_size, o_size):
    if mode == "tn":
        tm = _tile(m, _MM_RESIDENT_BYTES // (k * a_size))
        tn = _tile(n, min(_MM_STREAM_BYTES // (k * b_size), _MM_OUT_BYTES // (tm * o_size)))
    else:
        tn = _tile(n, _MM_RESIDENT_BYTES // (k * b_size))
        tm = _tile(m, min(_MM_STREAM_BYTES // (k * a_size), _MM_OUT_BYTES // (tn * o_size), 512))
    return tm, tn


def _mm(a, b, mode="nn", out_dtype=F32, add=None):
    if mode == "nn":
        (m, k), (k2, n) = a.shape, b.shape
    elif mode == "nt":
        (m, k), (n, k2) = a.shape, b.shape
    else:
        (k, m), (k2, n) = a.shape, b.shape
    assert k == k2, (a.shape, b.shape, mode)
    tm, tn = _mm_tiles(m, n, k, mode, a.dtype.itemsize, b.dtype.itemsize, jnp.dtype(out_dtype).itemsize)
    dims = _DIMS[mode]
    has_add = add is not None

    def body(*refs):
        if has_add:
            a_ref, b_ref, add_ref, o_ref = refs
        else:
            a_ref, b_ref, o_ref = refs
        r = lax.dot_general(a_ref[...].astype(BF16), b_ref[...].astype(BF16), dims, preferred_element_type=F32)
        if has_add:
            r = r + add_ref[...]
        o_ref[...] = r.astype(out_dtype)

    a_spec = pl.BlockSpec((k, tm), lambda i, j: (0, i)) if mode == "tn" else pl.BlockSpec((tm, k), lambda i, j: (i, 0))
    b_spec = pl.BlockSpec((tn, k), lambda i, j: (j, 0)) if mode == "nt" else pl.BlockSpec((k, tn), lambda i, j: (0, j))
    o_spec = pl.BlockSpec((tm, tn), lambda i, j: (i, j))
    in_specs = [a_spec, b_spec] + ([o_spec] if has_add else [])
    args = (a, b) + ((add,) if has_add else ())
    return pl.pallas_call(
        body,
        out_shape=jax.ShapeDtypeStruct((m, n), out_dtype),
        grid=(m // tm, n // tn),
        in_specs=in_specs,
        out_specs=o_spec,
        compiler_params=_params(("parallel", "parallel")),
        name=f"mm_{mode}_{m}x{k}x{n}",
    )(*args)


def _row(width):
    return pl.BlockSpec((ROW_TILE, width), lambda i: (i, 0))


def _vec(width):
    return pl.BlockSpec((1, width), lambda i: (0, 0))


def _rms_fwd(x, g):
    rows, d = x.shape

    def body(x_ref, g_ref, o_ref):
        xv = x_ref[...]
        rstd = lax.rsqrt(jnp.mean(xv * xv, axis=-1, keepdims=True) + RMS_EPS)
        o_ref[...] = (xv * rstd * g_ref[...]).astype(BF16)

    return pl.pallas_call(
        body, out_shape=jax.ShapeDtypeStruct((rows, d), BF16), grid=(rows // ROW_TILE,),
        in_specs=[_row(d), _vec(d)], out_specs=_row(d), compiler_params=_params(("parallel",)), name=f"rms_fwd_{rows}",
    )(x, g)


def _accumulate(ref, value, step):
    @pl.when(step == 0)
    def _():
        ref[...] = value

    @pl.when(step > 0)
    def _():
        ref[...] += value


def _rms_bwd(x, g, dh, dres):
    rows, d = x.shape
    has_res = dres is not None

    def body(*refs):
        if has_res:
            x_ref, g_ref, dh_ref, dres_ref, dx_ref, dxb_ref, dg_ref = refs
        else:
            x_ref, g_ref, dh_ref, dx_ref, dxb_ref, dg_ref = refs
        xv = x_ref[...]
        dhv = dh_ref[...]
        rstd = lax.rsqrt(jnp.mean(xv * xv, axis=-1, keepdims=True) + RMS_EPS)
        xhat = xv * rstd
        gd = dhv * g_ref[...]
        dx = rstd * (gd - xhat * jnp.mean(gd * xhat, axis=-1, keepdims=True))
        if has_res:
            dx = dx + dres_ref[...]
        dx_ref[...] = dx
        dxb_ref[...] = dx.astype(BF16)
        _accumulate(dg_ref, jnp.sum(dhv * xhat, axis=0, keepdims=True), pl.program_id(0))

    in_specs = [_row(d), _vec(d), _row(d)] + ([_row(d)] if has_res else [])
    args = (x, g, dh) + ((dres,) if has_res else ())
    return pl.pallas_call(
        body,
        out_shape=(jax.ShapeDtypeStruct((rows, d), F32), jax.ShapeDtypeStruct((rows, d), BF16), jax.ShapeDtypeStruct((1, d), F32)),
        grid=(rows // ROW_TILE,), in_specs=in_specs, out_specs=(_row(d), _row(d), _vec(d)),
        compiler_params=_params(("arbitrary",)), name=f"rms_bwd_{rows}_{int(has_res)}",
    )(*args)


def _loss_head(x, g, target):
    rows, d = x.shape

    def body(x_ref, g_ref, t_ref, loss_ref, dx_ref, dxb_ref, dg_ref):
        xv = x_ref[...]
        gv = g_ref[...]
        rstd = lax.rsqrt(jnp.mean(xv * xv, axis=-1, keepdims=True) + RMS_EPS)
        xhat = xv * rstd
        err = xhat * gv - t_ref[...]
        part = 0.5 * jnp.sum(jnp.mean(err * err, axis=-1, keepdims=True), axis=0, keepdims=True)
        dy = err * (1.0 / d)
        gd = dy * gv
        dx = rstd * (gd - xhat * jnp.mean(gd * xhat, axis=-1, keepdims=True))
        dx_ref[...] = dx
        dxb_ref[...] = dx.astype(BF16)
        step = pl.program_id(0)
        _accumulate(loss_ref, jnp.broadcast_to(part, (1, LANE)), step)
        _accumulate(dg_ref, jnp.sum(dy * xhat, axis=0, keepdims=True), step)

    return pl.pallas_call(
        body,
        out_shape=(jax.ShapeDtypeStruct((1, LANE), F32), jax.ShapeDtypeStruct((rows, d), F32),
                   jax.ShapeDtypeStruct((rows, d), BF16), jax.ShapeDtypeStruct((1, d), F32)),
        grid=(rows // ROW_TILE,), in_specs=[_row(d), _vec(d), _row(d)],
        out_specs=(_vec(LANE), _row(d), _row(d), _vec(d)),
        compiler_params=_params(("arbitrary",)), name="loss_head",
    )(x, g, target)


def _sigmoid(x):
    return 1.0 / (1.0 + jnp.exp(-x))


def _ln_silu_fwd(c, ln_g, ln_b):
    rows, ch = c.shape

    def body(c_ref, g_ref, b_ref, o_ref):
        cv = c_ref[...]
        mu = jnp.mean(cv, axis=-1, keepdims=True)
        cen = cv - mu
        rstd = lax.rsqrt(jnp.mean(cen * cen, axis=-1, keepdims=True) + LN_EPS)
        y = cen * rstd * g_ref[...] + b_ref[...]
        o_ref[...] = (y * _sigmoid(y)).astype(BF16)

    return pl.pallas_call(
        body, out_shape=jax.ShapeDtypeStruct((rows, ch), BF16), grid=(rows // ROW_TILE,),
        in_specs=[_row(ch), _vec(ch), _vec(ch)], out_specs=_row(ch), compiler_params=_params(("parallel",)), name="ln_silu_fwd",
    )(c, ln_g, ln_b)


def _ln_silu_bwd(dmix, c, ln_g, ln_b):
    rows, ch = c.shape

    def body(dm_ref, c_ref, g_ref, b_ref, dc_ref, dg_ref, db_ref):
        cv = c_ref[...]
        gv = g_ref[...]
        mu = jnp.mean(cv, axis=-1, keepdims=True)
        cen = cv - mu
        rstd = lax.rsqrt(jnp.mean(cen * cen, axis=-1, keepdims=True) + LN_EPS)
        xhat = cen * rstd
        y = xhat * gv + b_ref[...]
        sg = _sigmoid(y)
        dy = dm_ref[...] * (sg * (1.0 + y * (1.0 - sg)))
        dxh = dy * gv
        dc = rstd * (dxh - jnp.mean(dxh, axis=-1, keepdims=True) - xhat * jnp.mean(dxh * xhat, axis=-1, keepdims=True))
        dc_ref[...] = dc
        step = pl.program_id(0)
        _accumulate(dg_ref, jnp.sum(dy * xhat, axis=0, keepdims=True), step)
        _accumulate(db_ref, jnp.sum(dy, axis=0, keepdims=True), step)

    return pl.pallas_call(
        body,
        out_shape=(jax.ShapeDtypeStruct((rows, ch), F32), jax.ShapeDtypeStruct((1, ch), F32), jax.ShapeDtypeStruct((1, ch), F32)),
        grid=(rows // ROW_TILE,), in_specs=[_row(ch), _row(ch), _vec(ch), _vec(ch)],
        out_specs=(_row(ch), _vec(ch), _vec(ch)), compiler_params=_params(("arbitrary",)), name="ln_silu_bwd",
    )(dmix, c, ln_g, ln_b)


def _col(rows, cb):
    return pl.BlockSpec((rows, cb), lambda j: (0, j))


SUBLANES = 8
_WINDOW = CHUNK + CONV_PAD


def _realign(win, shifted_ref):
    shifted_ref[0] = win
    for r in range(1, SUBLANES):
        shifted_ref[r, 0:_WINDOW - SUBLANES, :] = win[r:r + _WINDOW - SUBLANES, :]


def _tap(shifted_ref, off):
    r = off % SUBLANES
    return shifted_ref[r, off - r:off - r + CHUNK, :]


def _conv_glu_fwd(pa, pg, ba, bg, w_dw, b_dw):
    seq, ch = pa.shape
    cb = LANE
    n_chunks = seq // CHUNK

    def body(pa_ref, pg_ref, ba_ref, bg_ref, w_ref, b_ref, v_ref, c_ref, vpad_ref, vshift_ref):
        vpad_ref[0:CONV_PAD, :] = jnp.zeros((CONV_PAD, cb), F32)

        def glu(r, carry):
            r0 = pl.multiple_of(r * CHUNK, CHUNK)
            rows = pl.ds(r0, CHUNK)
            v = (pa_ref[rows, :] + ba_ref[...]) * _sigmoid(pg_ref[rows, :] + bg_ref[...])
            v_ref[rows, :] = v
            vpad_ref[pl.ds(r0 + CONV_PAD, CHUNK), :] = v
            return carry

        lax.fori_loop(0, n_chunks, glu, 0)

        def conv(r, carry):
            r0 = pl.multiple_of(r * CHUNK, CHUNK)
            _realign(vpad_ref[pl.ds(r0, _WINDOW), :], vshift_ref)
            acc = jnp.broadcast_to(b_ref[...], (CHUNK, cb))
            for t in range(CONV_W):
                acc = acc + w_ref[t:t + 1, :] * _tap(vshift_ref, CONV_PAD - (CONV_W - 1) + t)
            c_ref[pl.ds(r0, CHUNK), :] = acc
            return carry

        lax.fori_loop(0, n_chunks, conv, 0)

    return pl.pallas_call(
        body,
        out_shape=(jax.ShapeDtypeStruct((seq, ch), F32), jax.ShapeDtypeStruct((seq, ch), F32)),
        grid=(ch // cb,),
        in_specs=[_col(seq, cb), _col(seq, cb), _col(1, cb), _col(1, cb), _col(CONV_PAD, cb), _col(1, cb)],
        out_specs=(_col(seq, cb), _col(seq, cb)),
        scratch_shapes=[pltpu.VMEM((seq + CONV_PAD, cb), F32), pltpu.VMEM((SUBLANES, _WINDOW, cb), F32)],
        compiler_params=_params(("parallel",)), name="conv_glu_fwd",
    )(pa, pg, ba, bg, w_dw, b_dw)


def _conv_glu_bwd(dc, v, pa, pg, ba, bg, w_dw):
    seq, ch = dc.shape
    cb = LANE
    n_chunks = seq // CHUNK

    def body(dc_ref, v_ref, pa_ref, pg_ref, ba_ref, bg_ref, w_ref, da_ref, dg_ref, dw_ref, dbdw_ref, dba_ref, dbg_ref,
             dcpad_ref, vpad_ref, dcshift_ref, vshift_ref):
        vpad_ref[0:CONV_PAD, :] = jnp.zeros((CONV_PAD, cb), F32)
        dcpad_ref[seq:seq + CONV_PAD, :] = jnp.zeros((CONV_PAD, cb), F32)
        dw_ref[...] = jnp.zeros((CONV_PAD, cb), F32)
        dbdw_ref[...] = jnp.zeros((1, cb), F32)
        dba_ref[...] = jnp.zeros((1, cb), F32)
        dbg_ref[...] = jnp.zeros((1, cb), F32)

        def fill(r, carry):
            r0 = pl.multiple_of(r * CHUNK, CHUNK)
            vpad_ref[pl.ds(r0 + CONV_PAD, CHUNK), :] = v_ref[pl.ds(r0, CHUNK), :]
            dcpad_ref[pl.ds(r0, CHUNK), :] = dc_ref[pl.ds(r0, CHUNK), :]
            return carry

        lax.fori_loop(0, n_chunks, fill, 0)

        def step(r, carry):
            r0 = pl.multiple_of(r * CHUNK, CHUNK)
            rows = pl.ds(r0, CHUNK)
            _realign(dcpad_ref[pl.ds(r0, _WINDOW), :], dcshift_ref)
            _realign(vpad_ref[pl.ds(r0, _WINDOW), :], vshift_ref)
            dcur = _tap(dcshift_ref, 0)
            dv = jnp.zeros((CHUNK, cb), F32)
            for t in range(CONV_W):
                dv = dv + w_ref[t:t + 1, :] * _tap(dcshift_ref, CONV_W - 1 - t)
                dw_ref[t:t + 1, :] += jnp.sum(dcur * _tap(vshift_ref, CONV_PAD - (CONV_W - 1) + t), axis=0, keepdims=True)
            a = pa_ref[rows, :] + ba_ref[...]
            sg = _sigmoid(pg_ref[rows, :] + bg_ref[...])
            da = dv * sg
            dgate = dv * a * sg * (1.0 - sg)
            da_ref[rows, :] = da.astype(BF16)
            dg_ref[rows, :] = dgate.astype(BF16)
            dbdw_ref[...] += jnp.sum(dcur, axis=0, keepdims=True)
            dba_ref[...] += jnp.sum(da, axis=0, keepdims=True)
            dbg_ref[...] += jnp.sum(dgate, axis=0, keepdims=True)
            return carry

        lax.fori_loop(0, n_chunks, step, 0)

    return pl.pallas_call(
        body,
        out_shape=(jax.ShapeDtypeStruct((seq, ch), BF16), jax.ShapeDtypeStruct((seq, ch), BF16),
                   jax.ShapeDtypeStruct((CONV_PAD, ch), F32), jax.ShapeDtypeStruct((1, ch), F32),
                   jax.ShapeDtypeStruct((1, ch), F32), jax.ShapeDtypeStruct((1, ch), F32)),
        grid=(ch // cb,),
        in_specs=[_col(seq, cb), _col(seq, cb), _col(seq, cb), _col(seq, cb), _col(1, cb), _col(1, cb), _col(CONV_PAD, cb)],
        out_specs=(_col(seq, cb), _col(seq, cb), _col(CONV_PAD, cb), _col(1, cb), _col(1, cb), _col(1, cb)),
        scratch_shapes=[pltpu.VMEM((seq + CONV_PAD, cb), F32), pltpu.VMEM((seq + CONV_PAD, cb), F32),
                        pltpu.VMEM((SUBLANES, _WINDOW, cb), F32), pltpu.VMEM((SUBLANES, _WINDOW, cb), F32)],
        compiler_params=_params(("parallel",)), name="conv_glu_bwd",
    )(dc, v, pa, pg, ba, bg, w_dw)


def _ffn_conv(pad_ref, r0, w_ref, b_ref, cb):
    win = pad_ref[pl.ds(r0, CHUNK + FFN_PAD), :]
    y = jnp.broadcast_to(b_ref[...], (CHUNK, cb))
    for t in range(FFN_CONV_W):
        off = FFN_PAD - (FFN_CONV_W - 1) + t
        y = y + w_ref[t:t + 1, :] * win[off:off + CHUNK, :]
    return y, win


def _ffn_mid_fwd(ug, uv, wg, wv, bg, bv):
    seq, ch = ug.shape
    cb = _tile(ch, 256)
    n_chunks = seq // CHUNK

    def body(ug_ref, uv_ref, wg_ref, wv_ref, bg_ref, bv_ref, z_ref, gpad_ref, vpad_ref):
        gpad_ref[0:FFN_PAD, :] = jnp.zeros((FFN_PAD, cb), F32)
        vpad_ref[0:FFN_PAD, :] = jnp.zeros((FFN_PAD, cb), F32)

        def fill(r, carry):
            r0 = pl.multiple_of(r * CHUNK, CHUNK)
            gpad_ref[pl.ds(r0 + FFN_PAD, CHUNK), :] = ug_ref[pl.ds(r0, CHUNK), :].astype(F32)
            vpad_ref[pl.ds(r0 + FFN_PAD, CHUNK), :] = uv_ref[pl.ds(r0, CHUNK), :].astype(F32)
            return carry

        lax.fori_loop(0, n_chunks, fill, 0)

        def step(r, carry):
            r0 = pl.multiple_of(r * CHUNK, CHUNK)
            yg, _ = _ffn_conv(gpad_ref, r0, wg_ref, bg_ref, cb)
            yv, _ = _ffn_conv(vpad_ref, r0, wv_ref, bv_ref, cb)
            z_ref[pl.ds(r0, CHUNK), :] = (yg * _sigmoid(yg) * yv).astype(BF16)
            return carry

        lax.fori_loop(0, n_chunks, step, 0)

    return pl.pallas_call(
        body, out_shape=jax.ShapeDtypeStruct((seq, ch), BF16), grid=(ch // cb,),
        in_specs=[_col(seq, cb), _col(seq, cb), _col(FFN_PAD, cb), _col(FFN_PAD, cb), _col(1, cb), _col(1, cb)],
        out_specs=_col(seq, cb),
        scratch_shapes=[pltpu.VMEM((seq + FFN_PAD, cb), F32), pltpu.VMEM((seq + FFN_PAD, cb), F32)],
        compiler_params=_params(("parallel",)), name="ffn_mid_fwd",
    )(ug, uv, wg, wv, bg, bv)


def _ffn_mid_bwd(ug, uv, dz, wg, wv, bg, bv):
    seq, ch = ug.shape
    cb = _tile(ch, 256)
    n_chunks = seq // CHUNK

    def body(ug_ref, uv_ref, dz_ref, wg_ref, wv_ref, bg_ref, bv_ref, dug_ref, duv_ref, dwg_ref, dwv_ref, dbg_ref, dbv_ref,
             gpad_ref, vpad_ref, dyg_ref, dyv_ref):
        gpad_ref[0:FFN_PAD, :] = jnp.zeros((FFN_PAD, cb), F32)
        vpad_ref[0:FFN_PAD, :] = jnp.zeros((FFN_PAD, cb), F32)
        dyg_ref[seq:seq + FFN_PAD, :] = jnp.zeros((FFN_PAD, cb), F32)
        dyv_ref[seq:seq + FFN_PAD, :] = jnp.zeros((FFN_PAD, cb), F32)
        dwg_ref[...] = jnp.zeros((FFN_PAD, cb), F32)
        dwv_ref[...] = jnp.zeros((FFN_PAD, cb), F32)
        dbg_ref[...] = jnp.zeros((1, cb), F32)
        dbv_ref[...] = jnp.zeros((1, cb), F32)

        def fill(r, carry):
            r0 = pl.multiple_of(r * CHUNK, CHUNK)
            gpad_ref[pl.ds(r0 + FFN_PAD, CHUNK), :] = ug_ref[pl.ds(r0, CHUNK), :].astype(F32)
            vpad_ref[pl.ds(r0 + FFN_PAD, CHUNK), :] = uv_ref[pl.ds(r0, CHUNK), :].astype(F32)
            return carry

        lax.fori_loop(0, n_chunks, fill, 0)

        def grads_of_conv_out(r, carry):
            r0 = pl.multiple_of(r * CHUNK, CHUNK)
            rows = pl.ds(r0, CHUNK)
            yg, gwin = _ffn_conv(gpad_ref, r0, wg_ref, bg_ref, cb)
            yv, vwin = _ffn_conv(vpad_ref, r0, wv_ref, bv_ref, cb)
            dzv = dz_ref[rows, :].astype(F32)
            sg = _sigmoid(yg)
            dyg = dzv * yv * (sg * (1.0 + yg * (1.0 - sg)))
            dyv = dzv * yg * sg
            dyg_ref[rows, :] = dyg
            dyv_ref[rows, :] = dyv
            for t in range(FFN_CONV_W):
                off = FFN_PAD - (FFN_CONV_W - 1) + t
                dwg_ref[t:t + 1, :] += jnp.sum(dyg * gwin[off:off + CHUNK, :], axis=0, keepdims=True)
                dwv_ref[t:t + 1, :] += jnp.sum(dyv * vwin[off:off + CHUNK, :], axis=0, keepdims=True)
            dbg_ref[...] += jnp.sum(dyg, axis=0, keepdims=True)
            dbv_ref[...] += jnp.sum(dyv, axis=0, keepdims=True)
            return carry

        lax.fori_loop(0, n_chunks, grads_of_conv_out, 0)

        def grads_of_conv_in(r, carry):
            r0 = pl.multiple_of(r * CHUNK, CHUNK)
            gwin = dyg_ref[pl.ds(r0, CHUNK + FFN_PAD), :]
            vwin = dyv_ref[pl.ds(r0, CHUNK + FFN_PAD), :]
            dug = jnp.zeros((CHUNK, cb), F32)
            duv = jnp.zeros((CHUNK, cb), F32)
            for t in range(FFN_CONV_W):
                off = FFN_CONV_W - 1 - t
                dug = dug + wg_ref[t:t + 1, :] * gwin[off:off + CHUNK, :]
                duv = duv + wv_ref[t:t + 1, :] * vwin[off:off + CHUNK, :]
            dug_ref[pl.ds(r0, CHUNK), :] = dug.astype(BF16)
            duv_ref[pl.ds(r0, CHUNK), :] = duv.astype(BF16)
            return carry

        lax.fori_loop(0, n_chunks, grads_of_conv_in, 0)

    return pl.pallas_call(
        body,
        out_shape=(jax.ShapeDtypeStruct((seq, ch), BF16), jax.ShapeDtypeStruct((seq, ch), BF16),
                   jax.ShapeDtypeStruct((FFN_PAD, ch), F32), jax.ShapeDtypeStruct((FFN_PAD, ch), F32),
                   jax.ShapeDtypeStruct((1, ch), F32), jax.ShapeDtypeStruct((1, ch), F32)),
        grid=(ch // cb,),
        in_specs=[_col(seq, cb), _col(seq, cb), _col(seq, cb), _col(FFN_PAD, cb), _col(FFN_PAD, cb), _col(1, cb), _col(1, cb)],
        out_specs=(_col(seq, cb), _col(seq, cb), _col(FFN_PAD, cb), _col(FFN_PAD, cb), _col(1, cb), _col(1, cb)),
        scratch_shapes=[pltpu.VMEM((seq + FFN_PAD, cb), F32) for _ in range(4)],
        compiler_params=_params(("parallel",)), name="ffn_mid_bwd",
    )(ug, uv, dz, wg, wv, bg, bv)


def _dot(a, b, mode):
    return lax.dot_general(a.astype(BF16), b.astype(BF16), _DIMS[mode], preferred_element_type=F32)


def _head(h):
    return slice(h * HEAD_DIM, (h + 1) * HEAD_DIM)


def _mem_softmax(q, k):
    s = _dot(q, k, "nt") * ATT_SCALE
    e = jnp.exp(s - jnp.max(s, axis=-1, keepdims=True))
    return e / jnp.sum(e, axis=-1, keepdims=True)


def _memattn_fwd(pq, mk, mv):
    seq, w = pq.shape
    m = mk.shape[0]

    def body(q_ref, k_ref, v_ref, o_ref):
        for h in range(N_MEM_HEADS):
            p = _mem_softmax(q_ref[:, _head(h)], k_ref[:, _head(h)])
            o_ref[:, _head(h)] = _dot(p, v_ref[:, _head(h)], "nn").astype(BF16)

    full = pl.BlockSpec((m, w), lambda i: (0, 0))
    return pl.pallas_call(
        body, out_shape=jax.ShapeDtypeStruct((seq, w), BF16), grid=(seq // ROW_TILE,),
        in_specs=[_row(w), full, full], out_specs=_row(w), compiler_params=_params(("parallel",)), name="memattn_fwd",
    )(pq, mk, mv)


def _memattn_bwd(pq, mk, mv, dmo):
    seq, w = pq.shape
    m = mk.shape[0]

    def body(q_ref, k_ref, v_ref, do_ref, dq_ref, dk_ref, dv_ref):
        step = pl.program_id(0)

        @pl.when(step == 0)
        def _():
            dk_ref[...] = jnp.zeros((m, w), F32)
            dv_ref[...] = jnp.zeros((m, w), F32)

        for h in range(N_MEM_HEADS):
            q, k, v, do = q_ref[:, _head(h)], k_ref[:, _head(h)], v_ref[:, _head(h)], do_ref[:, _head(h)]
            p = _mem_softmax(q, k)
            dp = _dot(do, v, "nt")
            ds = p * (dp - jnp.sum(dp * p, axis=-1, keepdims=True))
            dq_ref[:, _head(h)] = (_dot(ds, k, "nn") * ATT_SCALE).astype(BF16)
            dk_ref[:, _head(h)] += _dot(ds, q, "tn") * ATT_SCALE
            dv_ref[:, _head(h)] += _dot(p, do, "tn")

    full = pl.BlockSpec((m, w), lambda i: (0, 0))
    return pl.pallas_call(
        body,
        out_shape=(jax.ShapeDtypeStruct((seq, w), BF16), jax.ShapeDtypeStruct((m, w), F32), jax.ShapeDtypeStruct((m, w), F32)),
        grid=(seq // ROW_TILE,), in_specs=[_row(w), full, full, _row(w)], out_specs=(_row(w), full, full),
        compiler_params=_params(("arbitrary",)), name="memattn_bwd",
    )(pq, mk, mv, dmo)


FWD_KEY_TILE = 512
BWD_KEY_TILE = 512


def _causal_t(s_t, q_blk, k_blk, key_tile):
    kpos = k_blk * key_tile + lax.broadcasted_iota(jnp.int32, (key_tile, ROW_TILE), 0)
    qpos = q_blk * ROW_TILE + lax.broadcasted_iota(jnp.int32, (key_tile, ROW_TILE), 1)
    return jnp.where(kpos <= qpos, s_t, NEG_BIG)


def _fox_fwd(q, k, v, cum_cols, cum_rows):
    seq, w = q.shape
    nq = seq // ROW_TILE

    def body(q_ref, k_ref, v_ref, cc_ref, cr_ref, o_ref, ob_ref, lse_ref):
        i = pl.program_id(1)
        qs = [q_ref[:, _head(h)].astype(BF16) for h in range(2)]
        crs = [cr_ref[0, 0, h:h + 1, :] for h in range(2)]

        def step(j, carry):
            rows = pl.ds(pl.multiple_of(j * FWD_KEY_TILE, FWD_KEY_TILE), FWD_KEY_TILE)
            new = []
            for h in range(2):
                m_run, l_run, acc = carry[h]
                s_t = _dot(k_ref[rows, _head(h)], qs[h], "nt") * ATT_SCALE + crs[h] - cc_ref[0, rows, h * HEAD_DIM:h * HEAD_DIM + 1]
                s_t = _causal_t(s_t, i, j, FWD_KEY_TILE)
                m_new = jnp.maximum(m_run, jnp.max(s_t, axis=0, keepdims=True))
                alpha = jnp.exp(m_run - m_new)
                p_t = jnp.exp(s_t - m_new)
                l_new = alpha * l_run + jnp.sum(p_t, axis=0, keepdims=True)
                new.append((m_new, l_new, alpha * acc + _dot(v_ref[rows, _head(h)], p_t, "tn")))
            return tuple(new)

        one_head = (jnp.full((1, ROW_TILE), NEG_BIG, F32), jnp.zeros((1, ROW_TILE), F32), jnp.zeros((HEAD_DIM, ROW_TILE), F32))
        res = lax.fori_loop(0, ((i + 1) * ROW_TILE + FWD_KEY_TILE - 1) // FWD_KEY_TILE, step, (one_head, one_head))
        o = jnp.concatenate([acc / l_run for _, l_run, acc in res], axis=0).T
        o_ref[...] = o
        ob_ref[...] = o.astype(BF16)
        lse_ref[...] = jnp.zeros((1, 1, 8, ROW_TILE), F32)
        for h in range(2):
            lse_ref[0, 0, h:h + 1, :] = res[h][0] + jnp.log(res[h][1])

    blk = pl.BlockSpec((ROW_TILE, LANE), lambda hp, i: (i, hp))
    full = pl.BlockSpec((seq, LANE), lambda hp, i: (0, hp))
    cols = pl.BlockSpec((1, seq, LANE), lambda hp, i: (hp, 0, 0))
    rows = pl.BlockSpec((1, 1, 8, ROW_TILE), lambda hp, i: (hp, i, 0, 0))
    return pl.pallas_call(
        body,
        out_shape=(jax.ShapeDtypeStruct((seq, w), F32), jax.ShapeDtypeStruct((seq, w), BF16),
                   jax.ShapeDtypeStruct((N_HEAD_PAIRS, nq, 8, ROW_TILE), F32)),
        grid=(N_HEAD_PAIRS, nq), in_specs=[blk, full, full, cols, rows], out_specs=(blk, blk, rows),
        compiler_params=_params(("parallel", "parallel")), name="fox_fwd",
    )(q, k, v, cum_cols, cum_rows)


def _fox_bwd(q, k, v, cum_cols, cum_rows, o, lse, do):
    seq, w = q.shape
    nq = seq // ROW_TILE
    nk = seq // BWD_KEY_TILE

    def body(q_ref, k_ref, v_ref, cc_ref, cr_ref, o_ref, lse_ref, do_ref, dq_ref, dk_ref, dv_ref, dcc_ref, dcr_ref):
        kj = pl.program_id(1)

        @pl.when(kj == 0)
        def _():
            dq_ref[...] = jnp.zeros((seq, LANE), F32)
            dcr_ref[...] = jnp.zeros((1, nq, 8, ROW_TILE), F32)

        ks = [k_ref[:, _head(h)].astype(BF16) for h in range(2)]
        vs = [v_ref[:, _head(h)].astype(BF16) for h in range(2)]
        ccs = [cc_ref[0, :, h * HEAD_DIM:h * HEAD_DIM + 1] for h in range(2)]
        ones = jnp.ones((8, HEAD_DIM), BF16)

        def step(i, carry):
            rows = pl.ds(pl.multiple_of(i * ROW_TILE, ROW_TILE), ROW_TILE)
            new = []
            for h in range(2):
                dk, dv, fold = carry[h]
                qh = q_ref[rows, _head(h)].astype(BF16)
                doh = do_ref[rows, _head(h)]
                s_t = _dot(ks[h], qh, "nt") * ATT_SCALE + cr_ref[0, i, h:h + 1, :] - ccs[h]
                p_t = jnp.exp(_causal_t(s_t, i, kj, BWD_KEY_TILE) - lse_ref[0, i, h:h + 1, :])
                dp_t = _dot(vs[h], doh, "nt")
                hi, mid, lo = _split3(doh * o_ref[rows, _head(h)])
                row_sum = lambda part: lax.dot_general(ones, part, _DIMS["nt"], preferred_element_type=F32)
                delta = ((row_sum(lo) + row_sum(mid)) + row_sum(hi))[0:1, :]
                ds_t = p_t * (dp_t - delta)
                dq_ref[rows, _head(h)] += _dot(ds_t, ks[h], "tn") * ATT_SCALE
                dcr_ref[0, i, h:h + 1, :] += jnp.sum(ds_t, axis=0, keepdims=True)
                new.append((dk + _dot(ds_t, qh, "nn") * ATT_SCALE, dv + _dot(p_t, doh, "nn"), fold + (ds_t[:, :LANE] + ds_t[:, LANE:])))
            return tuple(new)

        one_head = (jnp.zeros((BWD_KEY_TILE, HEAD_DIM), F32), jnp.zeros((BWD_KEY_TILE, HEAD_DIM), F32), jnp.zeros((BWD_KEY_TILE, LANE), F32))
        res = lax.fori_loop(kj * BWD_KEY_TILE // ROW_TILE, nq, step, (one_head, one_head))
        for h in range(2):
            dk_ref[:, _head(h)] = res[h][0]
            dv_ref[:, _head(h)] = res[h][1]
            dcc_ref[0, :, _head(h)] = jnp.broadcast_to(-jnp.sum(res[h][2], axis=-1, keepdims=True), (BWD_KEY_TILE, HEAD_DIM))

    blk = pl.BlockSpec((BWD_KEY_TILE, LANE), lambda hp, j: (j, hp))
    full = pl.BlockSpec((seq, LANE), lambda hp, j: (0, hp))
    cols = pl.BlockSpec((1, BWD_KEY_TILE, LANE), lambda hp, j: (hp, j, 0))
    rows = pl.BlockSpec((1, nq, 8, ROW_TILE), lambda hp, j: (hp, 0, 0, 0))
    return pl.pallas_call(
        body,
        out_shape=(jax.ShapeDtypeStruct((seq, w), F32), jax.ShapeDtypeStruct((seq, w), F32), jax.ShapeDtypeStruct((seq, w), F32),
                   jax.ShapeDtypeStruct((N_HEAD_PAIRS, seq, LANE), F32), jax.ShapeDtypeStruct((N_HEAD_PAIRS, nq, 8, ROW_TILE), F32)),
        grid=(N_HEAD_PAIRS, nk), in_specs=[full, blk, blk, cols, rows, full, rows, full],
        out_specs=(full, blk, blk, cols, rows),
        compiler_params=_params(("parallel", "arbitrary")), name="fox_bwd",
    )(q, k, v, cum_cols, cum_rows, o, lse, do)


def _split3(x):
    hi = x.astype(BF16)
    r1 = x - hi.astype(F32)
    mid = r1.astype(BF16)
    lo = (r1 - mid.astype(F32)).astype(BF16)
    return hi, mid, lo


def _tri_sum(tri, x):
    hi, mid, lo = _split3(x)
    dot = lambda p: lax.dot_general(tri, p, _DIMS["nn"], preferred_element_type=F32)
    return (dot(lo) + dot(mid)) + dot(hi)


def _tri(lower):
    r = lax.broadcasted_iota(jnp.int32, (LANE, LANE), 0)
    c = lax.broadcasted_iota(jnp.int32, (LANE, LANE), 1)
    return jnp.where((c <= r) if lower else (c >= r), 1.0, 0.0).astype(BF16)


def _fgate_fwd(fl, bf):
    seq, w = fl.shape
    nb = seq // LANE

    def body(f_ref, b_ref, cum_ref):
        tri = _tri(True)

        def step(i, carry):
            rows = pl.ds(pl.multiple_of(i * LANE, LANE), LANE)
            x = f_ref[rows, :] + b_ref[...]
            logsig = jnp.minimum(x, 0.0) - jnp.log(1.0 + jnp.exp(-jnp.abs(x)))
            cum = _tri_sum(tri, logsig) + carry
            cum_ref[rows, :] = cum
            return cum[LANE - 1:LANE, :]

        lax.fori_loop(0, nb, step, jnp.zeros((1, w), F32))

    return pl.pallas_call(body, out_shape=jax.ShapeDtypeStruct((seq, w), F32), compiler_params=_params(), name="fgate_fwd")(fl, bf)


def _fgate_bwd(fl, bf, dcum):
    seq, w = fl.shape
    nb = seq // LANE

    def body(f_ref, b_ref, dc_ref, df_ref, db_ref):
        tri = _tri(False)

        def step(i, carry):
            tail, db = carry
            rows = pl.ds(pl.multiple_of((nb - 1 - i) * LANE, LANE), LANE)
            suffix = _tri_sum(tri, dc_ref[rows, :]) + tail
            df = suffix * (1.0 - _sigmoid(f_ref[rows, :] + b_ref[...]))
            df_ref[rows, :] = df
            return suffix[0:1, :], db + jnp.sum(df, axis=0, keepdims=True)

        _, db = lax.fori_loop(0, nb, step, (jnp.zeros((1, w), F32), jnp.zeros((1, w), F32)))
        db_ref[...] = db

    return pl.pallas_call(
        body, out_shape=(jax.ShapeDtypeStruct((seq, w), F32), jax.ShapeDtypeStruct((1, w), F32)),
        compiler_params=_params(), name="fgate_bwd",
    )(fl, bf, dcum)


def _cum_layouts(cum):
    seq = cum.shape[0]
    nblk = seq // ROW_TILE
    heads = cum[:, :N_FOX_HEADS]
    cq = jnp.repeat(heads, HEAD_DIM, axis=1).reshape(seq, N_HEAD_PAIRS, LANE).transpose(1, 0, 2)
    ck = heads.T.reshape(N_HEAD_PAIRS, 2, nblk, ROW_TILE).transpose(0, 2, 1, 3)
    ck = jnp.pad(ck, ((0, 0), (0, 0), (0, 6), (0, 0)))
    return cq, ck


def _dcum_from_layouts(dcq, dck):
    seq = dcq.shape[1]
    key_side = dck[:, :, :2, :].transpose(0, 2, 1, 3).reshape(N_FOX_HEADS, seq).T
    query_side = dcq[:, :, ::HEAD_DIM].transpose(1, 0, 2).reshape(seq, N_FOX_HEADS)
    return jnp.pad(key_side + query_side, ((0, 0), (0, F_PAD - N_FOX_HEADS)))


def _local_step(x, mem, target, w, arrive, emit):
    vec = lambda a: a.reshape(1, -1)
    mem_n = _rms_fwd(mem, vec(w["g_mem"]))
    saved = []
    shared = None
    for l in range(N_LAYERS):
        arrive(l, x, "mixer")
        s = {"x_in": x}
        s["h"] = h = _rms_fwd(x, vec(w["g_mix"][l]))
        s["mk"] = mk = _mm(mem_n, w["w_mk"][l])
        s["mv"] = mv = _mm(mem_n, w["w_mv"][l])
        if l < N_A:
            s["pa"] = pa = _mm(h, w["w_a"][l])
            s["pg"] = pg = _mm(h, w["w_g"][l])
            s["pq"] = pq = _mm(h, w["w_qm"][l])
            s["v"], s["c"] = _conv_glu_fwd(pa, pg, vec(w["b_glu_a"][l]), vec(w["b_glu_g"][l]), w["w_dw_a"][l], vec(w["b_dw_a"][l]))
            s["mix"] = mix = _ln_silu_fwd(s["c"], vec(w["ln_g"][l]), vec(w["ln_b"][l]))
        else:
            if l == N_A:
                shared = {"x_in": x}
                shared["hk"] = hk = _rms_fwd(x, vec(w["g_kv"]))
                shared["k"] = _mm(hk, w["w_k"])
                shared["v"] = _mm(hk, w["w_v"])
                shared["fl"] = _mm(hk, w["w_f"])
                cum = _fgate_fwd(shared["fl"], w["b_f"])
                shared["cq"], shared["ck"] = _cum_layouts(cum)
            s["q"] = q = _mm(h, w["w_q"][l - N_A])
            s["pq"] = pq = _mm(h, w["w_qm"][l])
            s["o"], mix, s["lse"] = _fox_fwd(q, shared["k"], shared["v"], shared["cq"], shared["ck"])
            s["mix"] = mix
        s["mo"] = mo = _memattn_fwd(pq, mk, mv)
        x = _mm(mix, w["w_o_mix"][l], add=x)
        s["x_mid"] = x = _mm(mo, w["w_o_mem"][l], add=x)
        arrive(l, x, "ffn")
        s["hf"] = hf = _rms_fwd(x, vec(w["g_ffn"][l]))
        s["ug"] = ug = _mm(hf, w["w_up_g"][l], out_dtype=BF16)
        s["uv"] = uv = _mm(hf, w["w_up_v"][l], out_dtype=BF16)
        s["z"] = z = _ffn_mid_fwd(ug, uv, w["w_dw_f_g"][l], w["w_dw_f_v"][l], vec(w["b_dw_f_g"][l]), vec(w["b_dw_f_v"][l]))
        x = _mm(z, w["w_down"][l], add=x)
        saved.append(s)

    loss_row, dx, dxb, dg_final = _loss_head(x, vec(w["g_final"]), target)
    loss = loss_row[0, 0]

    g = {n: [None] * N_LAYERS for n in ("g_mix", "w_mem_kv", "w_out", "g_ffn", "w_up", "w_dw_f", "b_dw_f", "w_down")}
    for n in ("w_in_a", "b_glu", "w_dw_a", "b_dw_a", "ln_g", "ln_b", "w_in_b"):
        g[n] = [None] * N_A
    one = lambda arr: ([arr], 0)
    after_start = lambda started, grad: grad if started is None else grad + started[0, 0].astype(BF16)
    g["g_final"] = [one(dg_final[0])]
    dmem_n = None
    dk_sum = dv_sum = dcq_sum = dck_sum = None
    for l in reversed(range(N_LAYERS)):
        s = saved[l]
        dz = _mm(dxb, w["w_down"][l], "nt", BF16)
        g["w_down"][l] = one(_mm(s["z"], dxb, "tn", BF16))
        dug, duv, dwg, dwv, dbg, dbv = _ffn_mid_bwd(s["ug"], s["uv"], dz, w["w_dw_f_g"][l], w["w_dw_f_v"][l],
                                                    vec(w["b_dw_f_g"][l]), vec(w["b_dw_f_v"][l]))
        g["w_up"][l] = ([_mm(s["hf"], dug, "tn", BF16), _mm(s["hf"], duv, "tn", BF16)], 1)
        g["w_dw_f"][l] = ([dwg[:FFN_CONV_W], dwv[:FFN_CONV_W]], 1)
        g["b_dw_f"][l] = ([dbg[0], dbv[0]], 0)
        dhf = _mm(duv, w["w_up_v"][l], "nt", add=_mm(dug, w["w_up_g"][l], "nt"))
        dx, dxb, dg_ffn = _rms_bwd(s["x_mid"], vec(w["g_ffn"][l]), dhf, dx)
        g["g_ffn"][l] = one(dg_ffn[0])
        dxb = after_start(emit(l, g, "ffn"), dxb)
        dmix = _mm(dxb, w["w_o_mix"][l], "nt")
        dmo = _mm(dxb, w["w_o_mem"][l], "nt")
        g["w_out"][l] = ([_mm(s["mix"], dxb, "tn", BF16), _mm(s["mo"], dxb, "tn", BF16)], 0)
        dpq, dmk, dmv = _memattn_bwd(s["pq"], s["mk"], s["mv"], dmo)
        g["w_mem_kv"][l] = ([_mm(mem_n, dmk, "tn", BF16), _mm(mem_n, dmv, "tn", BF16)], 1)
        dmem_n = _mm(dmk, w["w_mk"][l], "nt", add=dmem_n)
        dmem_n = _mm(dmv, w["w_mv"][l], "nt", add=dmem_n)
        if l < N_A:
            dc, dlng, dlnb = _ln_silu_bwd(dmix, s["c"], vec(w["ln_g"][l]), vec(w["ln_b"][l]))
            da, dgate, dwdw, dbdw, dba, dbg2 = _conv_glu_bwd(dc, s["v"], s["pa"], s["pg"], vec(w["b_glu_a"][l]),
                                                              vec(w["b_glu_g"][l]), w["w_dw_a"][l])
            g["ln_g"][l], g["ln_b"][l], g["b_dw_a"][l] = one(dlng[0]), one(dlnb[0]), one(dbdw[0])
            g["w_dw_a"][l] = ([dwdw[:CONV_W]], 1)
            g["b_glu"][l] = ([dba[0], dbg2[0]], 0)
            g["w_in_a"][l] = ([_mm(s["h"], da, "tn", BF16), _mm(s["h"], dgate, "tn", BF16), _mm(s["h"], dpq, "tn", BF16)], 1)
            dh = _mm(da, w["w_a"][l], "nt")
            dh = _mm(dgate, w["w_g"][l], "nt", add=dh)
            dh = _mm(dpq, w["w_qm"][l], "nt", add=dh)
        else:
            dq, dk, dv, dcq, dck = _fox_bwd(s["q"], shared["k"], shared["v"], shared["cq"], shared["ck"], s["o"], s["lse"], dmix)
            dk_sum = dk if dk_sum is None else dk_sum + dk
            dv_sum = dv if dv_sum is None else dv_sum + dv
            dcq_sum = dcq if dcq_sum is None else dcq_sum + dcq
            dck_sum = dck if dck_sum is None else dck_sum + dck
            g["w_in_b"][l - N_A] = ([_mm(s["h"], dq, "tn", BF16), _mm(s["h"], dpq, "tn", BF16)], 1)
            dh = _mm(dq, w["w_q"][l - N_A], "nt")
            dh = _mm(dpq, w["w_qm"][l], "nt", add=dh)
        dx, dxb, dg_mix = _rms_bwd(s["x_in"], vec(w["g_mix"][l]), dh, dx)
        g["g_mix"][l] = one(dg_mix[0])
        if l == N_A:
            df, dbf = _fgate_bwd(shared["fl"], w["b_f"], _dcum_from_layouts(dcq_sum, dck_sum))
            hk = shared["hk"]
            g["w_kvf"] = [([_mm(hk, dk_sum, "tn", BF16), _mm(hk, dv_sum, "tn", BF16), _mm(hk, df, "tn", BF16)[:, :N_FOX_HEADS]], 1)]
            g["b_f"] = [one(dbf[0, :N_FOX_HEADS])]
            dhk = _mm(dk_sum, w["w_k"], "nt")
            dhk = _mm(dv_sum, w["w_v"], "nt", add=dhk)
            dhk = _mm(df, w["w_f"], "nt", add=dhk)
            dx, dxb, dg_kv = _rms_bwd(shared["x_in"], vec(w["g_kv"]), dhk, dx)
            g["g_kv"] = [one(dg_kv[0])]
        dxb = after_start(emit(l, g, "mixer"), dxb)
    _, _, dg_mem = _rms_bwd(mem, vec(w["g_mem"]), dmem_n, None)
    g["g_mem"] = [one(dg_mem[0])]
    return loss, dx, g


SHARDED = {
    "w_in_a": ((N_A, D_MODEL, 2 * CONV_CH + MEM_W), 2),
    "b_glu": ((N_A, 2 * CONV_CH), 1),
    "w_dw_a": ((N_A, CONV_W, CONV_CH), 2),
    "b_dw_a": ((N_A, CONV_CH), 1),
    "ln_g": ((N_A, CONV_CH), 1),
    "ln_b": ((N_A, CONV_CH), 1),
    "w_kvf": ((D_MODEL, 2 * CONV_CH + N_FOX_HEADS), 1),
    "w_in_b": ((N_LAYERS - N_A, D_MODEL, CONV_CH + MEM_W), 1),
    "w_mem_kv": ((N_LAYERS, D_MODEL, 2 * MEM_W), 1),
    "w_out": ((N_LAYERS, D_MODEL, D_MODEL), 1),
    "w_up": ((N_LAYERS, D_MODEL, 2 * D_FF), 2),
    "w_dw_f": ((N_LAYERS, FFN_CONV_W, 2 * D_FF), 2),
    "w_down": ((N_LAYERS, D_FF, D_MODEL), 1),
}
REPLICATED = {
    "g_mix": (N_LAYERS, D_MODEL), "g_kv": (D_MODEL,), "b_f": (N_FOX_HEADS,), "g_mem": (D_MODEL,),
    "g_ffn": (N_LAYERS, D_MODEL), "b_dw_f": (N_LAYERS, 2 * D_FF), "g_final": (D_MODEL,),
}
WEIGHT_ORDER = ["g_mix", "w_in_a", "b_glu", "w_dw_a", "b_dw_a", "ln_g", "ln_b", "g_kv", "w_kvf", "b_f", "w_in_b", "g_mem",
                "w_mem_kv", "w_out", "g_ffn", "w_up", "w_dw_f", "b_dw_f", "w_down", "g_final"]


def _shard_shape(name):
    shape, axis = SHARDED[name]
    return tuple(d // N_CHIPS if i == axis else d for i, d in enumerate(shape))


def _size(shape):
    n = 1
    for d in shape:
        n *= d
    return n


VECTOR_LEAVES = ("b_glu", "w_dw_a", "b_dw_a", "ln_g", "ln_b", "w_dw_f")
_HALF_ROW_TILE = 16
_VECTOR_ELEMS = sum(_size(_shard_shape(n)) for n in VECTOR_LEAVES)
VECTOR_ROWS = -(-_VECTOR_ELEMS // (FLAT_COLS * 2 * _HALF_ROW_TILE)) * 2 * _HALF_ROW_TILE
REP_ELEMS = sum(_size(s) for s in REPLICATED.values())
REP_ROWS = 8
REP_COLS = -(-REP_ELEMS // (REP_ROWS * LANE)) * LANE


def _flatten_vectors(parts, dtype):
    flat = jnp.concatenate([p.reshape(-1).astype(dtype) for p in parts])
    return jnp.pad(flat, (0, VECTOR_ROWS * FLAT_COLS - _VECTOR_ELEMS)).reshape(2, VECTOR_ROWS // 2, FLAT_COLS)


def _unflatten_vectors(flat, lead=()):
    flat = flat.reshape(lead + (-1,))
    out, off = {}, 0
    for n in VECTOR_LEAVES:
        shp = _shard_shape(n)
        out[n] = flat[..., off:off + _size(shp)].reshape(lead + shp)
        off += _size(shp)
    return out


def _span(parts, cat_axis, sel_axis, lo, hi):
    if sel_axis != cat_axis:
        return jnp.concatenate([lax.slice_in_dim(p, lo, hi, axis=sel_axis) for p in parts], axis=cat_axis)
    taken, off = [], 0
    for p in parts:
        n = p.shape[cat_axis]
        a, b = max(lo, off), min(hi, off + n)
        if a < b:
            taken.append(lax.slice_in_dim(p, a - off, b - off, axis=cat_axis))
        off += n
    return taken[0] if len(taken) == 1 else jnp.concatenate(taken, axis=cat_axis)


FFN_LEAVES = 2
LAYER_LEAVES = tuple(
    ([("w_in_a", l)] if l < N_A else ([("w_kvf", None)] if l == N_A else []) + [("w_in_b", l - N_A)])
    + [("w_mem_kv", l), ("w_out", l), ("w_up", l), ("w_down", l)]
    for l in range(N_LAYERS))


def _layer_halves(weights, leaves):
    out = []
    for name, idx in leaves:
        shard = weights[name] if idx is None else weights[name][idx]
        rows, cols = shard.shape
        assert (rows // 2) % _HALF_ROW_TILE == 0, name
        out.append(shard.astype(BF16).reshape(2, rows // 2, cols))
    return out


def _layer_pieces(l, gathered, vectors):
    parts = {}
    for (name, idx), g in zip(LAYER_LEAVES[l], gathered):
        if g is not None:
            parts[name] = [g[t].reshape(-1, g.shape[-1]) for t in range(N_CHIPS)]

    def cut(name, sel_axis, lo, hi):
        axis = SHARDED[name][1] - (len(SHARDED[name][0]) - 2)
        return _span(parts[name], axis, sel_axis, lo, hi)

    def vcut(name, idx, sel_axis, lo, hi):
        axis = SHARDED[name][1] - 1
        return _span([vectors[name][t, idx] for t in range(N_CHIPS)], axis, sel_axis, lo, hi)

    def pad_rows(arr, rows):
        return jnp.pad(arr, ((0, rows - arr.shape[0]), (0, 0)))

    w = {}
    if "w_up" in parts:
        w["w_up_g"] = cut("w_up", 1, 0, D_FF)
        w["w_up_v"] = cut("w_up", 1, D_FF, 2 * D_FF)
        w["w_dw_f_g"] = pad_rows(vcut("w_dw_f", l, 1, 0, D_FF), FFN_PAD)
        w["w_dw_f_v"] = pad_rows(vcut("w_dw_f", l, 1, D_FF, 2 * D_FF), FFN_PAD)
        w["w_down"] = cut("w_down", 0, 0, D_FF)
    if "w_out" not in parts:
        return w
    if l < N_A:
        w["w_a"] = cut("w_in_a", 1, 0, CONV_CH)
        w["w_g"] = cut("w_in_a", 1, CONV_CH, 2 * CONV_CH)
        w["w_qm"] = cut("w_in_a", 1, 2 * CONV_CH, 2 * CONV_CH + MEM_W)
        w["b_glu_a"] = vcut("b_glu", l, 0, 0, CONV_CH)
        w["b_glu_g"] = vcut("b_glu", l, 0, CONV_CH, 2 * CONV_CH)
        w["w_dw_a"] = pad_rows(vcut("w_dw_a", l, 1, 0, CONV_CH), CONV_PAD)
        for n in ("b_dw_a", "ln_g", "ln_b"):
            w[n] = vcut(n, l, 0, 0, CONV_CH)
    else:
        if l == N_A:
            w["w_k"] = cut("w_kvf", 1, 0, CONV_CH)
            w["w_v"] = cut("w_kvf", 1, CONV_CH, 2 * CONV_CH)
            w["w_f"] = jnp.pad(cut("w_kvf", 1, 2 * CONV_CH, 2 * CONV_CH + N_FOX_HEADS), ((0, 0), (0, F_PAD - N_FOX_HEADS)))
        w["w_q"] = cut("w_in_b", 1, 0, CONV_CH)
        w["w_qm"] = cut("w_in_b", 1, CONV_CH, CONV_CH + MEM_W)
    w["w_mk"] = cut("w_mem_kv", 1, 0, MEM_W)
    w["w_mv"] = cut("w_mem_kv", 1, MEM_W, 2 * MEM_W)
    w["w_o_mix"] = cut("w_out", 0, 0, CONV_CH)
    w["w_o_mem"] = cut("w_out", 0, CONV_CH, D_MODEL)
    return w


_PER_LAYER = ("w_a", "w_g", "w_q", "w_qm", "b_glu_a", "b_glu_g", "w_dw_a", "b_dw_a", "ln_g", "ln_b", "w_mk", "w_mv", "w_o_mix", "w_o_mem",
              "w_up_g", "w_up_v", "w_dw_f_g", "w_dw_f_v", "w_down")


def _weight_table(rep):
    w = dict(rep)
    w["b_f"] = jnp.pad(rep["b_f"], (0, F_PAD - N_FOX_HEADS)).reshape(1, F_PAD)
    w["b_dw_f_g"], w["b_dw_f_v"] = rep["b_dw_f"][:, :D_FF], rep["b_dw_f"][:, D_FF:]
    for n in _PER_LAYER:
        w[n] = {}
    return w


def _install_layer(w, l, pieces):
    for n, p in pieces.items():
        if n in _PER_LAYER:
            w[n][l - N_A if n == "w_q" else l] = p
        else:
            w[n] = p


def _shard_of(entry, name, t):
    shape, axis = SHARDED[name]
    pieces, cat_axis = entry
    axis -= len(shape) - len(pieces[0].shape)
    step = shape[SHARDED[name][1]] // N_CHIPS
    return _span(pieces, cat_axis, axis, t * step, (t + 1) * step)


def _layer_slabs(grads, leaves):
    out = []
    for name, idx in leaves:
        shards = [_shard_of(grads[name][0 if idx is None else idx], name, t).astype(BF16) for t in range(N_CHIPS)]
        rows, cols = shards[0].shape
        out.append(jnp.stack(shards).reshape(N_CHIPS, 2, rows // 2, cols))
    return out


def _vector_slabs(grads):
    def shard(name, t):
        return jnp.stack([_shard_of(entry, name, t) for entry in grads[name]])
    return jnp.stack([_flatten_vectors([shard(n, t) for n in VECTOR_LEAVES], BF16) for t in range(N_CHIPS)])


def _replicated_rows(grads):
    parts = [p.reshape(-1) for n in REPLICATED for pieces, _ in grads[n] for p in pieces]
    flat = jnp.concatenate(parts)
    return jnp.pad(flat, (0, REP_ROWS * REP_COLS - REP_ELEMS)).reshape(REP_ROWS, REP_COLS)


_ANY = pl.BlockSpec(memory_space=pl.ANY)


def _place():
    x, y, c = lax.axis_index("x"), lax.axis_index("y"), lax.axis_index("c")
    chips = [(1 - x, y), (x, 1 - y), (1 - x, 1 - y)]
    return x, y, c, chips


def _chip_index(chip):
    return 2 * chip[0] + chip[1]


def _gather_leaves(halves):
    n = len(halves)

    def body(*refs):
        w_refs, out_refs, (send_sems, recv_sems) = refs[:n], refs[n:2 * n], refs[2 * n:]
        x, y, c, chips = _place()
        me = 2 * x + y
        sibling = (x, y, 1 - c)

        def copy(a, sem, chip_idx, half, to, src=None):
            dst = out_refs[a].at[chip_idx, half]
            return pltpu.make_async_remote_copy(src_ref=dst if src is None else src, dst_ref=dst, send_sem=send_sems.at[6 * a + sem],
                                                recv_sem=recv_sems.at[6 * a + sem], device_id=to, device_id_type=MESH_ID)

        first = [copy(a, j, me, c, (*chip, c), src=w_refs[a].at[c]) for j, chip in enumerate(chips) for a in range(n)]
        for cp in first:
            cp.start()
        passed = []
        for j, chip in enumerate(chips):
            for a in range(n):
                copy(a, j, _chip_index(chip), c, (x, y, c)).wait_recv()
                passed.append(copy(a, 3 + j, _chip_index(chip), c, sibling))
                passed[-1].start()
        for j, chip in enumerate(chips):
            for a in range(n):
                copy(a, 3 + j, _chip_index(chip), 1 - c, (x, y, c)).wait_recv()
        for cp in first + passed:
            cp.wait_send()

    return pl.pallas_call(
        body, out_shape=[jax.ShapeDtypeStruct((N_CHIPS,) + h.shape, h.dtype) for h in halves],
        in_specs=[_ANY] * n, out_specs=[_ANY] * n,
        scratch_shapes=[pltpu.SemaphoreType.DMA((6 * n,)), pltpu.SemaphoreType.DMA((6 * n,))], name="gather_leaves",
    )(*halves)


_HBM = pl.BlockSpec(memory_space=pltpu.HBM)
_SEM = pl.BlockSpec(memory_space=pltpu.SEMAPHORE)


def _in_hbm(a):
    return pltpu.with_memory_space_constraint(a, pltpu.HBM)


def _gather_start(groups, after):
    flat = [h for g in groups for h in g]
    n, ng = len(flat), len(groups)

    def body(*refs):
        srcs, lands, sems = refs[1:1 + n], refs[1 + n:1 + 2 * n], refs[1 + 2 * n:1 + 2 * n + 2 * ng]
        token = refs[-1]
        x, y, c, chips = _place()
        me = 2 * x + y
        a = 0
        for gi, g in enumerate(groups):
            for k in range(len(g)):
                for j, chip in enumerate(chips):
                    pltpu.make_async_remote_copy(src_ref=srcs[a].at[c], dst_ref=lands[a].at[me, c], send_sem=sems[2 * gi].at[3 * k + j],
                                                 recv_sem=sems[2 * gi + 1].at[3 * k + j], device_id=(*chip, c), device_id_type=MESH_ID).start()
                a += 1
        token[...] = jnp.zeros_like(token)

    sem_shapes = [pltpu.SemaphoreType.DMA((3 * len(g),)) for g in groups for _ in range(2)]
    out = pl.pallas_call(
        body, name="gather_start",
        out_shape=sem_shapes + [pltpu.HBM(h.shape, h.dtype) for h in flat] + [pltpu.HBM((N_CHIPS,) + h.shape, h.dtype) for h in flat]
        + [jax.ShapeDtypeStruct((8, LANE), F32)],
        in_specs=[_ANY] + [_HBM] * (2 * n), out_specs=[_SEM] * (2 * ng) + [_HBM] * (2 * n) + [pl.BlockSpec(memory_space=pltpu.VMEM)],
        input_output_aliases={1 + i: 2 * ng + i for i in range(2 * n)},
        compiler_params=pltpu.CompilerParams(has_side_effects=pltpu.SideEffectType.DATAFLOW_SIDE_EFFECTING),
    )(after, *[_in_hbm(h) for h in flat], *[_in_hbm(lax.empty((N_CHIPS,) + h.shape, h.dtype)) for h in flat])
    sems, srcs, lands = out[:2 * ng], out[2 * ng:2 * ng + n], out[2 * ng + n:2 * ng + 2 * n]
    res, a = [], 0
    for gi, g in enumerate(groups):
        res.append((sems[2 * gi], sems[2 * gi + 1], list(srcs[a:a + len(g)]), list(lands[a:a + len(g)])))
        a += len(g)
    return res


def _gather_wait(send_sems, recv_sems, srcs, lands, after, name):
    n = len(srcs)

    def body(*refs):
        src_refs, land_refs, send, recv = refs[:n], refs[n:2 * n], refs[2 * n], refs[2 * n + 1]
        x, y, c, chips = _place()
        for k in range(n):
            for j, chip in enumerate(chips):
                cp = pltpu.make_async_remote_copy(src_ref=src_refs[k].at[c], dst_ref=land_refs[k].at[_chip_index(chip), c], send_sem=send.at[3 * k + j],
                                                  recv_sem=recv.at[3 * k + j], device_id=(*chip, c), device_id_type=MESH_ID)
                cp.wait_send()
                cp.wait_recv()

    out = pl.pallas_call(
        body, name=name, out_shape=[pltpu.HBM(a.shape, a.dtype) for a in list(srcs) + list(lands)],
        in_specs=[_HBM] * (2 * n) + [_SEM, _SEM, _ANY], out_specs=[_HBM] * (2 * n), input_output_aliases={i: i for i in range(2 * n)},
        compiler_params=pltpu.CompilerParams(has_side_effects=pltpu.SideEffectType.DATAFLOW_SIDE_EFFECTING),
    )(*srcs, *lands, send_sems, recv_sems, after)
    return list(out[n:])


def _forward_halves(lands, name):
    n = len(lands)

    def body(*refs):
        out_refs, (send_sems, recv_sems) = refs[n:2 * n], refs[2 * n:]
        x, y, c, chips = _place()

        def copy(a, j, half, to):
            blk = out_refs[a].at[_chip_index(chips[j]), half]
            return pltpu.make_async_remote_copy(src_ref=blk, dst_ref=blk, send_sem=send_sems.at[3 * a + j], recv_sem=recv_sems.at[3 * a + j],
                                                device_id=to, device_id_type=MESH_ID)

        sends = [copy(a, j, c, (x, y, 1 - c)) for a in range(n) for j in range(3)]
        for cp in sends:
            cp.start()
        for a in range(n):
            for j in range(3):
                copy(a, j, 1 - c, (x, y, c)).wait_recv()
        for cp in sends:
            cp.wait_send()

    return pl.pallas_call(
        body, out_shape=[jax.ShapeDtypeStruct(a.shape, a.dtype) for a in lands], in_specs=[_ANY] * n, out_specs=[_ANY] * n,
        input_output_aliases={i: i for i in range(n)},
        scratch_shapes=[pltpu.SemaphoreType.DMA((3 * n,)), pltpu.SemaphoreType.DMA((3 * n,))], name=name,
    )(*lands)


def _swap_halves(slabs):
    n = len(slabs)

    def body(*refs):
        g_refs, got_refs, (send_sem, recv_sem) = refs[:n], refs[n:2 * n], refs[2 * n:]
        x, y, c, _ = _place()
        copies = [pltpu.make_async_remote_copy(src_ref=g_refs[a].at[t, 1 - c], dst_ref=got_refs[a].at[t], send_sem=send_sem.at[N_CHIPS * a + t],
                                               recv_sem=recv_sem.at[N_CHIPS * a + t], device_id=(x, y, 1 - c), device_id_type=MESH_ID)
                  for a in range(n) for t in range(N_CHIPS)]
        for cp in copies:
            cp.start()
        for cp in copies:
            cp.wait()

    return pl.pallas_call(
        body, out_shape=[jax.ShapeDtypeStruct((g.shape[0],) + g.shape[2:], g.dtype) for g in slabs],
        in_specs=[_ANY] * n, out_specs=[_ANY] * n,
        scratch_shapes=[pltpu.SemaphoreType.DMA((N_CHIPS * n,)), pltpu.SemaphoreType.DMA((N_CHIPS * n,))], name="swap_halves",
    )(*slabs)


def _scatter_start(parts, name):
    n = len(parts)

    def body(*refs):
        srcs, lands, send, recv = refs[:n], refs[n:2 * n], refs[2 * n], refs[2 * n + 1]
        token = refs[-1]
        x, y, c, chips = _place()
        me = 2 * x + y
        for k in range(n):
            for j, chip in enumerate(chips):
                pltpu.make_async_remote_copy(src_ref=srcs[k].at[_chip_index(chip)], dst_ref=lands[k].at[me], send_sem=send.at[3 * k + j],
                                             recv_sem=recv.at[3 * k + j], device_id=(*chip, c), device_id_type=MESH_ID).start()
        token[...] = jnp.zeros_like(token)

    out = pl.pallas_call(
        body, name=name,
        out_shape=[pltpu.SemaphoreType.DMA((3 * n,)), pltpu.SemaphoreType.DMA((3 * n,))] + [pltpu.HBM(p.shape, p.dtype) for p in parts] * 2
        + [jax.ShapeDtypeStruct((8, LANE), F32)],
        in_specs=[_HBM] * (2 * n), out_specs=[_SEM, _SEM] + [_HBM] * (2 * n) + [pl.BlockSpec(memory_space=pltpu.VMEM)],
        input_output_aliases={i: 2 + i for i in range(2 * n)},
        compiler_params=pltpu.CompilerParams(has_side_effects=pltpu.SideEffectType.DATAFLOW_SIDE_EFFECTING),
    )(*[_in_hbm(p) for p in parts], *[_in_hbm(lax.empty(p.shape, p.dtype)) for p in parts])
    return (out[0], out[1], list(out[2:2 + n]), list(out[2 + n:2 + 2 * n])), out[-1]


def _scatter_wait(send_sems, recv_sems, srcs, lands, after, name):
    n = len(srcs)

    def body(*refs):
        src_refs, land_refs, send, recv = refs[:n], refs[n:2 * n], refs[2 * n], refs[2 * n + 1]
        x, y, c, chips = _place()
        for k in range(n):
            for j, chip in enumerate(chips):
                cp = pltpu.make_async_remote_copy(src_ref=src_refs[k].at[_chip_index(chip)], dst_ref=land_refs[k].at[_chip_index(chip)],
                                                  send_sem=send.at[3 * k + j], recv_sem=recv.at[3 * k + j], device_id=(*chip, c), device_id_type=MESH_ID)
                cp.wait_send()
                cp.wait_recv()

    out = pl.pallas_call(
        body, name=name, out_shape=[pltpu.HBM(a.shape, a.dtype) for a in list(srcs) + list(lands)],
        in_specs=[_HBM] * (2 * n) + [_SEM, _SEM, _ANY], out_specs=[_HBM] * (2 * n), input_output_aliases={i: i for i in range(2 * n)},
        compiler_params=pltpu.CompilerParams(has_side_effects=pltpu.SideEffectType.DATAFLOW_SIDE_EFFECTING),
    )(*srcs, *lands, send_sems, recv_sems, after)
    return list(out[n:])


def _sibling_halves(halves):
    n = len(halves)

    def body(*refs):
        h_refs, got_refs, (send_sem, recv_sem) = refs[:n], refs[n:2 * n], refs[2 * n:]
        x, y, c, _ = _place()
        copies = [pltpu.make_async_remote_copy(src_ref=h_refs[a], dst_ref=got_refs[a], send_sem=send_sem.at[a], recv_sem=recv_sem.at[a],
                                               device_id=(x, y, 1 - c), device_id_type=MESH_ID) for a in range(n)]
        for cp in copies:
            cp.start()
        for cp in copies:
            cp.wait()

    return pl.pallas_call(
        body, out_shape=[jax.ShapeDtypeStruct(h.shape, h.dtype) for h in halves], in_specs=[_ANY] * n, out_specs=[_ANY] * n,
        scratch_shapes=[pltpu.SemaphoreType.DMA((n,)), pltpu.SemaphoreType.DMA((n,))], name="sibling_halves",
    )(*halves)


def _gather_replicated(rows):
    m_per, n = rows.shape

    def body(x_ref, out_ref, send_sems, recv_sems, local_sem):
        x, y, c, chips = _place()
        me, sibling = (x, y, c), (x, y, 1 - c)

        def block(px, py, pc):
            return out_ref.at[pl.ds((4 * px + 2 * py + pc) * m_per, m_per), :]

        def copy(k, blk, to, src=None):
            return pltpu.make_async_remote_copy(src_ref=block(*blk) if src is None else src, dst_ref=block(*blk),
                                                send_sem=send_sems.at[k], recv_sem=recv_sems.at[k], device_id=to, device_id_type=MESH_ID)

        mine = pltpu.make_async_copy(x_ref, block(*me), local_sem)
        mine.start()
        first = [copy(0, me, sibling, src=x_ref)]
        first += [copy(1 + j, me, (*chip, c), src=x_ref) for j, chip in enumerate(chips)]
        for cp in first:
            cp.start()
        passed = [copy(4 + j, (*chip, c), sibling) for j, chip in enumerate(chips)]
        for j, chip in enumerate(chips):
            copy(1 + j, (*chip, c), me).wait_recv()
            passed[j].start()
        copy(0, sibling, me).wait_recv()
        for j, chip in enumerate(chips):
            copy(4 + j, (*chip, 1 - c), me).wait_recv()
        for cp in first + passed:
            cp.wait_send()
        mine.wait()

    vmem = pl.BlockSpec(memory_space=pltpu.VMEM)
    return pl.pallas_call(
        body, out_shape=jax.ShapeDtypeStruct((N_DEV * m_per, n), rows.dtype), in_specs=[vmem], out_specs=vmem,
        scratch_shapes=[pltpu.SemaphoreType.DMA((7,)), pltpu.SemaphoreType.DMA((7,)), pltpu.SemaphoreType.DMA],
        name="gather_replicated",
    )(rows)


_SUM_ROWS = 256


def _tile_rows(rows):
    if rows <= _SUM_ROWS:
        return rows
    best = None
    for r in range(16, _SUM_ROWS + 1, 16):
        if rows % r == 0:
            best = r
    assert best is not None, rows
    return best


def _add_pairs(slabs, got, core):
    n, _, rows, cols = slabs.shape
    tr = _tile_rows(rows)

    def body(core_ref, a_ref, b_ref, o_ref):
        o_ref[...] = (a_ref[0].astype(F32) + b_ref[...].astype(F32)).astype(BF16)

    spec = pl.BlockSpec((1, tr, cols), lambda t, i, core_ref: (t, i, 0))
    return pl.pallas_call(
        body, out_shape=jax.ShapeDtypeStruct((n, rows, cols), BF16),
        grid_spec=pltpu.PrefetchScalarGridSpec(
            num_scalar_prefetch=1, grid=(n, rows // tr),
            in_specs=[pl.BlockSpec((1, 1, tr, cols), lambda t, i, core_ref: (t, core_ref[0], i, 0)), spec], out_specs=spec),
        compiler_params=_params(("parallel", "parallel")), name=f"add_pairs_{rows}x{cols}",
    )(core, slabs, got)


def _sum_slabs(slabs):
    n, rows, cols = slabs.shape
    tr = _tile_rows(rows)

    def body(s_ref, o_ref):
        acc = s_ref[0].astype(F32)
        for t in range(1, n):
            acc = acc + s_ref[t].astype(F32)
        o_ref[...] = acc

    return pl.pallas_call(
        body, out_shape=jax.ShapeDtypeStruct((rows, cols), F32), grid=(rows // tr,),
        in_specs=[pl.BlockSpec((n, tr, cols), lambda i: (0, i, 0))], out_specs=pl.BlockSpec((tr, cols), lambda i: (i, 0)),
        compiler_params=_params(("parallel",)), name=f"sum_slabs_{n}x{rows}x{cols}",
    )(slabs)


_ADAM_BLOCK_BYTES = 2 * 1024 * 1024


def _adamw(w, g, m, v):
    shape = w.shape
    cols = shape[-1]
    rows = _size(shape) // cols
    tr = rows
    if rows * cols * 4 > _ADAM_BLOCK_BYTES:
        for r in range(8, rows, 8):
            if rows % r == 0 and r * cols * 4 <= _ADAM_BLOCK_BYTES:
                tr = r

    def body(w_ref, g_ref, m_ref, v_ref, d_ref, nm_ref, nv_ref):
        gv = g_ref[...]
        nm = ADAM_B1 * m_ref[...] + (1.0 - ADAM_B1) * gv
        nv = ADAM_B2 * v_ref[...] + (1.0 - ADAM_B2) * jnp.square(gv)
        m_hat = nm / (1.0 - ADAM_B1 ** ADAM_STEP)
        v_hat = nv / (1.0 - ADAM_B2 ** ADAM_STEP)
        d_ref[...] = -ADAM_LR * (m_hat / (jnp.sqrt(v_hat) + ADAM_EPS) + ADAM_WD * w_ref[...])
        nm_ref[...] = nm
        nv_ref[...] = nv

    spec = pl.BlockSpec((tr, cols), lambda i: (i, 0))
    out = pl.pallas_call(
        body, out_shape=tuple(jax.ShapeDtypeStruct((rows, cols), F32) for _ in range(3)), grid=(rows // tr,),
        in_specs=[spec] * 4, out_specs=(spec,) * 3, compiler_params=_params(("parallel",)), name=f"adamw_{rows}x{cols}",
    )(*(a.reshape(rows, cols) for a in (w, g, m, v)))
    return tuple(o.reshape(shape) for o in out)


def kernel(x, mem, g_mix, w_in_a, b_glu, w_dw_a, b_dw_a, ln_g, ln_b, g_kv, w_kvf, b_f, w_in_b, g_mem, w_mem_kv, w_out, g_ffn, w_up, w_dw_f, b_dw_f, w_down, g_final, loss_target, m_g_mix, m_w_in_a, m_b_glu, m_w_dw_a, m_b_dw_a, m_ln_g, m_ln_b, m_g_kv, m_w_kvf, m_b_f, m_w_in_b, m_g_mem, m_w_mem_kv, m_w_out, m_g_ffn, m_w_up, m_w_dw_f, m_b_dw_f, m_w_down, m_g_final, v_g_mix, v_w_in_a, v_b_glu, v_w_dw_a, v_b_dw_a, v_ln_g, v_ln_b, v_g_kv, v_w_kvf, v_b_f, v_w_in_b, v_g_mem, v_w_mem_kv, v_w_out, v_g_ffn, v_w_up, v_w_dw_f, v_b_dw_f, v_w_down, v_g_final):
    weights = dict(g_mix=g_mix, w_in_a=w_in_a, b_glu=b_glu, w_dw_a=w_dw_a, b_dw_a=b_dw_a, ln_g=ln_g, ln_b=ln_b, g_kv=g_kv,
                   w_kvf=w_kvf, b_f=b_f, w_in_b=w_in_b, g_mem=g_mem, w_mem_kv=w_mem_kv, w_out=w_out, g_ffn=g_ffn, w_up=w_up,
                   w_dw_f=w_dw_f, b_dw_f=b_dw_f, w_down=w_down, g_final=g_final)
    mom1 = dict(g_mix=m_g_mix, w_in_a=m_w_in_a, b_glu=m_b_glu, w_dw_a=m_w_dw_a, b_dw_a=m_b_dw_a, ln_g=m_ln_g, ln_b=m_ln_b,
                g_kv=m_g_kv, w_kvf=m_w_kvf, b_f=m_b_f, w_in_b=m_w_in_b, g_mem=m_g_mem, w_mem_kv=m_w_mem_kv, w_out=m_w_out,
                g_ffn=m_g_ffn, w_up=m_w_up, w_dw_f=m_w_dw_f, b_dw_f=m_b_dw_f, w_down=m_w_down, g_final=m_g_final)
    mom2 = dict(g_mix=v_g_mix, w_in_a=v_w_in_a, b_glu=v_b_glu, w_dw_a=v_w_dw_a, b_dw_a=v_b_dw_a, ln_g=v_ln_g, ln_b=v_ln_b,
                g_kv=v_g_kv, w_kvf=v_w_kvf, b_f=v_b_f, w_in_b=v_w_in_b, g_mem=v_g_mem, w_mem_kv=v_w_mem_kv, w_out=v_w_out,
                g_ffn=v_g_ffn, w_up=v_w_up, w_dw_f=v_w_dw_f, b_dw_f=v_b_dw_f, w_down=v_w_down, g_final=v_g_final)

    x_pos, y_pos, core = lax.axis_index("x"), lax.axis_index("y"), lax.axis_index("c")
    chip = 2 * x_pos + y_pos

    mixer_of = lambda l: LAYER_LEAVES[l][:-FFN_LEAVES]
    ffn_of = lambda l: LAYER_LEAVES[l][-FFN_LEAVES:]
    local_first = _layer_halves(weights, mixer_of(0))
    vector_halves = _flatten_vectors([weights[n] for n in VECTOR_LEAVES], F32)
    first = _gather_leaves(local_first + [vector_halves])
    later = [ffn_of(0)] + [LAYER_LEAVES[l] for l in range(1, N_LAYERS)]
    local_later = [_layer_halves(weights, leaves) for leaves in later]
    in_flight = _gather_start(local_later, first[-1])

    def with_own(gathered, own):
        return [lax.dynamic_update_index_in_dim(g, o[None], chip, 0) for g, o in zip(gathered, own)]

    first = with_own(first, local_first + [vector_halves])
    vectors = _unflatten_vectors(first[-1], (N_CHIPS,))
    table = _weight_table({n: weights[n] for n in REPLICATED})

    def fetch(k, after):
        send_sems, recv_sems, srcs, lands = in_flight[k]
        lands = _gather_wait(send_sems, recv_sems, srcs, lands, after, f"gather_wait_{k}")
        return with_own(_forward_halves(lands, f"forward_halves_{k}"), local_later[k])

    def arrive(l, after, part):
        if l == 0:
            gathered = first[:-1] + [None] * FFN_LEAVES if part == "mixer" else [None] * len(mixer_of(0)) + fetch(0, after)
        elif part == "mixer":
            gathered = fetch(l, after)
        else:
            return
        _install_layer(table, l, _layer_pieces(l, gathered, vectors))

    core_index = core.astype(jnp.int32).reshape(1)
    in_flight_back = []

    def emit(l, g, part):
        if l == 0:
            leaves = ffn_of(0) if part == "ffn" else mixer_of(0)
        elif part == "mixer":
            leaves = LAYER_LEAVES[l]
        else:
            return None
        with_vectors = l == 0 and part == "mixer"
        slabs = _layer_slabs(g, leaves) + ([_vector_slabs(g)] if with_vectors else [])
        chip_sums = [_add_pairs(mine, got, core_index) for mine, got in zip(slabs, _swap_halves(slabs))]
        handles, started = _scatter_start(chip_sums, f"scatter_start_{len(in_flight_back)}")
        in_flight_back.append((leaves, with_vectors, chip_sums, handles))
        return started

    loss, dx, grads = _local_step(x[0], mem[0], loss_target[0], table, arrive, emit)
    loss = lax.psum(loss, ("x", "y", "c"))

    names, mine = [], []
    for k, (leaves, with_vectors, chip_sums, (send_sems, recv_sems, srcs, lands)) in enumerate(in_flight_back):
        lands = _scatter_wait(send_sems, recv_sems, srcs, lands, dx, f"scatter_wait_{k}")
        from_chips = [lax.dynamic_update_index_in_dim(got, lax.dynamic_index_in_dim(own, chip, 0, keepdims=True), chip, 0)
                      for got, own in zip(lands, chip_sums)]
        mine += [_sum_slabs(f) for f in from_chips]
        names += list(leaves) + ([("vectors", None)] if with_vectors else [])
    theirs = _sibling_halves(mine)
    per_leaf = {}
    for (name, idx), m, t in zip(names, mine, theirs):
        per_leaf.setdefault(name, {})[idx] = jnp.concatenate([jnp.where(core == 0, m, t), jnp.where(core == 0, t, m)])
    vector_grads = per_leaf.pop("vectors")[None]
    grad_leaves = {n: (d[None] if None in d else jnp.stack([d[i] for i in sorted(d)])) for n, d in per_leaf.items()}
    grad_leaves.update(_unflatten_vectors(vector_grads))

    rep_sum = _sum_slabs(_gather_replicated(_replicated_rows(grads)).reshape(N_DEV, REP_ROWS, REP_COLS)).reshape(-1)
    off = 0
    for n, shp in REPLICATED.items():
        grad_leaves[n] = rep_sum[off:off + _size(shp)].reshape(shp)
        off += _size(shp)

    deltas, new_m, new_v = {}, {}, {}
    for n in WEIGHT_ORDER:
        deltas[n], new_m[n], new_v[n] = _adamw(weights[n], grad_leaves[n], mom1[n], mom2[n])
    return (loss, dx[None], *[grad_leaves[n] for n in WEIGHT_ORDER], *[deltas[n] for n in WEIGHT_ORDER],
            *[new_m[n] for n in WEIGHT_ORDER], *[new_v[n] for n in WEIGHT_ORDER])
```

```python
import jax
import jax.numpy as jnp
from jax import lax
from jax.experimental import pallas as pl
from jax.experimental.pallas import tpu as pltpu

F32 = jnp.float32
BF16 = jnp.bfloat16

D_MODEL = 1024
N_LAYERS = 4
N_A = 2
CONV_CH = 768
MEM_W = 256
HEAD_DIM = 64
N_MEM_HEADS = 4
N_FOX_HEADS = 12
N_HEAD_PAIRS = N_FOX_HEADS // 2
D_FF = 2816
CONV_W = 31
CONV_PAD = 32
FFN_CONV_W = 3
FFN_PAD = 8
F_PAD = 128
RMS_EPS = 1e-6
LN_EPS = 1e-5
ATT_SCALE = HEAD_DIM ** -0.5
NEG_BIG = -1e30

ADAM_LR = 0.001
ADAM_B1 = 0.9
ADAM_B2 = 0.999
ADAM_EPS = 1e-08
ADAM_WD = 0.01
ADAM_STEP = 10

LANE = 128
ROW_TILE = 256
CHUNK = 128
VMEM_LIMIT = 48 * 1024 * 1024
FLAT_COLS = 1024
N_CHIPS = 4
N_DEV = 8
MESH_ID = pl.DeviceIdType.MESH


def _params(sem=None):
    return pltpu.CompilerParams(dimension_semantics=sem, vmem_limit_bytes=VMEM_LIMIT)


def _tile(dim, pref):
    if dim <= pref:
        return dim
    best = None
    for m in range(1, dim // LANE + 1):
        d = m * LANE
        if dim % d == 0 and d <= pref:
            best = d
    assert best is not None, dim
    return best


_DIMS = {"nn": (((1,), (0,)), ((), ())), "nt": (((1,), (1,)), ((), ())), "tn": (((0,), (0,)), ((), ()))}


_MM_RESIDENT_BYTES = 8 * 1024 * 1024
_MM_STREAM_BYTES = 6 * 1024 * 1024
_MM_OUT_BYTES = 6 * 1024 * 1024


def _mm_tiles(m, n, k, mode, a_size, b_size, o_size):
    if mode == "tn":
        tm = _tile(m, _MM_RESIDENT_BYTES // (k * a_size))
        tn = _tile(n, min(_MM_STREAM_BYTES // (k * b_size), _MM_OUT_BYTES // (tm * o_size)))
    else:
        tn = _tile(n, _MM_RESIDENT_BYTES // (k * b_size))
        tm = _tile(m, min(_MM_STREAM_BYTES // (k * a_size), _MM_OUT_BYTES // (tn * o_size), 512))
    return tm, tn


def _mm(a, b, mode="nn", out_dtype=F32, add=None):
    if mode == "nn":
        (m, k), (k2, n) = a.shape, b.shape
    elif mode == "nt":
        (m, k), (n, k2) = a.shape, b.shape
    else:
        (k, m), (k2, n) = a.shape, b.shape
    assert k == k2, (a.shape, b.shape, mode)
    tm, tn = _mm_tiles(m, n, k, mode, a.dtype.itemsize, b.dtype.itemsize, jnp.dtype(out_dtype).itemsize)
    dims = _DIMS[mode]
    has_add = add is not None

    def body(*refs):
        if has_add:
            a_ref, b_ref, add_ref, o_ref = refs
        else:
            a_ref, b_ref, o_ref = refs
        r = lax.dot_general(a_ref[...].astype(BF16), b_ref[...].astype(BF16), dims, preferred_element_type=F32)
        if has_add:
            r = r + add_ref[...]
        o_ref[...] = r.astype(out_dtype)

    a_spec = pl.BlockSpec((k, tm), lambda i, j: (0, i)) if mode == "tn" else pl.BlockSpec((tm, k), lambda i, j: (i, 0))
    b_spec = pl.BlockSpec((tn, k), lambda i, j: (j, 0)) if mode == "nt" else pl.BlockSpec((k, tn), lambda i, j: (0, j))
    o_spec = pl.BlockSpec((tm, tn), lambda i, j: (i, j))
    in_specs = [a_spec, b_spec] + ([o_spec] if has_add else [])
    args = (a, b) + ((add,) if has_add else ())
    return pl.pallas_call(
        body,
        out_shape=jax.ShapeDtypeStruct((m, n), out_dtype),
        grid=(m // tm, n // tn),
        in_specs=in_specs,
        out_specs=o_spec,
        compiler_params=_params(("parallel", "parallel")),
        name=f"mm_{mode}_{m}x{k}x{n}",
    )(*args)


def _row(width):
    return pl.BlockSpec((ROW_TILE, width), lambda i: (i, 0))


def _vec(width):
    return pl.BlockSpec((1, width), lambda i: (0, 0))


def _rms_fwd(x, g):
    rows, d = x.shape

    def body(x_ref, g_ref, o_ref):
        xv = x_ref[...]
        rstd = lax.rsqrt(jnp.mean(xv * xv, axis=-1, keepdims=True) + RMS_EPS)
        o_ref[...] = (xv * rstd * g_ref[...]).astype(BF16)

    return pl.pallas_call(
        body, out_shape=jax.ShapeDtypeStruct((rows, d), BF16), grid=(rows // ROW_TILE,),
        in_specs=[_row(d), _vec(d)], out_specs=_row(d), compiler_params=_params(("parallel",)), name=f"rms_fwd_{rows}",
    )(x, g)


def _accumulate(ref, value, step):
    @pl.when(step == 0)
    def _():
        ref[...] = value

    @pl.when(step > 0)
    def _():
        ref[...] += value


def _rms_bwd(x, g, dh, dres):
    rows, d = x.shape
    has_res = dres is not None

    def body(*refs):
        if has_res:
            x_ref, g_ref, dh_ref, dres_ref, dx_ref, dxb_ref, dg_ref = refs
        else:
            x_ref, g_ref, dh_ref, dx_ref, dxb_ref, dg_ref = refs
        xv = x_ref[...]
        dhv = dh_ref[...]
        rstd = lax.rsqrt(jnp.mean(xv * xv, axis=-1, keepdims=True) + RMS_EPS)
        xhat = xv * rstd
        gd = dhv * g_ref[...]
        dx = rstd * (gd - xhat * jnp.mean(gd * xhat, axis=-1, keepdims=True))
        if has_res:
            dx = dx + dres_ref[...]
        dx_ref[...] = dx
        dxb_ref[...] = dx.astype(BF16)
        _accumulate(dg_ref, jnp.sum(dhv * xhat, axis=0, keepdims=True), pl.program_id(0))

    in_specs = [_row(d), _vec(d), _row(d)] + ([_row(d)] if has_res else [])
    args = (x, g, dh) + ((dres,) if has_res else ())
    return pl.pallas_call(
        body,
        out_shape=(jax.ShapeDtypeStruct((rows, d), F32), jax.ShapeDtypeStruct((rows, d), BF16), jax.ShapeDtypeStruct((1, d), F32)),
        grid=(rows // ROW_TILE,), in_specs=in_specs, out_specs=(_row(d), _row(d), _vec(d)),
        compiler_params=_params(("arbitrary",)), name=f"rms_bwd_{rows}_{int(has_res)}",
    )(*args)


def _loss_head(x, g, target):
    rows, d = x.shape

    def body(x_ref, g_ref, t_ref, loss_ref, dx_ref, dxb_ref, dg_ref):
        xv = x_ref[...]
        gv = g_ref[...]
        rstd = lax.rsqrt(jnp.mean(xv * xv, axis=-1, keepdims=True) + RMS_EPS)
        xhat = xv * rstd
        err = xhat * gv - t_ref[...]
        part = 0.5 * jnp.sum(jnp.mean(err * err, axis=-1, keepdims=True), axis=0, keepdims=True)
        dy = err * (1.0 / d)
        gd = dy * gv
        dx = rstd * (gd - xhat * jnp.mean(gd * xhat, axis=-1, keepdims=True))
        dx_ref[...] = dx
        dxb_ref[...] = dx.astype(BF16)
        step = pl.program_id(0)
        _accumulate(loss_ref, jnp.broadcast_to(part, (1, LANE)), step)
        _accumulate(dg_ref, jnp.sum(dy * xhat, axis=0, keepdims=True), step)

    return pl.pallas_call(
        body,
        out_shape=(jax.ShapeDtypeStruct((1, LANE), F32), jax.ShapeDtypeStruct((rows, d), F32),
                   jax.ShapeDtypeStruct((rows, d), BF16), jax.ShapeDtypeStruct((1, d), F32)),
        grid=(rows // ROW_TILE,), in_specs=[_row(d), _vec(d), _row(d)],
        out_specs=(_vec(LANE), _row(d), _row(d), _vec(d)),
        compiler_params=_params(("arbitrary",)), name="loss_head",
    )(x, g, target)


def _sigmoid(x):
    return 1.0 / (1.0 + jnp.exp(-x))


def _ln_silu_fwd(c, ln_g, ln_b):
    rows, ch = c.shape

    def body(c_ref, g_ref, b_ref, o_ref):
        cv = c_ref[...]
        mu = jnp.mean(cv, axis=-1, keepdims=True)
        cen = cv - mu
        rstd = lax.rsqrt(jnp.mean(cen * cen, axis=-1, keepdims=True) + LN_EPS)
        y = cen * rstd * g_ref[...] + b_ref[...]
        o_ref[...] = (y * _sigmoid(y)).astype(BF16)

    return pl.pallas_call(
        body, out_shape=jax.ShapeDtypeStruct((rows, ch), BF16), grid=(rows // ROW_TILE,),
        in_specs=[_row(ch), _vec(ch), _vec(ch)], out_specs=_row(ch), compiler_params=_params(("parallel",)), name="ln_silu_fwd",
    )(c, ln_g, ln_b)


def _ln_silu_bwd(dmix, c, ln_g, ln_b):
    rows, ch = c.shape

    def body(dm_ref, c_ref, g_ref, b_ref, dc_ref, dg_ref, db_ref):
        cv = c_ref[...]
        gv = g_ref[...]
        mu = jnp.mean(cv, axis=-1, keepdims=True)
        cen = cv - mu
        rstd = lax.rsqrt(jnp.mean(cen * cen, axis=-1, keepdims=True) + LN_EPS)
        xhat = cen * rstd
        y = xhat * gv + b_ref[...]
        sg = _sigmoid(y)
        dy = dm_ref[...] * (sg * (1.0 + y * (1.0 - sg)))
        dxh = dy * gv
        dc = rstd * (dxh - jnp.mean(dxh, axis=-1, keepdims=True) - xhat * jnp.mean(dxh * xhat, axis=-1, keepdims=True))
        dc_ref[...] = dc
        step = pl.program_id(0)
        _accumulate(dg_ref, jnp.sum(dy * xhat, axis=0, keepdims=True), step)
        _accumulate(db_ref, jnp.sum(dy, axis=0, keepdims=True), step)

    return pl.pallas_call(
        body,
        out_shape=(jax.ShapeDtypeStruct((rows, ch), F32), jax.ShapeDtypeStruct((1, ch), F32), jax.ShapeDtypeStruct((1, ch), F32)),
        grid=(rows // ROW_TILE,), in_specs=[_row(ch), _row(ch), _vec(ch), _vec(ch)],
        out_specs=(_row(ch), _vec(ch), _vec(ch)), compiler_params=_params(("arbitrary",)), name="ln_silu_bwd",
    )(dmix, c, ln_g, ln_b)


def _col(rows, cb):
    return pl.BlockSpec((rows, cb), lambda j: (0, j))


SUBLANES = 8
_WINDOW = CHUNK + CONV_PAD


def _realign(win, shifted_ref):
    shifted_ref[0] = win
    for r in range(1, SUBLANES):
        shifted_ref[r, 0:_WINDOW - SUBLANES, :] = win[r:r + _WINDOW - SUBLANES, :]


def _tap(shifted_ref, off):
    r = off % SUBLANES
    return shifted_ref[r, off - r:off - r + CHUNK, :]


def _conv_glu_fwd(pa, pg, ba, bg, w_dw, b_dw):
    seq, ch = pa.shape
    cb = LANE
    n_chunks = seq // CHUNK

    def body(pa_ref, pg_ref, ba_ref, bg_ref, w_ref, b_ref, v_ref, c_ref, vpad_ref, vshift_ref):
        vpad_ref[0:CONV_PAD, :] = jnp.zeros((CONV_PAD, cb), F32)

        def glu(r, carry):
            r0 = pl.multiple_of(r * CHUNK, CHUNK)
            rows = pl.ds(r0, CHUNK)
            v = (pa_ref[rows, :] + ba_ref[...]) * _sigmoid(pg_ref[rows, :] + bg_ref[...])
            v_ref[rows, :] = v
            vpad_ref[pl.ds(r0 + CONV_PAD, CHUNK), :] = v
            return carry

        lax.fori_loop(0, n_chunks, glu, 0)

        def conv(r, carry):
            r0 = pl.multiple_of(r * CHUNK, CHUNK)
            _realign(vpad_ref[pl.ds(r0, _WINDOW), :], vshift_ref)
            acc = jnp.broadcast_to(b_ref[...], (CHUNK, cb))
            for t in range(CONV_W):
                acc = acc + w_ref[t:t + 1, :] * _tap(vshift_ref, CONV_PAD - (CONV_W - 1) + t)
            c_ref[pl.ds(r0, CHUNK), :] = acc
            return carry

        lax.fori_loop(0, n_chunks, conv, 0)

    return pl.pallas_call(
        body,
        out_shape=(jax.ShapeDtypeStruct((seq, ch), F32), jax.ShapeDtypeStruct((seq, ch), F32)),
        grid=(ch // cb,),
        in_specs=[_col(seq, cb), _col(seq, cb), _col(1, cb), _col(1, cb), _col(CONV_PAD, cb), _col(1, cb)],
        out_specs=(_col(seq, cb), _col(seq, cb)),
        scratch_shapes=[pltpu.VMEM((seq + CONV_PAD, cb), F32), pltpu.VMEM((SUBLANES, _WINDOW, cb), F32)],
        compiler_params=_params(("parallel",)), name="conv_glu_fwd",
    )(pa, pg, ba, bg, w_dw, b_dw)


def _conv_glu_bwd(dc, v, pa, pg, ba, bg, w_dw):
    seq, ch = dc.shape
    cb = LANE
    n_chunks = seq // CHUNK

    def body(dc_ref, v_ref, pa_ref, pg_ref, ba_ref, bg_ref, w_ref, da_ref, dg_ref, dw_ref, dbdw_ref, dba_ref, dbg_ref,
             dcpad_ref, vpad_ref, dcshift_ref, vshift_ref):
        vpad_ref[0:CONV_PAD, :] = jnp.zeros((CONV_PAD, cb), F32)
        dcpad_ref[seq:seq + CONV_PAD, :] = jnp.zeros((CONV_PAD, cb), F32)
        dw_ref[...] = jnp.zeros((CONV_PAD, cb), F32)
        dbdw_ref[...] = jnp.zeros((1, cb), F32)
        dba_ref[...] = jnp.zeros((1, cb), F32)
        dbg_ref[...] = jnp.zeros((1, cb), F32)

        def fill(r, carry):
            r0 = pl.multiple_of(r * CHUNK, CHUNK)
            vpad_ref[pl.ds(r0 + CONV_PAD, CHUNK), :] = v_ref[pl.ds(r0, CHUNK), :]
            dcpad_ref[pl.ds(r0, CHUNK), :] = dc_ref[pl.ds(r0, CHUNK), :]
            return carry

        lax.fori_loop(0, n_chunks, fill, 0)

        def step(r, carry):
            r0 = pl.multiple_of(r * CHUNK, CHUNK)
            rows = pl.ds(r0, CHUNK)
            _realign(dcpad_ref[pl.ds(r0, _WINDOW), :], dcshift_ref)
            _realign(vpad_ref[pl.ds(r0, _WINDOW), :], vshift_ref)
            dcur = _tap(dcshift_ref, 0)
            dv = jnp.zeros((CHUNK, cb), F32)
            for t in range(CONV_W):
                dv = dv + w_ref[t:t + 1, :] * _tap(dcshift_ref, CONV_W - 1 - t)
                dw_ref[t:t + 1, :] += jnp.sum(dcur * _tap(vshift_ref, CONV_PAD - (CONV_W - 1) + t), axis=0, keepdims=True)
            a = pa_ref[rows, :] + ba_ref[...]
            sg = _sigmoid(pg_ref[rows, :] + bg_ref[...])
            da = dv * sg
            dgate = dv * a * sg * (1.0 - sg)
            da_ref[rows, :] = da.astype(BF16)
            dg_ref[rows, :] = dgate.astype(BF16)
            dbdw_ref[...] += jnp.sum(dcur, axis=0, keepdims=True)
            dba_ref[...] += jnp.sum(da, axis=0, keepdims=True)
            dbg_ref[...] += jnp.sum(dgate, axis=0, keepdims=True)
            return carry

        lax.fori_loop(0, n_chunks, step, 0)

    return pl.pallas_call(
        body,
        out_shape=(jax.ShapeDtypeStruct((seq, ch), BF16), jax.ShapeDtypeStruct((seq, ch), BF16),
                   jax.ShapeDtypeStruct((CONV_PAD, ch), F32), jax.ShapeDtypeStruct((1, ch), F32),
                   jax.ShapeDtypeStruct((1, ch), F32), jax.ShapeDtypeStruct((1, ch), F32)),
        grid=(ch // cb,),
        in_specs=[_col(seq, cb), _col(seq, cb), _col(seq, cb), _col(seq, cb), _col(1, cb), _col(1, cb), _col(CONV_PAD, cb)],
        out_specs=(_col(seq, cb), _col(seq, cb), _col(CONV_PAD, cb), _col(1, cb), _col(1, cb), _col(1, cb)),
        scratch_shapes=[pltpu.VMEM((seq + CONV_PAD, cb), F32), pltpu.VMEM((seq + CONV_PAD, cb), F32),
                        pltpu.VMEM((SUBLANES, _WINDOW, cb), F32), pltpu.VMEM((SUBLANES, _WINDOW, cb), F32)],
        compiler_params=_params(("parallel",)), name="conv_glu_bwd",
    )(dc, v, pa, pg, ba, bg, w_dw)


def _ffn_conv(pad_ref, r0, w_ref, b_ref, cb):
    win = pad_ref[pl.ds(r0, CHUNK + FFN_PAD), :]
    y = jnp.broadcast_to(b_ref[...], (CHUNK, cb))
    for t in range(FFN_CONV_W):
        off = FFN_PAD - (FFN_CONV_W - 1) + t
        y = y + w_ref[t:t + 1, :] * win[off:off + CHUNK, :]
    return y, win


def _ffn_mid_fwd(ug, uv, wg, wv, bg, bv):
    seq, ch = ug.shape
    cb = _tile(ch, 256)
    n_chunks = seq // CHUNK

    def body(ug_ref, uv_ref, wg_ref, wv_ref, bg_ref, bv_ref, z_ref, gpad_ref, vpad_ref):
        gpad_ref[0:FFN_PAD, :] = jnp.zeros((FFN_PAD, cb), F32)
        vpad_ref[0:FFN_PAD, :] = jnp.zeros((FFN_PAD, cb), F32)

        def fill(r, carry):
            r0 = pl.multiple_of(r * CHUNK, CHUNK)
            gpad_ref[pl.ds(r0 + FFN_PAD, CHUNK), :] = ug_ref[pl.ds(r0, CHUNK), :]
            vpad_ref[pl.ds(r0 + FFN_PAD, CHUNK), :] = uv_ref[pl.ds(r0, CHUNK), :]
            return carry

        lax.fori_loop(0, n_chunks, fill, 0)

        def step(r, carry):
            r0 = pl.multiple_of(r * CHUNK, CHUNK)
            yg, _ = _ffn_conv(gpad_ref, r0, wg_ref, bg_ref, cb)
            yv, _ = _ffn_conv(vpad_ref, r0, wv_ref, bv_ref, cb)
            z_ref[pl.ds(r0, CHUNK), :] = (yg * _sigmoid(yg) * yv).astype(BF16)
            return carry

        lax.fori_loop(0, n_chunks, step, 0)

    return pl.pallas_call(
        body, out_shape=jax.ShapeDtypeStruct((seq, ch), BF16), grid=(ch // cb,),
        in_specs=[_col(seq, cb), _col(seq, cb), _col(FFN_PAD, cb), _col(FFN_PAD, cb), _col(1, cb), _col(1, cb)],
        out_specs=_col(seq, cb),
        scratch_shapes=[pltpu.VMEM((seq + FFN_PAD, cb), F32), pltpu.VMEM((seq + FFN_PAD, cb), F32)],
        compiler_params=_params(("parallel",)), name="ffn_mid_fwd",
    )(ug, uv, wg, wv, bg, bv)


def _ffn_mid_bwd(ug, uv, dz, wg, wv, bg, bv):
    seq, ch = ug.shape
    cb = _tile(ch, 256)
    n_chunks = seq // CHUNK

    def body(ug_ref, uv_ref, dz_ref, wg_ref, wv_ref, bg_ref, bv_ref, dug_ref, duv_ref, dwg_ref, dwv_ref, dbg_ref, dbv_ref,
             gpad_ref, vpad_ref, dyg_ref, dyv_ref):
        gpad_ref[0:FFN_PAD, :] = jnp.zeros((FFN_PAD, cb), F32)
        vpad_ref[0:FFN_PAD, :] = jnp.zeros((FFN_PAD, cb), F32)
        dyg_ref[seq:seq + FFN_PAD, :] = jnp.zeros((FFN_PAD, cb), F32)
        dyv_ref[seq:seq + FFN_PAD, :] = jnp.zeros((FFN_PAD, cb), F32)
        dwg_ref[...] = jnp.zeros((FFN_PAD, cb), F32)
        dwv_ref[...] = jnp.zeros((FFN_PAD, cb), F32)
        dbg_ref[...] = jnp.zeros((1, cb), F32)
        dbv_ref[...] = jnp.zeros((1, cb), F32)

        def fill(r, carry):
            r0 = pl.multiple_of(r * CHUNK, CHUNK)
            gpad_ref[pl.ds(r0 + FFN_PAD, CHUNK), :] = ug_ref[pl.ds(r0, CHUNK), :]
            vpad_ref[pl.ds(r0 + FFN_PAD, CHUNK), :] = uv_ref[pl.ds(r0, CHUNK), :]
            return carry

        lax.fori_loop(0, n_chunks, fill, 0)

        def grads_of_conv_out(r, carry):
            r0 = pl.multiple_of(r * CHUNK, CHUNK)
            rows = pl.ds(r0, CHUNK)
            yg, gwin = _ffn_conv(gpad_ref, r0, wg_ref, bg_ref, cb)
            yv, vwin = _ffn_conv(vpad_ref, r0, wv_ref, bv_ref, cb)
            dzv = dz_ref[rows, :]
            sg = _sigmoid(yg)
            dyg = dzv * yv * (sg * (1.0 + yg * (1.0 - sg)))
            dyv = dzv * yg * sg
            dyg_ref[rows, :] = dyg
            dyv_ref[rows, :] = dyv
            for t in range(FFN_CONV_W):
                off = FFN_PAD - (FFN_CONV_W - 1) + t
                dwg_ref[t:t + 1, :] += jnp.sum(dyg * gwin[off:off + CHUNK, :], axis=0, keepdims=True)
                dwv_ref[t:t + 1, :] += jnp.sum(dyv * vwin[off:off + CHUNK, :], axis=0, keepdims=True)
            dbg_ref[...] += jnp.sum(dyg, axis=0, keepdims=True)
            dbv_ref[...] += jnp.sum(dyv, axis=0, keepdims=True)
            return carry

        lax.fori_loop(0, n_chunks, grads_of_conv_out, 0)

        def grads_of_conv_in(r, carry):
            r0 = pl.multiple_of(r * CHUNK, CHUNK)
            gwin = dyg_ref[pl.ds(r0, CHUNK + FFN_PAD), :]
            vwin = dyv_ref[pl.ds(r0, CHUNK + FFN_PAD), :]
            dug = jnp.zeros((CHUNK, cb), F32)
            duv = jnp.zeros((CHUNK, cb), F32)
            for t in range(FFN_CONV_W):
                off = FFN_CONV_W - 1 - t
                dug = dug + wg_ref[t:t + 1, :] * gwin[off:off + CHUNK, :]
                duv = duv + wv_ref[t:t + 1, :] * vwin[off:off + CHUNK, :]
            dug_ref[pl.ds(r0, CHUNK), :] = dug.astype(BF16)
            duv_ref[pl.ds(r0, CHUNK), :] = duv.astype(BF16)
            return carry

        lax.fori_loop(0, n_chunks, grads_of_conv_in, 0)

    return pl.pallas_call(
        body,
        out_shape=(jax.ShapeDtypeStruct((seq, ch), BF16), jax.ShapeDtypeStruct((seq, ch), BF16),
                   jax.ShapeDtypeStruct((FFN_PAD, ch), F32), jax.ShapeDtypeStruct((FFN_PAD, ch), F32),
                   jax.ShapeDtypeStruct((1, ch), F32), jax.ShapeDtypeStruct((1, ch), F32)),
        grid=(ch // cb,),
        in_specs=[_col(seq, cb), _col(seq, cb), _col(seq, cb), _col(FFN_PAD, cb), _col(FFN_PAD, cb), _col(1, cb), _col(1, cb)],
        out_specs=(_col(seq, cb), _col(seq, cb), _col(FFN_PAD, cb), _col(FFN_PAD, cb), _col(1, cb), _col(1, cb)),
        scratch_shapes=[pltpu.VMEM((seq + FFN_PAD, cb), F32) for _ in range(4)],
        compiler_params=_params(("parallel",)), name="ffn_mid_bwd",
    )(ug, uv, dz, wg, wv, bg, bv)


def _dot(a, b, mode):
    return lax.dot_general(a.astype(BF16), b.astype(BF16), _DIMS[mode], preferred_element_type=F32)


def _head(h):
    return slice(h * HEAD_DIM, (h + 1) * HEAD_DIM)


def _mem_softmax(q, k):
    s = _dot(q, k, "nt") * ATT_SCALE
    e = jnp.exp(s - jnp.max(s, axis=-1, keepdims=True))
    return e / jnp.sum(e, axis=-1, keepdims=True)


MEM_ROW_TILE = 512


def _mem_row(width):
    return pl.BlockSpec((MEM_ROW_TILE, width), lambda i: (i, 0))


def _memattn_fwd(pq, mk, mv):
    seq, w = pq.shape
    m = mk.shape[0]

    def body(q_ref, k_ref, v_ref, o_ref):
        for h in range(N_MEM_HEADS):
            p = _mem_softmax(q_ref[:, _head(h)], k_ref[:, _head(h)])
            o_ref[:, _head(h)] = _dot(p, v_ref[:, _head(h)], "nn").astype(BF16)

    full = pl.BlockSpec((m, w), lambda i: (0, 0))
    return pl.pallas_call(
        body, out_shape=jax.ShapeDtypeStruct((seq, w), BF16), grid=(seq // MEM_ROW_TILE,),
        in_specs=[_mem_row(w), full, full], out_specs=_mem_row(w), compiler_params=_params(("parallel",)), name="memattn_fwd",
    )(pq, mk, mv)


def _memattn_bwd(pq, mk, mv, dmo):
    seq, w = pq.shape
    m = mk.shape[0]

    def body(q_ref, k_ref, v_ref, do_ref, dq_ref, dk_ref, dv_ref):
        step = pl.program_id(0)

        @pl.when(step == 0)
        def _():
            dk_ref[...] = jnp.zeros((m, w), F32)
            dv_ref[...] = jnp.zeros((m, w), F32)

        for h in range(N_MEM_HEADS):
            q, k, v, do = q_ref[:, _head(h)], k_ref[:, _head(h)], v_ref[:, _head(h)], do_ref[:, _head(h)]
            p = _mem_softmax(q, k)
            dp = _dot(do, v, "nt")
            ds = p * (dp - jnp.sum(dp * p, axis=-1, keepdims=True))
            dq_ref[:, _head(h)] = (_dot(ds, k, "nn") * ATT_SCALE).astype(BF16)
            dk_ref[:, _head(h)] += _dot(ds, q, "tn") * ATT_SCALE
            dv_ref[:, _head(h)] += _dot(p, do, "tn")

    full = pl.BlockSpec((m, w), lambda i: (0, 0))
    return pl.pallas_call(
        body,
        out_shape=(jax.ShapeDtypeStruct((seq, w), BF16), jax.ShapeDtypeStruct((m, w), F32), jax.ShapeDtypeStruct((m, w), F32)),
        grid=(seq // MEM_ROW_TILE,), in_specs=[_mem_row(w), full, full, _mem_row(w)], out_specs=(_mem_row(w), full, full),
        compiler_params=_params(("arbitrary",)), name="memattn_bwd",
    )(pq, mk, mv, dmo)


FWD_KEY_TILE = 512
BWD_KEY_TILE = 512


def _causal_t(s_t, q_blk, k_blk, key_tile):
    kpos = k_blk * key_tile + lax.broadcasted_iota(jnp.int32, (key_tile, ROW_TILE), 0)
    qpos = q_blk * ROW_TILE + lax.broadcasted_iota(jnp.int32, (key_tile, ROW_TILE), 1)
    return jnp.where(kpos <= qpos, s_t, NEG_BIG)


def _fox_fwd(q, k, v, cum_cols, cum_rows):
    seq, w = q.shape
    nq = seq // ROW_TILE

    def body(q_ref, k_ref, v_ref, cc_ref, cr_ref, o_ref, ob_ref, lse_ref):
        i = pl.program_id(1)
        qs = [q_ref[:, _head(h)].astype(BF16) for h in range(2)]
        crs = [cr_ref[0, 0, h:h + 1, :] for h in range(2)]

        def step(j, carry):
            rows = pl.ds(pl.multiple_of(j * FWD_KEY_TILE, FWD_KEY_TILE), FWD_KEY_TILE)
            new = []
            for h in range(2):
                m_run, l_run, acc = carry[h]
                s_t = _dot(k_ref[rows, _head(h)], qs[h], "nt") * ATT_SCALE + crs[h] - cc_ref[0, rows, h * HEAD_DIM:h * HEAD_DIM + 1]
                s_t = _causal_t(s_t, i, j, FWD_KEY_TILE)
                m_new = jnp.maximum(m_run, jnp.max(s_t, axis=0, keepdims=True))
                alpha = jnp.exp(m_run - m_new)
                p_t = jnp.exp(s_t - m_new)
                l_new = alpha * l_run + jnp.sum(p_t, axis=0, keepdims=True)
                new.append((m_new, l_new, alpha * acc + _dot(v_ref[rows, _head(h)], p_t, "tn")))
            return tuple(new)

        one_head = (jnp.full((1, ROW_TILE), NEG_BIG, F32), jnp.zeros((1, ROW_TILE), F32), jnp.zeros((HEAD_DIM, ROW_TILE), F32))
        res = lax.fori_loop(0, ((i + 1) * ROW_TILE + FWD_KEY_TILE - 1) // FWD_KEY_TILE, step, (one_head, one_head))
        o = jnp.concatenate([acc / l_run for _, l_run, acc in res], axis=0).T
        o_ref[...] = o
        ob_ref[...] = o.astype(BF16)
        lse_ref[...] = jnp.zeros((1, 1, 8, ROW_TILE), F32)
        for h in range(2):
            lse_ref[0, 0, h:h + 1, :] = res[h][0] + jnp.log(res[h][1])

    blk = pl.BlockSpec((ROW_TILE, LANE), lambda hp, i: (i, hp))
    full = pl.BlockSpec((seq, LANE), lambda hp, i: (0, hp))
    cols = pl.BlockSpec((1, seq, LANE), lambda hp, i: (hp, 0, 0))
    rows = pl.BlockSpec((1, 1, 8, ROW_TILE), lambda hp, i: (hp, i, 0, 0))
    return pl.pallas_call(
        body,
        out_shape=(jax.ShapeDtypeStruct((seq, w), F32), jax.ShapeDtypeStruct((seq, w), BF16),
                   jax.ShapeDtypeStruct((N_HEAD_PAIRS, nq, 8, ROW_TILE), F32)),
        grid=(N_HEAD_PAIRS, nq), in_specs=[blk, full, full, cols, rows], out_specs=(blk, blk, rows),
        compiler_params=_params(("parallel", "parallel")), name="fox_fwd",
    )(q, k, v, cum_cols, cum_rows)


def _fox_bwd(q, k, v, cum_cols, cum_rows, o, lse, do):
    seq, w = q.shape
    nq = seq // ROW_TILE
    nk = seq // BWD_KEY_TILE

    def body(q_ref, k_ref, v_ref, cc_ref, cr_ref, o_ref, lse_ref, do_ref, dq_ref, dk_ref, dv_ref, dcc_ref, dcr_ref):
        kj = pl.program_id(1)

        @pl.when(kj == 0)
        def _():
            dq_ref[...] = jnp.zeros((seq, LANE), F32)
            dcr_ref[...] = jnp.zeros((1, nq, 8, ROW_TILE), F32)

        ks = [k_ref[:, _head(h)].astype(BF16) for h in range(2)]
        vs = [v_ref[:, _head(h)].astype(BF16) for h in range(2)]
        ccs = [cc_ref[0, :, h * HEAD_DIM:h * HEAD_DIM + 1] for h in range(2)]
        ones = jnp.ones((8, HEAD_DIM), BF16)

        def step(i, carry):
            rows = pl.ds(pl.multiple_of(i * ROW_TILE, ROW_TILE), ROW_TILE)
            new = []
            for h in range(2):
                dk, dv, fold = carry[h]
                qh = q_ref[rows, _head(h)].astype(BF16)
                doh = do_ref[rows, _head(h)]
                s_t = _dot(ks[h], qh, "nt") * ATT_SCALE + cr_ref[0, i, h:h + 1, :] - ccs[h]
                p_t = jnp.exp(_causal_t(s_t, i, kj, BWD_KEY_TILE) - lse_ref[0, i, h:h + 1, :])
                dp_t = _dot(vs[h], doh, "nt")
                hi, mid, lo = _split3(doh * o_ref[rows, _head(h)])
                row_sum = lambda part: lax.dot_general(ones, part, _DIMS["nt"], preferred_element_type=F32)
                delta = ((row_sum(lo) + row_sum(mid)) + row_sum(hi))[0:1, :]
                ds_t = p_t * (dp_t - delta)
                dq_ref[rows, _head(h)] += _dot(ds_t, ks[h], "tn") * ATT_SCALE
                dcr_ref[0, i, h:h + 1, :] += jnp.sum(ds_t, axis=0, keepdims=True)
                new.append((dk + _dot(ds_t, qh, "nn") * ATT_SCALE, dv + _dot(p_t, doh, "nn"), fold + (ds_t[:, :LANE] + ds_t[:, LANE:])))
            return tuple(new)

        one_head = (jnp.zeros((BWD_KEY_TILE, HEAD_DIM), F32), jnp.zeros((BWD_KEY_TILE, HEAD_DIM), F32), jnp.zeros((BWD_KEY_TILE, LANE), F32))
        res = lax.fori_loop(kj * BWD_KEY_TILE // ROW_TILE, nq, step, (one_head, one_head))
        for h in range(2):
            dk_ref[:, _head(h)] = res[h][0]
            dv_ref[:, _head(h)] = res[h][1]
            dcc_ref[0, :, _head(h)] = jnp.broadcast_to(-jnp.sum(res[h][2], axis=-1, keepdims=True), (BWD_KEY_TILE, HEAD_DIM))

    blk = pl.BlockSpec((BWD_KEY_TILE, LANE), lambda hp, j: (j, hp))
    full = pl.BlockSpec((seq, LANE), lambda hp, j: (0, hp))
    cols = pl.BlockSpec((1, BWD_KEY_TILE, LANE), lambda hp, j: (hp, j, 0))
    rows = pl.BlockSpec((1, nq, 8, ROW_TILE), lambda hp, j: (hp, 0, 0, 0))
    return pl.pallas_call(
        body,
        out_shape=(jax.ShapeDtypeStruct((seq, w), F32), jax.ShapeDtypeStruct((seq, w), F32), jax.ShapeDtypeStruct((seq, w), F32),
                   jax.ShapeDtypeStruct((N_HEAD_PAIRS, seq, LANE), F32), jax.ShapeDtypeStruct((N_HEAD_PAIRS, nq, 8, ROW_TILE), F32)),
        grid=(N_HEAD_PAIRS, nk), in_specs=[full, blk, blk, cols, rows, full, rows, full],
        out_specs=(full, blk, blk, cols, rows),
        compiler_params=_params(("parallel", "arbitrary")), name="fox_bwd",
    )(q, k, v, cum_cols, cum_rows, o, lse, do)


def _split3(x):
    hi = x.astype(BF16)
    r1 = x - hi.astype(F32)
    mid = r1.astype(BF16)
    lo = (r1 - mid.astype(F32)).astype(BF16)
    return hi, mid, lo


def _tri_sum(tri, x):
    hi, mid, lo = _split3(x)
    dot = lambda p: lax.dot_general(tri, p, _DIMS["nn"], preferred_element_type=F32)
    return (dot(lo) + dot(mid)) + dot(hi)


def _tri(lower):
    r = lax.broadcasted_iota(jnp.int32, (LANE, LANE), 0)
    c = lax.broadcasted_iota(jnp.int32, (LANE, LANE), 1)
    return jnp.where((c <= r) if lower else (c >= r), 1.0, 0.0).astype(BF16)


def _fgate_fwd(fl, bf):
    seq, w = fl.shape
    nb = seq // LANE

    def body(f_ref, b_ref, cum_ref):
        tri = _tri(True)

        def step(i, carry):
            rows = pl.ds(pl.multiple_of(i * LANE, LANE), LANE)
            x = f_ref[rows, :] + b_ref[...]
            logsig = jnp.minimum(x, 0.0) - jnp.log(1.0 + jnp.exp(-jnp.abs(x)))
            cum = _tri_sum(tri, logsig) + carry
            cum_ref[rows, :] = cum
            return cum[LANE - 1:LANE, :]

        lax.fori_loop(0, nb, step, jnp.zeros((1, w), F32))

    return pl.pallas_call(body, out_shape=jax.ShapeDtypeStruct((seq, w), F32), compiler_params=_params(), name="fgate_fwd")(fl, bf)


def _fgate_bwd(fl, bf, dcum):
    seq, w = fl.shape
    nb = seq // LANE

    def body(f_ref, b_ref, dc_ref, df_ref, db_ref):
        tri = _tri(False)

        def step(i, carry):
            tail, db = carry
            rows = pl.ds(pl.multiple_of((nb - 1 - i) * LANE, LANE), LANE)
            suffix = _tri_sum(tri, dc_ref[rows, :]) + tail
            df = suffix * (1.0 - _sigmoid(f_ref[rows, :] + b_ref[...]))
            df_ref[rows, :] = df
            return suffix[0:1, :], db + jnp.sum(df, axis=0, keepdims=True)

        _, db = lax.fori_loop(0, nb, step, (jnp.zeros((1, w), F32), jnp.zeros((1, w), F32)))
        db_ref[...] = db

    return pl.pallas_call(
        body, out_shape=(jax.ShapeDtypeStruct((seq, w), F32), jax.ShapeDtypeStruct((1, w), F32)),
        compiler_params=_params(), name="fgate_bwd",
    )(fl, bf, dcum)


def _cum_layouts(cum):
    seq = cum.shape[0]
    nblk = seq // ROW_TILE
    heads = cum[:, :N_FOX_HEADS]
    cq = jnp.repeat(heads, HEAD_DIM, axis=1).reshape(seq, N_HEAD_PAIRS, LANE).transpose(1, 0, 2)
    ck = heads.T.reshape(N_HEAD_PAIRS, 2, nblk, ROW_TILE).transpose(0, 2, 1, 3)
    ck = jnp.pad(ck, ((0, 0), (0, 0), (0, 6), (0, 0)))
    return cq, ck


def _dcum_from_layouts(dcq, dck):
    seq = dcq.shape[1]
    key_side = dck[:, :, :2, :].transpose(0, 2, 1, 3).reshape(N_FOX_HEADS, seq).T
    query_side = dcq[:, :, ::HEAD_DIM].transpose(1, 0, 2).reshape(seq, N_FOX_HEADS)
    return jnp.pad(key_side + query_side, ((0, 0), (0, F_PAD - N_FOX_HEADS)))


def _local_step(x, mem, target, w, arrive, emit):
    vec = lambda a: a.reshape(1, -1)
    mem_n = _rms_fwd(mem, vec(w["g_mem"]))
    saved = []
    shared = None
    for l in range(N_LAYERS):
        arrive(l, x, "mixer")
        s = {"x_in": x}
        s["h"] = h = _rms_fwd(x, vec(w["g_mix"][l]))
        s["mk"] = mk = _mm(mem_n, w["w_mk"][l])
        s["mv"] = mv = _mm(mem_n, w["w_mv"][l])
        if l < N_A:
            s["pa"] = pa = _mm(h, w["w_a"][l])
            s["pg"] = pg = _mm(h, w["w_g"][l])
            s["pq"] = pq = _mm(h, w["w_qm"][l])
            s["v"], s["c"] = _conv_glu_fwd(pa, pg, vec(w["b_glu_a"][l]), vec(w["b_glu_g"][l]), w["w_dw_a"][l], vec(w["b_dw_a"][l]))
            s["mix"] = mix = _ln_silu_fwd(s["c"], vec(w["ln_g"][l]), vec(w["ln_b"][l]))
        else:
            if l == N_A:
                shared = {"x_in": x}
                shared["hk"] = hk = _rms_fwd(x, vec(w["g_kv"]))
                shared["k"] = _mm(hk, w["w_k"])
                shared["v"] = _mm(hk, w["w_v"])
                shared["fl"] = _mm(hk, w["w_f"])
                cum = _fgate_fwd(shared["fl"], w["b_f"])
                shared["cq"], shared["ck"] = _cum_layouts(cum)
            s["q"] = q = _mm(h, w["w_q"][l - N_A])
            s["pq"] = pq = _mm(h, w["w_qm"][l])
            s["o"], mix, s["lse"] = _fox_fwd(q, shared["k"], shared["v"], shared["cq"], shared["ck"])
            s["mix"] = mix
        s["mo"] = mo = _memattn_fwd(pq, mk, mv)
        x = _mm(mix, w["w_o_mix"][l], add=x)
        s["x_mid"] = x = _mm(mo, w["w_o_mem"][l], add=x)
        arrive(l, x, "ffn")
        s["hf"] = hf = _rms_fwd(x, vec(w["g_ffn"][l]))
        s["ug"] = ug = _mm(hf, w["w_up_g"][l])
        s["uv"] = uv = _mm(hf, w["w_up_v"][l])
        s["z"] = z = _ffn_mid_fwd(ug, uv, w["w_dw_f_g"][l], w["w_dw_f_v"][l], vec(w["b_dw_f_g"][l]), vec(w["b_dw_f_v"][l]))
        x = _mm(z, w["w_down"][l], add=x)
        saved.append(s)

    loss_row, dx, dxb, dg_final = _loss_head(x, vec(w["g_final"]), target)
    loss = loss_row[0, 0]

    g = {n: [None] * N_LAYERS for n in ("g_mix", "w_mem_kv", "w_out", "g_ffn", "w_up", "w_dw_f", "b_dw_f", "w_down")}
    for n in ("w_in_a", "b_glu", "w_dw_a", "b_dw_a", "ln_g", "ln_b", "w_in_b"):
        g[n] = [None] * N_A
    one = lambda arr: ([arr], 0)
    after_start = lambda started, grad: grad if started is None else grad + started[0, 0].astype(BF16)
    g["g_final"] = [one(dg_final[0])]
    dmem_n = None
    dk_sum = dv_sum = dcq_sum = dck_sum = None
    for l in reversed(range(N_LAYERS)):
        s = saved[l]
        dz = _mm(dxb, w["w_down"][l], "nt")
        g["w_down"][l] = one(_mm(s["z"], dxb, "tn", BF16))
        dug, duv, dwg, dwv, dbg, dbv = _ffn_mid_bwd(s["ug"], s["uv"], dz, w["w_dw_f_g"][l], w["w_dw_f_v"][l],
                                                    vec(w["b_dw_f_g"][l]), vec(w["b_dw_f_v"][l]))
        g["w_up"][l] = ([_mm(s["hf"], dug, "tn", BF16), _mm(s["hf"], duv, "tn", BF16)], 1)
        g["w_dw_f"][l] = ([dwg[:FFN_CONV_W], dwv[:FFN_CONV_W]], 1)
        g["b_dw_f"][l] = ([dbg[0], dbv[0]], 0)
        dhf = _mm(duv, w["w_up_v"][l], "nt", add=_mm(dug, w["w_up_g"][l], "nt"))
        dx, dxb, dg_ffn = _rms_bwd(s["x_mid"], vec(w["g_ffn"][l]), dhf, dx)
        g["g_ffn"][l] = one(dg_ffn[0])
        dxb = after_start(emit(l, g, "ffn"), dxb)
        dmix = _mm(dxb, w["w_o_mix"][l], "nt")
        dmo = _mm(dxb, w["w_o_mem"][l], "nt")
        g["w_out"][l] = ([_mm(s["mix"], dxb, "tn", BF16), _mm(s["mo"], dxb, "tn", BF16)], 0)
        dpq, dmk, dmv = _memattn_bwd(s["pq"], s["mk"], s["mv"], dmo)
        g["w_mem_kv"][l] = ([_mm(mem_n, dmk, "tn", BF16), _mm(mem_n, dmv, "tn", BF16)], 1)
        dmem_n = _mm(dmk, w["w_mk"][l], "nt", add=dmem_n)
        dmem_n = _mm(dmv, w["w_mv"][l], "nt", add=dmem_n)
        if l < N_A:
            dc, dlng, dlnb = _ln_silu_bwd(dmix, s["c"], vec(w["ln_g"][l]), vec(w["ln_b"][l]))
            da, dgate, dwdw, dbdw, dba, dbg2 = _conv_glu_bwd(dc, s["v"], s["pa"], s["pg"], vec(w["b_glu_a"][l]),
                                                              vec(w["b_glu_g"][l]), w["w_dw_a"][l])
            g["ln_g"][l], g["ln_b"][l], g["b_dw_a"][l] = one(dlng[0]), one(dlnb[0]), one(dbdw[0])
            g["w_dw_a"][l] = ([dwdw[:CONV_W]], 1)
            g["b_glu"][l] = ([dba[0], dbg2[0]], 0)
            g["w_in_a"][l] = ([_mm(s["h"], da, "tn", BF16), _mm(s["h"], dgate, "tn", BF16), _mm(s["h"], dpq, "tn", BF16)], 1)
            dh = _mm(da, w["w_a"][l], "nt")
            dh = _mm(dgate, w["w_g"][l], "nt", add=dh)
            dh = _mm(dpq, w["w_qm"][l], "nt", add=dh)
        else:
            dq, dk, dv, dcq, dck = _fox_bwd(s["q"], shared["k"], shared["v"], shared["cq"], shared["ck"], s["o"], s["lse"], dmix)
            dk_sum = dk if dk_sum is None else dk_sum + dk
            dv_sum = dv if dv_sum is None else dv_sum + dv
            dcq_sum = dcq if dcq_sum is None else dcq_sum + dcq
            dck_sum = dck if dck_sum is None else dck_sum + dck
            g["w_in_b"][l - N_A] = ([_mm(s["h"], dq, "tn", BF16), _mm(s["h"], dpq, "tn", BF16)], 1)
            dh = _mm(dq, w["w_q"][l - N_A], "nt")
            dh = _mm(dpq, w["w_qm"][l], "nt", add=dh)
        dx, dxb, dg_mix = _rms_bwd(s["x_in"], vec(w["g_mix"][l]), dh, dx)
        g["g_mix"][l] = one(dg_mix[0])
        if l == N_A:
            df, dbf = _fgate_bwd(shared["fl"], w["b_f"], _dcum_from_layouts(dcq_sum, dck_sum))
            hk = shared["hk"]
            g["w_kvf"] = [([_mm(hk, dk_sum, "tn", BF16), _mm(hk, dv_sum, "tn", BF16), _mm(hk, df, "tn", BF16)[:, :N_FOX_HEADS]], 1)]
            g["b_f"] = [one(dbf[0, :N_FOX_HEADS])]
            dhk = _mm(dk_sum, w["w_k"], "nt")
            dhk = _mm(dv_sum, w["w_v"], "nt", add=dhk)
            dhk = _mm(df, w["w_f"], "nt", add=dhk)
            dx, dxb, dg_kv = _rms_bwd(shared["x_in"], vec(w["g_kv"]), dhk, dx)
            g["g_kv"] = [one(dg_kv[0])]
        dxb = after_start(emit(l, g, "mixer"), dxb)
    _, _, dg_mem = _rms_bwd(mem, vec(w["g_mem"]), dmem_n, None)
    g["g_mem"] = [one(dg_mem[0])]
    return loss, dx, g


SHARDED = {
    "w_in_a": ((N_A, D_MODEL, 2 * CONV_CH + MEM_W), 2),
    "b_glu": ((N_A, 2 * CONV_CH), 1),
    "w_dw_a": ((N_A, CONV_W, CONV_CH), 2),
    "b_dw_a": ((N_A, CONV_CH), 1),
    "ln_g": ((N_A, CONV_CH), 1),
    "ln_b": ((N_A, CONV_CH), 1),
    "w_kvf": ((D_MODEL, 2 * CONV_CH + N_FOX_HEADS), 1),
    "w_in_b": ((N_LAYERS - N_A, D_MODEL, CONV_CH + MEM_W), 1),
    "w_mem_kv": ((N_LAYERS, D_MODEL, 2 * MEM_W), 1),
    "w_out": ((N_LAYERS, D_MODEL, D_MODEL), 1),
    "w_up": ((N_LAYERS, D_MODEL, 2 * D_FF), 2),
    "w_dw_f": ((N_LAYERS, FFN_CONV_W, 2 * D_FF), 2),
    "w_down": ((N_LAYERS, D_FF, D_MODEL), 1),
}
REPLICATED = {
    "g_mix": (N_LAYERS, D_MODEL), "g_kv": (D_MODEL,), "b_f": (N_FOX_HEADS,), "g_mem": (D_MODEL,),
    "g_ffn": (N_LAYERS, D_MODEL), "b_dw_f": (N_LAYERS, 2 * D_FF), "g_final": (D_MODEL,),
}
WEIGHT_ORDER = ["g_mix", "w_in_a", "b_glu", "w_dw_a", "b_dw_a", "ln_g", "ln_b", "g_kv", "w_kvf", "b_f", "w_in_b", "g_mem",
                "w_mem_kv", "w_out", "g_ffn", "w_up", "w_dw_f", "b_dw_f", "w_down", "g_final"]


def _shard_shape(name):
    shape, axis = SHARDED[name]
    return tuple(d // N_CHIPS if i == axis else d for i, d in enumerate(shape))


def _size(shape):
    n = 1
    for d in shape:
        n *= d
    return n


VECTOR_LEAVES = ("b_glu", "w_dw_a", "b_dw_a", "ln_g", "ln_b", "w_dw_f")
_HALF_ROW_TILE = 16
_VECTOR_ELEMS = sum(_size(_shard_shape(n)) for n in VECTOR_LEAVES)
VECTOR_ROWS = -(-_VECTOR_ELEMS // (FLAT_COLS * 2 * _HALF_ROW_TILE)) * 2 * _HALF_ROW_TILE
REP_ELEMS = sum(_size(s) for s in REPLICATED.values())
REP_ROWS = 8
REP_COLS = -(-REP_ELEMS // (REP_ROWS * LANE)) * LANE


def _flatten_vectors(parts, dtype):
    flat = jnp.concatenate([p.reshape(-1).astype(dtype) for p in parts])
    return jnp.pad(flat, (0, VECTOR_ROWS * FLAT_COLS - _VECTOR_ELEMS)).reshape(2, VECTOR_ROWS // 2, FLAT_COLS)


def _unflatten_vectors(flat, lead=()):
    flat = flat.reshape(lead + (-1,))
    out, off = {}, 0
    for n in VECTOR_LEAVES:
        shp = _shard_shape(n)
        out[n] = flat[..., off:off + _size(shp)].reshape(lead + shp)
        off += _size(shp)
    return out


def _span(parts, cat_axis, sel_axis, lo, hi):
    if sel_axis != cat_axis:
        return jnp.concatenate([lax.slice_in_dim(p, lo, hi, axis=sel_axis) for p in parts], axis=cat_axis)
    taken, off = [], 0
    for p in parts:
        n = p.shape[cat_axis]
        a, b = max(lo, off), min(hi, off + n)
        if a < b:
            taken.append(lax.slice_in_dim(p, a - off, b - off, axis=cat_axis))
        off += n
    return taken[0] if len(taken) == 1 else jnp.concatenate(taken, axis=cat_axis)


FFN_LEAVES = 2
LAYER_LEAVES = tuple(
    ([("w_in_a", l)] if l < N_A else ([("w_kvf", None)] if l == N_A else []) + [("w_in_b", l - N_A)])
    + [("w_mem_kv", l), ("w_out", l), ("w_up", l), ("w_down", l)]
    for l in range(N_LAYERS))


def _layer_halves(weights, leaves):
    out = []
    for name, idx in leaves:
        shard = weights[name] if idx is None else weights[name][idx]
        rows, cols = shard.shape
        assert (rows // 2) % _HALF_ROW_TILE == 0, name
        out.append(shard.astype(BF16).reshape(2, rows // 2, cols))
    return out


def _layer_pieces(l, gathered, vectors):
    parts = {}
    for (name, idx), g in zip(LAYER_LEAVES[l], gathered):
        if g is not None:
            parts[name] = g

    def cut(name, sel_axis, lo, hi):
        g = parts[name]
        axis = SHARDED[name][1] - (len(SHARDED[name][0]) - 2)
        if axis == 0:
            return lax.slice_in_dim(g.reshape(-1, g.shape[-1]), lo, hi, axis=sel_axis)
        return _span([g[t].reshape(-1, g.shape[-1]) for t in range(N_CHIPS)], axis, sel_axis, lo, hi)

    def vcut(name, idx, sel_axis, lo, hi):
        axis = SHARDED[name][1] - 1
        return _span([vectors[name][t, idx] for t in range(N_CHIPS)], axis, sel_axis, lo, hi)

    def pad_rows(arr, rows):
        return jnp.pad(arr, ((0, rows - arr.shape[0]), (0, 0)))

    w = {}
    if "w_up" in parts:
        w["w_up_g"] = cut("w_up", 1, 0, D_FF)
        w["w_up_v"] = cut("w_up", 1, D_FF, 2 * D_FF)
        w["w_dw_f_g"] = pad_rows(vcut("w_dw_f", l, 1, 0, D_FF), FFN_PAD)
        w["w_dw_f_v"] = pad_rows(vcut("w_dw_f", l, 1, D_FF, 2 * D_FF), FFN_PAD)
        w["w_down"] = cut("w_down", 0, 0, D_FF)
    if "w_out" not in parts:
        return w
    if l < N_A:
        w["w_a"] = cut("w_in_a", 1, 0, CONV_CH)
        w["w_g"] = cut("w_in_a", 1, CONV_CH, 2 * CONV_CH)
        w["w_qm"] = cut("w_in_a", 1, 2 * CONV_CH, 2 * CONV_CH + MEM_W)
        w["b_glu_a"] = vcut("b_glu", l, 0, 0, CONV_CH)
        w["b_glu_g"] = vcut("b_glu", l, 0, CONV_CH, 2 * CONV_CH)
        w["w_dw_a"] = pad_rows(vcut("w_dw_a", l, 1, 0, CONV_CH), CONV_PAD)
        for n in ("b_dw_a", "ln_g", "ln_b"):
            w[n] = vcut(n, l, 0, 0, CONV_CH)
    else:
        if l == N_A:
            w["w_k"] = cut("w_kvf", 1, 0, CONV_CH)
            w["w_v"] = cut("w_kvf", 1, CONV_CH, 2 * CONV_CH)
            w["w_f"] = jnp.pad(cut("w_kvf", 1, 2 * CONV_CH, 2 * CONV_CH + N_FOX_HEADS), ((0, 0), (0, F_PAD - N_FOX_HEADS)))
        w["w_q"] = cut("w_in_b", 1, 0, CONV_CH)
        w["w_qm"] = cut("w_in_b", 1, CONV_CH, CONV_CH + MEM_W)
    w["w_mk"] = cut("w_mem_kv", 1, 0, MEM_W)
    w["w_mv"] = cut("w_mem_kv", 1, MEM_W, 2 * MEM_W)
    w["w_o_mix"] = cut("w_out", 0, 0, CONV_CH)
    w["w_o_mem"] = cut("w_out", 0, CONV_CH, D_MODEL)
    return w


_PER_LAYER = ("w_a", "w_g", "w_q", "w_qm", "b_glu_a", "b_glu_g", "w_dw_a", "b_dw_a", "ln_g", "ln_b", "w_mk", "w_mv", "w_o_mix", "w_o_mem",
              "w_up_g", "w_up_v", "w_dw_f_g", "w_dw_f_v", "w_down")


def _weight_table(rep):
    w = dict(rep)
    w["b_f"] = jnp.pad(rep["b_f"], (0, F_PAD - N_FOX_HEADS)).reshape(1, F_PAD)
    w["b_dw_f_g"], w["b_dw_f_v"] = rep["b_dw_f"][:, :D_FF], rep["b_dw_f"][:, D_FF:]
    for n in _PER_LAYER:
        w[n] = {}
    return w


def _install_layer(w, l, pieces):
    for n, p in pieces.items():
        if n in _PER_LAYER:
            w[n][l - N_A if n == "w_q" else l] = p
        else:
            w[n] = p


def _shard_of(entry, name, t):
    shape, axis = SHARDED[name]
    pieces, cat_axis = entry
    axis -= len(shape) - len(pieces[0].shape)
    step = shape[SHARDED[name][1]] // N_CHIPS
    return _span(pieces, cat_axis, axis, t * step, (t + 1) * step)


def _layer_slabs(grads, leaves):
    out = []
    for name, idx in leaves:
        pieces, cat_axis = grads[name][0 if idx is None else idx]
        if cat_axis == 0 and SHARDED[name][1] - (len(SHARDED[name][0]) - 2) == 0:
            whole = pieces[0] if len(pieces) == 1 else jnp.concatenate(pieces, axis=0)
            out.append(whole.astype(BF16).reshape(N_CHIPS, 2, whole.shape[0] // (2 * N_CHIPS), whole.shape[1]))
            continue
        shards = [_shard_of((pieces, cat_axis), name, t).astype(BF16) for t in range(N_CHIPS)]
        rows, cols = shards[0].shape
        out.append(jnp.stack(shards).reshape(N_CHIPS, 2, rows // 2, cols))
    return out


def _vector_slabs(grads):
    def shard(name, t):
        return jnp.stack([_shard_of(entry, name, t) for entry in grads[name]])
    return jnp.stack([_flatten_vectors([shard(n, t) for n in VECTOR_LEAVES], BF16) for t in range(N_CHIPS)])


def _replicated_rows(grads):
    parts = [p.reshape(-1) for n in REPLICATED for pieces, _ in grads[n] for p in pieces]
    flat = jnp.concatenate(parts)
    return jnp.pad(flat, (0, REP_ROWS * REP_COLS - REP_ELEMS)).reshape(REP_ROWS, REP_COLS)


_ANY = pl.BlockSpec(memory_space=pl.ANY)


def _place():
    x, y, c = lax.axis_index("x"), lax.axis_index("y"), lax.axis_index("c")
    chips = [(1 - x, y), (x, 1 - y), (1 - x, 1 - y)]
    return x, y, c, chips


def _chip_index(chip):
    return 2 * chip[0] + chip[1]


def _gather_leaves(halves):
    n = len(halves)

    def body(*refs):
        w_refs, out_refs, (send_sems, recv_sems) = refs[:n], refs[n:2 * n], refs[2 * n:]
        x, y, c, chips = _place()
        me = 2 * x + y
        sibling = (x, y, 1 - c)

        def copy(a, sem, chip_idx, half, to, src=None):
            dst = out_refs[a].at[chip_idx, half]
            return pltpu.make_async_remote_copy(src_ref=dst if src is None else src, dst_ref=dst, send_sem=send_sems.at[6 * a + sem],
                                                recv_sem=recv_sems.at[6 * a + sem], device_id=to, device_id_type=MESH_ID)

        first = [copy(a, j, me, c, (*chip, c), src=w_refs[a].at[c]) for j, chip in enumerate(chips) for a in range(n)]
        for cp in first:
            cp.start()
        passed = []
        for j, chip in enumerate(chips):
            for a in range(n):
                copy(a, j, _chip_index(chip), c, (x, y, c)).wait_recv()
                passed.append(copy(a, 3 + j, _chip_index(chip), c, sibling))
                passed[-1].start()
        for j, chip in enumerate(chips):
            for a in range(n):
                copy(a, 3 + j, _chip_index(chip), 1 - c, (x, y, c)).wait_recv()
        for cp in first + passed:
            cp.wait_send()

    return pl.pallas_call(
        body, out_shape=[jax.ShapeDtypeStruct((N_CHIPS,) + h.shape, h.dtype) for h in halves],
        in_specs=[_ANY] * n, out_specs=[_ANY] * n,
        scratch_shapes=[pltpu.SemaphoreType.DMA((6 * n,)), pltpu.SemaphoreType.DMA((6 * n,))], name="gather_leaves",
    )(*halves)


_HBM = pl.BlockSpec(memory_space=pltpu.HBM)
_SEM = pl.BlockSpec(memory_space=pltpu.SEMAPHORE)


def _in_hbm(a):
    return pltpu.with_memory_space_constraint(a, pltpu.HBM)


def _gather_start(groups, after):
    flat = [h for g in groups for h in g]
    n, ng = len(flat), len(groups)

    def body(*refs):
        srcs, lands, sems = refs[1:1 + n], refs[1 + n:1 + 2 * n], refs[1 + 2 * n:1 + 2 * n + 2 * ng]
        token = refs[-1]
        x, y, c, chips = _place()
        me = 2 * x + y
        a = 0
        for gi, g in enumerate(groups):
            for k in range(len(g)):
                for j, chip in enumerate(chips):
                    pltpu.make_async_remote_copy(src_ref=srcs[a].at[c], dst_ref=lands[a].at[me, c], send_sem=sems[2 * gi].at[3 * k + j],
                                                 recv_sem=sems[2 * gi + 1].at[3 * k + j], device_id=(*chip, c), device_id_type=MESH_ID).start()
                a += 1
        token[...] = jnp.zeros_like(token)

    sem_shapes = [pltpu.SemaphoreType.DMA((3 * len(g),)) for g in groups for _ in range(2)]
    out = pl.pallas_call(
        body, name="gather_start",
        out_shape=sem_shapes + [pltpu.HBM(h.shape, h.dtype) for h in flat] + [pltpu.HBM((N_CHIPS,) + h.shape, h.dtype) for h in flat]
        + [jax.ShapeDtypeStruct((8, LANE), F32)],
        in_specs=[_ANY] + [_HBM] * (2 * n), out_specs=[_SEM] * (2 * ng) + [_HBM] * (2 * n) + [pl.BlockSpec(memory_space=pltpu.VMEM)],
        input_output_aliases={1 + i: 2 * ng + i for i in range(2 * n)},
        compiler_params=pltpu.CompilerParams(has_side_effects=pltpu.SideEffectType.DATAFLOW_SIDE_EFFECTING),
    )(after, *[_in_hbm(h) for h in flat], *[_in_hbm(lax.empty((N_CHIPS,) + h.shape, h.dtype)) for h in flat])
    sems, srcs, lands = out[:2 * ng], out[2 * ng:2 * ng + n], out[2 * ng + n:2 * ng + 2 * n]
    res, a = [], 0
    for gi, g in enumerate(groups):
        res.append((sems[2 * gi], sems[2 * gi + 1], list(srcs[a:a + len(g)]), list(lands[a:a + len(g)])))
        a += len(g)
    return res


def _gather_wait(send_sems, recv_sems, srcs, lands, after, name):
    n = len(srcs)

    def body(*refs):
        src_refs, land_refs, send, recv = refs[:n], refs[n:2 * n], refs[2 * n], refs[2 * n + 1]
        x, y, c, chips = _place()
        for k in range(n):
            for j, chip in enumerate(chips):
                cp = pltpu.make_async_remote_copy(src_ref=src_refs[k].at[c], dst_ref=land_refs[k].at[_chip_index(chip), c], send_sem=send.at[3 * k + j],
                                                  recv_sem=recv.at[3 * k + j], device_id=(*chip, c), device_id_type=MESH_ID)
                cp.wait_send()
                cp.wait_recv()

    out = pl.pallas_call(
        body, name=name, out_shape=[pltpu.HBM(a.shape, a.dtype) for a in list(srcs) + list(lands)],
        in_specs=[_HBM] * (2 * n) + [_SEM, _SEM, _ANY], out_specs=[_HBM] * (2 * n), input_output_aliases={i: i for i in range(2 * n)},
        compiler_params=pltpu.CompilerParams(has_side_effects=pltpu.SideEffectType.DATAFLOW_SIDE_EFFECTING),
    )(*srcs, *lands, send_sems, recv_sems, after)
    return list(out[n:])


def _forward_halves(lands, name):
    n = len(lands)

    def body(*refs):
        out_refs, (send_sems, recv_sems) = refs[n:2 * n], refs[2 * n:]
        x, y, c, chips = _place()

        def copy(a, j, half, to):
            blk = out_refs[a].at[_chip_index(chips[j]), half]
            return pltpu.make_async_remote_copy(src_ref=blk, dst_ref=blk, send_sem=send_sems.at[3 * a + j], recv_sem=recv_sems.at[3 * a + j],
                                                device_id=to, device_id_type=MESH_ID)

        sends = [copy(a, j, c, (x, y, 1 - c)) for a in range(n) for j in range(3)]
        for cp in sends:
            cp.start()
        for a in range(n):
            for j in range(3):
                copy(a, j, 1 - c, (x, y, c)).wait_recv()
        for cp in sends:
            cp.wait_send()

    return pl.pallas_call(
        body, out_shape=[jax.ShapeDtypeStruct(a.shape, a.dtype) for a in lands], in_specs=[_ANY] * n, out_specs=[_ANY] * n,
        input_output_aliases={i: i for i in range(n)},
        scratch_shapes=[pltpu.SemaphoreType.DMA((3 * n,)), pltpu.SemaphoreType.DMA((3 * n,))], name=name,
    )(*lands)


def _swap_halves(slabs):
    n = len(slabs)

    def body(*refs):
        g_refs, got_refs, (send_sem, recv_sem) = refs[:n], refs[n:2 * n], refs[2 * n:]
        x, y, c, _ = _place()
        copies = [pltpu.make_async_remote_copy(src_ref=g_refs[a].at[t, 1 - c], dst_ref=got_refs[a].at[t], send_sem=send_sem.at[N_CHIPS * a + t],
                                               recv_sem=recv_sem.at[N_CHIPS * a + t], device_id=(x, y, 1 - c), device_id_type=MESH_ID)
                  for a in range(n) for t in range(N_CHIPS)]
        for cp in copies:
            cp.start()
        for cp in copies:
            cp.wait()

    return pl.pallas_call(
        body, out_shape=[jax.ShapeDtypeStruct((g.shape[0],) + g.shape[2:], g.dtype) for g in slabs],
        in_specs=[_ANY] * n, out_specs=[_ANY] * n,
        scratch_shapes=[pltpu.SemaphoreType.DMA((N_CHIPS * n,)), pltpu.SemaphoreType.DMA((N_CHIPS * n,))], name="swap_halves",
    )(*slabs)


def _scatter_start(parts, name):
    n = len(parts)

    def body(*refs):
        srcs, lands, send, recv = refs[:n], refs[n:2 * n], refs[2 * n], refs[2 * n + 1]
        token = refs[-1]
        x, y, c, chips = _place()
        me = 2 * x + y
        for k in range(n):
            for j, chip in enumerate(chips):
                pltpu.make_async_remote_copy(src_ref=srcs[k].at[_chip_index(chip)], dst_ref=lands[k].at[me], send_sem=send.at[3 * k + j],
                                             recv_sem=recv.at[3 * k + j], device_id=(*chip, c), device_id_type=MESH_ID).start()
        token[...] = jnp.zeros_like(token)

    out = pl.pallas_call(
        body, name=name,
        out_shape=[pltpu.SemaphoreType.DMA((3 * n,)), pltpu.SemaphoreType.DMA((3 * n,))] + [pltpu.HBM(p.shape, p.dtype) for p in parts] * 2
        + [jax.ShapeDtypeStruct((8, LANE), F32)],
        in_specs=[_HBM] * (2 * n), out_specs=[_SEM, _SEM] + [_HBM] * (2 * n) + [pl.BlockSpec(memory_space=pltpu.VMEM)],
        input_output_aliases={i: 2 + i for i in range(2 * n)},
        compiler_params=pltpu.CompilerParams(has_side_effects=pltpu.SideEffectType.DATAFLOW_SIDE_EFFECTING),
    )(*[_in_hbm(p) for p in parts], *[_in_hbm(lax.empty(p.shape, p.dtype)) for p in parts])
    return (out[0], out[1], list(out[2:2 + n]), list(out[2 + n:2 + 2 * n])), out[-1]


def _scatter_wait(send_sems, recv_sems, srcs, lands, after, name):
    n = len(srcs)

    def body(*refs):
        src_refs, land_refs, send, recv = refs[:n], refs[n:2 * n], refs[2 * n], refs[2 * n + 1]
        x, y, c, chips = _place()
        for k in range(n):
            for j, chip in enumerate(chips):
                cp = pltpu.make_async_remote_copy(src_ref=src_refs[k].at[_chip_index(chip)], dst_ref=land_refs[k].at[_chip_index(chip)],
                                                  send_sem=send.at[3 * k + j], recv_sem=recv.at[3 * k + j], device_id=(*chip, c), device_id_type=MESH_ID)
                cp.wait_send()
                cp.wait_recv()

    out = pl.pallas_call(
        body, name=name, out_shape=[pltpu.HBM(a.shape, a.dtype) for a in list(srcs) + list(lands)],
        in_specs=[_HBM] * (2 * n) + [_SEM, _SEM, _ANY], out_specs=[_HBM] * (2 * n), input_output_aliases={i: i for i in range(2 * n)},
        compiler_params=pltpu.CompilerParams(has_side_effects=pltpu.SideEffectType.DATAFLOW_SIDE_EFFECTING),
    )(*srcs, *lands, send_sems, recv_sems, after)
    return list(out[n:])


def _sibling_halves(halves):
    n = len(halves)

    def body(*refs):
        h_refs, got_refs, (send_sem, recv_sem) = refs[:n], refs[n:2 * n], refs[2 * n:]
        x, y, c, _ = _place()
        copies = [pltpu.make_async_remote_copy(src_ref=h_refs[a], dst_ref=got_refs[a], send_sem=send_sem.at[a], recv_sem=recv_sem.at[a],
                                               device_id=(x, y, 1 - c), device_id_type=MESH_ID) for a in range(n)]
        for cp in copies:
            cp.start()
        for cp in copies:
            cp.wait()

    return pl.pallas_call(
        body, out_shape=[jax.ShapeDtypeStruct(h.shape, h.dtype) for h in halves], in_specs=[_ANY] * n, out_specs=[_ANY] * n,
        scratch_shapes=[pltpu.SemaphoreType.DMA((n,)), pltpu.SemaphoreType.DMA((n,))], name="sibling_halves",
    )(*halves)


def _gather_replicated(rows):
    m_per, n = rows.shape

    def body(x_ref, out_ref, send_sems, recv_sems, local_sem):
        x, y, c, chips = _place()
        me, sibling = (x, y, c), (x, y, 1 - c)

        def block(px, py, pc):
            return out_ref.at[pl.ds((4 * px + 2 * py + pc) * m_per, m_per), :]

        def copy(k, blk, to, src=None):
            return pltpu.make_async_remote_copy(src_ref=block(*blk) if src is None else src, dst_ref=block(*blk),
                                                send_sem=send_sems.at[k], recv_sem=recv_sems.at[k], device_id=to, device_id_type=MESH_ID)

        mine = pltpu.make_async_copy(x_ref, block(*me), local_sem)
        mine.start()
        first = [copy(0, me, sibling, src=x_ref)]
        first += [copy(1 + j, me, (*chip, c), src=x_ref) for j, chip in enumerate(chips)]
        for cp in first:
            cp.start()
        passed = [copy(4 + j, (*chip, c), sibling) for j, chip in enumerate(chips)]
        for j, chip in enumerate(chips):
            copy(1 + j, (*chip, c), me).wait_recv()
            passed[j].start()
        copy(0, sibling, me).wait_recv()
        for j, chip in enumerate(chips):
            copy(4 + j, (*chip, 1 - c), me).wait_recv()
        for cp in first + passed:
            cp.wait_send()
        mine.wait()

    vmem = pl.BlockSpec(memory_space=pltpu.VMEM)
    return pl.pallas_call(
        body, out_shape=jax.ShapeDtypeStruct((N_DEV * m_per, n), rows.dtype), in_specs=[vmem], out_specs=vmem,
        scratch_shapes=[pltpu.SemaphoreType.DMA((7,)), pltpu.SemaphoreType.DMA((7,)), pltpu.SemaphoreType.DMA],
        name="gather_replicated",
    )(rows)


_SUM_ROWS = 256


def _tile_rows(rows):
    if rows <= _SUM_ROWS:
        return rows
    best = None
    for r in range(16, _SUM_ROWS + 1, 16):
        if rows % r == 0:
            best = r
    assert best is not None, rows
    return best


def _add_pairs(slabs, got, core):
    n, _, rows, cols = slabs.shape
    tr = _tile_rows(rows)

    def body(core_ref, a_ref, b_ref, o_ref):
        o_ref[...] = (a_ref[0].astype(F32) + b_ref[...].astype(F32)).astype(BF16)

    spec = pl.BlockSpec((1, tr, cols), lambda t, i, core_ref: (t, i, 0))
    return pl.pallas_call(
        body, out_shape=jax.ShapeDtypeStruct((n, rows, cols), BF16),
        grid_spec=pltpu.PrefetchScalarGridSpec(
            num_scalar_prefetch=1, grid=(n, rows // tr),
            in_specs=[pl.BlockSpec((1, 1, tr, cols), lambda t, i, core_ref: (t, core_ref[0], i, 0)), spec], out_specs=spec),
        compiler_params=_params(("parallel", "parallel")), name=f"add_pairs_{rows}x{cols}",
    )(core, slabs, got)


def _sum_slabs(slabs):
    n, rows, cols = slabs.shape
    tr = _tile_rows(rows)

    def body(s_ref, o_ref):
        acc = s_ref[0].astype(F32)
        for t in range(1, n):
            acc = acc + s_ref[t].astype(F32)
        o_ref[...] = acc

    return pl.pallas_call(
        body, out_shape=jax.ShapeDtypeStruct((rows, cols), F32), grid=(rows // tr,),
        in_specs=[pl.BlockSpec((n, tr, cols), lambda i: (0, i, 0))], out_specs=pl.BlockSpec((tr, cols), lambda i: (i, 0)),
        compiler_params=_params(("parallel",)), name=f"sum_slabs_{n}x{rows}x{cols}",
    )(slabs)


_ADAM_BLOCK_BYTES = 2 * 1024 * 1024


def _adamw(w, g, m, v):
    shape = w.shape
    cols = shape[-1]
    rows = _size(shape) // cols
    tr = rows
    if rows * cols * 4 > _ADAM_BLOCK_BYTES:
        for r in range(8, rows, 8):
            if rows % r == 0 and r * cols * 4 <= _ADAM_BLOCK_BYTES:
                tr = r

    def body(w_ref, g_ref, m_ref, v_ref, d_ref, nm_ref, nv_ref):
        gv = g_ref[...]
        nm = ADAM_B1 * m_ref[...] + (1.0 - ADAM_B1) * gv
        nv = ADAM_B2 * v_ref[...] + (1.0 - ADAM_B2) * jnp.square(gv)
        m_hat = nm / (1.0 - ADAM_B1 ** ADAM_STEP)
        v_hat = nv / (1.0 - ADAM_B2 ** ADAM_STEP)
        d_ref[...] = -ADAM_LR * (m_hat / (jnp.sqrt(v_hat) + ADAM_EPS) + ADAM_WD * w_ref[...])
        nm_ref[...] = nm
        nv_ref[...] = nv

    spec = pl.BlockSpec((tr, cols), lambda i: (i, 0))
    out = pl.pallas_call(
        body, out_shape=tuple(jax.ShapeDtypeStruct((rows, cols), F32) for _ in range(3)), grid=(rows // tr,),
        in_specs=[spec] * 4, out_specs=(spec,) * 3, compiler_params=_params(("parallel",)), name=f"adamw_{rows}x{cols}",
    )(*(a.reshape(rows, cols) for a in (w, g, m, v)))
    return tuple(o.reshape(shape) for o in out)


def kernel(x, mem, g_mix, w_in_a, b_glu, w_dw_a, b_dw_a, ln_g, ln_b, g_kv, w_kvf, b_f, w_in_b, g_mem, w_mem_kv, w_out, g_ffn, w_up, w_dw_f, b_dw_f, w_down, g_final, loss_target, m_g_mix, m_w_in_a, m_b_glu, m_w_dw_a, m_b_dw_a, m_ln_g, m_ln_b, m_g_kv, m_w_kvf, m_b_f, m_w_in_b, m_g_mem, m_w_mem_kv, m_w_out, m_g_ffn, m_w_up, m_w_dw_f, m_b_dw_f, m_w_down, m_g_final, v_g_mix, v_w_in_a, v_b_glu, v_w_dw_a, v_b_dw_a, v_ln_g, v_ln_b, v_g_kv, v_w_kvf, v_b_f, v_w_in_b, v_g_mem, v_w_mem_kv, v_w_out, v_g_ffn, v_w_up, v_w_dw_f, v_b_dw_f, v_w_down, v_g_final):
    weights = dict(g_mix=g_mix, w_in_a=w_in_a, b_glu=b_glu, w_dw_a=w_dw_a, b_dw_a=b_dw_a, ln_g=ln_g, ln_b=ln_b, g_kv=g_kv,
                   w_kvf=w_kvf, b_f=b_f, w_in_b=w_in_b, g_mem=g_mem, w_mem_kv=w_mem_kv, w_out=w_out, g_ffn=g_ffn, w_up=w_up,
                   w_dw_f=w_dw_f, b_dw_f=b_dw_f, w_down=w_down, g_final=g_final)
    mom1 = dict(g_mix=m_g_mix, w_in_a=m_w_in_a, b_glu=m_b_glu, w_dw_a=m_w_dw_a, b_dw_a=m_b_dw_a, ln_g=m_ln_g, ln_b=m_ln_b,
                g_kv=m_g_kv, w_kvf=m_w_kvf, b_f=m_b_f, w_in_b=m_w_in_b, g_mem=m_g_mem, w_mem_kv=m_w_mem_kv, w_out=m_w_out,
                g_ffn=m_g_ffn, w_up=m_w_up, w_dw_f=m_w_dw_f, b_dw_f=m_b_dw_f, w_down=m_w_down, g_final=m_g_final)
    mom2 = dict(g_mix=v_g_mix, w_in_a=v_w_in_a, b_glu=v_b_glu, w_dw_a=v_w_dw_a, b_dw_a=v_b_dw_a, ln_g=v_ln_g, ln_b=v_ln_b,
                g_kv=v_g_kv, w_kvf=v_w_kvf, b_f=v_b_f, w_in_b=v_w_in_b, g_mem=v_g_mem, w_mem_kv=v_w_mem_kv, w_out=v_w_out,
                g_ffn=v_g_ffn, w_up=v_w_up, w_dw_f=v_w_dw_f, b_dw_f=v_b_dw_f, w_down=v_w_down, g_final=v_g_final)

    x_pos, y_pos, core = lax.axis_index("x"), lax.axis_index("y"), lax.axis_index("c")
    chip = 2 * x_pos + y_pos

    mixer_of = lambda l: LAYER_LEAVES[l][:-FFN_LEAVES]
    ffn_of = lambda l: LAYER_LEAVES[l][-FFN_LEAVES:]
    local_first = _layer_halves(weights, mixer_of(0))
    vector_halves = _flatten_vectors([weights[n] for n in VECTOR_LEAVES], F32)
    first = _gather_leaves(local_first + [vector_halves])
    later = [ffn_of(0)] + [LAYER_LEAVES[l] for l in range(1, N_LAYERS)]
    local_later = [_layer_halves(weights, leaves) for leaves in later]
    in_flight = _gather_start(local_later, first[-1])

    def with_own(gathered, own):
        return [lax.dynamic_update_index_in_dim(g, o[None], chip, 0) for g, o in zip(gathered, own)]

    first = with_own(first, local_first + [vector_halves])
    vectors = _unflatten_vectors(first[-1], (N_CHIPS,))
    table = _weight_table({n: weights[n] for n in REPLICATED})

    def fetch(k, after):
        send_sems, recv_sems, srcs, lands = in_flight[k]
        lands = _gather_wait(send_sems, recv_sems, srcs, lands, after, f"gather_wait_{k}")
        return with_own(_forward_halves(lands, f"forward_halves_{k}"), local_later[k])

    def arrive(l, after, part):
        if l == 0:
            gathered = first[:-1] + [None] * FFN_LEAVES if part == "mixer" else [None] * len(mixer_of(0)) + fetch(0, after)
        elif part == "mixer":
            gathered = fetch(l, after)
        else:
            return
        _install_layer(table, l, _layer_pieces(l, gathered, vectors))

    core_index = core.astype(jnp.int32).reshape(1)
    in_flight_back = []

    def emit(l, g, part):
        if l == 0:
            leaves = ffn_of(0) if part == "ffn" else mixer_of(0)
        elif part == "mixer":
            leaves = LAYER_LEAVES[l]
        else:
            return None
        with_vectors = l == 0 and part == "mixer"
        slabs = _layer_slabs(g, leaves) + ([_vector_slabs(g)] if with_vectors else [])
        chip_sums = [_add_pairs(mine, got, core_index) for mine, got in zip(slabs, _swap_halves(slabs))]
        handles, started = _scatter_start(chip_sums, f"scatter_start_{len(in_flight_back)}")
        in_flight_back.append((leaves, with_vectors, chip_sums, handles))
        return started

    loss, dx, grads = _local_step(x[0], mem[0], loss_target[0], table, arrive, emit)
    loss = lax.psum(loss, ("x", "y", "c"))

    names, mine = [], []
    for k, (leaves, with_vectors, chip_sums, (send_sems, recv_sems, srcs, lands)) in enumerate(in_flight_back):
        lands = _scatter_wait(send_sems, recv_sems, srcs, lands, dx, f"scatter_wait_{k}")
        from_chips = [lax.dynamic_update_index_in_dim(got, lax.dynamic_index_in_dim(own, chip, 0, keepdims=True), chip, 0)
                      for got, own in zip(lands, chip_sums)]
        mine += [_sum_slabs(f) for f in from_chips]
        names += list(leaves) + ([("vectors", None)] if with_vectors else [])
    theirs = _sibling_halves(mine)
    per_leaf = {}
    for (name, idx), m, t in zip(names, mine, theirs):
        per_leaf.setdefault(name, {})[idx] = jnp.concatenate([jnp.where(core == 0, m, t), jnp.where(core == 0, t, m)])
    vector_grads = per_leaf.pop("vectors")[None]
    grad_leaves = {n: (d[None] if None in d else jnp.stack([d[i] for i in sorted(d)])) for n, d in per_leaf.items()}
    grad_leaves.update(_unflatten_vectors(vector_grads))

    rep_sum = _sum_slabs(_gather_replicated(_replicated_rows(grads)).reshape(N_DEV, REP_ROWS, REP_COLS)).reshape(-1)
    off = 0
    for n, shp in REPLICATED.items():
        grad_leaves[n] = rep_sum[off:off + _size(shp)].reshape(shp)
        off += _size(shp)

    deltas, new_m, new_v = {}, {}, {}
    for n in WEIGHT_ORDER:
        deltas[n], new_m[n], new_v[n] = _adamw(weights[n], grad_leaves[n], mom1[n], mom2[n])
    return (loss, dx[None], *[grad_leaves[n] for n in WEIGHT_ORDER], *[deltas[n] for n in WEIGHT_ORDER],
            *[new_m[n] for n in WEIGHT_ORDER], *[new_v[n] for n in WEIGHT_ORDER])
```

```python
import jax
import jax.numpy as jnp
from jax import lax
from jax.experimental import pallas as pl
from jax.experimental.pallas import tpu as pltpu

F32 = jnp.float32
BF16 = jnp.bfloat16

D_MODEL = 1024
N_LAYERS = 4
N_A = 2
CONV_CH = 768
MEM_W = 256
HEAD_DIM = 64
N_MEM_HEADS = 4
N_FOX_HEADS = 12
N_HEAD_PAIRS = N_FOX_HEADS // 2
D_FF = 2816
CONV_W = 31
CONV_PAD = 32
FFN_CONV_W = 3
FFN_PAD = 8
F_PAD = 128
RMS_EPS = 1e-6
LN_EPS = 1e-5
ATT_SCALE = HEAD_DIM ** -0.5
NEG_BIG = -1e30

ADAM_LR = 0.001
ADAM_B1 = 0.9
ADAM_B2 = 0.999
ADAM_EPS = 1e-08
ADAM_WD = 0.01
ADAM_STEP = 10

LANE = 128
ROW_TILE = 256
CHUNK = 128
VMEM_LIMIT = 48 * 1024 * 1024
FLAT_COLS = 1024
N_CHIPS = 4
N_DEV = 8
MESH_ID = pl.DeviceIdType.MESH


def _params(sem=None):
    return pltpu.CompilerParams(dimension_semantics=sem, vmem_limit_bytes=VMEM_LIMIT)


def _tile(dim, pref):
    if dim <= pref:
        return dim
    best = None
    for m in range(1, dim // LANE + 1):
        d = m * LANE
        if dim % d == 0 and d <= pref:
            best = d
    assert best is not None, dim
    return best


_DIMS = {"nn": (((1,), (0,)), ((), ())), "nt": (((1,), (1,)), ((), ())), "tn": (((0,), (0,)), ((), ()))}


_MM_RESIDENT_BYTES = 8 * 1024 * 1024
_MM_STREAM_BYTES = 6 * 1024 * 1024
_MM_OUT_BYTES = 6 * 1024 * 1024


def _mm_tiles(m, n, k, mode, a_size, b_size, o_size):
    if mode == "tn":
        tm = _tile(m, _MM_RESIDENT_BYTES // (k * a_size))
        tn = _tile(n, min(_MM_STREAM_BYTES // (k * b_size), _MM_OUT_BYTES // (tm * o_size)))
    else:
        tn = _tile(n, _MM_RESIDENT_BYTES // (k * b_size))
        tm = _tile(m, min(_MM_STREAM_BYTES // (k * a_size), _MM_OUT_BYTES // (tn * o_size), 512))
    return tm, tn


def _mm(a, b, mode="nn", out_dtype=F32, add=None):
    if mode == "nn":
        (m, k), (k2, n) = a.shape, b.shape
    elif mode == "nt":
        (m, k), (n, k2) = a.shape, b.shape
    else:
        (k, m), (k2, n) = a.shape, b.shape
    assert k == k2, (a.shape, b.shape, mode)
    tm, tn = _mm_tiles(m, n, k, mode, a.dtype.itemsize, b.dtype.itemsize, jnp.dtype(out_dtype).itemsize)
    dims = _DIMS[mode]
    has_add = add is not None

    def body(*refs):
        if has_add:
            a_ref, b_ref, add_ref, o_ref = refs
        else:
            a_ref, b_ref, o_ref = refs
        r = lax.dot_general(a_ref[...].astype(BF16), b_ref[...].astype(BF16), dims, preferred_element_type=F32)
        if has_add:
            r = r + add_ref[...]
        o_ref[...] = r.astype(out_dtype)

    a_spec = pl.BlockSpec((k, tm), lambda i, j: (0, i)) if mode == "tn" else pl.BlockSpec((tm, k), lambda i, j: (i, 0))
    b_spec = pl.BlockSpec((tn, k), lambda i, j: (j, 0)) if mode == "nt" else pl.BlockSpec((k, tn), lambda i, j: (0, j))
    o_spec = pl.BlockSpec((tm, tn), lambda i, j: (i, j))
    in_specs = [a_spec, b_spec] + ([o_spec] if has_add else [])
    args = (a, b) + ((add,) if has_add else ())
    return pl.pallas_call(
        body,
        out_shape=jax.ShapeDtypeStruct((m, n), out_dtype),
        grid=(m // tm, n // tn),
        in_specs=in_specs,
        out_specs=o_spec,
        compiler_params=_params(("parallel", "parallel")),
        name=f"mm_{mode}_{m}x{k}x{n}",
    )(*args)


def _row(width):
    return pl.BlockSpec((ROW_TILE, width), lambda i: (i, 0))


def _vec(width):
    return pl.BlockSpec((1, width), lambda i: (0, 0))


def _rms_fwd(x, g):
    rows, d = x.shape

    def body(x_ref, g_ref, o_ref):
        xv = x_ref[...]
        rstd = lax.rsqrt(jnp.mean(xv * xv, axis=-1, keepdims=True) + RMS_EPS)
        o_ref[...] = (xv * rstd * g_ref[...]).astype(BF16)

    return pl.pallas_call(
        body, out_shape=jax.ShapeDtypeStruct((rows, d), BF16), grid=(rows // ROW_TILE,),
        in_specs=[_row(d), _vec(d)], out_specs=_row(d), compiler_params=_params(("parallel",)), name=f"rms_fwd_{rows}",
    )(x, g)


def _accumulate(ref, value, step):
    @pl.when(step == 0)
    def _():
        ref[...] = value

    @pl.when(step > 0)
    def _():
        ref[...] += value


def _rms_bwd(x, g, dh, dres):
    rows, d = x.shape
    has_res = dres is not None

    def body(*refs):
        if has_res:
            x_ref, g_ref, dh_ref, dres_ref, dx_ref, dxb_ref, dg_ref = refs
        else:
            x_ref, g_ref, dh_ref, dx_ref, dxb_ref, dg_ref = refs
        xv = x_ref[...]
        dhv = dh_ref[...]
        rstd = lax.rsqrt(jnp.mean(xv * xv, axis=-1, keepdims=True) + RMS_EPS)
        xhat = xv * rstd
        gd = dhv * g_ref[...]
        dx = rstd * (gd - xhat * jnp.mean(gd * xhat, axis=-1, keepdims=True))
        if has_res:
            dx = dx + dres_ref[...]
        dx_ref[...] = dx
        dxb_ref[...] = dx.astype(BF16)
        _accumulate(dg_ref, jnp.sum(dhv * xhat, axis=0, keepdims=True), pl.program_id(0))

    in_specs = [_row(d), _vec(d), _row(d)] + ([_row(d)] if has_res else [])
    args = (x, g, dh) + ((dres,) if has_res else ())
    return pl.pallas_call(
        body,
        out_shape=(jax.ShapeDtypeStruct((rows, d), F32), jax.ShapeDtypeStruct((rows, d), BF16), jax.ShapeDtypeStruct((1, d), F32)),
        grid=(rows // ROW_TILE,), in_specs=in_specs, out_specs=(_row(d), _row(d), _vec(d)),
        compiler_params=_params(("arbitrary",)), name=f"rms_bwd_{rows}_{int(has_res)}",
    )(*args)


def _loss_head(x, g, target):
    rows, d = x.shape

    def body(x_ref, g_ref, t_ref, loss_ref, dx_ref, dxb_ref, dg_ref):
        xv = x_ref[...]
        gv = g_ref[...]
        rstd = lax.rsqrt(jnp.mean(xv * xv, axis=-1, keepdims=True) + RMS_EPS)
        xhat = xv * rstd
        err = xhat * gv - t_ref[...]
        part = 0.5 * jnp.sum(jnp.mean(err * err, axis=-1, keepdims=True), axis=0, keepdims=True)
        dy = err * (1.0 / d)
        gd = dy * gv
        dx = rstd * (gd - xhat * jnp.mean(gd * xhat, axis=-1, keepdims=True))
        dx_ref[...] = dx
        dxb_ref[...] = dx.astype(BF16)
        step = pl.program_id(0)
        _accumulate(loss_ref, jnp.broadcast_to(part, (1, LANE)), step)
        _accumulate(dg_ref, jnp.sum(dy * xhat, axis=0, keepdims=True), step)

    return pl.pallas_call(
        body,
        out_shape=(jax.ShapeDtypeStruct((1, LANE), F32), jax.ShapeDtypeStruct((rows, d), F32),
                   jax.ShapeDtypeStruct((rows, d), BF16), jax.ShapeDtypeStruct((1, d), F32)),
        grid=(rows // ROW_TILE,), in_specs=[_row(d), _vec(d), _row(d)],
        out_specs=(_vec(LANE), _row(d), _row(d), _vec(d)),
        compiler_params=_params(("arbitrary",)), name="loss_head",
    )(x, g, target)


def _sigmoid(x):
    return 1.0 / (1.0 + jnp.exp(-x))


def _ln_silu_fwd(c, ln_g, ln_b):
    rows, ch = c.shape

    def body(c_ref, g_ref, b_ref, o_ref):
        cv = c_ref[...]
        mu = jnp.mean(cv, axis=-1, keepdims=True)
        cen = cv - mu
        rstd = lax.rsqrt(jnp.mean(cen * cen, axis=-1, keepdims=True) + LN_EPS)
        y = cen * rstd * g_ref[...] + b_ref[...]
        o_ref[...] = (y * _sigmoid(y)).astype(BF16)

    return pl.pallas_call(
        body, out_shape=jax.ShapeDtypeStruct((rows, ch), BF16), grid=(rows // ROW_TILE,),
        in_specs=[_row(ch), _vec(ch), _vec(ch)], out_specs=_row(ch), compiler_params=_params(("parallel",)), name="ln_silu_fwd",
    )(c, ln_g, ln_b)


def _ln_silu_bwd(dmix, c, ln_g, ln_b):
    rows, ch = c.shape

    def body(dm_ref, c_ref, g_ref, b_ref, dc_ref, dg_ref, db_ref):
        cv = c_ref[...]
        gv = g_ref[...]
        mu = jnp.mean(cv, axis=-1, keepdims=True)
        cen = cv - mu
        rstd = lax.rsqrt(jnp.mean(cen * cen, axis=-1, keepdims=True) + LN_EPS)
        xhat = cen * rstd
        y = xhat * gv + b_ref[...]
        sg = _sigmoid(y)
        dy = dm_ref[...] * (sg * (1.0 + y * (1.0 - sg)))
        dxh = dy * gv
        dc = rstd * (dxh - jnp.mean(dxh, axis=-1, keepdims=True) - xhat * jnp.mean(dxh * xhat, axis=-1, keepdims=True))
        dc_ref[...] = dc
        step = pl.program_id(0)
        _accumulate(dg_ref, jnp.sum(dy * xhat, axis=0, keepdims=True), step)
        _accumulate(db_ref, jnp.sum(dy, axis=0, keepdims=True), step)

    return pl.pallas_call(
        body,
        out_shape=(jax.ShapeDtypeStruct((rows, ch), F32), jax.ShapeDtypeStruct((1, ch), F32), jax.ShapeDtypeStruct((1, ch), F32)),
        grid=(rows // ROW_TILE,), in_specs=[_row(ch), _row(ch), _vec(ch), _vec(ch)],
        out_specs=(_row(ch), _vec(ch), _vec(ch)), compiler_params=_params(("arbitrary",)), name="ln_silu_bwd",
    )(dmix, c, ln_g, ln_b)


def _col(rows, cb):
    return pl.BlockSpec((rows, cb), lambda j: (0, j))


SUBLANES = 8
_WINDOW = CHUNK + CONV_PAD


def _realign(win, shifted_ref):
    shifted_ref[0] = win
    for r in range(1, SUBLANES):
        shifted_ref[r, 0:_WINDOW - SUBLANES, :] = win[r:r + _WINDOW - SUBLANES, :]


def _tap(shifted_ref, off):
    r = off % SUBLANES
    return shifted_ref[r, off - r:off - r + CHUNK, :]


def _conv_glu_fwd(pa, pg, ba, bg, w_dw, b_dw):
    seq, ch = pa.shape
    cb = LANE
    n_chunks = seq // CHUNK

    def body(pa_ref, pg_ref, ba_ref, bg_ref, w_ref, b_ref, v_ref, c_ref, vpad_ref, vshift_ref):
        vpad_ref[0:CONV_PAD, :] = jnp.zeros((CONV_PAD, cb), F32)

        def glu(r, carry):
            r0 = pl.multiple_of(r * CHUNK, CHUNK)
            rows = pl.ds(r0, CHUNK)
            v = (pa_ref[rows, :] + ba_ref[...]) * _sigmoid(pg_ref[rows, :] + bg_ref[...])
            v_ref[rows, :] = v
            vpad_ref[pl.ds(r0 + CONV_PAD, CHUNK), :] = v
            return carry

        lax.fori_loop(0, n_chunks, glu, 0)

        def conv(r, carry):
            r0 = pl.multiple_of(r * CHUNK, CHUNK)
            _realign(vpad_ref[pl.ds(r0, _WINDOW), :], vshift_ref)
            acc = jnp.broadcast_to(b_ref[...], (CHUNK, cb))
            for t in range(CONV_W):
                acc = acc + w_ref[t:t + 1, :] * _tap(vshift_ref, CONV_PAD - (CONV_W - 1) + t)
            c_ref[pl.ds(r0, CHUNK), :] = acc
            return carry

        lax.fori_loop(0, n_chunks, conv, 0)

    return pl.pallas_call(
        body,
        out_shape=(jax.ShapeDtypeStruct((seq, ch), F32), jax.ShapeDtypeStruct((seq, ch), F32)),
        grid=(ch // cb,),
        in_specs=[_col(seq, cb), _col(seq, cb), _col(1, cb), _col(1, cb), _col(CONV_PAD, cb), _col(1, cb)],
        out_specs=(_col(seq, cb), _col(seq, cb)),
        scratch_shapes=[pltpu.VMEM((seq + CONV_PAD, cb), F32), pltpu.VMEM((SUBLANES, _WINDOW, cb), F32)],
        compiler_params=_params(("parallel",)), name="conv_glu_fwd",
    )(pa, pg, ba, bg, w_dw, b_dw)


def _conv_glu_bwd(dc, v, pa, pg, ba, bg, w_dw):
    seq, ch = dc.shape
    cb = LANE
    n_chunks = seq // CHUNK

    def body(dc_ref, v_ref, pa_ref, pg_ref, ba_ref, bg_ref, w_ref, da_ref, dg_ref, dw_ref, dbdw_ref, dba_ref, dbg_ref,
             dcpad_ref, vpad_ref, dcshift_ref, vshift_ref):
        vpad_ref[0:CONV_PAD, :] = jnp.zeros((CONV_PAD, cb), F32)
        dcpad_ref[seq:seq + CONV_PAD, :] = jnp.zeros((CONV_PAD, cb), F32)
        dw_ref[...] = jnp.zeros((CONV_PAD, cb), F32)
        dbdw_ref[...] = jnp.zeros((1, cb), F32)
        dba_ref[...] = jnp.zeros((1, cb), F32)
        dbg_ref[...] = jnp.zeros((1, cb), F32)

        def fill(r, carry):
            r0 = pl.multiple_of(r * CHUNK, CHUNK)
            vpad_ref[pl.ds(r0 + CONV_PAD, CHUNK), :] = v_ref[pl.ds(r0, CHUNK), :]
            dcpad_ref[pl.ds(r0, CHUNK), :] = dc_ref[pl.ds(r0, CHUNK), :]
            return carry

        lax.fori_loop(0, n_chunks, fill, 0)

        def step(r, carry):
            r0 = pl.multiple_of(r * CHUNK, CHUNK)
            rows = pl.ds(r0, CHUNK)
            _realign(dcpad_ref[pl.ds(r0, _WINDOW), :], dcshift_ref)
            _realign(vpad_ref[pl.ds(r0, _WINDOW), :], vshift_ref)
            dcur = _tap(dcshift_ref, 0)
            dv = jnp.zeros((CHUNK, cb), F32)
            for t in range(CONV_W):
                dv = dv + w_ref[t:t + 1, :] * _tap(dcshift_ref, CONV_W - 1 - t)
                dw_ref[t:t + 1, :] += jnp.sum(dcur * _tap(vshift_ref, CONV_PAD - (CONV_W - 1) + t), axis=0, keepdims=True)
            a = pa_ref[rows, :] + ba_ref[...]
            sg = _sigmoid(pg_ref[rows, :] + bg_ref[...])
            da = dv * sg
            dgate = dv * a * sg * (1.0 - sg)
            da_ref[rows, :] = da.astype(BF16)
            dg_ref[rows, :] = dgate.astype(BF16)
            dbdw_ref[...] += jnp.sum(dcur, axis=0, keepdims=True)
            dba_ref[...] += jnp.sum(da, axis=0, keepdims=True)
            dbg_ref[...] += jnp.sum(dgate, axis=0, keepdims=True)
            return carry

        lax.fori_loop(0, n_chunks, step, 0)

    return pl.pallas_call(
        body,
        out_shape=(jax.ShapeDtypeStruct((seq, ch), BF16), jax.ShapeDtypeStruct((seq, ch), BF16),
                   jax.ShapeDtypeStruct((CONV_PAD, ch), F32), jax.ShapeDtypeStruct((1, ch), F32),
                   jax.ShapeDtypeStruct((1, ch), F32), jax.ShapeDtypeStruct((1, ch), F32)),
        grid=(ch // cb,),
        in_specs=[_col(seq, cb), _col(seq, cb), _col(seq, cb), _col(seq, cb), _col(1, cb), _col(1, cb), _col(CONV_PAD, cb)],
        out_specs=(_col(seq, cb), _col(seq, cb), _col(CONV_PAD, cb), _col(1, cb), _col(1, cb), _col(1, cb)),
        scratch_shapes=[pltpu.VMEM((seq + CONV_PAD, cb), F32), pltpu.VMEM((seq + CONV_PAD, cb), F32),
                        pltpu.VMEM((SUBLANES, _WINDOW, cb), F32), pltpu.VMEM((SUBLANES, _WINDOW, cb), F32)],
        compiler_params=_params(("parallel",)), name="conv_glu_bwd",
    )(dc, v, pa, pg, ba, bg, w_dw)


def _ffn_conv(pad_ref, r0, w_ref, b_ref, cb):
    win = pad_ref[pl.ds(r0, CHUNK + FFN_PAD), :]
    y = jnp.broadcast_to(b_ref[...], (CHUNK, cb))
    for t in range(FFN_CONV_W):
        off = FFN_PAD - (FFN_CONV_W - 1) + t
        y = y + w_ref[t:t + 1, :] * win[off:off + CHUNK, :]
    return y, win


def _ffn_mid_fwd(ug, uv, wg, wv, bg, bv):
    seq, ch = ug.shape
    cb = _tile(ch, 256)
    n_chunks = seq // CHUNK

    def body(ug_ref, uv_ref, wg_ref, wv_ref, bg_ref, bv_ref, z_ref, gpad_ref, vpad_ref):
        gpad_ref[0:FFN_PAD, :] = jnp.zeros((FFN_PAD, cb), F32)
        vpad_ref[0:FFN_PAD, :] = jnp.zeros((FFN_PAD, cb), F32)

        def fill(r, carry):
            r0 = pl.multiple_of(r * CHUNK, CHUNK)
            gpad_ref[pl.ds(r0 + FFN_PAD, CHUNK), :] = ug_ref[pl.ds(r0, CHUNK), :]
            vpad_ref[pl.ds(r0 + FFN_PAD, CHUNK), :] = uv_ref[pl.ds(r0, CHUNK), :]
            return carry

        lax.fori_loop(0, n_chunks, fill, 0)

        def step(r, carry):
            r0 = pl.multiple_of(r * CHUNK, CHUNK)
            yg, _ = _ffn_conv(gpad_ref, r0, wg_ref, bg_ref, cb)
            yv, _ = _ffn_conv(vpad_ref, r0, wv_ref, bv_ref, cb)
            z_ref[pl.ds(r0, CHUNK), :] = (yg * _sigmoid(yg) * yv).astype(BF16)
            return carry

        lax.fori_loop(0, n_chunks, step, 0)

    return pl.pallas_call(
        body, out_shape=jax.ShapeDtypeStruct((seq, ch), BF16), grid=(ch // cb,),
        in_specs=[_col(seq, cb), _col(seq, cb), _col(FFN_PAD, cb), _col(FFN_PAD, cb), _col(1, cb), _col(1, cb)],
        out_specs=_col(seq, cb),
        scratch_shapes=[pltpu.VMEM((seq + FFN_PAD, cb), F32), pltpu.VMEM((seq + FFN_PAD, cb), F32)],
        compiler_params=_params(("parallel",)), name="ffn_mid_fwd",
    )(ug, uv, wg, wv, bg, bv)


def _ffn_mid_bwd(ug, uv, dz, wg, wv, bg, bv):
    seq, ch = ug.shape
    cb = _tile(ch, 256)
    n_chunks = seq // CHUNK

    def body(ug_ref, uv_ref, dz_ref, wg_ref, wv_ref, bg_ref, bv_ref, dug_ref, duv_ref, dwg_ref, dwv_ref, dbg_ref, dbv_ref,
             gpad_ref, vpad_ref, dyg_ref, dyv_ref):
        gpad_ref[0:FFN_PAD, :] = jnp.zeros((FFN_PAD, cb), F32)
        vpad_ref[0:FFN_PAD, :] = jnp.zeros((FFN_PAD, cb), F32)
        dyg_ref[seq:seq + FFN_PAD, :] = jnp.zeros((FFN_PAD, cb), F32)
        dyv_ref[seq:seq + FFN_PAD, :] = jnp.zeros((FFN_PAD, cb), F32)
        dwg_ref[...] = jnp.zeros((FFN_PAD, cb), F32)
        dwv_ref[...] = jnp.zeros((FFN_PAD, cb), F32)
        dbg_ref[...] = jnp.zeros((1, cb), F32)
        dbv_ref[...] = jnp.zeros((1, cb), F32)

        def fill(r, carry):
            r0 = pl.multiple_of(r * CHUNK, CHUNK)
            gpad_ref[pl.ds(r0 + FFN_PAD, CHUNK), :] = ug_ref[pl.ds(r0, CHUNK), :]
            vpad_ref[pl.ds(r0 + FFN_PAD, CHUNK), :] = uv_ref[pl.ds(r0, CHUNK), :]
            return carry

        lax.fori_loop(0, n_chunks, fill, 0)

        def grads_of_conv_out(r, carry):
            r0 = pl.multiple_of(r * CHUNK, CHUNK)
            rows = pl.ds(r0, CHUNK)
            yg, gwin = _ffn_conv(gpad_ref, r0, wg_ref, bg_ref, cb)
            yv, vwin = _ffn_conv(vpad_ref, r0, wv_ref, bv_ref, cb)
            dzv = dz_ref[rows, :]
            sg = _sigmoid(yg)
            dyg = dzv * yv * (sg * (1.0 + yg * (1.0 - sg)))
            dyv = dzv * yg * sg
            dyg_ref[rows, :] = dyg
            dyv_ref[rows, :] = dyv
            for t in range(FFN_CONV_W):
                off = FFN_PAD - (FFN_CONV_W - 1) + t
                dwg_ref[t:t + 1, :] += jnp.sum(dyg * gwin[off:off + CHUNK, :], axis=0, keepdims=True)
                dwv_ref[t:t + 1, :] += jnp.sum(dyv * vwin[off:off + CHUNK, :], axis=0, keepdims=True)
            dbg_ref[...] += jnp.sum(dyg, axis=0, keepdims=True)
            dbv_ref[...] += jnp.sum(dyv, axis=0, keepdims=True)
            return carry

        lax.fori_loop(0, n_chunks, grads_of_conv_out, 0)

        def grads_of_conv_in(r, carry):
            r0 = pl.multiple_of(r * CHUNK, CHUNK)
            gwin = dyg_ref[pl.ds(r0, CHUNK + FFN_PAD), :]
            vwin = dyv_ref[pl.ds(r0, CHUNK + FFN_PAD), :]
            dug = jnp.zeros((CHUNK, cb), F32)
            duv = jnp.zeros((CHUNK, cb), F32)
            for t in range(FFN_CONV_W):
                off = FFN_CONV_W - 1 - t
                dug = dug + wg_ref[t:t + 1, :] * gwin[off:off + CHUNK, :]
                duv = duv + wv_ref[t:t + 1, :] * vwin[off:off + CHUNK, :]
            dug_ref[pl.ds(r0, CHUNK), :] = dug.astype(BF16)
            duv_ref[pl.ds(r0, CHUNK), :] = duv.astype(BF16)
            return carry

        lax.fori_loop(0, n_chunks, grads_of_conv_in, 0)

    return pl.pallas_call(
        body,
        out_shape=(jax.ShapeDtypeStruct((seq, ch), BF16), jax.ShapeDtypeStruct((seq, ch), BF16),
                   jax.ShapeDtypeStruct((FFN_PAD, ch), F32), jax.ShapeDtypeStruct((FFN_PAD, ch), F32),
                   jax.ShapeDtypeStruct((1, ch), F32), jax.ShapeDtypeStruct((1, ch), F32)),
        grid=(ch // cb,),
        in_specs=[_col(seq, cb), _col(seq, cb), _col(seq, cb), _col(FFN_PAD, cb), _col(FFN_PAD, cb), _col(1, cb), _col(1, cb)],
        out_specs=(_col(seq, cb), _col(seq, cb), _col(FFN_PAD, cb), _col(FFN_PAD, cb), _col(1, cb), _col(1, cb)),
        scratch_shapes=[pltpu.VMEM((seq + FFN_PAD, cb), F32) for _ in range(4)],
        compiler_params=_params(("parallel",)), name="ffn_mid_bwd",
    )(ug, uv, dz, wg, wv, bg, bv)


def _dot(a, b, mode):
    return lax.dot_general(a.astype(BF16), b.astype(BF16), _DIMS[mode], preferred_element_type=F32)


def _head(h):
    return slice(h * HEAD_DIM, (h + 1) * HEAD_DIM)


def _mem_softmax(q, k):
    s = _dot(q, k, "nt") * ATT_SCALE
    e = jnp.exp(s - jnp.max(s, axis=-1, keepdims=True))
    return e / jnp.sum(e, axis=-1, keepdims=True)


MEM_ROW_TILE = 512


def _mem_row(width):
    return pl.BlockSpec((MEM_ROW_TILE, width), lambda i: (i, 0))


def _memattn_fwd(pq, mk, mv):
    seq, w = pq.shape
    m = mk.shape[0]

    def body(q_ref, k_ref, v_ref, o_ref):
        for h in range(N_MEM_HEADS):
            p = _mem_softmax(q_ref[:, _head(h)], k_ref[:, _head(h)])
            o_ref[:, _head(h)] = _dot(p, v_ref[:, _head(h)], "nn").astype(BF16)

    full = pl.BlockSpec((m, w), lambda i: (0, 0))
    return pl.pallas_call(
        body, out_shape=jax.ShapeDtypeStruct((seq, w), BF16), grid=(seq // MEM_ROW_TILE,),
        in_specs=[_mem_row(w), full, full], out_specs=_mem_row(w), compiler_params=_params(("parallel",)), name="memattn_fwd",
    )(pq, mk, mv)


def _memattn_bwd(pq, mk, mv, dmo):
    seq, w = pq.shape
    m = mk.shape[0]

    def body(q_ref, k_ref, v_ref, do_ref, dq_ref, dk_ref, dv_ref):
        step = pl.program_id(0)

        @pl.when(step == 0)
        def _():
            dk_ref[...] = jnp.zeros((m, w), F32)
            dv_ref[...] = jnp.zeros((m, w), F32)

        for h in range(N_MEM_HEADS):
            q, k, v, do = q_ref[:, _head(h)], k_ref[:, _head(h)], v_ref[:, _head(h)], do_ref[:, _head(h)]
            p = _mem_softmax(q, k)
            dp = _dot(do, v, "nt")
            ds = p * (dp - jnp.sum(dp * p, axis=-1, keepdims=True))
            dq_ref[:, _head(h)] = (_dot(ds, k, "nn") * ATT_SCALE).astype(BF16)
            dk_ref[:, _head(h)] += _dot(ds, q, "tn") * ATT_SCALE
            dv_ref[:, _head(h)] += _dot(p, do, "tn")

    full = pl.BlockSpec((m, w), lambda i: (0, 0))
    return pl.pallas_call(
        body,
        out_shape=(jax.ShapeDtypeStruct((seq, w), BF16), jax.ShapeDtypeStruct((m, w), F32), jax.ShapeDtypeStruct((m, w), F32)),
        grid=(seq // MEM_ROW_TILE,), in_specs=[_mem_row(w), full, full, _mem_row(w)], out_specs=(_mem_row(w), full, full),
        compiler_params=_params(("arbitrary",)), name="memattn_bwd",
    )(pq, mk, mv, dmo)


FWD_KEY_TILE = 512
BWD_KEY_TILE = 512


def _causal_t(s_t, q_blk, k_blk, key_tile):
    kpos = k_blk * key_tile + lax.broadcasted_iota(jnp.int32, (key_tile, ROW_TILE), 0)
    qpos = q_blk * ROW_TILE + lax.broadcasted_iota(jnp.int32, (key_tile, ROW_TILE), 1)
    return jnp.where(kpos <= qpos, s_t, NEG_BIG)


def _fox_fwd(q, k, v, cum_cols, cum_rows):
    seq, w = q.shape
    nq = seq // ROW_TILE

    def body(q_ref, k_ref, v_ref, cc_ref, cr_ref, o_ref, ob_ref, lse_ref):
        i = pl.program_id(1)
        qs = [q_ref[:, _head(h)].astype(BF16) for h in range(2)]
        crs = [cr_ref[0, 0, h:h + 1, :] for h in range(2)]

        def step(j, carry):
            rows = pl.ds(pl.multiple_of(j * FWD_KEY_TILE, FWD_KEY_TILE), FWD_KEY_TILE)
            new = []
            for h in range(2):
                m_run, l_run, acc = carry[h]
                s_t = _dot(k_ref[rows, _head(h)], qs[h], "nt") * ATT_SCALE + crs[h] - cc_ref[0, rows, h * HEAD_DIM:h * HEAD_DIM + 1]
                s_t = _causal_t(s_t, i, j, FWD_KEY_TILE)
                m_new = jnp.maximum(m_run, jnp.max(s_t, axis=0, keepdims=True))
                alpha = jnp.exp(m_run - m_new)
                p_t = jnp.exp(s_t - m_new)
                l_new = alpha * l_run + jnp.sum(p_t, axis=0, keepdims=True)
                new.append((m_new, l_new, alpha * acc + _dot(v_ref[rows, _head(h)], p_t, "tn")))
            return tuple(new)

        one_head = (jnp.full((1, ROW_TILE), NEG_BIG, F32), jnp.zeros((1, ROW_TILE), F32), jnp.zeros((HEAD_DIM, ROW_TILE), F32))
        res = lax.fori_loop(0, ((i + 1) * ROW_TILE + FWD_KEY_TILE - 1) // FWD_KEY_TILE, step, (one_head, one_head))
        o = jnp.concatenate([acc / l_run for _, l_run, acc in res], axis=0).T
        o_ref[...] = o
        ob_ref[...] = o.astype(BF16)
        lse_ref[...] = jnp.zeros((1, 1, 8, ROW_TILE), F32)
        for h in range(2):
            lse_ref[0, 0, h:h + 1, :] = res[h][0] + jnp.log(res[h][1])

    blk = pl.BlockSpec((ROW_TILE, LANE), lambda hp, i: (i, hp))
    full = pl.BlockSpec((seq, LANE), lambda hp, i: (0, hp))
    cols = pl.BlockSpec((1, seq, LANE), lambda hp, i: (hp, 0, 0))
    rows = pl.BlockSpec((1, 1, 8, ROW_TILE), lambda hp, i: (hp, i, 0, 0))
    return pl.pallas_call(
        body,
        out_shape=(jax.ShapeDtypeStruct((seq, w), F32), jax.ShapeDtypeStruct((seq, w), BF16),
                   jax.ShapeDtypeStruct((N_HEAD_PAIRS, nq, 8, ROW_TILE), F32)),
        grid=(N_HEAD_PAIRS, nq), in_specs=[blk, full, full, cols, rows], out_specs=(blk, blk, rows),
        compiler_params=_params(("parallel", "parallel")), name="fox_fwd",
    )(q, k, v, cum_cols, cum_rows)


def _fox_bwd(q, k, v, cum_cols, cum_rows, o, lse, do):
    seq, w = q.shape
    nq = seq // ROW_TILE
    nk = seq // BWD_KEY_TILE

    def body(q_ref, k_ref, v_ref, cc_ref, cr_ref, o_ref, lse_ref, do_ref, dq_ref, dk_ref, dv_ref, dcc_ref, dcr_ref):
        kj = pl.program_id(1)

        @pl.when(kj == 0)
        def _():
            dq_ref[...] = jnp.zeros((seq, LANE), F32)
            dcr_ref[...] = jnp.zeros((1, nq, 8, ROW_TILE), F32)

        ks = [k_ref[:, _head(h)].astype(BF16) for h in range(2)]
        vs = [v_ref[:, _head(h)].astype(BF16) for h in range(2)]
        ccs = [cc_ref[0, :, h * HEAD_DIM:h * HEAD_DIM + 1] for h in range(2)]
        ones = jnp.ones((8, HEAD_DIM), BF16)

        def step(i, carry):
            rows = pl.ds(pl.multiple_of(i * ROW_TILE, ROW_TILE), ROW_TILE)
            new = []
            for h in range(2):
                dk, dv, fold = carry[h]
                qh = q_ref[rows, _head(h)].astype(BF16)
                doh = do_ref[rows, _head(h)]
                s_t = _dot(ks[h], qh, "nt") * ATT_SCALE + cr_ref[0, i, h:h + 1, :] - ccs[h]
                p_t = jnp.exp(_causal_t(s_t, i, kj, BWD_KEY_TILE) - lse_ref[0, i, h:h + 1, :])
                dp_t = _dot(vs[h], doh, "nt")
                hi, mid, lo = _split3(doh * o_ref[rows, _head(h)])
                row_sum = lambda part: lax.dot_general(ones, part, _DIMS["nt"], preferred_element_type=F32)
                delta = ((row_sum(lo) + row_sum(mid)) + row_sum(hi))[0:1, :]
                ds_t = p_t * (dp_t - delta)
                dq_ref[rows, _head(h)] += _dot(ds_t, ks[h], "tn") * ATT_SCALE
                dcr_ref[0, i, h:h + 1, :] += jnp.sum(ds_t, axis=0, keepdims=True)
                new.append((dk + _dot(ds_t, qh, "nn") * ATT_SCALE, dv + _dot(p_t, doh, "nn"), fold + (ds_t[:, :LANE] + ds_t[:, LANE:])))
            return tuple(new)

        one_head = (jnp.zeros((BWD_KEY_TILE, HEAD_DIM), F32), jnp.zeros((BWD_KEY_TILE, HEAD_DIM), F32), jnp.zeros((BWD_KEY_TILE, LANE), F32))
        res = lax.fori_loop(kj * BWD_KEY_TILE // ROW_TILE, nq, step, (one_head, one_head))
        dcc_ref[...] = jnp.zeros((1, BWD_KEY_TILE // ROW_TILE, 8, ROW_TILE), F32)
        for h in range(2):
            dk_ref[:, _head(h)] = res[h][0]
            dv_ref[:, _head(h)] = res[h][1]
            per_key = jnp.broadcast_to(-jnp.sum(res[h][2], axis=-1, keepdims=True), (BWD_KEY_TILE, LANE)).T[0:1, :]
            for b in range(BWD_KEY_TILE // ROW_TILE):
                dcc_ref[0, b, h:h + 1, :] = per_key[:, b * ROW_TILE:(b + 1) * ROW_TILE]

    blk = pl.BlockSpec((BWD_KEY_TILE, LANE), lambda hp, j: (j, hp))
    full = pl.BlockSpec((seq, LANE), lambda hp, j: (0, hp))
    cols = pl.BlockSpec((1, BWD_KEY_TILE, LANE), lambda hp, j: (hp, j, 0))
    rows = pl.BlockSpec((1, nq, 8, ROW_TILE), lambda hp, j: (hp, 0, 0, 0))
    key_rows = pl.BlockSpec((1, BWD_KEY_TILE // ROW_TILE, 8, ROW_TILE), lambda hp, j: (hp, j, 0, 0))
    return pl.pallas_call(
        body,
        out_shape=(jax.ShapeDtypeStruct((seq, w), F32), jax.ShapeDtypeStruct((seq, w), F32), jax.ShapeDtypeStruct((seq, w), F32),
                   jax.ShapeDtypeStruct((N_HEAD_PAIRS, nq, 8, ROW_TILE), F32), jax.ShapeDtypeStruct((N_HEAD_PAIRS, nq, 8, ROW_TILE), F32)),
        grid=(N_HEAD_PAIRS, nk), in_specs=[full, blk, blk, cols, rows, full, rows, full],
        out_specs=(full, blk, blk, key_rows, rows),
        compiler_params=_params(("parallel", "arbitrary")), name="fox_bwd",
    )(q, k, v, cum_cols, cum_rows, o, lse, do)


def _split3(x):
    hi = x.astype(BF16)
    r1 = x - hi.astype(F32)
    mid = r1.astype(BF16)
    lo = (r1 - mid.astype(F32)).astype(BF16)
    return hi, mid, lo


def _tri_sum(tri, x):
    hi, mid, lo = _split3(x)
    dot = lambda p: lax.dot_general(tri, p, _DIMS["nn"], preferred_element_type=F32)
    return (dot(lo) + dot(mid)) + dot(hi)


def _tri(lower):
    r = lax.broadcasted_iota(jnp.int32, (LANE, LANE), 0)
    c = lax.broadcasted_iota(jnp.int32, (LANE, LANE), 1)
    return jnp.where((c <= r) if lower else (c >= r), 1.0, 0.0).astype(BF16)


def _fgate_fwd(fl, bf):
    seq, w = fl.shape
    nb = seq // LANE

    def body(f_ref, b_ref, cum_ref):
        tri = _tri(True)

        def step(i, carry):
            rows = pl.ds(pl.multiple_of(i * LANE, LANE), LANE)
            x = f_ref[rows, :] + b_ref[...]
            logsig = jnp.minimum(x, 0.0) - jnp.log(1.0 + jnp.exp(-jnp.abs(x)))
            cum = _tri_sum(tri, logsig) + carry
            cum_ref[rows, :] = cum
            return cum[LANE - 1:LANE, :]

        lax.fori_loop(0, nb, step, jnp.zeros((1, w), F32))

    return pl.pallas_call(body, out_shape=jax.ShapeDtypeStruct((seq, w), F32), compiler_params=_params(), name="fgate_fwd")(fl, bf)


def _fgate_bwd(fl, bf, dcum):
    seq, w = fl.shape
    nb = seq // LANE

    def body(f_ref, b_ref, dc_ref, df_ref, db_ref):
        tri = _tri(False)

        def step(i, carry):
            tail, db = carry
            rows = pl.ds(pl.multiple_of((nb - 1 - i) * LANE, LANE), LANE)
            suffix = _tri_sum(tri, dc_ref[rows, :]) + tail
            df = suffix * (1.0 - _sigmoid(f_ref[rows, :] + b_ref[...]))
            df_ref[rows, :] = df
            return suffix[0:1, :], db + jnp.sum(df, axis=0, keepdims=True)

        _, db = lax.fori_loop(0, nb, step, (jnp.zeros((1, w), F32), jnp.zeros((1, w), F32)))
        db_ref[...] = db

    return pl.pallas_call(
        body, out_shape=(jax.ShapeDtypeStruct((seq, w), F32), jax.ShapeDtypeStruct((1, w), F32)),
        compiler_params=_params(), name="fgate_bwd",
    )(fl, bf, dcum)


def _cum_layouts(cum):
    seq = cum.shape[0]
    nblk = seq // ROW_TILE
    heads = cum[:, :N_FOX_HEADS]
    cq = jnp.repeat(heads, HEAD_DIM, axis=1).reshape(seq, N_HEAD_PAIRS, LANE).transpose(1, 0, 2)
    ck = heads.T.reshape(N_HEAD_PAIRS, 2, nblk, ROW_TILE).transpose(0, 2, 1, 3)
    ck = jnp.pad(ck, ((0, 0), (0, 0), (0, 6), (0, 0)))
    return cq, ck


def _dcum_from_layouts(key_side, query_side):
    both = key_side + query_side
    seq = both.shape[1] * both.shape[3]
    heads = both[:, :, :2, :].transpose(0, 2, 1, 3).reshape(N_FOX_HEADS, seq).T
    return jnp.pad(heads, ((0, 0), (0, F_PAD - N_FOX_HEADS)))


def _local_step(x, mem, target, w, arrive, emit):
    vec = lambda a: a.reshape(1, -1)
    mem_n = _rms_fwd(mem, vec(w["g_mem"]))
    saved = []
    shared = None
    for l in range(N_LAYERS):
        arrive(l, x, "mixer")
        s = {"x_in": x}
        s["h"] = h = _rms_fwd(x, vec(w["g_mix"][l]))
        s["mk"] = mk = _mm(mem_n, w["w_mk"][l])
        s["mv"] = mv = _mm(mem_n, w["w_mv"][l])
        if l < N_A:
            s["pa"] = pa = _mm(h, w["w_a"][l])
            s["pg"] = pg = _mm(h, w["w_g"][l])
            s["pq"] = pq = _mm(h, w["w_qm"][l])
            s["v"], s["c"] = _conv_glu_fwd(pa, pg, vec(w["b_glu_a"][l]), vec(w["b_glu_g"][l]), w["w_dw_a"][l], vec(w["b_dw_a"][l]))
            s["mix"] = mix = _ln_silu_fwd(s["c"], vec(w["ln_g"][l]), vec(w["ln_b"][l]))
        else:
            if l == N_A:
                shared = {"x_in": x}
                shared["hk"] = hk = _rms_fwd(x, vec(w["g_kv"]))
                shared["k"] = _mm(hk, w["w_k"])
                shared["v"] = _mm(hk, w["w_v"])
                shared["fl"] = _mm(hk, w["w_f"])
                cum = _fgate_fwd(shared["fl"], w["b_f"])
                shared["cq"], shared["ck"] = _cum_layouts(cum)
            s["q"] = q = _mm(h, w["w_q"][l - N_A])
            s["pq"] = pq = _mm(h, w["w_qm"][l])
            s["o"], mix, s["lse"] = _fox_fwd(q, shared["k"], shared["v"], shared["cq"], shared["ck"])
            s["mix"] = mix
        s["mo"] = mo = _memattn_fwd(pq, mk, mv)
        x = _mm(mix, w["w_o_mix"][l], add=x)
        s["x_mid"] = x = _mm(mo, w["w_o_mem"][l], add=x)
        arrive(l, x, "ffn")
        s["hf"] = hf = _rms_fwd(x, vec(w["g_ffn"][l]))
        s["ug"] = ug = _mm(hf, w["w_up_g"][l])
        s["uv"] = uv = _mm(hf, w["w_up_v"][l])
        s["z"] = z = _ffn_mid_fwd(ug, uv, w["w_dw_f_g"][l], w["w_dw_f_v"][l], vec(w["b_dw_f_g"][l]), vec(w["b_dw_f_v"][l]))
        x = _mm(z, w["w_down"][l], add=x)
        saved.append(s)

    loss_row, dx, dxb, dg_final = _loss_head(x, vec(w["g_final"]), target)
    loss = loss_row[0, 0]

    g = {n: [None] * N_LAYERS for n in ("g_mix", "w_mem_kv", "w_out", "g_ffn", "w_up", "w_dw_f", "b_dw_f", "w_down")}
    for n in ("w_in_a", "b_glu", "w_dw_a", "b_dw_a", "ln_g", "ln_b", "w_in_b"):
        g[n] = [None] * N_A
    one = lambda arr: ([arr], 0)
    after_start = lambda started, grad: grad if started is None else grad + started[0, 0].astype(BF16)
    g["g_final"] = [one(dg_final[0])]
    dmem_n = None
    dk_sum = dv_sum = dcq_sum = dck_sum = None
    for l in reversed(range(N_LAYERS)):
        s = saved[l]
        dz = _mm(dxb, w["w_down"][l], "nt")
        g["w_down"][l] = one(_mm(s["z"], dxb, "tn", BF16))
        dug, duv, dwg, dwv, dbg, dbv = _ffn_mid_bwd(s["ug"], s["uv"], dz, w["w_dw_f_g"][l], w["w_dw_f_v"][l],
                                                    vec(w["b_dw_f_g"][l]), vec(w["b_dw_f_v"][l]))
        g["w_up"][l] = ([_mm(s["hf"], dug, "tn", BF16), _mm(s["hf"], duv, "tn", BF16)], 1)
        g["w_dw_f"][l] = ([dwg[:FFN_CONV_W], dwv[:FFN_CONV_W]], 1)
        g["b_dw_f"][l] = ([dbg[0], dbv[0]], 0)
        dhf = _mm(duv, w["w_up_v"][l], "nt", add=_mm(dug, w["w_up_g"][l], "nt"))
        dx, dxb, dg_ffn = _rms_bwd(s["x_mid"], vec(w["g_ffn"][l]), dhf, dx)
        g["g_ffn"][l] = one(dg_ffn[0])
        dxb = after_start(emit(l, g, "ffn"), dxb)
        dmix = _mm(dxb, w["w_o_mix"][l], "nt")
        dmo = _mm(dxb, w["w_o_mem"][l], "nt")
        g["w_out"][l] = ([_mm(s["mix"], dxb, "tn", BF16), _mm(s["mo"], dxb, "tn", BF16)], 0)
        dpq, dmk, dmv = _memattn_bwd(s["pq"], s["mk"], s["mv"], dmo)
        g["w_mem_kv"][l] = ([_mm(mem_n, dmk, "tn", BF16), _mm(mem_n, dmv, "tn", BF16)], 1)
        dmem_n = _mm(dmk, w["w_mk"][l], "nt", add=dmem_n)
        dmem_n = _mm(dmv, w["w_mv"][l], "nt", add=dmem_n)
        if l < N_A:
            dc, dlng, dlnb = _ln_silu_bwd(dmix, s["c"], vec(w["ln_g"][l]), vec(w["ln_b"][l]))
            da, dgate, dwdw, dbdw, dba, dbg2 = _conv_glu_bwd(dc, s["v"], s["pa"], s["pg"], vec(w["b_glu_a"][l]),
                                                              vec(w["b_glu_g"][l]), w["w_dw_a"][l])
            g["ln_g"][l], g["ln_b"][l], g["b_dw_a"][l] = one(dlng[0]), one(dlnb[0]), one(dbdw[0])
            g["w_dw_a"][l] = ([dwdw[:CONV_W]], 1)
            g["b_glu"][l] = ([dba[0], dbg2[0]], 0)
            g["w_in_a"][l] = ([_mm(s["h"], da, "tn", BF16), _mm(s["h"], dgate, "tn", BF16), _mm(s["h"], dpq, "tn", BF16)], 1)
            dh = _mm(da, w["w_a"][l], "nt")
            dh = _mm(dgate, w["w_g"][l], "nt", add=dh)
            dh = _mm(dpq, w["w_qm"][l], "nt", add=dh)
        else:
            dq, dk, dv, dcq, dck = _fox_bwd(s["q"], shared["k"], shared["v"], shared["cq"], shared["ck"], s["o"], s["lse"], dmix)
            dk_sum = dk if dk_sum is None else dk_sum + dk
            dv_sum = dv if dv_sum is None else dv_sum + dv
            dcq_sum = dcq if dcq_sum is None else dcq_sum + dcq
            dck_sum = dck if dck_sum is None else dck_sum + dck
            g["w_in_b"][l - N_A] = ([_mm(s["h"], dq, "tn", BF16), _mm(s["h"], dpq, "tn", BF16)], 1)
            dh = _mm(dq, w["w_q"][l - N_A], "nt")
            dh = _mm(dpq, w["w_qm"][l], "nt", add=dh)
        dx, dxb, dg_mix = _rms_bwd(s["x_in"], vec(w["g_mix"][l]), dh, dx)
        g["g_mix"][l] = one(dg_mix[0])
        if l == N_A:
            df, dbf = _fgate_bwd(shared["fl"], w["b_f"], _dcum_from_layouts(dcq_sum, dck_sum))
            hk = shared["hk"]
            g["w_kvf"] = [([_mm(hk, dk_sum, "tn", BF16), _mm(hk, dv_sum, "tn", BF16), _mm(hk, df, "tn", BF16)[:, :N_FOX_HEADS]], 1)]
            g["b_f"] = [one(dbf[0, :N_FOX_HEADS])]
            dhk = _mm(dk_sum, w["w_k"], "nt")
            dhk = _mm(dv_sum, w["w_v"], "nt", add=dhk)
            dhk = _mm(df, w["w_f"], "nt", add=dhk)
            dx, dxb, dg_kv = _rms_bwd(shared["x_in"], vec(w["g_kv"]), dhk, dx)
            g["g_kv"] = [one(dg_kv[0])]
        dxb = after_start(emit(l, g, "mixer"), dxb)
    _, _, dg_mem = _rms_bwd(mem, vec(w["g_mem"]), dmem_n, None)
    g["g_mem"] = [one(dg_mem[0])]
    return loss, dx, g


SHARDED = {
    "w_in_a": ((N_A, D_MODEL, 2 * CONV_CH + MEM_W), 2),
    "b_glu": ((N_A, 2 * CONV_CH), 1),
    "w_dw_a": ((N_A, CONV_W, CONV_CH), 2),
    "b_dw_a": ((N_A, CONV_CH), 1),
    "ln_g": ((N_A, CONV_CH), 1),
    "ln_b": ((N_A, CONV_CH), 1),
    "w_kvf": ((D_MODEL, 2 * CONV_CH + N_FOX_HEADS), 1),
    "w_in_b": ((N_LAYERS - N_A, D_MODEL, CONV_CH + MEM_W), 1),
    "w_mem_kv": ((N_LAYERS, D_MODEL, 2 * MEM_W), 1),
    "w_out": ((N_LAYERS, D_MODEL, D_MODEL), 1),
    "w_up": ((N_LAYERS, D_MODEL, 2 * D_FF), 2),
    "w_dw_f": ((N_LAYERS, FFN_CONV_W, 2 * D_FF), 2),
    "w_down": ((N_LAYERS, D_FF, D_MODEL), 1),
}
REPLICATED = {
    "g_mix": (N_LAYERS, D_MODEL), "g_kv": (D_MODEL,), "b_f": (N_FOX_HEADS,), "g_mem": (D_MODEL,),
    "g_ffn": (N_LAYERS, D_MODEL), "b_dw_f": (N_LAYERS, 2 * D_FF), "g_final": (D_MODEL,),
}
WEIGHT_ORDER = ["g_mix", "w_in_a", "b_glu", "w_dw_a", "b_dw_a", "ln_g", "ln_b", "g_kv", "w_kvf", "b_f", "w_in_b", "g_mem",
                "w_mem_kv", "w_out", "g_ffn", "w_up", "w_dw_f", "b_dw_f", "w_down", "g_final"]


def _shard_shape(name):
    shape, axis = SHARDED[name]
    return tuple(d // N_CHIPS if i == axis else d for i, d in enumerate(shape))


def _size(shape):
    n = 1
    for d in shape:
        n *= d
    return n


VECTOR_LEAVES = ("b_glu", "w_dw_a", "b_dw_a", "ln_g", "ln_b", "w_dw_f")
_HALF_ROW_TILE = 16
_VECTOR_ELEMS = sum(_size(_shard_shape(n)) for n in VECTOR_LEAVES)
VECTOR_ROWS = -(-_VECTOR_ELEMS // (FLAT_COLS * 2 * _HALF_ROW_TILE)) * 2 * _HALF_ROW_TILE
REP_ELEMS = sum(_size(s) for s in REPLICATED.values())
REP_ROWS = 8
REP_COLS = -(-REP_ELEMS // (REP_ROWS * LANE)) * LANE


def _flatten_vectors(parts, dtype):
    flat = jnp.concatenate([p.reshape(-1).astype(dtype) for p in parts])
    return jnp.pad(flat, (0, VECTOR_ROWS * FLAT_COLS - _VECTOR_ELEMS)).reshape(2, VECTOR_ROWS // 2, FLAT_COLS)


def _unflatten_vectors(flat, lead=()):
    flat = flat.reshape(lead + (-1,))
    out, off = {}, 0
    for n in VECTOR_LEAVES:
        shp = _shard_shape(n)
        out[n] = flat[..., off:off + _size(shp)].reshape(lead + shp)
        off += _size(shp)
    return out


def _span(parts, cat_axis, sel_axis, lo, hi):
    if sel_axis != cat_axis:
        return jnp.concatenate([lax.slice_in_dim(p, lo, hi, axis=sel_axis) for p in parts], axis=cat_axis)
    taken, off = [], 0
    for p in parts:
        n = p.shape[cat_axis]
        a, b = max(lo, off), min(hi, off + n)
        if a < b:
            taken.append(lax.slice_in_dim(p, a - off, b - off, axis=cat_axis))
        off += n
    return taken[0] if len(taken) == 1 else jnp.concatenate(taken, axis=cat_axis)


FFN_LEAVES = 2
LAYER_LEAVES = tuple(
    ([("w_in_a", l)] if l < N_A else ([("w_kvf", None)] if l == N_A else []) + [("w_in_b", l - N_A)])
    + [("w_mem_kv", l), ("w_out", l), ("w_up", l), ("w_down", l)]
    for l in range(N_LAYERS))


def _layer_halves(weights, leaves):
    out = []
    for name, idx in leaves:
        shard = weights[name] if idx is None else weights[name][idx]
        rows, cols = shard.shape
        assert (rows // 2) % _HALF_ROW_TILE == 0, name
        out.append(shard.astype(BF16).reshape(2, rows // 2, cols))
    return out


def _layer_pieces(l, gathered, vectors):
    parts = {}
    for (name, idx), g in zip(LAYER_LEAVES[l], gathered):
        if g is not None:
            parts[name] = g

    def cut(name, sel_axis, lo, hi):
        g = parts[name]
        axis = SHARDED[name][1] - (len(SHARDED[name][0]) - 2)
        if axis == 0:
            return lax.slice_in_dim(g.reshape(-1, g.shape[-1]), lo, hi, axis=sel_axis)
        return _span([g[t].reshape(-1, g.shape[-1]) for t in range(N_CHIPS)], axis, sel_axis, lo, hi)

    def vcut(name, idx, sel_axis, lo, hi):
        axis = SHARDED[name][1] - 1
        return _span([vectors[name][t, idx] for t in range(N_CHIPS)], axis, sel_axis, lo, hi)

    def pad_rows(arr, rows):
        return jnp.pad(arr, ((0, rows - arr.shape[0]), (0, 0)))

    w = {}
    if "w_up" in parts:
        w["w_up_g"] = cut("w_up", 1, 0, D_FF)
        w["w_up_v"] = cut("w_up", 1, D_FF, 2 * D_FF)
        w["w_dw_f_g"] = pad_rows(vcut("w_dw_f", l, 1, 0, D_FF), FFN_PAD)
        w["w_dw_f_v"] = pad_rows(vcut("w_dw_f", l, 1, D_FF, 2 * D_FF), FFN_PAD)
        w["w_down"] = cut("w_down", 0, 0, D_FF)
    if "w_out" not in parts:
        return w
    if l < N_A:
        w["w_a"] = cut("w_in_a", 1, 0, CONV_CH)
        w["w_g"] = cut("w_in_a", 1, CONV_CH, 2 * CONV_CH)
        w["w_qm"] = cut("w_in_a", 1, 2 * CONV_CH, 2 * CONV_CH + MEM_W)
        w["b_glu_a"] = vcut("b_glu", l, 0, 0, CONV_CH)
        w["b_glu_g"] = vcut("b_glu", l, 0, CONV_CH, 2 * CONV_CH)
        w["w_dw_a"] = pad_rows(vcut("w_dw_a", l, 1, 0, CONV_CH), CONV_PAD)
        for n in ("b_dw_a", "ln_g", "ln_b"):
            w[n] = vcut(n, l, 0, 0, CONV_CH)
    else:
        if l == N_A:
            w["w_k"] = cut("w_kvf", 1, 0, CONV_CH)
            w["w_v"] = cut("w_kvf", 1, CONV_CH, 2 * CONV_CH)
            w["w_f"] = jnp.pad(cut("w_kvf", 1, 2 * CONV_CH, 2 * CONV_CH + N_FOX_HEADS), ((0, 0), (0, F_PAD - N_FOX_HEADS)))
        w["w_q"] = cut("w_in_b", 1, 0, CONV_CH)
        w["w_qm"] = cut("w_in_b", 1, CONV_CH, CONV_CH + MEM_W)
    w["w_mk"] = cut("w_mem_kv", 1, 0, MEM_W)
    w["w_mv"] = cut("w_mem_kv", 1, MEM_W, 2 * MEM_W)
    w["w_o_mix"] = cut("w_out", 0, 0, CONV_CH)
    w["w_o_mem"] = cut("w_out", 0, CONV_CH, D_MODEL)
    return w


_PER_LAYER = ("w_a", "w_g", "w_q", "w_qm", "b_glu_a", "b_glu_g", "w_dw_a", "b_dw_a", "ln_g", "ln_b", "w_mk", "w_mv", "w_o_mix", "w_o_mem",
              "w_up_g", "w_up_v", "w_dw_f_g", "w_dw_f_v", "w_down")


def _weight_table(rep):
    w = dict(rep)
    w["b_f"] = jnp.pad(rep["b_f"], (0, F_PAD - N_FOX_HEADS)).reshape(1, F_PAD)
    w["b_dw_f_g"], w["b_dw_f_v"] = rep["b_dw_f"][:, :D_FF], rep["b_dw_f"][:, D_FF:]
    for n in _PER_LAYER:
        w[n] = {}
    return w


def _install_layer(w, l, pieces):
    for n, p in pieces.items():
        if n in _PER_LAYER:
            w[n][l - N_A if n == "w_q" else l] = p
        else:
            w[n] = p


def _shard_of(entry, name, t):
    shape, axis = SHARDED[name]
    pieces, cat_axis = entry
    axis -= len(shape) - len(pieces[0].shape)
    step = shape[SHARDED[name][1]] // N_CHIPS
    return _span(pieces, cat_axis, axis, t * step, (t + 1) * step)


def _layer_slabs(grads, leaves):
    out = []
    for name, idx in leaves:
        pieces, cat_axis = grads[name][0 if idx is None else idx]
        if cat_axis == 0 and SHARDED[name][1] - (len(SHARDED[name][0]) - 2) == 0:
            whole = pieces[0] if len(pieces) == 1 else jnp.concatenate(pieces, axis=0)
            out.append(whole.astype(BF16).reshape(N_CHIPS, 2, whole.shape[0] // (2 * N_CHIPS), whole.shape[1]))
            continue
        shards = [_shard_of((pieces, cat_axis), name, t).astype(BF16) for t in range(N_CHIPS)]
        rows, cols = shards[0].shape
        out.append(jnp.stack(shards).reshape(N_CHIPS, 2, rows // 2, cols))
    return out


def _vector_slabs(grads):
    def shard(name, t):
        return jnp.stack([_shard_of(entry, name, t) for entry in grads[name]])
    return jnp.stack([_flatten_vectors([shard(n, t) for n in VECTOR_LEAVES], BF16) for t in range(N_CHIPS)])


def _replicated_rows(grads):
    parts = [p.reshape(-1) for n in REPLICATED for pieces, _ in grads[n] for p in pieces]
    flat = jnp.concatenate(parts)
    return jnp.pad(flat, (0, REP_ROWS * REP_COLS - REP_ELEMS)).reshape(REP_ROWS, REP_COLS)


_ANY = pl.BlockSpec(memory_space=pl.ANY)


def _place():
    x, y, c = lax.axis_index("x"), lax.axis_index("y"), lax.axis_index("c")
    chips = [(1 - x, y), (x, 1 - y), (1 - x, 1 - y)]
    return x, y, c, chips


def _chip_index(chip):
    return 2 * chip[0] + chip[1]


def _gather_leaves(halves):
    n = len(halves)

    def body(*refs):
        w_refs, out_refs, (send_sems, recv_sems) = refs[:n], refs[n:2 * n], refs[2 * n:]
        x, y, c, chips = _place()
        me = 2 * x + y
        sibling = (x, y, 1 - c)

        def copy(a, sem, chip_idx, half, to, src=None):
            dst = out_refs[a].at[chip_idx, half]
            return pltpu.make_async_remote_copy(src_ref=dst if src is None else src, dst_ref=dst, send_sem=send_sems.at[6 * a + sem],
                                                recv_sem=recv_sems.at[6 * a + sem], device_id=to, device_id_type=MESH_ID)

        first = [copy(a, j, me, c, (*chip, c), src=w_refs[a].at[c]) for j, chip in enumerate(chips) for a in range(n)]
        for cp in first:
            cp.start()
        passed = []
        for j, chip in enumerate(chips):
            for a in range(n):
                copy(a, j, _chip_index(chip), c, (x, y, c)).wait_recv()
                passed.append(copy(a, 3 + j, _chip_index(chip), c, sibling))
                passed[-1].start()
        for j, chip in enumerate(chips):
            for a in range(n):
                copy(a, 3 + j, _chip_index(chip), 1 - c, (x, y, c)).wait_recv()
        for cp in first + passed:
            cp.wait_send()

    return pl.pallas_call(
        body, out_shape=[jax.ShapeDtypeStruct((N_CHIPS,) + h.shape, h.dtype) for h in halves],
        in_specs=[_ANY] * n, out_specs=[_ANY] * n,
        scratch_shapes=[pltpu.SemaphoreType.DMA((6 * n,)), pltpu.SemaphoreType.DMA((6 * n,))], name="gather_leaves",
    )(*halves)


_HBM = pl.BlockSpec(memory_space=pltpu.HBM)
_SEM = pl.BlockSpec(memory_space=pltpu.SEMAPHORE)


def _in_hbm(a):
    return pltpu.with_memory_space_constraint(a, pltpu.HBM)


def _gather_start(groups, after):
    flat = [h for g in groups for h in g]
    n, ng = len(flat), len(groups)

    def body(*refs):
        srcs, lands, sems = refs[1:1 + n], refs[1 + n:1 + 2 * n], refs[1 + 2 * n:1 + 2 * n + 2 * ng]
        token = refs[-1]
        x, y, c, chips = _place()
        me = 2 * x + y
        a = 0
        for gi, g in enumerate(groups):
            for k in range(len(g)):
                for j, chip in enumerate(chips):
                    pltpu.make_async_remote_copy(src_ref=srcs[a].at[c], dst_ref=lands[a].at[me, c], send_sem=sems[2 * gi].at[3 * k + j],
                                                 recv_sem=sems[2 * gi + 1].at[3 * k + j], device_id=(*chip, c), device_id_type=MESH_ID).start()
                a += 1
        token[...] = jnp.zeros_like(token)

    sem_shapes = [pltpu.SemaphoreType.DMA((3 * len(g),)) for g in groups for _ in range(2)]
    out = pl.pallas_call(
        body, name="gather_start",
        out_shape=sem_shapes + [pltpu.HBM(h.shape, h.dtype) for h in flat] + [pltpu.HBM((N_CHIPS,) + h.shape, h.dtype) for h in flat]
        + [jax.ShapeDtypeStruct((8, LANE), F32)],
        in_specs=[_ANY] + [_HBM] * (2 * n), out_specs=[_SEM] * (2 * ng) + [_HBM] * (2 * n) + [pl.BlockSpec(memory_space=pltpu.VMEM)],
        input_output_aliases={1 + i: 2 * ng + i for i in range(2 * n)},
        compiler_params=pltpu.CompilerParams(has_side_effects=pltpu.SideEffectType.DATAFLOW_SIDE_EFFECTING),
    )(after, *[_in_hbm(h) for h in flat], *[_in_hbm(lax.empty((N_CHIPS,) + h.shape, h.dtype)) for h in flat])
    sems, srcs, lands = out[:2 * ng], out[2 * ng:2 * ng + n], out[2 * ng + n:2 * ng + 2 * n]
    res, a = [], 0
    for gi, g in enumerate(groups):
        res.append((sems[2 * gi], sems[2 * gi + 1], list(srcs[a:a + len(g)]), list(lands[a:a + len(g)])))
        a += len(g)
    return res


def _gather_wait(send_sems, recv_sems, srcs, lands, after, name):
    n = len(srcs)

    def body(*refs):
        src_refs, land_refs, send, recv = refs[:n], refs[n:2 * n], refs[2 * n], refs[2 * n + 1]
        x, y, c, chips = _place()
        for k in range(n):
            for j, chip in enumerate(chips):
                cp = pltpu.make_async_remote_copy(src_ref=src_refs[k].at[c], dst_ref=land_refs[k].at[_chip_index(chip), c], send_sem=send.at[3 * k + j],
                                                  recv_sem=recv.at[3 * k + j], device_id=(*chip, c), device_id_type=MESH_ID)
                cp.wait_send()
                cp.wait_recv()

    out = pl.pallas_call(
        body, name=name, out_shape=[pltpu.HBM(a.shape, a.dtype) for a in list(srcs) + list(lands)],
        in_specs=[_HBM] * (2 * n) + [_SEM, _SEM, _ANY], out_specs=[_HBM] * (2 * n), input_output_aliases={i: i for i in range(2 * n)},
        compiler_params=pltpu.CompilerParams(has_side_effects=pltpu.SideEffectType.DATAFLOW_SIDE_EFFECTING),
    )(*srcs, *lands, send_sems, recv_sems, after)
    return list(out[n:])


def _forward_halves(lands, name):
    n = len(lands)

    def body(*refs):
        out_refs, (send_sems, recv_sems) = refs[n:2 * n], refs[2 * n:]
        x, y, c, chips = _place()

        def copy(a, j, half, to):
            blk = out_refs[a].at[_chip_index(chips[j]), half]
            return pltpu.make_async_remote_copy(src_ref=blk, dst_ref=blk, send_sem=send_sems.at[3 * a + j], recv_sem=recv_sems.at[3 * a + j],
                                                device_id=to, device_id_type=MESH_ID)

        sends = [copy(a, j, c, (x, y, 1 - c)) for a in range(n) for j in range(3)]
        for cp in sends:
            cp.start()
        for a in range(n):
            for j in range(3):
                copy(a, j, 1 - c, (x, y, c)).wait_recv()
        for cp in sends:
            cp.wait_send()

    return pl.pallas_call(
        body, out_shape=[jax.ShapeDtypeStruct(a.shape, a.dtype) for a in lands], in_specs=[_ANY] * n, out_specs=[_ANY] * n,
        input_output_aliases={i: i for i in range(n)},
        scratch_shapes=[pltpu.SemaphoreType.DMA((3 * n,)), pltpu.SemaphoreType.DMA((3 * n,))], name=name,
    )(*lands)


def _swap_halves(slabs):
    n = len(slabs)

    def body(*refs):
        g_refs, got_refs, (send_sem, recv_sem) = refs[:n], refs[n:2 * n], refs[2 * n:]
        x, y, c, _ = _place()
        copies = [pltpu.make_async_remote_copy(src_ref=g_refs[a].at[t, 1 - c], dst_ref=got_refs[a].at[t], send_sem=send_sem.at[N_CHIPS * a + t],
                                               recv_sem=recv_sem.at[N_CHIPS * a + t], device_id=(x, y, 1 - c), device_id_type=MESH_ID)
                  for a in range(n) for t in range(N_CHIPS)]
        for cp in copies:
            cp.start()
        for cp in copies:
            cp.wait()

    return pl.pallas_call(
        body, out_shape=[jax.ShapeDtypeStruct((g.shape[0],) + g.shape[2:], g.dtype) for g in slabs],
        in_specs=[_ANY] * n, out_specs=[_ANY] * n,
        scratch_shapes=[pltpu.SemaphoreType.DMA((N_CHIPS * n,)), pltpu.SemaphoreType.DMA((N_CHIPS * n,))], name="swap_halves",
    )(*slabs)


def _scatter_start(parts, name):
    n = len(parts)

    def body(*refs):
        srcs, lands, send, recv = refs[:n], refs[n:2 * n], refs[2 * n], refs[2 * n + 1]
        token = refs[-1]
        x, y, c, chips = _place()
        me = 2 * x + y
        for k in range(n):
            for j, chip in enumerate(chips):
                pltpu.make_async_remote_copy(src_ref=srcs[k].at[_chip_index(chip)], dst_ref=lands[k].at[me], send_sem=send.at[3 * k + j],
                                             recv_sem=recv.at[3 * k + j], device_id=(*chip, c), device_id_type=MESH_ID).start()
        token[...] = jnp.zeros_like(token)

    out = pl.pallas_call(
        body, name=name,
        out_shape=[pltpu.SemaphoreType.DMA((3 * n,)), pltpu.SemaphoreType.DMA((3 * n,))] + [pltpu.HBM(p.shape, p.dtype) for p in parts] * 2
        + [jax.ShapeDtypeStruct((8, LANE), F32)],
        in_specs=[_HBM] * (2 * n), out_specs=[_SEM, _SEM] + [_HBM] * (2 * n) + [pl.BlockSpec(memory_space=pltpu.VMEM)],
        input_output_aliases={i: 2 + i for i in range(2 * n)},
        compiler_params=pltpu.CompilerParams(has_side_effects=pltpu.SideEffectType.DATAFLOW_SIDE_EFFECTING),
    )(*[_in_hbm(p) for p in parts], *[_in_hbm(lax.empty(p.shape, p.dtype)) for p in parts])
    return (out[0], out[1], list(out[2:2 + n]), list(out[2 + n:2 + 2 * n])), out[-1]


def _scatter_wait(send_sems, recv_sems, srcs, lands, after, name):
    n = len(srcs)

    def body(*refs):
        src_refs, land_refs, send, recv = refs[:n], refs[n:2 * n], refs[2 * n], refs[2 * n + 1]
        x, y, c, chips = _place()
        for k in range(n):
            for j, chip in enumerate(chips):
                cp = pltpu.make_async_remote_copy(src_ref=src_refs[k].at[_chip_index(chip)], dst_ref=land_refs[k].at[_chip_index(chip)],
                                                  send_sem=send.at[3 * k + j], recv_sem=recv.at[3 * k + j], device_id=(*chip, c), device_id_type=MESH_ID)
                cp.wait_send()
                cp.wait_recv()

    out = pl.pallas_call(
        body, name=name, out_shape=[pltpu.HBM(a.shape, a.dtype) for a in list(srcs) + list(lands)],
        in_specs=[_HBM] * (2 * n) + [_SEM, _SEM, _ANY], out_specs=[_HBM] * (2 * n), input_output_aliases={i: i for i in range(2 * n)},
        compiler_params=pltpu.CompilerParams(has_side_effects=pltpu.SideEffectType.DATAFLOW_SIDE_EFFECTING),
    )(*srcs, *lands, send_sems, recv_sems, after)
    return list(out[n:])


def _sibling_halves(halves):
    n = len(halves)

    def body(*refs):
        h_refs, got_refs, (send_sem, recv_sem) = refs[:n], refs[n:2 * n], refs[2 * n:]
        x, y, c, _ = _place()
        copies = [pltpu.make_async_remote_copy(src_ref=h_refs[a], dst_ref=got_refs[a], send_sem=send_sem.at[a], recv_sem=recv_sem.at[a],
                                               device_id=(x, y, 1 - c), device_id_type=MESH_ID) for a in range(n)]
        for cp in copies:
            cp.start()
        for cp in copies:
            cp.wait()

    return pl.pallas_call(
        body, out_shape=[jax.ShapeDtypeStruct(h.shape, h.dtype) for h in halves], in_specs=[_ANY] * n, out_specs=[_ANY] * n,
        scratch_shapes=[pltpu.SemaphoreType.DMA((n,)), pltpu.SemaphoreType.DMA((n,))], name="sibling_halves",
    )(*halves)


def _gather_replicated(rows):
    m_per, n = rows.shape

    def body(x_ref, out_ref, send_sems, recv_sems, local_sem):
        x, y, c, chips = _place()
        me, sibling = (x, y, c), (x, y, 1 - c)

        def block(px, py, pc):
            return out_ref.at[pl.ds((4 * px + 2 * py + pc) * m_per, m_per), :]

        def copy(k, blk, to, src=None):
            return pltpu.make_async_remote_copy(src_ref=block(*blk) if src is None else src, dst_ref=block(*blk),
                                                send_sem=send_sems.at[k], recv_sem=recv_sems.at[k], device_id=to, device_id_type=MESH_ID)

        mine = pltpu.make_async_copy(x_ref, block(*me), local_sem)
        mine.start()
        first = [copy(0, me, sibling, src=x_ref)]
        first += [copy(1 + j, me, (*chip, c), src=x_ref) for j, chip in enumerate(chips)]
        for cp in first:
            cp.start()
        passed = [copy(4 + j, (*chip, c), sibling) for j, chip in enumerate(chips)]
        for j, chip in enumerate(chips):
            copy(1 + j, (*chip, c), me).wait_recv()
            passed[j].start()
        copy(0, sibling, me).wait_recv()
        for j, chip in enumerate(chips):
            copy(4 + j, (*chip, 1 - c), me).wait_recv()
        for cp in first + passed:
            cp.wait_send()
        mine.wait()

    vmem = pl.BlockSpec(memory_space=pltpu.VMEM)
    return pl.pallas_call(
        body, out_shape=jax.ShapeDtypeStruct((N_DEV * m_per, n), rows.dtype), in_specs=[vmem], out_specs=vmem,
        scratch_shapes=[pltpu.SemaphoreType.DMA((7,)), pltpu.SemaphoreType.DMA((7,)), pltpu.SemaphoreType.DMA],
        name="gather_replicated",
    )(rows)


_SUM_ROWS = 256


def _tile_rows(rows):
    if rows <= _SUM_ROWS:
        return rows
    best = None
    for r in range(16, _SUM_ROWS + 1, 16):
        if rows % r == 0:
            best = r
    assert best is not None, rows
    return best


def _add_pairs(slabs, got, core):
    n, _, rows, cols = slabs.shape
    tr = _tile_rows(rows)

    def body(core_ref, a_ref, b_ref, o_ref):
        o_ref[...] = (a_ref[0].astype(F32) + b_ref[...].astype(F32)).astype(BF16)

    spec = pl.BlockSpec((1, tr, cols), lambda t, i, core_ref: (t, i, 0))
    return pl.pallas_call(
        body, out_shape=jax.ShapeDtypeStruct((n, rows, cols), BF16),
        grid_spec=pltpu.PrefetchScalarGridSpec(
            num_scalar_prefetch=1, grid=(n, rows // tr),
            in_specs=[pl.BlockSpec((1, 1, tr, cols), lambda t, i, core_ref: (t, core_ref[0], i, 0)), spec], out_specs=spec),
        compiler_params=_params(("parallel", "parallel")), name=f"add_pairs_{rows}x{cols}",
    )(core, slabs, got)


def _sum_slabs(slabs):
    n, rows, cols = slabs.shape
    tr = _tile_rows(rows)

    def body(s_ref, o_ref):
        acc = s_ref[0].astype(F32)
        for t in range(1, n):
            acc = acc + s_ref[t].astype(F32)
        o_ref[...] = acc

    return pl.pallas_call(
        body, out_shape=jax.ShapeDtypeStruct((rows, cols), F32), grid=(rows // tr,),
        in_specs=[pl.BlockSpec((n, tr, cols), lambda i: (0, i, 0))], out_specs=pl.BlockSpec((tr, cols), lambda i: (i, 0)),
        compiler_params=_params(("parallel",)), name=f"sum_slabs_{n}x{rows}x{cols}",
    )(slabs)


_ADAM_BLOCK_BYTES = 2 * 1024 * 1024


def _adamw(w, g, m, v):
    shape = w.shape
    cols = shape[-1]
    rows = _size(shape) // cols
    tr = rows
    if rows * cols * 4 > _ADAM_BLOCK_BYTES:
        for r in range(8, rows, 8):
            if rows % r == 0 and r * cols * 4 <= _ADAM_BLOCK_BYTES:
                tr = r

    def body(w_ref, g_ref, m_ref, v_ref, d_ref, nm_ref, nv_ref):
        gv = g_ref[...]
        nm = ADAM_B1 * m_ref[...] + (1.0 - ADAM_B1) * gv
        nv = ADAM_B2 * v_ref[...] + (1.0 - ADAM_B2) * jnp.square(gv)
        m_hat = nm / (1.0 - ADAM_B1 ** ADAM_STEP)
        v_hat = nv / (1.0 - ADAM_B2 ** ADAM_STEP)
        d_ref[...] = -ADAM_LR * (m_hat / (jnp.sqrt(v_hat) + ADAM_EPS) + ADAM_WD * w_ref[...])
        nm_ref[...] = nm
        nv_ref[...] = nv

    spec = pl.BlockSpec((tr, cols), lambda i: (i, 0))
    out = pl.pallas_call(
        body, out_shape=tuple(jax.ShapeDtypeStruct((rows, cols), F32) for _ in range(3)), grid=(rows // tr,),
        in_specs=[spec] * 4, out_specs=(spec,) * 3, compiler_params=_params(("parallel",)), name=f"adamw_{rows}x{cols}",
    )(*(a.reshape(rows, cols) for a in (w, g, m, v)))
    return tuple(o.reshape(shape) for o in out)


def kernel(x, mem, g_mix, w_in_a, b_glu, w_dw_a, b_dw_a, ln_g, ln_b, g_kv, w_kvf, b_f, w_in_b, g_mem, w_mem_kv, w_out, g_ffn, w_up, w_dw_f, b_dw_f, w_down, g_final, loss_target, m_g_mix, m_w_in_a, m_b_glu, m_w_dw_a, m_b_dw_a, m_ln_g, m_ln_b, m_g_kv, m_w_kvf, m_b_f, m_w_in_b, m_g_mem, m_w_mem_kv, m_w_out, m_g_ffn, m_w_up, m_w_dw_f, m_b_dw_f, m_w_down, m_g_final, v_g_mix, v_w_in_a, v_b_glu, v_w_dw_a, v_b_dw_a, v_ln_g, v_ln_b, v_g_kv, v_w_kvf, v_b_f, v_w_in_b, v_g_mem, v_w_mem_kv, v_w_out, v_g_ffn, v_w_up, v_w_dw_f, v_b_dw_f, v_w_down, v_g_final):
    weights = dict(g_mix=g_mix, w_in_a=w_in_a, b_glu=b_glu, w_dw_a=w_dw_a, b_dw_a=b_dw_a, ln_g=ln_g, ln_b=ln_b, g_kv=g_kv,
                   w_kvf=w_kvf, b_f=b_f, w_in_b=w_in_b, g_mem=g_mem, w_mem_kv=w_mem_kv, w_out=w_out, g_ffn=g_ffn, w_up=w_up,
                   w_dw_f=w_dw_f, b_dw_f=b_dw_f, w_down=w_down, g_final=g_final)
    mom1 = dict(g_mix=m_g_mix, w_in_a=m_w_in_a, b_glu=m_b_glu, w_dw_a=m_w_dw_a, b_dw_a=m_b_dw_a, ln_g=m_ln_g, ln_b=m_ln_b,
                g_kv=m_g_kv, w_kvf=m_w_kvf, b_f=m_b_f, w_in_b=m_w_in_b, g_mem=m_g_mem, w_mem_kv=m_w_mem_kv, w_out=m_w_out,
                g_ffn=m_g_ffn, w_up=m_w_up, w_dw_f=m_w_dw_f, b_dw_f=m_b_dw_f, w_down=m_w_down, g_final=m_g_final)
    mom2 = dict(g_mix=v_g_mix, w_in_a=v_w_in_a, b_glu=v_b_glu, w_dw_a=v_w_dw_a, b_dw_a=v_b_dw_a, ln_g=v_ln_g, ln_b=v_ln_b,
                g_kv=v_g_kv, w_kvf=v_w_kvf, b_f=v_b_f, w_in_b=v_w_in_b, g_mem=v_g_mem, w_mem_kv=v_w_mem_kv, w_out=v_w_out,
                g_ffn=v_g_ffn, w_up=v_w_up, w_dw_f=v_w_dw_f, b_dw_f=v_b_dw_f, w_down=v_w_down, g_final=v_g_final)

    x_pos, y_pos, core = lax.axis_index("x"), lax.axis_index("y"), lax.axis_index("c")
    chip = 2 * x_pos + y_pos

    mixer_of = lambda l: LAYER_LEAVES[l][:-FFN_LEAVES]
    ffn_of = lambda l: LAYER_LEAVES[l][-FFN_LEAVES:]
    local_first = _layer_halves(weights, mixer_of(0))
    vector_halves = _flatten_vectors([weights[n] for n in VECTOR_LEAVES], F32)
    first = _gather_leaves(local_first + [vector_halves])
    later = [ffn_of(0)] + [LAYER_LEAVES[l] for l in range(1, N_LAYERS)]
    local_later = [_layer_halves(weights, leaves) for leaves in later]
    in_flight = _gather_start(local_later, first[-1])

    def with_own(gathered, own):
        return [lax.dynamic_update_index_in_dim(g, o[None], chip, 0) for g, o in zip(gathered, own)]

    first = with_own(first, local_first + [vector_halves])
    vectors = _unflatten_vectors(first[-1], (N_CHIPS,))
    table = _weight_table({n: weights[n] for n in REPLICATED})

    def fetch(k, after):
        send_sems, recv_sems, srcs, lands = in_flight[k]
        lands = _gather_wait(send_sems, recv_sems, srcs, lands, after, f"gather_wait_{k}")
        return with_own(_forward_halves(lands, f"forward_halves_{k}"), local_later[k])

    def arrive(l, after, part):
        if l == 0:
            gathered = first[:-1] + [None] * FFN_LEAVES if part == "mixer" else [None] * len(mixer_of(0)) + fetch(0, after)
        elif part == "mixer":
            gathered = fetch(l, after)
        else:
            return
        _install_layer(table, l, _layer_pieces(l, gathered, vectors))

    core_index = core.astype(jnp.int32).reshape(1)
    in_flight_back = []

    def emit(l, g, part):
        if l == 0:
            leaves = ffn_of(0) if part == "ffn" else mixer_of(0)
        elif part == "mixer":
            leaves = LAYER_LEAVES[l]
        else:
            return None
        with_vectors = l == 0 and part == "mixer"
        slabs = _layer_slabs(g, leaves) + ([_vector_slabs(g)] if with_vectors else [])
        chip_sums = [_add_pairs(mine, got, core_index) for mine, got in zip(slabs, _swap_halves(slabs))]
        handles, started = _scatter_start(chip_sums, f"scatter_start_{len(in_flight_back)}")
        in_flight_back.append((leaves, with_vectors, chip_sums, handles))
        return started

    loss, dx, grads = _local_step(x[0], mem[0], loss_target[0], table, arrive, emit)
    loss = lax.psum(loss, ("x", "y", "c"))

    names, mine = [], []
    for k, (leaves, with_vectors, chip_sums, (send_sems, recv_sems, srcs, lands)) in enumerate(in_flight_back):
        lands = _scatter_wait(send_sems, recv_sems, srcs, lands, dx, f"scatter_wait_{k}")
        from_chips = [lax.dynamic_update_index_in_dim(got, lax.dynamic_index_in_dim(own, chip, 0, keepdims=True), chip, 0)
                      for got, own in zip(lands, chip_sums)]
        mine += [_sum_slabs(f) for f in from_chips]
        names += list(leaves) + ([("vectors", None)] if with_vectors else [])
    theirs = _sibling_halves(mine)
    per_leaf = {}
    for (name, idx), m, t in zip(names, mine, theirs):
        per_leaf.setdefault(name, {})[idx] = jnp.concatenate([jnp.where(core == 0, m, t), jnp.where(core == 0, t, m)])
    vector_grads = per_leaf.pop("vectors")[None]
    grad_leaves = {n: (d[None] if None in d else jnp.stack([d[i] for i in sorted(d)])) for n, d in per_leaf.items()}
    grad_leaves.update(_unflatten_vectors(vector_grads))

    rep_sum = _sum_slabs(_gather_replicated(_replicated_rows(grads)).reshape(N_DEV, REP_ROWS, REP_COLS)).reshape(-1)
    off = 0
    for n, shp in REPLICATED.items():
        grad_leaves[n] = rep_sum[off:off + _size(shp)].reshape(shp)
        off += _size(shp)

    deltas, new_m, new_v = {}, {}, {}
    for n in WEIGHT_ORDER:
        deltas[n], new_m[n], new_v[n] = _adamw(weights[n], grad_leaves[n], mom1[n], mom2[n])
    return (loss, dx[None], *[grad_leaves[n] for n in WEIGHT_ORDER], *[deltas[n] for n in WEIGHT_ORDER],
            *[new_m[n] for n in WEIGHT_ORDER], *[new_v[n] for n in WEIGHT_ORDER])
```

```python
import jax
import jax.numpy as jnp
from jax import lax
from jax.experimental import pallas as pl
from jax.experimental.pallas import tpu as pltpu

F32 = jnp.float32
BF16 = jnp.bfloat16

D_MODEL = 1024
N_LAYERS = 4
N_A = 2
CONV_CH = 768
MEM_W = 256
HEAD_DIM = 64
N_MEM_HEADS = 4
N_FOX_HEADS = 12
N_HEAD_PAIRS = N_FOX_HEADS // 2
D_FF = 2816
CONV_W = 31
CONV_PAD = 32
FFN_CONV_W = 3
FFN_PAD = 8
F_PAD = 128
RMS_EPS = 1e-6
LN_EPS = 1e-5
ATT_SCALE = HEAD_DIM ** -0.5
NEG_BIG = -1e30

ADAM_LR = 0.001
ADAM_B1 = 0.9
ADAM_B2 = 0.999
ADAM_EPS = 1e-08
ADAM_WD = 0.01
ADAM_STEP = 10

LANE = 128
ROW_TILE = 256
CHUNK = 128
VMEM_LIMIT = 48 * 1024 * 1024
FLAT_COLS = 1024
N_CHIPS = 4
N_DEV = 8
MESH_ID = pl.DeviceIdType.MESH


def _params(sem=None):
    return pltpu.CompilerParams(dimension_semantics=sem, vmem_limit_bytes=VMEM_LIMIT)


def _tile(dim, pref):
    if dim <= pref:
        return dim
    best = None
    for m in range(1, dim // LANE + 1):
        d = m * LANE
        if dim % d == 0 and d <= pref:
            best = d
    assert best is not None, dim
    return best


_DIMS = {"nn": (((1,), (0,)), ((), ())), "nt": (((1,), (1,)), ((), ())), "tn": (((0,), (0,)), ((), ()))}


_MM_RESIDENT_BYTES = 8 * 1024 * 1024
_MM_STREAM_BYTES = 6 * 1024 * 1024
_MM_OUT_BYTES = 6 * 1024 * 1024


def _mm_tiles(m, n, k, mode, a_size, b_size, o_size):
    if mode == "tn":
        tm = _tile(m, _MM_RESIDENT_BYTES // (k * a_size))
        tn = _tile(n, min(_MM_STREAM_BYTES // (k * b_size), _MM_OUT_BYTES // (tm * o_size)))
    else:
        tn = _tile(n, _MM_RESIDENT_BYTES // (k * b_size))
        tm = _tile(m, min(_MM_STREAM_BYTES // (k * a_size), _MM_OUT_BYTES // (tn * o_size), 512))
    return tm, tn


def _mm(a, b, mode="nn", out_dtype=F32, add=None):
    if mode == "nn":
        (m, k), (k2, n) = a.shape, b.shape
    elif mode == "nt":
        (m, k), (n, k2) = a.shape, b.shape
    else:
        (k, m), (k2, n) = a.shape, b.shape
    assert k == k2, (a.shape, b.shape, mode)
    tm, tn = _mm_tiles(m, n, k, mode, a.dtype.itemsize, b.dtype.itemsize, jnp.dtype(out_dtype).itemsize)
    dims = _DIMS[mode]
    has_add = add is not None

    def body(*refs):
        if has_add:
            a_ref, b_ref, add_ref, o_ref = refs
        else:
            a_ref, b_ref, o_ref = refs
        r = lax.dot_general(a_ref[...].astype(BF16), b_ref[...].astype(BF16), dims, preferred_element_type=F32)
        if has_add:
            r = r + add_ref[...]
        o_ref[...] = r.astype(out_dtype)

    a_spec = pl.BlockSpec((k, tm), lambda i, j: (0, i)) if mode == "tn" else pl.BlockSpec((tm, k), lambda i, j: (i, 0))
    b_spec = pl.BlockSpec((tn, k), lambda i, j: (j, 0)) if mode == "nt" else pl.BlockSpec((k, tn), lambda i, j: (0, j))
    o_spec = pl.BlockSpec((tm, tn), lambda i, j: (i, j))
    in_specs = [a_spec, b_spec] + ([o_spec] if has_add else [])
    args = (a, b) + ((add,) if has_add else ())
    return pl.pallas_call(
        body,
        out_shape=jax.ShapeDtypeStruct((m, n), out_dtype),
        grid=(m // tm, n // tn),
        in_specs=in_specs,
        out_specs=o_spec,
        compiler_params=_params(("parallel", "parallel")),
        name=f"mm_{mode}_{m}x{k}x{n}",
    )(*args)


def _row(width):
    return pl.BlockSpec((ROW_TILE, width), lambda i: (i, 0))


def _vec(width):
    return pl.BlockSpec((1, width), lambda i: (0, 0))


def _rms_fwd(x, g):
    rows, d = x.shape

    def body(x_ref, g_ref, o_ref):
        xv = x_ref[...]
        rstd = lax.rsqrt(jnp.mean(xv * xv, axis=-1, keepdims=True) + RMS_EPS)
        o_ref[...] = (xv * rstd * g_ref[...]).astype(BF16)

    return pl.pallas_call(
        body, out_shape=jax.ShapeDtypeStruct((rows, d), BF16), grid=(rows // ROW_TILE,),
        in_specs=[_row(d), _vec(d)], out_specs=_row(d), compiler_params=_params(("parallel",)), name=f"rms_fwd_{rows}",
    )(x, g)


def _accumulate(ref, value, step):
    @pl.when(step == 0)
    def _():
        ref[...] = value

    @pl.when(step > 0)
    def _():
        ref[...] += value


def _rms_bwd(x, g, dh, dres):
    rows, d = x.shape
    has_res = dres is not None

    def body(*refs):
        if has_res:
            x_ref, g_ref, dh_ref, dres_ref, dx_ref, dxb_ref, dg_ref = refs
        else:
            x_ref, g_ref, dh_ref, dx_ref, dxb_ref, dg_ref = refs
        xv = x_ref[...]
        dhv = dh_ref[...]
        rstd = lax.rsqrt(jnp.mean(xv * xv, axis=-1, keepdims=True) + RMS_EPS)
        xhat = xv * rstd
        gd = dhv * g_ref[...]
        dx = rstd * (gd - xhat * jnp.mean(gd * xhat, axis=-1, keepdims=True))
        if has_res:
            dx = dx + dres_ref[...]
        dx_ref[...] = dx
        dxb_ref[...] = dx.astype(BF16)
        _accumulate(dg_ref, jnp.sum(dhv * xhat, axis=0, keepdims=True), pl.program_id(0))

    in_specs = [_row(d), _vec(d), _row(d)] + ([_row(d)] if has_res else [])
    args = (x, g, dh) + ((dres,) if has_res else ())
    return pl.pallas_call(
        body,
        out_shape=(jax.ShapeDtypeStruct((rows, d), F32), jax.ShapeDtypeStruct((rows, d), BF16), jax.ShapeDtypeStruct((1, d), F32)),
        grid=(rows // ROW_TILE,), in_specs=in_specs, out_specs=(_row(d), _row(d), _vec(d)),
        compiler_params=_params(("arbitrary",)), name=f"rms_bwd_{rows}_{int(has_res)}",
    )(*args)


def _loss_head(x, g, target):
    rows, d = x.shape

    def body(x_ref, g_ref, t_ref, loss_ref, dx_ref, dxb_ref, dg_ref):
        xv = x_ref[...]
        gv = g_ref[...]
        rstd = lax.rsqrt(jnp.mean(xv * xv, axis=-1, keepdims=True) + RMS_EPS)
        xhat = xv * rstd
        err = xhat * gv - t_ref[...]
        part = 0.5 * jnp.sum(jnp.mean(err * err, axis=-1, keepdims=True), axis=0, keepdims=True)
        dy = err * (1.0 / d)
        gd = dy * gv
        dx = rstd * (gd - xhat * jnp.mean(gd * xhat, axis=-1, keepdims=True))
        dx_ref[...] = dx
        dxb_ref[...] = dx.astype(BF16)
        step = pl.program_id(0)
        _accumulate(loss_ref, jnp.broadcast_to(part, (1, LANE)), step)
        _accumulate(dg_ref, jnp.sum(dy * xhat, axis=0, keepdims=True), step)

    return pl.pallas_call(
        body,
        out_shape=(jax.ShapeDtypeStruct((1, LANE), F32), jax.ShapeDtypeStruct((rows, d), F32),
                   jax.ShapeDtypeStruct((rows, d), BF16), jax.ShapeDtypeStruct((1, d), F32)),
        grid=(rows // ROW_TILE,), in_specs=[_row(d), _vec(d), _row(d)],
        out_specs=(_vec(LANE), _row(d), _row(d), _vec(d)),
        compiler_params=_params(("arbitrary",)), name="loss_head",
    )(x, g, target)


def _sigmoid(x):
    return 1.0 / (1.0 + jnp.exp(-x))


def _ln_silu_fwd(c, ln_g, ln_b):
    rows, ch = c.shape

    def body(c_ref, g_ref, b_ref, o_ref):
        cv = c_ref[...]
        mu = jnp.mean(cv, axis=-1, keepdims=True)
        cen = cv - mu
        rstd = lax.rsqrt(jnp.mean(cen * cen, axis=-1, keepdims=True) + LN_EPS)
        y = cen * rstd * g_ref[...] + b_ref[...]
        o_ref[...] = (y * _sigmoid(y)).astype(BF16)

    return pl.pallas_call(
        body, out_shape=jax.ShapeDtypeStruct((rows, ch), BF16), grid=(rows // ROW_TILE,),
        in_specs=[_row(ch), _vec(ch), _vec(ch)], out_specs=_row(ch), compiler_params=_params(("parallel",)), name="ln_silu_fwd",
    )(c, ln_g, ln_b)


def _ln_silu_bwd(dmix, c, ln_g, ln_b):
    rows, ch = c.shape

    def body(dm_ref, c_ref, g_ref, b_ref, dc_ref, dg_ref, db_ref):
        cv = c_ref[...]
        gv = g_ref[...]
        mu = jnp.mean(cv, axis=-1, keepdims=True)
        cen = cv - mu
        rstd = lax.rsqrt(jnp.mean(cen * cen, axis=-1, keepdims=True) + LN_EPS)
        xhat = cen * rstd
        y = xhat * gv + b_ref[...]
        sg = _sigmoid(y)
        dy = dm_ref[...] * (sg * (1.0 + y * (1.0 - sg)))
        dxh = dy * gv
        dc = rstd * (dxh - jnp.mean(dxh, axis=-1, keepdims=True) - xhat * jnp.mean(dxh * xhat, axis=-1, keepdims=True))
        dc_ref[...] = dc
        step = pl.program_id(0)
        _accumulate(dg_ref, jnp.sum(dy * xhat, axis=0, keepdims=True), step)
        _accumulate(db_ref, jnp.sum(dy, axis=0, keepdims=True), step)

    return pl.pallas_call(
        body,
        out_shape=(jax.ShapeDtypeStruct((rows, ch), F32), jax.ShapeDtypeStruct((1, ch), F32), jax.ShapeDtypeStruct((1, ch), F32)),
        grid=(rows // ROW_TILE,), in_specs=[_row(ch), _row(ch), _vec(ch), _vec(ch)],
        out_specs=(_row(ch), _vec(ch), _vec(ch)), compiler_params=_params(("arbitrary",)), name="ln_silu_bwd",
    )(dmix, c, ln_g, ln_b)


def _col(rows, cb):
    return pl.BlockSpec((rows, cb), lambda j: (0, j))


SUBLANES = 8
_WINDOW = CHUNK + CONV_PAD


def _realign(win, shifted_ref):
    shifted_ref[0] = win
    for r in range(1, SUBLANES):
        shifted_ref[r, 0:_WINDOW - SUBLANES, :] = win[r:r + _WINDOW - SUBLANES, :]


def _tap(shifted_ref, off):
    r = off % SUBLANES
    return shifted_ref[r, off - r:off - r + CHUNK, :]


def _conv_glu_fwd(pa, pg, ba, bg, w_dw, b_dw):
    seq, ch = pa.shape
    cb = LANE
    n_chunks = seq // CHUNK

    def body(pa_ref, pg_ref, ba_ref, bg_ref, w_ref, b_ref, v_ref, c_ref, vpad_ref, vshift_ref):
        vpad_ref[0:CONV_PAD, :] = jnp.zeros((CONV_PAD, cb), F32)

        def glu(r, carry):
            r0 = pl.multiple_of(r * CHUNK, CHUNK)
            rows = pl.ds(r0, CHUNK)
            v = (pa_ref[rows, :] + ba_ref[...]) * _sigmoid(pg_ref[rows, :] + bg_ref[...])
            v_ref[rows, :] = v
            vpad_ref[pl.ds(r0 + CONV_PAD, CHUNK), :] = v
            return carry

        lax.fori_loop(0, n_chunks, glu, 0)

        def conv(r, carry):
            r0 = pl.multiple_of(r * CHUNK, CHUNK)
            _realign(vpad_ref[pl.ds(r0, _WINDOW), :], vshift_ref)
            acc = jnp.broadcast_to(b_ref[...], (CHUNK, cb))
            for t in range(CONV_W):
                acc = acc + w_ref[t:t + 1, :] * _tap(vshift_ref, CONV_PAD - (CONV_W - 1) + t)
            c_ref[pl.ds(r0, CHUNK), :] = acc
            return carry

        lax.fori_loop(0, n_chunks, conv, 0)

    return pl.pallas_call(
        body,
        out_shape=(jax.ShapeDtypeStruct((seq, ch), F32), jax.ShapeDtypeStruct((seq, ch), F32)),
        grid=(ch // cb,),
        in_specs=[_col(seq, cb), _col(seq, cb), _col(1, cb), _col(1, cb), _col(CONV_PAD, cb), _col(1, cb)],
        out_specs=(_col(seq, cb), _col(seq, cb)),
        scratch_shapes=[pltpu.VMEM((seq + CONV_PAD, cb), F32), pltpu.VMEM((SUBLANES, _WINDOW, cb), F32)],
        compiler_params=_params(("parallel",)), name="conv_glu_fwd",
    )(pa, pg, ba, bg, w_dw, b_dw)


def _conv_glu_bwd(dc, v, pa, pg, ba, bg, w_dw):
    seq, ch = dc.shape
    cb = LANE
    n_chunks = seq // CHUNK

    def body(dc_ref, v_ref, pa_ref, pg_ref, ba_ref, bg_ref, w_ref, da_ref, dg_ref, dw_ref, dbdw_ref, dba_ref, dbg_ref,
             dcpad_ref, vpad_ref, dcshift_ref, vshift_ref):
        vpad_ref[0:CONV_PAD, :] = jnp.zeros((CONV_PAD, cb), F32)
        dcpad_ref[seq:seq + CONV_PAD, :] = jnp.zeros((CONV_PAD, cb), F32)
        dw_ref[...] = jnp.zeros((CONV_PAD, cb), F32)
        dbdw_ref[...] = jnp.zeros((1, cb), F32)
        dba_ref[...] = jnp.zeros((1, cb), F32)
        dbg_ref[...] = jnp.zeros((1, cb), F32)

        def fill(r, carry):
            r0 = pl.multiple_of(r * CHUNK, CHUNK)
            vpad_ref[pl.ds(r0 + CONV_PAD, CHUNK), :] = v_ref[pl.ds(r0, CHUNK), :]
            dcpad_ref[pl.ds(r0, CHUNK), :] = dc_ref[pl.ds(r0, CHUNK), :]
            return carry

        lax.fori_loop(0, n_chunks, fill, 0)

        def step(r, carry):
            r0 = pl.multiple_of(r * CHUNK, CHUNK)
            rows = pl.ds(r0, CHUNK)
            _realign(dcpad_ref[pl.ds(r0, _WINDOW), :], dcshift_ref)
            _realign(vpad_ref[pl.ds(r0, _WINDOW), :], vshift_ref)
            dcur = _tap(dcshift_ref, 0)
            dv = jnp.zeros((CHUNK, cb), F32)
            for t in range(CONV_W):
                dv = dv + w_ref[t:t + 1, :] * _tap(dcshift_ref, CONV_W - 1 - t)
                dw_ref[t:t + 1, :] += jnp.sum(dcur * _tap(vshift_ref, CONV_PAD - (CONV_W - 1) + t), axis=0, keepdims=True)
            a = pa_ref[rows, :] + ba_ref[...]
            sg = _sigmoid(pg_ref[rows, :] + bg_ref[...])
            da = dv * sg
            dgate = dv * a * sg * (1.0 - sg)
            da_ref[rows, :] = da.astype(BF16)
            dg_ref[rows, :] = dgate.astype(BF16)
            dbdw_ref[...] += jnp.sum(dcur, axis=0, keepdims=True)
            dba_ref[...] += jnp.sum(da, axis=0, keepdims=True)
            dbg_ref[...] += jnp.sum(dgate, axis=0, keepdims=True)
            return carry

        lax.fori_loop(0, n_chunks, step, 0)

    return pl.pallas_call(
        body,
        out_shape=(jax.ShapeDtypeStruct((seq, ch), BF16), jax.ShapeDtypeStruct((seq, ch), BF16),
                   jax.ShapeDtypeStruct((CONV_PAD, ch), F32), jax.ShapeDtypeStruct((1, ch), F32),
                   jax.ShapeDtypeStruct((1, ch), F32), jax.ShapeDtypeStruct((1, ch), F32)),
        grid=(ch // cb,),
        in_specs=[_col(seq, cb), _col(seq, cb), _col(seq, cb), _col(seq, cb), _col(1, cb), _col(1, cb), _col(CONV_PAD, cb)],
        out_specs=(_col(seq, cb), _col(seq, cb), _col(CONV_PAD, cb), _col(1, cb), _col(1, cb), _col(1, cb)),
        scratch_shapes=[pltpu.VMEM((seq + CONV_PAD, cb), F32), pltpu.VMEM((seq + CONV_PAD, cb), F32),
                        pltpu.VMEM((SUBLANES, _WINDOW, cb), F32), pltpu.VMEM((SUBLANES, _WINDOW, cb), F32)],
        compiler_params=_params(("parallel",)), name="conv_glu_bwd",
    )(dc, v, pa, pg, ba, bg, w_dw)


def _ffn_conv(pad_ref, r0, w_ref, b_ref, cb):
    win = pad_ref[pl.ds(r0, CHUNK + FFN_PAD), :]
    y = jnp.broadcast_to(b_ref[...], (CHUNK, cb))
    for t in range(FFN_CONV_W):
        off = FFN_PAD - (FFN_CONV_W - 1) + t
        y = y + w_ref[t:t + 1, :] * win[off:off + CHUNK, :]
    return y, win


def _ffn_mid_fwd(ug, uv, wg, wv, bg, bv):
    seq, ch = ug.shape
    cb = _tile(ch, 256)
    n_chunks = seq // CHUNK

    def body(ug_ref, uv_ref, wg_ref, wv_ref, bg_ref, bv_ref, z_ref, gpad_ref, vpad_ref):
        gpad_ref[0:FFN_PAD, :] = jnp.zeros((FFN_PAD, cb), F32)
        vpad_ref[0:FFN_PAD, :] = jnp.zeros((FFN_PAD, cb), F32)

        def fill(r, carry):
            r0 = pl.multiple_of(r * CHUNK, CHUNK)
            gpad_ref[pl.ds(r0 + FFN_PAD, CHUNK), :] = ug_ref[pl.ds(r0, CHUNK), :]
            vpad_ref[pl.ds(r0 + FFN_PAD, CHUNK), :] = uv_ref[pl.ds(r0, CHUNK), :]
            return carry

        lax.fori_loop(0, n_chunks, fill, 0)

        def step(r, carry):
            r0 = pl.multiple_of(r * CHUNK, CHUNK)
            yg, _ = _ffn_conv(gpad_ref, r0, wg_ref, bg_ref, cb)
            yv, _ = _ffn_conv(vpad_ref, r0, wv_ref, bv_ref, cb)
            z_ref[pl.ds(r0, CHUNK), :] = (yg * _sigmoid(yg) * yv).astype(BF16)
            return carry

        lax.fori_loop(0, n_chunks, step, 0)

    return pl.pallas_call(
        body, out_shape=jax.ShapeDtypeStruct((seq, ch), BF16), grid=(ch // cb,),
        in_specs=[_col(seq, cb), _col(seq, cb), _col(FFN_PAD, cb), _col(FFN_PAD, cb), _col(1, cb), _col(1, cb)],
        out_specs=_col(seq, cb),
        scratch_shapes=[pltpu.VMEM((seq + FFN_PAD, cb), F32), pltpu.VMEM((seq + FFN_PAD, cb), F32)],
        compiler_params=_params(("parallel",)), name="ffn_mid_fwd",
    )(ug, uv, wg, wv, bg, bv)


def _ffn_mid_bwd(ug, uv, dz, wg, wv, bg, bv):
    seq, ch = ug.shape
    cb = _tile(ch, 256)
    n_chunks = seq // CHUNK

    def body(ug_ref, uv_ref, dz_ref, wg_ref, wv_ref, bg_ref, bv_ref, dug_ref, duv_ref, dwg_ref, dwv_ref, dbg_ref, dbv_ref,
             gpad_ref, vpad_ref, dyg_ref, dyv_ref):
        gpad_ref[0:FFN_PAD, :] = jnp.zeros((FFN_PAD, cb), F32)
        vpad_ref[0:FFN_PAD, :] = jnp.zeros((FFN_PAD, cb), F32)
        dyg_ref[seq:seq + FFN_PAD, :] = jnp.zeros((FFN_PAD, cb), F32)
        dyv_ref[seq:seq + FFN_PAD, :] = jnp.zeros((FFN_PAD, cb), F32)
        dwg_ref[...] = jnp.zeros((FFN_PAD, cb), F32)
        dwv_ref[...] = jnp.zeros((FFN_PAD, cb), F32)
        dbg_ref[...] = jnp.zeros((1, cb), F32)
        dbv_ref[...] = jnp.zeros((1, cb), F32)

        def fill(r, carry):
            r0 = pl.multiple_of(r * CHUNK, CHUNK)
            gpad_ref[pl.ds(r0 + FFN_PAD, CHUNK), :] = ug_ref[pl.ds(r0, CHUNK), :]
            vpad_ref[pl.ds(r0 + FFN_PAD, CHUNK), :] = uv_ref[pl.ds(r0, CHUNK), :]
            return carry

        lax.fori_loop(0, n_chunks, fill, 0)

        def grads_of_conv_out(r, carry):
            r0 = pl.multiple_of(r * CHUNK, CHUNK)
            rows = pl.ds(r0, CHUNK)
            yg, gwin = _ffn_conv(gpad_ref, r0, wg_ref, bg_ref, cb)
            yv, vwin = _ffn_conv(vpad_ref, r0, wv_ref, bv_ref, cb)
            dzv = dz_ref[rows, :]
            sg = _sigmoid(yg)
            dyg = dzv * yv * (sg * (1.0 + yg * (1.0 - sg)))
            dyv = dzv * yg * sg
            dyg_ref[rows, :] = dyg
            dyv_ref[rows, :] = dyv
            for t in range(FFN_CONV_W):
                off = FFN_PAD - (FFN_CONV_W - 1) + t
                dwg_ref[t:t + 1, :] += jnp.sum(dyg * gwin[off:off + CHUNK, :], axis=0, keepdims=True)
                dwv_ref[t:t + 1, :] += jnp.sum(dyv * vwin[off:off + CHUNK, :], axis=0, keepdims=True)
            dbg_ref[...] += jnp.sum(dyg, axis=0, keepdims=True)
            dbv_ref[...] += jnp.sum(dyv, axis=0, keepdims=True)
            return carry

        lax.fori_loop(0, n_chunks, grads_of_conv_out, 0)

        def grads_of_conv_in(r, carry):
            r0 = pl.multiple_of(r * CHUNK, CHUNK)
            gwin = dyg_ref[pl.ds(r0, CHUNK + FFN_PAD), :]
            vwin = dyv_ref[pl.ds(r0, CHUNK + FFN_PAD), :]
            dug = jnp.zeros((CHUNK, cb), F32)
            duv = jnp.zeros((CHUNK, cb), F32)
            for t in range(FFN_CONV_W):
                off = FFN_CONV_W - 1 - t
                dug = dug + wg_ref[t:t + 1, :] * gwin[off:off + CHUNK, :]
                duv = duv + wv_ref[t:t + 1, :] * vwin[off:off + CHUNK, :]
            dug_ref[pl.ds(r0, CHUNK), :] = dug.astype(BF16)
            duv_ref[pl.ds(r0, CHUNK), :] = duv.astype(BF16)
            return carry

        lax.fori_loop(0, n_chunks, grads_of_conv_in, 0)

    return pl.pallas_call(
        body,
        out_shape=(jax.ShapeDtypeStruct((seq, ch), BF16), jax.ShapeDtypeStruct((seq, ch), BF16),
                   jax.ShapeDtypeStruct((FFN_PAD, ch), F32), jax.ShapeDtypeStruct((FFN_PAD, ch), F32),
                   jax.ShapeDtypeStruct((1, ch), F32), jax.ShapeDtypeStruct((1, ch), F32)),
        grid=(ch // cb,),
        in_specs=[_col(seq, cb), _col(seq, cb), _col(seq, cb), _col(FFN_PAD, cb), _col(FFN_PAD, cb), _col(1, cb), _col(1, cb)],
        out_specs=(_col(seq, cb), _col(seq, cb), _col(FFN_PAD, cb), _col(FFN_PAD, cb), _col(1, cb), _col(1, cb)),
        scratch_shapes=[pltpu.VMEM((seq + FFN_PAD, cb), F32) for _ in range(4)],
        compiler_params=_params(("parallel",)), name="ffn_mid_bwd",
    )(ug, uv, dz, wg, wv, bg, bv)


def _dot(a, b, mode):
    return lax.dot_general(a.astype(BF16), b.astype(BF16), _DIMS[mode], preferred_element_type=F32)


def _head(h):
    return slice(h * HEAD_DIM, (h + 1) * HEAD_DIM)


def _mem_softmax(q, k):
    s = _dot(q, k, "nt") * ATT_SCALE
    e = jnp.exp(s - jnp.max(s, axis=-1, keepdims=True))
    return e / jnp.sum(e, axis=-1, keepdims=True)


MEM_ROW_TILE = 512


def _mem_row(width):
    return pl.BlockSpec((MEM_ROW_TILE, width), lambda i: (i, 0))


def _memattn_fwd(pq, mk, mv):
    seq, w = pq.shape
    m = mk.shape[0]

    def body(q_ref, k_ref, v_ref, o_ref):
        for h in range(N_MEM_HEADS):
            p = _mem_softmax(q_ref[:, _head(h)], k_ref[:, _head(h)])
            o_ref[:, _head(h)] = _dot(p, v_ref[:, _head(h)], "nn").astype(BF16)

    full = pl.BlockSpec((m, w), lambda i: (0, 0))
    return pl.pallas_call(
        body, out_shape=jax.ShapeDtypeStruct((seq, w), BF16), grid=(seq // MEM_ROW_TILE,),
        in_specs=[_mem_row(w), full, full], out_specs=_mem_row(w), compiler_params=_params(("parallel",)), name="memattn_fwd",
    )(pq, mk, mv)


def _memattn_bwd(pq, mk, mv, dmo):
    seq, w = pq.shape
    m = mk.shape[0]

    def body(q_ref, k_ref, v_ref, do_ref, dq_ref, dk_ref, dv_ref):
        step = pl.program_id(0)

        @pl.when(step == 0)
        def _():
            dk_ref[...] = jnp.zeros((m, w), F32)
            dv_ref[...] = jnp.zeros((m, w), F32)

        for h in range(N_MEM_HEADS):
            q, k, v, do = q_ref[:, _head(h)], k_ref[:, _head(h)], v_ref[:, _head(h)], do_ref[:, _head(h)]
            p = _mem_softmax(q, k)
            dp = _dot(do, v, "nt")
            ds = p * (dp - jnp.sum(dp * p, axis=-1, keepdims=True))
            dq_ref[:, _head(h)] = (_dot(ds, k, "nn") * ATT_SCALE).astype(BF16)
            dk_ref[:, _head(h)] += _dot(ds, q, "tn") * ATT_SCALE
            dv_ref[:, _head(h)] += _dot(p, do, "tn")

    full = pl.BlockSpec((m, w), lambda i: (0, 0))
    return pl.pallas_call(
        body,
        out_shape=(jax.ShapeDtypeStruct((seq, w), BF16), jax.ShapeDtypeStruct((m, w), F32), jax.ShapeDtypeStruct((m, w), F32)),
        grid=(seq // MEM_ROW_TILE,), in_specs=[_mem_row(w), full, full, _mem_row(w)], out_specs=(_mem_row(w), full, full),
        compiler_params=_params(("arbitrary",)), name="memattn_bwd",
    )(pq, mk, mv, dmo)


FWD_KEY_TILE = 512
BWD_KEY_TILE = 512


def _causal_t(s_t, q_blk, k_blk, key_tile):
    kpos = k_blk * key_tile + lax.broadcasted_iota(jnp.int32, (key_tile, ROW_TILE), 0)
    qpos = q_blk * ROW_TILE + lax.broadcasted_iota(jnp.int32, (key_tile, ROW_TILE), 1)
    return jnp.where(kpos <= qpos, s_t, NEG_BIG)


def _fox_fwd(q, k, v, cum_cols, cum_rows):
    seq, w = q.shape
    nq = seq // ROW_TILE

    def body(q_ref, k_ref, v_ref, cc_ref, cr_ref, o_ref, ob_ref, lse_ref):
        i = pl.program_id(1)
        qs = [q_ref[:, _head(h)].astype(BF16) for h in range(2)]
        crs = [cr_ref[0, 0, h:h + 1, :] for h in range(2)]

        def step(j, carry):
            rows = pl.ds(pl.multiple_of(j * FWD_KEY_TILE, FWD_KEY_TILE), FWD_KEY_TILE)
            new = []
            for h in range(2):
                m_run, l_run, acc = carry[h]
                s_t = _dot(k_ref[rows, _head(h)], qs[h], "nt") * ATT_SCALE + crs[h] - cc_ref[0, rows, h * HEAD_DIM:h * HEAD_DIM + 1]
                s_t = _causal_t(s_t, i, j, FWD_KEY_TILE)
                m_new = jnp.maximum(m_run, jnp.max(s_t, axis=0, keepdims=True))
                alpha = jnp.exp(m_run - m_new)
                p_t = jnp.exp(s_t - m_new)
                l_new = alpha * l_run + jnp.sum(p_t, axis=0, keepdims=True)
                new.append((m_new, l_new, alpha * acc + _dot(v_ref[rows, _head(h)], p_t, "tn")))
            return tuple(new)

        one_head = (jnp.full((1, ROW_TILE), NEG_BIG, F32), jnp.zeros((1, ROW_TILE), F32), jnp.zeros((HEAD_DIM, ROW_TILE), F32))
        res = lax.fori_loop(0, ((i + 1) * ROW_TILE + FWD_KEY_TILE - 1) // FWD_KEY_TILE, step, (one_head, one_head))
        o = jnp.concatenate([acc / l_run for _, l_run, acc in res], axis=0).T
        o_ref[...] = o
        ob_ref[...] = o.astype(BF16)
        lse_ref[...] = jnp.zeros((1, 1, 8, ROW_TILE), F32)
        for h in range(2):
            lse_ref[0, 0, h:h + 1, :] = res[h][0] + jnp.log(res[h][1])

    blk = pl.BlockSpec((ROW_TILE, LANE), lambda hp, i: (i, hp))
    full = pl.BlockSpec((seq, LANE), lambda hp, i: (0, hp))
    cols = pl.BlockSpec((1, seq, LANE), lambda hp, i: (hp, 0, 0))
    rows = pl.BlockSpec((1, 1, 8, ROW_TILE), lambda hp, i: (hp, i, 0, 0))
    return pl.pallas_call(
        body,
        out_shape=(jax.ShapeDtypeStruct((seq, w), F32), jax.ShapeDtypeStruct((seq, w), BF16),
                   jax.ShapeDtypeStruct((N_HEAD_PAIRS, nq, 8, ROW_TILE), F32)),
        grid=(N_HEAD_PAIRS, nq), in_specs=[blk, full, full, cols, rows], out_specs=(blk, blk, rows),
        compiler_params=_params(("parallel", "parallel")), name="fox_fwd",
    )(q, k, v, cum_cols, cum_rows)


def _fox_delta(o, do):
    seq, w = o.shape
    nq = seq // ROW_TILE

    def body(o_ref, do_ref, d_ref):
        prod_t = (o_ref[...] * do_ref[...]).T
        d_ref[...] = jnp.zeros((1, 1, 8, ROW_TILE), F32)
        for h in range(2):
            d_ref[0, 0, h:h + 1, :] = jnp.sum(prod_t[h * HEAD_DIM:(h + 1) * HEAD_DIM, :], axis=0, keepdims=True)

    blk = pl.BlockSpec((ROW_TILE, LANE), lambda hp, i: (i, hp))
    return pl.pallas_call(
        body, out_shape=jax.ShapeDtypeStruct((N_HEAD_PAIRS, nq, 8, ROW_TILE), F32), grid=(N_HEAD_PAIRS, nq),
        in_specs=[blk, blk], out_specs=pl.BlockSpec((1, 1, 8, ROW_TILE), lambda hp, i: (hp, i, 0, 0)),
        compiler_params=_params(("parallel", "parallel")), name="fox_delta",
    )(o, do)


def _fox_bwd(q, k, v, cum_cols, cum_rows, delta, lse, do):
    seq, w = q.shape
    nq = seq // ROW_TILE
    nk = seq // BWD_KEY_TILE

    def body(q_ref, k_ref, v_ref, cc_ref, cr_ref, delta_ref, lse_ref, do_ref, dq_ref, dk_ref, dv_ref, dcc_ref, dcr_ref):
        kj = pl.program_id(1)

        @pl.when(kj == 0)
        def _():
            dq_ref[...] = jnp.zeros((seq, LANE), F32)
            dcr_ref[...] = jnp.zeros((1, nq, 8, ROW_TILE), F32)

        ks = [k_ref[:, _head(h)].astype(BF16) for h in range(2)]
        vs = [v_ref[:, _head(h)].astype(BF16) for h in range(2)]
        ccs = [cc_ref[0, :, h * HEAD_DIM:h * HEAD_DIM + 1] for h in range(2)]

        def step(i, carry):
            rows = pl.ds(pl.multiple_of(i * ROW_TILE, ROW_TILE), ROW_TILE)
            new = []
            for h in range(2):
                dk, dv, fold = carry[h]
                qh = q_ref[rows, _head(h)].astype(BF16)
                doh = do_ref[rows, _head(h)]
                s_t = _dot(ks[h], qh, "nt") * ATT_SCALE + cr_ref[0, i, h:h + 1, :] - ccs[h]
                p_t = jnp.exp(_causal_t(s_t, i, kj, BWD_KEY_TILE) - lse_ref[0, i, h:h + 1, :])
                dp_t = _dot(vs[h], doh, "nt")
                ds_t = p_t * (dp_t - delta_ref[0, i, h:h + 1, :])
                dq_ref[rows, _head(h)] += _dot(ds_t, ks[h], "tn") * ATT_SCALE
                dcr_ref[0, i, h:h + 1, :] += jnp.sum(ds_t, axis=0, keepdims=True)
                new.append((dk + _dot(ds_t, qh, "nn") * ATT_SCALE, dv + _dot(p_t, doh, "nn"), fold + (ds_t[:, :LANE] + ds_t[:, LANE:])))
            return tuple(new)

        one_head = (jnp.zeros((BWD_KEY_TILE, HEAD_DIM), F32), jnp.zeros((BWD_KEY_TILE, HEAD_DIM), F32), jnp.zeros((BWD_KEY_TILE, LANE), F32))
        res = lax.fori_loop(kj * BWD_KEY_TILE // ROW_TILE, nq, step, (one_head, one_head))
        dcc_ref[...] = jnp.zeros((1, BWD_KEY_TILE // ROW_TILE, 8, ROW_TILE), F32)
        for h in range(2):
            dk_ref[:, _head(h)] = res[h][0]
            dv_ref[:, _head(h)] = res[h][1]
            per_key = jnp.broadcast_to(-jnp.sum(res[h][2], axis=-1, keepdims=True), (BWD_KEY_TILE, LANE)).T[0:1, :]
            for b in range(BWD_KEY_TILE // ROW_TILE):
                dcc_ref[0, b, h:h + 1, :] = per_key[:, b * ROW_TILE:(b + 1) * ROW_TILE]

    blk = pl.BlockSpec((BWD_KEY_TILE, LANE), lambda hp, j: (j, hp))
    full = pl.BlockSpec((seq, LANE), lambda hp, j: (0, hp))
    cols = pl.BlockSpec((1, BWD_KEY_TILE, LANE), lambda hp, j: (hp, j, 0))
    rows = pl.BlockSpec((1, nq, 8, ROW_TILE), lambda hp, j: (hp, 0, 0, 0))
    key_rows = pl.BlockSpec((1, BWD_KEY_TILE // ROW_TILE, 8, ROW_TILE), lambda hp, j: (hp, j, 0, 0))
    return pl.pallas_call(
        body,
        out_shape=(jax.ShapeDtypeStruct((seq, w), F32), jax.ShapeDtypeStruct((seq, w), F32), jax.ShapeDtypeStruct((seq, w), F32),
                   jax.ShapeDtypeStruct((N_HEAD_PAIRS, nq, 8, ROW_TILE), F32), jax.ShapeDtypeStruct((N_HEAD_PAIRS, nq, 8, ROW_TILE), F32)),
        grid=(N_HEAD_PAIRS, nk), in_specs=[full, blk, blk, cols, rows, rows, rows, full],
        out_specs=(full, blk, blk, key_rows, rows),
        compiler_params=_params(("parallel", "arbitrary")), name="fox_bwd",
    )(q, k, v, cum_cols, cum_rows, delta, lse, do)


def _split3(x):
    hi = x.astype(BF16)
    r1 = x - hi.astype(F32)
    mid = r1.astype(BF16)
    lo = (r1 - mid.astype(F32)).astype(BF16)
    return hi, mid, lo


def _tri_sum(tri, x):
    hi, mid, lo = _split3(x)
    dot = lambda p: lax.dot_general(tri, p, _DIMS["nn"], preferred_element_type=F32)
    return (dot(lo) + dot(mid)) + dot(hi)


def _tri(lower):
    r = lax.broadcasted_iota(jnp.int32, (LANE, LANE), 0)
    c = lax.broadcasted_iota(jnp.int32, (LANE, LANE), 1)
    return jnp.where((c <= r) if lower else (c >= r), 1.0, 0.0).astype(BF16)


def _fgate_fwd(fl, bf):
    seq, w = fl.shape
    nb = seq // LANE

    def body(f_ref, b_ref, cum_ref):
        tri = _tri(True)

        def step(i, carry):
            rows = pl.ds(pl.multiple_of(i * LANE, LANE), LANE)
            x = f_ref[rows, :] + b_ref[...]
            logsig = jnp.minimum(x, 0.0) - jnp.log(1.0 + jnp.exp(-jnp.abs(x)))
            cum = _tri_sum(tri, logsig) + carry
            cum_ref[rows, :] = cum
            return cum[LANE - 1:LANE, :]

        lax.fori_loop(0, nb, step, jnp.zeros((1, w), F32))

    return pl.pallas_call(body, out_shape=jax.ShapeDtypeStruct((seq, w), F32), compiler_params=_params(), name="fgate_fwd")(fl, bf)


def _fgate_bwd(fl, bf, dcum):
    seq, w = fl.shape
    nb = seq // LANE

    def body(f_ref, b_ref, dc_ref, df_ref, db_ref):
        tri = _tri(False)

        def step(i, carry):
            tail, db = carry
            rows = pl.ds(pl.multiple_of((nb - 1 - i) * LANE, LANE), LANE)
            suffix = _tri_sum(tri, dc_ref[rows, :]) + tail
            df = suffix * (1.0 - _sigmoid(f_ref[rows, :] + b_ref[...]))
            df_ref[rows, :] = df
            return suffix[0:1, :], db + jnp.sum(df, axis=0, keepdims=True)

        _, db = lax.fori_loop(0, nb, step, (jnp.zeros((1, w), F32), jnp.zeros((1, w), F32)))
        db_ref[...] = db

    return pl.pallas_call(
        body, out_shape=(jax.ShapeDtypeStruct((seq, w), F32), jax.ShapeDtypeStruct((1, w), F32)),
        compiler_params=_params(), name="fgate_bwd",
    )(fl, bf, dcum)


def _cum_layouts(cum):
    seq = cum.shape[0]
    nblk = seq // ROW_TILE
    heads = cum[:, :N_FOX_HEADS]
    cq = jnp.repeat(heads, HEAD_DIM, axis=1).reshape(seq, N_HEAD_PAIRS, LANE).transpose(1, 0, 2)
    ck = heads.T.reshape(N_HEAD_PAIRS, 2, nblk, ROW_TILE).transpose(0, 2, 1, 3)
    ck = jnp.pad(ck, ((0, 0), (0, 0), (0, 6), (0, 0)))
    return cq, ck


def _dcum_from_layouts(key_side, query_side):
    both = key_side + query_side
    seq = both.shape[1] * both.shape[3]
    heads = both[:, :, :2, :].transpose(0, 2, 1, 3).reshape(N_FOX_HEADS, seq).T
    return jnp.pad(heads, ((0, 0), (0, F_PAD - N_FOX_HEADS)))


def _local_step(x, mem, target, w, arrive, emit):
    vec = lambda a: a.reshape(1, -1)
    mem_n = _rms_fwd(mem, vec(w["g_mem"]))
    saved = []
    shared = None
    for l in range(N_LAYERS):
        arrive(l, x, "mixer")
        s = {"x_in": x}
        s["h"] = h = _rms_fwd(x, vec(w["g_mix"][l]))
        s["mk"] = mk = _mm(mem_n, w["w_mk"][l])
        s["mv"] = mv = _mm(mem_n, w["w_mv"][l])
        if l < N_A:
            s["pa"] = pa = _mm(h, w["w_a"][l])
            s["pg"] = pg = _mm(h, w["w_g"][l])
            s["pq"] = pq = _mm(h, w["w_qm"][l])
            s["v"], s["c"] = _conv_glu_fwd(pa, pg, vec(w["b_glu_a"][l]), vec(w["b_glu_g"][l]), w["w_dw_a"][l], vec(w["b_dw_a"][l]))
            s["mix"] = mix = _ln_silu_fwd(s["c"], vec(w["ln_g"][l]), vec(w["ln_b"][l]))
        else:
            if l == N_A:
                shared = {"x_in": x}
                shared["hk"] = hk = _rms_fwd(x, vec(w["g_kv"]))
                shared["k"] = _mm(hk, w["w_k"])
                shared["v"] = _mm(hk, w["w_v"])
                shared["fl"] = _mm(hk, w["w_f"])
                cum = _fgate_fwd(shared["fl"], w["b_f"])
                shared["cq"], shared["ck"] = _cum_layouts(cum)
            s["q"] = q = _mm(h, w["w_q"][l - N_A])
            s["pq"] = pq = _mm(h, w["w_qm"][l])
            s["o"], mix, s["lse"] = _fox_fwd(q, shared["k"], shared["v"], shared["cq"], shared["ck"])
            s["mix"] = mix
        s["mo"] = mo = _memattn_fwd(pq, mk, mv)
        x = _mm(mix, w["w_o_mix"][l], add=x)
        s["x_mid"] = x = _mm(mo, w["w_o_mem"][l], add=x)
        arrive(l, x, "ffn")
        s["hf"] = hf = _rms_fwd(x, vec(w["g_ffn"][l]))
        s["ug"] = ug = _mm(hf, w["w_up_g"][l])
        s["uv"] = uv = _mm(hf, w["w_up_v"][l])
        s["z"] = z = _ffn_mid_fwd(ug, uv, w["w_dw_f_g"][l], w["w_dw_f_v"][l], vec(w["b_dw_f_g"][l]), vec(w["b_dw_f_v"][l]))
        x = _mm(z, w["w_down"][l], add=x)
        saved.append(s)

    loss_row, dx, dxb, dg_final = _loss_head(x, vec(w["g_final"]), target)
    loss = loss_row[0, 0]

    g = {n: [None] * N_LAYERS for n in ("g_mix", "w_mem_kv", "w_out", "g_ffn", "w_up", "w_dw_f", "b_dw_f", "w_down")}
    for n in ("w_in_a", "b_glu", "w_dw_a", "b_dw_a", "ln_g", "ln_b", "w_in_b"):
        g[n] = [None] * N_A
    one = lambda arr: ([arr], 0)
    after_start = lambda started, grad: grad if started is None else grad + started[0, 0].astype(BF16)
    g["g_final"] = [one(dg_final[0])]
    dmem_n = None
    dk_sum = dv_sum = dcq_sum = dck_sum = None
    for l in reversed(range(N_LAYERS)):
        s = saved[l]
        dz = _mm(dxb, w["w_down"][l], "nt")
        g["w_down"][l] = one(_mm(s["z"], dxb, "tn", BF16))
        dug, duv, dwg, dwv, dbg, dbv = _ffn_mid_bwd(s["ug"], s["uv"], dz, w["w_dw_f_g"][l], w["w_dw_f_v"][l],
                                                    vec(w["b_dw_f_g"][l]), vec(w["b_dw_f_v"][l]))
        g["w_up"][l] = ([_mm(s["hf"], dug, "tn", BF16), _mm(s["hf"], duv, "tn", BF16)], 1)
        g["w_dw_f"][l] = ([dwg[:FFN_CONV_W], dwv[:FFN_CONV_W]], 1)
        g["b_dw_f"][l] = ([dbg[0], dbv[0]], 0)
        dhf = _mm(duv, w["w_up_v"][l], "nt", add=_mm(dug, w["w_up_g"][l], "nt"))
        dx, dxb, dg_ffn = _rms_bwd(s["x_mid"], vec(w["g_ffn"][l]), dhf, dx)
        g["g_ffn"][l] = one(dg_ffn[0])
        dxb = after_start(emit(l, g, "ffn"), dxb)
        dmix = _mm(dxb, w["w_o_mix"][l], "nt")
        dmo = _mm(dxb, w["w_o_mem"][l], "nt")
        g["w_out"][l] = ([_mm(s["mix"], dxb, "tn", BF16), _mm(s["mo"], dxb, "tn", BF16)], 0)
        dpq, dmk, dmv = _memattn_bwd(s["pq"], s["mk"], s["mv"], dmo)
        g["w_mem_kv"][l] = ([_mm(mem_n, dmk, "tn", BF16), _mm(mem_n, dmv, "tn", BF16)], 1)
        dmem_n = _mm(dmk, w["w_mk"][l], "nt", add=dmem_n)
        dmem_n = _mm(dmv, w["w_mv"][l], "nt", add=dmem_n)
        if l < N_A:
            dc, dlng, dlnb = _ln_silu_bwd(dmix, s["c"], vec(w["ln_g"][l]), vec(w["ln_b"][l]))
            da, dgate, dwdw, dbdw, dba, dbg2 = _conv_glu_bwd(dc, s["v"], s["pa"], s["pg"], vec(w["b_glu_a"][l]),
                                                              vec(w["b_glu_g"][l]), w["w_dw_a"][l])
            g["ln_g"][l], g["ln_b"][l], g["b_dw_a"][l] = one(dlng[0]), one(dlnb[0]), one(dbdw[0])
            g["w_dw_a"][l] = ([dwdw[:CONV_W]], 1)
            g["b_glu"][l] = ([dba[0], dbg2[0]], 0)
            g["w_in_a"][l] = ([_mm(s["h"], da, "tn", BF16), _mm(s["h"], dgate, "tn", BF16), _mm(s["h"], dpq, "tn", BF16)], 1)
            dh = _mm(da, w["w_a"][l], "nt")
            dh = _mm(dgate, w["w_g"][l], "nt", add=dh)
            dh = _mm(dpq, w["w_qm"][l], "nt", add=dh)
        else:
            dq, dk, dv, dcq, dck = _fox_bwd(s["q"], shared["k"], shared["v"], shared["cq"], shared["ck"], _fox_delta(s["o"], dmix), s["lse"], dmix)
            dk_sum = dk if dk_sum is None else dk_sum + dk
            dv_sum = dv if dv_sum is None else dv_sum + dv
            dcq_sum = dcq if dcq_sum is None else dcq_sum + dcq
            dck_sum = dck if dck_sum is None else dck_sum + dck
            g["w_in_b"][l - N_A] = ([_mm(s["h"], dq, "tn", BF16), _mm(s["h"], dpq, "tn", BF16)], 1)
            dh = _mm(dq, w["w_q"][l - N_A], "nt")
            dh = _mm(dpq, w["w_qm"][l], "nt", add=dh)
        dx, dxb, dg_mix = _rms_bwd(s["x_in"], vec(w["g_mix"][l]), dh, dx)
        g["g_mix"][l] = one(dg_mix[0])
        if l == N_A:
            df, dbf = _fgate_bwd(shared["fl"], w["b_f"], _dcum_from_layouts(dcq_sum, dck_sum))
            hk = shared["hk"]
            g["w_kvf"] = [([_mm(hk, dk_sum, "tn", BF16), _mm(hk, dv_sum, "tn", BF16), _mm(hk, df, "tn", BF16)[:, :N_FOX_HEADS]], 1)]
            g["b_f"] = [one(dbf[0, :N_FOX_HEADS])]
            dhk = _mm(dk_sum, w["w_k"], "nt")
            dhk = _mm(dv_sum, w["w_v"], "nt", add=dhk)
            dhk = _mm(df, w["w_f"], "nt", add=dhk)
            dx, dxb, dg_kv = _rms_bwd(shared["x_in"], vec(w["g_kv"]), dhk, dx)
            g["g_kv"] = [one(dg_kv[0])]
        dxb = after_start(emit(l, g, "mixer"), dxb)
    _, _, dg_mem = _rms_bwd(mem, vec(w["g_mem"]), dmem_n, None)
    g["g_mem"] = [one(dg_mem[0])]
    return loss, dx, g


SHARDED = {
    "w_in_a": ((N_A, D_MODEL, 2 * CONV_CH + MEM_W), 2),
    "b_glu": ((N_A, 2 * CONV_CH), 1),
    "w_dw_a": ((N_A, CONV_W, CONV_CH), 2),
    "b_dw_a": ((N_A, CONV_CH), 1),
    "ln_g": ((N_A, CONV_CH), 1),
    "ln_b": ((N_A, CONV_CH), 1),
    "w_kvf": ((D_MODEL, 2 * CONV_CH + N_FOX_HEADS), 1),
    "w_in_b": ((N_LAYERS - N_A, D_MODEL, CONV_CH + MEM_W), 1),
    "w_mem_kv": ((N_LAYERS, D_MODEL, 2 * MEM_W), 1),
    "w_out": ((N_LAYERS, D_MODEL, D_MODEL), 1),
    "w_up": ((N_LAYERS, D_MODEL, 2 * D_FF), 2),
    "w_dw_f": ((N_LAYERS, FFN_CONV_W, 2 * D_FF), 2),
    "w_down": ((N_LAYERS, D_FF, D_MODEL), 1),
}
REPLICATED = {
    "g_mix": (N_LAYERS, D_MODEL), "g_kv": (D_MODEL,), "b_f": (N_FOX_HEADS,), "g_mem": (D_MODEL,),
    "g_ffn": (N_LAYERS, D_MODEL), "b_dw_f": (N_LAYERS, 2 * D_FF), "g_final": (D_MODEL,),
}
WEIGHT_ORDER = ["g_mix", "w_in_a", "b_glu", "w_dw_a", "b_dw_a", "ln_g", "ln_b", "g_kv", "w_kvf", "b_f", "w_in_b", "g_mem",
                "w_mem_kv", "w_out", "g_ffn", "w_up", "w_dw_f", "b_dw_f", "w_down", "g_final"]


def _shard_shape(name):
    shape, axis = SHARDED[name]
    return tuple(d // N_CHIPS if i == axis else d for i, d in enumerate(shape))


def _size(shape):
    n = 1
    for d in shape:
        n *= d
    return n


VECTOR_LEAVES = ("b_glu", "w_dw_a", "b_dw_a", "ln_g", "ln_b", "w_dw_f")
_HALF_ROW_TILE = 16
_VECTOR_ELEMS = sum(_size(_shard_shape(n)) for n in VECTOR_LEAVES)
VECTOR_ROWS = -(-_VECTOR_ELEMS // (FLAT_COLS * 2 * _HALF_ROW_TILE)) * 2 * _HALF_ROW_TILE
REP_ELEMS = sum(_size(s) for s in REPLICATED.values())
REP_ROWS = 8
REP_COLS = -(-REP_ELEMS // (REP_ROWS * LANE)) * LANE


def _flatten_vectors(parts, dtype):
    flat = jnp.concatenate([p.reshape(-1).astype(dtype) for p in parts])
    return jnp.pad(flat, (0, VECTOR_ROWS * FLAT_COLS - _VECTOR_ELEMS)).reshape(2, VECTOR_ROWS // 2, FLAT_COLS)


def _unflatten_vectors(flat, lead=()):
    flat = flat.reshape(lead + (-1,))
    out, off = {}, 0
    for n in VECTOR_LEAVES:
        shp = _shard_shape(n)
        out[n] = flat[..., off:off + _size(shp)].reshape(lead + shp)
        off += _size(shp)
    return out


def _span(parts, cat_axis, sel_axis, lo, hi):
    if sel_axis != cat_axis:
        return jnp.concatenate([lax.slice_in_dim(p, lo, hi, axis=sel_axis) for p in parts], axis=cat_axis)
    taken, off = [], 0
    for p in parts:
        n = p.shape[cat_axis]
        a, b = max(lo, off), min(hi, off + n)
        if a < b:
            taken.append(lax.slice_in_dim(p, a - off, b - off, axis=cat_axis))
        off += n
    return taken[0] if len(taken) == 1 else jnp.concatenate(taken, axis=cat_axis)


FFN_LEAVES = 2
LAYER_LEAVES = tuple(
    ([("w_in_a", l)] if l < N_A else ([("w_kvf", None)] if l == N_A else []) + [("w_in_b", l - N_A)])
    + [("w_mem_kv", l), ("w_out", l), ("w_up", l), ("w_down", l)]
    for l in range(N_LAYERS))


def _layer_halves(weights, leaves):
    out = []
    for name, idx in leaves:
        shard = weights[name] if idx is None else weights[name][idx]
        rows, cols = shard.shape
        assert (rows // 2) % _HALF_ROW_TILE == 0, name
        out.append(shard.astype(BF16).reshape(2, rows // 2, cols))
    return out


def _layer_pieces(l, gathered, vectors):
    parts = {}
    for (name, idx), g in zip(LAYER_LEAVES[l], gathered):
        if g is not None:
            parts[name] = g

    def cut(name, sel_axis, lo, hi):
        g = parts[name]
        axis = SHARDED[name][1] - (len(SHARDED[name][0]) - 2)
        if axis == 0:
            return lax.slice_in_dim(g.reshape(-1, g.shape[-1]), lo, hi, axis=sel_axis)
        return _span([g[t].reshape(-1, g.shape[-1]) for t in range(N_CHIPS)], axis, sel_axis, lo, hi)

    def vcut(name, idx, sel_axis, lo, hi):
        axis = SHARDED[name][1] - 1
        return _span([vectors[name][t, idx] for t in range(N_CHIPS)], axis, sel_axis, lo, hi)

    def pad_rows(arr, rows):
        return jnp.pad(arr, ((0, rows - arr.shape[0]), (0, 0)))

    w = {}
    if "w_up" in parts:
        w["w_up_g"] = cut("w_up", 1, 0, D_FF)
        w["w_up_v"] = cut("w_up", 1, D_FF, 2 * D_FF)
        w["w_dw_f_g"] = pad_rows(vcut("w_dw_f", l, 1, 0, D_FF), FFN_PAD)
        w["w_dw_f_v"] = pad_rows(vcut("w_dw_f", l, 1, D_FF, 2 * D_FF), FFN_PAD)
        w["w_down"] = cut("w_down", 0, 0, D_FF)
    if "w_out" not in parts:
        return w
    if l < N_A:
        w["w_a"] = cut("w_in_a", 1, 0, CONV_CH)
        w["w_g"] = cut("w_in_a", 1, CONV_CH, 2 * CONV_CH)
        w["w_qm"] = cut("w_in_a", 1, 2 * CONV_CH, 2 * CONV_CH + MEM_W)
        w["b_glu_a"] = vcut("b_glu", l, 0, 0, CONV_CH)
        w["b_glu_g"] = vcut("b_glu", l, 0, CONV_CH, 2 * CONV_CH)
        w["w_dw_a"] = pad_rows(vcut("w_dw_a", l, 1, 0, CONV_CH), CONV_PAD)
        for n in ("b_dw_a", "ln_g", "ln_b"):
            w[n] = vcut(n, l, 0, 0, CONV_CH)
    else:
        if l == N_A:
            w["w_k"] = cut("w_kvf", 1, 0, CONV_CH)
            w["w_v"] = cut("w_kvf", 1, CONV_CH, 2 * CONV_CH)
            w["w_f"] = jnp.pad(cut("w_kvf", 1, 2 * CONV_CH, 2 * CONV_CH + N_FOX_HEADS), ((0, 0), (0, F_PAD - N_FOX_HEADS)))
        w["w_q"] = cut("w_in_b", 1, 0, CONV_CH)
        w["w_qm"] = cut("w_in_b", 1, CONV_CH, CONV_CH + MEM_W)
    w["w_mk"] = cut("w_mem_kv", 1, 0, MEM_W)
    w["w_mv"] = cut("w_mem_kv", 1, MEM_W, 2 * MEM_W)
    w["w_o_mix"] = cut("w_out", 0, 0, CONV_CH)
    w["w_o_mem"] = cut("w_out", 0, CONV_CH, D_MODEL)
    return w


_PER_LAYER = ("w_a", "w_g", "w_q", "w_qm", "b_glu_a", "b_glu_g", "w_dw_a", "b_dw_a", "ln_g", "ln_b", "w_mk", "w_mv", "w_o_mix", "w_o_mem",
              "w_up_g", "w_up_v", "w_dw_f_g", "w_dw_f_v", "w_down")


def _weight_table(rep):
    w = dict(rep)
    w["b_f"] = jnp.pad(rep["b_f"], (0, F_PAD - N_FOX_HEADS)).reshape(1, F_PAD)
    w["b_dw_f_g"], w["b_dw_f_v"] = rep["b_dw_f"][:, :D_FF], rep["b_dw_f"][:, D_FF:]
    for n in _PER_LAYER:
        w[n] = {}
    return w


def _install_layer(w, l, pieces):
    for n, p in pieces.items():
        if n in _PER_LAYER:
            w[n][l - N_A if n == "w_q" else l] = p
        else:
            w[n] = p


def _shard_of(entry, name, t):
    shape, axis = SHARDED[name]
    pieces, cat_axis = entry
    axis -= len(shape) - len(pieces[0].shape)
    step = shape[SHARDED[name][1]] // N_CHIPS
    return _span(pieces, cat_axis, axis, t * step, (t + 1) * step)


def _layer_slabs(grads, leaves):
    out = []
    for name, idx in leaves:
        pieces, cat_axis = grads[name][0 if idx is None else idx]
        if cat_axis == 0 and SHARDED[name][1] - (len(SHARDED[name][0]) - 2) == 0:
            whole = pieces[0] if len(pieces) == 1 else jnp.concatenate(pieces, axis=0)
            out.append(whole.astype(BF16).reshape(N_CHIPS, 2, whole.shape[0] // (2 * N_CHIPS), whole.shape[1]))
            continue
        shards = [_shard_of((pieces, cat_axis), name, t).astype(BF16) for t in range(N_CHIPS)]
        rows, cols = shards[0].shape
        out.append(jnp.stack(shards).reshape(N_CHIPS, 2, rows // 2, cols))
    return out


def _vector_slabs(grads):
    def shard(name, t):
        return jnp.stack([_shard_of(entry, name, t) for entry in grads[name]])
    return jnp.stack([_flatten_vectors([shard(n, t) for n in VECTOR_LEAVES], BF16) for t in range(N_CHIPS)])


def _replicated_rows(grads):
    parts = [p.reshape(-1) for n in REPLICATED for pieces, _ in grads[n] for p in pieces]
    flat = jnp.concatenate(parts)
    return jnp.pad(flat, (0, REP_ROWS * REP_COLS - REP_ELEMS)).reshape(REP_ROWS, REP_COLS)


_ANY = pl.BlockSpec(memory_space=pl.ANY)


def _place():
    x, y, c = lax.axis_index("x"), lax.axis_index("y"), lax.axis_index("c")
    chips = [(1 - x, y), (x, 1 - y), (1 - x, 1 - y)]
    return x, y, c, chips


def _chip_index(chip):
    return 2 * chip[0] + chip[1]


def _gather_leaves(halves):
    n = len(halves)

    def body(*refs):
        w_refs, out_refs, (send_sems, recv_sems) = refs[:n], refs[n:2 * n], refs[2 * n:]
        x, y, c, chips = _place()
        me = 2 * x + y
        sibling = (x, y, 1 - c)

        def copy(a, sem, chip_idx, half, to, src=None):
            dst = out_refs[a].at[chip_idx, half]
            return pltpu.make_async_remote_copy(src_ref=dst if src is None else src, dst_ref=dst, send_sem=send_sems.at[6 * a + sem],
                                                recv_sem=recv_sems.at[6 * a + sem], device_id=to, device_id_type=MESH_ID)

        first = [copy(a, j, me, c, (*chip, c), src=w_refs[a].at[c]) for j, chip in enumerate(chips) for a in range(n)]
        for cp in first:
            cp.start()
        passed = []
        for j, chip in enumerate(chips):
            for a in range(n):
                copy(a, j, _chip_index(chip), c, (x, y, c)).wait_recv()
                passed.append(copy(a, 3 + j, _chip_index(chip), c, sibling))
                passed[-1].start()
        for j, chip in enumerate(chips):
            for a in range(n):
                copy(a, 3 + j, _chip_index(chip), 1 - c, (x, y, c)).wait_recv()
        for cp in first + passed:
            cp.wait_send()

    return pl.pallas_call(
        body, out_shape=[jax.ShapeDtypeStruct((N_CHIPS,) + h.shape, h.dtype) for h in halves],
        in_specs=[_ANY] * n, out_specs=[_ANY] * n,
        scratch_shapes=[pltpu.SemaphoreType.DMA((6 * n,)), pltpu.SemaphoreType.DMA((6 * n,))], name="gather_leaves",
    )(*halves)


_HBM = pl.BlockSpec(memory_space=pltpu.HBM)
_SEM = pl.BlockSpec(memory_space=pltpu.SEMAPHORE)


def _in_hbm(a):
    return pltpu.with_memory_space_constraint(a, pltpu.HBM)


def _gather_start(groups, after):
    flat = [h for g in groups for h in g]
    n, ng = len(flat), len(groups)

    def body(*refs):
        srcs, lands, sems = refs[1:1 + n], refs[1 + n:1 + 2 * n], refs[1 + 2 * n:1 + 2 * n + 2 * ng]
        token = refs[-1]
        x, y, c, chips = _place()
        me = 2 * x + y
        a = 0
        for gi, g in enumerate(groups):
            for k in range(len(g)):
                for j, chip in enumerate(chips):
                    pltpu.make_async_remote_copy(src_ref=srcs[a].at[c], dst_ref=lands[a].at[me, c], send_sem=sems[2 * gi].at[3 * k + j],
                                                 recv_sem=sems[2 * gi + 1].at[3 * k + j], device_id=(*chip, c), device_id_type=MESH_ID).start()
                a += 1
        token[...] = jnp.zeros_like(token)

    sem_shapes = [pltpu.SemaphoreType.DMA((3 * len(g),)) for g in groups for _ in range(2)]
    out = pl.pallas_call(
        body, name="gather_start",
        out_shape=sem_shapes + [pltpu.HBM(h.shape, h.dtype) for h in flat] + [pltpu.HBM((N_CHIPS,) + h.shape, h.dtype) for h in flat]
        + [jax.ShapeDtypeStruct((8, LANE), F32)],
        in_specs=[_ANY] + [_HBM] * (2 * n), out_specs=[_SEM] * (2 * ng) + [_HBM] * (2 * n) + [pl.BlockSpec(memory_space=pltpu.VMEM)],
        input_output_aliases={1 + i: 2 * ng + i for i in range(2 * n)},
        compiler_params=pltpu.CompilerParams(has_side_effects=pltpu.SideEffectType.DATAFLOW_SIDE_EFFECTING),
    )(after, *[_in_hbm(h) for h in flat], *[_in_hbm(lax.empty((N_CHIPS,) + h.shape, h.dtype)) for h in flat])
    sems, srcs, lands = out[:2 * ng], out[2 * ng:2 * ng + n], out[2 * ng + n:2 * ng + 2 * n]
    res, a = [], 0
    for gi, g in enumerate(groups):
        res.append((sems[2 * gi], sems[2 * gi + 1], list(srcs[a:a + len(g)]), list(lands[a:a + len(g)])))
        a += len(g)
    return res


def _gather_wait(send_sems, recv_sems, srcs, lands, after, name):
    n = len(srcs)

    def body(*refs):
        src_refs, land_refs, send, recv = refs[:n], refs[n:2 * n], refs[2 * n], refs[2 * n + 1]
        x, y, c, chips = _place()
        for k in range(n):
            for j, chip in enumerate(chips):
                cp = pltpu.make_async_remote_copy(src_ref=src_refs[k].at[c], dst_ref=land_refs[k].at[_chip_index(chip), c], send_sem=send.at[3 * k + j],
                                                  recv_sem=recv.at[3 * k + j], device_id=(*chip, c), device_id_type=MESH_ID)
                cp.wait_send()
                cp.wait_recv()

    out = pl.pallas_call(
        body, name=name, out_shape=[pltpu.HBM(a.shape, a.dtype) for a in list(srcs) + list(lands)],
        in_specs=[_HBM] * (2 * n) + [_SEM, _SEM, _ANY], out_specs=[_HBM] * (2 * n), input_output_aliases={i: i for i in range(2 * n)},
        compiler_params=pltpu.CompilerParams(has_side_effects=pltpu.SideEffectType.DATAFLOW_SIDE_EFFECTING),
    )(*srcs, *lands, send_sems, recv_sems, after)
    return list(out[n:])


def _forward_halves(lands, name):
    n = len(lands)

    def body(*refs):
        out_refs, (send_sems, recv_sems) = refs[n:2 * n], refs[2 * n:]
        x, y, c, chips = _place()

        def copy(a, j, half, to):
            blk = out_refs[a].at[_chip_index(chips[j]), half]
            return pltpu.make_async_remote_copy(src_ref=blk, dst_ref=blk, send_sem=send_sems.at[3 * a + j], recv_sem=recv_sems.at[3 * a + j],
                                                device_id=to, device_id_type=MESH_ID)

        sends = [copy(a, j, c, (x, y, 1 - c)) for a in range(n) for j in range(3)]
        for cp in sends:
            cp.start()
        for a in range(n):
            for j in range(3):
                copy(a, j, 1 - c, (x, y, c)).wait_recv()
        for cp in sends:
            cp.wait_send()

    return pl.pallas_call(
        body, out_shape=[jax.ShapeDtypeStruct(a.shape, a.dtype) for a in lands], in_specs=[_ANY] * n, out_specs=[_ANY] * n,
        input_output_aliases={i: i for i in range(n)},
        scratch_shapes=[pltpu.SemaphoreType.DMA((3 * n,)), pltpu.SemaphoreType.DMA((3 * n,))], name=name,
    )(*lands)


def _swap_halves(slabs):
    n = len(slabs)

    def body(*refs):
        g_refs, got_refs, (send_sem, recv_sem) = refs[:n], refs[n:2 * n], refs[2 * n:]
        x, y, c, _ = _place()
        copies = [pltpu.make_async_remote_copy(src_ref=g_refs[a].at[t, 1 - c], dst_ref=got_refs[a].at[t], send_sem=send_sem.at[N_CHIPS * a + t],
                                               recv_sem=recv_sem.at[N_CHIPS * a + t], device_id=(x, y, 1 - c), device_id_type=MESH_ID)
                  for a in range(n) for t in range(N_CHIPS)]
        for cp in copies:
            cp.start()
        for cp in copies:
            cp.wait()

    return pl.pallas_call(
        body, out_shape=[jax.ShapeDtypeStruct((g.shape[0],) + g.shape[2:], g.dtype) for g in slabs],
        in_specs=[_ANY] * n, out_specs=[_ANY] * n,
        scratch_shapes=[pltpu.SemaphoreType.DMA((N_CHIPS * n,)), pltpu.SemaphoreType.DMA((N_CHIPS * n,))], name="swap_halves",
    )(*slabs)


def _scatter_start(parts, name):
    n = len(parts)

    def body(*refs):
        srcs, lands, send, recv = refs[:n], refs[n:2 * n], refs[2 * n], refs[2 * n + 1]
        token = refs[-1]
        x, y, c, chips = _place()
        me = 2 * x + y
        for k in range(n):
            for j, chip in enumerate(chips):
                pltpu.make_async_remote_copy(src_ref=srcs[k].at[_chip_index(chip)], dst_ref=lands[k].at[me], send_sem=send.at[3 * k + j],
                                             recv_sem=recv.at[3 * k + j], device_id=(*chip, c), device_id_type=MESH_ID).start()
        token[...] = jnp.zeros_like(token)

    out = pl.pallas_call(
        body, name=name,
        out_shape=[pltpu.SemaphoreType.DMA((3 * n,)), pltpu.SemaphoreType.DMA((3 * n,))] + [pltpu.HBM(p.shape, p.dtype) for p in parts] * 2
        + [jax.ShapeDtypeStruct((8, LANE), F32)],
        in_specs=[_HBM] * (2 * n), out_specs=[_SEM, _SEM] + [_HBM] * (2 * n) + [pl.BlockSpec(memory_space=pltpu.VMEM)],
        input_output_aliases={i: 2 + i for i in range(2 * n)},
        compiler_params=pltpu.CompilerParams(has_side_effects=pltpu.SideEffectType.DATAFLOW_SIDE_EFFECTING),
    )(*[_in_hbm(p) for p in parts], *[_in_hbm(lax.empty(p.shape, p.dtype)) for p in parts])
    return (out[0], out[1], list(out[2:2 + n]), list(out[2 + n:2 + 2 * n])), out[-1]


def _scatter_wait(send_sems, recv_sems, srcs, lands, after, name):
    n = len(srcs)

    def body(*refs):
        src_refs, land_refs, send, recv = refs[:n], refs[n:2 * n], refs[2 * n], refs[2 * n + 1]
        x, y, c, chips = _place()
        for k in range(n):
            for j, chip in enumerate(chips):
                cp = pltpu.make_async_remote_copy(src_ref=src_refs[k].at[_chip_index(chip)], dst_ref=land_refs[k].at[_chip_index(chip)],
                                                  send_sem=send.at[3 * k + j], recv_sem=recv.at[3 * k + j], device_id=(*chip, c), device_id_type=MESH_ID)
                cp.wait_send()
                cp.wait_recv()

    out = pl.pallas_call(
        body, name=name, out_shape=[pltpu.HBM(a.shape, a.dtype) for a in list(srcs) + list(lands)],
        in_specs=[_HBM] * (2 * n) + [_SEM, _SEM, _ANY], out_specs=[_HBM] * (2 * n), input_output_aliases={i: i for i in range(2 * n)},
        compiler_params=pltpu.CompilerParams(has_side_effects=pltpu.SideEffectType.DATAFLOW_SIDE_EFFECTING),
    )(*srcs, *lands, send_sems, recv_sems, after)
    return list(out[n:])


def _sibling_halves(halves):
    n = len(halves)

    def body(*refs):
        h_refs, got_refs, (send_sem, recv_sem) = refs[:n], refs[n:2 * n], refs[2 * n:]
        x, y, c, _ = _place()
        copies = [pltpu.make_async_remote_copy(src_ref=h_refs[a], dst_ref=got_refs[a], send_sem=send_sem.at[a], recv_sem=recv_sem.at[a],
                                               device_id=(x, y, 1 - c), device_id_type=MESH_ID) for a in range(n)]
        for cp in copies:
            cp.start()
        for cp in copies:
            cp.wait()

    return pl.pallas_call(
        body, out_shape=[jax.ShapeDtypeStruct(h.shape, h.dtype) for h in halves], in_specs=[_ANY] * n, out_specs=[_ANY] * n,
        scratch_shapes=[pltpu.SemaphoreType.DMA((n,)), pltpu.SemaphoreType.DMA((n,))], name="sibling_halves",
    )(*halves)


def _gather_replicated(rows):
    m_per, n = rows.shape

    def body(x_ref, out_ref, send_sems, recv_sems, local_sem):
        x, y, c, chips = _place()
        me, sibling = (x, y, c), (x, y, 1 - c)

        def block(px, py, pc):
            return out_ref.at[pl.ds((4 * px + 2 * py + pc) * m_per, m_per), :]

        def copy(k, blk, to, src=None):
            return pltpu.make_async_remote_copy(src_ref=block(*blk) if src is None else src, dst_ref=block(*blk),
                                                send_sem=send_sems.at[k], recv_sem=recv_sems.at[k], device_id=to, device_id_type=MESH_ID)

        mine = pltpu.make_async_copy(x_ref, block(*me), local_sem)
        mine.start()
        first = [copy(0, me, sibling, src=x_ref)]
        first += [copy(1 + j, me, (*chip, c), src=x_ref) for j, chip in enumerate(chips)]
        for cp in first:
            cp.start()
        passed = [copy(4 + j, (*chip, c), sibling) for j, chip in enumerate(chips)]
        for j, chip in enumerate(chips):
            copy(1 + j, (*chip, c), me).wait_recv()
            passed[j].start()
        copy(0, sibling, me).wait_recv()
        for j, chip in enumerate(chips):
            copy(4 + j, (*chip, 1 - c), me).wait_recv()
        for cp in first + passed:
            cp.wait_send()
        mine.wait()

    vmem = pl.BlockSpec(memory_space=pltpu.VMEM)
    return pl.pallas_call(
        body, out_shape=jax.ShapeDtypeStruct((N_DEV * m_per, n), rows.dtype), in_specs=[vmem], out_specs=vmem,
        scratch_shapes=[pltpu.SemaphoreType.DMA((7,)), pltpu.SemaphoreType.DMA((7,)), pltpu.SemaphoreType.DMA],
        name="gather_replicated",
    )(rows)


_SUM_ROWS = 256


def _tile_rows(rows):
    if rows <= _SUM_ROWS:
        return rows
    best = None
    for r in range(16, _SUM_ROWS + 1, 16):
        if rows % r == 0:
            best = r
    assert best is not None, rows
    return best


def _add_pairs(slabs, got, core):
    n, _, rows, cols = slabs.shape
    tr = _tile_rows(rows)

    def body(core_ref, a_ref, b_ref, o_ref):
        o_ref[...] = (a_ref[0].astype(F32) + b_ref[...].astype(F32)).astype(BF16)

    spec = pl.BlockSpec((1, tr, cols), lambda t, i, core_ref: (t, i, 0))
    return pl.pallas_call(
        body, out_shape=jax.ShapeDtypeStruct((n, rows, cols), BF16),
        grid_spec=pltpu.PrefetchScalarGridSpec(
            num_scalar_prefetch=1, grid=(n, rows // tr),
            in_specs=[pl.BlockSpec((1, 1, tr, cols), lambda t, i, core_ref: (t, core_ref[0], i, 0)), spec], out_specs=spec),
        compiler_params=_params(("parallel", "parallel")), name=f"add_pairs_{rows}x{cols}",
    )(core, slabs, got)


def _sum_slabs(slabs):
    n, rows, cols = slabs.shape
    tr = _tile_rows(rows)

    def body(s_ref, o_ref):
        acc = s_ref[0].astype(F32)
        for t in range(1, n):
            acc = acc + s_ref[t].astype(F32)
        o_ref[...] = acc

    return pl.pallas_call(
        body, out_shape=jax.ShapeDtypeStruct((rows, cols), F32), grid=(rows // tr,),
        in_specs=[pl.BlockSpec((n, tr, cols), lambda i: (0, i, 0))], out_specs=pl.BlockSpec((tr, cols), lambda i: (i, 0)),
        compiler_params=_params(("parallel",)), name=f"sum_slabs_{n}x{rows}x{cols}",
    )(slabs)


_ADAM_BLOCK_BYTES = 2 * 1024 * 1024


def _adamw(w, g, m, v):
    shape = w.shape
    cols = shape[-1]
    rows = _size(shape) // cols
    tr = rows
    if rows * cols * 4 > _ADAM_BLOCK_BYTES:
        for r in range(8, rows, 8):
            if rows % r == 0 and r * cols * 4 <= _ADAM_BLOCK_BYTES:
                tr = r

    def body(w_ref, g_ref, m_ref, v_ref, d_ref, nm_ref, nv_ref):
        gv = g_ref[...]
        nm = ADAM_B1 * m_ref[...] + (1.0 - ADAM_B1) * gv
        nv = ADAM_B2 * v_ref[...] + (1.0 - ADAM_B2) * jnp.square(gv)
        m_hat = nm / (1.0 - ADAM_B1 ** ADAM_STEP)
        v_hat = nv / (1.0 - ADAM_B2 ** ADAM_STEP)
        d_ref[...] = -ADAM_LR * (m_hat / (jnp.sqrt(v_hat) + ADAM_EPS) + ADAM_WD * w_ref[...])
        nm_ref[...] = nm
        nv_ref[...] = nv

    spec = pl.BlockSpec((tr, cols), lambda i: (i, 0))
    out = pl.pallas_call(
        body, out_shape=tuple(jax.ShapeDtypeStruct((rows, cols), F32) for _ in range(3)), grid=(rows // tr,),
        in_specs=[spec] * 4, out_specs=(spec,) * 3, compiler_params=_params(("parallel",)), name=f"adamw_{rows}x{cols}",
    )(*(a.reshape(rows, cols) for a in (w, g, m, v)))
    return tuple(o.reshape(shape) for o in out)


def kernel(x, mem, g_mix, w_in_a, b_glu, w_dw_a, b_dw_a, ln_g, ln_b, g_kv, w_kvf, b_f, w_in_b, g_mem, w_mem_kv, w_out, g_ffn, w_up, w_dw_f, b_dw_f, w_down, g_final, loss_target, m_g_mix, m_w_in_a, m_b_glu, m_w_dw_a, m_b_dw_a, m_ln_g, m_ln_b, m_g_kv, m_w_kvf, m_b_f, m_w_in_b, m_g_mem, m_w_mem_kv, m_w_out, m_g_ffn, m_w_up, m_w_dw_f, m_b_dw_f, m_w_down, m_g_final, v_g_mix, v_w_in_a, v_b_glu, v_w_dw_a, v_b_dw_a, v_ln_g, v_ln_b, v_g_kv, v_w_kvf, v_b_f, v_w_in_b, v_g_mem, v_w_mem_kv, v_w_out, v_g_ffn, v_w_up, v_w_dw_f, v_b_dw_f, v_w_down, v_g_final):
    weights = dict(g_mix=g_mix, w_in_a=w_in_a, b_glu=b_glu, w_dw_a=w_dw_a, b_dw_a=b_dw_a, ln_g=ln_g, ln_b=ln_b, g_kv=g_kv,
                   w_kvf=w_kvf, b_f=b_f, w_in_b=w_in_b, g_mem=g_mem, w_mem_kv=w_mem_kv, w_out=w_out, g_ffn=g_ffn, w_up=w_up,
                   w_dw_f=w_dw_f, b_dw_f=b_dw_f, w_down=w_down, g_final=g_final)
    mom1 = dict(g_mix=m_g_mix, w_in_a=m_w_in_a, b_glu=m_b_glu, w_dw_a=m_w_dw_a, b_dw_a=m_b_dw_a, ln_g=m_ln_g, ln_b=m_ln_b,
                g_kv=m_g_kv, w_kvf=m_w_kvf, b_f=m_b_f, w_in_b=m_w_in_b, g_mem=m_g_mem, w_mem_kv=m_w_mem_kv, w_out=m_w_out,
                g_ffn=m_g_ffn, w_up=m_w_up, w_dw_f=m_w_dw_f, b_dw_f=m_b_dw_f, w_down=m_w_down, g_final=m_g_final)
    mom2 = dict(g_mix=v_g_mix, w_in_a=v_w_in_a, b_glu=v_b_glu, w_dw_a=v_w_dw_a, b_dw_a=v_b_dw_a, ln_g=v_ln_g, ln_b=v_ln_b,
                g_kv=v_g_kv, w_kvf=v_w_kvf, b_f=v_b_f, w_in_b=v_w_in_b, g_mem=v_g_mem, w_mem_kv=v_w_mem_kv, w_out=v_w_out,
                g_ffn=v_g_ffn, w_up=v_w_up, w_dw_f=v_w_dw_f, b_dw_f=v_b_dw_f, w_down=v_w_down, g_final=v_g_final)

    x_pos, y_pos, core = lax.axis_index("x"), lax.axis_index("y"), lax.axis_index("c")
    chip = 2 * x_pos + y_pos

    mixer_of = lambda l: LAYER_LEAVES[l][:-FFN_LEAVES]
    ffn_of = lambda l: LAYER_LEAVES[l][-FFN_LEAVES:]
    local_first = _layer_halves(weights, mixer_of(0))
    vector_halves = _flatten_vectors([weights[n] for n in VECTOR_LEAVES], F32)
    first = _gather_leaves(local_first + [vector_halves])
    later = [ffn_of(0)] + [LAYER_LEAVES[l] for l in range(1, N_LAYERS)]
    local_later = [_layer_halves(weights, leaves) for leaves in later]
    in_flight = _gather_start(local_later, first[-1])

    def with_own(gathered, own):
        return [lax.dynamic_update_index_in_dim(g, o[None], chip, 0) for g, o in zip(gathered, own)]

    first = with_own(first, local_first + [vector_halves])
    vectors = _unflatten_vectors(first[-1], (N_CHIPS,))
    table = _weight_table({n: weights[n] for n in REPLICATED})

    def fetch(k, after):
        send_sems, recv_sems, srcs, lands = in_flight[k]
        lands = _gather_wait(send_sems, recv_sems, srcs, lands, after, f"gather_wait_{k}")
        return with_own(_forward_halves(lands, f"forward_halves_{k}"), local_later[k])

    def arrive(l, after, part):
        if l == 0:
            gathered = first[:-1] + [None] * FFN_LEAVES if part == "mixer" else [None] * len(mixer_of(0)) + fetch(0, after)
        elif part == "mixer":
            gathered = fetch(l, after)
        else:
            return
        _install_layer(table, l, _layer_pieces(l, gathered, vectors))

    core_index = core.astype(jnp.int32).reshape(1)
    in_flight_back = []

    def emit(l, g, part):
        if l == 0:
            leaves = ffn_of(0) if part == "ffn" else mixer_of(0)
        elif part == "mixer":
            leaves = LAYER_LEAVES[l]
        else:
            return None
        with_vectors = l == 0 and part == "mixer"
        slabs = _layer_slabs(g, leaves) + ([_vector_slabs(g)] if with_vectors else [])
        chip_sums = [_add_pairs(mine, got, core_index) for mine, got in zip(slabs, _swap_halves(slabs))]
        handles, started = _scatter_start(chip_sums, f"scatter_start_{len(in_flight_back)}")
        in_flight_back.append((leaves, with_vectors, chip_sums, handles))
        return started

    loss, dx, grads = _local_step(x[0], mem[0], loss_target[0], table, arrive, emit)
    loss = lax.psum(loss, ("x", "y", "c"))

    names, mine = [], []
    for k, (leaves, with_vectors, chip_sums, (send_sems, recv_sems, srcs, lands)) in enumerate(in_flight_back):
        lands = _scatter_wait(send_sems, recv_sems, srcs, lands, dx, f"scatter_wait_{k}")
        from_chips = [lax.dynamic_update_index_in_dim(got, lax.dynamic_index_in_dim(own, chip, 0, keepdims=True), chip, 0)
                      for got, own in zip(lands, chip_sums)]
        mine += [_sum_slabs(f) for f in from_chips]
        names += list(leaves) + ([("vectors", None)] if with_vectors else [])
    theirs = _sibling_halves(mine)
    per_leaf = {}
    for (name, idx), m, t in zip(names, mine, theirs):
        per_leaf.setdefault(name, {})[idx] = jnp.concatenate([jnp.where(core == 0, m, t), jnp.where(core == 0, t, m)])
    vector_grads = per_leaf.pop("vectors")[None]
    grad_leaves = {n: (d[None] if None in d else jnp.stack([d[i] for i in sorted(d)])) for n, d in per_leaf.items()}
    grad_leaves.update(_unflatten_vectors(vector_grads))

    rep_sum = _sum_slabs(_gather_replicated(_replicated_rows(grads)).reshape(N_DEV, REP_ROWS, REP_COLS)).reshape(-1)
    off = 0
    for n, shp in REPLICATED.items():
        grad_leaves[n] = rep_sum[off:off + _size(shp)].reshape(shp)
        off += _size(shp)

    deltas, new_m, new_v = {}, {}, {}
    for n in WEIGHT_ORDER:
        deltas[n], new_m[n], new_v[n] = _adamw(weights[n], grad_leaves[n], mom1[n], mom2[n])
    return (loss, dx[None], *[grad_leaves[n] for n in WEIGHT_ORDER], *[deltas[n] for n in WEIGHT_ORDER],
            *[new_m[n] for n in WEIGHT_ORDER], *[new_v[n] for n in WEIGHT_ORDER])
```

```python
import jax
import jax.numpy as jnp
from jax import lax
from jax.experimental import pallas as pl
from jax.experimental.pallas import tpu as pltpu

F32 = jnp.float32
BF16 = jnp.bfloat16

D_MODEL = 1024
N_LAYERS = 4
N_A = 2
CONV_CH = 768
MEM_W = 256
HEAD_DIM = 64
N_MEM_HEADS = 4
N_FOX_HEADS = 12
N_HEAD_PAIRS = N_FOX_HEADS // 2
D_FF = 2816
CONV_W = 31
CONV_PAD = 32
FFN_CONV_W = 3
FFN_PAD = 8
F_PAD = 128
RMS_EPS = 1e-6
LN_EPS = 1e-5
ATT_SCALE = HEAD_DIM ** -0.5
NEG_BIG = -1e30

ADAM_LR = 0.001
ADAM_B1 = 0.9
ADAM_B2 = 0.999
ADAM_EPS = 1e-08
ADAM_WD = 0.01
ADAM_STEP = 10

LANE = 128
ROW_TILE = 256
CHUNK = 128
VMEM_LIMIT = 48 * 1024 * 1024
FLAT_COLS = 1024
N_CHIPS = 4
N_DEV = 8
MESH_ID = pl.DeviceIdType.MESH


def _params(sem=None):
    return pltpu.CompilerParams(dimension_semantics=sem, vmem_limit_bytes=VMEM_LIMIT)


def _tile(dim, pref):
    if dim <= pref:
        return dim
    best = None
    for m in range(1, dim // LANE + 1):
        d = m * LANE
        if dim % d == 0 and d <= pref:
            best = d
    assert best is not None, dim
    return best


_DIMS = {"nn": (((1,), (0,)), ((), ())), "nt": (((1,), (1,)), ((), ())), "tn": (((0,), (0,)), ((), ()))}


_MM_RESIDENT_BYTES = 8 * 1024 * 1024
_MM_STREAM_BYTES = 6 * 1024 * 1024
_MM_OUT_BYTES = 6 * 1024 * 1024


def _mm_tiles(m, n, k, mode, a_size, b_size, o_size):
    if mode == "tn":
        tm = _tile(m, _MM_RESIDENT_BYTES // (k * a_size))
        tn = _tile(n, min(_MM_STREAM_BYTES // (k * b_size), _MM_OUT_BYTES // (tm * o_size)))
    else:
        tn = _tile(n, _MM_RESIDENT_BYTES // (k * b_size))
        tm = _tile(m, min(_MM_STREAM_BYTES // (k * a_size), _MM_OUT_BYTES // (tn * o_size), 512))
    return tm, tn


def _mm(a, b, mode="nn", out_dtype=F32, add=None):
    if mode == "nn":
        (m, k), (k2, n) = a.shape, b.shape
    elif mode == "nt":
        (m, k), (n, k2) = a.shape, b.shape
    else:
        (k, m), (k2, n) = a.shape, b.shape
    assert k == k2, (a.shape, b.shape, mode)
    tm, tn = _mm_tiles(m, n, k, mode, a.dtype.itemsize, b.dtype.itemsize, jnp.dtype(out_dtype).itemsize)
    dims = _DIMS[mode]
    has_add = add is not None

    def body(*refs):
        if has_add:
            a_ref, b_ref, add_ref, o_ref = refs
        else:
            a_ref, b_ref, o_ref = refs
        r = lax.dot_general(a_ref[...].astype(BF16), b_ref[...].astype(BF16), dims, preferred_element_type=F32)
        if has_add:
            r = r + add_ref[...]
        o_ref[...] = r.astype(out_dtype)

    a_spec = pl.BlockSpec((k, tm), lambda i, j: (0, i)) if mode == "tn" else pl.BlockSpec((tm, k), lambda i, j: (i, 0))
    b_spec = pl.BlockSpec((tn, k), lambda i, j: (j, 0)) if mode == "nt" else pl.BlockSpec((k, tn), lambda i, j: (0, j))
    o_spec = pl.BlockSpec((tm, tn), lambda i, j: (i, j))
    in_specs = [a_spec, b_spec] + ([o_spec] if has_add else [])
    args = (a, b) + ((add,) if has_add else ())
    return pl.pallas_call(
        body,
        out_shape=jax.ShapeDtypeStruct((m, n), out_dtype),
        grid=(m // tm, n // tn),
        in_specs=in_specs,
        out_specs=o_spec,
        compiler_params=_params(("parallel", "parallel")),
        name=f"mm_{mode}_{m}x{k}x{n}",
    )(*args)


def _row(width):
    return pl.BlockSpec((ROW_TILE, width), lambda i: (i, 0))


def _vec(width):
    return pl.BlockSpec((1, width), lambda i: (0, 0))


def _rms_fwd(x, g):
    rows, d = x.shape

    def body(x_ref, g_ref, o_ref):
        xv = x_ref[...]
        rstd = lax.rsqrt(jnp.mean(xv * xv, axis=-1, keepdims=True) + RMS_EPS)
        o_ref[...] = (xv * rstd * g_ref[...]).astype(BF16)

    return pl.pallas_call(
        body, out_shape=jax.ShapeDtypeStruct((rows, d), BF16), grid=(rows // ROW_TILE,),
        in_specs=[_row(d), _vec(d)], out_specs=_row(d), compiler_params=_params(("parallel",)), name=f"rms_fwd_{rows}",
    )(x, g)


def _accumulate(ref, value, step):
    @pl.when(step == 0)
    def _():
        ref[...] = value

    @pl.when(step > 0)
    def _():
        ref[...] += value


def _rms_bwd(x, g, dh, dres):
    rows, d = x.shape
    has_res = dres is not None

    def body(*refs):
        if has_res:
            x_ref, g_ref, dh_ref, dres_ref, dx_ref, dxb_ref, dg_ref = refs
        else:
            x_ref, g_ref, dh_ref, dx_ref, dxb_ref, dg_ref = refs
        xv = x_ref[...]
        dhv = dh_ref[...]
        rstd = lax.rsqrt(jnp.mean(xv * xv, axis=-1, keepdims=True) + RMS_EPS)
        xhat = xv * rstd
        gd = dhv * g_ref[...]
        dx = rstd * (gd - xhat * jnp.mean(gd * xhat, axis=-1, keepdims=True))
        if has_res:
            dx = dx + dres_ref[...]
        dx_ref[...] = dx
        dxb_ref[...] = dx.astype(BF16)
        _accumulate(dg_ref, jnp.sum(dhv * xhat, axis=0, keepdims=True), pl.program_id(0))

    in_specs = [_row(d), _vec(d), _row(d)] + ([_row(d)] if has_res else [])
    args = (x, g, dh) + ((dres,) if has_res else ())
    return pl.pallas_call(
        body,
        out_shape=(jax.ShapeDtypeStruct((rows, d), F32), jax.ShapeDtypeStruct((rows, d), BF16), jax.ShapeDtypeStruct((1, d), F32)),
        grid=(rows // ROW_TILE,), in_specs=in_specs, out_specs=(_row(d), _row(d), _vec(d)),
        compiler_params=_params(("arbitrary",)), name=f"rms_bwd_{rows}_{int(has_res)}",
    )(*args)


def _loss_head(x, g, target):
    rows, d = x.shape

    def body(x_ref, g_ref, t_ref, loss_ref, dx_ref, dxb_ref, dg_ref):
        xv = x_ref[...]
        gv = g_ref[...]
        rstd = lax.rsqrt(jnp.mean(xv * xv, axis=-1, keepdims=True) + RMS_EPS)
        xhat = xv * rstd
        err = xhat * gv - t_ref[...]
        part = 0.5 * jnp.sum(jnp.mean(err * err, axis=-1, keepdims=True), axis=0, keepdims=True)
        dy = err * (1.0 / d)
        gd = dy * gv
        dx = rstd * (gd - xhat * jnp.mean(gd * xhat, axis=-1, keepdims=True))
        dx_ref[...] = dx
        dxb_ref[...] = dx.astype(BF16)
        step = pl.program_id(0)
        _accumulate(loss_ref, jnp.broadcast_to(part, (1, LANE)), step)
        _accumulate(dg_ref, jnp.sum(dy * xhat, axis=0, keepdims=True), step)

    return pl.pallas_call(
        body,
        out_shape=(jax.ShapeDtypeStruct((1, LANE), F32), jax.ShapeDtypeStruct((rows, d), F32),
                   jax.ShapeDtypeStruct((rows, d), BF16), jax.ShapeDtypeStruct((1, d), F32)),
        grid=(rows // ROW_TILE,), in_specs=[_row(d), _vec(d), _row(d)],
        out_specs=(_vec(LANE), _row(d), _row(d), _vec(d)),
        compiler_params=_params(("arbitrary",)), name="loss_head",
    )(x, g, target)


def _sigmoid(x):
    return 1.0 / (1.0 + jnp.exp(-x))


def _ln_silu_fwd(c, ln_g, ln_b):
    rows, ch = c.shape

    def body(c_ref, g_ref, b_ref, o_ref):
        cv = c_ref[...]
        mu = jnp.mean(cv, axis=-1, keepdims=True)
        cen = cv - mu
        rstd = lax.rsqrt(jnp.mean(cen * cen, axis=-1, keepdims=True) + LN_EPS)
        y = cen * rstd * g_ref[...] + b_ref[...]
        o_ref[...] = (y * _sigmoid(y)).astype(BF16)

    return pl.pallas_call(
        body, out_shape=jax.ShapeDtypeStruct((rows, ch), BF16), grid=(rows // ROW_TILE,),
        in_specs=[_row(ch), _vec(ch), _vec(ch)], out_specs=_row(ch), compiler_params=_params(("parallel",)), name="ln_silu_fwd",
    )(c, ln_g, ln_b)


def _ln_silu_bwd(dmix, c, ln_g, ln_b):
    rows, ch = c.shape

    def body(dm_ref, c_ref, g_ref, b_ref, dc_ref, dg_ref, db_ref):
        cv = c_ref[...]
        gv = g_ref[...]
        mu = jnp.mean(cv, axis=-1, keepdims=True)
        cen = cv - mu
        rstd = lax.rsqrt(jnp.mean(cen * cen, axis=-1, keepdims=True) + LN_EPS)
        xhat = cen * rstd
        y = xhat * gv + b_ref[...]
        sg = _sigmoid(y)
        dy = dm_ref[...] * (sg * (1.0 + y * (1.0 - sg)))
        dxh = dy * gv
        dc = rstd * (dxh - jnp.mean(dxh, axis=-1, keepdims=True) - xhat * jnp.mean(dxh * xhat, axis=-1, keepdims=True))
        dc_ref[...] = dc
        step = pl.program_id(0)
        _accumulate(dg_ref, jnp.sum(dy * xhat, axis=0, keepdims=True), step)
        _accumulate(db_ref, jnp.sum(dy, axis=0, keepdims=True), step)

    return pl.pallas_call(
        body,
        out_shape=(jax.ShapeDtypeStruct((rows, ch), F32), jax.ShapeDtypeStruct((1, ch), F32), jax.ShapeDtypeStruct((1, ch), F32)),
        grid=(rows // ROW_TILE,), in_specs=[_row(ch), _row(ch), _vec(ch), _vec(ch)],
        out_specs=(_row(ch), _vec(ch), _vec(ch)), compiler_params=_params(("arbitrary",)), name="ln_silu_bwd",
    )(dmix, c, ln_g, ln_b)


def _col(rows, cb):
    return pl.BlockSpec((rows, cb), lambda j: (0, j))


SUBLANES = 8
_WINDOW = CHUNK + CONV_PAD


def _realign(win, shifted_ref):
    shifted_ref[0] = win
    for r in range(1, SUBLANES):
        shifted_ref[r, 0:_WINDOW - SUBLANES, :] = win[r:r + _WINDOW - SUBLANES, :]


def _tap(shifted_ref, off):
    r = off % SUBLANES
    return shifted_ref[r, off - r:off - r + CHUNK, :]


def _conv_glu_fwd(pa, pg, ba, bg, w_dw, b_dw):
    seq, ch = pa.shape
    cb = LANE
    n_chunks = seq // CHUNK

    def body(pa_ref, pg_ref, ba_ref, bg_ref, w_ref, b_ref, v_ref, c_ref, vpad_ref, vshift_ref):
        vpad_ref[0:CONV_PAD, :] = jnp.zeros((CONV_PAD, cb), F32)

        def glu(r, carry):
            r0 = pl.multiple_of(r * CHUNK, CHUNK)
            rows = pl.ds(r0, CHUNK)
            v = (pa_ref[rows, :] + ba_ref[...]) * _sigmoid(pg_ref[rows, :] + bg_ref[...])
            v_ref[rows, :] = v
            vpad_ref[pl.ds(r0 + CONV_PAD, CHUNK), :] = v
            return carry

        lax.fori_loop(0, n_chunks, glu, 0)

        def conv(r, carry):
            r0 = pl.multiple_of(r * CHUNK, CHUNK)
            _realign(vpad_ref[pl.ds(r0, _WINDOW), :], vshift_ref)
            acc = jnp.broadcast_to(b_ref[...], (CHUNK, cb))
            for t in range(CONV_W):
                acc = acc + w_ref[t:t + 1, :] * _tap(vshift_ref, CONV_PAD - (CONV_W - 1) + t)
            c_ref[pl.ds(r0, CHUNK), :] = acc
            return carry

        lax.fori_loop(0, n_chunks, conv, 0)

    return pl.pallas_call(
        body,
        out_shape=(jax.ShapeDtypeStruct((seq, ch), F32), jax.ShapeDtypeStruct((seq, ch), F32)),
        grid=(ch // cb,),
        in_specs=[_col(seq, cb), _col(seq, cb), _col(1, cb), _col(1, cb), _col(CONV_PAD, cb), _col(1, cb)],
        out_specs=(_col(seq, cb), _col(seq, cb)),
        scratch_shapes=[pltpu.VMEM((seq + CONV_PAD, cb), F32), pltpu.VMEM((SUBLANES, _WINDOW, cb), F32)],
        compiler_params=_params(("parallel",)), name="conv_glu_fwd",
    )(pa, pg, ba, bg, w_dw, b_dw)


def _conv_glu_bwd(dc, v, pa, pg, ba, bg, w_dw):
    seq, ch = dc.shape
    cb = LANE
    n_chunks = seq // CHUNK

    def body(dc_ref, v_ref, pa_ref, pg_ref, ba_ref, bg_ref, w_ref, da_ref, dg_ref, dw_ref, dbdw_ref, dba_ref, dbg_ref,
             dcpad_ref, vpad_ref, dcshift_ref, vshift_ref):
        vpad_ref[0:CONV_PAD, :] = jnp.zeros((CONV_PAD, cb), F32)
        dcpad_ref[seq:seq + CONV_PAD, :] = jnp.zeros((CONV_PAD, cb), F32)
        dw_ref[...] = jnp.zeros((CONV_PAD, cb), F32)
        dbdw_ref[...] = jnp.zeros((1, cb), F32)
        dba_ref[...] = jnp.zeros((1, cb), F32)
        dbg_ref[...] = jnp.zeros((1, cb), F32)

        def fill(r, carry):
            r0 = pl.multiple_of(r * CHUNK, CHUNK)
            vpad_ref[pl.ds(r0 + CONV_PAD, CHUNK), :] = v_ref[pl.ds(r0, CHUNK), :]
            dcpad_ref[pl.ds(r0, CHUNK), :] = dc_ref[pl.ds(r0, CHUNK), :]
            return carry

        lax.fori_loop(0, n_chunks, fill, 0)

        def step(r, carry):
            r0 = pl.multiple_of(r * CHUNK, CHUNK)
            rows = pl.ds(r0, CHUNK)
            _realign(dcpad_ref[pl.ds(r0, _WINDOW), :], dcshift_ref)
            _realign(vpad_ref[pl.ds(r0, _WINDOW), :], vshift_ref)
            dcur = _tap(dcshift_ref, 0)
            dv = jnp.zeros((CHUNK, cb), F32)
            for t in range(CONV_W):
                dv = dv + w_ref[t:t + 1, :] * _tap(dcshift_ref, CONV_W - 1 - t)
                dw_ref[t:t + 1, :] += jnp.sum(dcur * _tap(vshift_ref, CONV_PAD - (CONV_W - 1) + t), axis=0, keepdims=True)
            a = pa_ref[rows, :] + ba_ref[...]
            sg = _sigmoid(pg_ref[rows, :] + bg_ref[...])
            da = dv * sg
            dgate = dv * a * sg * (1.0 - sg)
            da_ref[rows, :] = da.astype(BF16)
            dg_ref[rows, :] = dgate.astype(BF16)
            dbdw_ref[...] += jnp.sum(dcur, axis=0, keepdims=True)
            dba_ref[...] += jnp.sum(da, axis=0, keepdims=True)
            dbg_ref[...] += jnp.sum(dgate, axis=0, keepdims=True)
            return carry

        lax.fori_loop(0, n_chunks, step, 0)

    return pl.pallas_call(
        body,
        out_shape=(jax.ShapeDtypeStruct((seq, ch), BF16), jax.ShapeDtypeStruct((seq, ch), BF16),
                   jax.ShapeDtypeStruct((CONV_PAD, ch), F32), jax.ShapeDtypeStruct((1, ch), F32),
                   jax.ShapeDtypeStruct((1, ch), F32), jax.ShapeDtypeStruct((1, ch), F32)),
        grid=(ch // cb,),
        in_specs=[_col(seq, cb), _col(seq, cb), _col(seq, cb), _col(seq, cb), _col(1, cb), _col(1, cb), _col(CONV_PAD, cb)],
        out_specs=(_col(seq, cb), _col(seq, cb), _col(CONV_PAD, cb), _col(1, cb), _col(1, cb), _col(1, cb)),
        scratch_shapes=[pltpu.VMEM((seq + CONV_PAD, cb), F32), pltpu.VMEM((seq + CONV_PAD, cb), F32),
                        pltpu.VMEM((SUBLANES, _WINDOW, cb), F32), pltpu.VMEM((SUBLANES, _WINDOW, cb), F32)],
        compiler_params=_params(("parallel",)), name="conv_glu_bwd",
    )(dc, v, pa, pg, ba, bg, w_dw)


def _ffn_conv(pad_ref, r0, w_ref, b_ref, cb):
    win = pad_ref[pl.ds(r0, CHUNK + FFN_PAD), :]
    y = jnp.broadcast_to(b_ref[...], (CHUNK, cb))
    for t in range(FFN_CONV_W):
        off = FFN_PAD - (FFN_CONV_W - 1) + t
        y = y + w_ref[t:t + 1, :] * win[off:off + CHUNK, :]
    return y, win


def _ffn_mid_fwd(ug, uv, wg, wv, bg, bv):
    seq, ch = ug.shape
    cb = _tile(ch, 256)
    n_chunks = seq // CHUNK

    def body(ug_ref, uv_ref, wg_ref, wv_ref, bg_ref, bv_ref, z_ref, gpad_ref, vpad_ref):
        gpad_ref[0:FFN_PAD, :] = jnp.zeros((FFN_PAD, cb), F32)
        vpad_ref[0:FFN_PAD, :] = jnp.zeros((FFN_PAD, cb), F32)

        def fill(r, carry):
            r0 = pl.multiple_of(r * CHUNK, CHUNK)
            gpad_ref[pl.ds(r0 + FFN_PAD, CHUNK), :] = ug_ref[pl.ds(r0, CHUNK), :]
            vpad_ref[pl.ds(r0 + FFN_PAD, CHUNK), :] = uv_ref[pl.ds(r0, CHUNK), :]
            return carry

        lax.fori_loop(0, n_chunks, fill, 0)

        def step(r, carry):
            r0 = pl.multiple_of(r * CHUNK, CHUNK)
            yg, _ = _ffn_conv(gpad_ref, r0, wg_ref, bg_ref, cb)
            yv, _ = _ffn_conv(vpad_ref, r0, wv_ref, bv_ref, cb)
            z_ref[pl.ds(r0, CHUNK), :] = (yg * _sigmoid(yg) * yv).astype(BF16)
            return carry

        lax.fori_loop(0, n_chunks, step, 0)

    return pl.pallas_call(
        body, out_shape=jax.ShapeDtypeStruct((seq, ch), BF16), grid=(ch // cb,),
        in_specs=[_col(seq, cb), _col(seq, cb), _col(FFN_PAD, cb), _col(FFN_PAD, cb), _col(1, cb), _col(1, cb)],
        out_specs=_col(seq, cb),
        scratch_shapes=[pltpu.VMEM((seq + FFN_PAD, cb), F32), pltpu.VMEM((seq + FFN_PAD, cb), F32)],
        compiler_params=_params(("parallel",)), name="ffn_mid_fwd",
    )(ug, uv, wg, wv, bg, bv)


def _ffn_mid_bwd(ug, uv, dz, wg, wv, bg, bv):
    seq, ch = ug.shape
    cb = _tile(ch, 256)
    n_chunks = seq // CHUNK

    def body(ug_ref, uv_ref, dz_ref, wg_ref, wv_ref, bg_ref, bv_ref, dug_ref, duv_ref, dwg_ref, dwv_ref, dbg_ref, dbv_ref,
             gpad_ref, vpad_ref, dyg_ref, dyv_ref):
        gpad_ref[0:FFN_PAD, :] = jnp.zeros((FFN_PAD, cb), F32)
        vpad_ref[0:FFN_PAD, :] = jnp.zeros((FFN_PAD, cb), F32)
        dyg_ref[seq:seq + FFN_PAD, :] = jnp.zeros((FFN_PAD, cb), F32)
        dyv_ref[seq:seq + FFN_PAD, :] = jnp.zeros((FFN_PAD, cb), F32)
        dwg_ref[...] = jnp.zeros((FFN_PAD, cb), F32)
        dwv_ref[...] = jnp.zeros((FFN_PAD, cb), F32)
        dbg_ref[...] = jnp.zeros((1, cb), F32)
        dbv_ref[...] = jnp.zeros((1, cb), F32)

        def fill(r, carry):
            r0 = pl.multiple_of(r * CHUNK, CHUNK)
            gpad_ref[pl.ds(r0 + FFN_PAD, CHUNK), :] = ug_ref[pl.ds(r0, CHUNK), :]
            vpad_ref[pl.ds(r0 + FFN_PAD, CHUNK), :] = uv_ref[pl.ds(r0, CHUNK), :]
            return carry

        lax.fori_loop(0, n_chunks, fill, 0)

        def grads_of_conv_out(r, carry):
            r0 = pl.multiple_of(r * CHUNK, CHUNK)
            rows = pl.ds(r0, CHUNK)
            yg, gwin = _ffn_conv(gpad_ref, r0, wg_ref, bg_ref, cb)
            yv, vwin = _ffn_conv(vpad_ref, r0, wv_ref, bv_ref, cb)
            dzv = dz_ref[rows, :]
            sg = _sigmoid(yg)
            dyg = dzv * yv * (sg * (1.0 + yg * (1.0 - sg)))
            dyv = dzv * yg * sg
            dyg_ref[rows, :] = dyg
            dyv_ref[rows, :] = dyv
            for t in range(FFN_CONV_W):
                off = FFN_PAD - (FFN_CONV_W - 1) + t
                dwg_ref[t:t + 1, :] += jnp.sum(dyg * gwin[off:off + CHUNK, :], axis=0, keepdims=True)
                dwv_ref[t:t + 1, :] += jnp.sum(dyv * vwin[off:off + CHUNK, :], axis=0, keepdims=True)
            dbg_ref[...] += jnp.sum(dyg, axis=0, keepdims=True)
            dbv_ref[...] += jnp.sum(dyv, axis=0, keepdims=True)
            return carry

        lax.fori_loop(0, n_chunks, grads_of_conv_out, 0)

        def grads_of_conv_in(r, carry):
            r0 = pl.multiple_of(r * CHUNK, CHUNK)
            gwin = dyg_ref[pl.ds(r0, CHUNK + FFN_PAD), :]
            vwin = dyv_ref[pl.ds(r0, CHUNK + FFN_PAD), :]
            dug = jnp.zeros((CHUNK, cb), F32)
            duv = jnp.zeros((CHUNK, cb), F32)
            for t in range(FFN_CONV_W):
                off = FFN_CONV_W - 1 - t
                dug = dug + wg_ref[t:t + 1, :] * gwin[off:off + CHUNK, :]
                duv = duv + wv_ref[t:t + 1, :] * vwin[off:off + CHUNK, :]
            dug_ref[pl.ds(r0, CHUNK), :] = dug.astype(BF16)
            duv_ref[pl.ds(r0, CHUNK), :] = duv.astype(BF16)
            return carry

        lax.fori_loop(0, n_chunks, grads_of_conv_in, 0)

    return pl.pallas_call(
        body,
        out_shape=(jax.ShapeDtypeStruct((seq, ch), BF16), jax.ShapeDtypeStruct((seq, ch), BF16),
                   jax.ShapeDtypeStruct((FFN_PAD, ch), F32), jax.ShapeDtypeStruct((FFN_PAD, ch), F32),
                   jax.ShapeDtypeStruct((1, ch), F32), jax.ShapeDtypeStruct((1, ch), F32)),
        grid=(ch // cb,),
        in_specs=[_col(seq, cb), _col(seq, cb), _col(seq, cb), _col(FFN_PAD, cb), _col(FFN_PAD, cb), _col(1, cb), _col(1, cb)],
        out_specs=(_col(seq, cb), _col(seq, cb), _col(FFN_PAD, cb), _col(FFN_PAD, cb), _col(1, cb), _col(1, cb)),
        scratch_shapes=[pltpu.VMEM((seq + FFN_PAD, cb), F32) for _ in range(4)],
        compiler_params=_params(("parallel",)), name="ffn_mid_bwd",
    )(ug, uv, dz, wg, wv, bg, bv)


def _dot(a, b, mode):
    return lax.dot_general(a.astype(BF16), b.astype(BF16), _DIMS[mode], preferred_element_type=F32)


def _head(h):
    return slice(h * HEAD_DIM, (h + 1) * HEAD_DIM)


def _mem_softmax(q, k):
    s = _dot(q, k, "nt") * ATT_SCALE
    e = jnp.exp(s - jnp.max(s, axis=-1, keepdims=True))
    return e / jnp.sum(e, axis=-1, keepdims=True)


MEM_ROW_TILE = 512


def _mem_row(width):
    return pl.BlockSpec((MEM_ROW_TILE, width), lambda i: (i, 0))


def _memattn_fwd(pq, mk, mv):
    seq, w = pq.shape
    m = mk.shape[0]

    def body(q_ref, k_ref, v_ref, o_ref):
        for h in range(N_MEM_HEADS):
            p = _mem_softmax(q_ref[:, _head(h)], k_ref[:, _head(h)])
            o_ref[:, _head(h)] = _dot(p, v_ref[:, _head(h)], "nn").astype(BF16)

    full = pl.BlockSpec((m, w), lambda i: (0, 0))
    return pl.pallas_call(
        body, out_shape=jax.ShapeDtypeStruct((seq, w), BF16), grid=(seq // MEM_ROW_TILE,),
        in_specs=[_mem_row(w), full, full], out_specs=_mem_row(w), compiler_params=_params(("parallel",)), name="memattn_fwd",
    )(pq, mk, mv)


def _memattn_bwd(pq, mk, mv, dmo):
    seq, w = pq.shape
    m = mk.shape[0]

    def body(q_ref, k_ref, v_ref, do_ref, dq_ref, dk_ref, dv_ref):
        step = pl.program_id(0)

        @pl.when(step == 0)
        def _():
            dk_ref[...] = jnp.zeros((m, w), F32)
            dv_ref[...] = jnp.zeros((m, w), F32)

        for h in range(N_MEM_HEADS):
            q, k, v, do = q_ref[:, _head(h)], k_ref[:, _head(h)], v_ref[:, _head(h)], do_ref[:, _head(h)]
            p = _mem_softmax(q, k)
            dp = _dot(do, v, "nt")
            ds = p * (dp - jnp.sum(dp * p, axis=-1, keepdims=True))
            dq_ref[:, _head(h)] = (_dot(ds, k, "nn") * ATT_SCALE).astype(BF16)
            dk_ref[:, _head(h)] += _dot(ds, q, "tn") * ATT_SCALE
            dv_ref[:, _head(h)] += _dot(p, do, "tn")

    full = pl.BlockSpec((m, w), lambda i: (0, 0))
    return pl.pallas_call(
        body,
        out_shape=(jax.ShapeDtypeStruct((seq, w), BF16), jax.ShapeDtypeStruct((m, w), F32), jax.ShapeDtypeStruct((m, w), F32)),
        grid=(seq // MEM_ROW_TILE,), in_specs=[_mem_row(w), full, full, _mem_row(w)], out_specs=(_mem_row(w), full, full),
        compiler_params=_params(("arbitrary",)), name="memattn_bwd",
    )(pq, mk, mv, dmo)


FWD_KEY_TILE = 512
BWD_KEY_TILE = 512


def _causal_t(s_t, q_blk, k_blk, key_tile):
    kpos = k_blk * key_tile + lax.broadcasted_iota(jnp.int32, (key_tile, ROW_TILE), 0)
    qpos = q_blk * ROW_TILE + lax.broadcasted_iota(jnp.int32, (key_tile, ROW_TILE), 1)
    return jnp.where(kpos <= qpos, s_t, NEG_BIG)


def _fox_fwd(q, k, v, cum_cols, cum_rows):
    seq, w = q.shape
    nq = seq // ROW_TILE

    def body(q_ref, k_ref, v_ref, cc_ref, cr_ref, o_ref, ob_ref, lse_ref):
        i = pl.program_id(1)
        qs = [q_ref[:, _head(h)].astype(BF16) for h in range(2)]
        crs = [cr_ref[0, 0, h:h + 1, :] for h in range(2)]

        def step(j, carry):
            rows = pl.ds(pl.multiple_of(j * FWD_KEY_TILE, FWD_KEY_TILE), FWD_KEY_TILE)
            new = []
            for h in range(2):
                m_run, l_run, acc = carry[h]
                s_t = _dot(k_ref[rows, _head(h)], qs[h], "nt") * ATT_SCALE + crs[h] - cc_ref[0, rows, h * HEAD_DIM:h * HEAD_DIM + 1]
                s_t = _causal_t(s_t, i, j, FWD_KEY_TILE)
                m_new = jnp.maximum(m_run, jnp.max(s_t, axis=0, keepdims=True))
                alpha = jnp.exp(m_run - m_new)
                p_t = jnp.exp(s_t - m_new)
                l_new = alpha * l_run + jnp.sum(p_t, axis=0, keepdims=True)
                new.append((m_new, l_new, alpha * acc + _dot(v_ref[rows, _head(h)], p_t, "tn")))
            return tuple(new)

        one_head = (jnp.full((1, ROW_TILE), NEG_BIG, F32), jnp.zeros((1, ROW_TILE), F32), jnp.zeros((HEAD_DIM, ROW_TILE), F32))
        res = lax.fori_loop(0, ((i + 1) * ROW_TILE + FWD_KEY_TILE - 1) // FWD_KEY_TILE, step, (one_head, one_head))
        o = jnp.concatenate([acc / l_run for _, l_run, acc in res], axis=0).T
        o_ref[...] = o
        ob_ref[...] = o.astype(BF16)
        lse_ref[...] = jnp.zeros((1, 1, 8, ROW_TILE), F32)
        for h in range(2):
            lse_ref[0, 0, h:h + 1, :] = res[h][0] + jnp.log(res[h][1])

    blk = pl.BlockSpec((ROW_TILE, LANE), lambda hp, i: (i, hp))
    full = pl.BlockSpec((seq, LANE), lambda hp, i: (0, hp))
    cols = pl.BlockSpec((1, seq, LANE), lambda hp, i: (hp, 0, 0))
    rows = pl.BlockSpec((1, 1, 8, ROW_TILE), lambda hp, i: (hp, i, 0, 0))
    return pl.pallas_call(
        body,
        out_shape=(jax.ShapeDtypeStruct((seq, w), F32), jax.ShapeDtypeStruct((seq, w), BF16),
                   jax.ShapeDtypeStruct((N_HEAD_PAIRS, nq, 8, ROW_TILE), F32)),
        grid=(N_HEAD_PAIRS, nq), in_specs=[blk, full, full, cols, rows], out_specs=(blk, blk, rows),
        compiler_params=_params(("parallel", "parallel")), name="fox_fwd",
    )(q, k, v, cum_cols, cum_rows)


def _fox_delta(o, do):
    seq, w = o.shape
    nq = seq // ROW_TILE

    def body(o_ref, do_ref, d_ref):
        d_ref[...] = jnp.zeros((1, nq, 8, ROW_TILE), F32)

        def block(i, carry):
            rows = pl.ds(pl.multiple_of(i * ROW_TILE, ROW_TILE), ROW_TILE)
            prod_t = (o_ref[rows, :] * do_ref[rows, :]).T
            for h in range(2):
                d_ref[0, i, h:h + 1, :] = jnp.sum(prod_t[h * HEAD_DIM:(h + 1) * HEAD_DIM, :], axis=0, keepdims=True)
            return carry

        lax.fori_loop(0, nq, block, 0)

    full = pl.BlockSpec((seq, LANE), lambda hp: (0, hp))
    return pl.pallas_call(
        body, out_shape=jax.ShapeDtypeStruct((N_HEAD_PAIRS, nq, 8, ROW_TILE), F32), grid=(N_HEAD_PAIRS,),
        in_specs=[full, full], out_specs=pl.BlockSpec((1, nq, 8, ROW_TILE), lambda hp: (hp, 0, 0, 0)),
        compiler_params=_params(("parallel",)), name="fox_delta",
    )(o, do)


def _fox_bwd(q, k, v, cum_cols, cum_rows, delta, lse, do):
    seq, w = q.shape
    nq = seq // ROW_TILE
    nk = seq // BWD_KEY_TILE

    def body(q_ref, k_ref, v_ref, cc_ref, cr_ref, delta_ref, lse_ref, do_ref, dq_ref, dk_ref, dv_ref, dcc_ref, dcr_ref):
        kj = pl.program_id(1)

        @pl.when(kj == 0)
        def _():
            dq_ref[...] = jnp.zeros((seq, LANE), F32)
            dcr_ref[...] = jnp.zeros((1, nq, 8, ROW_TILE), F32)

        ks = [k_ref[:, _head(h)].astype(BF16) for h in range(2)]
        vs = [v_ref[:, _head(h)].astype(BF16) for h in range(2)]
        ccs = [cc_ref[0, :, h * HEAD_DIM:h * HEAD_DIM + 1] for h in range(2)]

        def step(i, carry):
            rows = pl.ds(pl.multiple_of(i * ROW_TILE, ROW_TILE), ROW_TILE)
            new = []
            for h in range(2):
                dk, dv, fold = carry[h]
                qh = q_ref[rows, _head(h)].astype(BF16)
                doh = do_ref[rows, _head(h)]
                s_t = _dot(ks[h], qh, "nt") * ATT_SCALE + cr_ref[0, i, h:h + 1, :] - ccs[h]
                p_t = jnp.exp(_causal_t(s_t, i, kj, BWD_KEY_TILE) - lse_ref[0, i, h:h + 1, :])
                dp_t = _dot(vs[h], doh, "nt")
                ds_t = p_t * (dp_t - delta_ref[0, i, h:h + 1, :])
                dq_ref[rows, _head(h)] += _dot(ds_t, ks[h], "tn") * ATT_SCALE
                dcr_ref[0, i, h:h + 1, :] += jnp.sum(ds_t, axis=0, keepdims=True)
                new.append((dk + _dot(ds_t, qh, "nn") * ATT_SCALE, dv + _dot(p_t, doh, "nn"), fold + (ds_t[:, :LANE] + ds_t[:, LANE:])))
            return tuple(new)

        one_head = (jnp.zeros((BWD_KEY_TILE, HEAD_DIM), F32), jnp.zeros((BWD_KEY_TILE, HEAD_DIM), F32), jnp.zeros((BWD_KEY_TILE, LANE), F32))
        res = lax.fori_loop(kj * BWD_KEY_TILE // ROW_TILE, nq, step, (one_head, one_head))
        dcc_ref[...] = jnp.zeros((1, BWD_KEY_TILE // ROW_TILE, 8, ROW_TILE), F32)
        for h in range(2):
            dk_ref[:, _head(h)] = res[h][0]
            dv_ref[:, _head(h)] = res[h][1]
            per_key = jnp.broadcast_to(-jnp.sum(res[h][2], axis=-1, keepdims=True), (BWD_KEY_TILE, LANE)).T[0:1, :]
            for b in range(BWD_KEY_TILE // ROW_TILE):
                dcc_ref[0, b, h:h + 1, :] = per_key[:, b * ROW_TILE:(b + 1) * ROW_TILE]

    blk = pl.BlockSpec((BWD_KEY_TILE, LANE), lambda hp, j: (j, hp))
    full = pl.BlockSpec((seq, LANE), lambda hp, j: (0, hp))
    cols = pl.BlockSpec((1, BWD_KEY_TILE, LANE), lambda hp, j: (hp, j, 0))
    rows = pl.BlockSpec((1, nq, 8, ROW_TILE), lambda hp, j: (hp, 0, 0, 0))
    key_rows = pl.BlockSpec((1, BWD_KEY_TILE // ROW_TILE, 8, ROW_TILE), lambda hp, j: (hp, j, 0, 0))
    return pl.pallas_call(
        body,
        out_shape=(jax.ShapeDtypeStruct((seq, w), F32), jax.ShapeDtypeStruct((seq, w), F32), jax.ShapeDtypeStruct((seq, w), F32),
                   jax.ShapeDtypeStruct((N_HEAD_PAIRS, nq, 8, ROW_TILE), F32), jax.ShapeDtypeStruct((N_HEAD_PAIRS, nq, 8, ROW_TILE), F32)),
        grid=(N_HEAD_PAIRS, nk), in_specs=[full, blk, blk, cols, rows, rows, rows, full],
        out_specs=(full, blk, blk, key_rows, rows),
        compiler_params=_params(("parallel", "arbitrary")), name="fox_bwd",
    )(q, k, v, cum_cols, cum_rows, delta, lse, do)


def _split3(x):
    hi = x.astype(BF16)
    r1 = x - hi.astype(F32)
    mid = r1.astype(BF16)
    lo = (r1 - mid.astype(F32)).astype(BF16)
    return hi, mid, lo


def _tri_sum(tri, x):
    hi, mid, lo = _split3(x)
    dot = lambda p: lax.dot_general(tri, p, _DIMS["nn"], preferred_element_type=F32)
    return (dot(lo) + dot(mid)) + dot(hi)


def _tri(lower):
    r = lax.broadcasted_iota(jnp.int32, (LANE, LANE), 0)
    c = lax.broadcasted_iota(jnp.int32, (LANE, LANE), 1)
    return jnp.where((c <= r) if lower else (c >= r), 1.0, 0.0).astype(BF16)


def _fgate_fwd(fl, bf):
    seq, w = fl.shape
    nb = seq // LANE

    def body(f_ref, b_ref, cum_ref):
        tri = _tri(True)

        def step(i, carry):
            rows = pl.ds(pl.multiple_of(i * LANE, LANE), LANE)
            x = f_ref[rows, :] + b_ref[...]
            logsig = jnp.minimum(x, 0.0) - jnp.log(1.0 + jnp.exp(-jnp.abs(x)))
            cum = _tri_sum(tri, logsig) + carry
            cum_ref[rows, :] = cum
            return cum[LANE - 1:LANE, :]

        lax.fori_loop(0, nb, step, jnp.zeros((1, w), F32))

    return pl.pallas_call(body, out_shape=jax.ShapeDtypeStruct((seq, w), F32), compiler_params=_params(), name="fgate_fwd")(fl, bf)


def _fgate_bwd(fl, bf, dcum):
    seq, w = fl.shape
    nb = seq // LANE

    def body(f_ref, b_ref, dc_ref, df_ref, db_ref):
        tri = _tri(False)

        def step(i, carry):
            tail, db = carry
            rows = pl.ds(pl.multiple_of((nb - 1 - i) * LANE, LANE), LANE)
            suffix = _tri_sum(tri, dc_ref[rows, :]) + tail
            df = suffix * (1.0 - _sigmoid(f_ref[rows, :] + b_ref[...]))
            df_ref[rows, :] = df
            return suffix[0:1, :], db + jnp.sum(df, axis=0, keepdims=True)

        _, db = lax.fori_loop(0, nb, step, (jnp.zeros((1, w), F32), jnp.zeros((1, w), F32)))
        db_ref[...] = db

    return pl.pallas_call(
        body, out_shape=(jax.ShapeDtypeStruct((seq, w), F32), jax.ShapeDtypeStruct((1, w), F32)),
        compiler_params=_params(), name="fgate_bwd",
    )(fl, bf, dcum)


def _cum_layouts(cum):
    seq = cum.shape[0]
    nblk = seq // ROW_TILE
    heads = cum[:, :N_FOX_HEADS]
    cq = jnp.repeat(heads, HEAD_DIM, axis=1).reshape(seq, N_HEAD_PAIRS, LANE).transpose(1, 0, 2)
    ck = heads.T.reshape(N_HEAD_PAIRS, 2, nblk, ROW_TILE).transpose(0, 2, 1, 3)
    ck = jnp.pad(ck, ((0, 0), (0, 0), (0, 6), (0, 0)))
    return cq, ck


def _dcum_from_layouts(key_side, query_side):
    both = key_side + query_side
    seq = both.shape[1] * both.shape[3]
    heads = both[:, :, :2, :].transpose(0, 2, 1, 3).reshape(N_FOX_HEADS, seq).T
    return jnp.pad(heads, ((0, 0), (0, F_PAD - N_FOX_HEADS)))


def _local_step(x, mem, target, w, arrive, emit):
    vec = lambda a: a.reshape(1, -1)
    mem_n = _rms_fwd(mem, vec(w["g_mem"]))
    saved = []
    shared = None
    for l in range(N_LAYERS):
        arrive(l, x, "mixer")
        s = {"x_in": x}
        s["h"] = h = _rms_fwd(x, vec(w["g_mix"][l]))
        s["mk"] = mk = _mm(mem_n, w["w_mk"][l])
        s["mv"] = mv = _mm(mem_n, w["w_mv"][l])
        if l < N_A:
            s["pa"] = pa = _mm(h, w["w_a"][l])
            s["pg"] = pg = _mm(h, w["w_g"][l])
            s["pq"] = pq = _mm(h, w["w_qm"][l])
            s["v"], s["c"] = _conv_glu_fwd(pa, pg, vec(w["b_glu_a"][l]), vec(w["b_glu_g"][l]), w["w_dw_a"][l], vec(w["b_dw_a"][l]))
            s["mix"] = mix = _ln_silu_fwd(s["c"], vec(w["ln_g"][l]), vec(w["ln_b"][l]))
        else:
            if l == N_A:
                shared = {"x_in": x}
                shared["hk"] = hk = _rms_fwd(x, vec(w["g_kv"]))
                shared["k"] = _mm(hk, w["w_k"])
                shared["v"] = _mm(hk, w["w_v"])
                shared["fl"] = _mm(hk, w["w_f"])
                cum = _fgate_fwd(shared["fl"], w["b_f"])
                shared["cq"], shared["ck"] = _cum_layouts(cum)
            s["q"] = q = _mm(h, w["w_q"][l - N_A])
            s["pq"] = pq = _mm(h, w["w_qm"][l])
            s["o"], mix, s["lse"] = _fox_fwd(q, shared["k"], shared["v"], shared["cq"], shared["ck"])
            s["mix"] = mix
        s["mo"] = mo = _memattn_fwd(pq, mk, mv)
        x = _mm(mix, w["w_o_mix"][l], add=x)
        s["x_mid"] = x = _mm(mo, w["w_o_mem"][l], add=x)
        arrive(l, x, "ffn")
        s["hf"] = hf = _rms_fwd(x, vec(w["g_ffn"][l]))
        s["ug"] = ug = _mm(hf, w["w_up_g"][l])
        s["uv"] = uv = _mm(hf, w["w_up_v"][l])
        s["z"] = z = _ffn_mid_fwd(ug, uv, w["w_dw_f_g"][l], w["w_dw_f_v"][l], vec(w["b_dw_f_g"][l]), vec(w["b_dw_f_v"][l]))
        x = _mm(z, w["w_down"][l], add=x)
        saved.append(s)

    loss_row, dx, dxb, dg_final = _loss_head(x, vec(w["g_final"]), target)
    loss = loss_row[0, 0]

    g = {n: [None] * N_LAYERS for n in ("g_mix", "w_mem_kv", "w_out", "g_ffn", "w_up", "w_dw_f", "b_dw_f", "w_down")}
    for n in ("w_in_a", "b_glu", "w_dw_a", "b_dw_a", "ln_g", "ln_b", "w_in_b"):
        g[n] = [None] * N_A
    one = lambda arr: ([arr], 0)
    after_start = lambda started, grad: grad if started is None else grad + started[0, 0].astype(BF16)
    g["g_final"] = [one(dg_final[0])]
    dmem_n = None
    dk_sum = dv_sum = dcq_sum = dck_sum = None
    for l in reversed(range(N_LAYERS)):
        s = saved[l]
        dz = _mm(dxb, w["w_down"][l], "nt")
        g["w_down"][l] = one(_mm(s["z"], dxb, "tn", BF16))
        dug, duv, dwg, dwv, dbg, dbv = _ffn_mid_bwd(s["ug"], s["uv"], dz, w["w_dw_f_g"][l], w["w_dw_f_v"][l],
                                                    vec(w["b_dw_f_g"][l]), vec(w["b_dw_f_v"][l]))
        g["w_up"][l] = ([_mm(s["hf"], dug, "tn", BF16), _mm(s["hf"], duv, "tn", BF16)], 1)
        g["w_dw_f"][l] = ([dwg[:FFN_CONV_W], dwv[:FFN_CONV_W]], 1)
        g["b_dw_f"][l] = ([dbg[0], dbv[0]], 0)
        dhf = _mm(duv, w["w_up_v"][l], "nt", add=_mm(dug, w["w_up_g"][l], "nt"))
        dx, dxb, dg_ffn = _rms_bwd(s["x_mid"], vec(w["g_ffn"][l]), dhf, dx)
        g["g_ffn"][l] = one(dg_ffn[0])
        dxb = after_start(emit(l, g, "ffn"), dxb)
        dmix = _mm(dxb, w["w_o_mix"][l], "nt")
        dmo = _mm(dxb, w["w_o_mem"][l], "nt")
        g["w_out"][l] = ([_mm(s["mix"], dxb, "tn", BF16), _mm(s["mo"], dxb, "tn", BF16)], 0)
        dpq, dmk, dmv = _memattn_bwd(s["pq"], s["mk"], s["mv"], dmo)
        g["w_mem_kv"][l] = ([_mm(mem_n, dmk, "tn", BF16), _mm(mem_n, dmv, "tn", BF16)], 1)
        dmem_n = _mm(dmk, w["w_mk"][l], "nt", add=dmem_n)
        dmem_n = _mm(dmv, w["w_mv"][l], "nt", add=dmem_n)
        if l < N_A:
            dc, dlng, dlnb = _ln_silu_bwd(dmix, s["c"], vec(w["ln_g"][l]), vec(w["ln_b"][l]))
            da, dgate, dwdw, dbdw, dba, dbg2 = _conv_glu_bwd(dc, s["v"], s["pa"], s["pg"], vec(w["b_glu_a"][l]),
                                                              vec(w["b_glu_g"][l]), w["w_dw_a"][l])
            g["ln_g"][l], g["ln_b"][l], g["b_dw_a"][l] = one(dlng[0]), one(dlnb[0]), one(dbdw[0])
            g["w_dw_a"][l] = ([dwdw[:CONV_W]], 1)
            g["b_glu"][l] = ([dba[0], dbg2[0]], 0)
            g["w_in_a"][l] = ([_mm(s["h"], da, "tn", BF16), _mm(s["h"], dgate, "tn", BF16), _mm(s["h"], dpq, "tn", BF16)], 1)
            dh = _mm(da, w["w_a"][l], "nt")
            dh = _mm(dgate, w["w_g"][l], "nt", add=dh)
            dh = _mm(dpq, w["w_qm"][l], "nt", add=dh)
        else:
            dq, dk, dv, dcq, dck = _fox_bwd(s["q"], shared["k"], shared["v"], shared["cq"], shared["ck"], _fox_delta(s["o"], dmix), s["lse"], dmix)
            dk_sum = dk if dk_sum is None else dk_sum + dk
            dv_sum = dv if dv_sum is None else dv_sum + dv
            dcq_sum = dcq if dcq_sum is None else dcq_sum + dcq
            dck_sum = dck if dck_sum is None else dck_sum + dck
            g["w_in_b"][l - N_A] = ([_mm(s["h"], dq, "tn", BF16), _mm(s["h"], dpq, "tn", BF16)], 1)
            dh = _mm(dq, w["w_q"][l - N_A], "nt")
            dh = _mm(dpq, w["w_qm"][l], "nt", add=dh)
        dx, dxb, dg_mix = _rms_bwd(s["x_in"], vec(w["g_mix"][l]), dh, dx)
        g["g_mix"][l] = one(dg_mix[0])
        if l == N_A:
            df, dbf = _fgate_bwd(shared["fl"], w["b_f"], _dcum_from_layouts(dcq_sum, dck_sum))
            hk = shared["hk"]
            g["w_kvf"] = [([_mm(hk, dk_sum, "tn", BF16), _mm(hk, dv_sum, "tn", BF16), _mm(hk, df, "tn", BF16)[:, :N_FOX_HEADS]], 1)]
            g["b_f"] = [one(dbf[0, :N_FOX_HEADS])]
            dhk = _mm(dk_sum, w["w_k"], "nt")
            dhk = _mm(dv_sum, w["w_v"], "nt", add=dhk)
            dhk = _mm(df, w["w_f"], "nt", add=dhk)
            dx, dxb, dg_kv = _rms_bwd(shared["x_in"], vec(w["g_kv"]), dhk, dx)
            g["g_kv"] = [one(dg_kv[0])]
        dxb = after_start(emit(l, g, "mixer"), dxb)
    _, _, dg_mem = _rms_bwd(mem, vec(w["g_mem"]), dmem_n, None)
    g["g_mem"] = [one(dg_mem[0])]
    return loss, dx, g


SHARDED = {
    "w_in_a": ((N_A, D_MODEL, 2 * CONV_CH + MEM_W), 2),
    "b_glu": ((N_A, 2 * CONV_CH), 1),
    "w_dw_a": ((N_A, CONV_W, CONV_CH), 2),
    "b_dw_a": ((N_A, CONV_CH), 1),
    "ln_g": ((N_A, CONV_CH), 1),
    "ln_b": ((N_A, CONV_CH), 1),
    "w_kvf": ((D_MODEL, 2 * CONV_CH + N_FOX_HEADS), 1),
    "w_in_b": ((N_LAYERS - N_A, D_MODEL, CONV_CH + MEM_W), 1),
    "w_mem_kv": ((N_LAYERS, D_MODEL, 2 * MEM_W), 1),
    "w_out": ((N_LAYERS, D_MODEL, D_MODEL), 1),
    "w_up": ((N_LAYERS, D_MODEL, 2 * D_FF), 2),
    "w_dw_f": ((N_LAYERS, FFN_CONV_W, 2 * D_FF), 2),
    "w_down": ((N_LAYERS, D_FF, D_MODEL), 1),
}
REPLICATED = {
    "g_mix": (N_LAYERS, D_MODEL), "g_kv": (D_MODEL,), "b_f": (N_FOX_HEADS,), "g_mem": (D_MODEL,),
    "g_ffn": (N_LAYERS, D_MODEL), "b_dw_f": (N_LAYERS, 2 * D_FF), "g_final": (D_MODEL,),
}
WEIGHT_ORDER = ["g_mix", "w_in_a", "b_glu", "w_dw_a", "b_dw_a", "ln_g", "ln_b", "g_kv", "w_kvf", "b_f", "w_in_b", "g_mem",
                "w_mem_kv", "w_out", "g_ffn", "w_up", "w_dw_f", "b_dw_f", "w_down", "g_final"]


def _shard_shape(name):
    shape, axis = SHARDED[name]
    return tuple(d // N_CHIPS if i == axis else d for i, d in enumerate(shape))


def _size(shape):
    n = 1
    for d in shape:
        n *= d
    return n


VECTOR_LEAVES = ("b_glu", "w_dw_a", "b_dw_a", "ln_g", "ln_b", "w_dw_f")
_HALF_ROW_TILE = 16
_VECTOR_ELEMS = sum(_size(_shard_shape(n)) for n in VECTOR_LEAVES)
VECTOR_ROWS = -(-_VECTOR_ELEMS // (FLAT_COLS * 2 * _HALF_ROW_TILE)) * 2 * _HALF_ROW_TILE
REP_ELEMS = sum(_size(s) for s in REPLICATED.values())
REP_ROWS = 8
REP_COLS = -(-REP_ELEMS // (REP_ROWS * LANE)) * LANE


def _flatten_vectors(parts, dtype):
    flat = jnp.concatenate([p.reshape(-1).astype(dtype) for p in parts])
    return jnp.pad(flat, (0, VECTOR_ROWS * FLAT_COLS - _VECTOR_ELEMS)).reshape(2, VECTOR_ROWS // 2, FLAT_COLS)


def _unflatten_vectors(flat, lead=()):
    flat = flat.reshape(lead + (-1,))
    out, off = {}, 0
    for n in VECTOR_LEAVES:
        shp = _shard_shape(n)
        out[n] = flat[..., off:off + _size(shp)].reshape(lead + shp)
        off += _size(shp)
    return out


def _span(parts, cat_axis, sel_axis, lo, hi):
    if sel_axis != cat_axis:
        return jnp.concatenate([lax.slice_in_dim(p, lo, hi, axis=sel_axis) for p in parts], axis=cat_axis)
    taken, off = [], 0
    for p in parts:
        n = p.shape[cat_axis]
        a, b = max(lo, off), min(hi, off + n)
        if a < b:
            taken.append(lax.slice_in_dim(p, a - off, b - off, axis=cat_axis))
        off += n
    return taken[0] if len(taken) == 1 else jnp.concatenate(taken, axis=cat_axis)


FFN_LEAVES = 2
LAYER_LEAVES = tuple(
    ([("w_in_a", l)] if l < N_A else ([("w_kvf", None)] if l == N_A else []) + [("w_in_b", l - N_A)])
    + [("w_mem_kv", l), ("w_out", l), ("w_up", l), ("w_down", l)]
    for l in range(N_LAYERS))


def _layer_halves(weights, leaves):
    out = []
    for name, idx in leaves:
        shard = weights[name] if idx is None else weights[name][idx]
        rows, cols = shard.shape
        assert (rows // 2) % _HALF_ROW_TILE == 0, name
        out.append(shard.astype(BF16).reshape(2, rows // 2, cols))
    return out


def _layer_pieces(l, gathered, vectors):
    parts = {}
    for (name, idx), g in zip(LAYER_LEAVES[l], gathered):
        if g is not None:
            parts[name] = g

    def cut(name, sel_axis, lo, hi):
        g = parts[name]
        axis = SHARDED[name][1] - (len(SHARDED[name][0]) - 2)
        if axis == 0:
            return lax.slice_in_dim(g.reshape(-1, g.shape[-1]), lo, hi, axis=sel_axis)
        return _span([g[t].reshape(-1, g.shape[-1]) for t in range(N_CHIPS)], axis, sel_axis, lo, hi)

    def vcut(name, idx, sel_axis, lo, hi):
        axis = SHARDED[name][1] - 1
        return _span([vectors[name][t, idx] for t in range(N_CHIPS)], axis, sel_axis, lo, hi)

    def pad_rows(arr, rows):
        return jnp.pad(arr, ((0, rows - arr.shape[0]), (0, 0)))

    w = {}
    if "w_up" in parts:
        w["w_up_g"] = cut("w_up", 1, 0, D_FF)
        w["w_up_v"] = cut("w_up", 1, D_FF, 2 * D_FF)
        w["w_dw_f_g"] = pad_rows(vcut("w_dw_f", l, 1, 0, D_FF), FFN_PAD)
        w["w_dw_f_v"] = pad_rows(vcut("w_dw_f", l, 1, D_FF, 2 * D_FF), FFN_PAD)
        w["w_down"] = cut("w_down", 0, 0, D_FF)
    if "w_out" not in parts:
        return w
    if l < N_A:
        w["w_a"] = cut("w_in_a", 1, 0, CONV_CH)
        w["w_g"] = cut("w_in_a", 1, CONV_CH, 2 * CONV_CH)
        w["w_qm"] = cut("w_in_a", 1, 2 * CONV_CH, 2 * CONV_CH + MEM_W)
        w["b_glu_a"] = vcut("b_glu", l, 0, 0, CONV_CH)
        w["b_glu_g"] = vcut("b_glu", l, 0, CONV_CH, 2 * CONV_CH)
        w["w_dw_a"] = pad_rows(vcut("w_dw_a", l, 1, 0, CONV_CH), CONV_PAD)
        for n in ("b_dw_a", "ln_g", "ln_b"):
            w[n] = vcut(n, l, 0, 0, CONV_CH)
    else:
        if l == N_A:
            w["w_k"] = cut("w_kvf", 1, 0, CONV_CH)
            w["w_v"] = cut("w_kvf", 1, CONV_CH, 2 * CONV_CH)
            w["w_f"] = jnp.pad(cut("w_kvf", 1, 2 * CONV_CH, 2 * CONV_CH + N_FOX_HEADS), ((0, 0), (0, F_PAD - N_FOX_HEADS)))
        w["w_q"] = cut("w_in_b", 1, 0, CONV_CH)
        w["w_qm"] = cut("w_in_b", 1, CONV_CH, CONV_CH + MEM_W)
    w["w_mk"] = cut("w_mem_kv", 1, 0, MEM_W)
    w["w_mv"] = cut("w_mem_kv", 1, MEM_W, 2 * MEM_W)
    w["w_o_mix"] = cut("w_out", 0, 0, CONV_CH)
    w["w_o_mem"] = cut("w_out", 0, CONV_CH, D_MODEL)
    return w


_PER_LAYER = ("w_a", "w_g", "w_q", "w_qm", "b_glu_a", "b_glu_g", "w_dw_a", "b_dw_a", "ln_g", "ln_b", "w_mk", "w_mv", "w_o_mix", "w_o_mem",
              "w_up_g", "w_up_v", "w_dw_f_g", "w_dw_f_v", "w_down")


def _weight_table(rep):
    w = dict(rep)
    w["b_f"] = jnp.pad(rep["b_f"], (0, F_PAD - N_FOX_HEADS)).reshape(1, F_PAD)
    w["b_dw_f_g"], w["b_dw_f_v"] = rep["b_dw_f"][:, :D_FF], rep["b_dw_f"][:, D_FF:]
    for n in _PER_LAYER:
        w[n] = {}
    return w


def _install_layer(w, l, pieces):
    for n, p in pieces.items():
        if n in _PER_LAYER:
            w[n][l - N_A if n == "w_q" else l] = p
        else:
            w[n] = p


def _shard_of(entry, name, t):
    shape, axis = SHARDED[name]
    pieces, cat_axis = entry
    axis -= len(shape) - len(pieces[0].shape)
    step = shape[SHARDED[name][1]] // N_CHIPS
    return _span(pieces, cat_axis, axis, t * step, (t + 1) * step)


def _layer_slabs(grads, leaves):
    out = []
    for name, idx in leaves:
        pieces, cat_axis = grads[name][0 if idx is None else idx]
        if cat_axis == 0 and SHARDED[name][1] - (len(SHARDED[name][0]) - 2) == 0:
            whole = pieces[0] if len(pieces) == 1 else jnp.concatenate(pieces, axis=0)
            out.append(whole.astype(BF16).reshape(N_CHIPS, 2, whole.shape[0] // (2 * N_CHIPS), whole.shape[1]))
            continue
        shards = [_shard_of((pieces, cat_axis), name, t).astype(BF16) for t in range(N_CHIPS)]
        rows, cols = shards[0].shape
        out.append(jnp.stack(shards).reshape(N_CHIPS, 2, rows // 2, cols))
    return out


def _vector_slabs(grads):
    def shard(name, t):
        return jnp.stack([_shard_of(entry, name, t) for entry in grads[name]])
    return jnp.stack([_flatten_vectors([shard(n, t) for n in VECTOR_LEAVES], BF16) for t in range(N_CHIPS)])


def _replicated_rows(grads):
    parts = [p.reshape(-1) for n in REPLICATED for pieces, _ in grads[n] for p in pieces]
    flat = jnp.concatenate(parts)
    return jnp.pad(flat, (0, REP_ROWS * REP_COLS - REP_ELEMS)).reshape(REP_ROWS, REP_COLS)


_ANY = pl.BlockSpec(memory_space=pl.ANY)


def _place():
    x, y, c = lax.axis_index("x"), lax.axis_index("y"), lax.axis_index("c")
    chips = [(1 - x, y), (x, 1 - y), (1 - x, 1 - y)]
    return x, y, c, chips


def _chip_index(chip):
    return 2 * chip[0] + chip[1]


def _gather_leaves(halves):
    n = len(halves)

    def body(*refs):
        w_refs, out_refs, (send_sems, recv_sems) = refs[:n], refs[n:2 * n], refs[2 * n:]
        x, y, c, chips = _place()
        me = 2 * x + y
        sibling = (x, y, 1 - c)

        def copy(a, sem, chip_idx, half, to, src=None):
            dst = out_refs[a].at[chip_idx, half]
            return pltpu.make_async_remote_copy(src_ref=dst if src is None else src, dst_ref=dst, send_sem=send_sems.at[6 * a + sem],
                                                recv_sem=recv_sems.at[6 * a + sem], device_id=to, device_id_type=MESH_ID)

        first = [copy(a, j, me, c, (*chip, c), src=w_refs[a].at[c]) for j, chip in enumerate(chips) for a in range(n)]
        for cp in first:
            cp.start()
        passed = []
        for j, chip in enumerate(chips):
            for a in range(n):
                copy(a, j, _chip_index(chip), c, (x, y, c)).wait_recv()
                passed.append(copy(a, 3 + j, _chip_index(chip), c, sibling))
                passed[-1].start()
        for j, chip in enumerate(chips):
            for a in range(n):
                copy(a, 3 + j, _chip_index(chip), 1 - c, (x, y, c)).wait_recv()
        for cp in first + passed:
            cp.wait_send()

    return pl.pallas_call(
        body, out_shape=[jax.ShapeDtypeStruct((N_CHIPS,) + h.shape, h.dtype) for h in halves],
        in_specs=[_ANY] * n, out_specs=[_ANY] * n,
        scratch_shapes=[pltpu.SemaphoreType.DMA((6 * n,)), pltpu.SemaphoreType.DMA((6 * n,))], name="gather_leaves",
    )(*halves)


_HBM = pl.BlockSpec(memory_space=pltpu.HBM)
_SEM = pl.BlockSpec(memory_space=pltpu.SEMAPHORE)


def _in_hbm(a):
    return pltpu.with_memory_space_constraint(a, pltpu.HBM)


def _gather_start(groups, after):
    flat = [h for g in groups for h in g]
    n, ng = len(flat), len(groups)

    def body(*refs):
        srcs, lands, sems = refs[1:1 + n], refs[1 + n:1 + 2 * n], refs[1 + 2 * n:1 + 2 * n + 2 * ng]
        token = refs[-1]
        x, y, c, chips = _place()
        me = 2 * x + y
        a = 0
        for gi, g in enumerate(groups):
            for k in range(len(g)):
                for j, chip in enumerate(chips):
                    pltpu.make_async_remote_copy(src_ref=srcs[a].at[c], dst_ref=lands[a].at[me, c], send_sem=sems[2 * gi].at[3 * k + j],
                                                 recv_sem=sems[2 * gi + 1].at[3 * k + j], device_id=(*chip, c), device_id_type=MESH_ID).start()
                a += 1
        token[...] = jnp.zeros_like(token)

    sem_shapes = [pltpu.SemaphoreType.DMA((3 * len(g),)) for g in groups for _ in range(2)]
    out = pl.pallas_call(
        body, name="gather_start",
        out_shape=sem_shapes + [pltpu.HBM(h.shape, h.dtype) for h in flat] + [pltpu.HBM((N_CHIPS,) + h.shape, h.dtype) for h in flat]
        + [jax.ShapeDtypeStruct((8, LANE), F32)],
        in_specs=[_ANY] + [_HBM] * (2 * n), out_specs=[_SEM] * (2 * ng) + [_HBM] * (2 * n) + [pl.BlockSpec(memory_space=pltpu.VMEM)],
        input_output_aliases={1 + i: 2 * ng + i for i in range(2 * n)},
        compiler_params=pltpu.CompilerParams(has_side_effects=pltpu.SideEffectType.DATAFLOW_SIDE_EFFECTING),
    )(after, *[_in_hbm(h) for h in flat], *[_in_hbm(lax.empty((N_CHIPS,) + h.shape, h.dtype)) for h in flat])
    sems, srcs, lands = out[:2 * ng], out[2 * ng:2 * ng + n], out[2 * ng + n:2 * ng + 2 * n]
    res, a = [], 0
    for gi, g in enumerate(groups):
        res.append((sems[2 * gi], sems[2 * gi + 1], list(srcs[a:a + len(g)]), list(lands[a:a + len(g)])))
        a += len(g)
    return res


def _gather_wait(send_sems, recv_sems, srcs, lands, after, name):
    n = len(srcs)

    def body(*refs):
        src_refs, land_refs, send, recv = refs[:n], refs[n:2 * n], refs[2 * n], refs[2 * n + 1]
        x, y, c, chips = _place()
        for k in range(n):
            for j, chip in enumerate(chips):
                cp = pltpu.make_async_remote_copy(src_ref=src_refs[k].at[c], dst_ref=land_refs[k].at[_chip_index(chip), c], send_sem=send.at[3 * k + j],
                                                  recv_sem=recv.at[3 * k + j], device_id=(*chip, c), device_id_type=MESH_ID)
                cp.wait_send()
                cp.wait_recv()

    out = pl.pallas_call(
        body, name=name, out_shape=[pltpu.HBM(a.shape, a.dtype) for a in list(srcs) + list(lands)],
        in_specs=[_HBM] * (2 * n) + [_SEM, _SEM, _ANY], out_specs=[_HBM] * (2 * n), input_output_aliases={i: i for i in range(2 * n)},
        compiler_params=pltpu.CompilerParams(has_side_effects=pltpu.SideEffectType.DATAFLOW_SIDE_EFFECTING),
    )(*srcs, *lands, send_sems, recv_sems, after)
    return list(out[n:])


def _forward_halves(lands, name):
    n = len(lands)

    def body(*refs):
        out_refs, (send_sems, recv_sems) = refs[n:2 * n], refs[2 * n:]
        x, y, c, chips = _place()

        def copy(a, j, half, to):
            blk = out_refs[a].at[_chip_index(chips[j]), half]
            return pltpu.make_async_remote_copy(src_ref=blk, dst_ref=blk, send_sem=send_sems.at[3 * a + j], recv_sem=recv_sems.at[3 * a + j],
                                                device_id=to, device_id_type=MESH_ID)

        sends = [copy(a, j, c, (x, y, 1 - c)) for a in range(n) for j in range(3)]
        for cp in sends:
            cp.start()
        for a in range(n):
            for j in range(3):
                copy(a, j, 1 - c, (x, y, c)).wait_recv()
        for cp in sends:
            cp.wait_send()

    return pl.pallas_call(
        body, out_shape=[jax.ShapeDtypeStruct(a.shape, a.dtype) for a in lands], in_specs=[_ANY] * n, out_specs=[_ANY] * n,
        input_output_aliases={i: i for i in range(n)},
        scratch_shapes=[pltpu.SemaphoreType.DMA((3 * n,)), pltpu.SemaphoreType.DMA((3 * n,))], name=name,
    )(*lands)


def _swap_halves(slabs):
    n = len(slabs)

    def body(*refs):
        g_refs, got_refs, (send_sem, recv_sem) = refs[:n], refs[n:2 * n], refs[2 * n:]
        x, y, c, _ = _place()
        copies = [pltpu.make_async_remote_copy(src_ref=g_refs[a].at[t, 1 - c], dst_ref=got_refs[a].at[t], send_sem=send_sem.at[N_CHIPS * a + t],
                                               recv_sem=recv_sem.at[N_CHIPS * a + t], device_id=(x, y, 1 - c), device_id_type=MESH_ID)
                  for a in range(n) for t in range(N_CHIPS)]
        for cp in copies:
            cp.start()
        for cp in copies:
            cp.wait()

    return pl.pallas_call(
        body, out_shape=[jax.ShapeDtypeStruct((g.shape[0],) + g.shape[2:], g.dtype) for g in slabs],
        in_specs=[_ANY] * n, out_specs=[_ANY] * n,
        scratch_shapes=[pltpu.SemaphoreType.DMA((N_CHIPS * n,)), pltpu.SemaphoreType.DMA((N_CHIPS * n,))], name="swap_halves",
    )(*slabs)


def _scatter_start(parts, name):
    n = len(parts)

    def body(*refs):
        srcs, lands, send, recv = refs[:n], refs[n:2 * n], refs[2 * n], refs[2 * n + 1]
        token = refs[-1]
        x, y, c, chips = _place()
        me = 2 * x + y
        for k in range(n):
            for j, chip in enumerate(chips):
                pltpu.make_async_remote_copy(src_ref=srcs[k].at[_chip_index(chip)], dst_ref=lands[k].at[me], send_sem=send.at[3 * k + j],
                                             recv_sem=recv.at[3 * k + j], device_id=(*chip, c), device_id_type=MESH_ID).start()
        token[...] = jnp.zeros_like(token)

    out = pl.pallas_call(
        body, name=name,
        out_shape=[pltpu.SemaphoreType.DMA((3 * n,)), pltpu.SemaphoreType.DMA((3 * n,))] + [pltpu.HBM(p.shape, p.dtype) for p in parts] * 2
        + [jax.ShapeDtypeStruct((8, LANE), F32)],
        in_specs=[_HBM] * (2 * n), out_specs=[_SEM, _SEM] + [_HBM] * (2 * n) + [pl.BlockSpec(memory_space=pltpu.VMEM)],
        input_output_aliases={i: 2 + i for i in range(2 * n)},
        compiler_params=pltpu.CompilerParams(has_side_effects=pltpu.SideEffectType.DATAFLOW_SIDE_EFFECTING),
    )(*[_in_hbm(p) for p in parts], *[_in_hbm(lax.empty(p.shape, p.dtype)) for p in parts])
    return (out[0], out[1], list(out[2:2 + n]), list(out[2 + n:2 + 2 * n])), out[-1]


def _scatter_wait(send_sems, recv_sems, srcs, lands, after, name):
    n = len(srcs)

    def body(*refs):
        src_refs, land_refs, send, recv = refs[:n], refs[n:2 * n], refs[2 * n], refs[2 * n + 1]
        x, y, c, chips = _place()
        for k in range(n):
            for j, chip in enumerate(chips):
                cp = pltpu.make_async_remote_copy(src_ref=src_refs[k].at[_chip_index(chip)], dst_ref=land_refs[k].at[_chip_index(chip)],
                                                  send_sem=send.at[3 * k + j], recv_sem=recv.at[3 * k + j], device_id=(*chip, c), device_id_type=MESH_ID)
                cp.wait_send()
                cp.wait_recv()

    out = pl.pallas_call(
        body, name=name, out_shape=[pltpu.HBM(a.shape, a.dtype) for a in list(srcs) + list(lands)],
        in_specs=[_HBM] * (2 * n) + [_SEM, _SEM, _ANY], out_specs=[_HBM] * (2 * n), input_output_aliases={i: i for i in range(2 * n)},
        compiler_params=pltpu.CompilerParams(has_side_effects=pltpu.SideEffectType.DATAFLOW_SIDE_EFFECTING),
    )(*srcs, *lands, send_sems, recv_sems, after)
    return list(out[n:])


def _sibling_halves(halves):
    n = len(halves)

    def body(*refs):
        h_refs, got_refs, (send_sem, recv_sem) = refs[:n], refs[n:2 * n], refs[2 * n:]
        x, y, c, _ = _place()
        copies = [pltpu.make_async_remote_copy(src_ref=h_refs[a], dst_ref=got_refs[a], send_sem=send_sem.at[a], recv_sem=recv_sem.at[a],
                                               device_id=(x, y, 1 - c), device_id_type=MESH_ID) for a in range(n)]
        for cp in copies:
            cp.start()
        for cp in copies:
            cp.wait()

    return pl.pallas_call(
        body, out_shape=[jax.ShapeDtypeStruct(h.shape, h.dtype) for h in halves], in_specs=[_ANY] * n, out_specs=[_ANY] * n,
        scratch_shapes=[pltpu.SemaphoreType.DMA((n,)), pltpu.SemaphoreType.DMA((n,))], name="sibling_halves",
    )(*halves)


def _gather_replicated(rows):
    m_per, n = rows.shape

    def body(x_ref, out_ref, send_sems, recv_sems, local_sem):
        x, y, c, chips = _place()
        me, sibling = (x, y, c), (x, y, 1 - c)

        def block(px, py, pc):
            return out_ref.at[pl.ds((4 * px + 2 * py + pc) * m_per, m_per), :]

        def copy(k, blk, to, src=None):
            return pltpu.make_async_remote_copy(src_ref=block(*blk) if src is None else src, dst_ref=block(*blk),
                                                send_sem=send_sems.at[k], recv_sem=recv_sems.at[k], device_id=to, device_id_type=MESH_ID)

        mine = pltpu.make_async_copy(x_ref, block(*me), local_sem)
        mine.start()
        first = [copy(0, me, sibling, src=x_ref)]
        first += [copy(1 + j, me, (*chip, c), src=x_ref) for j, chip in enumerate(chips)]
        for cp in first:
            cp.start()
        passed = [copy(4 + j, (*chip, c), sibling) for j, chip in enumerate(chips)]
        for j, chip in enumerate(chips):
            copy(1 + j, (*chip, c), me).wait_recv()
            passed[j].start()
        copy(0, sibling, me).wait_recv()
        for j, chip in enumerate(chips):
            copy(4 + j, (*chip, 1 - c), me).wait_recv()
        for cp in first + passed:
            cp.wait_send()
        mine.wait()

    vmem = pl.BlockSpec(memory_space=pltpu.VMEM)
    return pl.pallas_call(
        body, out_shape=jax.ShapeDtypeStruct((N_DEV * m_per, n), rows.dtype), in_specs=[vmem], out_specs=vmem,
        scratch_shapes=[pltpu.SemaphoreType.DMA((7,)), pltpu.SemaphoreType.DMA((7,)), pltpu.SemaphoreType.DMA],
        name="gather_replicated",
    )(rows)


_SUM_ROWS = 256


def _tile_rows(rows):
    if rows <= _SUM_ROWS:
        return rows
    best = None
    for r in range(16, _SUM_ROWS + 1, 16):
        if rows % r == 0:
            best = r
    assert best is not None, rows
    return best


def _add_pairs(slabs, got, core):
    n, _, rows, cols = slabs.shape
    tr = _tile_rows(rows)

    def body(core_ref, a_ref, b_ref, o_ref):
        o_ref[...] = (a_ref[0].astype(F32) + b_ref[...].astype(F32)).astype(BF16)

    spec = pl.BlockSpec((1, tr, cols), lambda t, i, core_ref: (t, i, 0))
    return pl.pallas_call(
        body, out_shape=jax.ShapeDtypeStruct((n, rows, cols), BF16),
        grid_spec=pltpu.PrefetchScalarGridSpec(
            num_scalar_prefetch=1, grid=(n, rows // tr),
            in_specs=[pl.BlockSpec((1, 1, tr, cols), lambda t, i, core_ref: (t, core_ref[0], i, 0)), spec], out_specs=spec),
        compiler_params=_params(("parallel", "parallel")), name=f"add_pairs_{rows}x{cols}",
    )(core, slabs, got)


def _sum_slabs(slabs):
    n, rows, cols = slabs.shape
    tr = _tile_rows(rows)

    def body(s_ref, o_ref):
        acc = s_ref[0].astype(F32)
        for t in range(1, n):
            acc = acc + s_ref[t].astype(F32)
        o_ref[...] = acc

    return pl.pallas_call(
        body, out_shape=jax.ShapeDtypeStruct((rows, cols), F32), grid=(rows // tr,),
        in_specs=[pl.BlockSpec((n, tr, cols), lambda i: (0, i, 0))], out_specs=pl.BlockSpec((tr, cols), lambda i: (i, 0)),
        compiler_params=_params(("parallel",)), name=f"sum_slabs_{n}x{rows}x{cols}",
    )(slabs)


_ADAM_BLOCK_BYTES = 2 * 1024 * 1024


def _adamw(w, g, m, v):
    shape = w.shape
    cols = shape[-1]
    rows = _size(shape) // cols
    tr = rows
    if rows * cols * 4 > _ADAM_BLOCK_BYTES:
        for r in range(8, rows, 8):
            if rows % r == 0 and r * cols * 4 <= _ADAM_BLOCK_BYTES:
                tr = r

    def body(w_ref, g_ref, m_ref, v_ref, d_ref, nm_ref, nv_ref):
        gv = g_ref[...]
        nm = ADAM_B1 * m_ref[...] + (1.0 - ADAM_B1) * gv
        nv = ADAM_B2 * v_ref[...] + (1.0 - ADAM_B2) * jnp.square(gv)
        m_hat = nm / (1.0 - ADAM_B1 ** ADAM_STEP)
        v_hat = nv / (1.0 - ADAM_B2 ** ADAM_STEP)
        d_ref[...] = -ADAM_LR * (m_hat / (jnp.sqrt(v_hat) + ADAM_EPS) + ADAM_WD * w_ref[...])
        nm_ref[...] = nm
        nv_ref[...] = nv

    spec = pl.BlockSpec((tr, cols), lambda i: (i, 0))
    out = pl.pallas_call(
        body, out_shape=tuple(jax.ShapeDtypeStruct((rows, cols), F32) for _ in range(3)), grid=(rows // tr,),
        in_specs=[spec] * 4, out_specs=(spec,) * 3, compiler_params=_params(("parallel",)), name=f"adamw_{rows}x{cols}",
    )(*(a.reshape(rows, cols) for a in (w, g, m, v)))
    return tuple(o.reshape(shape) for o in out)


def kernel(x, mem, g_mix, w_in_a, b_glu, w_dw_a, b_dw_a, ln_g, ln_b, g_kv, w_kvf, b_f, w_in_b, g_mem, w_mem_kv, w_out, g_ffn, w_up, w_dw_f, b_dw_f, w_down, g_final, loss_target, m_g_mix, m_w_in_a, m_b_glu, m_w_dw_a, m_b_dw_a, m_ln_g, m_ln_b, m_g_kv, m_w_kvf, m_b_f, m_w_in_b, m_g_mem, m_w_mem_kv, m_w_out, m_g_ffn, m_w_up, m_w_dw_f, m_b_dw_f, m_w_down, m_g_final, v_g_mix, v_w_in_a, v_b_glu, v_w_dw_a, v_b_dw_a, v_ln_g, v_ln_b, v_g_kv, v_w_kvf, v_b_f, v_w_in_b, v_g_mem, v_w_mem_kv, v_w_out, v_g_ffn, v_w_up, v_w_dw_f, v_b_dw_f, v_w_down, v_g_final):
    weights = dict(g_mix=g_mix, w_in_a=w_in_a, b_glu=b_glu, w_dw_a=w_dw_a, b_dw_a=b_dw_a, ln_g=ln_g, ln_b=ln_b, g_kv=g_kv,
                   w_kvf=w_kvf, b_f=b_f, w_in_b=w_in_b, g_mem=g_mem, w_mem_kv=w_mem_kv, w_out=w_out, g_ffn=g_ffn, w_up=w_up,
                   w_dw_f=w_dw_f, b_dw_f=b_dw_f, w_down=w_down, g_final=g_final)
    mom1 = dict(g_mix=m_g_mix, w_in_a=m_w_in_a, b_glu=m_b_glu, w_dw_a=m_w_dw_a, b_dw_a=m_b_dw_a, ln_g=m_ln_g, ln_b=m_ln_b,
                g_kv=m_g_kv, w_kvf=m_w_kvf, b_f=m_b_f, w_in_b=m_w_in_b, g_mem=m_g_mem, w_mem_kv=m_w_mem_kv, w_out=m_w_out,
                g_ffn=m_g_ffn, w_up=m_w_up, w_dw_f=m_w_dw_f, b_dw_f=m_b_dw_f, w_down=m_w_down, g_final=m_g_final)
    mom2 = dict(g_mix=v_g_mix, w_in_a=v_w_in_a, b_glu=v_b_glu, w_dw_a=v_w_dw_a, b_dw_a=v_b_dw_a, ln_g=v_ln_g, ln_b=v_ln_b,
                g_kv=v_g_kv, w_kvf=v_w_kvf, b_f=v_b_f, w_in_b=v_w_in_b, g_mem=v_g_mem, w_mem_kv=v_w_mem_kv, w_out=v_w_out,
                g_ffn=v_g_ffn, w_up=v_w_up, w_dw_f=v_w_dw_f, b_dw_f=v_b_dw_f, w_down=v_w_down, g_final=v_g_final)

    x_pos, y_pos, core = lax.axis_index("x"), lax.axis_index("y"), lax.axis_index("c")
    chip = 2 * x_pos + y_pos

    mixer_of = lambda l: LAYER_LEAVES[l][:-FFN_LEAVES]
    ffn_of = lambda l: LAYER_LEAVES[l][-FFN_LEAVES:]
    local_first = _layer_halves(weights, mixer_of(0))
    vector_halves = _flatten_vectors([weights[n] for n in VECTOR_LEAVES], F32)
    first = _gather_leaves(local_first + [vector_halves])
    later = [ffn_of(0)] + [LAYER_LEAVES[l] for l in range(1, N_LAYERS)]
    local_later = [_layer_halves(weights, leaves) for leaves in later]
    in_flight = _gather_start(local_later, first[-1])

    def with_own(gathered, own):
        return [lax.dynamic_update_index_in_dim(g, o[None], chip, 0) for g, o in zip(gathered, own)]

    first = with_own(first, local_first + [vector_halves])
    vectors = _unflatten_vectors(first[-1], (N_CHIPS,))
    table = _weight_table({n: weights[n] for n in REPLICATED})

    def fetch(k, after):
        send_sems, recv_sems, srcs, lands = in_flight[k]
        lands = _gather_wait(send_sems, recv_sems, srcs, lands, after, f"gather_wait_{k}")
        return with_own(_forward_halves(lands, f"forward_halves_{k}"), local_later[k])

    def arrive(l, after, part):
        if l == 0:
            gathered = first[:-1] + [None] * FFN_LEAVES if part == "mixer" else [None] * len(mixer_of(0)) + fetch(0, after)
        elif part == "mixer":
            gathered = fetch(l, after)
        else:
            return
        _install_layer(table, l, _layer_pieces(l, gathered, vectors))

    core_index = core.astype(jnp.int32).reshape(1)
    in_flight_back = []

    def emit(l, g, part):
        if l == 0:
            leaves = ffn_of(0) if part == "ffn" else mixer_of(0)
        elif part == "mixer":
            leaves = LAYER_LEAVES[l]
        else:
            return None
        with_vectors = l == 0 and part == "mixer"
        slabs = _layer_slabs(g, leaves) + ([_vector_slabs(g)] if with_vectors else [])
        chip_sums = [_add_pairs(mine, got, core_index) for mine, got in zip(slabs, _swap_halves(slabs))]
        handles, started = _scatter_start(chip_sums, f"scatter_start_{len(in_flight_back)}")
        in_flight_back.append((leaves, with_vectors, chip_sums, handles))
        return started

    loss, dx, grads = _local_step(x[0], mem[0], loss_target[0], table, arrive, emit)
    loss = lax.psum(loss, ("x", "y", "c"))

    names, mine = [], []
    for k, (leaves, with_vectors, chip_sums, (send_sems, recv_sems, srcs, lands)) in enumerate(in_flight_back):
        lands = _scatter_wait(send_sems, recv_sems, srcs, lands, dx, f"scatter_wait_{k}")
        from_chips = [lax.dynamic_update_index_in_dim(got, lax.dynamic_index_in_dim(own, chip, 0, keepdims=True), chip, 0)
                      for got, own in zip(lands, chip_sums)]
        mine += [_sum_slabs(f) for f in from_chips]
        names += list(leaves) + ([("vectors", None)] if with_vectors else [])
    theirs = _sibling_halves(mine)
    per_leaf = {}
    for (name, idx), m, t in zip(names, mine, theirs):
        per_leaf.setdefault(name, {})[idx] = jnp.concatenate([jnp.where(core == 0, m, t), jnp.where(core == 0, t, m)])
    vector_grads = per_leaf.pop("vectors")[None]
    grad_leaves = {n: (d[None] if None in d else jnp.stack([d[i] for i in sorted(d)])) for n, d in per_leaf.items()}
    grad_leaves.update(_unflatten_vectors(vector_grads))

    rep_sum = _sum_slabs(_gather_replicated(_replicated_rows(grads)).reshape(N_DEV, REP_ROWS, REP_COLS)).reshape(-1)
    off = 0
    for n, shp in REPLICATED.items():
        grad_leaves[n] = rep_sum[off:off + _size(shp)].reshape(shp)
        off += _size(shp)

    deltas, new_m, new_v = {}, {}, {}
    for n in WEIGHT_ORDER:
        deltas[n], new_m[n], new_v[n] = _adamw(weights[n], grad_leaves[n], mom1[n], mom2[n])
    return (loss, dx[None], *[grad_leaves[n] for n in WEIGHT_ORDER], *[deltas[n] for n in WEIGHT_ORDER],
            *[new_m[n] for n in WEIGHT_ORDER], *[new_v[n] for n in WEIGHT_ORDER])
```

```python
import jax
import jax.numpy as jnp
from jax import lax
from jax.experimental import pallas as pl
from jax.experimental.pallas import tpu as pltpu

F32 = jnp.float32
BF16 = jnp.bfloat16

D_MODEL = 1024
N_LAYERS = 4
N_A = 2
CONV_CH = 768
MEM_W = 256
HEAD_DIM = 64
N_MEM_HEADS = 4
N_FOX_HEADS = 12
N_HEAD_PAIRS = N_FOX_HEADS // 2
D_FF = 2816
CONV_W = 31
CONV_PAD = 32
FFN_CONV_W = 3
FFN_PAD = 8
F_PAD = 128
RMS_EPS = 1e-6
LN_EPS = 1e-5
ATT_SCALE = HEAD_DIM ** -0.5
NEG_BIG = -1e30

ADAM_LR = 0.001
ADAM_B1 = 0.9
ADAM_B2 = 0.999
ADAM_EPS = 1e-08
ADAM_WD = 0.01
ADAM_STEP = 10

LANE = 128
ROW_TILE = 256
CHUNK = 128
VMEM_LIMIT = 48 * 1024 * 1024
FLAT_COLS = 1024
N_CHIPS = 4
N_DEV = 8
MESH_ID = pl.DeviceIdType.MESH


def _params(sem=None):
    return pltpu.CompilerParams(dimension_semantics=sem, vmem_limit_bytes=VMEM_LIMIT)


def _tile(dim, pref):
    if dim <= pref:
        return dim
    best = None
    for m in range(1, dim // LANE + 1):
        d = m * LANE
        if dim % d == 0 and d <= pref:
            best = d
    assert best is not None, dim
    return best


_DIMS = {"nn": (((1,), (0,)), ((), ())), "nt": (((1,), (1,)), ((), ())), "tn": (((0,), (0,)), ((), ()))}


_MM_RESIDENT_BYTES = 8 * 1024 * 1024
_MM_STREAM_BYTES = 6 * 1024 * 1024
_MM_OUT_BYTES = 6 * 1024 * 1024


def _mm_tiles(m, n, k, mode, a_size, b_size, o_size):
    if mode == "tn":
        tm = _tile(m, _MM_RESIDENT_BYTES // (k * a_size))
        tn = _tile(n, min(_MM_STREAM_BYTES // (k * b_size), _MM_OUT_BYTES // (tm * o_size)))
    else:
        tn = _tile(n, _MM_RESIDENT_BYTES // (k * b_size))
        tm = _tile(m, min(_MM_STREAM_BYTES // (k * a_size), _MM_OUT_BYTES // (tn * o_size), 512))
    return tm, tn


def _mm(a, b, mode="nn", out_dtype=F32, add=None):
    if mode == "nn":
        (m, k), (k2, n) = a.shape, b.shape
    elif mode == "nt":
        (m, k), (n, k2) = a.shape, b.shape
    else:
        (k, m), (k2, n) = a.shape, b.shape
    assert k == k2, (a.shape, b.shape, mode)
    tm, tn = _mm_tiles(m, n, k, mode, a.dtype.itemsize, b.dtype.itemsize, jnp.dtype(out_dtype).itemsize)
    dims = _DIMS[mode]
    has_add = add is not None

    def body(*refs):
        if has_add:
            a_ref, b_ref, add_ref, o_ref = refs
        else:
            a_ref, b_ref, o_ref = refs
        r = lax.dot_general(a_ref[...].astype(BF16), b_ref[...].astype(BF16), dims, preferred_element_type=F32)
        if has_add:
            r = r + add_ref[...]
        o_ref[...] = r.astype(out_dtype)

    a_spec = pl.BlockSpec((k, tm), lambda i, j: (0, i)) if mode == "tn" else pl.BlockSpec((tm, k), lambda i, j: (i, 0))
    b_spec = pl.BlockSpec((tn, k), lambda i, j: (j, 0)) if mode == "nt" else pl.BlockSpec((k, tn), lambda i, j: (0, j))
    o_spec = pl.BlockSpec((tm, tn), lambda i, j: (i, j))
    in_specs = [a_spec, b_spec] + ([o_spec] if has_add else [])
    args = (a, b) + ((add,) if has_add else ())
    return pl.pallas_call(
        body,
        out_shape=jax.ShapeDtypeStruct((m, n), out_dtype),
        grid=(m // tm, n // tn),
        in_specs=in_specs,
        out_specs=o_spec,
        compiler_params=_params(("parallel", "parallel")),
        name=f"mm_{mode}_{m}x{k}x{n}",
    )(*args)


def _row(width):
    return pl.BlockSpec((ROW_TILE, width), lambda i: (i, 0))


def _vec(width):
    return pl.BlockSpec((1, width), lambda i: (0, 0))


def _rms_fwd(x, g):
    rows, d = x.shape

    def body(x_ref, g_ref, o_ref):
        xv = x_ref[...]
        rstd = lax.rsqrt(jnp.mean(xv * xv, axis=-1, keepdims=True) + RMS_EPS)
        o_ref[...] = (xv * rstd * g_ref[...]).astype(BF16)

    return pl.pallas_call(
        body, out_shape=jax.ShapeDtypeStruct((rows, d), BF16), grid=(rows // ROW_TILE,),
        in_specs=[_row(d), _vec(d)], out_specs=_row(d), compiler_params=_params(("parallel",)), name=f"rms_fwd_{rows}",
    )(x, g)


def _accumulate(ref, value, step):
    @pl.when(step == 0)
    def _():
        ref[...] = value

    @pl.when(step > 0)
    def _():
        ref[...] += value


def _rms_bwd(x, g, dh, dres):
    rows, d = x.shape
    has_res = dres is not None

    def body(*refs):
        if has_res:
            x_ref, g_ref, dh_ref, dres_ref, dx_ref, dxb_ref, dg_ref = refs
        else:
            x_ref, g_ref, dh_ref, dx_ref, dxb_ref, dg_ref = refs
        xv = x_ref[...]
        dhv = dh_ref[...]
        rstd = lax.rsqrt(jnp.mean(xv * xv, axis=-1, keepdims=True) + RMS_EPS)
        xhat = xv * rstd
        gd = dhv * g_ref[...]
        dx = rstd * (gd - xhat * jnp.mean(gd * xhat, axis=-1, keepdims=True))
        if has_res:
            dx = dx + dres_ref[...]
        dx_ref[...] = dx
        dxb_ref[...] = dx.astype(BF16)
        _accumulate(dg_ref, jnp.sum(dhv * xhat, axis=0, keepdims=True), pl.program_id(0))

    in_specs = [_row(d), _vec(d), _row(d)] + ([_row(d)] if has_res else [])
    args = (x, g, dh) + ((dres,) if has_res else ())
    return pl.pallas_call(
        body,
        out_shape=(jax.ShapeDtypeStruct((rows, d), F32), jax.ShapeDtypeStruct((rows, d), BF16), jax.ShapeDtypeStruct((1, d), F32)),
        grid=(rows // ROW_TILE,), in_specs=in_specs, out_specs=(_row(d), _row(d), _vec(d)),
        compiler_params=_params(("arbitrary",)), name=f"rms_bwd_{rows}_{int(has_res)}",
    )(*args)


def _loss_head(x, g, target):
    rows, d = x.shape

    def body(x_ref, g_ref, t_ref, loss_ref, dx_ref, dxb_ref, dg_ref):
        xv = x_ref[...]
        gv = g_ref[...]
        rstd = lax.rsqrt(jnp.mean(xv * xv, axis=-1, keepdims=True) + RMS_EPS)
        xhat = xv * rstd
        err = xhat * gv - t_ref[...]
        part = 0.5 * jnp.sum(jnp.mean(err * err, axis=-1, keepdims=True), axis=0, keepdims=True)
        dy = err * (1.0 / d)
        gd = dy * gv
        dx = rstd * (gd - xhat * jnp.mean(gd * xhat, axis=-1, keepdims=True))
        dx_ref[...] = dx
        dxb_ref[...] = dx.astype(BF16)
        step = pl.program_id(0)
        _accumulate(loss_ref, jnp.broadcast_to(part, (1, LANE)), step)
        _accumulate(dg_ref, jnp.sum(dy * xhat, axis=0, keepdims=True), step)

    return pl.pallas_call(
        body,
        out_shape=(jax.ShapeDtypeStruct((1, LANE), F32), jax.ShapeDtypeStruct((rows, d), F32),
                   jax.ShapeDtypeStruct((rows, d), BF16), jax.ShapeDtypeStruct((1, d), F32)),
        grid=(rows // ROW_TILE,), in_specs=[_row(d), _vec(d), _row(d)],
        out_specs=(_vec(LANE), _row(d), _row(d), _vec(d)),
        compiler_params=_params(("arbitrary",)), name="loss_head",
    )(x, g, target)


def _sigmoid(x):
    return 1.0 / (1.0 + jnp.exp(-x))


def _ln_silu_fwd(c, ln_g, ln_b):
    rows, ch = c.shape

    def body(c_ref, g_ref, b_ref, o_ref):
        cv = c_ref[...]
        mu = jnp.mean(cv, axis=-1, keepdims=True)
        cen = cv - mu
        rstd = lax.rsqrt(jnp.mean(cen * cen, axis=-1, keepdims=True) + LN_EPS)
        y = cen * rstd * g_ref[...] + b_ref[...]
        o_ref[...] = (y * _sigmoid(y)).astype(BF16)

    return pl.pallas_call(
        body, out_shape=jax.ShapeDtypeStruct((rows, ch), BF16), grid=(rows // ROW_TILE,),
        in_specs=[_row(ch), _vec(ch), _vec(ch)], out_specs=_row(ch), compiler_params=_params(("parallel",)), name="ln_silu_fwd",
    )(c, ln_g, ln_b)


def _ln_silu_bwd(dmix, c, ln_g, ln_b):
    rows, ch = c.shape

    def body(dm_ref, c_ref, g_ref, b_ref, dc_ref, dg_ref, db_ref):
        cv = c_ref[...]
        gv = g_ref[...]
        mu = jnp.mean(cv, axis=-1, keepdims=True)
        cen = cv - mu
        rstd = lax.rsqrt(jnp.mean(cen * cen, axis=-1, keepdims=True) + LN_EPS)
        xhat = cen * rstd
        y = xhat * gv + b_ref[...]
        sg = _sigmoid(y)
        dy = dm_ref[...] * (sg * (1.0 + y * (1.0 - sg)))
        dxh = dy * gv
        dc = rstd * (dxh - jnp.mean(dxh, axis=-1, keepdims=True) - xhat * jnp.mean(dxh * xhat, axis=-1, keepdims=True))
        dc_ref[...] = dc
        step = pl.program_id(0)
        _accumulate(dg_ref, jnp.sum(dy * xhat, axis=0, keepdims=True), step)
        _accumulate(db_ref, jnp.sum(dy, axis=0, keepdims=True), step)

    return pl.pallas_call(
        body,
        out_shape=(jax.ShapeDtypeStruct((rows, ch), F32), jax.ShapeDtypeStruct((1, ch), F32), jax.ShapeDtypeStruct((1, ch), F32)),
        grid=(rows // ROW_TILE,), in_specs=[_row(ch), _row(ch), _vec(ch), _vec(ch)],
        out_specs=(_row(ch), _vec(ch), _vec(ch)), compiler_params=_params(("arbitrary",)), name="ln_silu_bwd",
    )(dmix, c, ln_g, ln_b)


def _col(rows, cb):
    return pl.BlockSpec((rows, cb), lambda j: (0, j))


SUBLANES = 8
_WINDOW = CHUNK + CONV_PAD


def _realign(win, shifted_ref):
    shifted_ref[0] = win
    for r in range(1, SUBLANES):
        shifted_ref[r, 0:_WINDOW - SUBLANES, :] = win[r:r + _WINDOW - SUBLANES, :]


def _tap(shifted_ref, off):
    r = off % SUBLANES
    return shifted_ref[r, off - r:off - r + CHUNK, :]


def _conv_glu_fwd(pa, pg, ba, bg, w_dw, b_dw):
    seq, ch = pa.shape
    cb = LANE
    n_chunks = seq // CHUNK

    def body(pa_ref, pg_ref, ba_ref, bg_ref, w_ref, b_ref, v_ref, c_ref, vpad_ref, vshift_ref):
        vpad_ref[0:CONV_PAD, :] = jnp.zeros((CONV_PAD, cb), F32)

        def glu(r, carry):
            r0 = pl.multiple_of(r * CHUNK, CHUNK)
            rows = pl.ds(r0, CHUNK)
            v = (pa_ref[rows, :] + ba_ref[...]) * _sigmoid(pg_ref[rows, :] + bg_ref[...])
            v_ref[rows, :] = v
            vpad_ref[pl.ds(r0 + CONV_PAD, CHUNK), :] = v
            return carry

        lax.fori_loop(0, n_chunks, glu, 0)

        def conv(r, carry):
            r0 = pl.multiple_of(r * CHUNK, CHUNK)
            _realign(vpad_ref[pl.ds(r0, _WINDOW), :], vshift_ref)
            acc = jnp.broadcast_to(b_ref[...], (CHUNK, cb))
            for t in range(CONV_W):
                acc = acc + w_ref[t:t + 1, :] * _tap(vshift_ref, CONV_PAD - (CONV_W - 1) + t)
            c_ref[pl.ds(r0, CHUNK), :] = acc
            return carry

        lax.fori_loop(0, n_chunks, conv, 0)

    return pl.pallas_call(
        body,
        out_shape=(jax.ShapeDtypeStruct((seq, ch), F32), jax.ShapeDtypeStruct((seq, ch), F32)),
        grid=(ch // cb,),
        in_specs=[_col(seq, cb), _col(seq, cb), _col(1, cb), _col(1, cb), _col(CONV_PAD, cb), _col(1, cb)],
        out_specs=(_col(seq, cb), _col(seq, cb)),
        scratch_shapes=[pltpu.VMEM((seq + CONV_PAD, cb), F32), pltpu.VMEM((SUBLANES, _WINDOW, cb), F32)],
        compiler_params=_params(("parallel",)), name="conv_glu_fwd",
    )(pa, pg, ba, bg, w_dw, b_dw)


def _conv_glu_bwd(dc, v, pa, pg, ba, bg, w_dw):
    seq, ch = dc.shape
    cb = LANE
    n_chunks = seq // CHUNK

    def body(dc_ref, v_ref, pa_ref, pg_ref, ba_ref, bg_ref, w_ref, da_ref, dg_ref, dw_ref, dbdw_ref, dba_ref, dbg_ref,
             dcpad_ref, vpad_ref, dcshift_ref, vshift_ref):
        vpad_ref[0:CONV_PAD, :] = jnp.zeros((CONV_PAD, cb), F32)
        dcpad_ref[seq:seq + CONV_PAD, :] = jnp.zeros((CONV_PAD, cb), F32)
        dw_ref[...] = jnp.zeros((CONV_PAD, cb), F32)
        dbdw_ref[...] = jnp.zeros((1, cb), F32)
        dba_ref[...] = jnp.zeros((1, cb), F32)
        dbg_ref[...] = jnp.zeros((1, cb), F32)

        def fill(r, carry):
            r0 = pl.multiple_of(r * CHUNK, CHUNK)
            vpad_ref[pl.ds(r0 + CONV_PAD, CHUNK), :] = v_ref[pl.ds(r0, CHUNK), :]
            dcpad_ref[pl.ds(r0, CHUNK), :] = dc_ref[pl.ds(r0, CHUNK), :]
            return carry

        lax.fori_loop(0, n_chunks, fill, 0)

        def step(r, carry):
            r0 = pl.multiple_of(r * CHUNK, CHUNK)
            rows = pl.ds(r0, CHUNK)
            _realign(dcpad_ref[pl.ds(r0, _WINDOW), :], dcshift_ref)
            _realign(vpad_ref[pl.ds(r0, _WINDOW), :], vshift_ref)
            dcur = _tap(dcshift_ref, 0)
            dv = jnp.zeros((CHUNK, cb), F32)
            for t in range(CONV_W):
                dv = dv + w_ref[t:t + 1, :] * _tap(dcshift_ref, CONV_W - 1 - t)
                dw_ref[t:t + 1, :] += jnp.sum(dcur * _tap(vshift_ref, CONV_PAD - (CONV_W - 1) + t), axis=0, keepdims=True)
            a = pa_ref[rows, :] + ba_ref[...]
            sg = _sigmoid(pg_ref[rows, :] + bg_ref[...])
            da = dv * sg
            dgate = dv * a * sg * (1.0 - sg)
            da_ref[rows, :] = da.astype(BF16)
            dg_ref[rows, :] = dgate.astype(BF16)
            dbdw_ref[...] += jnp.sum(dcur, axis=0, keepdims=True)
            dba_ref[...] += jnp.sum(da, axis=0, keepdims=True)
            dbg_ref[...] += jnp.sum(dgate, axis=0, keepdims=True)
            return carry

        lax.fori_loop(0, n_chunks, step, 0)

    return pl.pallas_call(
        body,
        out_shape=(jax.ShapeDtypeStruct((seq, ch), BF16), jax.ShapeDtypeStruct((seq, ch), BF16),
                   jax.ShapeDtypeStruct((CONV_PAD, ch), F32), jax.ShapeDtypeStruct((1, ch), F32),
                   jax.ShapeDtypeStruct((1, ch), F32), jax.ShapeDtypeStruct((1, ch), F32)),
        grid=(ch // cb,),
        in_specs=[_col(seq, cb), _col(seq, cb), _col(seq, cb), _col(seq, cb), _col(1, cb), _col(1, cb), _col(CONV_PAD, cb)],
        out_specs=(_col(seq, cb), _col(seq, cb), _col(CONV_PAD, cb), _col(1, cb), _col(1, cb), _col(1, cb)),
        scratch_shapes=[pltpu.VMEM((seq + CONV_PAD, cb), F32), pltpu.VMEM((seq + CONV_PAD, cb), F32),
                        pltpu.VMEM((SUBLANES, _WINDOW, cb), F32), pltpu.VMEM((SUBLANES, _WINDOW, cb), F32)],
        compiler_params=_params(("parallel",)), name="conv_glu_bwd",
    )(dc, v, pa, pg, ba, bg, w_dw)


def _ffn_window(u_ref, r, cb):
    if isinstance(r, int):
        assert r == 0
        return jnp.concatenate([jnp.zeros((FFN_PAD, cb), F32), u_ref[0:CHUNK, :]], axis=0)
    return u_ref[pl.ds(pl.multiple_of(r * CHUNK - FFN_PAD, FFN_PAD), CHUNK + FFN_PAD), :]


def _chunks(n_chunks, body):
    body(0)
    lax.fori_loop(1, n_chunks, lambda r, carry: (body(r), carry)[1], 0)


def _ffn_conv(u_ref, r, w_ref, b_ref, cb):
    win = _ffn_window(u_ref, r, cb)
    y = jnp.broadcast_to(b_ref[...], (CHUNK, cb))
    for t in range(FFN_CONV_W):
        off = FFN_PAD - (FFN_CONV_W - 1) + t
        y = y + w_ref[t:t + 1, :] * win[off:off + CHUNK, :]
    return y, win


def _ffn_mid_fwd(ug, uv, wg, wv, bg, bv):
    seq, ch = ug.shape
    cb = _tile(ch, 256)
    n_chunks = seq // CHUNK

    def body(ug_ref, uv_ref, wg_ref, wv_ref, bg_ref, bv_ref, z_ref):
        def step(r):
            yg, _ = _ffn_conv(ug_ref, r, wg_ref, bg_ref, cb)
            yv, _ = _ffn_conv(uv_ref, r, wv_ref, bv_ref, cb)
            z_ref[pl.ds(pl.multiple_of(r * CHUNK, CHUNK), CHUNK), :] = (yg * _sigmoid(yg) * yv).astype(BF16)

        _chunks(n_chunks, step)

    return pl.pallas_call(
        body, out_shape=jax.ShapeDtypeStruct((seq, ch), BF16), grid=(ch // cb,),
        in_specs=[_col(seq, cb), _col(seq, cb), _col(FFN_PAD, cb), _col(FFN_PAD, cb), _col(1, cb), _col(1, cb)],
        out_specs=_col(seq, cb),
        compiler_params=_params(("parallel",)), name="ffn_mid_fwd",
    )(ug, uv, wg, wv, bg, bv)


def _ffn_mid_bwd(ug, uv, dz, wg, wv, bg, bv):
    seq, ch = ug.shape
    cb = _tile(ch, 256)
    n_chunks = seq // CHUNK

    def body(ug_ref, uv_ref, dz_ref, wg_ref, wv_ref, bg_ref, bv_ref, dug_ref, duv_ref, dwg_ref, dwv_ref, dbg_ref, dbv_ref,
             dyg_ref, dyv_ref):
        dyg_ref[seq:seq + FFN_PAD, :] = jnp.zeros((FFN_PAD, cb), F32)
        dyv_ref[seq:seq + FFN_PAD, :] = jnp.zeros((FFN_PAD, cb), F32)
        dwg_ref[...] = jnp.zeros((FFN_PAD, cb), F32)
        dwv_ref[...] = jnp.zeros((FFN_PAD, cb), F32)
        dbg_ref[...] = jnp.zeros((1, cb), F32)
        dbv_ref[...] = jnp.zeros((1, cb), F32)

        def grads_of_conv_out(r):
            rows = pl.ds(pl.multiple_of(r * CHUNK, CHUNK), CHUNK)
            yg, gwin = _ffn_conv(ug_ref, r, wg_ref, bg_ref, cb)
            yv, vwin = _ffn_conv(uv_ref, r, wv_ref, bv_ref, cb)
            dzv = dz_ref[rows, :]
            sg = _sigmoid(yg)
            dyg = dzv * yv * (sg * (1.0 + yg * (1.0 - sg)))
            dyv = dzv * yg * sg
            dyg_ref[rows, :] = dyg
            dyv_ref[rows, :] = dyv
            for t in range(FFN_CONV_W):
                off = FFN_PAD - (FFN_CONV_W - 1) + t
                dwg_ref[t:t + 1, :] += jnp.sum(dyg * gwin[off:off + CHUNK, :], axis=0, keepdims=True)
                dwv_ref[t:t + 1, :] += jnp.sum(dyv * vwin[off:off + CHUNK, :], axis=0, keepdims=True)
            dbg_ref[...] += jnp.sum(dyg, axis=0, keepdims=True)
            dbv_ref[...] += jnp.sum(dyv, axis=0, keepdims=True)

        _chunks(n_chunks, grads_of_conv_out)

        def grads_of_conv_in(r, carry):
            r0 = pl.multiple_of(r * CHUNK, CHUNK)
            gwin = dyg_ref[pl.ds(r0, CHUNK + FFN_PAD), :]
            vwin = dyv_ref[pl.ds(r0, CHUNK + FFN_PAD), :]
            dug = jnp.zeros((CHUNK, cb), F32)
            duv = jnp.zeros((CHUNK, cb), F32)
            for t in range(FFN_CONV_W):
                off = FFN_CONV_W - 1 - t
                dug = dug + wg_ref[t:t + 1, :] * gwin[off:off + CHUNK, :]
                duv = duv + wv_ref[t:t + 1, :] * vwin[off:off + CHUNK, :]
            dug_ref[pl.ds(r0, CHUNK), :] = dug.astype(BF16)
            duv_ref[pl.ds(r0, CHUNK), :] = duv.astype(BF16)
            return carry

        lax.fori_loop(0, n_chunks, grads_of_conv_in, 0)

    return pl.pallas_call(
        body,
        out_shape=(jax.ShapeDtypeStruct((seq, ch), BF16), jax.ShapeDtypeStruct((seq, ch), BF16),
                   jax.ShapeDtypeStruct((FFN_PAD, ch), F32), jax.ShapeDtypeStruct((FFN_PAD, ch), F32),
                   jax.ShapeDtypeStruct((1, ch), F32), jax.ShapeDtypeStruct((1, ch), F32)),
        grid=(ch // cb,),
        in_specs=[_col(seq, cb), _col(seq, cb), _col(seq, cb), _col(FFN_PAD, cb), _col(FFN_PAD, cb), _col(1, cb), _col(1, cb)],
        out_specs=(_col(seq, cb), _col(seq, cb), _col(FFN_PAD, cb), _col(FFN_PAD, cb), _col(1, cb), _col(1, cb)),
        scratch_shapes=[pltpu.VMEM((seq + FFN_PAD, cb), F32) for _ in range(2)],
        compiler_params=_params(("parallel",)), name="ffn_mid_bwd",
    )(ug, uv, dz, wg, wv, bg, bv)


def _dot(a, b, mode):
    return lax.dot_general(a.astype(BF16), b.astype(BF16), _DIMS[mode], preferred_element_type=F32)


def _head(h):
    return slice(h * HEAD_DIM, (h + 1) * HEAD_DIM)


def _mem_softmax(q, k):
    s = _dot(q, k, "nt") * ATT_SCALE
    e = jnp.exp(s - jnp.max(s, axis=-1, keepdims=True))
    return e / jnp.sum(e, axis=-1, keepdims=True)


MEM_ROW_TILE = 1024


def _mem_row(width):
    return pl.BlockSpec((MEM_ROW_TILE, width), lambda i: (i, 0))


def _memattn_fwd(pq, mk, mv):
    seq, w = pq.shape
    m = mk.shape[0]

    def body(q_ref, k_ref, v_ref, o_ref):
        for h in range(N_MEM_HEADS):
            p = _mem_softmax(q_ref[:, _head(h)], k_ref[:, _head(h)])
            o_ref[:, _head(h)] = _dot(p, v_ref[:, _head(h)], "nn").astype(BF16)

    full = pl.BlockSpec((m, w), lambda i: (0, 0))
    return pl.pallas_call(
        body, out_shape=jax.ShapeDtypeStruct((seq, w), BF16), grid=(seq // MEM_ROW_TILE,),
        in_specs=[_mem_row(w), full, full], out_specs=_mem_row(w), compiler_params=_params(("parallel",)), name="memattn_fwd",
    )(pq, mk, mv)


def _memattn_bwd(pq, mk, mv, dmo):
    seq, w = pq.shape
    m = mk.shape[0]

    def body(q_ref, k_ref, v_ref, do_ref, dq_ref, dk_ref, dv_ref):
        step = pl.program_id(0)

        @pl.when(step == 0)
        def _():
            dk_ref[...] = jnp.zeros((m, w), F32)
            dv_ref[...] = jnp.zeros((m, w), F32)

        for h in range(N_MEM_HEADS):
            q, k, v, do = q_ref[:, _head(h)], k_ref[:, _head(h)], v_ref[:, _head(h)], do_ref[:, _head(h)]
            p = _mem_softmax(q, k)
            dp = _dot(do, v, "nt")
            ds = p * (dp - jnp.sum(dp * p, axis=-1, keepdims=True))
            dq_ref[:, _head(h)] = (_dot(ds, k, "nn") * ATT_SCALE).astype(BF16)
            dk_ref[:, _head(h)] += _dot(ds, q, "tn") * ATT_SCALE
            dv_ref[:, _head(h)] += _dot(p, do, "tn")

    full = pl.BlockSpec((m, w), lambda i: (0, 0))
    return pl.pallas_call(
        body,
        out_shape=(jax.ShapeDtypeStruct((seq, w), BF16), jax.ShapeDtypeStruct((m, w), F32), jax.ShapeDtypeStruct((m, w), F32)),
        grid=(seq // MEM_ROW_TILE,), in_specs=[_mem_row(w), full, full, _mem_row(w)], out_specs=(_mem_row(w), full, full),
        compiler_params=_params(("arbitrary",)), name="memattn_bwd",
    )(pq, mk, mv, dmo)


FWD_KEY_TILE = 512
BWD_KEY_TILE = 512


def _causal_t(s_t, q_blk, k_blk, key_tile):
    kpos = k_blk * key_tile + lax.broadcasted_iota(jnp.int32, (key_tile, ROW_TILE), 0)
    qpos = q_blk * ROW_TILE + lax.broadcasted_iota(jnp.int32, (key_tile, ROW_TILE), 1)
    return jnp.where(kpos <= qpos, s_t, NEG_BIG)


def _fox_fwd(q, k, v, cum_cols, cum_rows):
    seq, w = q.shape
    nq = seq // ROW_TILE

    def body(q_ref, k_ref, v_ref, cc_ref, cr_ref, o_ref, ob_ref, lse_ref):
        i = pl.program_id(1)
        qs = [q_ref[:, _head(h)].astype(BF16) for h in range(2)]
        crs = [cr_ref[0, 0, h:h + 1, :] for h in range(2)]

        def step(j, carry):
            rows = pl.ds(pl.multiple_of(j * FWD_KEY_TILE, FWD_KEY_TILE), FWD_KEY_TILE)
            new = []
            for h in range(2):
                m_run, l_run, acc = carry[h]
                s_t = _dot(k_ref[rows, _head(h)], qs[h], "nt") * ATT_SCALE + crs[h] - cc_ref[0, rows, h * HEAD_DIM:h * HEAD_DIM + 1]
                s_t = _causal_t(s_t, i, j, FWD_KEY_TILE)
                m_new = jnp.maximum(m_run, jnp.max(s_t, axis=0, keepdims=True))
                alpha = jnp.exp(m_run - m_new)
                p_t = jnp.exp(s_t - m_new)
                l_new = alpha * l_run + jnp.sum(p_t, axis=0, keepdims=True)
                new.append((m_new, l_new, alpha * acc + _dot(v_ref[rows, _head(h)], p_t, "tn")))
            return tuple(new)

        one_head = (jnp.full((1, ROW_TILE), NEG_BIG, F32), jnp.zeros((1, ROW_TILE), F32), jnp.zeros((HEAD_DIM, ROW_TILE), F32))
        res = lax.fori_loop(0, ((i + 1) * ROW_TILE + FWD_KEY_TILE - 1) // FWD_KEY_TILE, step, (one_head, one_head))
        o = jnp.concatenate([acc / l_run for _, l_run, acc in res], axis=0).T
        o_ref[...] = o
        ob_ref[...] = o.astype(BF16)
        lse_ref[...] = jnp.zeros((1, 1, 8, ROW_TILE), F32)
        for h in range(2):
            lse_ref[0, 0, h:h + 1, :] = res[h][0] + jnp.log(res[h][1])

    blk = pl.BlockSpec((ROW_TILE, LANE), lambda hp, i: (i, hp))
    full = pl.BlockSpec((seq, LANE), lambda hp, i: (0, hp))
    cols = pl.BlockSpec((1, seq, LANE), lambda hp, i: (hp, 0, 0))
    rows = pl.BlockSpec((1, 1, 8, ROW_TILE), lambda hp, i: (hp, i, 0, 0))
    return pl.pallas_call(
        body,
        out_shape=(jax.ShapeDtypeStruct((seq, w), F32), jax.ShapeDtypeStruct((seq, w), BF16),
                   jax.ShapeDtypeStruct((N_HEAD_PAIRS, nq, 8, ROW_TILE), F32)),
        grid=(N_HEAD_PAIRS, nq), in_specs=[blk, full, full, cols, rows], out_specs=(blk, blk, rows),
        compiler_params=_params(("parallel", "parallel")), name="fox_fwd",
    )(q, k, v, cum_cols, cum_rows)


def _fox_delta(o, do):
    seq, w = o.shape
    nq = seq // ROW_TILE

    def body(o_ref, do_ref, d_ref):
        d_ref[...] = jnp.zeros((1, nq, 8, ROW_TILE), F32)

        def block(i, carry):
            rows = pl.ds(pl.multiple_of(i * ROW_TILE, ROW_TILE), ROW_TILE)
            prod_t = (o_ref[rows, :] * do_ref[rows, :]).T
            for h in range(2):
                d_ref[0, i, h:h + 1, :] = jnp.sum(prod_t[h * HEAD_DIM:(h + 1) * HEAD_DIM, :], axis=0, keepdims=True)
            return carry

        lax.fori_loop(0, nq, block, 0)

    full = pl.BlockSpec((seq, LANE), lambda hp: (0, hp))
    return pl.pallas_call(
        body, out_shape=jax.ShapeDtypeStruct((N_HEAD_PAIRS, nq, 8, ROW_TILE), F32), grid=(N_HEAD_PAIRS,),
        in_specs=[full, full], out_specs=pl.BlockSpec((1, nq, 8, ROW_TILE), lambda hp: (hp, 0, 0, 0)),
        compiler_params=_params(("parallel",)), name="fox_delta",
    )(o, do)


def _fox_bwd(q, k, v, cum_cols, cum_rows, delta, lse, do):
    seq, w = q.shape
    nq = seq // ROW_TILE
    nk = seq // BWD_KEY_TILE

    def body(q_ref, k_ref, v_ref, cc_ref, cr_ref, delta_ref, lse_ref, do_ref, dq_ref, dk_ref, dv_ref, dcc_ref, dcr_ref):
        kj = pl.program_id(1)

        @pl.when(kj == 0)
        def _():
            dq_ref[...] = jnp.zeros((seq, LANE), F32)
            dcr_ref[...] = jnp.zeros((1, nq, 8, ROW_TILE), F32)

        ks = [k_ref[:, _head(h)].astype(BF16) for h in range(2)]
        vs = [v_ref[:, _head(h)].astype(BF16) for h in range(2)]
        ccs = [cc_ref[0, :, h * HEAD_DIM:h * HEAD_DIM + 1] for h in range(2)]

        def step(i, carry):
            rows = pl.ds(pl.multiple_of(i * ROW_TILE, ROW_TILE), ROW_TILE)
            new = []
            for h in range(2):
                dk, dv, fold = carry[h]
                qh = q_ref[rows, _head(h)].astype(BF16)
                doh = do_ref[rows, _head(h)]
                s_t = _dot(ks[h], qh, "nt") * ATT_SCALE + cr_ref[0, i, h:h + 1, :] - ccs[h]
                p_t = jnp.exp(_causal_t(s_t, i, kj, BWD_KEY_TILE) - lse_ref[0, i, h:h + 1, :])
                dp_t = _dot(vs[h], doh, "nt")
                ds_t = p_t * (dp_t - delta_ref[0, i, h:h + 1, :])
                dq_ref[rows, _head(h)] += _dot(ds_t, ks[h], "tn") * ATT_SCALE
                dcr_ref[0, i, h:h + 1, :] += jnp.sum(ds_t, axis=0, keepdims=True)
                new.append((dk + _dot(ds_t, qh, "nn") * ATT_SCALE, dv + _dot(p_t, doh, "nn"), fold + (ds_t[:, :LANE] + ds_t[:, LANE:])))
            return tuple(new)

        one_head = (jnp.zeros((BWD_KEY_TILE, HEAD_DIM), F32), jnp.zeros((BWD_KEY_TILE, HEAD_DIM), F32), jnp.zeros((BWD_KEY_TILE, LANE), F32))
        res = lax.fori_loop(kj * BWD_KEY_TILE // ROW_TILE, nq, step, (one_head, one_head))
        dcc_ref[...] = jnp.zeros((1, BWD_KEY_TILE // ROW_TILE, 8, ROW_TILE), F32)
        for h in range(2):
            dk_ref[:, _head(h)] = res[h][0]
            dv_ref[:, _head(h)] = res[h][1]
            per_key = jnp.broadcast_to(-jnp.sum(res[h][2], axis=-1, keepdims=True), (BWD_KEY_TILE, LANE)).T[0:1, :]
            for b in range(BWD_KEY_TILE // ROW_TILE):
                dcc_ref[0, b, h:h + 1, :] = per_key[:, b * ROW_TILE:(b + 1) * ROW_TILE]

    blk = pl.BlockSpec((BWD_KEY_TILE, LANE), lambda hp, j: (j, hp))
    full = pl.BlockSpec((seq, LANE), lambda hp, j: (0, hp))
    cols = pl.BlockSpec((1, BWD_KEY_TILE, LANE), lambda hp, j: (hp, j, 0))
    rows = pl.BlockSpec((1, nq, 8, ROW_TILE), lambda hp, j: (hp, 0, 0, 0))
    key_rows = pl.BlockSpec((1, BWD_KEY_TILE // ROW_TILE, 8, ROW_TILE), lambda hp, j: (hp, j, 0, 0))
    return pl.pallas_call(
        body,
        out_shape=(jax.ShapeDtypeStruct((seq, w), F32), jax.ShapeDtypeStruct((seq, w), F32), jax.ShapeDtypeStruct((seq, w), F32),
                   jax.ShapeDtypeStruct((N_HEAD_PAIRS, nq, 8, ROW_TILE), F32), jax.ShapeDtypeStruct((N_HEAD_PAIRS, nq, 8, ROW_TILE), F32)),
        grid=(N_HEAD_PAIRS, nk), in_specs=[full, blk, blk, cols, rows, rows, rows, full],
        out_specs=(full, blk, blk, key_rows, rows),
        compiler_params=_params(("parallel", "arbitrary")), name="fox_bwd",
    )(q, k, v, cum_cols, cum_rows, delta, lse, do)


def _split3(x):
    hi = x.astype(BF16)
    r1 = x - hi.astype(F32)
    mid = r1.astype(BF16)
    lo = (r1 - mid.astype(F32)).astype(BF16)
    return hi, mid, lo


def _tri_sum(tri, x):
    hi, mid, lo = _split3(x)
    dot = lambda p: lax.dot_general(tri, p, _DIMS["nn"], preferred_element_type=F32)
    return (dot(lo) + dot(mid)) + dot(hi)


def _tri(lower):
    r = lax.broadcasted_iota(jnp.int32, (LANE, LANE), 0)
    c = lax.broadcasted_iota(jnp.int32, (LANE, LANE), 1)
    return jnp.where((c <= r) if lower else (c >= r), 1.0, 0.0).astype(BF16)


def _fgate_fwd(fl, bf):
    seq, w = fl.shape
    nb = seq // LANE

    def body(f_ref, b_ref, cum_ref):
        tri = _tri(True)

        def step(i, carry):
            rows = pl.ds(pl.multiple_of(i * LANE, LANE), LANE)
            x = f_ref[rows, :] + b_ref[...]
            logsig = jnp.minimum(x, 0.0) - jnp.log(1.0 + jnp.exp(-jnp.abs(x)))
            cum = _tri_sum(tri, logsig) + carry
            cum_ref[rows, :] = cum
            return cum[LANE - 1:LANE, :]

        lax.fori_loop(0, nb, step, jnp.zeros((1, w), F32))

    return pl.pallas_call(body, out_shape=jax.ShapeDtypeStruct((seq, w), F32), compiler_params=_params(), name="fgate_fwd")(fl, bf)


def _fgate_bwd(fl, bf, dcum):
    seq, w = fl.shape
    nb = seq // LANE

    def body(f_ref, b_ref, dc_ref, df_ref, db_ref):
        tri = _tri(False)

        def step(i, carry):
            tail, db = carry
            rows = pl.ds(pl.multiple_of((nb - 1 - i) * LANE, LANE), LANE)
            suffix = _tri_sum(tri, dc_ref[rows, :]) + tail
            df = suffix * (1.0 - _sigmoid(f_ref[rows, :] + b_ref[...]))
            df_ref[rows, :] = df
            return suffix[0:1, :], db + jnp.sum(df, axis=0, keepdims=True)

        _, db = lax.fori_loop(0, nb, step, (jnp.zeros((1, w), F32), jnp.zeros((1, w), F32)))
        db_ref[...] = db

    return pl.pallas_call(
        body, out_shape=(jax.ShapeDtypeStruct((seq, w), F32), jax.ShapeDtypeStruct((1, w), F32)),
        compiler_params=_params(), name="fgate_bwd",
    )(fl, bf, dcum)


def _cum_layouts(cum):
    seq = cum.shape[0]
    nblk = seq // ROW_TILE
    heads = cum[:, :N_FOX_HEADS]
    cq = jnp.repeat(heads, HEAD_DIM, axis=1).reshape(seq, N_HEAD_PAIRS, LANE).transpose(1, 0, 2)
    ck = heads.T.reshape(N_HEAD_PAIRS, 2, nblk, ROW_TILE).transpose(0, 2, 1, 3)
    ck = jnp.pad(ck, ((0, 0), (0, 0), (0, 6), (0, 0)))
    return cq, ck


def _dcum_from_layouts(key_side, query_side):
    both = key_side + query_side
    seq = both.shape[1] * both.shape[3]
    heads = both[:, :, :2, :].transpose(0, 2, 1, 3).reshape(N_FOX_HEADS, seq).T
    return jnp.pad(heads, ((0, 0), (0, F_PAD - N_FOX_HEADS)))


def _local_step(x, mem, target, w, arrive, emit):
    vec = lambda a: a.reshape(1, -1)
    mem_n = _rms_fwd(mem, vec(w["g_mem"]))
    saved = []
    shared = None
    for l in range(N_LAYERS):
        arrive(l, x, "mixer")
        s = {"x_in": x}
        s["h"] = h = _rms_fwd(x, vec(w["g_mix"][l]))
        s["mk"] = mk = _mm(mem_n, w["w_mk"][l])
        s["mv"] = mv = _mm(mem_n, w["w_mv"][l])
        if l < N_A:
            s["pa"] = pa = _mm(h, w["w_a"][l])
            s["pg"] = pg = _mm(h, w["w_g"][l])
            s["pq"] = pq = _mm(h, w["w_qm"][l])
            s["v"], s["c"] = _conv_glu_fwd(pa, pg, vec(w["b_glu_a"][l]), vec(w["b_glu_g"][l]), w["w_dw_a"][l], vec(w["b_dw_a"][l]))
            s["mix"] = mix = _ln_silu_fwd(s["c"], vec(w["ln_g"][l]), vec(w["ln_b"][l]))
        else:
            if l == N_A:
                shared = {"x_in": x}
                shared["hk"] = hk = _rms_fwd(x, vec(w["g_kv"]))
                shared["k"] = _mm(hk, w["w_k"])
                shared["v"] = _mm(hk, w["w_v"])
                shared["fl"] = _mm(hk, w["w_f"])
                cum = _fgate_fwd(shared["fl"], w["b_f"])
                shared["cq"], shared["ck"] = _cum_layouts(cum)
            s["q"] = q = _mm(h, w["w_q"][l - N_A])
            s["pq"] = pq = _mm(h, w["w_qm"][l])
            s["o"], mix, s["lse"] = _fox_fwd(q, shared["k"], shared["v"], shared["cq"], shared["ck"])
            s["mix"] = mix
        s["mo"] = mo = _memattn_fwd(pq, mk, mv)
        x = _mm(mix, w["w_o_mix"][l], add=x)
        s["x_mid"] = x = _mm(mo, w["w_o_mem"][l], add=x)
        arrive(l, x, "ffn")
        s["hf"] = hf = _rms_fwd(x, vec(w["g_ffn"][l]))
        s["ug"] = ug = _mm(hf, w["w_up_g"][l])
        s["uv"] = uv = _mm(hf, w["w_up_v"][l])
        s["z"] = z = _ffn_mid_fwd(ug, uv, w["w_dw_f_g"][l], w["w_dw_f_v"][l], vec(w["b_dw_f_g"][l]), vec(w["b_dw_f_v"][l]))
        x = _mm(z, w["w_down"][l], add=x)
        saved.append(s)

    loss_row, dx, dxb, dg_final = _loss_head(x, vec(w["g_final"]), target)
    loss = loss_row[0, 0]

    g = {n: [None] * N_LAYERS for n in ("g_mix", "w_mem_kv", "w_out", "g_ffn", "w_up", "w_dw_f", "b_dw_f", "w_down")}
    for n in ("w_in_a", "b_glu", "w_dw_a", "b_dw_a", "ln_g", "ln_b", "w_in_b"):
        g[n] = [None] * N_A
    one = lambda arr: ([arr], 0)
    after_start = lambda started, grad: grad if started is None else grad + started[0, 0].astype(BF16)
    g["g_final"] = [one(dg_final[0])]
    dmem_n = None
    dk_sum = dv_sum = dcq_sum = dck_sum = None
    for l in reversed(range(N_LAYERS)):
        s = saved[l]
        dz = _mm(dxb, w["w_down"][l], "nt")
        g["w_down"][l] = one(_mm(s["z"], dxb, "tn", BF16))
        dug, duv, dwg, dwv, dbg, dbv = _ffn_mid_bwd(s["ug"], s["uv"], dz, w["w_dw_f_g"][l], w["w_dw_f_v"][l],
                                                    vec(w["b_dw_f_g"][l]), vec(w["b_dw_f_v"][l]))
        g["w_up"][l] = ([_mm(s["hf"], dug, "tn", BF16), _mm(s["hf"], duv, "tn", BF16)], 1)
        g["w_dw_f"][l] = ([dwg[:FFN_CONV_W], dwv[:FFN_CONV_W]], 1)
        g["b_dw_f"][l] = ([dbg[0], dbv[0]], 0)
        dhf = _mm(duv, w["w_up_v"][l], "nt", add=_mm(dug, w["w_up_g"][l], "nt"))
        dx, dxb, dg_ffn = _rms_bwd(s["x_mid"], vec(w["g_ffn"][l]), dhf, dx)
        g["g_ffn"][l] = one(dg_ffn[0])
        dxb = after_start(emit(l, g, "ffn"), dxb)
        dmix = _mm(dxb, w["w_o_mix"][l], "nt")
        dmo = _mm(dxb, w["w_o_mem"][l], "nt")
        g["w_out"][l] = ([_mm(s["mix"], dxb, "tn", BF16), _mm(s["mo"], dxb, "tn", BF16)], 0)
        dpq, dmk, dmv = _memattn_bwd(s["pq"], s["mk"], s["mv"], dmo)
        g["w_mem_kv"][l] = ([_mm(mem_n, dmk, "tn", BF16), _mm(mem_n, dmv, "tn", BF16)], 1)
        dmem_n = _mm(dmk, w["w_mk"][l], "nt", add=dmem_n)
        dmem_n = _mm(dmv, w["w_mv"][l], "nt", add=dmem_n)
        if l < N_A:
            dc, dlng, dlnb = _ln_silu_bwd(dmix, s["c"], vec(w["ln_g"][l]), vec(w["ln_b"][l]))
            da, dgate, dwdw, dbdw, dba, dbg2 = _conv_glu_bwd(dc, s["v"], s["pa"], s["pg"], vec(w["b_glu_a"][l]),
                                                              vec(w["b_glu_g"][l]), w["w_dw_a"][l])
            g["ln_g"][l], g["ln_b"][l], g["b_dw_a"][l] = one(dlng[0]), one(dlnb[0]), one(dbdw[0])
            g["w_dw_a"][l] = ([dwdw[:CONV_W]], 1)
            g["b_glu"][l] = ([dba[0], dbg2[0]], 0)
            g["w_in_a"][l] = ([_mm(s["h"], da, "tn", BF16), _mm(s["h"], dgate, "tn", BF16), _mm(s["h"], dpq, "tn", BF16)], 1)
            dh = _mm(da, w["w_a"][l], "nt")
            dh = _mm(dgate, w["w_g"][l], "nt", add=dh)
            dh = _mm(dpq, w["w_qm"][l], "nt", add=dh)
        else:
            dq, dk, dv, dcq, dck = _fox_bwd(s["q"], shared["k"], shared["v"], shared["cq"], shared["ck"], _fox_delta(s["o"], dmix), s["lse"], dmix)
            dk_sum = dk if dk_sum is None else dk_sum + dk
            dv_sum = dv if dv_sum is None else dv_sum + dv
            dcq_sum = dcq if dcq_sum is None else dcq_sum + dcq
            dck_sum = dck if dck_sum is None else dck_sum + dck
            g["w_in_b"][l - N_A] = ([_mm(s["h"], dq, "tn", BF16), _mm(s["h"], dpq, "tn", BF16)], 1)
            dh = _mm(dq, w["w_q"][l - N_A], "nt")
            dh = _mm(dpq, w["w_qm"][l], "nt", add=dh)
        dx, dxb, dg_mix = _rms_bwd(s["x_in"], vec(w["g_mix"][l]), dh, dx)
        g["g_mix"][l] = one(dg_mix[0])
        if l == N_A:
            df, dbf = _fgate_bwd(shared["fl"], w["b_f"], _dcum_from_layouts(dcq_sum, dck_sum))
            hk = shared["hk"]
            g["w_kvf"] = [([_mm(hk, dk_sum, "tn", BF16), _mm(hk, dv_sum, "tn", BF16), _mm(hk, df, "tn", BF16)[:, :N_FOX_HEADS]], 1)]
            g["b_f"] = [one(dbf[0, :N_FOX_HEADS])]
            dhk = _mm(dk_sum, w["w_k"], "nt")
            dhk = _mm(dv_sum, w["w_v"], "nt", add=dhk)
            dhk = _mm(df, w["w_f"], "nt", add=dhk)
            dx, dxb, dg_kv = _rms_bwd(shared["x_in"], vec(w["g_kv"]), dhk, dx)
            g["g_kv"] = [one(dg_kv[0])]
        dxb = after_start(emit(l, g, "mixer"), dxb)
    _, _, dg_mem = _rms_bwd(mem, vec(w["g_mem"]), dmem_n, None)
    g["g_mem"] = [one(dg_mem[0])]
    return loss, dx, g


SHARDED = {
    "w_in_a": ((N_A, D_MODEL, 2 * CONV_CH + MEM_W), 2),
    "b_glu": ((N_A, 2 * CONV_CH), 1),
    "w_dw_a": ((N_A, CONV_W, CONV_CH), 2),
    "b_dw_a": ((N_A, CONV_CH), 1),
    "ln_g": ((N_A, CONV_CH), 1),
    "ln_b": ((N_A, CONV_CH), 1),
    "w_kvf": ((D_MODEL, 2 * CONV_CH + N_FOX_HEADS), 1),
    "w_in_b": ((N_LAYERS - N_A, D_MODEL, CONV_CH + MEM_W), 1),
    "w_mem_kv": ((N_LAYERS, D_MODEL, 2 * MEM_W), 1),
    "w_out": ((N_LAYERS, D_MODEL, D_MODEL), 1),
    "w_up": ((N_LAYERS, D_MODEL, 2 * D_FF), 2),
    "w_dw_f": ((N_LAYERS, FFN_CONV_W, 2 * D_FF), 2),
    "w_down": ((N_LAYERS, D_FF, D_MODEL), 1),
}
REPLICATED = {
    "g_mix": (N_LAYERS, D_MODEL), "g_kv": (D_MODEL,), "b_f": (N_FOX_HEADS,), "g_mem": (D_MODEL,),
    "g_ffn": (N_LAYERS, D_MODEL), "b_dw_f": (N_LAYERS, 2 * D_FF), "g_final": (D_MODEL,),
}
WEIGHT_ORDER = ["g_mix", "w_in_a", "b_glu", "w_dw_a", "b_dw_a", "ln_g", "ln_b", "g_kv", "w_kvf", "b_f", "w_in_b", "g_mem",
                "w_mem_kv", "w_out", "g_ffn", "w_up", "w_dw_f", "b_dw_f", "w_down", "g_final"]


def _shard_shape(name):
    shape, axis = SHARDED[name]
    return tuple(d // N_CHIPS if i == axis else d for i, d in enumerate(shape))


def _size(shape):
    n = 1
    for d in shape:
        n *= d
    return n


VECTOR_LEAVES = ("b_glu", "w_dw_a", "b_dw_a", "ln_g", "ln_b", "w_dw_f")
_HALF_ROW_TILE = 16
_VECTOR_ELEMS = sum(_size(_shard_shape(n)) for n in VECTOR_LEAVES)
VECTOR_ROWS = -(-_VECTOR_ELEMS // (FLAT_COLS * 2 * _HALF_ROW_TILE)) * 2 * _HALF_ROW_TILE
REP_ELEMS = sum(_size(s) for s in REPLICATED.values())
REP_ROWS = 8
REP_COLS = -(-REP_ELEMS // (REP_ROWS * LANE)) * LANE


def _flatten_vectors(parts, dtype):
    flat = jnp.concatenate([p.reshape(-1).astype(dtype) for p in parts])
    return jnp.pad(flat, (0, VECTOR_ROWS * FLAT_COLS - _VECTOR_ELEMS)).reshape(2, VECTOR_ROWS // 2, FLAT_COLS)


def _unflatten_vectors(flat, lead=()):
    flat = flat.reshape(lead + (-1,))
    out, off = {}, 0
    for n in VECTOR_LEAVES:
        shp = _shard_shape(n)
        out[n] = flat[..., off:off + _size(shp)].reshape(lead + shp)
        off += _size(shp)
    return out


def _span(parts, cat_axis, sel_axis, lo, hi):
    if sel_axis != cat_axis:
        return jnp.concatenate([lax.slice_in_dim(p, lo, hi, axis=sel_axis) for p in parts], axis=cat_axis)
    taken, off = [], 0
    for p in parts:
        n = p.shape[cat_axis]
        a, b = max(lo, off), min(hi, off + n)
        if a < b:
            taken.append(lax.slice_in_dim(p, a - off, b - off, axis=cat_axis))
        off += n
    return taken[0] if len(taken) == 1 else jnp.concatenate(taken, axis=cat_axis)


FFN_LEAVES = 2
LAYER_LEAVES = tuple(
    ([("w_in_a", l)] if l < N_A else ([("w_kvf", None)] if l == N_A else []) + [("w_in_b", l - N_A)])
    + [("w_mem_kv", l), ("w_out", l), ("w_up", l), ("w_down", l)]
    for l in range(N_LAYERS))


def _layer_halves(weights, leaves):
    out = []
    for name, idx in leaves:
        shard = weights[name] if idx is None else weights[name][idx]
        rows, cols = shard.shape
        assert (rows // 2) % _HALF_ROW_TILE == 0, name
        out.append(shard.astype(BF16).reshape(2, rows // 2, cols))
    return out


def _layer_pieces(l, gathered, vectors):
    parts = {}
    for (name, idx), g in zip(LAYER_LEAVES[l], gathered):
        if g is not None:
            parts[name] = g

    def cut(name, sel_axis, lo, hi):
        g = parts[name]
        axis = SHARDED[name][1] - (len(SHARDED[name][0]) - 2)
        if axis == 0:
            return lax.slice_in_dim(g.reshape(-1, g.shape[-1]), lo, hi, axis=sel_axis)
        return _span([g[t].reshape(-1, g.shape[-1]) for t in range(N_CHIPS)], axis, sel_axis, lo, hi)

    def vcut(name, idx, sel_axis, lo, hi):
        axis = SHARDED[name][1] - 1
        return _span([vectors[name][t, idx] for t in range(N_CHIPS)], axis, sel_axis, lo, hi)

    def pad_rows(arr, rows):
        return jnp.pad(arr, ((0, rows - arr.shape[0]), (0, 0)))

    w = {}
    if "w_up" in parts:
        w["w_up_g"] = cut("w_up", 1, 0, D_FF)
        w["w_up_v"] = cut("w_up", 1, D_FF, 2 * D_FF)
        w["w_dw_f_g"] = pad_rows(vcut("w_dw_f", l, 1, 0, D_FF), FFN_PAD)
        w["w_dw_f_v"] = pad_rows(vcut("w_dw_f", l, 1, D_FF, 2 * D_FF), FFN_PAD)
        w["w_down"] = cut("w_down", 0, 0, D_FF)
    if "w_out" not in parts:
        return w
    if l < N_A:
        w["w_a"] = cut("w_in_a", 1, 0, CONV_CH)
        w["w_g"] = cut("w_in_a", 1, CONV_CH, 2 * CONV_CH)
        w["w_qm"] = cut("w_in_a", 1, 2 * CONV_CH, 2 * CONV_CH + MEM_W)
        w["b_glu_a"] = vcut("b_glu", l, 0, 0, CONV_CH)
        w["b_glu_g"] = vcut("b_glu", l, 0, CONV_CH, 2 * CONV_CH)
        w["w_dw_a"] = pad_rows(vcut("w_dw_a", l, 1, 0, CONV_CH), CONV_PAD)
        for n in ("b_dw_a", "ln_g", "ln_b"):
            w[n] = vcut(n, l, 0, 0, CONV_CH)
    else:
        if l == N_A:
            w["w_k"] = cut("w_kvf", 1, 0, CONV_CH)
            w["w_v"] = cut("w_kvf", 1, CONV_CH, 2 * CONV_CH)
            w["w_f"] = jnp.pad(cut("w_kvf", 1, 2 * CONV_CH, 2 * CONV_CH + N_FOX_HEADS), ((0, 0), (0, F_PAD - N_FOX_HEADS)))
        w["w_q"] = cut("w_in_b", 1, 0, CONV_CH)
        w["w_qm"] = cut("w_in_b", 1, CONV_CH, CONV_CH + MEM_W)
    w["w_mk"] = cut("w_mem_kv", 1, 0, MEM_W)
    w["w_mv"] = cut("w_mem_kv", 1, MEM_W, 2 * MEM_W)
    w["w_o_mix"] = cut("w_out", 0, 0, CONV_CH)
    w["w_o_mem"] = cut("w_out", 0, CONV_CH, D_MODEL)
    return w


_PER_LAYER = ("w_a", "w_g", "w_q", "w_qm", "b_glu_a", "b_glu_g", "w_dw_a", "b_dw_a", "ln_g", "ln_b", "w_mk", "w_mv", "w_o_mix", "w_o_mem",
              "w_up_g", "w_up_v", "w_dw_f_g", "w_dw_f_v", "w_down")


def _weight_table(rep):
    w = dict(rep)
    w["b_f"] = jnp.pad(rep["b_f"], (0, F_PAD - N_FOX_HEADS)).reshape(1, F_PAD)
    w["b_dw_f_g"], w["b_dw_f_v"] = rep["b_dw_f"][:, :D_FF], rep["b_dw_f"][:, D_FF:]
    for n in _PER_LAYER:
        w[n] = {}
    return w


def _install_layer(w, l, pieces):
    for n, p in pieces.items():
        if n in _PER_LAYER:
            w[n][l - N_A if n == "w_q" else l] = p
        else:
            w[n] = p


def _shard_of(entry, name, t):
    shape, axis = SHARDED[name]
    pieces, cat_axis = entry
    axis -= len(shape) - len(pieces[0].shape)
    step = shape[SHARDED[name][1]] // N_CHIPS
    return _span(pieces, cat_axis, axis, t * step, (t + 1) * step)


def _layer_slabs(grads, leaves):
    out = []
    for name, idx in leaves:
        pieces, cat_axis = grads[name][0 if idx is None else idx]
        if cat_axis == 0 and SHARDED[name][1] - (len(SHARDED[name][0]) - 2) == 0:
            whole = pieces[0] if len(pieces) == 1 else jnp.concatenate(pieces, axis=0)
            out.append(whole.astype(BF16).reshape(N_CHIPS, 2, whole.shape[0] // (2 * N_CHIPS), whole.shape[1]))
            continue
        shards = [_shard_of((pieces, cat_axis), name, t).astype(BF16) for t in range(N_CHIPS)]
        rows, cols = shards[0].shape
        out.append(jnp.stack(shards).reshape(N_CHIPS, 2, rows // 2, cols))
    return out


def _vector_slabs(grads):
    def shard(name, t):
        return jnp.stack([_shard_of(entry, name, t) for entry in grads[name]])
    return jnp.stack([_flatten_vectors([shard(n, t) for n in VECTOR_LEAVES], BF16) for t in range(N_CHIPS)])


def _replicated_rows(grads):
    parts = [p.reshape(-1) for n in REPLICATED for pieces, _ in grads[n] for p in pieces]
    flat = jnp.concatenate(parts)
    return jnp.pad(flat, (0, REP_ROWS * REP_COLS - REP_ELEMS)).reshape(REP_ROWS, REP_COLS)


_ANY = pl.BlockSpec(memory_space=pl.ANY)


def _place():
    x, y, c = lax.axis_index("x"), lax.axis_index("y"), lax.axis_index("c")
    chips = [(1 - x, y), (x, 1 - y), (1 - x, 1 - y)]
    return x, y, c, chips


def _chip_index(chip):
    return 2 * chip[0] + chip[1]


def _gather_leaves(halves):
    n = len(halves)

    def body(*refs):
        w_refs, out_refs, (send_sems, recv_sems) = refs[:n], refs[n:2 * n], refs[2 * n:]
        x, y, c, chips = _place()
        me = 2 * x + y
        sibling = (x, y, 1 - c)

        def copy(a, sem, chip_idx, half, to, src=None):
            dst = out_refs[a].at[chip_idx, half]
            return pltpu.make_async_remote_copy(src_ref=dst if src is None else src, dst_ref=dst, send_sem=send_sems.at[6 * a + sem],
                                                recv_sem=recv_sems.at[6 * a + sem], device_id=to, device_id_type=MESH_ID)

        first = [copy(a, j, me, c, (*chip, c), src=w_refs[a].at[c]) for j, chip in enumerate(chips) for a in range(n)]
        for cp in first:
            cp.start()
        passed = []
        for j, chip in enumerate(chips):
            for a in range(n):
                copy(a, j, _chip_index(chip), c, (x, y, c)).wait_recv()
                passed.append(copy(a, 3 + j, _chip_index(chip), c, sibling))
                passed[-1].start()
        for j, chip in enumerate(chips):
            for a in range(n):
                copy(a, 3 + j, _chip_index(chip), 1 - c, (x, y, c)).wait_recv()
        for cp in first + passed:
            cp.wait_send()

    return pl.pallas_call(
        body, out_shape=[jax.ShapeDtypeStruct((N_CHIPS,) + h.shape, h.dtype) for h in halves],
        in_specs=[_ANY] * n, out_specs=[_ANY] * n,
        scratch_shapes=[pltpu.SemaphoreType.DMA((6 * n,)), pltpu.SemaphoreType.DMA((6 * n,))], name="gather_leaves",
    )(*halves)


_HBM = pl.BlockSpec(memory_space=pltpu.HBM)
_SEM = pl.BlockSpec(memory_space=pltpu.SEMAPHORE)


def _in_hbm(a):
    return pltpu.with_memory_space_constraint(a, pltpu.HBM)


def _gather_start(groups, after):
    flat = [h for g in groups for h in g]
    n, ng = len(flat), len(groups)

    def body(*refs):
        srcs, lands, sems = refs[1:1 + n], refs[1 + n:1 + 2 * n], refs[1 + 2 * n:1 + 2 * n + 2 * ng]
        token = refs[-1]
        x, y, c, chips = _place()
        me = 2 * x + y
        a = 0
        for gi, g in enumerate(groups):
            for k in range(len(g)):
                for j, chip in enumerate(chips):
                    pltpu.make_async_remote_copy(src_ref=srcs[a].at[c], dst_ref=lands[a].at[me, c], send_sem=sems[2 * gi].at[3 * k + j],
                                                 recv_sem=sems[2 * gi + 1].at[3 * k + j], device_id=(*chip, c), device_id_type=MESH_ID).start()
                a += 1
        token[...] = jnp.zeros_like(token)

    sem_shapes = [pltpu.SemaphoreType.DMA((3 * len(g),)) for g in groups for _ in range(2)]
    out = pl.pallas_call(
        body, name="gather_start",
        out_shape=sem_shapes + [pltpu.HBM(h.shape, h.dtype) for h in flat] + [pltpu.HBM((N_CHIPS,) + h.shape, h.dtype) for h in flat]
        + [jax.ShapeDtypeStruct((8, LANE), F32)],
        in_specs=[_ANY] + [_HBM] * (2 * n), out_specs=[_SEM] * (2 * ng) + [_HBM] * (2 * n) + [pl.BlockSpec(memory_space=pltpu.VMEM)],
        input_output_aliases={1 + i: 2 * ng + i for i in range(2 * n)},
        compiler_params=pltpu.CompilerParams(has_side_effects=pltpu.SideEffectType.DATAFLOW_SIDE_EFFECTING),
    )(after, *[_in_hbm(h) for h in flat], *[_in_hbm(lax.empty((N_CHIPS,) + h.shape, h.dtype)) for h in flat])
    sems, srcs, lands = out[:2 * ng], out[2 * ng:2 * ng + n], out[2 * ng + n:2 * ng + 2 * n]
    res, a = [], 0
    for gi, g in enumerate(groups):
        res.append((sems[2 * gi], sems[2 * gi + 1], list(srcs[a:a + len(g)]), list(lands[a:a + len(g)])))
        a += len(g)
    return res


def _gather_wait(send_sems, recv_sems, srcs, lands, after, name):
    n = len(srcs)

    def body(*refs):
        src_refs, land_refs, send, recv = refs[:n], refs[n:2 * n], refs[2 * n], refs[2 * n + 1]
        x, y, c, chips = _place()
        for k in range(n):
            for j, chip in enumerate(chips):
                cp = pltpu.make_async_remote_copy(src_ref=src_refs[k].at[c], dst_ref=land_refs[k].at[_chip_index(chip), c], send_sem=send.at[3 * k + j],
                                                  recv_sem=recv.at[3 * k + j], device_id=(*chip, c), device_id_type=MESH_ID)
                cp.wait_send()
                cp.wait_recv()

    out = pl.pallas_call(
        body, name=name, out_shape=[pltpu.HBM(a.shape, a.dtype) for a in list(srcs) + list(lands)],
        in_specs=[_HBM] * (2 * n) + [_SEM, _SEM, _ANY], out_specs=[_HBM] * (2 * n), input_output_aliases={i: i for i in range(2 * n)},
        compiler_params=pltpu.CompilerParams(has_side_effects=pltpu.SideEffectType.DATAFLOW_SIDE_EFFECTING),
    )(*srcs, *lands, send_sems, recv_sems, after)
    return list(out[n:])


def _forward_halves(lands, name):
    n = len(lands)

    def body(*refs):
        out_refs, (send_sems, recv_sems) = refs[n:2 * n], refs[2 * n:]
        x, y, c, chips = _place()

        def copy(a, j, half, to):
            blk = out_refs[a].at[_chip_index(chips[j]), half]
            return pltpu.make_async_remote_copy(src_ref=blk, dst_ref=blk, send_sem=send_sems.at[3 * a + j], recv_sem=recv_sems.at[3 * a + j],
                                                device_id=to, device_id_type=MESH_ID)

        sends = [copy(a, j, c, (x, y, 1 - c)) for a in range(n) for j in range(3)]
        for cp in sends:
            cp.start()
        for a in range(n):
            for j in range(3):
                copy(a, j, 1 - c, (x, y, c)).wait_recv()
        for cp in sends:
            cp.wait_send()

    return pl.pallas_call(
        body, out_shape=[jax.ShapeDtypeStruct(a.shape, a.dtype) for a in lands], in_specs=[_ANY] * n, out_specs=[_ANY] * n,
        input_output_aliases={i: i for i in range(n)},
        scratch_shapes=[pltpu.SemaphoreType.DMA((3 * n,)), pltpu.SemaphoreType.DMA((3 * n,))], name=name,
    )(*lands)


def _swap_halves(slabs):
    n = len(slabs)

    def body(*refs):
        g_refs, got_refs, (send_sem, recv_sem) = refs[:n], refs[n:2 * n], refs[2 * n:]
        x, y, c, _ = _place()
        copies = [pltpu.make_async_remote_copy(src_ref=g_refs[a].at[t, 1 - c], dst_ref=got_refs[a].at[t], send_sem=send_sem.at[N_CHIPS * a + t],
                                               recv_sem=recv_sem.at[N_CHIPS * a + t], device_id=(x, y, 1 - c), device_id_type=MESH_ID)
                  for a in range(n) for t in range(N_CHIPS)]
        for cp in copies:
            cp.start()
        for cp in copies:
            cp.wait()

    return pl.pallas_call(
        body, out_shape=[jax.ShapeDtypeStruct((g.shape[0],) + g.shape[2:], g.dtype) for g in slabs],
        in_specs=[_ANY] * n, out_specs=[_ANY] * n,
        scratch_shapes=[pltpu.SemaphoreType.DMA((N_CHIPS * n,)), pltpu.SemaphoreType.DMA((N_CHIPS * n,))], name="swap_halves",
    )(*slabs)


def _scatter_start(parts, name):
    n = len(parts)

    def body(*refs):
        srcs, lands, send, recv = refs[:n], refs[n:2 * n], refs[2 * n], refs[2 * n + 1]
        token = refs[-1]
        x, y, c, chips = _place()
        me = 2 * x + y
        for k in range(n):
            for j, chip in enumerate(chips):
                pltpu.make_async_remote_copy(src_ref=srcs[k].at[_chip_index(chip)], dst_ref=lands[k].at[me], send_sem=send.at[3 * k + j],
                                             recv_sem=recv.at[3 * k + j], device_id=(*chip, c), device_id_type=MESH_ID).start()
        token[...] = jnp.zeros_like(token)

    out = pl.pallas_call(
        body, name=name,
        out_shape=[pltpu.SemaphoreType.DMA((3 * n,)), pltpu.SemaphoreType.DMA((3 * n,))] + [pltpu.HBM(p.shape, p.dtype) for p in parts] * 2
        + [jax.ShapeDtypeStruct((8, LANE), F32)],
        in_specs=[_HBM] * (2 * n), out_specs=[_SEM, _SEM] + [_HBM] * (2 * n) + [pl.BlockSpec(memory_space=pltpu.VMEM)],
        input_output_aliases={i: 2 + i for i in range(2 * n)},
        compiler_params=pltpu.CompilerParams(has_side_effects=pltpu.SideEffectType.DATAFLOW_SIDE_EFFECTING),
    )(*[_in_hbm(p) for p in parts], *[_in_hbm(lax.empty(p.shape, p.dtype)) for p in parts])
    return (out[0], out[1], list(out[2:2 + n]), list(out[2 + n:2 + 2 * n])), out[-1]


def _scatter_wait(send_sems, recv_sems, srcs, lands, after, name):
    n = len(srcs)

    def body(*refs):
        src_refs, land_refs, send, recv = refs[:n], refs[n:2 * n], refs[2 * n], refs[2 * n + 1]
        x, y, c, chips = _place()
        for k in range(n):
            for j, chip in enumerate(chips):
                cp = pltpu.make_async_remote_copy(src_ref=src_refs[k].at[_chip_index(chip)], dst_ref=land_refs[k].at[_chip_index(chip)],
                                                  send_sem=send.at[3 * k + j], recv_sem=recv.at[3 * k + j], device_id=(*chip, c), device_id_type=MESH_ID)
                cp.wait_send()
                cp.wait_recv()

    out = pl.pallas_call(
        body, name=name, out_shape=[pltpu.HBM(a.shape, a.dtype) for a in list(srcs) + list(lands)],
        in_specs=[_HBM] * (2 * n) + [_SEM, _SEM, _ANY], out_specs=[_HBM] * (2 * n), input_output_aliases={i: i for i in range(2 * n)},
        compiler_params=pltpu.CompilerParams(has_side_effects=pltpu.SideEffectType.DATAFLOW_SIDE_EFFECTING),
    )(*srcs, *lands, send_sems, recv_sems, after)
    return list(out[n:])


def _sibling_halves(halves):
    n = len(halves)

    def body(*refs):
        h_refs, got_refs, (send_sem, recv_sem) = refs[:n], refs[n:2 * n], refs[2 * n:]
        x, y, c, _ = _place()
        copies = [pltpu.make_async_remote_copy(src_ref=h_refs[a], dst_ref=got_refs[a], send_sem=send_sem.at[a], recv_sem=recv_sem.at[a],
                                               device_id=(x, y, 1 - c), device_id_type=MESH_ID) for a in range(n)]
        for cp in copies:
            cp.start()
        for cp in copies:
            cp.wait()

    return pl.pallas_call(
        body, out_shape=[jax.ShapeDtypeStruct(h.shape, h.dtype) for h in halves], in_specs=[_ANY] * n, out_specs=[_ANY] * n,
        scratch_shapes=[pltpu.SemaphoreType.DMA((n,)), pltpu.SemaphoreType.DMA((n,))], name="sibling_halves",
    )(*halves)


def _gather_replicated(rows):
    m_per, n = rows.shape

    def body(x_ref, out_ref, send_sems, recv_sems, local_sem):
        x, y, c, chips = _place()
        me, sibling = (x, y, c), (x, y, 1 - c)

        def block(px, py, pc):
            return out_ref.at[pl.ds((4 * px + 2 * py + pc) * m_per, m_per), :]

        def copy(k, blk, to, src=None):
            return pltpu.make_async_remote_copy(src_ref=block(*blk) if src is None else src, dst_ref=block(*blk),
                                                send_sem=send_sems.at[k], recv_sem=recv_sems.at[k], device_id=to, device_id_type=MESH_ID)

        mine = pltpu.make_async_copy(x_ref, block(*me), local_sem)
        mine.start()
        first = [copy(0, me, sibling, src=x_ref)]
        first += [copy(1 + j, me, (*chip, c), src=x_ref) for j, chip in enumerate(chips)]
        for cp in first:
            cp.start()
        passed = [copy(4 + j, (*chip, c), sibling) for j, chip in enumerate(chips)]
        for j, chip in enumerate(chips):
            copy(1 + j, (*chip, c), me).wait_recv()
            passed[j].start()
        copy(0, sibling, me).wait_recv()
        for j, chip in enumerate(chips):
            copy(4 + j, (*chip, 1 - c), me).wait_recv()
        for cp in first + passed:
            cp.wait_send()
        mine.wait()

    vmem = pl.BlockSpec(memory_space=pltpu.VMEM)
    return pl.pallas_call(
        body, out_shape=jax.ShapeDtypeStruct((N_DEV * m_per, n), rows.dtype), in_specs=[vmem], out_specs=vmem,
        scratch_shapes=[pltpu.SemaphoreType.DMA((7,)), pltpu.SemaphoreType.DMA((7,)), pltpu.SemaphoreType.DMA],
        name="gather_replicated",
    )(rows)


_SUM_ROWS = 256


def _tile_rows(rows):
    if rows <= _SUM_ROWS:
        return rows
    best = None
    for r in range(16, _SUM_ROWS + 1, 16):
        if rows % r == 0:
            best = r
    assert best is not None, rows
    return best


def _add_pairs(slabs, got, core):
    n, _, rows, cols = slabs.shape
    tr = _tile_rows(rows)

    def body(core_ref, a_ref, b_ref, o_ref):
        o_ref[...] = (a_ref[0].astype(F32) + b_ref[...].astype(F32)).astype(BF16)

    spec = pl.BlockSpec((1, tr, cols), lambda t, i, core_ref: (t, i, 0))
    return pl.pallas_call(
        body, out_shape=jax.ShapeDtypeStruct((n, rows, cols), BF16),
        grid_spec=pltpu.PrefetchScalarGridSpec(
            num_scalar_prefetch=1, grid=(n, rows // tr),
            in_specs=[pl.BlockSpec((1, 1, tr, cols), lambda t, i, core_ref: (t, core_ref[0], i, 0)), spec], out_specs=spec),
        compiler_params=_params(("parallel", "parallel")), name=f"add_pairs_{rows}x{cols}",
    )(core, slabs, got)


def _sum_slabs(slabs):
    n, rows, cols = slabs.shape
    tr = _tile_rows(rows)

    def body(s_ref, o_ref):
        acc = s_ref[0].astype(F32)
        for t in range(1, n):
            acc = acc + s_ref[t].astype(F32)
        o_ref[...] = acc

    return pl.pallas_call(
        body, out_shape=jax.ShapeDtypeStruct((rows, cols), F32), grid=(rows // tr,),
        in_specs=[pl.BlockSpec((n, tr, cols), lambda i: (0, i, 0))], out_specs=pl.BlockSpec((tr, cols), lambda i: (i, 0)),
        compiler_params=_params(("parallel",)), name=f"sum_slabs_{n}x{rows}x{cols}",
    )(slabs)


_ADAM_BLOCK_BYTES = 2 * 1024 * 1024


def _adamw(w, g, m, v):
    shape = w.shape
    cols = shape[-1]
    rows = _size(shape) // cols
    tr = rows
    if rows * cols * 4 > _ADAM_BLOCK_BYTES:
        for r in range(8, rows, 8):
            if rows % r == 0 and r * cols * 4 <= _ADAM_BLOCK_BYTES:
                tr = r

    def body(w_ref, g_ref, m_ref, v_ref, d_ref, nm_ref, nv_ref):
        gv = g_ref[...]
        nm = ADAM_B1 * m_ref[...] + (1.0 - ADAM_B1) * gv
        nv = ADAM_B2 * v_ref[...] + (1.0 - ADAM_B2) * jnp.square(gv)
        m_hat = nm / (1.0 - ADAM_B1 ** ADAM_STEP)
        v_hat = nv / (1.0 - ADAM_B2 ** ADAM_STEP)
        d_ref[...] = -ADAM_LR * (m_hat / (jnp.sqrt(v_hat) + ADAM_EPS) + ADAM_WD * w_ref[...])
        nm_ref[...] = nm
        nv_ref[...] = nv

    spec = pl.BlockSpec((tr, cols), lambda i: (i, 0))
    out = pl.pallas_call(
        body, out_shape=tuple(jax.ShapeDtypeStruct((rows, cols), F32) for _ in range(3)), grid=(rows // tr,),
        in_specs=[spec] * 4, out_specs=(spec,) * 3, compiler_params=_params(("parallel",)), name=f"adamw_{rows}x{cols}",
    )(*(a.reshape(rows, cols) for a in (w, g, m, v)))
    return tuple(o.reshape(shape) for o in out)


def kernel(x, mem, g_mix, w_in_a, b_glu, w_dw_a, b_dw_a, ln_g, ln_b, g_kv, w_kvf, b_f, w_in_b, g_mem, w_mem_kv, w_out, g_ffn, w_up, w_dw_f, b_dw_f, w_down, g_final, loss_target, m_g_mix, m_w_in_a, m_b_glu, m_w_dw_a, m_b_dw_a, m_ln_g, m_ln_b, m_g_kv, m_w_kvf, m_b_f, m_w_in_b, m_g_mem, m_w_mem_kv, m_w_out, m_g_ffn, m_w_up, m_w_dw_f, m_b_dw_f, m_w_down, m_g_final, v_g_mix, v_w_in_a, v_b_glu, v_w_dw_a, v_b_dw_a, v_ln_g, v_ln_b, v_g_kv, v_w_kvf, v_b_f, v_w_in_b, v_g_mem, v_w_mem_kv, v_w_out, v_g_ffn, v_w_up, v_w_dw_f, v_b_dw_f, v_w_down, v_g_final):
    weights = dict(g_mix=g_mix, w_in_a=w_in_a, b_glu=b_glu, w_dw_a=w_dw_a, b_dw_a=b_dw_a, ln_g=ln_g, ln_b=ln_b, g_kv=g_kv,
                   w_kvf=w_kvf, b_f=b_f, w_in_b=w_in_b, g_mem=g_mem, w_mem_kv=w_mem_kv, w_out=w_out, g_ffn=g_ffn, w_up=w_up,
                   w_dw_f=w_dw_f, b_dw_f=b_dw_f, w_down=w_down, g_final=g_final)
    mom1 = dict(g_mix=m_g_mix, w_in_a=m_w_in_a, b_glu=m_b_glu, w_dw_a=m_w_dw_a, b_dw_a=m_b_dw_a, ln_g=m_ln_g, ln_b=m_ln_b,
                g_kv=m_g_kv, w_kvf=m_w_kvf, b_f=m_b_f, w_in_b=m_w_in_b, g_mem=m_g_mem, w_mem_kv=m_w_mem_kv, w_out=m_w_out,
                g_ffn=m_g_ffn, w_up=m_w_up, w_dw_f=m_w_dw_f, b_dw_f=m_b_dw_f, w_down=m_w_down, g_final=m_g_final)
    mom2 = dict(g_mix=v_g_mix, w_in_a=v_w_in_a, b_glu=v_b_glu, w_dw_a=v_w_dw_a, b_dw_a=v_b_dw_a, ln_g=v_ln_g, ln_b=v_ln_b,
                g_kv=v_g_kv, w_kvf=v_w_kvf, b_f=v_b_f, w_in_b=v_w_in_b, g_mem=v_g_mem, w_mem_kv=v_w_mem_kv, w_out=v_w_out,
                g_ffn=v_g_ffn, w_up=v_w_up, w_dw_f=v_w_dw_f, b_dw_f=v_b_dw_f, w_down=v_w_down, g_final=v_g_final)

    x_pos, y_pos, core = lax.axis_index("x"), lax.axis_index("y"), lax.axis_index("c")
    chip = 2 * x_pos + y_pos

    mixer_of = lambda l: LAYER_LEAVES[l][:-FFN_LEAVES]
    ffn_of = lambda l: LAYER_LEAVES[l][-FFN_LEAVES:]
    local_first = _layer_halves(weights, mixer_of(0))
    vector_halves = _flatten_vectors([weights[n] for n in VECTOR_LEAVES], F32)
    first = _gather_leaves(local_first + [vector_halves])
    later = [ffn_of(0)] + [LAYER_LEAVES[l] for l in range(1, N_LAYERS)]
    local_later = [_layer_halves(weights, leaves) for leaves in later]
    in_flight = _gather_start(local_later, first[-1])

    def with_own(gathered, own):
        return [lax.dynamic_update_index_in_dim(g, o[None], chip, 0) for g, o in zip(gathered, own)]

    first = with_own(first, local_first + [vector_halves])
    vectors = _unflatten_vectors(first[-1], (N_CHIPS,))
    table = _weight_table({n: weights[n] for n in REPLICATED})

    def fetch(k, after):
        send_sems, recv_sems, srcs, lands = in_flight[k]
        lands = _gather_wait(send_sems, recv_sems, srcs, lands, after, f"gather_wait_{k}")
        return with_own(_forward_halves(lands, f"forward_halves_{k}"), local_later[k])

    def arrive(l, after, part):
        if l == 0:
            gathered = first[:-1] + [None] * FFN_LEAVES if part == "mixer" else [None] * len(mixer_of(0)) + fetch(0, after)
        elif part == "mixer":
            gathered = fetch(l, after)
        else:
            return
        _install_layer(table, l, _layer_pieces(l, gathered, vectors))

    core_index = core.astype(jnp.int32).reshape(1)
    in_flight_back = []

    def emit(l, g, part):
        if l == 0:
            leaves = ffn_of(0) if part == "ffn" else mixer_of(0)
        elif part == "mixer":
            leaves = LAYER_LEAVES[l]
        else:
            return None
        with_vectors = l == 0 and part == "mixer"
        slabs = _layer_slabs(g, leaves) + ([_vector_slabs(g)] if with_vectors else [])
        chip_sums = [_add_pairs(mine, got, core_index) for mine, got in zip(slabs, _swap_halves(slabs))]
        handles, started = _scatter_start(chip_sums, f"scatter_start_{len(in_flight_back)}")
        in_flight_back.append((leaves, with_vectors, chip_sums, handles))
        return started

    loss, dx, grads = _local_step(x[0], mem[0], loss_target[0], table, arrive, emit)
    loss = lax.psum(loss, ("x", "y", "c"))

    names, mine = [], []
    for k, (leaves, with_vectors, chip_sums, (send_sems, recv_sems, srcs, lands)) in enumerate(in_flight_back):
        lands = _scatter_wait(send_sems, recv_sems, srcs, lands, dx, f"scatter_wait_{k}")
        from_chips = [lax.dynamic_update_index_in_dim(got, lax.dynamic_index_in_dim(own, chip, 0, keepdims=True), chip, 0)
                      for got, own in zip(lands, chip_sums)]
        mine += [_sum_slabs(f) for f in from_chips]
        names += list(leaves) + ([("vectors", None)] if with_vectors else [])
    theirs = _sibling_halves(mine)
    per_leaf = {}
    for (name, idx), m, t in zip(names, mine, theirs):
        per_leaf.setdefault(name, {})[idx] = jnp.concatenate([jnp.where(core == 0, m, t), jnp.where(core == 0, t, m)])
    vector_grads = per_leaf.pop("vectors")[None]
    grad_leaves = {n: (d[None] if None in d else jnp.stack([d[i] for i in sorted(d)])) for n, d in per_leaf.items()}
    grad_leaves.update(_unflatten_vectors(vector_grads))

    rep_sum = _sum_slabs(_gather_replicated(_replicated_rows(grads)).reshape(N_DEV, REP_ROWS, REP_COLS)).reshape(-1)
    off = 0
    for n, shp in REPLICATED.items():
        grad_leaves[n] = rep_sum[off:off + _size(shp)].reshape(shp)
        off += _size(shp)

    deltas, new_m, new_v = {}, {}, {}
    for n in WEIGHT_ORDER:
        deltas[n], new_m[n], new_v[n] = _adamw(weights[n], grad_leaves[n], mom1[n], mom2[n])
    return (loss, dx[None], *[grad_leaves[n] for n in WEIGHT_ORDER], *[deltas[n] for n in WEIGHT_ORDER],
            *[new_m[n] for n in WEIGHT_ORDER], *[new_v[n] for n in WEIGHT_ORDER])
```
